```python
import jax, jax.numpy as jnp
from jax import lax
import numpy as np

D_MODEL = 1024
BATCH = 8
SEQ = 8192
DEPTH = 1

HEAD_DIM = 128
ATT_HEADS = 4
DILATED_PAIRS = ((128, 1), (512, 4), (2048, 16))
N_DIL_GROUPS = len(DILATED_PAIRS)
Q_W = N_DIL_GROUPS * ATT_HEADS * HEAD_DIM
KV_W = ATT_HEADS * HEAD_DIM
POOL_WINDOWS = (2, 4, 8, 16)
N_POOL_GROUPS = len(POOL_WINDOWS)
POOL_GROUP_DIM = 128
POOL_W = N_POOL_GROUPS * POOL_GROUP_DIM
MIX_W = KV_W + POOL_W
IN_W = Q_W + 2 * KV_W + POOL_W
BLOCK = 128
ROT_DIM = HEAD_DIM // 4
ROT_HALF = ROT_DIM // 2
ROPE_THETA = 500000.0
N_MEM = 256
X_HEADS = 4
X_W = X_HEADS * HEAD_DIM
D_FF = ((8 * D_MODEL // 3 + 255) // 256) * 256
EPS = 1e-6
NEG_INF = -1e30

kernel_name = "hybrid_pool_dilated_attn_block"


def rms_norm(x, g):
    xf = x.astype(jnp.float32)
    y = xf * lax.rsqrt(jnp.mean(xf * xf, axis=-1, keepdims=True) + EPS)
    return (y * g.astype(jnp.float32)).astype(x.dtype)


def rope_partial(t, cos, sin):
    ex = tuple(range(2, t.ndim - 1))
    c = jnp.expand_dims(cos, ex)
    s = jnp.expand_dims(sin, ex)
    tr = t[..., :ROT_DIM].astype(jnp.float32)
    x1, x2 = tr[..., :ROT_HALF], tr[..., ROT_HALF:]
    rot = jnp.concatenate([x1 * c - x2 * s, x2 * c + x1 * s], axis=-1).astype(t.dtype)
    return jnp.concatenate([rot, t[..., ROT_DIM:]], axis=-1)


def dilated_branch(q, k, v, window, dilation):
    B, S, H, D = q.shape
    n_back = window // dilation
    L = S // dilation
    nb = -(-L // BLOCK)
    Lp = nb * BLOCK

    def to_sub(t):
        t = t.astype(jnp.float32).reshape(B, L, dilation, H, D).transpose(0, 2, 3, 1, 4)
        return jnp.pad(t, ((0, 0), (0, 0), (0, 0), (0, Lp - L), (0, 0)))

    def windows(t):
        tp = jnp.pad(t, ((0, 0), (0, 0), (0, 0), (BLOCK, 0), (0, 0)))
        tp = tp.reshape(B, dilation, H, nb + 1, BLOCK, D)
        return jnp.concatenate([tp[:, :, :, :-1], tp[:, :, :, 1:]], axis=4)

    qb = to_sub(q).reshape(B, dilation, H, nb, BLOCK, D)
    kw = windows(to_sub(k))
    vw = windows(to_sub(v))
    s = jnp.einsum('bdhnqc,bdhnkc->bdhnqk', qb, kw)
    qi = jnp.arange(BLOCK)[:, None]
    kj = jnp.arange(2 * BLOCK)[None, :]
    delta = qi + BLOCK - kj
    key_idx = jnp.arange(nb)[:, None, None] * BLOCK - BLOCK + kj[None]
    valid = (delta >= 0) & (delta <= n_back) & (key_idx >= 0)
    s = jnp.where(valid, s, NEG_INF)
    m = jnp.max(s, axis=-1)
    p = jnp.exp(s - m[..., None])
    l = jnp.sum(p, axis=-1)
    acc = jnp.einsum('bdhnqk,bdhnkc->bdhnqc', p, vw)
    acc = acc.reshape(B, dilation, H, Lp, D)[:, :, :, :L].transpose(0, 3, 1, 2, 4).reshape(B, S, H, D)
    m = m.reshape(B, dilation, H, Lp)[..., :L].transpose(0, 3, 1, 2).reshape(B, S, H)
    l = l.reshape(B, dilation, H, Lp)[..., :L].transpose(0, 3, 1, 2).reshape(B, S, H)
    return acc, m, l


def pool_mixer(u, pool_w, pool_scale):
    B, S, _ = u.shape
    uf = u.astype(jnp.float32).reshape(B, S, N_POOL_GROUPS, POOL_GROUP_DIM)
    c = jnp.cumsum(uf, axis=1)
    t = jnp.arange(S)
    outs = []
    for g, w in enumerate(POOL_WINDOWS):
        cg = c[:, :, g]
        shifted = jnp.pad(cg, ((0, 0), (w, 0), (0, 0)))[:, :S]
        count = jnp.minimum(t + 1, w).astype(jnp.float32)[None, :, None]
        outs.append((cg - shifted) / count - uf[:, :, g])
    d = jnp.stack(outs, axis=2).astype(u.dtype)
    y = jnp.einsum('bsgc,gce->bsge', d, pool_w).reshape(B, S, POOL_W)
    return y * pool_scale


def parallel_mixer(xn, cos, sin, w_in, q_norm_g, k_norm_g, pool_w, pool_scale, w_out):
    B, S, _ = xn.shape
    proj = xn @ w_in
    q, k, v, u = jnp.split(proj, [Q_W, Q_W + KV_W, Q_W + 2 * KV_W], axis=-1)
    q = q.reshape(B, S, N_DIL_GROUPS, ATT_HEADS, HEAD_DIM)
    k = k.reshape(B, S, ATT_HEADS, HEAD_DIM)
    v = v.reshape(B, S, ATT_HEADS, HEAD_DIM)
    q = rope_partial(rms_norm(q, q_norm_g), cos, sin) * (HEAD_DIM ** -0.5)
    k = rope_partial(rms_norm(k, k_norm_g), cos, sin)
    accs, ms, ls = [], [], []
    for g, (window, dilation) in enumerate(DILATED_PAIRS):
        a, m, l = dilated_branch(q[:, :, g], k, v, window, dilation)
        accs.append(a); ms.append(m); ls.append(l)
    ms = jnp.stack(ms)
    wts = jnp.exp(ms - jnp.max(ms, axis=0, keepdims=True))
    num = jnp.sum(wts[..., None] * jnp.stack(accs), axis=0)
    den = jnp.sum(wts * jnp.stack(ls), axis=0)
    attn = (num / den[..., None]).astype(xn.dtype).reshape(B, S, KV_W)
    pooled = pool_mixer(u, pool_w, pool_scale)
    return jnp.concatenate([attn, pooled], axis=-1) @ w_out


def memory_cross_attention(hn, mem_n, w_cq, w_ckv, cq_norm_g, ck_norm_g, w_co):
    B, S, _ = hn.shape
    M = mem_n.shape[1]
    q = (hn @ w_cq).reshape(B, S, X_HEADS, HEAD_DIM)
    k, v = jnp.split(mem_n @ w_ckv, 2, axis=-1)
    k = k.reshape(B, M, X_HEADS, HEAD_DIM)
    v = v.reshape(B, M, X_HEADS, HEAD_DIM)
    q = rms_norm(q, cq_norm_g).astype(jnp.float32) * (HEAD_DIM ** -0.5)
    k = rms_norm(k, ck_norm_g).astype(jnp.float32)
    p = jax.nn.softmax(jnp.einsum('bshd,bmhd->bhsm', q, k), axis=-1)
    o = jnp.einsum('bhsm,bmhd->bshd', p, v.astype(jnp.float32)).astype(hn.dtype)
    return o.reshape(B, S, X_W) @ w_co


def swiglu_ffn(hn, w_gate_up, w_down):
    g, u = jnp.split(hn @ w_gate_up, 2, axis=-1)
    return (jax.nn.silu(g) * u) @ w_down


def _fwd_setup_inputs(seed: int = 0) -> dict:
    key = jax.random.key(seed)
    ks = jax.random.split(key, 24)
    f32 = jnp.float32

    def w(k, shape, fan_in):
        return jax.random.normal(k, shape, f32) * (fan_in ** -0.5)

    def gain(k, shape):
        return 1.0 + 0.02 * jax.random.normal(k, shape, f32)

    x = jax.random.normal(ks[0], (BATCH, SEQ, D_MODEL), f32)
    mem = jax.random.normal(ks[1], (BATCH, N_MEM, D_MODEL), f32)
    offset = jax.random.randint(ks[2], (BATCH, 1), 0, 4096, dtype=jnp.int32)
    positions = jnp.arange(SEQ, dtype=jnp.int32)[None, :] + offset
    return {
        "x": x,
        "mem": mem,
        "positions": positions,
        "mix_norm_g": gain(ks[3], (DEPTH, D_MODEL)),
        "w_in": w(ks[4], (DEPTH, D_MODEL, IN_W), D_MODEL),
        "q_norm_g": gain(ks[5], (DEPTH, HEAD_DIM)),
        "k_norm_g": gain(ks[6], (DEPTH, HEAD_DIM)),
        "pool_w": w(ks[7], (DEPTH, N_POOL_GROUPS, POOL_GROUP_DIM, POOL_GROUP_DIM), POOL_GROUP_DIM),
        "pool_scale": gain(ks[8], (DEPTH, POOL_W)),
        "w_out": w(ks[9], (DEPTH, MIX_W, D_MODEL), MIX_W),
        "cross_norm_g": gain(ks[10], (DEPTH, D_MODEL)),
        "mem_norm_g": gain(ks[11], (DEPTH, D_MODEL)),
        "w_cq": w(ks[12], (DEPTH, D_MODEL, X_W), D_MODEL),
        "w_ckv": w(ks[13], (DEPTH, D_MODEL, 2 * X_W), D_MODEL),
        "cq_norm_g": gain(ks[14], (DEPTH, HEAD_DIM)),
        "ck_norm_g": gain(ks[15], (DEPTH, HEAD_DIM)),
        "w_co": w(ks[16], (DEPTH, X_W, D_MODEL), X_W),
        "ffn_norm_g": gain(ks[17], (DEPTH, D_MODEL)),
        "w_gate_up": w(ks[18], (DEPTH, D_MODEL, 2 * D_FF), D_MODEL),
        "w_down": w(ks[19], (DEPTH, D_FF, D_MODEL), D_FF),
    }


def _fwd_reference(x, mem, positions, mix_norm_g, w_in, q_norm_g, k_norm_g, pool_w, pool_scale, w_out,
              cross_norm_g, mem_norm_g, w_cq, w_ckv, cq_norm_g, ck_norm_g, w_co,
              ffn_norm_g, w_gate_up, w_down):
    inv_freq = ROPE_THETA ** (-jnp.arange(0, ROT_DIM, 2, dtype=jnp.float32) / ROT_DIM)
    ang = positions.astype(jnp.float32)[..., None] * inv_freq
    cos, sin = jnp.cos(ang), jnp.sin(ang)
    h = x
    for layer in range(DEPTH):
        h = h + parallel_mixer(rms_norm(h, mix_norm_g[layer]), cos, sin, w_in[layer],
                               q_norm_g[layer], k_norm_g[layer], pool_w[layer],
                               pool_scale[layer], w_out[layer])
        h = h + memory_cross_attention(rms_norm(h, cross_norm_g[layer]),
                                       rms_norm(mem, mem_norm_g[layer]), w_cq[layer],
                                       w_ckv[layer], cq_norm_g[layer], ck_norm_g[layer],
                                       w_co[layer])
        h = h + swiglu_ffn(rms_norm(h, ffn_norm_g[layer]), w_gate_up[layer], w_down[layer])
    return h


import jax as _jax
import jax.numpy as _jnp

TWIN_FORMAT = 'train_step'
FWD_PARAMS = ['x', 'mem', 'positions', 'mix_norm_g', 'w_in', 'q_norm_g', 'k_norm_g', 'pool_w', 'pool_scale', 'w_out', 'cross_norm_g', 'mem_norm_g', 'w_cq', 'w_ckv', 'cq_norm_g', 'ck_norm_g', 'w_co', 'ffn_norm_g', 'w_gate_up', 'w_down']
TWIN_WEIGHTS = ['mix_norm_g', 'w_in', 'q_norm_g', 'k_norm_g', 'pool_w', 'pool_scale', 'w_out', 'cross_norm_g', 'mem_norm_g', 'w_cq', 'w_ckv', 'cq_norm_g', 'ck_norm_g', 'w_co', 'ffn_norm_g', 'w_gate_up', 'w_down']
TWIN_DIFF_INPUT = 'x'
TWIN_INPUTS = ['x', 'mem', 'positions', 'mix_norm_g', 'w_in', 'q_norm_g', 'k_norm_g', 'pool_w', 'pool_scale', 'w_out', 'cross_norm_g', 'mem_norm_g', 'w_cq', 'w_ckv', 'cq_norm_g', 'ck_norm_g', 'w_co', 'ffn_norm_g', 'w_gate_up', 'w_down', 'loss_target', 'm_mix_norm_g', 'm_w_in', 'm_q_norm_g', 'm_k_norm_g', 'm_pool_w', 'm_pool_scale', 'm_w_out', 'm_cross_norm_g', 'm_mem_norm_g', 'm_w_cq', 'm_w_ckv', 'm_cq_norm_g', 'm_ck_norm_g', 'm_w_co', 'm_ffn_norm_g', 'm_w_gate_up', 'm_w_down', 'v_mix_norm_g', 'v_w_in', 'v_q_norm_g', 'v_k_norm_g', 'v_pool_w', 'v_pool_scale', 'v_w_out', 'v_cross_norm_g', 'v_mem_norm_g', 'v_w_cq', 'v_w_ckv', 'v_cq_norm_g', 'v_ck_norm_g', 'v_w_co', 'v_ffn_norm_g', 'v_w_gate_up', 'v_w_down']
TWIN_OUTPUTS = ['loss', 'grad_x', 'grad_mix_norm_g', 'grad_w_in', 'grad_q_norm_g', 'grad_k_norm_g', 'grad_pool_w', 'grad_pool_scale', 'grad_w_out', 'grad_cross_norm_g', 'grad_mem_norm_g', 'grad_w_cq', 'grad_w_ckv', 'grad_cq_norm_g', 'grad_ck_norm_g', 'grad_w_co', 'grad_ffn_norm_g', 'grad_w_gate_up', 'grad_w_down', 'delta_mix_norm_g', 'delta_w_in', 'delta_q_norm_g', 'delta_k_norm_g', 'delta_pool_w', 'delta_pool_scale', 'delta_w_out', 'delta_cross_norm_g', 'delta_mem_norm_g', 'delta_w_cq', 'delta_w_ckv', 'delta_cq_norm_g', 'delta_ck_norm_g', 'delta_w_co', 'delta_ffn_norm_g', 'delta_w_gate_up', 'delta_w_down', 'new_m_mix_norm_g', 'new_m_w_in', 'new_m_q_norm_g', 'new_m_k_norm_g', 'new_m_pool_w', 'new_m_pool_scale', 'new_m_w_out', 'new_m_cross_norm_g', 'new_m_mem_norm_g', 'new_m_w_cq', 'new_m_w_ckv', 'new_m_cq_norm_g', 'new_m_ck_norm_g', 'new_m_w_co', 'new_m_ffn_norm_g', 'new_m_w_gate_up', 'new_m_w_down', 'new_v_mix_norm_g', 'new_v_w_in', 'new_v_q_norm_g', 'new_v_k_norm_g', 'new_v_pool_w', 'new_v_pool_scale', 'new_v_w_out', 'new_v_cross_norm_g', 'new_v_mem_norm_g', 'new_v_w_cq', 'new_v_w_ckv', 'new_v_cq_norm_g', 'new_v_ck_norm_g', 'new_v_w_co', 'new_v_ffn_norm_g', 'new_v_w_gate_up', 'new_v_w_down']
TWIN_LEAF_KINDS = {'loss': 'loss', 'grad_x': 'grad_x', 'grad_mix_norm_g': 'grad_w', 'grad_w_in': 'grad_w', 'grad_q_norm_g': 'grad_w', 'grad_k_norm_g': 'grad_w', 'grad_pool_w': 'grad_w', 'grad_pool_scale': 'grad_w', 'grad_w_out': 'grad_w', 'grad_cross_norm_g': 'grad_w', 'grad_mem_norm_g': 'grad_w', 'grad_w_cq': 'grad_w', 'grad_w_ckv': 'grad_w', 'grad_cq_norm_g': 'grad_w', 'grad_ck_norm_g': 'grad_w', 'grad_w_co': 'grad_w', 'grad_ffn_norm_g': 'grad_w', 'grad_w_gate_up': 'grad_w', 'grad_w_down': 'grad_w', 'delta_mix_norm_g': 'delta_w', 'delta_w_in': 'delta_w', 'delta_q_norm_g': 'delta_w', 'delta_k_norm_g': 'delta_w', 'delta_pool_w': 'delta_w', 'delta_pool_scale': 'delta_w', 'delta_w_out': 'delta_w', 'delta_cross_norm_g': 'delta_w', 'delta_mem_norm_g': 'delta_w', 'delta_w_cq': 'delta_w', 'delta_w_ckv': 'delta_w', 'delta_cq_norm_g': 'delta_w', 'delta_ck_norm_g': 'delta_w', 'delta_w_co': 'delta_w', 'delta_ffn_norm_g': 'delta_w', 'delta_w_gate_up': 'delta_w', 'delta_w_down': 'delta_w', 'new_m_mix_norm_g': 'new_m', 'new_m_w_in': 'new_m', 'new_m_q_norm_g': 'new_m', 'new_m_k_norm_g': 'new_m', 'new_m_pool_w': 'new_m', 'new_m_pool_scale': 'new_m', 'new_m_w_out': 'new_m', 'new_m_cross_norm_g': 'new_m', 'new_m_mem_norm_g': 'new_m', 'new_m_w_cq': 'new_m', 'new_m_w_ckv': 'new_m', 'new_m_cq_norm_g': 'new_m', 'new_m_ck_norm_g': 'new_m', 'new_m_w_co': 'new_m', 'new_m_ffn_norm_g': 'new_m', 'new_m_w_gate_up': 'new_m', 'new_m_w_down': 'new_m', 'new_v_mix_norm_g': 'new_v', 'new_v_w_in': 'new_v', 'new_v_q_norm_g': 'new_v', 'new_v_k_norm_g': 'new_v', 'new_v_pool_w': 'new_v', 'new_v_pool_scale': 'new_v', 'new_v_w_out': 'new_v', 'new_v_cross_norm_g': 'new_v', 'new_v_mem_norm_g': 'new_v', 'new_v_w_cq': 'new_v', 'new_v_w_ckv': 'new_v', 'new_v_cq_norm_g': 'new_v', 'new_v_ck_norm_g': 'new_v', 'new_v_w_co': 'new_v', 'new_v_ffn_norm_g': 'new_v', 'new_v_w_gate_up': 'new_v', 'new_v_w_down': 'new_v'}


def _forward(args):
    return _fwd_reference(*[args[k] for k in FWD_PARAMS])


def _output_shape():
    def fwd():
        inp = _fwd_setup_inputs(0)
        return _fwd_reference(*[inp[k] for k in FWD_PARAMS])
    out = _jax.eval_shape(fwd)
    return out.shape, out.dtype

N_MICROBATCH = 1
ADAM_LR = 0.001
ADAM_B1 = 0.9
ADAM_B2 = 0.999
ADAM_EPS = 1e-08
ADAM_WD = 0.01
ADAM_STEP = 10
PER_EXAMPLE_BATCH_AXIS = {'x': 0, 'mem': 0, 'positions': 0, 'loss_target': 0}
SHARED_INPUTS = []
_WEIGHT_DTYPES = {'mix_norm_g': _jnp.float32, 'w_in': _jnp.float32, 'q_norm_g': _jnp.float32, 'k_norm_g': _jnp.float32, 'pool_w': _jnp.float32, 'pool_scale': _jnp.float32, 'w_out': _jnp.float32, 'cross_norm_g': _jnp.float32, 'mem_norm_g': _jnp.float32, 'w_cq': _jnp.float32, 'w_ckv': _jnp.float32, 'cq_norm_g': _jnp.float32, 'ck_norm_g': _jnp.float32, 'w_co': _jnp.float32, 'ffn_norm_g': _jnp.float32, 'w_gate_up': _jnp.float32, 'w_down': _jnp.float32}
MOMENT_SCALE = {'mix_norm_g': 2.397026e+01, 'w_in': 9.029435e-01, 'q_norm_g': 1.175451e+00, 'k_norm_g': 1.175538e+00, 'pool_w': 4.802749e+00, 'pool_scale': 5.043227e+01, 'w_out': 1.724357e+00, 'cross_norm_g': 9.006283e-02, 'mem_norm_g': 6.995091e-01, 'w_cq': 1.291245e-01, 'w_ckv': 2.125386e-01, 'cq_norm_g': 5.018193e+00, 'ck_norm_g': 5.038251e+00, 'w_co': 1.884509e-01, 'ffn_norm_g': 4.954080e+01, 'w_gate_up': 3.377617e-01, 'w_down': 5.539344e-01}


def _to_microbatches(a, axis):
    t = _jnp.moveaxis(a, axis, 0)
    t = t.reshape((N_MICROBATCH, t.shape[0] // N_MICROBATCH) + t.shape[1:])
    return _jnp.moveaxis(t, 1, axis + 1)


def setup_inputs(seed: int = 0) -> dict:
    inp = _fwd_setup_inputs(seed)
    key = _jax.random.fold_in(_jax.random.key(seed), 7919)
    shape, _ = _output_shape()
    out = dict(inp)
    out["loss_target"] = _jax.random.normal(_jax.random.fold_in(key, 0), shape, _jnp.float32)
    for i, name in enumerate(TWIN_WEIGHTS):
        w = inp[name].astype(_jnp.float32)
        if MOMENT_SCALE is None:
            s = _jnp.sqrt(_jnp.mean(_jnp.square(w)) + 1e-30)
        else:
            s = MOMENT_SCALE[name]
        km, kv = _jax.random.split(_jax.random.fold_in(key, i + 1))
        out[name] = w
        out["m_" + name] = s * _jax.random.normal(km, w.shape, _jnp.float32)
        out["v_" + name] = (s * s) * _jax.random.uniform(kv, w.shape, _jnp.float32, 0.5, 1.5)
    if N_MICROBATCH > 1:
        for name, axis in PER_EXAMPLE_BATCH_AXIS.items():
            out[name] = _to_microbatches(out[name], axis)
    return {'x': out['x'], 'mem': out['mem'], 'positions': out['positions'], 'mix_norm_g': out['mix_norm_g'], 'w_in': out['w_in'], 'q_norm_g': out['q_norm_g'], 'k_norm_g': out['k_norm_g'], 'pool_w': out['pool_w'], 'pool_scale': out['pool_scale'], 'w_out': out['w_out'], 'cross_norm_g': out['cross_norm_g'], 'mem_norm_g': out['mem_norm_g'], 'w_cq': out['w_cq'], 'w_ckv': out['w_ckv'], 'cq_norm_g': out['cq_norm_g'], 'ck_norm_g': out['ck_norm_g'], 'w_co': out['w_co'], 'ffn_norm_g': out['ffn_norm_g'], 'w_gate_up': out['w_gate_up'], 'w_down': out['w_down'], 'loss_target': out['loss_target'], 'm_mix_norm_g': out['m_mix_norm_g'], 'm_w_in': out['m_w_in'], 'm_q_norm_g': out['m_q_norm_g'], 'm_k_norm_g': out['m_k_norm_g'], 'm_pool_w': out['m_pool_w'], 'm_pool_scale': out['m_pool_scale'], 'm_w_out': out['m_w_out'], 'm_cross_norm_g': out['m_cross_norm_g'], 'm_mem_norm_g': out['m_mem_norm_g'], 'm_w_cq': out['m_w_cq'], 'm_w_ckv': out['m_w_ckv'], 'm_cq_norm_g': out['m_cq_norm_g'], 'm_ck_norm_g': out['m_ck_norm_g'], 'm_w_co': out['m_w_co'], 'm_ffn_norm_g': out['m_ffn_norm_g'], 'm_w_gate_up': out['m_w_gate_up'], 'm_w_down': out['m_w_down'], 'v_mix_norm_g': out['v_mix_norm_g'], 'v_w_in': out['v_w_in'], 'v_q_norm_g': out['v_q_norm_g'], 'v_k_norm_g': out['v_k_norm_g'], 'v_pool_w': out['v_pool_w'], 'v_pool_scale': out['v_pool_scale'], 'v_w_out': out['v_w_out'], 'v_cross_norm_g': out['v_cross_norm_g'], 'v_mem_norm_g': out['v_mem_norm_g'], 'v_w_cq': out['v_w_cq'], 'v_w_ckv': out['v_w_ckv'], 'v_cq_norm_g': out['v_cq_norm_g'], 'v_ck_norm_g': out['v_ck_norm_g'], 'v_w_co': out['v_w_co'], 'v_ffn_norm_g': out['v_ffn_norm_g'], 'v_w_gate_up': out['v_w_gate_up'], 'v_w_down': out['v_w_down']}


def _loss(weights, diff, rest, loss_target):
    with _jax.named_scope("forward"):
        args = {**rest, TWIN_DIFF_INPUT: diff, **{k: w.astype(_WEIGHT_DTYPES[k]) for k, w in weights.items()}}
        y = _forward(args)
    with _jax.named_scope("loss_head"):
        err = _jnp.square(y.astype(_jnp.float32) - loss_target)
        return 0.5 * _jnp.sum(_jnp.mean(err, axis=-1)) if err.ndim else 0.5 * err


def _adamw(w, g, m, v):
    m = ADAM_B1 * m + (1.0 - ADAM_B1) * g
    v = ADAM_B2 * v + (1.0 - ADAM_B2) * _jnp.square(g)
    m_hat = m / (1.0 - ADAM_B1 ** ADAM_STEP)
    v_hat = v / (1.0 - ADAM_B2 ** ADAM_STEP)
    delta = -ADAM_LR * (m_hat / (_jnp.sqrt(v_hat) + ADAM_EPS) + ADAM_WD * w)
    return delta, m, v


def reference(x, mem, positions, mix_norm_g, w_in, q_norm_g, k_norm_g, pool_w, pool_scale, w_out, cross_norm_g, mem_norm_g, w_cq, w_ckv, cq_norm_g, ck_norm_g, w_co, ffn_norm_g, w_gate_up, w_down, loss_target, m_mix_norm_g, m_w_in, m_q_norm_g, m_k_norm_g, m_pool_w, m_pool_scale, m_w_out, m_cross_norm_g, m_mem_norm_g, m_w_cq, m_w_ckv, m_cq_norm_g, m_ck_norm_g, m_w_co, m_ffn_norm_g, m_w_gate_up, m_w_down, v_mix_norm_g, v_w_in, v_q_norm_g, v_k_norm_g, v_pool_w, v_pool_scale, v_w_out, v_cross_norm_g, v_mem_norm_g, v_w_cq, v_w_ckv, v_cq_norm_g, v_ck_norm_g, v_w_co, v_ffn_norm_g, v_w_gate_up, v_w_down):
    given = dict(x=x, mem=mem, positions=positions, mix_norm_g=mix_norm_g, w_in=w_in, q_norm_g=q_norm_g, k_norm_g=k_norm_g, pool_w=pool_w, pool_scale=pool_scale, w_out=w_out, cross_norm_g=cross_norm_g, mem_norm_g=mem_norm_g, w_cq=w_cq, w_ckv=w_ckv, cq_norm_g=cq_norm_g, ck_norm_g=ck_norm_g, w_co=w_co, ffn_norm_g=ffn_norm_g, w_gate_up=w_gate_up, w_down=w_down, loss_target=loss_target, m_mix_norm_g=m_mix_norm_g, m_w_in=m_w_in, m_q_norm_g=m_q_norm_g, m_k_norm_g=m_k_norm_g, m_pool_w=m_pool_w, m_pool_scale=m_pool_scale, m_w_out=m_w_out, m_cross_norm_g=m_cross_norm_g, m_mem_norm_g=m_mem_norm_g, m_w_cq=m_w_cq, m_w_ckv=m_w_ckv, m_cq_norm_g=m_cq_norm_g, m_ck_norm_g=m_ck_norm_g, m_w_co=m_w_co, m_ffn_norm_g=m_ffn_norm_g, m_w_gate_up=m_w_gate_up, m_w_down=m_w_down, v_mix_norm_g=v_mix_norm_g, v_w_in=v_w_in, v_q_norm_g=v_q_norm_g, v_k_norm_g=v_k_norm_g, v_pool_w=v_pool_w, v_pool_scale=v_pool_scale, v_w_out=v_w_out, v_cross_norm_g=v_cross_norm_g, v_mem_norm_g=v_mem_norm_g, v_w_cq=v_w_cq, v_w_ckv=v_w_ckv, v_cq_norm_g=v_cq_norm_g, v_ck_norm_g=v_ck_norm_g, v_w_co=v_w_co, v_ffn_norm_g=v_ffn_norm_g, v_w_gate_up=v_w_gate_up, v_w_down=v_w_down)
    weights = {n: given[n] for n in TWIN_WEIGHTS}
    shared = {n: given[n] for n in SHARED_INPUTS}
    per_example = {n: given[n] for n in ['x', 'mem', 'positions']}
    grad_fn = _jax.value_and_grad(_loss, argnums=(0, 1))

    def one_microbatch(ex, loss_target):
        ex = dict(ex)
        diff = ex.pop(TWIN_DIFF_INPUT)
        return grad_fn(weights, diff, {**shared, **ex}, loss_target)

    if N_MICROBATCH == 1:
        loss, (grad_w, grad_x) = one_microbatch(per_example, given["loss_target"])
    else:
        def body(carry, xs):
            loss_sum, grad_sum = carry
            l_k, (gw_k, gx_k) = one_microbatch(xs[0], xs[1])
            with _jax.named_scope("update"):
                return (loss_sum + l_k, _jax.tree.map(_jnp.add, grad_sum, gw_k)), gx_k

        init = (_jnp.zeros((), _jnp.float32), _jax.tree.map(_jnp.zeros_like, weights))
        (loss, grad_w), grad_x = _jax.lax.scan(body, init, (per_example, given["loss_target"]))
    with _jax.named_scope("update"):
        delta_w, new_m, new_v = {}, {}, {}
        for n in TWIN_WEIGHTS:
            delta_w[n], new_m[n], new_v[n] = _adamw(weights[n], grad_w[n], given["m_" + n], given["v_" + n])
    return (loss, grad_x, *[grad_w[n] for n in TWIN_WEIGHTS], *[delta_w[n] for n in TWIN_WEIGHTS],
            *[new_m[n] for n in TWIN_WEIGHTS], *[new_v[n] for n in TWIN_WEIGHTS])
```

```python
import functools

import jax
import jax.numpy as jnp
from jax import lax
from jax.experimental import pallas as pl
from jax.experimental.pallas import tpu as pltpu

F32 = jnp.float32
BF16 = jnp.bfloat16
SDS = jax.ShapeDtypeStruct

D_MODEL = 1024
HEAD_DIM = 128
N_GROUPS = 3
DILATIONS = (1, 4, 16)
ATT_HEADS = 4
Q_W = 1536
KV_W = 512
POOL_W = 512
POOL_WINDOWS = (2, 4, 8, 16)
IN_W = 3072
X_W = 512
N_MEM = 256
D_FF = 2816
FF_TILE = 768
FF_SHARD = 704
FF_PAD = 4 * FF_TILE
DOWN_SHARD = 352
ROT_DIM = 32
ROT_HALF = 16
ROPE_THETA = 500000.0
EPS = 1e-6
NEG_INF = -1e30
SCALE = HEAD_DIM ** -0.5
BLOCK = 128
HALO = 16

ADAM_LR = 0.001
ADAM_B1 = 0.9
ADAM_B2 = 0.999
ADAM_EPS = 1e-08
ADAM_WD = 0.01
ADAM_STEP = 10

N_DEV = 8
VMEM_LIMIT_BYTES = 56 * 1024 * 1024
MESH = pl.DeviceIdType.MESH


def _pcall(body, **kw):
    return pl.pallas_call(body, **kw)


def _cp():
    return pltpu.CompilerParams(vmem_limit_bytes=VMEM_LIMIT_BYTES)


def _dot(a, b):
    return lax.dot_general(a, b, (((1,), (0,)), ((), ())), preferred_element_type=F32)


def _dot_nt(a, b):
    return lax.dot_general(a, b, (((1,), (1,)), ((), ())), preferred_element_type=F32)


def _dot_tn(a, b):
    return lax.dot_general(a, b, (((0,), (0,)), ((), ())), preferred_element_type=F32)


def _rows(s):
    return min(512, s)


def _rms_r(x):
    return lax.rsqrt(jnp.mean(x * x, axis=-1, keepdims=True) + EPS)


def _norm_bwd(x, r, gain, dxn):
    z = dxn * gain
    dx = r * z - x * (r * r * r * jnp.mean(z * x, axis=-1, keepdims=True))
    dgain = jnp.sum(dxn * x * r, axis=0, keepdims=True)
    return dx, dgain


def _partner(t):
    lane = lax.broadcasted_iota(jnp.int32, t.shape, 1)
    return jnp.where(lane < ROT_HALF, pltpu.roll(t, HEAD_DIM - ROT_HALF, 1), pltpu.roll(t, ROT_HALF, 1))


def _rope(n, cos_t, sin_t):
    return n * cos_t + _partner(n) * sin_t


def _rope_bwd(d, cos_t, sin_t):
    lane = lax.broadcasted_iota(jnp.int32, d.shape, 1)
    return d * cos_t + jnp.where(lane < ROT_DIM, _partner(d * sin_t), 0.0)


def _norm_linear(x, gain, w, *, tn, out_dtype, name):
    s, k = x.shape
    n = w.shape[1]
    tm = _rows(s)

    def body(x_ref, g_ref, w_ref, y_ref, xn_ref, xs_ref):
        @pl.when(pl.program_id(1) == 0)
        def _():
            xv = x_ref[...]
            xn = (xv * _rms_r(xv) * g_ref[...]).astype(BF16)
            xs_ref[...] = xn
            xn_ref[...] = xn
        y_ref[...] = _dot(xs_ref[...], w_ref[...]).astype(out_dtype)

    return _pcall(
        body, name=name, grid=(s // tm, n // tn),
        in_specs=[pl.BlockSpec((tm, k), lambda i, j: (i, 0)), pl.BlockSpec((1, k), lambda i, j: (0, 0)),
                  pl.BlockSpec((k, tn), lambda i, j: (0, j))],
        out_specs=[pl.BlockSpec((tm, tn), lambda i, j: (i, j)), pl.BlockSpec((tm, k), lambda i, j: (i, 0))],
        out_shape=[SDS((s, n), out_dtype), SDS((s, k), BF16)],
        scratch_shapes=[pltpu.VMEM((tm, k), BF16)], compiler_params=_cp())(x, gain, w)


def _norm_linear_swiglu(x, gain, wgu, *, name):
    s, k = x.shape
    tm = _rows(s)
    nt = FF_PAD // FF_TILE

    def body(x_ref, g_ref, wg_ref, wu_ref, gu_ref, a_ref, xn_ref, xs_ref, us_ref):
        t, h = pl.program_id(1), pl.program_id(2)

        @pl.when((t == 0) & (h == 0))
        def _():
            xv = x_ref[...]
            xn = (xv * _rms_r(xv) * g_ref[...]).astype(BF16)
            xs_ref[...] = xn
            xn_ref[...] = xn

        @pl.when(h == 0)
        def _():
            g = _dot(xs_ref[...], wg_ref[...])
            u = _dot(xs_ref[...], wu_ref[...])
            a_ref[...] = (g * jax.nn.sigmoid(g) * u).astype(BF16)
            gu_ref[...] = g.astype(BF16)
            us_ref[...] = u.astype(BF16)

        @pl.when(h == 1)
        def _():
            gu_ref[...] = us_ref[...]

    return _pcall(
        body, name=name, grid=(s // tm, nt, 2),
        in_specs=[pl.BlockSpec((tm, k), lambda i, t, h: (i, 0)), pl.BlockSpec((1, k), lambda i, t, h: (0, 0)),
                  pl.BlockSpec((k, FF_TILE), lambda i, t, h: (0, t)),
                  pl.BlockSpec((k, FF_TILE), lambda i, t, h: (0, nt + t))],
        out_specs=[pl.BlockSpec((tm, FF_TILE), lambda i, t, h: (i, t + nt * h)),
                   pl.BlockSpec((tm, FF_TILE), lambda i, t, h: (i, t)),
                   pl.BlockSpec((tm, k), lambda i, t, h: (i, 0))],
        out_shape=[SDS((s, 2 * FF_PAD), BF16), SDS((s, FF_PAD), BF16), SDS((s, k), BF16)],
        scratch_shapes=[pltpu.VMEM((tm, k), BF16), pltpu.VMEM((tm, FF_TILE), BF16)],
        compiler_params=_cp())(x, gain, wgu, wgu)


def _linear_res(a, w, res, *, tn, name):
    s, k = a.shape
    n = w.shape[1]
    tm = _rows(s)

    def body(a_ref, w_ref, r_ref, y_ref):
        y_ref[...] = r_ref[...] + _dot(a_ref[...], w_ref[...])

    return _pcall(
        body, name=name, grid=(s // tm, n // tn),
        in_specs=[pl.BlockSpec((tm, k), lambda i, j: (i, 0)), pl.BlockSpec((k, tn), lambda i, j: (0, j)),
                  pl.BlockSpec((tm, tn), lambda i, j: (i, j))],
        out_specs=pl.BlockSpec((tm, tn), lambda i, j: (i, j)),
        out_shape=SDS((s, n), F32), compiler_params=_cp())(a, w, res)


def _linear_res_loss(a, w, res, tgt, *, name):
    s, k = a.shape
    n = w.shape[1]
    tm = _rows(s) // 2

    def body(a_ref, w_ref, r_ref, t_ref, dy_ref, dyb_ref, sq_ref):
        e = r_ref[...] + _dot(a_ref[...], w_ref[...]) - t_ref[...]
        dy = e * (1.0 / n)
        dy_ref[...] = dy
        dyb_ref[...] = dy.astype(BF16)

        @pl.when(pl.program_id(0) == 0)
        def _():
            sq_ref[...] = jnp.zeros_like(sq_ref)
        sq_ref[...] += jnp.sum(e * e, axis=0, keepdims=True)

    return _pcall(
        body, name=name, grid=(s // tm,),
        in_specs=[pl.BlockSpec((tm, k), lambda i: (i, 0)), pl.BlockSpec((k, n), lambda i: (0, 0)),
                  pl.BlockSpec((tm, n), lambda i: (i, 0)), pl.BlockSpec((tm, n), lambda i: (i, 0))],
        out_specs=[pl.BlockSpec((tm, n), lambda i: (i, 0)), pl.BlockSpec((tm, n), lambda i: (i, 0)),
                   pl.BlockSpec((1, n), lambda i: (0, 0))],
        out_shape=[SDS((s, n), F32), SDS((s, n), BF16), SDS((1, n), F32)],
        compiler_params=_cp())(a, w, res, tgt)


def _linear_nt(g, w, *, tn, name):
    s, k = g.shape
    n = w.shape[0]
    tm = _rows(s)

    def body(g_ref, w_ref, y_ref):
        y_ref[...] = _dot_nt(g_ref[...], w_ref[...]).astype(BF16)

    return _pcall(
        body, name=name, grid=(s // tm, n // tn),
        in_specs=[pl.BlockSpec((tm, k), lambda i, j: (i, 0)), pl.BlockSpec((tn, k), lambda i, j: (j, 0))],
        out_specs=pl.BlockSpec((tm, tn), lambda i, j: (i, j)),
        out_shape=SDS((s, n), BF16), compiler_params=_cp())(g, w)


def _swiglu_bwd(dyb, wd, gu, *, name):
    s, n = dyb.shape
    tm = _rows(s)
    nt = FF_PAD // FF_TILE

    def body(dy_ref, wd_ref, g_ref, u_ref, dgu_ref, du_s):
        h = pl.program_id(2)

        @pl.when(h == 0)
        def _():
            da = _dot_nt(dy_ref[...], wd_ref[...])
            g = g_ref[...].astype(F32)
            u = u_ref[...].astype(F32)
            sg = jax.nn.sigmoid(g)
            dgu_ref[...] = (da * u * (sg * (1.0 + g * (1.0 - sg)))).astype(BF16)
            du_s[...] = (da * (g * sg)).astype(BF16)

        @pl.when(h == 1)
        def _():
            dgu_ref[...] = du_s[...]

    return _pcall(
        body, name=name, grid=(s // tm, nt, 2),
        in_specs=[pl.BlockSpec((tm, n), lambda i, t, h: (i, 0)), pl.BlockSpec((FF_TILE, n), lambda i, t, h: (t, 0)),
                  pl.BlockSpec((tm, FF_TILE), lambda i, t, h: (i, t)),
                  pl.BlockSpec((tm, FF_TILE), lambda i, t, h: (i, nt + t))],
        out_specs=pl.BlockSpec((tm, FF_TILE), lambda i, t, h: (i, t + nt * h)),
        out_shape=SDS((s, 2 * FF_PAD), BF16),
        scratch_shapes=[pltpu.VMEM((tm, FF_TILE), BF16)], compiler_params=_cp())(dyb, wd, gu, gu)


def _linear_nt_normbwd(g, w, x, dres, gain, *, tk, name):
    s, k = g.shape
    d = w.shape[0]
    tm = _rows(s)
    nk = k // tk

    def body(g_ref, w_ref, x_ref, dr_ref, gn_ref, dx_ref, dxb_ref, dg_ref, acc_ref):
        i, kk = pl.program_id(0), pl.program_id(1)

        @pl.when(kk == 0)
        def _():
            acc_ref[...] = jnp.zeros_like(acc_ref)

        @pl.when((kk == 0) & (i == 0))
        def _():
            dg_ref[...] = jnp.zeros_like(dg_ref)

        acc_ref[...] += _dot_nt(g_ref[...], w_ref[...])

        @pl.when(kk == nk - 1)
        def _():
            xv = x_ref[...]
            dx, dgain = _norm_bwd(xv, _rms_r(xv), gn_ref[...], acc_ref[...])
            out = dr_ref[...] + dx
            dx_ref[...] = out
            dxb_ref[...] = out.astype(BF16)
            dg_ref[...] += dgain

    return _pcall(
        body, name=name, grid=(s // tm, nk),
        in_specs=[pl.BlockSpec((tm, tk), lambda i, kk: (i, kk)), pl.BlockSpec((d, tk), lambda i, kk: (0, kk)),
                  pl.BlockSpec((tm, d), lambda i, kk: (i, 0)), pl.BlockSpec((tm, d), lambda i, kk: (i, 0)),
                  pl.BlockSpec((1, d), lambda i, kk: (0, 0))],
        out_specs=[pl.BlockSpec((tm, d), lambda i, kk: (i, 0)), pl.BlockSpec((tm, d), lambda i, kk: (i, 0)),
                   pl.BlockSpec((1, d), lambda i, kk: (0, 0))],
        out_shape=[SDS((s, d), F32), SDS((s, d), BF16), SDS((1, d), F32)],
        scratch_shapes=[pltpu.VMEM((tm, d), F32)], compiler_params=_cp())(g, w, x, dres, gain)


def _dw_tn(x, g, *, tkw, tn, name):
    s, kw = x.shape
    n = g.shape[1]
    ts = _rows(s)
    ns = s // ts

    def body(x_ref, g_ref, o_ref, acc_ref):
        ss = pl.program_id(2)

        @pl.when(ss == 0)
        def _():
            acc_ref[...] = jnp.zeros_like(acc_ref)

        acc_ref[...] += _dot_tn(x_ref[...], g_ref[...])

        @pl.when(ss == ns - 1)
        def _():
            o_ref[...] = acc_ref[...].astype(BF16)

    return _pcall(
        body, name=name, grid=(kw // tkw, n // tn, ns),
        in_specs=[pl.BlockSpec((ts, tkw), lambda a, b, ss: (ss, a)), pl.BlockSpec((ts, tn), lambda a, b, ss: (ss, b))],
        out_specs=pl.BlockSpec((tkw, tn), lambda a, b, ss: (a, b)),
        out_shape=SDS((kw, n), BF16),
        scratch_shapes=[pltpu.VMEM((tkw, tn), F32)], compiler_params=_cp())(x, g)


def _qk_prep(proj, cos_t, sin_t, qg, kg, *, name):
    s = proj.shape[0]
    tm = _rows(s)
    nqh = Q_W // HEAD_DIM

    def body(q_ref, k_ref, v_ref, c_ref, s_ref, qg_ref, kg_ref, qo_ref, ko_ref, vo_ref):
        c, sn = c_ref[...], s_ref[...]
        for h in range(nqh):
            sl = slice(h * HEAD_DIM, (h + 1) * HEAD_DIM)
            y = q_ref[:, sl]
            qo_ref[:, sl] = (_rope(y * _rms_r(y) * qg_ref[...], c, sn) * SCALE).astype(BF16)
        for h in range(ATT_HEADS):
            sl = slice(h * HEAD_DIM, (h + 1) * HEAD_DIM)
            y = k_ref[:, sl]
            ko_ref[:, sl] = _rope(y * _rms_r(y) * kg_ref[...], c, sn).astype(BF16)
        vo_ref[...] = v_ref[...].astype(BF16)

    row = lambda w, j: pl.BlockSpec((tm, w), lambda i: (i, j))
    one = pl.BlockSpec((1, HEAD_DIM), lambda i: (0, 0))
    return _pcall(
        body, name=name, grid=(s // tm,),
        in_specs=[row(Q_W, 0), row(KV_W, 3), row(KV_W, 4), row(HEAD_DIM, 0), row(HEAD_DIM, 0), one, one],
        out_specs=[row(Q_W, 0), row(KV_W, 0), row(KV_W, 0)],
        out_shape=[SDS((s, Q_W), BF16), SDS((s, KV_W), BF16), SDS((s, KV_W), BF16)],
        compiler_params=_cp())(proj, proj, proj, cos_t, sin_t, qg, kg)


def _qk_prep_bwd(proj, dq, dk, dv, du, cos_t, sin_t, qg, kg, *, name):
    s = proj.shape[0]
    tm = _rows(s)

    def body(q_ref, k_ref, dq0, dq1, dq2, dk0, dk1, dk2, dv0, dv1, dv2, du_ref, c_ref, s_ref, qg_ref, kg_ref,
             dp_ref, dqg_ref, dkg_ref):
        c, sn = c_ref[...], s_ref[...]

        @pl.when(pl.program_id(0) == 0)
        def _():
            dqg_ref[...] = jnp.zeros_like(dqg_ref)
            dkg_ref[...] = jnp.zeros_like(dkg_ref)

        dqg = jnp.zeros((1, HEAD_DIM), F32)
        for g, dq_ref in enumerate((dq0, dq1, dq2)):
            for h in range(ATT_HEADS):
                sl = slice(h * HEAD_DIM, (h + 1) * HEAD_DIM)
                col = slice(g * KV_W + h * HEAD_DIM, g * KV_W + (h + 1) * HEAD_DIM)
                y = q_ref[:, col]
                dn = _rope_bwd(dq_ref[:, sl].astype(F32) * SCALE, c, sn)
                dy, dgain = _norm_bwd(y, _rms_r(y), qg_ref[...], dn)
                dp_ref[:, col] = dy.astype(BF16)
                dqg = dqg + dgain
        dqg_ref[...] += dqg

        dkg = jnp.zeros((1, HEAD_DIM), F32)
        for h in range(ATT_HEADS):
            sl = slice(h * HEAD_DIM, (h + 1) * HEAD_DIM)
            y = k_ref[:, sl]
            dn = _rope_bwd(dk0[:, sl] + dk1[:, sl] + dk2[:, sl], c, sn)
            dy, dgain = _norm_bwd(y, _rms_r(y), kg_ref[...], dn)
            dp_ref[:, Q_W + h * HEAD_DIM:Q_W + (h + 1) * HEAD_DIM] = dy.astype(BF16)
            dkg = dkg + dgain
        dkg_ref[...] += dkg

        dp_ref[:, Q_W + KV_W:Q_W + 2 * KV_W] = (dv0[...] + dv1[...] + dv2[...]).astype(BF16)
        dp_ref[:, Q_W + 2 * KV_W:] = du_ref[...].astype(BF16)

    row = lambda w, j: pl.BlockSpec((tm, w), lambda i: (i, j))
    one = pl.BlockSpec((1, HEAD_DIM), lambda i: (0, 0))
    return _pcall(
        body, name=name, grid=(s // tm,),
        in_specs=[row(Q_W, 0), row(KV_W, 3)] + [row(KV_W, 0)] * 10 + [row(HEAD_DIM, 0), row(HEAD_DIM, 0), one, one],
        out_specs=[row(IN_W, 0), one, one],
        out_shape=[SDS((s, IN_W), BF16), SDS((1, HEAD_DIM), F32), SDS((1, HEAD_DIM), F32)],
        compiler_params=_cp())(proj, proj, *dq, *dk, *dv, du, cos_t, sin_t, qg, kg)


def _masks(n, prev_or_next_valid):
    row = lax.broadcasted_iota(jnp.int32, (BLOCK, BLOCK), 0)
    col = lax.broadcasted_iota(jnp.int32, (BLOCK, BLOCK), 1)
    return col <= row, (col >= row) & prev_or_next_valid


def _attn_fwd(q_rot, k_rot, v, *, group, name):
    s = q_rot.shape[0]
    d = DILATIONS[group]
    ln = s // d
    nb = ln // BLOCK

    def body(q_ref, kc_ref, kp_ref, vc_ref, vp_ref, o_ref, lse_ref):
        n = pl.program_id(1)
        m_cur, m_prev = _masks(n, n > 0)
        for h in range(ATT_HEADS):
            sl = slice(h * HEAD_DIM, (h + 1) * HEAD_DIM)
            qh = q_ref[:, sl]
            s_c = jnp.where(m_cur, _dot_nt(qh, kc_ref[:, sl]), NEG_INF)
            s_p = jnp.where(m_prev, _dot_nt(qh, kp_ref[:, sl]), NEG_INF)
            m = jnp.maximum(jnp.max(s_c, axis=-1, keepdims=True), jnp.max(s_p, axis=-1, keepdims=True))
            p_c = jnp.exp(s_c - m)
            p_p = jnp.exp(s_p - m)
            l = jnp.sum(p_c, axis=-1, keepdims=True) + jnp.sum(p_p, axis=-1, keepdims=True)
            acc = _dot(p_c.astype(BF16), vc_ref[:, sl]) + _dot(p_p.astype(BF16), vp_ref[:, sl])
            o_ref[:, sl] = acc / l
            lse_ref[:, sl] = jnp.broadcast_to(m + jnp.log(l), (BLOCK, HEAD_DIM))

    cur = pl.BlockSpec((BLOCK, KV_W), lambda r, n: (n, r))
    prev = pl.BlockSpec((BLOCK, KV_W), lambda r, n: (jnp.maximum(n - 1, 0), r))
    o, lse = _pcall(
        body, name=name, grid=(d, nb),
        in_specs=[pl.BlockSpec((BLOCK, KV_W), lambda r, n: (n, r * N_GROUPS + group)), cur, prev, cur, prev],
        out_specs=[cur, cur],
        out_shape=[SDS((ln, d * KV_W), F32), SDS((ln, d * KV_W), F32)],
        compiler_params=_cp())(q_rot.reshape(ln, d * Q_W), k_rot.reshape(ln, d * KV_W), k_rot.reshape(ln, d * KV_W),
                               v.reshape(ln, d * KV_W), v.reshape(ln, d * KV_W))
    return o.reshape(s, KV_W), lse.reshape(s, KV_W)


def _attn_bwd(q_rot, k_rot, v, mix, dmix, lse, *, group, name):
    s = q_rot.shape[0]
    d = DILATIONS[group]
    ln = s // d
    nb = ln // BLOCK
    mix_w = mix.shape[1]

    def body(q_ref, qx_ref, kp_ref, kc_ref, vp_ref, vc_ref, do_ref, dox_ref, o_ref, ox_ref, l_ref, lx_ref,
             dq_ref, dk_ref, dv_ref):
        n = pl.program_id(1)
        m_cur, m_prev = _masks(n, n > 0)
        _, m_next = _masks(n, n + 1 < nb)
        for h in range(ATT_HEADS):
            sl = slice(h * HEAD_DIM, (h + 1) * HEAD_DIM)
            q, qx = q_ref[:, sl], qx_ref[:, sl]
            kp, kc, vp, vc = kp_ref[:, sl], kc_ref[:, sl], vp_ref[:, sl], vc_ref[:, sl]
            do, dox = do_ref[:, sl], dox_ref[:, sl]
            lse_n, lse_x = l_ref[:, sl][:, :1], lx_ref[:, sl][:, :1]
            dl_n = jnp.sum(do.astype(F32) * o_ref[:, sl].astype(F32), axis=-1, keepdims=True)
            dl_x = jnp.sum(dox.astype(F32) * ox_ref[:, sl].astype(F32), axis=-1, keepdims=True)

            def pair(qq, kk, vv, dd, lse_r, dl_r, mask):
                p = jnp.where(mask, jnp.exp(_dot_nt(qq, kk) - lse_r), 0.0)
                ds = p * (_dot_nt(dd, vv) - dl_r)
                return p.astype(BF16), ds.astype(BF16)

            p_a, ds_a = pair(q, kc, vc, do, lse_n, dl_n, m_cur)
            _, ds_b = pair(q, kp, vp, do, lse_n, dl_n, m_prev)
            p_c, ds_c = pair(qx, kc, vc, dox, lse_x, dl_x, m_next)
            dq_ref[:, sl] = (_dot(ds_a, kc) + _dot(ds_b, kp)).astype(BF16)
            dk_ref[:, sl] = _dot_tn(ds_a, q) + _dot_tn(ds_c, qx)
            dv_ref[:, sl] = _dot_tn(p_a, do) + _dot_tn(p_c, dox)

    nxt = lambda n: jnp.minimum(n + 1, nb - 1)
    prv = lambda n: jnp.maximum(n - 1, 0)
    blk = lambda f: pl.BlockSpec((BLOCK, KV_W), f)
    qcol = lambda r: r * N_GROUPS + group
    mcol = lambda r: r * (mix_w // KV_W)
    qv = q_rot.reshape(ln, d * Q_W)
    kv = k_rot.reshape(ln, d * KV_W)
    vv = v.reshape(ln, d * KV_W)
    mv = mix.reshape(ln, d * mix_w)
    dmv = dmix.reshape(ln, d * mix_w)
    lv = lse.reshape(ln, d * KV_W)
    dq, dk, dv = _pcall(
        body, name=name, grid=(d, nb),
        in_specs=[blk(lambda r, n: (n, qcol(r))), blk(lambda r, n: (nxt(n), qcol(r))),
                  blk(lambda r, n: (prv(n), r)), blk(lambda r, n: (n, r)),
                  blk(lambda r, n: (prv(n), r)), blk(lambda r, n: (n, r)),
                  blk(lambda r, n: (n, mcol(r))), blk(lambda r, n: (nxt(n), mcol(r))),
                  blk(lambda r, n: (n, mcol(r))), blk(lambda r, n: (nxt(n), mcol(r))),
                  blk(lambda r, n: (n, r)), blk(lambda r, n: (nxt(n), r))],
        out_specs=[blk(lambda r, n: (n, r))] * 3,
        out_shape=[SDS((ln, d * KV_W), BF16), SDS((ln, d * KV_W), F32), SDS((ln, d * KV_W), F32)],
        compiler_params=_cp())(qv, qv, kv, kv, vv, vv, dmv, dmv, mv, mv, lv, lv)
    return dq.reshape(s, KV_W), dk.reshape(s, KV_W), dv.reshape(s, KV_W)


def _combine(o, lse, pooled, *, name):
    s = pooled.shape[0]
    tm = _rows(s)

    def body(o0, o1, o2, l0, l1, l2, p_ref, mix_ref, lse_ref):
        a, b, c = l0[...], l1[...], l2[...]
        m = jnp.maximum(jnp.maximum(a, b), c)
        wa, wb, wc = jnp.exp(a - m), jnp.exp(b - m), jnp.exp(c - m)
        den = wa + wb + wc
        mix_ref[:, :KV_W] = ((wa * o0[...] + wb * o1[...] + wc * o2[...]) / den).astype(BF16)
        mix_ref[:, KV_W:] = p_ref[...]
        lse_ref[...] = m + jnp.log(den)

    row = pl.BlockSpec((tm, KV_W), lambda i: (i, 0))
    return _pcall(
        body, name=name, grid=(s // tm,), in_specs=[row] * 7,
        out_specs=[pl.BlockSpec((tm, KV_W + POOL_W), lambda i: (i, 0)), row],
        out_shape=[SDS((s, KV_W + POOL_W), BF16), SDS((s, KV_W), F32)],
        compiler_params=_cp())(*o, *lse, pooled)


def _pool_d(u_ref, halo_ref, i, tm):
    halo = jnp.where(i > 0, halo_ref[...], 0.0)
    t = i * tm + lax.broadcasted_iota(jnp.int32, (tm, 1), 0)
    out = []
    for g, w in enumerate(POOL_WINDOWS):
        sl = slice(g * HEAD_DIM, (g + 1) * HEAD_DIM)
        u = u_ref[:, sl]
        acc = jnp.concatenate([halo[:, sl], u], axis=0)
        sh = 1
        while sh < w:
            acc = acc + pltpu.roll(acc, sh, 0)
            sh *= 2
        cnt = jnp.minimum(t + 1, w).astype(F32)
        out.append(acc[HALO:, :] / cnt - u)
    return out


def _pool_fwd(proj, pool_w, pool_scale, *, name):
    s = proj.shape[0]
    tm = _rows(s)
    ucol = (IN_W - POOL_W) // POOL_W

    def body(u_ref, halo_ref, w_ref, sc_ref, o_ref):
        dd = _pool_d(u_ref, halo_ref, pl.program_id(0), tm)
        for g in range(len(POOL_WINDOWS)):
            sl = slice(g * HEAD_DIM, (g + 1) * HEAD_DIM)
            y = _dot(dd[g].astype(BF16), w_ref[g].astype(BF16))
            o_ref[:, sl] = (y * sc_ref[:, sl]).astype(BF16)

    return _pcall(
        body, name=name, grid=(s // tm,),
        in_specs=[pl.BlockSpec((tm, POOL_W), lambda i: (i, ucol)),
                  pl.BlockSpec((HALO, POOL_W), lambda i: (jnp.maximum(i * (tm // HALO) - 1, 0), ucol)),
                  pl.BlockSpec((len(POOL_WINDOWS), HEAD_DIM, HEAD_DIM), lambda i: (0, 0, 0)),
                  pl.BlockSpec((1, POOL_W), lambda i: (0, 0))],
        out_specs=pl.BlockSpec((tm, POOL_W), lambda i: (i, 0)),
        out_shape=SDS((s, POOL_W), BF16), compiler_params=_cp())(proj, proj, pool_w, pool_scale)


def _pool_bwd(proj, dmix, pool_w, pool_scale, *, name):
    s = proj.shape[0]
    tm = _rows(s)
    nblk = s // tm
    ucol = (IN_W - POOL_W) // POOL_W
    ng = len(POOL_WINDOWS)

    def body(u_ref, halo_ref, dp_ref, dpn_ref, w_ref, sc_ref, du_ref, dw_ref, dsc_ref):
        i = pl.program_id(0)

        @pl.when(i == 0)
        def _():
            dw_ref[...] = jnp.zeros_like(dw_ref)
            dsc_ref[...] = jnp.zeros_like(dsc_ref)

        dd = _pool_d(u_ref, halo_ref, i, tm)
        t = i * tm + lax.broadcasted_iota(jnp.int32, (tm, 1), 0)
        dpn = jnp.where(i + 1 < nblk, dpn_ref[...].astype(F32), 0.0)
        for g, w in enumerate(POOL_WINDOWS):
            sl = slice(g * HEAD_DIM, (g + 1) * HEAD_DIM)
            wg = w_ref[g].astype(BF16)
            db = dd[g].astype(BF16)
            dp = dp_ref[:, sl].astype(F32)
            dsc_ref[:, sl] += jnp.sum(dp * _dot(db, wg), axis=0, keepdims=True)
            dy = (dp * sc_ref[:, sl]).astype(BF16)
            dw_ref[g] += _dot_tn(db, dy)
            g_d = _dot_nt(dy, wg)
            g_dn = _dot_nt((dpn[:, sl] * sc_ref[:, sl]).astype(BF16), wg)
            cnt = jnp.minimum(t + 1, w).astype(F32)
            acc = jnp.concatenate([g_d / cnt, g_dn * (1.0 / w)], axis=0)
            sh = 1
            while sh < w:
                acc = acc + pltpu.roll(acc, tm + HALO - sh, 0)
                sh *= 2
            du_ref[:, sl] = acc[:tm, :] - g_d

    nh = s // HALO
    return _pcall(
        body, name=name, grid=(nblk,),
        in_specs=[pl.BlockSpec((tm, POOL_W), lambda i: (i, ucol)),
                  pl.BlockSpec((HALO, POOL_W), lambda i: (jnp.maximum(i * (tm // HALO) - 1, 0), ucol)),
                  pl.BlockSpec((tm, POOL_W), lambda i: (i, 1)),
                  pl.BlockSpec((HALO, POOL_W), lambda i: (jnp.minimum((i + 1) * (tm // HALO), nh - 1), 1)),
                  pl.BlockSpec((ng, HEAD_DIM, HEAD_DIM), lambda i: (0, 0, 0)),
                  pl.BlockSpec((1, POOL_W), lambda i: (0, 0))],
        out_specs=[pl.BlockSpec((tm, POOL_W), lambda i: (i, 0)),
                   pl.BlockSpec((ng, HEAD_DIM, HEAD_DIM), lambda i: (0, 0, 0)),
                   pl.BlockSpec((1, POOL_W), lambda i: (0, 0))],
        out_shape=[SDS((s, POOL_W), F32), SDS((ng, HEAD_DIM, HEAD_DIM), F32), SDS((1, POOL_W), F32)],
        compiler_params=_cp())(proj, proj, dmix, dmix, pool_w, pool_scale)


def _mem_fwd(mem, mem_g, wckv, ck_g, *, name):
    def body(m_ref, g_ref, w_ref, kg_ref, mn_ref, ckr_ref, ckn_ref, cv_ref):
        mv = m_ref[...]
        mn = (mv * _rms_r(mv) * g_ref[...]).astype(BF16)
        mn_ref[...] = mn
        ckv = _dot(mn, w_ref[...])
        ckr_ref[...] = ckv[:, :X_W]
        cv_ref[...] = ckv[:, X_W:].astype(BF16)
        for h in range(ATT_HEADS):
            sl = slice(h * HEAD_DIM, (h + 1) * HEAD_DIM)
            y = ckv[:, sl]
            ckn_ref[:, sl] = (y * _rms_r(y) * kg_ref[...]).astype(BF16)

    return _pcall(
        body, name=name,
        out_shape=[SDS((N_MEM, D_MODEL), BF16), SDS((N_MEM, X_W), F32), SDS((N_MEM, X_W), BF16),
                   SDS((N_MEM, X_W), BF16)],
        compiler_params=_cp())(mem, mem_g, wckv, ck_g)


def _cross_q(cq_ref, g_ref, sl):
    y = cq_ref[:, sl]
    r = _rms_r(y)
    return y, r, y * r * g_ref[...] * SCALE


def _cross_fwd(cq_raw, ck_n, cv, cq_g, *, name):
    s = cq_raw.shape[0]
    tm = _rows(s)

    def body(cq_ref, k_ref, v_ref, g_ref, o_ref):
        for h in range(ATT_HEADS):
            sl = slice(h * HEAD_DIM, (h + 1) * HEAD_DIM)
            _, _, qn = _cross_q(cq_ref, g_ref, sl)
            sc = _dot_nt(qn.astype(BF16), k_ref[:, sl])
            p = jnp.exp(sc - jnp.max(sc, axis=-1, keepdims=True))
            p = p / jnp.sum(p, axis=-1, keepdims=True)
            o_ref[:, sl] = _dot(p.astype(BF16), v_ref[:, sl]).astype(BF16)

    full = lambda a: pl.BlockSpec(a.shape, lambda i: (0, 0))
    return _pcall(
        body, name=name, grid=(s // tm,),
        in_specs=[pl.BlockSpec((tm, X_W), lambda i: (i, 0)), full(ck_n), full(cv), full(cq_g)],
        out_specs=pl.BlockSpec((tm, X_W), lambda i: (i, 0)),
        out_shape=SDS((s, X_W), BF16), compiler_params=_cp())(cq_raw, ck_n, cv, cq_g)


def _cross_bwd(d_o, cq_raw, ck_n, cv, cq_g, *, name):
    s = cq_raw.shape[0]
    tm = _rows(s)

    def body(do_ref, cq_ref, k_ref, v_ref, g_ref, dcq_ref, dk_ref, dv_ref, dg_ref):
        @pl.when(pl.program_id(0) == 0)
        def _():
            dk_ref[...] = jnp.zeros_like(dk_ref)
            dv_ref[...] = jnp.zeros_like(dv_ref)
            dg_ref[...] = jnp.zeros_like(dg_ref)

        dg = jnp.zeros((1, HEAD_DIM), F32)
        for h in range(ATT_HEADS):
            sl = slice(h * HEAD_DIM, (h + 1) * HEAD_DIM)
            y, r, qn = _cross_q(cq_ref, g_ref, sl)
            qb = qn.astype(BF16)
            do = do_ref[:, sl]
            sc = _dot_nt(qb, k_ref[:, sl])
            p = jnp.exp(sc - jnp.max(sc, axis=-1, keepdims=True))
            p = p / jnp.sum(p, axis=-1, keepdims=True)
            dv_ref[:, sl] += _dot_tn(p.astype(BF16), do)
            dp = _dot_nt(do, v_ref[:, sl])
            ds = (p * (dp - jnp.sum(dp * p, axis=-1, keepdims=True))).astype(BF16)
            dk_ref[:, sl] += _dot_tn(ds, qb)
            dn = _dot(ds, k_ref[:, sl]) * SCALE
            dy, dgain = _norm_bwd(y, r, g_ref[...], dn)
            dcq_ref[:, sl] = dy.astype(BF16)
            dg = dg + dgain
        dg_ref[...] += dg

    full = lambda a: pl.BlockSpec(a.shape, lambda i: (0, 0))
    row = pl.BlockSpec((tm, X_W), lambda i: (i, 0))
    return _pcall(
        body, name=name, grid=(s // tm,),
        in_specs=[row, row, full(ck_n), full(cv), full(cq_g)],
        out_specs=[row, pl.BlockSpec((N_MEM, X_W), lambda i: (0, 0)), pl.BlockSpec((N_MEM, X_W), lambda i: (0, 0)),
                   pl.BlockSpec((1, HEAD_DIM), lambda i: (0, 0))],
        out_shape=[SDS((s, X_W), BF16), SDS((N_MEM, X_W), F32), SDS((N_MEM, X_W), F32), SDS((1, HEAD_DIM), F32)],
        compiler_params=_cp())(d_o, cq_raw, ck_n, cv, cq_g)


def _mem_bwd(dck_n, dcv, ck_raw, memn, mem, wckv, mem_g, ck_g, *, name):
    def body(dk_ref, dv_ref, ckr_ref, mn_ref, m_ref, w_ref, mg_ref, kg_ref, dw_ref, dmg_ref, dkg_ref, dckv_s):
        dkg = jnp.zeros((1, HEAD_DIM), F32)
        for h in range(ATT_HEADS):
            sl = slice(h * HEAD_DIM, (h + 1) * HEAD_DIM)
            y = ckr_ref[:, sl]
            dy, dgain = _norm_bwd(y, _rms_r(y), kg_ref[...], dk_ref[:, sl])
            dckv_s[:, sl] = dy.astype(BF16)
            dkg = dkg + dgain
        dkg_ref[...] = dkg
        dckv_s[:, X_W:] = dv_ref[...].astype(BF16)
        dckv = dckv_s[...]
        dw_ref[...] = _dot_tn(mn_ref[...], dckv).astype(BF16)
        dmn = _dot_nt(dckv, w_ref[...])
        mv = m_ref[...]
        dmg_ref[...] = jnp.sum(dmn * mv * _rms_r(mv), axis=0, keepdims=True)

    return _pcall(
        body, name=name,
        out_shape=[SDS((D_MODEL, 2 * X_W), BF16), SDS((1, D_MODEL), F32), SDS((1, HEAD_DIM), F32)],
        scratch_shapes=[pltpu.VMEM((N_MEM, 2 * X_W), BF16)],
        compiler_params=_cp())(dck_n, dcv, ck_raw, memn, mem, wckv, mem_g, ck_g)


def _rope_tables(pos):
    inv_freq = ROPE_THETA ** (-jnp.arange(0, ROT_DIM, 2, dtype=F32) / ROT_DIM)
    ang = pos.astype(F32)[:, None] * inv_freq
    cos, sin = jnp.cos(ang), jnp.sin(ang)
    s = pos.shape[0]
    rest = HEAD_DIM - ROT_DIM
    cos_t = jnp.concatenate([cos, cos, jnp.ones((s, rest), F32)], axis=1)
    sin_t = jnp.concatenate([-sin, sin, jnp.zeros((s, rest), F32)], axis=1)
    return cos_t, sin_t


def _local_step(x, mem, pos, tgt, wb, sm):
    cos_t, sin_t = _rope_tables(pos)

    proj, xn = _norm_linear(x, sm["mix_norm_g"], wb["w_in"], tn=768, out_dtype=F32, name="fwd_in_proj")
    q_rot, k_rot, v = _qk_prep(proj, cos_t, sin_t, sm["q_norm_g"], sm["k_norm_g"], name="fwd_qk_prep")
    o, lse = [], []
    for g in range(N_GROUPS):
        o_g, lse_g = _attn_fwd(q_rot, k_rot, v, group=g, name=f"fwd_attn{g}")
        o.append(o_g)
        lse.append(lse_g)
    pooled = _pool_fwd(proj, sm["pool_w"], sm["pool_scale"], name="fwd_pool")
    mix, lse_all = _combine(o, lse, pooled, name="fwd_combine")
    h1 = _linear_res(mix, wb["w_out"], x, tn=1024, name="fwd_out_proj")
    cq_raw, hn = _norm_linear(h1, sm["cross_norm_g"], wb["w_cq"], tn=512, out_dtype=F32, name="fwd_cq_proj")
    memn, ck_raw, ck_n, cv = _mem_fwd(mem, sm["mem_norm_g"], wb["w_ckv"], sm["ck_norm_g"], name="fwd_mem")
    co = _cross_fwd(cq_raw, ck_n, cv, sm["cq_norm_g"], name="fwd_cross")
    h2 = _linear_res(co, wb["w_co"], h1, tn=1024, name="fwd_co_proj")
    gu, act, fn = _norm_linear_swiglu(h2, sm["ffn_norm_g"], wb["w_gate_up"], name="fwd_gate_up")
    dy, dyb, sq = _linear_res_loss(act, wb["w_down"], h2, tgt, name="fwd_down_loss")

    gw = {}
    gs = {}
    dgu = _swiglu_bwd(dyb, wb["w_down"], gu, name="bwd_swiglu")
    gw["w_down"] = _dw_tn(act, dyb, tkw=1024, tn=1024, name="bwd_dw_down")
    dh2, dh2b, gs["ffn_norm_g"] = _linear_nt_normbwd(dgu, wb["w_gate_up"], h2, dy, sm["ffn_norm_g"], tk=FF_TILE,
                                                     name="bwd_ffn_in")
    gw["w_gate_up"] = _dw_tn(fn, dgu, tkw=1024, tn=1536, name="bwd_dw_gate_up")

    d_co = _linear_nt(dh2b, wb["w_co"], tn=512, name="bwd_co_proj")
    gw["w_co"] = _dw_tn(co, dh2b, tkw=512, tn=1024, name="bwd_dw_co")
    dcq, dck_n, dcv, gs["cq_norm_g"] = _cross_bwd(d_co, cq_raw, ck_n, cv, sm["cq_norm_g"], name="bwd_cross")
    gw["w_ckv"], gs["mem_norm_g"], gs["ck_norm_g"] = _mem_bwd(dck_n, dcv, ck_raw, memn, mem, wb["w_ckv"],
                                                             sm["mem_norm_g"], sm["ck_norm_g"], name="bwd_mem")
    dh1, dh1b, gs["cross_norm_g"] = _linear_nt_normbwd(dcq, wb["w_cq"], h1, dh2, sm["cross_norm_g"], tk=X_W,
                                                       name="bwd_cq_in")
    gw["w_cq"] = _dw_tn(hn, dcq, tkw=1024, tn=512, name="bwd_dw_cq")

    dmix = _linear_nt(dh1b, wb["w_out"], tn=1024, name="bwd_out_proj")
    gw["w_out"] = _dw_tn(mix, dh1b, tkw=1024, tn=1024, name="bwd_dw_out")
    du, gs["pool_w"], gs["pool_scale"] = _pool_bwd(proj, dmix, sm["pool_w"], sm["pool_scale"], name="bwd_pool")
    dq, dk, dv = [], [], []
    for g in range(N_GROUPS):
        dq_g, dk_g, dv_g = _attn_bwd(q_rot, k_rot, v, mix, dmix, lse_all, group=g, name=f"bwd_attn{g}")
        dq.append(dq_g)
        dk.append(dk_g)
        dv.append(dv_g)
    dproj, gs["q_norm_g"], gs["k_norm_g"] = _qk_prep_bwd(proj, dq, dk, dv, du, cos_t, sin_t, sm["q_norm_g"],
                                                         sm["k_norm_g"], name="bwd_qk_prep")
    dx, _, gs["mix_norm_g"] = _linear_nt_normbwd(dproj, wb["w_in"], x, dh1, sm["mix_norm_g"], tk=768,
                                                 name="bwd_in_proj")
    gw["w_in"] = _dw_tn(xn, dproj, tkw=1024, tn=1536, name="bwd_dw_in")
    return sq, dx, gw, gs


SHARDED = ("w_in", "w_out", "w_cq", "w_ckv", "w_co", "w_gate_up", "w_down")
SMALL = ("mix_norm_g", "q_norm_g", "k_norm_g", "pool_w", "pool_scale", "cross_norm_g", "mem_norm_g", "cq_norm_g",
         "ck_norm_g", "ffn_norm_g")
FULL = {
    "w_in": ((D_MODEL, IN_W), 1, IN_W // N_DEV),
    "w_out": ((D_MODEL, D_MODEL), 0, D_MODEL // N_DEV),
    "w_cq": ((D_MODEL, X_W), 0, D_MODEL // N_DEV),
    "w_ckv": ((D_MODEL, 2 * X_W), 0, D_MODEL // N_DEV),
    "w_co": ((X_W, D_MODEL), 1, D_MODEL // N_DEV),
    "w_gate_up": ((D_MODEL, 2 * FF_PAD), 1, FF_TILE),
    "w_down": ((FF_PAD, D_MODEL), 0, DOWN_SHARD),
}


def _shard_shape(name):
    shape, axis, width = FULL[name]
    return tuple(width if a == axis else n for a, n in enumerate(shape))


def _window(ref, name, dev):
    _, axis, width = FULL[name]
    if name == "w_down":
        start = pl.multiple_of((dev // 2) * FF_TILE + (dev % 2) * DOWN_SHARD, HALO)
    else:
        start = pl.multiple_of(dev * width, BLOCK)
    return ref.at[pl.ds(start, width), :] if axis == 0 else ref.at[:, pl.ds(start, width)]


def _mesh_place():
    x, y, c = lax.axis_index("x"), lax.axis_index("y"), lax.axis_index("c")
    return x, y, c, 4 * x + 2 * y + c


def _peer(x, y, c, k):
    px = 1 - x if k & 4 else x
    py = 1 - y if k & 2 else y
    pc = 1 - c if k & 1 else c
    return (px, py, pc), 4 * px + 2 * py + pc


HBM_SPEC = pl.BlockSpec(memory_space=pltpu.HBM)


def _gather_weights(shards):
    names = SHARDED
    nw = len(names)
    n_pad = FF_PAD // FF_TILE

    def body(*refs):
        ins, outs = refs[:nw], refs[nw:2 * nw]
        zero_ref, send_sems, recv_sems, local_sems, zero_sems = refs[2 * nw:]
        x, y, c, me = _mesh_place()
        zero_ref[...] = jnp.zeros_like(zero_ref)
        down = outs[names.index("w_down")]
        pads = [pltpu.make_async_copy(zero_ref, down.at[pl.ds(t * FF_TILE + FF_SHARD, FF_TILE - FF_SHARD), :],
                                      zero_sems.at[t]) for t in range(n_pad)]
        for cp in pads:
            cp.start()
        local, sent = [], []
        for wi, name in enumerate(names):
            cp = pltpu.make_async_copy(ins[wi], _window(outs[wi], name, me), local_sems.at[wi])
            cp.start()
            local.append(cp)
            for k in range(1, N_DEV):
                peer, _ = _peer(x, y, c, k)
                cp = pltpu.make_async_remote_copy(
                    src_ref=ins[wi], dst_ref=_window(outs[wi], name, me), send_sem=send_sems.at[wi, k - 1],
                    recv_sem=recv_sems.at[wi, k - 1], device_id=peer, device_id_type=MESH)
                cp.start()
                sent.append(cp)
        for wi, name in enumerate(names):
            for k in range(1, N_DEV):
                peer, pidx = _peer(x, y, c, k)
                pltpu.make_async_remote_copy(
                    src_ref=ins[wi], dst_ref=_window(outs[wi], name, pidx), send_sem=send_sems.at[wi, k - 1],
                    recv_sem=recv_sems.at[wi, k - 1], device_id=peer, device_id_type=MESH).wait_recv()
        for cp in sent:
            cp.wait_send()
        for cp in local + pads:
            cp.wait()

    outs = _pcall(
        body, name="gather_weights",
        in_specs=[HBM_SPEC] * nw, out_specs=[HBM_SPEC] * nw,
        out_shape=[SDS(FULL[n][0], BF16) for n in names],
        scratch_shapes=[pltpu.VMEM((FF_TILE - FF_SHARD, D_MODEL), BF16), pltpu.SemaphoreType.DMA((nw, N_DEV - 1)),
                        pltpu.SemaphoreType.DMA((nw, N_DEV - 1)), pltpu.SemaphoreType.DMA((nw,)),
                        pltpu.SemaphoreType.DMA((n_pad,))],
        compiler_params=pltpu.CompilerParams(has_side_effects=True))(*[shards[n] for n in names])
    return dict(zip(names, outs))


def _exchange_grads(gw, small):
    names = SHARDED
    nw = len(names)

    def body(*refs):
        ins, small_ref = refs[:nw], refs[nw]
        outs, small_out = refs[nw + 1:2 * nw + 1], refs[2 * nw + 1]
        send_sems, recv_sems, local_sems = refs[2 * nw + 2:]
        x, y, c, me = _mesh_place()
        local, sent = [], []
        for wi, name in enumerate(names):
            cp = pltpu.make_async_copy(_window(ins[wi], name, me), outs[wi].at[0], local_sems.at[wi])
            cp.start()
            local.append(cp)
        cp = pltpu.make_async_copy(small_ref, small_out.at[me], local_sems.at[nw])
        cp.start()
        local.append(cp)
        for k in range(1, N_DEV):
            peer, pidx = _peer(x, y, c, k)
            for wi, name in enumerate(names):
                cp = pltpu.make_async_remote_copy(
                    src_ref=_window(ins[wi], name, pidx), dst_ref=outs[wi].at[k], send_sem=send_sems.at[wi, k - 1],
                    recv_sem=recv_sems.at[wi, k - 1], device_id=peer, device_id_type=MESH)
                cp.start()
                sent.append(cp)
            cp = pltpu.make_async_remote_copy(
                src_ref=small_ref, dst_ref=small_out.at[me], send_sem=send_sems.at[nw, k - 1],
                recv_sem=recv_sems.at[nw, k - 1], device_id=peer, device_id_type=MESH)
            cp.start()
            sent.append(cp)
        for k in range(1, N_DEV):
            peer, pidx = _peer(x, y, c, k)
            for wi, name in enumerate(names):
                pltpu.make_async_remote_copy(
                    src_ref=_window(ins[wi], name, me), dst_ref=outs[wi].at[k], send_sem=send_sems.at[wi, k - 1],
                    recv_sem=recv_sems.at[wi, k - 1], device_id=peer, device_id_type=MESH).wait_recv()
            pltpu.make_async_remote_copy(
                src_ref=small_ref, dst_ref=small_out.at[pidx], send_sem=send_sems.at[nw, k - 1],
                recv_sem=recv_sems.at[nw, k - 1], device_id=peer, device_id_type=MESH).wait_recv()
        for cp in sent:
            cp.wait_send()
        for cp in local:
            cp.wait()

    outs = _pcall(
        body, name="exchange_grads",
        in_specs=[HBM_SPEC] * (nw + 1), out_specs=[HBM_SPEC] * (nw + 1),
        out_shape=[SDS((N_DEV,) + _shard_shape(n), BF16) for n in names] + [SDS((N_DEV,) + small.shape, F32)],
        scratch_shapes=[pltpu.SemaphoreType.DMA((nw + 1, N_DEV - 1)), pltpu.SemaphoreType.DMA((nw + 1, N_DEV - 1)),
                        pltpu.SemaphoreType.DMA((nw + 1,))],
        compiler_params=pltpu.CompilerParams(has_side_effects=True))(*[gw[n] for n in names], small)
    return dict(zip(names, outs[:nw])), outs[nw]


def _adamw(parts, w, m, v, *, name):
    r, c = w.shape
    tr = r
    for cand in (256, 128, 88, 64, 8):
        if r % cand == 0:
            tr = cand
            break

    def body(p_ref, w_ref, m_ref, v_ref, g_ref, d_ref, mo_ref, vo_ref):
        g = p_ref[0].astype(F32)
        for k in range(1, N_DEV):
            g = g + p_ref[k].astype(F32)
        m_new = ADAM_B1 * m_ref[...] + (1.0 - ADAM_B1) * g
        v_new = ADAM_B2 * v_ref[...] + (1.0 - ADAM_B2) * (g * g)
        m_hat = m_new / (1.0 - ADAM_B1 ** ADAM_STEP)
        v_hat = v_new / (1.0 - ADAM_B2 ** ADAM_STEP)
        g_ref[...] = g
        d_ref[...] = -ADAM_LR * (m_hat / (jnp.sqrt(v_hat) + ADAM_EPS) + ADAM_WD * w_ref[...])
        mo_ref[...] = m_new
        vo_ref[...] = v_new

    row = pl.BlockSpec((tr, c), lambda i: (i, 0))
    return _pcall(
        body, name=name, grid=(r // tr,),
        in_specs=[pl.BlockSpec((N_DEV, tr, c), lambda i: (0, i, 0)), row, row, row],
        out_specs=[row] * 4, out_shape=[SDS((r, c), F32)] * 4, compiler_params=_cp())(parts, w, m, v)


def _pack_small(d):
    return jnp.concatenate([d[n].reshape(-1, HEAD_DIM) for n in SMALL], axis=0)


def _unpack_small(packed, like):
    out, row = {}, 0
    for n in SMALL:
        rows = like[n].size // HEAD_DIM
        out[n] = packed[row:row + rows].reshape(like[n].shape)
        row += rows
    return out


def _pad_cols(a, width):
    return jnp.pad(a, ((0, 0), (0, width - a.shape[1])))


def kernel(x, mem, positions, mix_norm_g, w_in, q_norm_g, k_norm_g, pool_w, pool_scale, w_out, cross_norm_g, mem_norm_g, w_cq, w_ckv, cq_norm_g, ck_norm_g, w_co, ffn_norm_g, w_gate_up, w_down, loss_target, m_mix_norm_g, m_w_in, m_q_norm_g, m_k_norm_g, m_pool_w, m_pool_scale, m_w_out, m_cross_norm_g, m_mem_norm_g, m_w_cq, m_w_ckv, m_cq_norm_g, m_ck_norm_g, m_w_co, m_ffn_norm_g, m_w_gate_up, m_w_down, v_mix_norm_g, v_w_in, v_q_norm_g, v_k_norm_g, v_pool_w, v_pool_scale, v_w_out, v_cross_norm_g, v_mem_norm_g, v_w_cq, v_w_ckv, v_cq_norm_g, v_ck_norm_g, v_w_co, v_ffn_norm_g, v_w_gate_up, v_w_down):
    given = dict(locals())
    w_f32 = {n: given[n][0] for n in SHARDED + SMALL}
    m_f32 = {n: given["m_" + n][0] for n in SHARDED + SMALL}
    v_f32 = {n: given["v_" + n][0] for n in SHARDED + SMALL}
    for d in (w_f32, m_f32, v_f32):
        d["w_gate_up"] = _pad_cols(d["w_gate_up"], FF_TILE)

    wb = _gather_weights({n: w_f32[n].astype(BF16) for n in SHARDED})
    sm = {n: w_f32[n] for n in SMALL}
    sm_rows = {n: (a if a.ndim == 3 else a.reshape(1, -1)) for n, a in sm.items()}
    sq, dx, gw, gs = _local_step(x[0], mem[0], positions[0], loss_target[0], wb, sm_rows)
    loss = lax.psum(0.5 / D_MODEL * jnp.sum(sq), ("x", "y", "c"))

    parts, small_parts = _exchange_grads(gw, _pack_small(gs))
    res = {}
    for n in SHARDED:
        res[n] = _adamw(parts[n], w_f32[n], m_f32[n], v_f32[n], name="adamw_" + n)
    res["w_gate_up"] = [a[:, :FF_SHARD] for a in res["w_gate_up"]]
    small_res = _adamw(small_parts, _pack_small(sm), _pack_small({n: m_f32[n] for n in SMALL}),
                       _pack_small({n: v_f32[n] for n in SMALL}), name="adamw_small")
    small_res = [_unpack_small(a, sm) for a in small_res]
    order = ("mix_norm_g", "w_in", "q_norm_g", "k_norm_g", "pool_w", "pool_scale", "w_out", "cross_norm_g",
             "mem_norm_g", "w_cq", "w_ckv", "cq_norm_g", "ck_norm_g", "w_co", "ffn_norm_g", "w_gate_up", "w_down")
    outs = [loss, dx[None]]
    for which in range(4):
        for n in order:
            a = res[n][which] if n in SHARDED else small_res[which][n]
            outs.append(a[None])
    return tuple(outs)
```

```python
import functools

import jax
import jax.numpy as jnp
from jax import lax
from jax.experimental import pallas as pl
from jax.experimental.pallas import tpu as pltpu

F32 = jnp.float32
BF16 = jnp.bfloat16
SDS = jax.ShapeDtypeStruct

D_MODEL = 1024
HEAD_DIM = 128
N_GROUPS = 3
DILATIONS = (1, 4, 16)
ATT_HEADS = 4
Q_W = 1536
KV_W = 512
POOL_W = 512
POOL_WINDOWS = (2, 4, 8, 16)
IN_W = 3072
X_W = 512
N_MEM = 256
D_FF = 2816
FF_TILE = 768
FF_SHARD = 704
FF_PAD = 4 * FF_TILE
DOWN_SHARD = 352
ROT_DIM = 32
ROT_HALF = 16
ROPE_THETA = 500000.0
EPS = 1e-6
NEG_INF = -1e30
SCALE = HEAD_DIM ** -0.5
BLOCK = 128
HALO = 16

ADAM_LR = 0.001
ADAM_B1 = 0.9
ADAM_B2 = 0.999
ADAM_EPS = 1e-08
ADAM_WD = 0.01
ADAM_STEP = 10

N_DEV = 8
VMEM_LIMIT_BYTES = 56 * 1024 * 1024
MESH = pl.DeviceIdType.MESH


def _pcall(body, **kw):
    return pl.pallas_call(body, **kw)


def _cp():
    return pltpu.CompilerParams(vmem_limit_bytes=VMEM_LIMIT_BYTES)


def _dot(a, b):
    return lax.dot_general(a, b, (((1,), (0,)), ((), ())), preferred_element_type=F32)


def _dot_nt(a, b):
    return lax.dot_general(a, b, (((1,), (1,)), ((), ())), preferred_element_type=F32)


def _dot_tn(a, b):
    return lax.dot_general(a, b, (((0,), (0,)), ((), ())), preferred_element_type=F32)


def _rows(s):
    return min(512, s)


def _rms_r(x):
    return lax.rsqrt(jnp.mean(x * x, axis=-1, keepdims=True) + EPS)


def _norm_bwd(x, r, gain, dxn):
    z = dxn * gain
    dx = r * z - x * (r * r * r * jnp.mean(z * x, axis=-1, keepdims=True))
    dgain = jnp.sum(dxn * x * r, axis=0, keepdims=True)
    return dx, dgain


def _partner(t):
    lane = lax.broadcasted_iota(jnp.int32, t.shape, 1)
    return jnp.where(lane < ROT_HALF, pltpu.roll(t, HEAD_DIM - ROT_HALF, 1), pltpu.roll(t, ROT_HALF, 1))


def _rope(n, cos_t, sin_t):
    return n * cos_t + _partner(n) * sin_t


def _rope_bwd(d, cos_t, sin_t):
    lane = lax.broadcasted_iota(jnp.int32, d.shape, 1)
    return d * cos_t + jnp.where(lane < ROT_DIM, _partner(d * sin_t), 0.0)


def _resident(shape):
    return pl.BlockSpec(shape, lambda i: (0,) * len(shape), pipeline_mode=pl.Buffered(1))


def _chunks(n, cn):
    return [slice(j * cn, (j + 1) * cn) for j in range(n // cn)]


def _norm_linear(x, gain, w, *, cn, out_dtype, name):
    s, k = x.shape
    n = w.shape[1]
    tm = _rows(s)

    def body(x_ref, g_ref, w_ref, y_ref, xn_ref):
        xv = x_ref[...]
        xn_ref[...] = (xv * _rms_r(xv) * g_ref[...]).astype(BF16)
        for c in _chunks(n, cn):
            y_ref[:, c] = _dot(xn_ref[...], w_ref[:, c]).astype(out_dtype)

    row = lambda w_: pl.BlockSpec((tm, w_), lambda i: (i, 0))
    return _pcall(
        body, name=name, grid=(s // tm,),
        in_specs=[row(k), _resident((1, k)), _resident((k, n))],
        out_specs=[row(n), row(k)],
        out_shape=[SDS((s, n), out_dtype), SDS((s, k), BF16)], compiler_params=_cp())(x, gain, w)


def _norm_linear_swiglu(x, gain, wgu, *, name):
    s, k = x.shape
    tm = _rows(s)

    def body(x_ref, g_ref, w_ref, gu_ref, a_ref, xn_ref):
        xv = x_ref[...]
        xn_ref[...] = (xv * _rms_r(xv) * g_ref[...]).astype(BF16)
        for c in _chunks(FF_PAD, FF_TILE):
            g = _dot(xn_ref[...], w_ref[:, c])
            u = _dot(xn_ref[...], w_ref[:, slice(FF_PAD + c.start, FF_PAD + c.stop)])
            a_ref[:, c] = (g * jax.nn.sigmoid(g) * u).astype(BF16)
            gu_ref[0, :, c] = g.astype(BF16)
            gu_ref[1, :, c] = u.astype(BF16)

    row = lambda w_: pl.BlockSpec((tm, w_), lambda i: (i, 0))
    return _pcall(
        body, name=name, grid=(s // tm,),
        in_specs=[row(k), _resident((1, k)), _resident((k, 2 * FF_PAD))],
        out_specs=[pl.BlockSpec((2, tm, FF_PAD), lambda i: (0, i, 0)), row(FF_PAD), row(k)],
        out_shape=[SDS((2, s, FF_PAD), BF16), SDS((s, FF_PAD), BF16), SDS((s, k), BF16)],
        compiler_params=_cp())(x, gain, wgu)


def _linear_res(a, w, res, *, cn, name):
    s, k = a.shape
    n = w.shape[1]
    tm = _rows(s)

    def body(a_ref, w_ref, r_ref, y_ref):
        for c in _chunks(n, cn):
            y_ref[:, c] = r_ref[:, c] + _dot(a_ref[...], w_ref[:, c])

    row = lambda w_: pl.BlockSpec((tm, w_), lambda i: (i, 0))
    return _pcall(
        body, name=name, grid=(s // tm,),
        in_specs=[row(k), _resident((k, n)), row(n)], out_specs=row(n),
        out_shape=SDS((s, n), F32), compiler_params=_cp())(a, w, res)


def _linear_res_loss(a, w, res, tgt, *, name):
    s, k = a.shape
    n = w.shape[1]
    tm = _rows(s)

    def body(a_ref, w_ref, r_ref, t_ref, dy_ref, dyb_ref, sq_ref):
        e = r_ref[...] + _dot(a_ref[...], w_ref[...]) - t_ref[...]
        dy = e * (1.0 / n)
        dy_ref[...] = dy
        dyb_ref[...] = dy.astype(BF16)

        @pl.when(pl.program_id(0) == 0)
        def _():
            sq_ref[...] = jnp.zeros_like(sq_ref)
        sq_ref[...] += jnp.sum(e * e, axis=0, keepdims=True)

    row = lambda w_: pl.BlockSpec((tm, w_), lambda i: (i, 0))
    return _pcall(
        body, name=name, grid=(s // tm,),
        in_specs=[row(k), _resident((k, n)), row(n), row(n)],
        out_specs=[row(n), row(n), pl.BlockSpec((1, n), lambda i: (0, 0))],
        out_shape=[SDS((s, n), F32), SDS((s, n), BF16), SDS((1, n), F32)],
        compiler_params=_cp())(a, w, res, tgt)


def _linear_nt(g, w, *, cn, name):
    s, k = g.shape
    n = w.shape[0]
    tm = _rows(s)

    def body(g_ref, w_ref, y_ref):
        for c in _chunks(n, cn):
            y_ref[:, c] = _dot_nt(g_ref[...], w_ref[c, :]).astype(BF16)

    row = lambda w_: pl.BlockSpec((tm, w_), lambda i: (i, 0))
    return _pcall(
        body, name=name, grid=(s // tm,),
        in_specs=[row(k), _resident((n, k))], out_specs=row(n),
        out_shape=SDS((s, n), BF16), compiler_params=_cp())(g, w)


def _swiglu_bwd(dyb, wd, gu, *, name):
    s, n = dyb.shape
    tm = _rows(s)

    def body(dy_ref, wd_ref, gu_ref, dgu_ref):
        for c in _chunks(FF_PAD, FF_TILE):
            da = _dot_nt(dy_ref[...], wd_ref[c, :])
            g = gu_ref[0, :, c].astype(F32)
            u = gu_ref[1, :, c].astype(F32)
            sg = jax.nn.sigmoid(g)
            dgu_ref[0, :, c] = (da * u * (sg * (1.0 + g * (1.0 - sg)))).astype(BF16)
            dgu_ref[1, :, c] = (da * (g * sg)).astype(BF16)

    half = pl.BlockSpec((2, tm, FF_PAD), lambda i: (0, i, 0))
    return _pcall(
        body, name=name, grid=(s // tm,),
        in_specs=[pl.BlockSpec((tm, n), lambda i: (i, 0)), _resident((FF_PAD, n)), half],
        out_specs=half, out_shape=SDS((2, s, FF_PAD), BF16), compiler_params=_cp())(dyb, wd, gu)


def _linear_nt_normbwd(g, w, x, dres, gain, *, name):
    d, k = w.shape
    s = x.shape[0]
    tm = _rows(s)

    def body(g_ref, w_ref, x_ref, dr_ref, gn_ref, dx_ref, dxb_ref, dg_ref):
        @pl.when(pl.program_id(0) == 0)
        def _():
            dg_ref[...] = jnp.zeros_like(dg_ref)

        if g.ndim == 3:
            dxn = _dot_nt(g_ref[0], w_ref[:, :k // 2]) + _dot_nt(g_ref[1], w_ref[:, k // 2:])
        else:
            dxn = _dot_nt(g_ref[...], w_ref[...])
        xv = x_ref[...]
        dx, dgain = _norm_bwd(xv, _rms_r(xv), gn_ref[...], dxn)
        out = dr_ref[...] + dx
        dx_ref[...] = out
        dxb_ref[...] = out.astype(BF16)
        dg_ref[...] += dgain

    row = pl.BlockSpec((tm, d), lambda i: (i, 0))
    g_spec = (pl.BlockSpec((2, tm, k // 2), lambda i: (0, i, 0)) if g.ndim == 3
              else pl.BlockSpec((tm, k), lambda i: (i, 0)))
    return _pcall(
        body, name=name, grid=(s // tm,),
        in_specs=[g_spec, _resident((d, k)), row, row, _resident((1, d))],
        out_specs=[row, row, pl.BlockSpec((1, d), lambda i: (0, 0))],
        out_shape=[SDS((s, d), F32), SDS((s, d), BF16), SDS((1, d), F32)],
        compiler_params=_cp())(g, w, x, dres, gain)


def _dw_tn(x, g, *, tkw, tn, name):
    s, kw = x.shape
    halves = g.ndim == 3
    n = 2 * g.shape[2] if halves else g.shape[1]
    ts = _rows(s)
    ns = s // ts
    per_half = n // 2 // tn

    def body(x_ref, g_ref, o_ref, acc_ref):
        ss = pl.program_id(2)

        @pl.when(ss == 0)
        def _():
            acc_ref[...] = jnp.zeros_like(acc_ref)

        acc_ref[...] += _dot_tn(x_ref[...], g_ref[...])

        @pl.when(ss == ns - 1)
        def _():
            o_ref[...] = acc_ref[...].astype(BF16)

    g_spec = (pl.BlockSpec((None, ts, tn), lambda a, b, ss: (b // per_half, ss, b % per_half)) if halves
              else pl.BlockSpec((ts, tn), lambda a, b, ss: (ss, b)))
    return _pcall(
        body, name=name, grid=(kw // tkw, n // tn, ns),
        in_specs=[pl.BlockSpec((ts, tkw), lambda a, b, ss: (ss, a)), g_spec],
        out_specs=pl.BlockSpec((tkw, tn), lambda a, b, ss: (a, b)),
        out_shape=SDS((kw, n), BF16),
        scratch_shapes=[pltpu.VMEM((tkw, tn), F32)], compiler_params=_cp())(x, g)


def _qk_prep(proj, cos_t, sin_t, qg, kg, *, name):
    s = proj.shape[0]
    tm = _rows(s)
    nqh = Q_W // HEAD_DIM

    def body(q_ref, k_ref, v_ref, c_ref, s_ref, qg_ref, kg_ref, qo_ref, ko_ref, vo_ref):
        c, sn = c_ref[...], s_ref[...]
        for h in range(nqh):
            sl = slice(h * HEAD_DIM, (h + 1) * HEAD_DIM)
            y = q_ref[:, sl]
            qo_ref[:, sl] = (_rope(y * _rms_r(y) * qg_ref[...], c, sn) * SCALE).astype(BF16)
        for h in range(ATT_HEADS):
            sl = slice(h * HEAD_DIM, (h + 1) * HEAD_DIM)
            y = k_ref[:, sl]
            ko_ref[:, sl] = _rope(y * _rms_r(y) * kg_ref[...], c, sn).astype(BF16)
        vo_ref[...] = v_ref[...].astype(BF16)

    row = lambda w, j: pl.BlockSpec((tm, w), lambda i: (i, j))
    one = pl.BlockSpec((1, HEAD_DIM), lambda i: (0, 0))
    return _pcall(
        body, name=name, grid=(s // tm,),
        in_specs=[row(Q_W, 0), row(KV_W, 3), row(KV_W, 4), row(HEAD_DIM, 0), row(HEAD_DIM, 0), one, one],
        out_specs=[row(Q_W, 0), row(KV_W, 0), row(KV_W, 0)],
        out_shape=[SDS((s, Q_W), BF16), SDS((s, KV_W), BF16), SDS((s, KV_W), BF16)],
        compiler_params=_cp())(proj, proj, proj, cos_t, sin_t, qg, kg)


def _qk_prep_bwd(proj, dq, dk, dv, du, cos_t, sin_t, qg, kg, *, name):
    s = proj.shape[0]
    tm = _rows(s)

    def body(q_ref, k_ref, dq_ref, dk_ref, dv_ref, du_ref, c_ref, s_ref, qg_ref, kg_ref, dp_ref, dqg_ref, dkg_ref):
        c, sn = c_ref[...], s_ref[...]

        @pl.when(pl.program_id(0) == 0)
        def _():
            dqg_ref[...] = jnp.zeros_like(dqg_ref)
            dkg_ref[...] = jnp.zeros_like(dkg_ref)

        dqg = jnp.zeros((1, HEAD_DIM), F32)
        for g in range(N_GROUPS):
            for h in range(ATT_HEADS):
                sl = slice(h * HEAD_DIM, (h + 1) * HEAD_DIM)
                col = slice(g * KV_W + h * HEAD_DIM, g * KV_W + (h + 1) * HEAD_DIM)
                y = q_ref[:, col]
                dn = _rope_bwd(dq_ref[g, :, sl].astype(F32) * SCALE, c, sn)
                dy, dgain = _norm_bwd(y, _rms_r(y), qg_ref[...], dn)
                dp_ref[:, col] = dy.astype(BF16)
                dqg = dqg + dgain
        dqg_ref[...] += dqg

        dkg = jnp.zeros((1, HEAD_DIM), F32)
        for h in range(ATT_HEADS):
            sl = slice(h * HEAD_DIM, (h + 1) * HEAD_DIM)
            y = k_ref[:, sl]
            dn = _rope_bwd(dk_ref[:, sl], c, sn)
            dy, dgain = _norm_bwd(y, _rms_r(y), kg_ref[...], dn)
            dp_ref[:, Q_W + h * HEAD_DIM:Q_W + (h + 1) * HEAD_DIM] = dy.astype(BF16)
            dkg = dkg + dgain
        dkg_ref[...] += dkg

        dp_ref[:, Q_W + KV_W:Q_W + 2 * KV_W] = dv_ref[...].astype(BF16)
        dp_ref[:, Q_W + 2 * KV_W:] = du_ref[...].astype(BF16)

    row = lambda w, j: pl.BlockSpec((tm, w), lambda i: (i, j))
    one = pl.BlockSpec((1, HEAD_DIM), lambda i: (0, 0))
    return _pcall(
        body, name=name, grid=(s // tm,),
        in_specs=[row(Q_W, 0), row(KV_W, 3), pl.BlockSpec((N_GROUPS, tm, KV_W), lambda i: (0, i, 0))]
        + [row(KV_W, 0)] * 3 + [row(HEAD_DIM, 0), row(HEAD_DIM, 0), one, one],
        out_specs=[row(IN_W, 0), one, one],
        out_shape=[SDS((s, IN_W), BF16), SDS((1, HEAD_DIM), F32), SDS((1, HEAD_DIM), F32)],
        compiler_params=_cp())(proj, proj, dq, dk, dv, du, cos_t, sin_t, qg, kg)


ATT_ROWS = 16 * BLOCK


def _sub(ref, start, d):
    return ref[pl.ds(start, BLOCK, stride=d), :] if d > 1 else ref[pl.ds(start, BLOCK), :]


def _sub_set(ref, start, d, val):
    if d > 1:
        ref[pl.ds(start, BLOCK, stride=d), :] = val
    else:
        ref[pl.ds(start, BLOCK), :] = val


def _band_masks():
    row = lax.broadcasted_iota(jnp.int32, (BLOCK, BLOCK), 0)
    col = lax.broadcasted_iota(jnp.int32, (BLOCK, BLOCK), 1)
    return col <= row, col >= row


def _attn_fwd(q_rot, k_rot, v, *, name):
    s = q_rot.shape[0]
    rr = ATT_ROWS
    nblk = s // rr

    def body(q0, q1, q2, kp, kc, vp, vc, mix_ref, lse_ref, qs, ks, vs, os_, ls):
        n = pl.program_id(1)
        for g, q_ref in enumerate((q0, q1, q2)):
            qs[g] = q_ref[...].astype(F32)
        ks[:rr] = kp[...].astype(F32)
        ks[rr:] = kc[...].astype(F32)
        vs[:rr] = vp[...].astype(F32)
        vs[rr:] = vc[...].astype(F32)
        m_cur, m_band = _band_masks()
        m_first = m_band & (n > 0)
        for g, d in enumerate(DILATIONS):
            for r in range(d):
                for j in range(rr // (BLOCK * d)):
                    base = j * BLOCK * d + r
                    prev = rr + base - BLOCK * d
                    q = _sub(qs.at[g], base, d).astype(BF16)
                    s_c = jnp.where(m_cur, _dot_nt(q, _sub(ks, rr + base, d).astype(BF16)), NEG_INF)
                    s_p = jnp.where(m_first if j == 0 else m_band, _dot_nt(q, _sub(ks, prev, d).astype(BF16)), NEG_INF)
                    m = jnp.maximum(jnp.max(s_c, axis=-1, keepdims=True), jnp.max(s_p, axis=-1, keepdims=True))
                    p_c = jnp.exp(s_c - m)
                    p_p = jnp.exp(s_p - m)
                    l = jnp.sum(p_c, axis=-1, keepdims=True) + jnp.sum(p_p, axis=-1, keepdims=True)
                    acc = (_dot(p_c.astype(BF16), _sub(vs, rr + base, d).astype(BF16))
                           + _dot(p_p.astype(BF16), _sub(vs, prev, d).astype(BF16)))
                    _sub_set(os_.at[g], base, d, acc / l)
                    _sub_set(ls.at[g], base, d, jnp.broadcast_to(m + jnp.log(l), (BLOCK, HEAD_DIM)))
        for c in _chunks(rr, 2 * BLOCK):
            a, b, cc = ls[0, c, :], ls[1, c, :], ls[2, c, :]
            m = jnp.maximum(jnp.maximum(a, b), cc)
            wa, wb, wc = jnp.exp(a - m), jnp.exp(b - m), jnp.exp(cc - m)
            den = wa + wb + wc
            mix_ref[c, :] = ((wa * os_[0, c, :] + wb * os_[1, c, :] + wc * os_[2, c, :]) / den).astype(BF16)
            lse_ref[c, :] = m + jnp.log(den)

    blk = lambda f: pl.BlockSpec((rr, HEAD_DIM), f)
    prv = lambda n: jnp.maximum(n - 1, 0)
    return _pcall(
        body, name=name, grid=(ATT_HEADS, nblk),
        in_specs=[blk(lambda h, n, g=g: (n, g * ATT_HEADS + h)) for g in range(N_GROUPS)]
        + [blk(lambda h, n: (prv(n), h)), blk(lambda h, n: (n, h))] * 2,
        out_specs=[blk(lambda h, n: (n, h)), blk(lambda h, n: (n, h))],
        out_shape=[SDS((s, KV_W + POOL_W), BF16), SDS((s, KV_W), F32)],
        scratch_shapes=[pltpu.VMEM((N_GROUPS, rr, HEAD_DIM), F32), pltpu.VMEM((2 * rr, HEAD_DIM), F32),
                        pltpu.VMEM((2 * rr, HEAD_DIM), F32), pltpu.VMEM((N_GROUPS, rr, HEAD_DIM), F32),
                        pltpu.VMEM((N_GROUPS, rr, HEAD_DIM), F32)],
        compiler_params=_cp())(q_rot, q_rot, q_rot, k_rot, k_rot, v, v)


def _attn_bwd(q_rot, k_rot, v, mix, dmix, lse, *, name):
    s = q_rot.shape[0]
    rr = ATT_ROWS
    nblk = s // rr

    def body(q0, q1, q2, qx0, qx1, qx2, kp, kc, vp, vc, do_c, do_x, o_c, o_x, l_c, l_x,
             dq_ref, dk_ref, dv_ref, qs, ks, vs, dos, lss, dls, dqs, dks, dvs):
        n = pl.program_id(1)
        for g, (qc_ref, qx_ref) in enumerate(((q0, qx0), (q1, qx1), (q2, qx2))):
            qs[g, :rr] = qc_ref[...].astype(F32)
            qs[g, rr:] = qx_ref[...].astype(F32)
        ks[:rr] = kp[...].astype(F32)
        ks[rr:] = kc[...].astype(F32)
        vs[:rr] = vp[...].astype(F32)
        vs[rr:] = vc[...].astype(F32)
        lss[:rr] = l_c[...]
        lss[rr:] = l_x[...]
        for half, (d_ref, o_ref) in enumerate(((do_c, o_c), (do_x, o_x))):
            for c in _chunks(rr, 2 * BLOCK):
                cs = slice(half * rr + c.start, half * rr + c.stop)
                dof = d_ref[c, :].astype(F32)
                dos[cs, :] = dof
                dls[cs, :] = jnp.broadcast_to(jnp.sum(dof * o_ref[c, :].astype(F32), axis=-1, keepdims=True),
                                              (2 * BLOCK, HEAD_DIM))
        m_cur, m_band = _band_masks()
        m_first = m_band & (n > 0)
        m_last = m_band & (n + 1 < nblk)

        def pair(qrow, krow, d, g, mask):
            q = _sub(qs.at[g], qrow, d).astype(BF16)
            do = _sub(dos, qrow, d).astype(BF16)
            k = _sub(ks, krow, d).astype(BF16)
            vv = _sub(vs, krow, d).astype(BF16)
            p = jnp.where(mask, jnp.exp(_dot_nt(q, k) - _sub(lss, qrow, d)[:, :1]), 0.0)
            ds = p * (_dot_nt(do, vv) - _sub(dls, qrow, d)[:, :1])
            return q, do, k, p.astype(BF16), ds.astype(BF16)

        for g, d in enumerate(DILATIONS):
            nsub = rr // (BLOCK * d)
            for r in range(d):
                _, _, k, _, ds = pair(r, rr + r - BLOCK * d, d, g, m_first)
                dq_acc = _dot(ds, k)
                for j in range(nsub):
                    base = j * BLOCK * d + r
                    q, do, k, p, ds = pair(base, rr + base, d, g, m_cur)
                    _sub_set(dqs.at[g], base, d, dq_acc + _dot(ds, k))
                    nxt = base + BLOCK * d
                    qn, don, _, pn, dsn = pair(nxt, rr + base, d, g, m_band if j + 1 < nsub else m_last)
                    dq_acc = _dot(dsn, k)
                    dk = _dot_tn(ds, q) + _dot_tn(dsn, qn)
                    dv = _dot_tn(p, do) + _dot_tn(pn, don)
                    if g == 0:
                        _sub_set(dks, base, d, dk)
                        _sub_set(dvs, base, d, dv)
                    else:
                        _sub_set(dks, base, d, _sub(dks, base, d) + dk)
                        _sub_set(dvs, base, d, _sub(dvs, base, d) + dv)
        for g in range(N_GROUPS):
            dq_ref[g] = dqs[g].astype(BF16)
        dk_ref[...] = dks[...]
        dv_ref[...] = dvs[...]

    blk = lambda f: pl.BlockSpec((rr, HEAD_DIM), f)
    prv = lambda n: jnp.maximum(n - 1, 0)
    nxt = lambda n: jnp.minimum(n + 1, nblk - 1)
    cur_kv = blk(lambda h, n: (n, h))
    dq, dk, dv = _pcall(
        body, name=name, grid=(ATT_HEADS, nblk),
        in_specs=[blk(lambda h, n, g=g: (n, g * ATT_HEADS + h)) for g in range(N_GROUPS)]
        + [blk(lambda h, n, g=g: (nxt(n), g * ATT_HEADS + h)) for g in range(N_GROUPS)]
        + [blk(lambda h, n: (prv(n), h)), cur_kv] * 2
        + [cur_kv, blk(lambda h, n: (nxt(n), h))] * 3,
        out_specs=[pl.BlockSpec((N_GROUPS, rr, HEAD_DIM), lambda h, n: (0, n, h)), cur_kv, cur_kv],
        out_shape=[SDS((N_GROUPS, s, KV_W), BF16), SDS((s, KV_W), F32), SDS((s, KV_W), F32)],
        scratch_shapes=[pltpu.VMEM((N_GROUPS, 2 * rr, HEAD_DIM), F32)] + [pltpu.VMEM((2 * rr, HEAD_DIM), F32)] * 5
        + [pltpu.VMEM((N_GROUPS, rr, HEAD_DIM), F32), pltpu.VMEM((rr, HEAD_DIM), F32), pltpu.VMEM((rr, HEAD_DIM), F32)],
        compiler_params=_cp())(q_rot, q_rot, q_rot, q_rot, q_rot, q_rot, k_rot, k_rot, v, v, dmix, dmix, mix, mix,
                               lse, lse)
    return dq, dk, dv


def _pool_d(u_ref, halo_ref, i, tm):
    halo = jnp.where(i > 0, halo_ref[...], 0.0)
    t = i * tm + lax.broadcasted_iota(jnp.int32, (tm, 1), 0)
    out = []
    for g, w in enumerate(POOL_WINDOWS):
        sl = slice(g * HEAD_DIM, (g + 1) * HEAD_DIM)
        u = u_ref[:, sl]
        acc = jnp.concatenate([halo[:, sl], u], axis=0)
        sh = 1
        while sh < w:
            acc = acc + pltpu.roll(acc, sh, 0)
            sh *= 2
        cnt = jnp.minimum(t + 1, w).astype(F32)
        out.append(acc[HALO:, :] / cnt - u)
    return out


def _pool_fwd(proj, mix, pool_w, pool_scale, *, name):
    s = proj.shape[0]
    tm = _rows(s)
    ucol = (IN_W - POOL_W) // POOL_W

    def body(u_ref, halo_ref, mix_in, w_ref, sc_ref, o_ref):
        del mix_in
        dd = _pool_d(u_ref, halo_ref, pl.program_id(0), tm)
        for g in range(len(POOL_WINDOWS)):
            sl = slice(g * HEAD_DIM, (g + 1) * HEAD_DIM)
            y = _dot(dd[g].astype(BF16), w_ref[g].astype(BF16))
            o_ref[:, sl] = (y * sc_ref[:, sl]).astype(BF16)

    return _pcall(
        body, name=name, grid=(s // tm,),
        in_specs=[pl.BlockSpec((tm, POOL_W), lambda i: (i, ucol)),
                  pl.BlockSpec((HALO, POOL_W), lambda i: (jnp.maximum(i * (tm // HALO) - 1, 0), ucol)),
                  pl.BlockSpec(memory_space=pl.ANY),
                  pl.BlockSpec((len(POOL_WINDOWS), HEAD_DIM, HEAD_DIM), lambda i: (0, 0, 0)),
                  pl.BlockSpec((1, POOL_W), lambda i: (0, 0))],
        out_specs=pl.BlockSpec((tm, POOL_W), lambda i: (i, 1)),
        out_shape=SDS(mix.shape, BF16), input_output_aliases={2: 0},
        compiler_params=_cp())(proj, proj, mix, pool_w, pool_scale)


def _pool_bwd(proj, dmix, pool_w, pool_scale, *, name):
    s = proj.shape[0]
    tm = _rows(s)
    nblk = s // tm
    ucol = (IN_W - POOL_W) // POOL_W
    ng = len(POOL_WINDOWS)

    def body(u_ref, halo_ref, dp_ref, dpn_ref, w_ref, sc_ref, du_ref, dw_ref, dsc_ref):
        i = pl.program_id(0)

        @pl.when(i == 0)
        def _():
            dw_ref[...] = jnp.zeros_like(dw_ref)
            dsc_ref[...] = jnp.zeros_like(dsc_ref)

        dd = _pool_d(u_ref, halo_ref, i, tm)
        t = i * tm + lax.broadcasted_iota(jnp.int32, (tm, 1), 0)
        dpn = jnp.where(i + 1 < nblk, dpn_ref[...].astype(F32), 0.0)
        for g, w in enumerate(POOL_WINDOWS):
            sl = slice(g * HEAD_DIM, (g + 1) * HEAD_DIM)
            wg = w_ref[g].astype(BF16)
            db = dd[g].astype(BF16)
            dp = dp_ref[:, sl].astype(F32)
            dsc_ref[:, sl] += jnp.sum(dp * _dot(db, wg), axis=0, keepdims=True)
            dy = (dp * sc_ref[:, sl]).astype(BF16)
            dw_ref[g] += _dot_tn(db, dy)
            g_d = _dot_nt(dy, wg)
            g_dn = _dot_nt((dpn[:, sl] * sc_ref[:, sl]).astype(BF16), wg)
            cnt = jnp.minimum(t + 1, w).astype(F32)
            acc = jnp.concatenate([g_d / cnt, g_dn * (1.0 / w)], axis=0)
            sh = 1
            while sh < w:
                acc = acc + pltpu.roll(acc, tm + HALO - sh, 0)
                sh *= 2
            du_ref[:, sl] = acc[:tm, :] - g_d

    nh = s // HALO
    return _pcall(
        body, name=name, grid=(nblk,),
        in_specs=[pl.BlockSpec((tm, POOL_W), lambda i: (i, ucol)),
                  pl.BlockSpec((HALO, POOL_W), lambda i: (jnp.maximum(i * (tm // HALO) - 1, 0), ucol)),
                  pl.BlockSpec((tm, POOL_W), lambda i: (i, 1)),
                  pl.BlockSpec((HALO, POOL_W), lambda i: (jnp.minimum((i + 1) * (tm // HALO), nh - 1), 1)),
                  pl.BlockSpec((ng, HEAD_DIM, HEAD_DIM), lambda i: (0, 0, 0)),
                  pl.BlockSpec((1, POOL_W), lambda i: (0, 0))],
        out_specs=[pl.BlockSpec((tm, POOL_W), lambda i: (i, 0)),
                   pl.BlockSpec((ng, HEAD_DIM, HEAD_DIM), lambda i: (0, 0, 0)),
                   pl.BlockSpec((1, POOL_W), lambda i: (0, 0))],
        out_shape=[SDS((s, POOL_W), F32), SDS((ng, HEAD_DIM, HEAD_DIM), F32), SDS((1, POOL_W), F32)],
        compiler_params=_cp())(proj, proj, dmix, dmix, pool_w, pool_scale)


def _mem_fwd(mem, mem_g, wckv, ck_g, *, name):
    def body(m_ref, g_ref, w_ref, kg_ref, mn_ref, ckr_ref, ckn_ref, cv_ref):
        mv = m_ref[...]
        mn = (mv * _rms_r(mv) * g_ref[...]).astype(BF16)
        mn_ref[...] = mn
        ckv = _dot(mn, w_ref[...])
        ckr_ref[...] = ckv[:, :X_W]
        cv_ref[...] = ckv[:, X_W:].astype(BF16)
        for h in range(ATT_HEADS):
            sl = slice(h * HEAD_DIM, (h + 1) * HEAD_DIM)
            y = ckv[:, sl]
            ckn_ref[:, sl] = (y * _rms_r(y) * kg_ref[...]).astype(BF16)

    return _pcall(
        body, name=name,
        out_shape=[SDS((N_MEM, D_MODEL), BF16), SDS((N_MEM, X_W), F32), SDS((N_MEM, X_W), BF16),
                   SDS((N_MEM, X_W), BF16)],
        compiler_params=_cp())(mem, mem_g, wckv, ck_g)


def _cross_q(cq_ref, g_ref, sl):
    y = cq_ref[:, sl]
    r = _rms_r(y)
    return y, r, y * r * g_ref[...] * SCALE


def _cross_fwd(cq_raw, ck_n, cv, cq_g, *, name):
    s = cq_raw.shape[0]
    tm = _rows(s)

    def body(cq_ref, k_ref, v_ref, g_ref, o_ref):
        for h in range(ATT_HEADS):
            sl = slice(h * HEAD_DIM, (h + 1) * HEAD_DIM)
            _, _, qn = _cross_q(cq_ref, g_ref, sl)
            sc = _dot_nt(qn.astype(BF16), k_ref[:, sl])
            p = jnp.exp(sc - jnp.max(sc, axis=-1, keepdims=True))
            p = p / jnp.sum(p, axis=-1, keepdims=True)
            o_ref[:, sl] = _dot(p.astype(BF16), v_ref[:, sl]).astype(BF16)

    full = lambda a: pl.BlockSpec(a.shape, lambda i: (0, 0))
    return _pcall(
        body, name=name, grid=(s // tm,),
        in_specs=[pl.BlockSpec((tm, X_W), lambda i: (i, 0)), full(ck_n), full(cv), full(cq_g)],
        out_specs=pl.BlockSpec((tm, X_W), lambda i: (i, 0)),
        out_shape=SDS((s, X_W), BF16), compiler_params=_cp())(cq_raw, ck_n, cv, cq_g)


def _cross_bwd(d_o, cq_raw, ck_n, cv, cq_g, *, name):
    s = cq_raw.shape[0]
    tm = _rows(s)

    def body(do_ref, cq_ref, k_ref, v_ref, g_ref, dcq_ref, dk_ref, dv_ref, dg_ref):
        @pl.when(pl.program_id(0) == 0)
        def _():
            dk_ref[...] = jnp.zeros_like(dk_ref)
            dv_ref[...] = jnp.zeros_like(dv_ref)
            dg_ref[...] = jnp.zeros_like(dg_ref)

        dg = jnp.zeros((1, HEAD_DIM), F32)
        for h in range(ATT_HEADS):
            sl = slice(h * HEAD_DIM, (h + 1) * HEAD_DIM)
            y, r, qn = _cross_q(cq_ref, g_ref, sl)
            qb = qn.astype(BF16)
            do = do_ref[:, sl]
            sc = _dot_nt(qb, k_ref[:, sl])
            p = jnp.exp(sc - jnp.max(sc, axis=-1, keepdims=True))
            p = p / jnp.sum(p, axis=-1, keepdims=True)
            dv_ref[:, sl] += _dot_tn(p.astype(BF16), do)
            dp = _dot_nt(do, v_ref[:, sl])
            ds = (p * (dp - jnp.sum(dp * p, axis=-1, keepdims=True))).astype(BF16)
            dk_ref[:, sl] += _dot_tn(ds, qb)
            dn = _dot(ds, k_ref[:, sl]) * SCALE
            dy, dgain = _norm_bwd(y, r, g_ref[...], dn)
            dcq_ref[:, sl] = dy.astype(BF16)
            dg = dg + dgain
        dg_ref[...] += dg

    full = lambda a: pl.BlockSpec(a.shape, lambda i: (0, 0))
    row = pl.BlockSpec((tm, X_W), lambda i: (i, 0))
    return _pcall(
        body, name=name, grid=(s // tm,),
        in_specs=[row, row, full(ck_n), full(cv), full(cq_g)],
        out_specs=[row, pl.BlockSpec((N_MEM, X_W), lambda i: (0, 0)), pl.BlockSpec((N_MEM, X_W), lambda i: (0, 0)),
                   pl.BlockSpec((1, HEAD_DIM), lambda i: (0, 0))],
        out_shape=[SDS((s, X_W), BF16), SDS((N_MEM, X_W), F32), SDS((N_MEM, X_W), F32), SDS((1, HEAD_DIM), F32)],
        compiler_params=_cp())(d_o, cq_raw, ck_n, cv, cq_g)


def _mem_bwd(dck_n, dcv, ck_raw, memn, mem, wckv, mem_g, ck_g, *, name):
    def body(dk_ref, dv_ref, ckr_ref, mn_ref, m_ref, w_ref, mg_ref, kg_ref, dw_ref, dmg_ref, dkg_ref, dckv_s):
        dkg = jnp.zeros((1, HEAD_DIM), F32)
        for h in range(ATT_HEADS):
            sl = slice(h * HEAD_DIM, (h + 1) * HEAD_DIM)
            y = ckr_ref[:, sl]
            dy, dgain = _norm_bwd(y, _rms_r(y), kg_ref[...], dk_ref[:, sl])
            dckv_s[:, sl] = dy.astype(BF16)
            dkg = dkg + dgain
        dkg_ref[...] = dkg
        dckv_s[:, X_W:] = dv_ref[...].astype(BF16)
        dckv = dckv_s[...]
        dw_ref[...] = _dot_tn(mn_ref[...], dckv).astype(BF16)
        dmn = _dot_nt(dckv, w_ref[...])
        mv = m_ref[...]
        dmg_ref[...] = jnp.sum(dmn * mv * _rms_r(mv), axis=0, keepdims=True)

    return _pcall(
        body, name=name,
        out_shape=[SDS((D_MODEL, 2 * X_W), BF16), SDS((1, D_MODEL), F32), SDS((1, HEAD_DIM), F32)],
        scratch_shapes=[pltpu.VMEM((N_MEM, 2 * X_W), BF16)],
        compiler_params=_cp())(dck_n, dcv, ck_raw, memn, mem, wckv, mem_g, ck_g)


def _rope_tables(pos):
    inv_freq = ROPE_THETA ** (-jnp.arange(0, ROT_DIM, 2, dtype=F32) / ROT_DIM)
    ang = pos.astype(F32)[:, None] * inv_freq
    cos, sin = jnp.cos(ang), jnp.sin(ang)
    s = pos.shape[0]
    rest = HEAD_DIM - ROT_DIM
    cos_t = jnp.concatenate([cos, cos, jnp.ones((s, rest), F32)], axis=1)
    sin_t = jnp.concatenate([-sin, sin, jnp.zeros((s, rest), F32)], axis=1)
    return cos_t, sin_t


def _local_step(x, mem, pos, tgt, wb, sm):
    cos_t, sin_t = _rope_tables(pos)

    proj, xn = _norm_linear(x, sm["mix_norm_g"], wb["w_in"], cn=768, out_dtype=F32, name="fwd_in_proj")
    q_rot, k_rot, v = _qk_prep(proj, cos_t, sin_t, sm["q_norm_g"], sm["k_norm_g"], name="fwd_qk_prep")
    mix, lse_all = _attn_fwd(q_rot, k_rot, v, name="fwd_attn")
    mix = _pool_fwd(proj, mix, sm["pool_w"], sm["pool_scale"], name="fwd_pool")
    h1 = _linear_res(mix, wb["w_out"], x, cn=512, name="fwd_out_proj")
    cq_raw, hn = _norm_linear(h1, sm["cross_norm_g"], wb["w_cq"], cn=512, out_dtype=F32, name="fwd_cq_proj")
    memn, ck_raw, ck_n, cv = _mem_fwd(mem, sm["mem_norm_g"], wb["w_ckv"], sm["ck_norm_g"], name="fwd_mem")
    co = _cross_fwd(cq_raw, ck_n, cv, sm["cq_norm_g"], name="fwd_cross")
    h2 = _linear_res(co, wb["w_co"], h1, cn=512, name="fwd_co_proj")
    gu, act, fn = _norm_linear_swiglu(h2, sm["ffn_norm_g"], wb["w_gate_up"], name="fwd_gate_up")
    dy, dyb, sq = _linear_res_loss(act, wb["w_down"], h2, tgt, name="fwd_down_loss")

    gw = {}
    gs = {}
    dgu = _swiglu_bwd(dyb, wb["w_down"], gu, name="bwd_swiglu")
    gw["w_down"] = _dw_tn(act, dyb, tkw=1024, tn=1024, name="bwd_dw_down")
    dh2, dh2b, gs["ffn_norm_g"] = _linear_nt_normbwd(dgu, wb["w_gate_up"], h2, dy, sm["ffn_norm_g"],
                                                     name="bwd_ffn_in")
    gw["w_gate_up"] = _dw_tn(fn, dgu, tkw=1024, tn=1536, name="bwd_dw_gate_up")

    d_co = _linear_nt(dh2b, wb["w_co"], cn=512, name="bwd_co_proj")
    gw["w_co"] = _dw_tn(co, dh2b, tkw=512, tn=1024, name="bwd_dw_co")
    dcq, dck_n, dcv, gs["cq_norm_g"] = _cross_bwd(d_co, cq_raw, ck_n, cv, sm["cq_norm_g"], name="bwd_cross")
    gw["w_ckv"], gs["mem_norm_g"], gs["ck_norm_g"] = _mem_bwd(dck_n, dcv, ck_raw, memn, mem, wb["w_ckv"],
                                                             sm["mem_norm_g"], sm["ck_norm_g"], name="bwd_mem")
    dh1, dh1b, gs["cross_norm_g"] = _linear_nt_normbwd(dcq, wb["w_cq"], h1, dh2, sm["cross_norm_g"],
                                                       name="bwd_cq_in")
    gw["w_cq"] = _dw_tn(hn, dcq, tkw=1024, tn=512, name="bwd_dw_cq")

    dmix = _linear_nt(dh1b, wb["w_out"], cn=512, name="bwd_out_proj")
    gw["w_out"] = _dw_tn(mix, dh1b, tkw=1024, tn=1024, name="bwd_dw_out")
    du, gs["pool_w"], gs["pool_scale"] = _pool_bwd(proj, dmix, sm["pool_w"], sm["pool_scale"], name="bwd_pool")
    dq, dk, dv = _attn_bwd(q_rot, k_rot, v, mix, dmix, lse_all, name="bwd_attn")
    dproj, gs["q_norm_g"], gs["k_norm_g"] = _qk_prep_bwd(proj, dq, dk, dv, du, cos_t, sin_t, sm["q_norm_g"],
                                                         sm["k_norm_g"], name="bwd_qk_prep")
    dx, _, gs["mix_norm_g"] = _linear_nt_normbwd(dproj, wb["w_in"], x, dh1, sm["mix_norm_g"],
                                                 name="bwd_in_proj")
    gw["w_in"] = _dw_tn(xn, dproj, tkw=1024, tn=1536, name="bwd_dw_in")
    return sq, dx, gw, gs


SHARDED = ("w_in", "w_out", "w_cq", "w_ckv", "w_co", "w_gate_up", "w_down")
SMALL = ("mix_norm_g", "q_norm_g", "k_norm_g", "pool_w", "pool_scale", "cross_norm_g", "mem_norm_g", "cq_norm_g",
         "ck_norm_g", "ffn_norm_g")
FULL = {
    "w_in": ((D_MODEL, IN_W), 1, IN_W // N_DEV),
    "w_out": ((D_MODEL, D_MODEL), 0, D_MODEL // N_DEV),
    "w_cq": ((D_MODEL, X_W), 0, D_MODEL // N_DEV),
    "w_ckv": ((D_MODEL, 2 * X_W), 0, D_MODEL // N_DEV),
    "w_co": ((X_W, D_MODEL), 1, D_MODEL // N_DEV),
    "w_gate_up": ((D_MODEL, 2 * FF_PAD), 1, FF_TILE),
    "w_down": ((FF_PAD, D_MODEL), 0, DOWN_SHARD),
}


def _shard_shape(name):
    shape, axis, width = FULL[name]
    return tuple(width if a == axis else n for a, n in enumerate(shape))


def _window(ref, name, dev):
    _, axis, width = FULL[name]
    if name == "w_down":
        start = pl.multiple_of((dev // 2) * FF_TILE + (dev % 2) * DOWN_SHARD, HALO)
    else:
        start = pl.multiple_of(dev * width, BLOCK)
    return ref.at[pl.ds(start, width), :] if axis == 0 else ref.at[:, pl.ds(start, width)]


def _mesh_place():
    x, y, c = lax.axis_index("x"), lax.axis_index("y"), lax.axis_index("c")
    return x, y, c, 4 * x + 2 * y + c


def _peer(x, y, c, k):
    px = 1 - x if k & 4 else x
    py = 1 - y if k & 2 else y
    pc = 1 - c if k & 1 else c
    return (px, py, pc), 4 * px + 2 * py + pc


HBM_SPEC = pl.BlockSpec(memory_space=pltpu.HBM)


def _gather_weights(shards):
    names = SHARDED
    nw = len(names)
    n_pad = FF_PAD // FF_TILE

    def body(*refs):
        ins, outs = refs[:nw], refs[nw:2 * nw]
        zero_ref, send_sems, recv_sems, local_sems, zero_sems = refs[2 * nw:]
        x, y, c, me = _mesh_place()
        zero_ref[...] = jnp.zeros_like(zero_ref)
        down = outs[names.index("w_down")]
        pads = [pltpu.make_async_copy(zero_ref, down.at[pl.ds(t * FF_TILE + FF_SHARD, FF_TILE - FF_SHARD), :],
                                      zero_sems.at[t]) for t in range(n_pad)]
        for cp in pads:
            cp.start()
        local, sent = [], []
        for wi, name in enumerate(names):
            cp = pltpu.make_async_copy(ins[wi], _window(outs[wi], name, me), local_sems.at[wi])
            cp.start()
            local.append(cp)
            for k in range(1, N_DEV):
                peer, _ = _peer(x, y, c, k)
                cp = pltpu.make_async_remote_copy(
                    src_ref=ins[wi], dst_ref=_window(outs[wi], name, me), send_sem=send_sems.at[wi, k - 1],
                    recv_sem=recv_sems.at[wi, k - 1], device_id=peer, device_id_type=MESH)
                cp.start()
                sent.append(cp)
        for wi, name in enumerate(names):
            for k in range(1, N_DEV):
                peer, pidx = _peer(x, y, c, k)
                pltpu.make_async_remote_copy(
                    src_ref=ins[wi], dst_ref=_window(outs[wi], name, pidx), send_sem=send_sems.at[wi, k - 1],
                    recv_sem=recv_sems.at[wi, k - 1], device_id=peer, device_id_type=MESH).wait_recv()
        for cp in sent:
            cp.wait_send()
        for cp in local + pads:
            cp.wait()

    outs = _pcall(
        body, name="gather_weights",
        in_specs=[HBM_SPEC] * nw, out_specs=[HBM_SPEC] * nw,
        out_shape=[SDS(FULL[n][0], BF16) for n in names],
        scratch_shapes=[pltpu.VMEM((FF_TILE - FF_SHARD, D_MODEL), BF16), pltpu.SemaphoreType.DMA((nw, N_DEV - 1)),
                        pltpu.SemaphoreType.DMA((nw, N_DEV - 1)), pltpu.SemaphoreType.DMA((nw,)),
                        pltpu.SemaphoreType.DMA((n_pad,))],
        compiler_params=pltpu.CompilerParams(has_side_effects=True))(*[shards[n] for n in names])
    return dict(zip(names, outs))


def _exchange_grads(gw, small):
    names = SHARDED
    nw = len(names)

    def body(*refs):
        ins, small_ref = refs[:nw], refs[nw]
        outs, small_out = refs[nw + 1:2 * nw + 1], refs[2 * nw + 1]
        send_sems, recv_sems, local_sems = refs[2 * nw + 2:]
        x, y, c, me = _mesh_place()
        local, sent = [], []
        for wi, name in enumerate(names):
            cp = pltpu.make_async_copy(_window(ins[wi], name, me), outs[wi].at[0], local_sems.at[wi])
            cp.start()
            local.append(cp)
        cp = pltpu.make_async_copy(small_ref, small_out.at[me], local_sems.at[nw])
        cp.start()
        local.append(cp)
        for k in range(1, N_DEV):
            peer, pidx = _peer(x, y, c, k)
            for wi, name in enumerate(names):
                cp = pltpu.make_async_remote_copy(
                    src_ref=_window(ins[wi], name, pidx), dst_ref=outs[wi].at[k], send_sem=send_sems.at[wi, k - 1],
                    recv_sem=recv_sems.at[wi, k - 1], device_id=peer, device_id_type=MESH)
                cp.start()
                sent.append(cp)
            cp = pltpu.make_async_remote_copy(
                src_ref=small_ref, dst_ref=small_out.at[me], send_sem=send_sems.at[nw, k - 1],
                recv_sem=recv_sems.at[nw, k - 1], device_id=peer, device_id_type=MESH)
            cp.start()
            sent.append(cp)
        for k in range(1, N_DEV):
            peer, pidx = _peer(x, y, c, k)
            for wi, name in enumerate(names):
                pltpu.make_async_remote_copy(
                    src_ref=_window(ins[wi], name, me), dst_ref=outs[wi].at[k], send_sem=send_sems.at[wi, k - 1],
                    recv_sem=recv_sems.at[wi, k - 1], device_id=peer, device_id_type=MESH).wait_recv()
            pltpu.make_async_remote_copy(
                src_ref=small_ref, dst_ref=small_out.at[pidx], send_sem=send_sems.at[nw, k - 1],
                recv_sem=recv_sems.at[nw, k - 1], device_id=peer, device_id_type=MESH).wait_recv()
        for cp in sent:
            cp.wait_send()
        for cp in local:
            cp.wait()

    outs = _pcall(
        body, name="exchange_grads",
        in_specs=[HBM_SPEC] * (nw + 1), out_specs=[HBM_SPEC] * (nw + 1),
        out_shape=[SDS((N_DEV,) + _shard_shape(n), BF16) for n in names] + [SDS((N_DEV,) + small.shape, F32)],
        scratch_shapes=[pltpu.SemaphoreType.DMA((nw + 1, N_DEV - 1)), pltpu.SemaphoreType.DMA((nw + 1, N_DEV - 1)),
                        pltpu.SemaphoreType.DMA((nw + 1,))],
        compiler_params=pltpu.CompilerParams(has_side_effects=True))(*[gw[n] for n in names], small)
    return dict(zip(names, outs[:nw])), outs[nw]


def _adamw(parts, w, m, v, *, name):
    r, c = w.shape
    tr = r
    for cand in (256, 128, 88):
        if r % cand == 0:
            tr = cand
            break

    def body(p_ref, w_ref, m_ref, v_ref, g_ref, d_ref, mo_ref, vo_ref):
        g = p_ref[0].astype(F32)
        for k in range(1, N_DEV):
            g = g + p_ref[k].astype(F32)
        m_new = ADAM_B1 * m_ref[...] + (1.0 - ADAM_B1) * g
        v_new = ADAM_B2 * v_ref[...] + (1.0 - ADAM_B2) * (g * g)
        m_hat = m_new / (1.0 - ADAM_B1 ** ADAM_STEP)
        v_hat = v_new / (1.0 - ADAM_B2 ** ADAM_STEP)
        g_ref[...] = g
        d_ref[...] = -ADAM_LR * (m_hat / (jnp.sqrt(v_hat) + ADAM_EPS) + ADAM_WD * w_ref[...])
        mo_ref[...] = m_new
        vo_ref[...] = v_new

    row = pl.BlockSpec((tr, c), lambda i: (i, 0))
    return _pcall(
        body, name=name, grid=(r // tr,),
        in_specs=[pl.BlockSpec((N_DEV, tr, c), lambda i: (0, i, 0)), row, row, row],
        out_specs=[row] * 4, out_shape=[SDS((r, c), F32)] * 4, compiler_params=_cp())(parts, w, m, v)


def _pack_small(d):
    parts = []
    for n in SMALL:
        a = d[n].reshape(-1, HEAD_DIM)
        parts.append(jnp.pad(a, ((0, -a.shape[0] % 8), (0, 0))))
    return jnp.concatenate(parts, axis=0)


def _unpack_small(packed, like):
    out, row = {}, 0
    for n in SMALL:
        rows = like[n].size // HEAD_DIM
        out[n] = packed[row:row + rows].reshape(like[n].shape)
        row += rows + (-rows % 8)
    return out


def _pad_cols(a, width):
    return jnp.pad(a, ((0, 0), (0, width - a.shape[1])))


def kernel(x, mem, positions, mix_norm_g, w_in, q_norm_g, k_norm_g, pool_w, pool_scale, w_out, cross_norm_g, mem_norm_g, w_cq, w_ckv, cq_norm_g, ck_norm_g, w_co, ffn_norm_g, w_gate_up, w_down, loss_target, m_mix_norm_g, m_w_in, m_q_norm_g, m_k_norm_g, m_pool_w, m_pool_scale, m_w_out, m_cross_norm_g, m_mem_norm_g, m_w_cq, m_w_ckv, m_cq_norm_g, m_ck_norm_g, m_w_co, m_ffn_norm_g, m_w_gate_up, m_w_down, v_mix_norm_g, v_w_in, v_q_norm_g, v_k_norm_g, v_pool_w, v_pool_scale, v_w_out, v_cross_norm_g, v_mem_norm_g, v_w_cq, v_w_ckv, v_cq_norm_g, v_ck_norm_g, v_w_co, v_ffn_norm_g, v_w_gate_up, v_w_down):
    given = dict(locals())
    w_f32 = {n: given[n][0] for n in SHARDED + SMALL}
    m_f32 = {n: given["m_" + n][0] for n in SHARDED + SMALL}
    v_f32 = {n: given["v_" + n][0] for n in SHARDED + SMALL}
    for d in (w_f32, m_f32, v_f32):
        d["w_gate_up"] = _pad_cols(d["w_gate_up"], FF_TILE)

    wb = _gather_weights({n: w_f32[n].astype(BF16) for n in SHARDED})
    sm = {n: w_f32[n] for n in SMALL}
    sm_rows = {n: (a if a.ndim == 3 else a.reshape(1, -1)) for n, a in sm.items()}
    sq, dx, gw, gs = _local_step(x[0], mem[0], positions[0], loss_target[0], wb, sm_rows)
    loss = lax.psum(0.5 / D_MODEL * jnp.sum(sq), ("x", "y", "c"))

    parts, small_parts = _exchange_grads(gw, _pack_small(gs))
    res = {}
    for n in SHARDED:
        res[n] = _adamw(parts[n], w_f32[n], m_f32[n], v_f32[n], name="adamw_" + n)
    res["w_gate_up"] = [a[:, :FF_SHARD] for a in res["w_gate_up"]]
    small_res = _adamw(small_parts, _pack_small(sm), _pack_small({n: m_f32[n] for n in SMALL}),
                       _pack_small({n: v_f32[n] for n in SMALL}), name="adamw_small")
    small_res = [_unpack_small(a, sm) for a in small_res]
    order = ("mix_norm_g", "w_in", "q_norm_g", "k_norm_g", "pool_w", "pool_scale", "w_out", "cross_norm_g",
             "mem_norm_g", "w_cq", "w_ckv", "cq_norm_g", "ck_norm_g", "w_co", "ffn_norm_g", "w_gate_up", "w_down")
    outs = [loss, dx[None]]
    for which in range(4):
        for n in order:
            a = res[n][which] if n in SHARDED else small_res[which][n]
            outs.append(a[None])
    return tuple(outs)
```

```python
import functools

import jax
import jax.numpy as jnp
from jax import lax
from jax.experimental import pallas as pl
from jax.experimental.pallas import tpu as pltpu

F32 = jnp.float32
BF16 = jnp.bfloat16
SDS = jax.ShapeDtypeStruct

D_MODEL = 1024
HEAD_DIM = 128
N_GROUPS = 3
DILATIONS = (1, 4, 16)
ATT_HEADS = 4
Q_W = 1536
KV_W = 512
POOL_W = 512
POOL_WINDOWS = (2, 4, 8, 16)
IN_W = 3072
X_W = 512
N_MEM = 256
D_FF = 2816
FF_TILE = 768
FF_SHARD = 704
FF_PAD = 4 * FF_TILE
DOWN_SHARD = 352
ROT_DIM = 32
ROT_HALF = 16
ROPE_THETA = 500000.0
EPS = 1e-6
NEG_INF = -1e30
SCALE = HEAD_DIM ** -0.5
BLOCK = 128
HALO = 16

ADAM_LR = 0.001
ADAM_B1 = 0.9
ADAM_B2 = 0.999
ADAM_EPS = 1e-08
ADAM_WD = 0.01
ADAM_STEP = 10

N_DEV = 8
VMEM_LIMIT_BYTES = 56 * 1024 * 1024
MESH = pl.DeviceIdType.MESH


def _pcall(body, **kw):
    return pl.pallas_call(body, **kw)


def _cp():
    return pltpu.CompilerParams(vmem_limit_bytes=VMEM_LIMIT_BYTES)


def _dot(a, b):
    return lax.dot_general(a, b, (((1,), (0,)), ((), ())), preferred_element_type=F32)


def _dot_nt(a, b):
    return lax.dot_general(a, b, (((1,), (1,)), ((), ())), preferred_element_type=F32)


def _dot_tn(a, b):
    return lax.dot_general(a, b, (((0,), (0,)), ((), ())), preferred_element_type=F32)


def _rows(s):
    return min(512, s)


def _rms_r(x):
    return lax.rsqrt(jnp.mean(x * x, axis=-1, keepdims=True) + EPS)


def _norm_bwd(x, r, gain, dxn):
    z = dxn * gain
    dx = r * z - x * (r * r * r * jnp.mean(z * x, axis=-1, keepdims=True))
    dgain = jnp.sum(dxn * x * r, axis=0, keepdims=True)
    return dx, dgain


def _partner(t):
    lane = lax.broadcasted_iota(jnp.int32, t.shape, 1)
    return jnp.where(lane < ROT_HALF, pltpu.roll(t, HEAD_DIM - ROT_HALF, 1), pltpu.roll(t, ROT_HALF, 1))


def _rope(n, cos_t, sin_t):
    return n * cos_t + _partner(n) * sin_t


def _rope_bwd(d, cos_t, sin_t):
    lane = lax.broadcasted_iota(jnp.int32, d.shape, 1)
    return d * cos_t + jnp.where(lane < ROT_DIM, _partner(d * sin_t), 0.0)


def _resident(shape):
    return pl.BlockSpec(shape, lambda i: (0,) * len(shape), pipeline_mode=pl.Buffered(1))


def _chunks(n, cn):
    return [slice(j * cn, (j + 1) * cn) for j in range(n // cn)]


def _norm_linear(x, gain, w, *, cn, out_dtype, name):
    s, k = x.shape
    n = w.shape[1]
    tm = _rows(s)

    def body(x_ref, g_ref, w_ref, y_ref, xn_ref):
        xv = x_ref[...]
        xn_ref[...] = (xv * _rms_r(xv) * g_ref[...]).astype(BF16)
        for c in _chunks(n, cn):
            y_ref[:, c] = _dot(xn_ref[...], w_ref[:, c]).astype(out_dtype)

    row = lambda w_: pl.BlockSpec((tm, w_), lambda i: (i, 0))
    return _pcall(
        body, name=name, grid=(s // tm,),
        in_specs=[row(k), _resident((1, k)), _resident((k, n))],
        out_specs=[row(n), row(k)],
        out_shape=[SDS((s, n), out_dtype), SDS((s, k), BF16)], compiler_params=_cp())(x, gain, w)


def _norm_linear_swiglu(x, gain, wgu, *, name):
    s, k = x.shape
    tm = _rows(s)

    def body(x_ref, g_ref, w_ref, gu_ref, a_ref, xn_ref):
        xv = x_ref[...]
        xn_ref[...] = (xv * _rms_r(xv) * g_ref[...]).astype(BF16)
        for c in _chunks(FF_PAD, FF_TILE):
            g = _dot(xn_ref[...], w_ref[:, c])
            u = _dot(xn_ref[...], w_ref[:, slice(FF_PAD + c.start, FF_PAD + c.stop)])
            a_ref[:, c] = (g * jax.nn.sigmoid(g) * u).astype(BF16)
            gu_ref[0, :, c] = g.astype(BF16)
            gu_ref[1, :, c] = u.astype(BF16)

    row = lambda w_: pl.BlockSpec((tm, w_), lambda i: (i, 0))
    return _pcall(
        body, name=name, grid=(s // tm,),
        in_specs=[row(k), _resident((1, k)), _resident((k, 2 * FF_PAD))],
        out_specs=[pl.BlockSpec((2, tm, FF_PAD), lambda i: (0, i, 0)), row(FF_PAD), row(k)],
        out_shape=[SDS((2, s, FF_PAD), BF16), SDS((s, FF_PAD), BF16), SDS((s, k), BF16)],
        compiler_params=_cp())(x, gain, wgu)


def _linear_res(a, w, res, *, cn, name):
    s, k = a.shape
    n = w.shape[1]
    tm = _rows(s)

    def body(a_ref, w_ref, r_ref, y_ref):
        for c in _chunks(n, cn):
            y_ref[:, c] = r_ref[:, c] + _dot(a_ref[...], w_ref[:, c])

    row = lambda w_: pl.BlockSpec((tm, w_), lambda i: (i, 0))
    return _pcall(
        body, name=name, grid=(s // tm,),
        in_specs=[row(k), _resident((k, n)), row(n)], out_specs=row(n),
        out_shape=SDS((s, n), F32), compiler_params=_cp())(a, w, res)


def _linear_res_loss(a, w, res, tgt, *, name):
    s, k = a.shape
    n = w.shape[1]
    tm = _rows(s)

    def body(a_ref, w_ref, r_ref, t_ref, dy_ref, dyb_ref, sq_ref):
        e = r_ref[...] + _dot(a_ref[...], w_ref[...]) - t_ref[...]
        dy = e * (1.0 / n)
        dy_ref[...] = dy
        dyb_ref[...] = dy.astype(BF16)

        @pl.when(pl.program_id(0) == 0)
        def _():
            sq_ref[...] = jnp.zeros_like(sq_ref)
        sq_ref[...] += jnp.sum(e * e, axis=0, keepdims=True)

    row = lambda w_: pl.BlockSpec((tm, w_), lambda i: (i, 0))
    return _pcall(
        body, name=name, grid=(s // tm,),
        in_specs=[row(k), _resident((k, n)), row(n), row(n)],
        out_specs=[row(n), row(n), pl.BlockSpec((1, n), lambda i: (0, 0))],
        out_shape=[SDS((s, n), F32), SDS((s, n), BF16), SDS((1, n), F32)],
        compiler_params=_cp())(a, w, res, tgt)


def _linear_nt(g, w, *, cn, name):
    s, k = g.shape
    n = w.shape[0]
    tm = _rows(s)

    def body(g_ref, w_ref, y_ref):
        for c in _chunks(n, cn):
            y_ref[:, c] = _dot_nt(g_ref[...], w_ref[c, :]).astype(BF16)

    row = lambda w_: pl.BlockSpec((tm, w_), lambda i: (i, 0))
    return _pcall(
        body, name=name, grid=(s // tm,),
        in_specs=[row(k), _resident((n, k))], out_specs=row(n),
        out_shape=SDS((s, n), BF16), compiler_params=_cp())(g, w)


def _swiglu_bwd(dyb, wd, gu, *, name):
    s, n = dyb.shape
    tm = _rows(s)

    def body(dy_ref, wd_ref, gu_ref, dgu_ref):
        for c in _chunks(FF_PAD, FF_TILE):
            da = _dot_nt(dy_ref[...], wd_ref[c, :])
            g = gu_ref[0, :, c].astype(F32)
            u = gu_ref[1, :, c].astype(F32)
            sg = jax.nn.sigmoid(g)
            dgu_ref[0, :, c] = (da * u * (sg * (1.0 + g * (1.0 - sg)))).astype(BF16)
            dgu_ref[1, :, c] = (da * (g * sg)).astype(BF16)

    half = pl.BlockSpec((2, tm, FF_PAD), lambda i: (0, i, 0))
    return _pcall(
        body, name=name, grid=(s // tm,),
        in_specs=[pl.BlockSpec((tm, n), lambda i: (i, 0)), _resident((FF_PAD, n)), half],
        out_specs=half, out_shape=SDS((2, s, FF_PAD), BF16), compiler_params=_cp())(dyb, wd, gu)


def _linear_nt_normbwd(g, w, x, dres, gain, *, name, grads=None):
    d, k = w.shape
    s = x.shape[0]
    tm = _rows(s)
    names = tuple(grads or ())
    nr = len(names)

    def body(*refs):
        g_ref, w_ref, x_ref, dr_ref, gn_ref = refs[:5]
        dx_ref, dxb_ref, dg_ref = refs[5 + nr:8 + nr]
        if nr:
            copies = _exchange_copies(names, refs[5:5 + nr], refs[8 + nr:8 + 2 * nr], *refs[8 + 2 * nr:])

        @pl.when(pl.program_id(0) == 0)
        def _():
            dg_ref[...] = jnp.zeros_like(dg_ref)
            if nr:
                _start(copies)

        if g.ndim == 3:
            dxn = _dot_nt(g_ref[0], w_ref[:, :k // 2]) + _dot_nt(g_ref[1], w_ref[:, k // 2:])
        else:
            dxn = _dot_nt(g_ref[...], w_ref[...])
        xv = x_ref[...]
        dx, dgain = _norm_bwd(xv, _rms_r(xv), gn_ref[...], dxn)
        out = dr_ref[...] + dx
        dx_ref[...] = out
        dxb_ref[...] = out.astype(BF16)
        dg_ref[...] += dgain

        if nr:
            @pl.when(pl.program_id(0) == s // tm - 1)
            def _():
                _finish(copies)

    row = pl.BlockSpec((tm, d), lambda i: (i, 0))
    g_spec = (pl.BlockSpec((2, tm, k // 2), lambda i: (0, i, 0)) if g.ndim == 3
              else pl.BlockSpec((tm, k), lambda i: (i, 0)))
    outs = _pcall(
        body, name=name, grid=(s // tm,),
        in_specs=[g_spec, _resident((d, k)), row, row, _resident((1, d))] + [HBM_SPEC] * nr,
        out_specs=[row, row, pl.BlockSpec((1, d), lambda i: (0, 0))] + [HBM_SPEC] * nr,
        out_shape=[SDS((s, d), F32), SDS((s, d), BF16), SDS((1, d), F32)]
        + [SDS((N_DEV,) + _shard_shape(n), BF16) for n in names],
        scratch_shapes=_comm_sems(nr) if nr else [],
        compiler_params=_cp())(g, w, x, dres, gain, *[grads[n] for n in names])
    return (outs[0], outs[1], outs[2], dict(zip(names, outs[3:]))) if nr else tuple(outs)


def _dw_tn(x, g, *, tkw, tn, name):
    s, kw = x.shape
    halves = g.ndim == 3
    n = 2 * g.shape[2] if halves else g.shape[1]
    ts = _rows(s)
    ns = s // ts
    per_half = n // 2 // tn

    def body(x_ref, g_ref, o_ref, acc_ref):
        ss = pl.program_id(2)

        @pl.when(ss == 0)
        def _():
            acc_ref[...] = jnp.zeros_like(acc_ref)

        acc_ref[...] += _dot_tn(x_ref[...], g_ref[...])

        @pl.when(ss == ns - 1)
        def _():
            o_ref[...] = acc_ref[...].astype(BF16)

    g_spec = (pl.BlockSpec((None, ts, tn), lambda a, b, ss: (b // per_half, ss, b % per_half)) if halves
              else pl.BlockSpec((ts, tn), lambda a, b, ss: (ss, b)))
    return _pcall(
        body, name=name, grid=(kw // tkw, n // tn, ns),
        in_specs=[pl.BlockSpec((ts, tkw), lambda a, b, ss: (ss, a)), g_spec],
        out_specs=pl.BlockSpec((tkw, tn), lambda a, b, ss: (a, b)),
        out_shape=SDS((kw, n), BF16),
        scratch_shapes=[pltpu.VMEM((tkw, tn), F32)], compiler_params=_cp())(x, g)


def _qk_prep(proj, cos_t, sin_t, qg, kg, *, name):
    s = proj.shape[0]
    tm = _rows(s)
    nqh = Q_W // HEAD_DIM

    def body(q_ref, k_ref, v_ref, c_ref, s_ref, qg_ref, kg_ref, qo_ref, ko_ref, vo_ref):
        c, sn = c_ref[...], s_ref[...]
        for h in range(nqh):
            sl = slice(h * HEAD_DIM, (h + 1) * HEAD_DIM)
            y = q_ref[:, sl]
            qo_ref[:, sl] = (_rope(y * _rms_r(y) * qg_ref[...], c, sn) * SCALE).astype(BF16)
        for h in range(ATT_HEADS):
            sl = slice(h * HEAD_DIM, (h + 1) * HEAD_DIM)
            y = k_ref[:, sl]
            ko_ref[:, sl] = _rope(y * _rms_r(y) * kg_ref[...], c, sn).astype(BF16)
        vo_ref[...] = v_ref[...].astype(BF16)

    row = lambda w, j: pl.BlockSpec((tm, w), lambda i: (i, j))
    one = pl.BlockSpec((1, HEAD_DIM), lambda i: (0, 0))
    return _pcall(
        body, name=name, grid=(s // tm,),
        in_specs=[row(Q_W, 0), row(KV_W, 3), row(KV_W, 4), row(HEAD_DIM, 0), row(HEAD_DIM, 0), one, one],
        out_specs=[row(Q_W, 0), row(KV_W, 0), row(KV_W, 0)],
        out_shape=[SDS((s, Q_W), BF16), SDS((s, KV_W), BF16), SDS((s, KV_W), BF16)],
        compiler_params=_cp())(proj, proj, proj, cos_t, sin_t, qg, kg)


def _qk_prep_bwd(proj, dq, dk, dv, du, cos_t, sin_t, qg, kg, *, name):
    s = proj.shape[0]
    tm = _rows(s)

    def body(q_ref, k_ref, dq_ref, dk_ref, dv_ref, du_ref, c_ref, s_ref, qg_ref, kg_ref, dp_ref, dqg_ref, dkg_ref):
        c, sn = c_ref[...], s_ref[...]

        @pl.when(pl.program_id(0) == 0)
        def _():
            dqg_ref[...] = jnp.zeros_like(dqg_ref)
            dkg_ref[...] = jnp.zeros_like(dkg_ref)

        dqg = jnp.zeros((1, HEAD_DIM), F32)
        for g in range(N_GROUPS):
            for h in range(ATT_HEADS):
                sl = slice(h * HEAD_DIM, (h + 1) * HEAD_DIM)
                col = slice(g * KV_W + h * HEAD_DIM, g * KV_W + (h + 1) * HEAD_DIM)
                y = q_ref[:, col]
                dn = _rope_bwd(dq_ref[g, :, sl].astype(F32) * SCALE, c, sn)
                dy, dgain = _norm_bwd(y, _rms_r(y), qg_ref[...], dn)
                dp_ref[:, col] = dy.astype(BF16)
                dqg = dqg + dgain
        dqg_ref[...] += dqg

        dkg = jnp.zeros((1, HEAD_DIM), F32)
        for h in range(ATT_HEADS):
            sl = slice(h * HEAD_DIM, (h + 1) * HEAD_DIM)
            y = k_ref[:, sl]
            dn = _rope_bwd(dk_ref[:, sl], c, sn)
            dy, dgain = _norm_bwd(y, _rms_r(y), kg_ref[...], dn)
            dp_ref[:, Q_W + h * HEAD_DIM:Q_W + (h + 1) * HEAD_DIM] = dy.astype(BF16)
            dkg = dkg + dgain
        dkg_ref[...] += dkg

        dp_ref[:, Q_W + KV_W:Q_W + 2 * KV_W] = dv_ref[...].astype(BF16)
        dp_ref[:, Q_W + 2 * KV_W:] = du_ref[...].astype(BF16)

    row = lambda w, j: pl.BlockSpec((tm, w), lambda i: (i, j))
    one = pl.BlockSpec((1, HEAD_DIM), lambda i: (0, 0))
    return _pcall(
        body, name=name, grid=(s // tm,),
        in_specs=[row(Q_W, 0), row(KV_W, 3), pl.BlockSpec((N_GROUPS, tm, KV_W), lambda i: (0, i, 0))]
        + [row(KV_W, 0)] * 3 + [row(HEAD_DIM, 0), row(HEAD_DIM, 0), one, one],
        out_specs=[row(IN_W, 0), one, one],
        out_shape=[SDS((s, IN_W), BF16), SDS((1, HEAD_DIM), F32), SDS((1, HEAD_DIM), F32)],
        compiler_params=_cp())(proj, proj, dq, dk, dv, du, cos_t, sin_t, qg, kg)


ATT_ROWS = 16 * BLOCK


def _sub(ref, start, d):
    return ref[pl.ds(start, BLOCK, stride=d), :] if d > 1 else ref[pl.ds(start, BLOCK), :]


def _sub_set(ref, start, d, val):
    if d > 1:
        ref[pl.ds(start, BLOCK, stride=d), :] = val
    else:
        ref[pl.ds(start, BLOCK), :] = val


def _band_masks():
    row = lax.broadcasted_iota(jnp.int32, (BLOCK, BLOCK), 0)
    col = lax.broadcasted_iota(jnp.int32, (BLOCK, BLOCK), 1)
    return col <= row, col >= row


def _attn_fwd(q_rot, k_rot, v, shards, *, name):
    s = q_rot.shape[0]
    rr = ATT_ROWS
    nblk = s // rr
    names = tuple(shards)
    nr = len(names)

    def body(*refs):
        q0, q1, q2, kp, kc, vp, vc = refs[:7]
        mix_ref, lse_ref = refs[7 + nr:9 + nr]
        qs, ks, vs, os_, ls, zero_ref, send_sems, recv_sems, local_sems, zero_sems = refs[9 + 2 * nr:]
        h, n = pl.program_id(0), pl.program_id(1)
        w_full = refs[9 + nr:9 + 2 * nr]
        copies = _gather_copies(names, refs[7:7 + nr], w_full, send_sems, recv_sems, local_sems)
        pads = _down_pads(w_full[names.index("w_down")], zero_ref, zero_sems)

        @pl.when((h == 0) & (n == 0))
        def _():
            zero_ref[...] = jnp.zeros_like(zero_ref)
            _start(copies)
            for cp in pads:
                cp.start()

        for g, q_ref in enumerate((q0, q1, q2)):
            qs[g] = q_ref[...].astype(F32)
        ks[:rr] = kp[...].astype(F32)
        ks[rr:] = kc[...].astype(F32)
        vs[:rr] = vp[...].astype(F32)
        vs[rr:] = vc[...].astype(F32)
        m_cur, m_band = _band_masks()
        m_first = m_band & (n > 0)
        for g, d in enumerate(DILATIONS):
            for r in range(d):
                for j in range(rr // (BLOCK * d)):
                    base = j * BLOCK * d + r
                    prev = rr + base - BLOCK * d
                    q = _sub(qs.at[g], base, d).astype(BF16)
                    s_c = jnp.where(m_cur, _dot_nt(q, _sub(ks, rr + base, d).astype(BF16)), NEG_INF)
                    s_p = jnp.where(m_first if j == 0 else m_band, _dot_nt(q, _sub(ks, prev, d).astype(BF16)), NEG_INF)
                    m = jnp.maximum(jnp.max(s_c, axis=-1, keepdims=True), jnp.max(s_p, axis=-1, keepdims=True))
                    p_c = jnp.exp(s_c - m)
                    p_p = jnp.exp(s_p - m)
                    l = jnp.sum(p_c, axis=-1, keepdims=True) + jnp.sum(p_p, axis=-1, keepdims=True)
                    acc = (_dot(p_c.astype(BF16), _sub(vs, rr + base, d).astype(BF16))
                           + _dot(p_p.astype(BF16), _sub(vs, prev, d).astype(BF16)))
                    _sub_set(os_.at[g], base, d, acc / l)
                    _sub_set(ls.at[g], base, d, jnp.broadcast_to(m + jnp.log(l), (BLOCK, HEAD_DIM)))
        for c in _chunks(rr, 2 * BLOCK):
            a, b, cc = ls[0, c, :], ls[1, c, :], ls[2, c, :]
            m = jnp.maximum(jnp.maximum(a, b), cc)
            wa, wb, wc = jnp.exp(a - m), jnp.exp(b - m), jnp.exp(cc - m)
            den = wa + wb + wc
            mix_ref[c, :] = ((wa * os_[0, c, :] + wb * os_[1, c, :] + wc * os_[2, c, :]) / den).astype(BF16)
            lse_ref[c, :] = m + jnp.log(den)

        @pl.when((h == ATT_HEADS - 1) & (n == nblk - 1))
        def _():
            _finish(copies)
            for cp in pads:
                cp.wait()

    blk = lambda f: pl.BlockSpec((rr, HEAD_DIM), f)
    prv = lambda n: jnp.maximum(n - 1, 0)
    outs = _pcall(
        body, name=name, grid=(ATT_HEADS, nblk),
        in_specs=[blk(lambda h, n, g=g: (n, g * ATT_HEADS + h)) for g in range(N_GROUPS)]
        + [blk(lambda h, n: (prv(n), h)), blk(lambda h, n: (n, h))] * 2 + [HBM_SPEC] * nr,
        out_specs=[blk(lambda h, n: (n, h)), blk(lambda h, n: (n, h))] + [HBM_SPEC] * nr,
        out_shape=[SDS((s, KV_W + POOL_W), BF16), SDS((s, KV_W), F32)] + [SDS(FULL[w][0], BF16) for w in names],
        scratch_shapes=[pltpu.VMEM((N_GROUPS, rr, HEAD_DIM), F32), pltpu.VMEM((2 * rr, HEAD_DIM), F32),
                        pltpu.VMEM((2 * rr, HEAD_DIM), F32), pltpu.VMEM((N_GROUPS, rr, HEAD_DIM), F32),
                        pltpu.VMEM((N_GROUPS, rr, HEAD_DIM), F32), pltpu.VMEM((FF_TILE - FF_SHARD, D_MODEL), BF16)]
        + _comm_sems(nr) + [pltpu.SemaphoreType.DMA((FF_PAD // FF_TILE,))],
        compiler_params=_cp())(q_rot, q_rot, q_rot, k_rot, k_rot, v, v, *[shards[w] for w in names])
    return outs[0], outs[1], dict(zip(names, outs[2:]))


def _attn_bwd(q_rot, k_rot, v, mix, dmix, lse, grads, *, name):
    s = q_rot.shape[0]
    rr = ATT_ROWS
    nblk = s // rr
    names = tuple(grads)
    nr = len(names)

    def body(*refs):
        q0, q1, q2, qx0, qx1, qx2, kp, kc, vp, vc, do_c, do_x, o_c, o_x, l_c, l_x = refs[:16]
        dq_ref, dk_ref, dv_ref = refs[16 + nr:19 + nr]
        qs, ks, vs, dos, lss, dls, dqs, dks, dvs, send_sems, recv_sems, local_sems = refs[19 + 2 * nr:]
        h, n = pl.program_id(0), pl.program_id(1)
        copies = _exchange_copies(names, refs[16:16 + nr], refs[19 + nr:19 + 2 * nr], send_sems, recv_sems, local_sems)

        @pl.when((h == 0) & (n == 0))
        def _():
            _start(copies)

        for g, (qc_ref, qx_ref) in enumerate(((q0, qx0), (q1, qx1), (q2, qx2))):
            qs[g, :rr] = qc_ref[...].astype(F32)
            qs[g, rr:] = qx_ref[...].astype(F32)
        ks[:rr] = kp[...].astype(F32)
        ks[rr:] = kc[...].astype(F32)
        vs[:rr] = vp[...].astype(F32)
        vs[rr:] = vc[...].astype(F32)
        lss[:rr] = l_c[...]
        lss[rr:] = l_x[...]
        for half, (d_ref, o_ref) in enumerate(((do_c, o_c), (do_x, o_x))):
            for c in _chunks(rr, 2 * BLOCK):
                cs = slice(half * rr + c.start, half * rr + c.stop)
                dof = d_ref[c, :].astype(F32)
                dos[cs, :] = dof
                dls[cs, :] = jnp.broadcast_to(jnp.sum(dof * o_ref[c, :].astype(F32), axis=-1, keepdims=True),
                                              (2 * BLOCK, HEAD_DIM))
        m_cur, m_band = _band_masks()
        m_first = m_band & (n > 0)
        m_last = m_band & (n + 1 < nblk)

        def pair(qrow, krow, d, g, mask):
            q = _sub(qs.at[g], qrow, d).astype(BF16)
            do = _sub(dos, qrow, d).astype(BF16)
            k = _sub(ks, krow, d).astype(BF16)
            vv = _sub(vs, krow, d).astype(BF16)
            p = jnp.where(mask, jnp.exp(_dot_nt(q, k) - _sub(lss, qrow, d)[:, :1]), 0.0)
            ds = p * (_dot_nt(do, vv) - _sub(dls, qrow, d)[:, :1])
            return q, do, k, p.astype(BF16), ds.astype(BF16)

        for g, d in enumerate(DILATIONS):
            nsub = rr // (BLOCK * d)
            for r in range(d):
                _, _, k, _, ds = pair(r, rr + r - BLOCK * d, d, g, m_first)
                dq_acc = _dot(ds, k)
                for j in range(nsub):
                    base = j * BLOCK * d + r
                    q, do, k, p, ds = pair(base, rr + base, d, g, m_cur)
                    _sub_set(dqs.at[g], base, d, dq_acc + _dot(ds, k))
                    nxt = base + BLOCK * d
                    qn, don, _, pn, dsn = pair(nxt, rr + base, d, g, m_band if j + 1 < nsub else m_last)
                    dq_acc = _dot(dsn, k)
                    dk = _dot_tn(ds, q) + _dot_tn(dsn, qn)
                    dv = _dot_tn(p, do) + _dot_tn(pn, don)
                    if g == 0:
                        _sub_set(dks, base, d, dk)
                        _sub_set(dvs, base, d, dv)
                    else:
                        _sub_set(dks, base, d, _sub(dks, base, d) + dk)
                        _sub_set(dvs, base, d, _sub(dvs, base, d) + dv)
        for g in range(N_GROUPS):
            dq_ref[g] = dqs[g].astype(BF16)
        dk_ref[...] = dks[...]
        dv_ref[...] = dvs[...]

        @pl.when((h == ATT_HEADS - 1) & (n == nblk - 1))
        def _():
            _finish(copies)

    blk = lambda f: pl.BlockSpec((rr, HEAD_DIM), f)
    prv = lambda n: jnp.maximum(n - 1, 0)
    nxt = lambda n: jnp.minimum(n + 1, nblk - 1)
    cur_kv = blk(lambda h, n: (n, h))
    outs = _pcall(
        body, name=name, grid=(ATT_HEADS, nblk),
        in_specs=[blk(lambda h, n, g=g: (n, g * ATT_HEADS + h)) for g in range(N_GROUPS)]
        + [blk(lambda h, n, g=g: (nxt(n), g * ATT_HEADS + h)) for g in range(N_GROUPS)]
        + [blk(lambda h, n: (prv(n), h)), cur_kv] * 2
        + [cur_kv, blk(lambda h, n: (nxt(n), h))] * 3 + [HBM_SPEC] * nr,
        out_specs=[pl.BlockSpec((N_GROUPS, rr, HEAD_DIM), lambda h, n: (0, n, h)), cur_kv, cur_kv] + [HBM_SPEC] * nr,
        out_shape=[SDS((N_GROUPS, s, KV_W), BF16), SDS((s, KV_W), F32), SDS((s, KV_W), F32)]
        + [SDS((N_DEV,) + _shard_shape(w), BF16) for w in names],
        scratch_shapes=[pltpu.VMEM((N_GROUPS, 2 * rr, HEAD_DIM), F32)] + [pltpu.VMEM((2 * rr, HEAD_DIM), F32)] * 5
        + [pltpu.VMEM((N_GROUPS, rr, HEAD_DIM), F32), pltpu.VMEM((rr, HEAD_DIM), F32), pltpu.VMEM((rr, HEAD_DIM), F32)]
        + _comm_sems(nr),
        compiler_params=_cp())(q_rot, q_rot, q_rot, q_rot, q_rot, q_rot, k_rot, k_rot, v, v, dmix, dmix, mix, mix,
                               lse, lse, *[grads[w] for w in names])
    return outs[0], outs[1], outs[2], dict(zip(names, outs[3:]))


def _pool_d(u_ref, halo_ref, i, tm):
    halo = jnp.where(i > 0, halo_ref[...], 0.0)
    t = i * tm + lax.broadcasted_iota(jnp.int32, (tm, 1), 0)
    out = []
    for g, w in enumerate(POOL_WINDOWS):
        sl = slice(g * HEAD_DIM, (g + 1) * HEAD_DIM)
        u = u_ref[:, sl]
        acc = jnp.concatenate([halo[:, sl], u], axis=0)
        sh = 1
        while sh < w:
            acc = acc + pltpu.roll(acc, sh, 0)
            sh *= 2
        cnt = jnp.minimum(t + 1, w).astype(F32)
        out.append(acc[HALO:, :] / cnt - u)
    return out


def _pool_fwd(proj, mix, pool_w, pool_scale, *, name):
    s = proj.shape[0]
    tm = _rows(s)
    ucol = (IN_W - POOL_W) // POOL_W

    def body(u_ref, halo_ref, mix_in, w_ref, sc_ref, o_ref):
        del mix_in
        dd = _pool_d(u_ref, halo_ref, pl.program_id(0), tm)
        for g in range(len(POOL_WINDOWS)):
            sl = slice(g * HEAD_DIM, (g + 1) * HEAD_DIM)
            y = _dot(dd[g].astype(BF16), w_ref[g].astype(BF16))
            o_ref[:, sl] = (y * sc_ref[:, sl]).astype(BF16)

    return _pcall(
        body, name=name, grid=(s // tm,),
        in_specs=[pl.BlockSpec((tm, POOL_W), lambda i: (i, ucol)),
                  pl.BlockSpec((HALO, POOL_W), lambda i: (jnp.maximum(i * (tm // HALO) - 1, 0), ucol)),
                  pl.BlockSpec(memory_space=pl.ANY),
                  pl.BlockSpec((len(POOL_WINDOWS), HEAD_DIM, HEAD_DIM), lambda i: (0, 0, 0)),
                  pl.BlockSpec((1, POOL_W), lambda i: (0, 0))],
        out_specs=pl.BlockSpec((tm, POOL_W), lambda i: (i, 1)),
        out_shape=SDS(mix.shape, BF16), input_output_aliases={2: 0},
        compiler_params=_cp())(proj, proj, mix, pool_w, pool_scale)


def _pool_bwd(proj, dmix, pool_w, pool_scale, *, name):
    s = proj.shape[0]
    tm = _rows(s)
    nblk = s // tm
    ucol = (IN_W - POOL_W) // POOL_W
    ng = len(POOL_WINDOWS)

    def body(u_ref, halo_ref, dp_ref, dpn_ref, w_ref, sc_ref, du_ref, dw_ref, dsc_ref):
        i = pl.program_id(0)

        @pl.when(i == 0)
        def _():
            dw_ref[...] = jnp.zeros_like(dw_ref)
            dsc_ref[...] = jnp.zeros_like(dsc_ref)

        dd = _pool_d(u_ref, halo_ref, i, tm)
        t = i * tm + lax.broadcasted_iota(jnp.int32, (tm, 1), 0)
        dpn = jnp.where(i + 1 < nblk, dpn_ref[...].astype(F32), 0.0)
        for g, w in enumerate(POOL_WINDOWS):
            sl = slice(g * HEAD_DIM, (g + 1) * HEAD_DIM)
            wg = w_ref[g].astype(BF16)
            db = dd[g].astype(BF16)
            dp = dp_ref[:, sl].astype(F32)
            dsc_ref[:, sl] += jnp.sum(dp * _dot(db, wg), axis=0, keepdims=True)
            dy = (dp * sc_ref[:, sl]).astype(BF16)
            dw_ref[g] += _dot_tn(db, dy)
            g_d = _dot_nt(dy, wg)
            g_dn = _dot_nt((dpn[:, sl] * sc_ref[:, sl]).astype(BF16), wg)
            cnt = jnp.minimum(t + 1, w).astype(F32)
            acc = jnp.concatenate([g_d / cnt, g_dn * (1.0 / w)], axis=0)
            sh = 1
            while sh < w:
                acc = acc + pltpu.roll(acc, tm + HALO - sh, 0)
                sh *= 2
            du_ref[:, sl] = acc[:tm, :] - g_d

    nh = s // HALO
    return _pcall(
        body, name=name, grid=(nblk,),
        in_specs=[pl.BlockSpec((tm, POOL_W), lambda i: (i, ucol)),
                  pl.BlockSpec((HALO, POOL_W), lambda i: (jnp.maximum(i * (tm // HALO) - 1, 0), ucol)),
                  pl.BlockSpec((tm, POOL_W), lambda i: (i, 1)),
                  pl.BlockSpec((HALO, POOL_W), lambda i: (jnp.minimum((i + 1) * (tm // HALO), nh - 1), 1)),
                  pl.BlockSpec((ng, HEAD_DIM, HEAD_DIM), lambda i: (0, 0, 0)),
                  pl.BlockSpec((1, POOL_W), lambda i: (0, 0))],
        out_specs=[pl.BlockSpec((tm, POOL_W), lambda i: (i, 0)),
                   pl.BlockSpec((ng, HEAD_DIM, HEAD_DIM), lambda i: (0, 0, 0)),
                   pl.BlockSpec((1, POOL_W), lambda i: (0, 0))],
        out_shape=[SDS((s, POOL_W), F32), SDS((ng, HEAD_DIM, HEAD_DIM), F32), SDS((1, POOL_W), F32)],
        compiler_params=_cp())(proj, proj, dmix, dmix, pool_w, pool_scale)


def _mem_fwd(mem, mem_g, wckv, ck_g, *, name):
    def body(m_ref, g_ref, w_ref, kg_ref, mn_ref, ckr_ref, ckn_ref, cv_ref):
        mv = m_ref[...]
        mn = (mv * _rms_r(mv) * g_ref[...]).astype(BF16)
        mn_ref[...] = mn
        ckv = _dot(mn, w_ref[...])
        ckr_ref[...] = ckv[:, :X_W]
        cv_ref[...] = ckv[:, X_W:].astype(BF16)
        for h in range(ATT_HEADS):
            sl = slice(h * HEAD_DIM, (h + 1) * HEAD_DIM)
            y = ckv[:, sl]
            ckn_ref[:, sl] = (y * _rms_r(y) * kg_ref[...]).astype(BF16)

    return _pcall(
        body, name=name,
        out_shape=[SDS((N_MEM, D_MODEL), BF16), SDS((N_MEM, X_W), F32), SDS((N_MEM, X_W), BF16),
                   SDS((N_MEM, X_W), BF16)],
        compiler_params=_cp())(mem, mem_g, wckv, ck_g)


def _cross_q(cq_ref, g_ref, sl):
    y = cq_ref[:, sl]
    r = _rms_r(y)
    return y, r, y * r * g_ref[...] * SCALE


def _cross_fwd(cq_raw, ck_n, cv, cq_g, *, name):
    s = cq_raw.shape[0]
    tm = _rows(s)

    def body(cq_ref, k_ref, v_ref, g_ref, o_ref):
        for h in range(ATT_HEADS):
            sl = slice(h * HEAD_DIM, (h + 1) * HEAD_DIM)
            _, _, qn = _cross_q(cq_ref, g_ref, sl)
            sc = _dot_nt(qn.astype(BF16), k_ref[:, sl])
            p = jnp.exp(sc - jnp.max(sc, axis=-1, keepdims=True))
            p = p / jnp.sum(p, axis=-1, keepdims=True)
            o_ref[:, sl] = _dot(p.astype(BF16), v_ref[:, sl]).astype(BF16)

    full = lambda a: pl.BlockSpec(a.shape, lambda i: (0, 0))
    return _pcall(
        body, name=name, grid=(s // tm,),
        in_specs=[pl.BlockSpec((tm, X_W), lambda i: (i, 0)), full(ck_n), full(cv), full(cq_g)],
        out_specs=pl.BlockSpec((tm, X_W), lambda i: (i, 0)),
        out_shape=SDS((s, X_W), BF16), compiler_params=_cp())(cq_raw, ck_n, cv, cq_g)


def _cross_bwd(d_o, cq_raw, ck_n, cv, cq_g, *, name):
    s = cq_raw.shape[0]
    tm = _rows(s)

    def body(do_ref, cq_ref, k_ref, v_ref, g_ref, dcq_ref, dk_ref, dv_ref, dg_ref):
        @pl.when(pl.program_id(0) == 0)
        def _():
            dk_ref[...] = jnp.zeros_like(dk_ref)
            dv_ref[...] = jnp.zeros_like(dv_ref)
            dg_ref[...] = jnp.zeros_like(dg_ref)

        dg = jnp.zeros((1, HEAD_DIM), F32)
        for h in range(ATT_HEADS):
            sl = slice(h * HEAD_DIM, (h + 1) * HEAD_DIM)
            y, r, qn = _cross_q(cq_ref, g_ref, sl)
            qb = qn.astype(BF16)
            do = do_ref[:, sl]
            sc = _dot_nt(qb, k_ref[:, sl])
            p = jnp.exp(sc - jnp.max(sc, axis=-1, keepdims=True))
            p = p / jnp.sum(p, axis=-1, keepdims=True)
            dv_ref[:, sl] += _dot_tn(p.astype(BF16), do)
            dp = _dot_nt(do, v_ref[:, sl])
            ds = (p * (dp - jnp.sum(dp * p, axis=-1, keepdims=True))).astype(BF16)
            dk_ref[:, sl] += _dot_tn(ds, qb)
            dn = _dot(ds, k_ref[:, sl]) * SCALE
            dy, dgain = _norm_bwd(y, r, g_ref[...], dn)
            dcq_ref[:, sl] = dy.astype(BF16)
            dg = dg + dgain
        dg_ref[...] += dg

    full = lambda a: pl.BlockSpec(a.shape, lambda i: (0, 0))
    row = pl.BlockSpec((tm, X_W), lambda i: (i, 0))
    return _pcall(
        body, name=name, grid=(s // tm,),
        in_specs=[row, row, full(ck_n), full(cv), full(cq_g)],
        out_specs=[row, pl.BlockSpec((N_MEM, X_W), lambda i: (0, 0)), pl.BlockSpec((N_MEM, X_W), lambda i: (0, 0)),
                   pl.BlockSpec((1, HEAD_DIM), lambda i: (0, 0))],
        out_shape=[SDS((s, X_W), BF16), SDS((N_MEM, X_W), F32), SDS((N_MEM, X_W), F32), SDS((1, HEAD_DIM), F32)],
        compiler_params=_cp())(d_o, cq_raw, ck_n, cv, cq_g)


def _mem_bwd(dck_n, dcv, ck_raw, memn, mem, wckv, mem_g, ck_g, *, name):
    def body(dk_ref, dv_ref, ckr_ref, mn_ref, m_ref, w_ref, mg_ref, kg_ref, dw_ref, dmg_ref, dkg_ref, dckv_s):
        dkg = jnp.zeros((1, HEAD_DIM), F32)
        for h in range(ATT_HEADS):
            sl = slice(h * HEAD_DIM, (h + 1) * HEAD_DIM)
            y = ckr_ref[:, sl]
            dy, dgain = _norm_bwd(y, _rms_r(y), kg_ref[...], dk_ref[:, sl])
            dckv_s[:, sl] = dy.astype(BF16)
            dkg = dkg + dgain
        dkg_ref[...] = dkg
        dckv_s[:, X_W:] = dv_ref[...].astype(BF16)
        dckv = dckv_s[...]
        dw_ref[...] = _dot_tn(mn_ref[...], dckv).astype(BF16)
        dmn = _dot_nt(dckv, w_ref[...])
        mv = m_ref[...]
        dmg_ref[...] = jnp.sum(dmn * mv * _rms_r(mv), axis=0, keepdims=True)

    return _pcall(
        body, name=name,
        out_shape=[SDS((D_MODEL, 2 * X_W), BF16), SDS((1, D_MODEL), F32), SDS((1, HEAD_DIM), F32)],
        scratch_shapes=[pltpu.VMEM((N_MEM, 2 * X_W), BF16)],
        compiler_params=_cp())(dck_n, dcv, ck_raw, memn, mem, wckv, mem_g, ck_g)


def _rope_tables(pos):
    inv_freq = ROPE_THETA ** (-jnp.arange(0, ROT_DIM, 2, dtype=F32) / ROT_DIM)
    ang = pos.astype(F32)[:, None] * inv_freq
    cos, sin = jnp.cos(ang), jnp.sin(ang)
    s = pos.shape[0]
    rest = HEAD_DIM - ROT_DIM
    cos_t = jnp.concatenate([cos, cos, jnp.ones((s, rest), F32)], axis=1)
    sin_t = jnp.concatenate([-sin, sin, jnp.zeros((s, rest), F32)], axis=1)
    return cos_t, sin_t


def _local_step(x, mem, pos, tgt, w_in, shards, sm):
    cos_t, sin_t = _rope_tables(pos)
    wb = {"w_in": w_in}

    proj, xn = _norm_linear(x, sm["mix_norm_g"], wb["w_in"], cn=768, out_dtype=F32, name="fwd_in_proj")
    q_rot, k_rot, v = _qk_prep(proj, cos_t, sin_t, sm["q_norm_g"], sm["k_norm_g"], name="fwd_qk_prep")
    mix, lse_all, rest = _attn_fwd(q_rot, k_rot, v, shards, name="fwd_attn")
    wb.update(rest)
    mix = _pool_fwd(proj, mix, sm["pool_w"], sm["pool_scale"], name="fwd_pool")
    h1 = _linear_res(mix, wb["w_out"], x, cn=512, name="fwd_out_proj")
    cq_raw, hn = _norm_linear(h1, sm["cross_norm_g"], wb["w_cq"], cn=512, out_dtype=F32, name="fwd_cq_proj")
    memn, ck_raw, ck_n, cv = _mem_fwd(mem, sm["mem_norm_g"], wb["w_ckv"], sm["ck_norm_g"], name="fwd_mem")
    co = _cross_fwd(cq_raw, ck_n, cv, sm["cq_norm_g"], name="fwd_cross")
    h2 = _linear_res(co, wb["w_co"], h1, cn=512, name="fwd_co_proj")
    gu, act, fn = _norm_linear_swiglu(h2, sm["ffn_norm_g"], wb["w_gate_up"], name="fwd_gate_up")
    dy, dyb, sq = _linear_res_loss(act, wb["w_down"], h2, tgt, name="fwd_down_loss")

    gw = {}
    gs = {}
    dgu = _swiglu_bwd(dyb, wb["w_down"], gu, name="bwd_swiglu")
    gw["w_down"] = _dw_tn(act, dyb, tkw=1024, tn=1024, name="bwd_dw_down")
    dh2, dh2b, gs["ffn_norm_g"] = _linear_nt_normbwd(dgu, wb["w_gate_up"], h2, dy, sm["ffn_norm_g"],
                                                     name="bwd_ffn_in")
    gw["w_gate_up"] = _dw_tn(fn, dgu, tkw=1024, tn=1536, name="bwd_dw_gate_up")

    d_co = _linear_nt(dh2b, wb["w_co"], cn=512, name="bwd_co_proj")
    gw["w_co"] = _dw_tn(co, dh2b, tkw=512, tn=1024, name="bwd_dw_co")
    dcq, dck_n, dcv, gs["cq_norm_g"] = _cross_bwd(d_co, cq_raw, ck_n, cv, sm["cq_norm_g"], name="bwd_cross")
    gw["w_ckv"], gs["mem_norm_g"], gs["ck_norm_g"] = _mem_bwd(dck_n, dcv, ck_raw, memn, mem, wb["w_ckv"],
                                                             sm["mem_norm_g"], sm["ck_norm_g"], name="bwd_mem")
    dh1, dh1b, gs["cross_norm_g"] = _linear_nt_normbwd(dcq, wb["w_cq"], h1, dh2, sm["cross_norm_g"],
                                                       name="bwd_cq_in")
    gw["w_cq"] = _dw_tn(hn, dcq, tkw=1024, tn=512, name="bwd_dw_cq")

    dmix = _linear_nt(dh1b, wb["w_out"], cn=512, name="bwd_out_proj")
    gw["w_out"] = _dw_tn(mix, dh1b, tkw=1024, tn=1024, name="bwd_dw_out")
    du, gs["pool_w"], gs["pool_scale"] = _pool_bwd(proj, dmix, sm["pool_w"], sm["pool_scale"], name="bwd_pool")
    dq, dk, dv, parts = _attn_bwd(q_rot, k_rot, v, mix, dmix, lse_all, gw, name="bwd_attn")
    dproj, gs["q_norm_g"], gs["k_norm_g"] = _qk_prep_bwd(proj, dq, dk, dv, du, cos_t, sin_t, sm["q_norm_g"],
                                                         sm["k_norm_g"], name="bwd_qk_prep")
    gw_in = _dw_tn(xn, dproj, tkw=1024, tn=1536, name="bwd_dw_in")
    dx, _, gs["mix_norm_g"], last = _linear_nt_normbwd(dproj, wb["w_in"], x, dh1, sm["mix_norm_g"],
                                                       name="bwd_in_proj", grads={"w_in": gw_in})
    parts.update(last)
    return sq, dx, parts, gs


SHARDED = ("w_in", "w_out", "w_cq", "w_ckv", "w_co", "w_gate_up", "w_down")
SMALL = ("mix_norm_g", "q_norm_g", "k_norm_g", "pool_w", "pool_scale", "cross_norm_g", "mem_norm_g", "cq_norm_g",
         "ck_norm_g", "ffn_norm_g")
FULL = {
    "w_in": ((D_MODEL, IN_W), 1, IN_W // N_DEV),
    "w_out": ((D_MODEL, D_MODEL), 0, D_MODEL // N_DEV),
    "w_cq": ((D_MODEL, X_W), 0, D_MODEL // N_DEV),
    "w_ckv": ((D_MODEL, 2 * X_W), 0, D_MODEL // N_DEV),
    "w_co": ((X_W, D_MODEL), 1, D_MODEL // N_DEV),
    "w_gate_up": ((D_MODEL, 2 * FF_PAD), 1, FF_TILE),
    "w_down": ((FF_PAD, D_MODEL), 0, DOWN_SHARD),
}


def _shard_shape(name):
    shape, axis, width = FULL[name]
    return tuple(width if a == axis else n for a, n in enumerate(shape))


def _window(ref, name, dev):
    _, axis, width = FULL[name]
    if name == "w_down":
        start = pl.multiple_of((dev // 2) * FF_TILE + (dev % 2) * DOWN_SHARD, HALO)
    else:
        start = pl.multiple_of(dev * width, BLOCK)
    return ref.at[pl.ds(start, width), :] if axis == 0 else ref.at[:, pl.ds(start, width)]


def _mesh_place():
    x, y, c = lax.axis_index("x"), lax.axis_index("y"), lax.axis_index("c")
    return x, y, c, 4 * x + 2 * y + c


def _peer(x, y, c, k):
    px = 1 - x if k & 4 else x
    py = 1 - y if k & 2 else y
    pc = 1 - c if k & 1 else c
    return (px, py, pc), 4 * px + 2 * py + pc


HBM_SPEC = pl.BlockSpec(memory_space=pltpu.HBM)


def _comm_sems(n):
    return [pltpu.SemaphoreType.DMA((n, N_DEV - 1)), pltpu.SemaphoreType.DMA((n, N_DEV - 1)),
            pltpu.SemaphoreType.DMA((n,))]


def _gather_copies(names, ins, outs, send_sems, recv_sems, local_sems):
    x, y, c, me = _mesh_place()
    local, sent, recv = [], [], []
    for wi, name in enumerate(names):
        local.append(pltpu.make_async_copy(ins[wi], _window(outs[wi], name, me), local_sems.at[wi]))
        for k in range(1, N_DEV):
            peer, pidx = _peer(x, y, c, k)
            sems = dict(send_sem=send_sems.at[wi, k - 1], recv_sem=recv_sems.at[wi, k - 1], device_id=peer,
                        device_id_type=MESH)
            sent.append(pltpu.make_async_remote_copy(src_ref=ins[wi], dst_ref=_window(outs[wi], name, me), **sems))
            recv.append(pltpu.make_async_remote_copy(src_ref=ins[wi], dst_ref=_window(outs[wi], name, pidx), **sems))
    return local, sent, recv


def _exchange_copies(names, ins, outs, send_sems, recv_sems, local_sems):
    x, y, c, me = _mesh_place()
    local, sent, recv = [], [], []
    for wi, name in enumerate(names):
        local.append(pltpu.make_async_copy(_window(ins[wi], name, me), outs[wi].at[0], local_sems.at[wi]))
        for k in range(1, N_DEV):
            peer, pidx = _peer(x, y, c, k)
            sems = dict(send_sem=send_sems.at[wi, k - 1], recv_sem=recv_sems.at[wi, k - 1], device_id=peer,
                        device_id_type=MESH)
            sent.append(pltpu.make_async_remote_copy(src_ref=_window(ins[wi], name, pidx), dst_ref=outs[wi].at[k], **sems))
            recv.append(pltpu.make_async_remote_copy(src_ref=_window(ins[wi], name, me), dst_ref=outs[wi].at[k], **sems))
    return local, sent, recv


def _down_pads(down_ref, zero_ref, zero_sems):
    return [pltpu.make_async_copy(zero_ref, down_ref.at[pl.ds(t * FF_TILE + FF_SHARD, FF_TILE - FF_SHARD), :],
                                  zero_sems.at[t]) for t in range(FF_PAD // FF_TILE)]


def _start(copies):
    local, sent, _ = copies
    for cp in local + sent:
        cp.start()


def _finish(copies):
    local, sent, recv = copies
    for cp in recv:
        cp.wait_recv()
    for cp in sent:
        cp.wait_send()
    for cp in local:
        cp.wait()


def _gather_weights(shards):
    names = tuple(shards)
    nw = len(names)

    def body(*refs):
        copies = _gather_copies(names, refs[:nw], refs[nw:2 * nw], *refs[2 * nw:])
        _start(copies)
        _finish(copies)

    outs = _pcall(
        body, name="gather_weights",
        in_specs=[HBM_SPEC] * nw, out_specs=[HBM_SPEC] * nw,
        out_shape=[SDS(FULL[n][0], BF16) for n in names],
        scratch_shapes=_comm_sems(nw))(*[shards[n] for n in names])
    return dict(zip(names, outs))


def _exchange_small(small):
    def body(small_ref, out_ref, send_sems, recv_sems, local_sems):
        x, y, c, me = _mesh_place()
        local = pltpu.make_async_copy(small_ref, out_ref.at[me], local_sems.at[0])
        local.start()
        sent = []
        for k in range(1, N_DEV):
            peer, _ = _peer(x, y, c, k)
            cp = pltpu.make_async_remote_copy(
                src_ref=small_ref, dst_ref=out_ref.at[me], send_sem=send_sems.at[0, k - 1],
                recv_sem=recv_sems.at[0, k - 1], device_id=peer, device_id_type=MESH)
            cp.start()
            sent.append(cp)
        for k in range(1, N_DEV):
            peer, pidx = _peer(x, y, c, k)
            pltpu.make_async_remote_copy(
                src_ref=small_ref, dst_ref=out_ref.at[pidx], send_sem=send_sems.at[0, k - 1],
                recv_sem=recv_sems.at[0, k - 1], device_id=peer, device_id_type=MESH).wait_recv()
        for cp in sent:
            cp.wait_send()
        local.wait()

    return _pcall(
        body, name="exchange_small", in_specs=[HBM_SPEC], out_specs=HBM_SPEC,
        out_shape=SDS((N_DEV,) + small.shape, F32), scratch_shapes=_comm_sems(1))(small)


def _adamw(parts, w, m, v, *, name):
    r, c = w.shape
    tr = r
    for cand in (256, 128, 88):
        if r % cand == 0:
            tr = cand
            break

    def body(p_ref, w_ref, m_ref, v_ref, g_ref, d_ref, mo_ref, vo_ref):
        g = p_ref[0].astype(F32)
        for k in range(1, N_DEV):
            g = g + p_ref[k].astype(F32)
        m_new = ADAM_B1 * m_ref[...] + (1.0 - ADAM_B1) * g
        v_new = ADAM_B2 * v_ref[...] + (1.0 - ADAM_B2) * (g * g)
        m_hat = m_new / (1.0 - ADAM_B1 ** ADAM_STEP)
        v_hat = v_new / (1.0 - ADAM_B2 ** ADAM_STEP)
        g_ref[...] = g
        d_ref[...] = -ADAM_LR * (m_hat / (jnp.sqrt(v_hat) + ADAM_EPS) + ADAM_WD * w_ref[...])
        mo_ref[...] = m_new
        vo_ref[...] = v_new

    row = pl.BlockSpec((tr, c), lambda i: (i, 0))
    return _pcall(
        body, name=name, grid=(r // tr,),
        in_specs=[pl.BlockSpec((N_DEV, tr, c), lambda i: (0, i, 0)), row, row, row],
        out_specs=[row] * 4, out_shape=[SDS((r, c), F32)] * 4, compiler_params=_cp())(parts, w, m, v)


def _pack_small(d):
    parts = []
    for n in SMALL:
        a = d[n].reshape(-1, HEAD_DIM)
        parts.append(jnp.pad(a, ((0, -a.shape[0] % 8), (0, 0))))
    return jnp.concatenate(parts, axis=0)


def _unpack_small(packed, like):
    out, row = {}, 0
    for n in SMALL:
        rows = like[n].size // HEAD_DIM
        out[n] = packed[row:row + rows].reshape(like[n].shape)
        row += rows + (-rows % 8)
    return out


def _pad_cols(a, width):
    return jnp.pad(a, ((0, 0), (0, width - a.shape[1])))


def kernel(x, mem, positions, mix_norm_g, w_in, q_norm_g, k_norm_g, pool_w, pool_scale, w_out, cross_norm_g, mem_norm_g, w_cq, w_ckv, cq_norm_g, ck_norm_g, w_co, ffn_norm_g, w_gate_up, w_down, loss_target, m_mix_norm_g, m_w_in, m_q_norm_g, m_k_norm_g, m_pool_w, m_pool_scale, m_w_out, m_cross_norm_g, m_mem_norm_g, m_w_cq, m_w_ckv, m_cq_norm_g, m_ck_norm_g, m_w_co, m_ffn_norm_g, m_w_gate_up, m_w_down, v_mix_norm_g, v_w_in, v_q_norm_g, v_k_norm_g, v_pool_w, v_pool_scale, v_w_out, v_cross_norm_g, v_mem_norm_g, v_w_cq, v_w_ckv, v_cq_norm_g, v_ck_norm_g, v_w_co, v_ffn_norm_g, v_w_gate_up, v_w_down):
    given = dict(locals())
    w_f32 = {n: given[n][0] for n in SHARDED + SMALL}
    m_f32 = {n: given["m_" + n][0] for n in SHARDED + SMALL}
    v_f32 = {n: given["v_" + n][0] for n in SHARDED + SMALL}
    for d in (w_f32, m_f32, v_f32):
        d["w_gate_up"] = _pad_cols(d["w_gate_up"], FF_TILE)

    shards = {n: w_f32[n].astype(BF16) for n in SHARDED}
    w_in_full = _gather_weights({"w_in": shards.pop("w_in")})["w_in"]
    sm = {n: w_f32[n] for n in SMALL}
    sm_rows = {n: (a if a.ndim == 3 else a.reshape(1, -1)) for n, a in sm.items()}
    sq, dx, parts, gs = _local_step(x[0], mem[0], positions[0], loss_target[0], w_in_full, shards, sm_rows)
    loss = lax.psum(0.5 / D_MODEL * jnp.sum(sq), ("x", "y", "c"))

    small_parts = _exchange_small(_pack_small(gs))
    res = {}
    for n in SHARDED:
        res[n] = _adamw(parts[n], w_f32[n], m_f32[n], v_f32[n], name="adamw_" + n)
    res["w_gate_up"] = [a[:, :FF_SHARD] for a in res["w_gate_up"]]
    small_res = _adamw(small_parts, _pack_small(sm), _pack_small({n: m_f32[n] for n in SMALL}),
                       _pack_small({n: v_f32[n] for n in SMALL}), name="adamw_small")
    small_res = [_unpack_small(a, sm) for a in small_res]
    order = ("mix_norm_g", "w_in", "q_norm_g", "k_norm_g", "pool_w", "pool_scale", "w_out", "cross_norm_g",
             "mem_norm_g", "w_cq", "w_ckv", "cq_norm_g", "ck_norm_g", "w_co", "ffn_norm_g", "w_gate_up", "w_down")
    outs = [loss, dx[None]]
    for which in range(4):
        for n in order:
            a = res[n][which] if n in SHARDED else small_res[which][n]
            outs.append(a[None])
    return tuple(outs)
```

```python
import functools

import jax
import jax.numpy as jnp
from jax import lax
from jax.experimental import pallas as pl
from jax.experimental.pallas import tpu as pltpu

F32 = jnp.float32
BF16 = jnp.bfloat16
SDS = jax.ShapeDtypeStruct

D_MODEL = 1024
HEAD_DIM = 128
N_GROUPS = 3
DILATIONS = (1, 4, 16)
ATT_HEADS = 4
Q_W = 1536
KV_W = 512
POOL_W = 512
POOL_WINDOWS = (2, 4, 8, 16)
IN_W = 3072
X_W = 512
N_MEM = 256
D_FF = 2816
FF_TILE = 768
FF_SHARD = 704
FF_PAD = 4 * FF_TILE
DOWN_SHARD = 352
ROT_DIM = 32
ROT_HALF = 16
ROPE_THETA = 500000.0
EPS = 1e-6
NEG_INF = -1e30
SCALE = HEAD_DIM ** -0.5
BLOCK = 128
HALO = 16

ADAM_LR = 0.001
ADAM_B1 = 0.9
ADAM_B2 = 0.999
ADAM_EPS = 1e-08
ADAM_WD = 0.01
ADAM_STEP = 10

N_DEV = 8
VMEM_LIMIT_BYTES = 56 * 1024 * 1024
MESH = pl.DeviceIdType.MESH


def _pcall(body, **kw):
    return pl.pallas_call(body, **kw)


def _cp():
    return pltpu.CompilerParams(vmem_limit_bytes=VMEM_LIMIT_BYTES)


def _dot(a, b):
    return lax.dot_general(a, b, (((1,), (0,)), ((), ())), preferred_element_type=F32)


def _dot_nt(a, b):
    return lax.dot_general(a, b, (((1,), (1,)), ((), ())), preferred_element_type=F32)


def _dot_tn(a, b):
    return lax.dot_general(a, b, (((0,), (0,)), ((), ())), preferred_element_type=F32)


def _rows(s):
    return min(512, s)


def _rms_r(x):
    return lax.rsqrt(jnp.mean(x * x, axis=-1, keepdims=True) + EPS)


def _norm_bwd(x, r, gain, dxn):
    z = dxn * gain
    dx = r * z - x * (r * r * r * jnp.mean(z * x, axis=-1, keepdims=True))
    dgain = jnp.sum(dxn * x * r, axis=0, keepdims=True)
    return dx, dgain


def _partner(t):
    lane = lax.broadcasted_iota(jnp.int32, t.shape, 1)
    return jnp.where(lane < ROT_HALF, pltpu.roll(t, HEAD_DIM - ROT_HALF, 1), pltpu.roll(t, ROT_HALF, 1))


def _rope(n, cos_t, sin_t):
    return n * cos_t + _partner(n) * sin_t


def _rope_bwd(d, cos_t, sin_t):
    lane = lax.broadcasted_iota(jnp.int32, d.shape, 1)
    return d * cos_t + jnp.where(lane < ROT_DIM, _partner(d * sin_t), 0.0)


def _resident(shape):
    return pl.BlockSpec(shape, lambda i: (0,) * len(shape), pipeline_mode=pl.Buffered(1))


def _chunks(n, cn):
    return [slice(j * cn, (j + 1) * cn) for j in range(n // cn)]


def _norm_linear(x, gain, w, *, cn, out_dtype, name):
    s, k = x.shape
    n = w.shape[1]
    tm = _rows(s)

    def body(x_ref, g_ref, w_ref, y_ref, xn_ref):
        xv = x_ref[...]
        xn_ref[...] = (xv * _rms_r(xv) * g_ref[...]).astype(BF16)
        for c in _chunks(n, cn):
            y_ref[:, c] = _dot(xn_ref[...], w_ref[:, c]).astype(out_dtype)

    row = lambda w_: pl.BlockSpec((tm, w_), lambda i: (i, 0))
    return _pcall(
        body, name=name, grid=(s // tm,),
        in_specs=[row(k), _resident((1, k)), _resident((k, n))],
        out_specs=[row(n), row(k)],
        out_shape=[SDS((s, n), out_dtype), SDS((s, k), BF16)], compiler_params=_cp())(x, gain, w)


def _norm_linear_swiglu(x, gain, wgu, *, name):
    s, k = x.shape
    tm = _rows(s)

    def body(x_ref, g_ref, w_ref, gu_ref, a_ref, xn_ref):
        xv = x_ref[...]
        xn_ref[...] = (xv * _rms_r(xv) * g_ref[...]).astype(BF16)
        for c in _chunks(FF_PAD, FF_TILE):
            g = _dot(xn_ref[...], w_ref[:, c])
            u = _dot(xn_ref[...], w_ref[:, slice(FF_PAD + c.start, FF_PAD + c.stop)])
            a_ref[:, c] = (g * jax.nn.sigmoid(g) * u).astype(BF16)
            gu_ref[0, :, c] = g.astype(BF16)
            gu_ref[1, :, c] = u.astype(BF16)

    row = lambda w_: pl.BlockSpec((tm, w_), lambda i: (i, 0))
    return _pcall(
        body, name=name, grid=(s // tm,),
        in_specs=[row(k), _resident((1, k)), _resident((k, 2 * FF_PAD))],
        out_specs=[pl.BlockSpec((2, tm, FF_PAD), lambda i: (0, i, 0)), row(FF_PAD), row(k)],
        out_shape=[SDS((2, s, FF_PAD), BF16), SDS((s, FF_PAD), BF16), SDS((s, k), BF16)],
        compiler_params=_cp())(x, gain, wgu)


def _linear_res(a, w, res, *, cn, name):
    s, k = a.shape
    n = w.shape[1]
    tm = _rows(s)

    def body(a_ref, w_ref, r_ref, y_ref):
        for c in _chunks(n, cn):
            y_ref[:, c] = r_ref[:, c] + _dot(a_ref[...], w_ref[:, c])

    row = lambda w_: pl.BlockSpec((tm, w_), lambda i: (i, 0))
    return _pcall(
        body, name=name, grid=(s // tm,),
        in_specs=[row(k), _resident((k, n)), row(n)], out_specs=row(n),
        out_shape=SDS((s, n), F32), compiler_params=_cp())(a, w, res)


def _linear_res_loss(a, w, res, tgt, *, name):
    s, k = a.shape
    n = w.shape[1]
    tm = _rows(s)

    def body(a_ref, w_ref, r_ref, t_ref, dy_ref, dyb_ref, sq_ref):
        e = r_ref[...] + _dot(a_ref[...], w_ref[...]) - t_ref[...]
        dy = e * (1.0 / n)
        dy_ref[...] = dy
        dyb_ref[...] = dy.astype(BF16)

        @pl.when(pl.program_id(0) == 0)
        def _():
            sq_ref[...] = jnp.zeros_like(sq_ref)
        sq_ref[...] += jnp.sum(e * e, axis=0, keepdims=True)

    row = lambda w_: pl.BlockSpec((tm, w_), lambda i: (i, 0))
    return _pcall(
        body, name=name, grid=(s // tm,),
        in_specs=[row(k), _resident((k, n)), row(n), row(n)],
        out_specs=[row(n), row(n), pl.BlockSpec((1, n), lambda i: (0, 0))],
        out_shape=[SDS((s, n), F32), SDS((s, n), BF16), SDS((1, n), F32)],
        compiler_params=_cp())(a, w, res, tgt)


def _linear_nt(g, w, *, cn, name):
    s, k = g.shape
    n = w.shape[0]
    tm = _rows(s)

    def body(g_ref, w_ref, y_ref):
        for c in _chunks(n, cn):
            y_ref[:, c] = _dot_nt(g_ref[...], w_ref[c, :]).astype(BF16)

    row = lambda w_: pl.BlockSpec((tm, w_), lambda i: (i, 0))
    return _pcall(
        body, name=name, grid=(s // tm,),
        in_specs=[row(k), _resident((n, k))], out_specs=row(n),
        out_shape=SDS((s, n), BF16), compiler_params=_cp())(g, w)


def _swiglu_bwd(dyb, wd, gu, *, name):
    s, n = dyb.shape
    tm = _rows(s)

    def body(dy_ref, wd_ref, gu_ref, dgu_ref):
        for c in _chunks(FF_PAD, FF_TILE):
            da = _dot_nt(dy_ref[...], wd_ref[c, :])
            g = gu_ref[0, :, c].astype(F32)
            u = gu_ref[1, :, c].astype(F32)
            sg = jax.nn.sigmoid(g)
            dgu_ref[0, :, c] = (da * u * (sg * (1.0 + g * (1.0 - sg)))).astype(BF16)
            dgu_ref[1, :, c] = (da * (g * sg)).astype(BF16)

    half = pl.BlockSpec((2, tm, FF_PAD), lambda i: (0, i, 0))
    return _pcall(
        body, name=name, grid=(s // tm,),
        in_specs=[pl.BlockSpec((tm, n), lambda i: (i, 0)), _resident((FF_PAD, n)), half],
        out_specs=half, out_shape=SDS((2, s, FF_PAD), BF16), compiler_params=_cp())(dyb, wd, gu)


def _linear_nt_normbwd(g, w, x, dres, gain, *, name, grads=None):
    d, k = w.shape
    s = x.shape[0]
    tm = _rows(s)
    names = tuple(grads or ())
    nr = len(names)

    def body(*refs):
        g_ref, w_ref, x_ref, dr_ref, gn_ref = refs[:5]
        dx_ref, dxb_ref, dg_ref = refs[5 + nr:8 + nr]
        if nr:
            copies = _exchange_copies(names, refs[5:5 + nr], refs[8 + nr:8 + 2 * nr], *refs[8 + 2 * nr:])

        @pl.when(pl.program_id(0) == 0)
        def _():
            dg_ref[...] = jnp.zeros_like(dg_ref)
            if nr:
                _start(copies)

        if g.ndim == 3:
            dxn = _dot_nt(g_ref[0], w_ref[:, :k // 2]) + _dot_nt(g_ref[1], w_ref[:, k // 2:])
        else:
            dxn = _dot_nt(g_ref[...], w_ref[...])
        xv = x_ref[...]
        dx, dgain = _norm_bwd(xv, _rms_r(xv), gn_ref[...], dxn)
        out = dr_ref[...] + dx
        dx_ref[...] = out
        dxb_ref[...] = out.astype(BF16)
        dg_ref[...] += dgain

        if nr:
            @pl.when(pl.program_id(0) == s // tm - 1)
            def _():
                _finish(copies)

    row = pl.BlockSpec((tm, d), lambda i: (i, 0))
    g_spec = (pl.BlockSpec((2, tm, k // 2), lambda i: (0, i, 0)) if g.ndim == 3
              else pl.BlockSpec((tm, k), lambda i: (i, 0)))
    outs = _pcall(
        body, name=name, grid=(s // tm,),
        in_specs=[g_spec, _resident((d, k)), row, row, _resident((1, d))] + [HBM_SPEC] * nr,
        out_specs=[row, row, pl.BlockSpec((1, d), lambda i: (0, 0))] + [HBM_SPEC] * nr,
        out_shape=[SDS((s, d), F32), SDS((s, d), BF16), SDS((1, d), F32)]
        + [SDS((N_DEV,) + _shard_shape(n), BF16) for n in names],
        scratch_shapes=_comm_sems(nr) if nr else [],
        compiler_params=_cp())(g, w, x, dres, gain, *[grads[n] for n in names])
    return (outs[0], outs[1], outs[2], dict(zip(names, outs[3:]))) if nr else tuple(outs)


def _dw_tn(x, g, *, tkw, tn, name):
    s, kw = x.shape
    halves = g.ndim == 3
    n = 2 * g.shape[2] if halves else g.shape[1]
    ts = _rows(s)
    ns = s // ts
    per_half = n // 2 // tn

    def body(x_ref, g_ref, o_ref, acc_ref):
        ss = pl.program_id(2)

        @pl.when(ss == 0)
        def _():
            acc_ref[...] = jnp.zeros_like(acc_ref)

        acc_ref[...] += _dot_tn(x_ref[...], g_ref[...])

        @pl.when(ss == ns - 1)
        def _():
            o_ref[...] = acc_ref[...].astype(BF16)

    g_spec = (pl.BlockSpec((None, ts, tn), lambda a, b, ss: (b // per_half, ss, b % per_half)) if halves
              else pl.BlockSpec((ts, tn), lambda a, b, ss: (ss, b)))
    return _pcall(
        body, name=name, grid=(kw // tkw, n // tn, ns),
        in_specs=[pl.BlockSpec((ts, tkw), lambda a, b, ss: (ss, a)), g_spec],
        out_specs=pl.BlockSpec((tkw, tn), lambda a, b, ss: (a, b)),
        out_shape=SDS((kw, n), BF16),
        scratch_shapes=[pltpu.VMEM((tkw, tn), F32)], compiler_params=_cp())(x, g)


def _qk_prep(proj, cos_t, sin_t, qg, kg, *, name):
    s = proj.shape[0]
    tm = _rows(s)
    nqh = Q_W // HEAD_DIM

    def body(q_ref, k_ref, v_ref, c_ref, s_ref, qg_ref, kg_ref, qo_ref, ko_ref, vo_ref):
        c, sn = c_ref[...], s_ref[...]
        for h in range(nqh):
            sl = slice(h * HEAD_DIM, (h + 1) * HEAD_DIM)
            y = q_ref[:, sl]
            qo_ref[:, sl] = (_rope(y * _rms_r(y) * qg_ref[...], c, sn) * SCALE).astype(BF16)
        for h in range(ATT_HEADS):
            sl = slice(h * HEAD_DIM, (h + 1) * HEAD_DIM)
            y = k_ref[:, sl]
            ko_ref[:, sl] = _rope(y * _rms_r(y) * kg_ref[...], c, sn).astype(BF16)
        vo_ref[...] = v_ref[...].astype(BF16)

    row = lambda w, j: pl.BlockSpec((tm, w), lambda i: (i, j))
    one = pl.BlockSpec((1, HEAD_DIM), lambda i: (0, 0))
    return _pcall(
        body, name=name, grid=(s // tm,),
        in_specs=[row(Q_W, 0), row(KV_W, 3), row(KV_W, 4), row(HEAD_DIM, 0), row(HEAD_DIM, 0), one, one],
        out_specs=[row(Q_W, 0), row(KV_W, 0), row(KV_W, 0)],
        out_shape=[SDS((s, Q_W), BF16), SDS((s, KV_W), BF16), SDS((s, KV_W), BF16)],
        compiler_params=_cp())(proj, proj, proj, cos_t, sin_t, qg, kg)


def _qk_prep_bwd(proj, dq, dk, dv, du, cos_t, sin_t, qg, kg, *, name):
    s = proj.shape[0]
    tm = _rows(s)

    def body(q_ref, k_ref, dq_ref, dk_ref, dv_ref, du_ref, c_ref, s_ref, qg_ref, kg_ref, dp_ref, dqg_ref, dkg_ref):
        c, sn = c_ref[...], s_ref[...]

        @pl.when(pl.program_id(0) == 0)
        def _():
            dqg_ref[...] = jnp.zeros_like(dqg_ref)
            dkg_ref[...] = jnp.zeros_like(dkg_ref)

        dqg = jnp.zeros((1, HEAD_DIM), F32)
        for g in range(N_GROUPS):
            for h in range(ATT_HEADS):
                sl = slice(h * HEAD_DIM, (h + 1) * HEAD_DIM)
                col = slice(g * KV_W + h * HEAD_DIM, g * KV_W + (h + 1) * HEAD_DIM)
                y = q_ref[:, col]
                dn = _rope_bwd(dq_ref[g, :, sl].astype(F32) * SCALE, c, sn)
                dy, dgain = _norm_bwd(y, _rms_r(y), qg_ref[...], dn)
                dp_ref[:, col] = dy.astype(BF16)
                dqg = dqg + dgain
        dqg_ref[...] += dqg

        dkg = jnp.zeros((1, HEAD_DIM), F32)
        for h in range(ATT_HEADS):
            sl = slice(h * HEAD_DIM, (h + 1) * HEAD_DIM)
            y = k_ref[:, sl]
            dn = _rope_bwd(dk_ref[:, sl], c, sn)
            dy, dgain = _norm_bwd(y, _rms_r(y), kg_ref[...], dn)
            dp_ref[:, Q_W + h * HEAD_DIM:Q_W + (h + 1) * HEAD_DIM] = dy.astype(BF16)
            dkg = dkg + dgain
        dkg_ref[...] += dkg

        dp_ref[:, Q_W + KV_W:Q_W + 2 * KV_W] = dv_ref[...].astype(BF16)
        dp_ref[:, Q_W + 2 * KV_W:] = du_ref[...].astype(BF16)

    row = lambda w, j: pl.BlockSpec((tm, w), lambda i: (i, j))
    one = pl.BlockSpec((1, HEAD_DIM), lambda i: (0, 0))
    return _pcall(
        body, name=name, grid=(s // tm,),
        in_specs=[row(Q_W, 0), row(KV_W, 3), pl.BlockSpec((N_GROUPS, tm, KV_W), lambda i: (0, i, 0))]
        + [row(KV_W, 0)] * 3 + [row(HEAD_DIM, 0), row(HEAD_DIM, 0), one, one],
        out_specs=[row(IN_W, 0), one, one],
        out_shape=[SDS((s, IN_W), BF16), SDS((1, HEAD_DIM), F32), SDS((1, HEAD_DIM), F32)],
        compiler_params=_cp())(proj, proj, dq, dk, dv, du, cos_t, sin_t, qg, kg)


ATT_ROWS = 16 * BLOCK


def _sub(ref, start, d, size=BLOCK):
    return ref[pl.ds(start, size, stride=d), :] if d > 1 else ref[pl.ds(start, size), :]


def _sub_set(ref, start, d, val):
    if d > 1:
        ref[pl.ds(start, BLOCK, stride=d), :] = val
    else:
        ref[pl.ds(start, BLOCK), :] = val


def _band_masks():
    row = lax.broadcasted_iota(jnp.int32, (BLOCK, BLOCK), 0)
    col = lax.broadcasted_iota(jnp.int32, (BLOCK, BLOCK), 1)
    return col <= row, col >= row


def _eye():
    row = lax.broadcasted_iota(jnp.int32, (BLOCK, BLOCK), 0)
    col = lax.broadcasted_iota(jnp.int32, (BLOCK, BLOCK), 1)
    return jnp.where(row == col, 1.0, 0.0).astype(BF16)


def _transpose_mxu(eye, a):
    return _dot_nt(eye, a).astype(BF16)


def _attn_fwd(q_rot, k_rot, v, shards, *, name):
    s = q_rot.shape[0]
    rr = ATT_ROWS
    nblk = s // rr
    names = tuple(shards)
    nr = len(names)

    def body(*refs):
        q0, q1, q2, kp, kc, vp, vc = refs[:7]
        mix_ref, lse_ref = refs[7 + nr:9 + nr]
        qs, ks, vs, os_, ls, zero_ref, send_sems, recv_sems, local_sems, zero_sems = refs[9 + 2 * nr:]
        h, n = pl.program_id(0), pl.program_id(1)
        w_full = refs[9 + nr:9 + 2 * nr]
        copies = _gather_copies(names, refs[7:7 + nr], w_full, send_sems, recv_sems, local_sems)
        pads = _down_pads(w_full[names.index("w_down")], zero_ref, zero_sems)

        @pl.when((h == 0) & (n == 0))
        def _():
            zero_ref[...] = jnp.zeros_like(zero_ref)
            _start(copies)
            for cp in pads:
                cp.start()

        for g, q_ref in enumerate((q0, q1, q2)):
            qs[g] = q_ref[...].astype(F32)
        ks[:rr] = kp[...].astype(F32)
        ks[rr:] = kc[...].astype(F32)
        vs[:rr] = vp[...].astype(F32)
        vs[rr:] = vc[...].astype(F32)
        m_cur, m_band = _band_masks()
        mask_in = jnp.concatenate([m_band, m_cur], axis=1)
        mask_first = jnp.concatenate([m_band & (n > 0), m_cur], axis=1)
        ones = jnp.ones((2 * BLOCK, HEAD_DIM), BF16)
        pieces = [(g, d, j * BLOCK * d + r, j) for g, d in enumerate(DILATIONS) for r in range(d)
                  for j in range(rr // (BLOCK * d))]

        def scores(piece):
            g, d, base, j = piece
            q = _sub(qs.at[g], base, d).astype(BF16)
            k2 = _sub(ks, rr + base - BLOCK * d, d, 2 * BLOCK).astype(BF16)
            return jnp.where(mask_first if j == 0 else mask_in, _dot_nt(q, k2), NEG_INF)

        sc = scores(pieces[0])
        for i, (g, d, base, j) in enumerate(pieces):
            cur = sc
            if i + 1 < len(pieces):
                sc = scores(pieces[i + 1])
            m = jnp.max(cur, axis=-1, keepdims=True)
            p = jnp.exp(cur - m).astype(BF16)
            v2 = _sub(vs, rr + base - BLOCK * d, d, 2 * BLOCK).astype(BF16)
            acc_l = _dot(p, jnp.concatenate([v2, ones], axis=1))
            l = acc_l[:, HEAD_DIM:]
            _sub_set(os_.at[g], base, d, acc_l[:, :HEAD_DIM] / l)
            _sub_set(ls.at[g], base, d, m + jnp.log(l))
        for c in _chunks(rr, 2 * BLOCK):
            a, b, cc = ls[0, c, :], ls[1, c, :], ls[2, c, :]
            m = jnp.maximum(jnp.maximum(a, b), cc)
            wa, wb, wc = jnp.exp(a - m), jnp.exp(b - m), jnp.exp(cc - m)
            den = wa + wb + wc
            mix_ref[c, :] = ((wa * os_[0, c, :] + wb * os_[1, c, :] + wc * os_[2, c, :]) / den).astype(BF16)
            lse_ref[c, :] = m + jnp.log(den)

        @pl.when((h == ATT_HEADS - 1) & (n == nblk - 1))
        def _():
            _finish(copies)
            for cp in pads:
                cp.wait()

    blk = lambda f: pl.BlockSpec((rr, HEAD_DIM), f)
    prv = lambda n: jnp.maximum(n - 1, 0)
    outs = _pcall(
        body, name=name, grid=(ATT_HEADS, nblk),
        in_specs=[blk(lambda h, n, g=g: (n, g * ATT_HEADS + h)) for g in range(N_GROUPS)]
        + [blk(lambda h, n: (prv(n), h)), blk(lambda h, n: (n, h))] * 2 + [HBM_SPEC] * nr,
        out_specs=[blk(lambda h, n: (n, h)), blk(lambda h, n: (n, h))] + [HBM_SPEC] * nr,
        out_shape=[SDS((s, KV_W + POOL_W), BF16), SDS((s, KV_W), F32)] + [SDS(FULL[w][0], BF16) for w in names],
        scratch_shapes=[pltpu.VMEM((N_GROUPS, rr, HEAD_DIM), F32), pltpu.VMEM((2 * rr, HEAD_DIM), F32),
                        pltpu.VMEM((2 * rr, HEAD_DIM), F32), pltpu.VMEM((N_GROUPS, rr, HEAD_DIM), F32),
                        pltpu.VMEM((N_GROUPS, rr, HEAD_DIM), F32), pltpu.VMEM((FF_TILE - FF_SHARD, D_MODEL), BF16)]
        + _comm_sems(nr) + [pltpu.SemaphoreType.DMA((FF_PAD // FF_TILE,))],
        compiler_params=_cp())(q_rot, q_rot, q_rot, k_rot, k_rot, v, v, *[shards[w] for w in names])
    return outs[0], outs[1], dict(zip(names, outs[2:]))


def _attn_bwd(q_rot, k_rot, v, mix, dmix, lse, grads, *, name):
    s = q_rot.shape[0]
    rr = ATT_ROWS
    nblk = s // rr
    names = tuple(grads)
    nr = len(names)

    def body(*refs):
        q0, q1, q2, qx0, qx1, qx2, kp, kc, vp, vc, do_c, do_x, o_c, o_x, l_c, l_x = refs[:16]
        dq_ref, dk_ref, dv_ref = refs[16 + nr:19 + nr]
        qs, ks, vs, dos, lss, dls, dqs, dks, dvs, send_sems, recv_sems, local_sems = refs[19 + 2 * nr:]
        h, n = pl.program_id(0), pl.program_id(1)
        copies = _exchange_copies(names, refs[16:16 + nr], refs[19 + nr:19 + 2 * nr], send_sems, recv_sems, local_sems)

        @pl.when((h == 0) & (n == 0))
        def _():
            _start(copies)

        for g, (qc_ref, qx_ref) in enumerate(((q0, qx0), (q1, qx1), (q2, qx2))):
            qs[g, :rr] = qc_ref[...].astype(F32)
            qs[g, rr:] = qx_ref[...].astype(F32)
        ks[:rr] = kp[...].astype(F32)
        ks[rr:] = kc[...].astype(F32)
        vs[:rr] = vp[...].astype(F32)
        vs[rr:] = vc[...].astype(F32)
        lss[:rr] = l_c[...]
        lss[rr:] = l_x[...]
        for half, (d_ref, o_ref) in enumerate(((do_c, o_c), (do_x, o_x))):
            for c in _chunks(rr, 2 * BLOCK):
                cs = slice(half * rr + c.start, half * rr + c.stop)
                dof = d_ref[c, :].astype(F32)
                dos[cs, :] = dof
                dls[cs, :] = jnp.broadcast_to(jnp.sum(dof * o_ref[c, :].astype(F32), axis=-1, keepdims=True),
                                              (2 * BLOCK, HEAD_DIM))
        m_cur, m_band = _band_masks()
        mask_in = jnp.concatenate([m_cur, m_band], axis=0)
        mask_last = jnp.concatenate([m_cur, m_band & (n + 1 < nblk)], axis=0)
        m_first = m_band & (n > 0)
        eye = _eye()
        pieces = [(g, d, j * BLOCK * d + r, j, rr // (BLOCK * d)) for g, d in enumerate(DILATIONS) for r in range(d)
                  for j in range(rr // (BLOCK * d))]

        def front(piece):
            g, d, base, _, _ = piece
            q2 = _sub(qs.at[g], base, d, 2 * BLOCK).astype(BF16)
            do2 = _sub(dos, base, d, 2 * BLOCK).astype(BF16)
            k = _sub(ks, rr + base, d).astype(BF16)
            vv = _sub(vs, rr + base, d).astype(BF16)
            return q2, do2, k, _dot_nt(q2, k), _dot_nt(do2, vv)

        nxt_front = front(pieces[0])
        dq_acc = None
        for i, (g, d, base, j, nsub) in enumerate(pieces):
            q2, do2, k, s2, dp2 = nxt_front
            if j == 0:
                kp_ = _sub(ks, rr + base - BLOCK * d, d).astype(BF16)
                p0 = jnp.where(m_first, jnp.exp(_dot_nt(q2[:BLOCK], kp_) - _sub(lss, base, d)), 0.0)
                ds0 = p0 * (_dot_nt(do2[:BLOCK], _sub(vs, rr + base - BLOCK * d, d).astype(BF16)) - _sub(dls, base, d))
                dq_acc = _dot(ds0.astype(BF16), kp_)
            if i + 1 < len(pieces):
                nxt_front = front(pieces[i + 1])
            p2 = jnp.where(mask_last if j + 1 == nsub else mask_in,
                           jnp.exp(s2 - _sub(lss, base, d, 2 * BLOCK)), 0.0)
            ds2 = (p2 * (dp2 - _sub(dls, base, d, 2 * BLOCK))).astype(BF16)
            dq2 = _dot(ds2, k)
            _sub_set(dqs.at[g], base, d, dq_acc + dq2[:BLOCK])
            dq_acc = dq2[BLOCK:]
            dk = _dot(_transpose_mxu(eye, ds2), q2)
            dv = _dot(_transpose_mxu(eye, p2.astype(BF16)), do2)
            if g == 0:
                _sub_set(dks, base, d, dk)
                _sub_set(dvs, base, d, dv)
            else:
                _sub_set(dks, base, d, _sub(dks, base, d) + dk)
                _sub_set(dvs, base, d, _sub(dvs, base, d) + dv)
        for g in range(N_GROUPS):
            dq_ref[g] = dqs[g].astype(BF16)
        dk_ref[...] = dks[...]
        dv_ref[...] = dvs[...]

        @pl.when((h == ATT_HEADS - 1) & (n == nblk - 1))
        def _():
            _finish(copies)

    blk = lambda f: pl.BlockSpec((rr, HEAD_DIM), f)
    prv = lambda n: jnp.maximum(n - 1, 0)
    nxt = lambda n: jnp.minimum(n + 1, nblk - 1)
    cur_kv = blk(lambda h, n: (n, h))
    outs = _pcall(
        body, name=name, grid=(ATT_HEADS, nblk),
        in_specs=[blk(lambda h, n, g=g: (n, g * ATT_HEADS + h)) for g in range(N_GROUPS)]
        + [blk(lambda h, n, g=g: (nxt(n), g * ATT_HEADS + h)) for g in range(N_GROUPS)]
        + [blk(lambda h, n: (prv(n), h)), cur_kv] * 2
        + [cur_kv, blk(lambda h, n: (nxt(n), h))] * 3 + [HBM_SPEC] * nr,
        out_specs=[pl.BlockSpec((N_GROUPS, rr, HEAD_DIM), lambda h, n: (0, n, h)), cur_kv, cur_kv] + [HBM_SPEC] * nr,
        out_shape=[SDS((N_GROUPS, s, KV_W), BF16), SDS((s, KV_W), F32), SDS((s, KV_W), F32)]
        + [SDS((N_DEV,) + _shard_shape(w), BF16) for w in names],
        scratch_shapes=[pltpu.VMEM((N_GROUPS, 2 * rr, HEAD_DIM), F32)] + [pltpu.VMEM((2 * rr, HEAD_DIM), F32)] * 5
        + [pltpu.VMEM((N_GROUPS, rr, HEAD_DIM), F32), pltpu.VMEM((rr, HEAD_DIM), F32), pltpu.VMEM((rr, HEAD_DIM), F32)]
        + _comm_sems(nr),
        compiler_params=_cp())(q_rot, q_rot, q_rot, q_rot, q_rot, q_rot, k_rot, k_rot, v, v, dmix, dmix, mix, mix,
                               lse, lse, *[grads[w] for w in names])
    return outs[0], outs[1], outs[2], dict(zip(names, outs[3:]))


def _pool_d(u_ref, halo_ref, i, tm):
    halo = jnp.where(i > 0, halo_ref[...], 0.0)
    t = i * tm + lax.broadcasted_iota(jnp.int32, (tm, 1), 0)
    out = []
    for g, w in enumerate(POOL_WINDOWS):
        sl = slice(g * HEAD_DIM, (g + 1) * HEAD_DIM)
        u = u_ref[:, sl]
        acc = jnp.concatenate([halo[:, sl], u], axis=0)
        sh = 1
        while sh < w:
            acc = acc + pltpu.roll(acc, sh, 0)
            sh *= 2
        cnt = jnp.minimum(t + 1, w).astype(F32)
        out.append(acc[HALO:, :] / cnt - u)
    return out


def _pool_fwd(proj, mix, pool_w, pool_scale, *, name):
    s = proj.shape[0]
    tm = _rows(s)
    ucol = (IN_W - POOL_W) // POOL_W

    def body(u_ref, halo_ref, mix_in, w_ref, sc_ref, o_ref):
        del mix_in
        dd = _pool_d(u_ref, halo_ref, pl.program_id(0), tm)
        for g in range(len(POOL_WINDOWS)):
            sl = slice(g * HEAD_DIM, (g + 1) * HEAD_DIM)
            y = _dot(dd[g].astype(BF16), w_ref[g].astype(BF16))
            o_ref[:, sl] = (y * sc_ref[:, sl]).astype(BF16)

    return _pcall(
        body, name=name, grid=(s // tm,),
        in_specs=[pl.BlockSpec((tm, POOL_W), lambda i: (i, ucol)),
                  pl.BlockSpec((HALO, POOL_W), lambda i: (jnp.maximum(i * (tm // HALO) - 1, 0), ucol)),
                  pl.BlockSpec(memory_space=pl.ANY),
                  pl.BlockSpec((len(POOL_WINDOWS), HEAD_DIM, HEAD_DIM), lambda i: (0, 0, 0)),
                  pl.BlockSpec((1, POOL_W), lambda i: (0, 0))],
        out_specs=pl.BlockSpec((tm, POOL_W), lambda i: (i, 1)),
        out_shape=SDS(mix.shape, BF16), input_output_aliases={2: 0},
        compiler_params=_cp())(proj, proj, mix, pool_w, pool_scale)


def _pool_bwd(proj, dmix, pool_w, pool_scale, *, name):
    s = proj.shape[0]
    tm = _rows(s)
    nblk = s // tm
    ucol = (IN_W - POOL_W) // POOL_W
    ng = len(POOL_WINDOWS)

    def body(u_ref, halo_ref, dp_ref, dpn_ref, w_ref, sc_ref, du_ref, dw_ref, dsc_ref):
        i = pl.program_id(0)

        @pl.when(i == 0)
        def _():
            dw_ref[...] = jnp.zeros_like(dw_ref)
            dsc_ref[...] = jnp.zeros_like(dsc_ref)

        dd = _pool_d(u_ref, halo_ref, i, tm)
        t = i * tm + lax.broadcasted_iota(jnp.int32, (tm, 1), 0)
        dpn = jnp.where(i + 1 < nblk, dpn_ref[...].astype(F32), 0.0)
        for g, w in enumerate(POOL_WINDOWS):
            sl = slice(g * HEAD_DIM, (g + 1) * HEAD_DIM)
            wg = w_ref[g].astype(BF16)
            db = dd[g].astype(BF16)
            dp = dp_ref[:, sl].astype(F32)
            dsc_ref[:, sl] += jnp.sum(dp * _dot(db, wg), axis=0, keepdims=True)
            dy = (dp * sc_ref[:, sl]).astype(BF16)
            dw_ref[g] += _dot_tn(db, dy)
            g_d = _dot_nt(dy, wg)
            g_dn = _dot_nt((dpn[:, sl] * sc_ref[:, sl]).astype(BF16), wg)
            cnt = jnp.minimum(t + 1, w).astype(F32)
            acc = jnp.concatenate([g_d / cnt, g_dn * (1.0 / w)], axis=0)
            sh = 1
            while sh < w:
                acc = acc + pltpu.roll(acc, tm + HALO - sh, 0)
                sh *= 2
            du_ref[:, sl] = acc[:tm, :] - g_d

    nh = s // HALO
    return _pcall(
        body, name=name, grid=(nblk,),
        in_specs=[pl.BlockSpec((tm, POOL_W), lambda i: (i, ucol)),
                  pl.BlockSpec((HALO, POOL_W), lambda i: (jnp.maximum(i * (tm // HALO) - 1, 0), ucol)),
                  pl.BlockSpec((tm, POOL_W), lambda i: (i, 1)),
                  pl.BlockSpec((HALO, POOL_W), lambda i: (jnp.minimum((i + 1) * (tm // HALO), nh - 1), 1)),
                  pl.BlockSpec((ng, HEAD_DIM, HEAD_DIM), lambda i: (0, 0, 0)),
                  pl.BlockSpec((1, POOL_W), lambda i: (0, 0))],
        out_specs=[pl.BlockSpec((tm, POOL_W), lambda i: (i, 0)),
                   pl.BlockSpec((ng, HEAD_DIM, HEAD_DIM), lambda i: (0, 0, 0)),
                   pl.BlockSpec((1, POOL_W), lambda i: (0, 0))],
        out_shape=[SDS((s, POOL_W), F32), SDS((ng, HEAD_DIM, HEAD_DIM), F32), SDS((1, POOL_W), F32)],
        compiler_params=_cp())(proj, proj, dmix, dmix, pool_w, pool_scale)


def _mem_fwd(mem, mem_g, wckv, ck_g, *, name):
    def body(m_ref, g_ref, w_ref, kg_ref, mn_ref, ckr_ref, ckn_ref, cv_ref):
        mv = m_ref[...]
        mn = (mv * _rms_r(mv) * g_ref[...]).astype(BF16)
        mn_ref[...] = mn
        ckv = _dot(mn, w_ref[...])
        ckr_ref[...] = ckv[:, :X_W]
        cv_ref[...] = ckv[:, X_W:].astype(BF16)
        for h in range(ATT_HEADS):
            sl = slice(h * HEAD_DIM, (h + 1) * HEAD_DIM)
            y = ckv[:, sl]
            ckn_ref[:, sl] = (y * _rms_r(y) * kg_ref[...]).astype(BF16)

    return _pcall(
        body, name=name,
        out_shape=[SDS((N_MEM, D_MODEL), BF16), SDS((N_MEM, X_W), F32), SDS((N_MEM, X_W), BF16),
                   SDS((N_MEM, X_W), BF16)],
        compiler_params=_cp())(mem, mem_g, wckv, ck_g)


def _cross_q(cq_ref, g_ref, sl):
    y = cq_ref[:, sl]
    r = _rms_r(y)
    return y, r, y * r * g_ref[...] * SCALE


def _cross_fwd(cq_raw, ck_n, cv, cq_g, *, name):
    s = cq_raw.shape[0]
    tm = _rows(s)

    def body(cq_ref, k_ref, v_ref, g_ref, o_ref):
        for h in range(ATT_HEADS):
            sl = slice(h * HEAD_DIM, (h + 1) * HEAD_DIM)
            _, _, qn = _cross_q(cq_ref, g_ref, sl)
            sc = _dot_nt(qn.astype(BF16), k_ref[:, sl])
            p = jnp.exp(sc - jnp.max(sc, axis=-1, keepdims=True))
            p = p / jnp.sum(p, axis=-1, keepdims=True)
            o_ref[:, sl] = _dot(p.astype(BF16), v_ref[:, sl]).astype(BF16)

    full = lambda a: pl.BlockSpec(a.shape, lambda i: (0, 0))
    return _pcall(
        body, name=name, grid=(s // tm,),
        in_specs=[pl.BlockSpec((tm, X_W), lambda i: (i, 0)), full(ck_n), full(cv), full(cq_g)],
        out_specs=pl.BlockSpec((tm, X_W), lambda i: (i, 0)),
        out_shape=SDS((s, X_W), BF16), compiler_params=_cp())(cq_raw, ck_n, cv, cq_g)


def _cross_bwd(d_o, cq_raw, ck_n, cv, cq_g, *, name):
    s = cq_raw.shape[0]
    tm = _rows(s)

    def body(do_ref, cq_ref, k_ref, v_ref, g_ref, dcq_ref, dk_ref, dv_ref, dg_ref):
        @pl.when(pl.program_id(0) == 0)
        def _():
            dk_ref[...] = jnp.zeros_like(dk_ref)
            dv_ref[...] = jnp.zeros_like(dv_ref)
            dg_ref[...] = jnp.zeros_like(dg_ref)

        dg = jnp.zeros((1, HEAD_DIM), F32)
        for h in range(ATT_HEADS):
            sl = slice(h * HEAD_DIM, (h + 1) * HEAD_DIM)
            y, r, qn = _cross_q(cq_ref, g_ref, sl)
            qb = qn.astype(BF16)
            do = do_ref[:, sl]
            sc = _dot_nt(qb, k_ref[:, sl])
            p = jnp.exp(sc - jnp.max(sc, axis=-1, keepdims=True))
            p = p / jnp.sum(p, axis=-1, keepdims=True)
            dv_ref[:, sl] += _dot_tn(p.astype(BF16), do)
            dp = _dot_nt(do, v_ref[:, sl])
            ds = (p * (dp - jnp.sum(dp * p, axis=-1, keepdims=True))).astype(BF16)
            dk_ref[:, sl] += _dot_tn(ds, qb)
            dn = _dot(ds, k_ref[:, sl]) * SCALE
            dy, dgain = _norm_bwd(y, r, g_ref[...], dn)
            dcq_ref[:, sl] = dy.astype(BF16)
            dg = dg + dgain
        dg_ref[...] += dg

    full = lambda a: pl.BlockSpec(a.shape, lambda i: (0, 0))
    row = pl.BlockSpec((tm, X_W), lambda i: (i, 0))
    return _pcall(
        body, name=name, grid=(s // tm,),
        in_specs=[row, row, full(ck_n), full(cv), full(cq_g)],
        out_specs=[row, pl.BlockSpec((N_MEM, X_W), lambda i: (0, 0)), pl.BlockSpec((N_MEM, X_W), lambda i: (0, 0)),
                   pl.BlockSpec((1, HEAD_DIM), lambda i: (0, 0))],
        out_shape=[SDS((s, X_W), BF16), SDS((N_MEM, X_W), F32), SDS((N_MEM, X_W), F32), SDS((1, HEAD_DIM), F32)],
        compiler_params=_cp())(d_o, cq_raw, ck_n, cv, cq_g)


def _mem_bwd(dck_n, dcv, ck_raw, memn, mem, wckv, mem_g, ck_g, *, name):
    def body(dk_ref, dv_ref, ckr_ref, mn_ref, m_ref, w_ref, mg_ref, kg_ref, dw_ref, dmg_ref, dkg_ref, dckv_s):
        dkg = jnp.zeros((1, HEAD_DIM), F32)
        for h in range(ATT_HEADS):
            sl = slice(h * HEAD_DIM, (h + 1) * HEAD_DIM)
            y = ckr_ref[:, sl]
            dy, dgain = _norm_bwd(y, _rms_r(y), kg_ref[...], dk_ref[:, sl])
            dckv_s[:, sl] = dy.astype(BF16)
            dkg = dkg + dgain
        dkg_ref[...] = dkg
        dckv_s[:, X_W:] = dv_ref[...].astype(BF16)
        dckv = dckv_s[...]
        dw_ref[...] = _dot_tn(mn_ref[...], dckv).astype(BF16)
        dmn = _dot_nt(dckv, w_ref[...])
        mv = m_ref[...]
        dmg_ref[...] = jnp.sum(dmn * mv * _rms_r(mv), axis=0, keepdims=True)

    return _pcall(
        body, name=name,
        out_shape=[SDS((D_MODEL, 2 * X_W), BF16), SDS((1, D_MODEL), F32), SDS((1, HEAD_DIM), F32)],
        scratch_shapes=[pltpu.VMEM((N_MEM, 2 * X_W), BF16)],
        compiler_params=_cp())(dck_n, dcv, ck_raw, memn, mem, wckv, mem_g, ck_g)


def _rope_tables(pos):
    inv_freq = ROPE_THETA ** (-jnp.arange(0, ROT_DIM, 2, dtype=F32) / ROT_DIM)
    ang = pos.astype(F32)[:, None] * inv_freq
    cos, sin = jnp.cos(ang), jnp.sin(ang)
    s = pos.shape[0]
    rest = HEAD_DIM - ROT_DIM
    cos_t = jnp.concatenate([cos, cos, jnp.ones((s, rest), F32)], axis=1)
    sin_t = jnp.concatenate([-sin, sin, jnp.zeros((s, rest), F32)], axis=1)
    return cos_t, sin_t


def _local_step(x, mem, pos, tgt, w_in, shards, sm):
    cos_t, sin_t = _rope_tables(pos)
    wb = {"w_in": w_in}

    proj, xn = _norm_linear(x, sm["mix_norm_g"], wb["w_in"], cn=768, out_dtype=F32, name="fwd_in_proj")
    q_rot, k_rot, v = _qk_prep(proj, cos_t, sin_t, sm["q_norm_g"], sm["k_norm_g"], name="fwd_qk_prep")
    mix, lse_all, rest = _attn_fwd(q_rot, k_rot, v, shards, name="fwd_attn")
    wb.update(rest)
    mix = _pool_fwd(proj, mix, sm["pool_w"], sm["pool_scale"], name="fwd_pool")
    h1 = _linear_res(mix, wb["w_out"], x, cn=512, name="fwd_out_proj")
    cq_raw, hn = _norm_linear(h1, sm["cross_norm_g"], wb["w_cq"], cn=512, out_dtype=F32, name="fwd_cq_proj")
    memn, ck_raw, ck_n, cv = _mem_fwd(mem, sm["mem_norm_g"], wb["w_ckv"], sm["ck_norm_g"], name="fwd_mem")
    co = _cross_fwd(cq_raw, ck_n, cv, sm["cq_norm_g"], name="fwd_cross")
    h2 = _linear_res(co, wb["w_co"], h1, cn=512, name="fwd_co_proj")
    gu, act, fn = _norm_linear_swiglu(h2, sm["ffn_norm_g"], wb["w_gate_up"], name="fwd_gate_up")
    dy, dyb, sq = _linear_res_loss(act, wb["w_down"], h2, tgt, name="fwd_down_loss")

    gw = {}
    gs = {}
    dgu = _swiglu_bwd(dyb, wb["w_down"], gu, name="bwd_swiglu")
    gw["w_down"] = _dw_tn(act, dyb, tkw=1024, tn=1024, name="bwd_dw_down")
    dh2, dh2b, gs["ffn_norm_g"] = _linear_nt_normbwd(dgu, wb["w_gate_up"], h2, dy, sm["ffn_norm_g"],
                                                     name="bwd_ffn_in")
    gw["w_gate_up"] = _dw_tn(fn, dgu, tkw=1024, tn=1536, name="bwd_dw_gate_up")

    d_co = _linear_nt(dh2b, wb["w_co"], cn=512, name="bwd_co_proj")
    gw["w_co"] = _dw_tn(co, dh2b, tkw=512, tn=1024, name="bwd_dw_co")
    dcq, dck_n, dcv, gs["cq_norm_g"] = _cross_bwd(d_co, cq_raw, ck_n, cv, sm["cq_norm_g"], name="bwd_cross")
    gw["w_ckv"], gs["mem_norm_g"], gs["ck_norm_g"] = _mem_bwd(dck_n, dcv, ck_raw, memn, mem, wb["w_ckv"],
                                                             sm["mem_norm_g"], sm["ck_norm_g"], name="bwd_mem")
    dh1, dh1b, gs["cross_norm_g"] = _linear_nt_normbwd(dcq, wb["w_cq"], h1, dh2, sm["cross_norm_g"],
                                                       name="bwd_cq_in")
    gw["w_cq"] = _dw_tn(hn, dcq, tkw=1024, tn=512, name="bwd_dw_cq")

    dmix = _linear_nt(dh1b, wb["w_out"], cn=512, name="bwd_out_proj")
    gw["w_out"] = _dw_tn(mix, dh1b, tkw=1024, tn=1024, name="bwd_dw_out")
    du, gs["pool_w"], gs["pool_scale"] = _pool_bwd(proj, dmix, sm["pool_w"], sm["pool_scale"], name="bwd_pool")
    dq, dk, dv, parts = _attn_bwd(q_rot, k_rot, v, mix, dmix, lse_all, gw, name="bwd_attn")
    dproj, gs["q_norm_g"], gs["k_norm_g"] = _qk_prep_bwd(proj, dq, dk, dv, du, cos_t, sin_t, sm["q_norm_g"],
                                                         sm["k_norm_g"], name="bwd_qk_prep")
    gw_in = _dw_tn(xn, dproj, tkw=1024, tn=1536, name="bwd_dw_in")
    dx, _, gs["mix_norm_g"], last = _linear_nt_normbwd(dproj, wb["w_in"], x, dh1, sm["mix_norm_g"],
                                                       name="bwd_in_proj", grads={"w_in": gw_in})
    parts.update(last)
    return sq, dx, parts, gs


SHARDED = ("w_in", "w_out", "w_cq", "w_ckv", "w_co", "w_gate_up", "w_down")
SMALL = ("mix_norm_g", "q_norm_g", "k_norm_g", "pool_w", "pool_scale", "cross_norm_g", "mem_norm_g", "cq_norm_g",
         "ck_norm_g", "ffn_norm_g")
FULL = {
    "w_in": ((D_MODEL, IN_W), 1, IN_W // N_DEV),
    "w_out": ((D_MODEL, D_MODEL), 0, D_MODEL // N_DEV),
    "w_cq": ((D_MODEL, X_W), 0, D_MODEL // N_DEV),
    "w_ckv": ((D_MODEL, 2 * X_W), 0, D_MODEL // N_DEV),
    "w_co": ((X_W, D_MODEL), 1, D_MODEL // N_DEV),
    "w_gate_up": ((D_MODEL, 2 * FF_PAD), 1, FF_TILE),
    "w_down": ((FF_PAD, D_MODEL), 0, DOWN_SHARD),
}


def _shard_shape(name):
    shape, axis, width = FULL[name]
    return tuple(width if a == axis else n for a, n in enumerate(shape))


def _window(ref, name, dev):
    _, axis, width = FULL[name]
    if name == "w_down":
        start = pl.multiple_of((dev // 2) * FF_TILE + (dev % 2) * DOWN_SHARD, HALO)
    else:
        start = pl.multiple_of(dev * width, BLOCK)
    return ref.at[pl.ds(start, width), :] if axis == 0 else ref.at[:, pl.ds(start, width)]


def _mesh_place():
    x, y, c = lax.axis_index("x"), lax.axis_index("y"), lax.axis_index("c")
    return x, y, c, 4 * x + 2 * y + c


def _peer(x, y, c, k):
    px = 1 - x if k & 4 else x
    py = 1 - y if k & 2 else y
    pc = 1 - c if k & 1 else c
    return (px, py, pc), 4 * px + 2 * py + pc


HBM_SPEC = pl.BlockSpec(memory_space=pltpu.HBM)


def _comm_sems(n):
    return [pltpu.SemaphoreType.DMA((n, N_DEV - 1)), pltpu.SemaphoreType.DMA((n, N_DEV - 1)),
            pltpu.SemaphoreType.DMA((n,))]


def _gather_copies(names, ins, outs, send_sems, recv_sems, local_sems):
    x, y, c, me = _mesh_place()
    local, sent, recv = [], [], []
    for wi, name in enumerate(names):
        local.append(pltpu.make_async_copy(ins[wi], _window(outs[wi], name, me), local_sems.at[wi]))
        for k in range(1, N_DEV):
            peer, pidx = _peer(x, y, c, k)
            sems = dict(send_sem=send_sems.at[wi, k - 1], recv_sem=recv_sems.at[wi, k - 1], device_id=peer,
                        device_id_type=MESH)
            sent.append(pltpu.make_async_remote_copy(src_ref=ins[wi], dst_ref=_window(outs[wi], name, me), **sems))
            recv.append(pltpu.make_async_remote_copy(src_ref=ins[wi], dst_ref=_window(outs[wi], name, pidx), **sems))
    return local, sent, recv


def _exchange_copies(names, ins, outs, send_sems, recv_sems, local_sems):
    x, y, c, me = _mesh_place()
    local, sent, recv = [], [], []
    for wi, name in enumerate(names):
        local.append(pltpu.make_async_copy(_window(ins[wi], name, me), outs[wi].at[0], local_sems.at[wi]))
        for k in range(1, N_DEV):
            peer, pidx = _peer(x, y, c, k)
            sems = dict(send_sem=send_sems.at[wi, k - 1], recv_sem=recv_sems.at[wi, k - 1], device_id=peer,
                        device_id_type=MESH)
            sent.append(pltpu.make_async_remote_copy(src_ref=_window(ins[wi], name, pidx), dst_ref=outs[wi].at[k], **sems))
            recv.append(pltpu.make_async_remote_copy(src_ref=_window(ins[wi], name, me), dst_ref=outs[wi].at[k], **sems))
    return local, sent, recv


def _down_pads(down_ref, zero_ref, zero_sems):
    return [pltpu.make_async_copy(zero_ref, down_ref.at[pl.ds(t * FF_TILE + FF_SHARD, FF_TILE - FF_SHARD), :],
                                  zero_sems.at[t]) for t in range(FF_PAD // FF_TILE)]


def _start(copies):
    local, sent, _ = copies
    for cp in local + sent:
        cp.start()


def _finish(copies):
    local, sent, recv = copies
    for cp in recv:
        cp.wait_recv()
    for cp in sent:
        cp.wait_send()
    for cp in local:
        cp.wait()


def _gather_weights(shards):
    names = tuple(shards)
    nw = len(names)

    def body(*refs):
        copies = _gather_copies(names, refs[:nw], refs[nw:2 * nw], *refs[2 * nw:])
        _start(copies)
        _finish(copies)

    outs = _pcall(
        body, name="gather_weights",
        in_specs=[HBM_SPEC] * nw, out_specs=[HBM_SPEC] * nw,
        out_shape=[SDS(FULL[n][0], BF16) for n in names],
        scratch_shapes=_comm_sems(nw))(*[shards[n] for n in names])
    return dict(zip(names, outs))


def _exchange_small(small):
    def body(small_ref, out_ref, send_sems, recv_sems, local_sems):
        x, y, c, me = _mesh_place()
        local = pltpu.make_async_copy(small_ref, out_ref.at[me], local_sems.at[0])
        local.start()
        sent = []
        for k in range(1, N_DEV):
            peer, _ = _peer(x, y, c, k)
            cp = pltpu.make_async_remote_copy(
                src_ref=small_ref, dst_ref=out_ref.at[me], send_sem=send_sems.at[0, k - 1],
                recv_sem=recv_sems.at[0, k - 1], device_id=peer, device_id_type=MESH)
            cp.start()
            sent.append(cp)
        for k in range(1, N_DEV):
            peer, pidx = _peer(x, y, c, k)
            pltpu.make_async_remote_copy(
                src_ref=small_ref, dst_ref=out_ref.at[pidx], send_sem=send_sems.at[0, k - 1],
                recv_sem=recv_sems.at[0, k - 1], device_id=peer, device_id_type=MESH).wait_recv()
        for cp in sent:
            cp.wait_send()
        local.wait()

    return _pcall(
        body, name="exchange_small", in_specs=[HBM_SPEC], out_specs=HBM_SPEC,
        out_shape=SDS((N_DEV,) + small.shape, F32), scratch_shapes=_comm_sems(1))(small)


def _adamw(parts, w, m, v, *, name):
    r, c = w.shape
    tr = r
    for cand in (256, 128, 88):
        if r % cand == 0:
            tr = cand
            break

    def body(p_ref, w_ref, m_ref, v_ref, g_ref, d_ref, mo_ref, vo_ref):
        g = p_ref[0].astype(F32)
        for k in range(1, N_DEV):
            g = g + p_ref[k].astype(F32)
        m_new = ADAM_B1 * m_ref[...] + (1.0 - ADAM_B1) * g
        v_new = ADAM_B2 * v_ref[...] + (1.0 - ADAM_B2) * (g * g)
        m_hat = m_new / (1.0 - ADAM_B1 ** ADAM_STEP)
        v_hat = v_new / (1.0 - ADAM_B2 ** ADAM_STEP)
        g_ref[...] = g
        d_ref[...] = -ADAM_LR * (m_hat / (jnp.sqrt(v_hat) + ADAM_EPS) + ADAM_WD * w_ref[...])
        mo_ref[...] = m_new
        vo_ref[...] = v_new

    row = pl.BlockSpec((tr, c), lambda i: (i, 0))
    return _pcall(
        body, name=name, grid=(r // tr,),
        in_specs=[pl.BlockSpec((N_DEV, tr, c), lambda i: (0, i, 0)), row, row, row],
        out_specs=[row] * 4, out_shape=[SDS((r, c), F32)] * 4, compiler_params=_cp())(parts, w, m, v)


def _pack_small(d):
    parts = []
    for n in SMALL:
        a = d[n].reshape(-1, HEAD_DIM)
        parts.append(jnp.pad(a, ((0, -a.shape[0] % 8), (0, 0))))
    return jnp.concatenate(parts, axis=0)


def _unpack_small(packed, like):
    out, row = {}, 0
    for n in SMALL:
        rows = like[n].size // HEAD_DIM
        out[n] = packed[row:row + rows].reshape(like[n].shape)
        row += rows + (-rows % 8)
    return out


def _pad_cols(a, width):
    return jnp.pad(a, ((0, 0), (0, width - a.shape[1])))


def kernel(x, mem, positions, mix_norm_g, w_in, q_norm_g, k_norm_g, pool_w, pool_scale, w_out, cross_norm_g, mem_norm_g, w_cq, w_ckv, cq_norm_g, ck_norm_g, w_co, ffn_norm_g, w_gate_up, w_down, loss_target, m_mix_norm_g, m_w_in, m_q_norm_g, m_k_norm_g, m_pool_w, m_pool_scale, m_w_out, m_cross_norm_g, m_mem_norm_g, m_w_cq, m_w_ckv, m_cq_norm_g, m_ck_norm_g, m_w_co, m_ffn_norm_g, m_w_gate_up, m_w_down, v_mix_norm_g, v_w_in, v_q_norm_g, v_k_norm_g, v_pool_w, v_pool_scale, v_w_out, v_cross_norm_g, v_mem_norm_g, v_w_cq, v_w_ckv, v_cq_norm_g, v_ck_norm_g, v_w_co, v_ffn_norm_g, v_w_gate_up, v_w_down):
    given = dict(locals())
    w_f32 = {n: given[n][0] for n in SHARDED + SMALL}
    m_f32 = {n: given["m_" + n][0] for n in SHARDED + SMALL}
    v_f32 = {n: given["v_" + n][0] for n in SHARDED + SMALL}
    for d in (w_f32, m_f32, v_f32):
        d["w_gate_up"] = _pad_cols(d["w_gate_up"], FF_TILE)

    shards = {n: w_f32[n].astype(BF16) for n in SHARDED}
    w_in_full = _gather_weights({"w_in": shards.pop("w_in")})["w_in"]
    sm = {n: w_f32[n] for n in SMALL}
    sm_rows = {n: (a if a.ndim == 3 else a.reshape(1, -1)) for n, a in sm.items()}
    sq, dx, parts, gs = _local_step(x[0], mem[0], positions[0], loss_target[0], w_in_full, shards, sm_rows)
    loss = lax.psum(0.5 / D_MODEL * jnp.sum(sq), ("x", "y", "c"))

    small_parts = _exchange_small(_pack_small(gs))
    res = {}
    for n in SHARDED:
        res[n] = _adamw(parts[n], w_f32[n], m_f32[n], v_f32[n], name="adamw_" + n)
    res["w_gate_up"] = [a[:, :FF_SHARD] for a in res["w_gate_up"]]
    small_res = _adamw(small_parts, _pack_small(sm), _pack_small({n: m_f32[n] for n in SMALL}),
                       _pack_small({n: v_f32[n] for n in SMALL}), name="adamw_small")
    small_res = [_unpack_small(a, sm) for a in small_res]
    order = ("mix_norm_g", "w_in", "q_norm_g", "k_norm_g", "pool_w", "pool_scale", "w_out", "cross_norm_g",
             "mem_norm_g", "w_cq", "w_ckv", "cq_norm_g", "ck_norm_g", "w_co", "ffn_norm_g", "w_gate_up", "w_down")
    outs = [loss, dx[None]]
    for which in range(4):
        for n in order:
            a = res[n][which] if n in SHARDED else small_res[which][n]
            outs.append(a[None])
    return tuple(outs)
```

```python
import functools

import jax
import jax.numpy as jnp
from jax import lax
from jax.experimental import pallas as pl
from jax.experimental.pallas import tpu as pltpu

F32 = jnp.float32
BF16 = jnp.bfloat16
SDS = jax.ShapeDtypeStruct

D_MODEL = 1024
HEAD_DIM = 128
N_GROUPS = 3
DILATIONS = (1, 4, 16)
ATT_HEADS = 4
Q_W = 1536
KV_W = 512
POOL_W = 512
POOL_WINDOWS = (2, 4, 8, 16)
IN_W = 3072
X_W = 512
N_MEM = 256
D_FF = 2816
FF_TILE = 768
FF_SHARD = 704
FF_PAD = 4 * FF_TILE
DOWN_SHARD = 352
ROT_DIM = 32
ROT_HALF = 16
ROPE_THETA = 500000.0
EPS = 1e-6
NEG_INF = -1e30
SCALE = HEAD_DIM ** -0.5
BLOCK = 128
HALO = 16

ADAM_LR = 0.001
ADAM_B1 = 0.9
ADAM_B2 = 0.999
ADAM_EPS = 1e-08
ADAM_WD = 0.01
ADAM_STEP = 10

N_DEV = 8
VMEM_LIMIT_BYTES = 56 * 1024 * 1024
MESH = pl.DeviceIdType.MESH


def _pcall(body, **kw):
    return pl.pallas_call(body, **kw)


def _cp():
    return pltpu.CompilerParams(vmem_limit_bytes=VMEM_LIMIT_BYTES)


def _dot(a, b):
    return lax.dot_general(a, b, (((1,), (0,)), ((), ())), preferred_element_type=F32)


def _dot_nt(a, b):
    return lax.dot_general(a, b, (((1,), (1,)), ((), ())), preferred_element_type=F32)


def _dot_tn(a, b):
    return lax.dot_general(a, b, (((0,), (0,)), ((), ())), preferred_element_type=F32)


def _rows(s):
    return min(512, s)


def _rms_r(x):
    return lax.rsqrt(jnp.mean(x * x, axis=-1, keepdims=True) + EPS)


def _norm_bwd(x, r, gain, dxn):
    z = dxn * gain
    dx = r * z - x * (r * r * r * jnp.mean(z * x, axis=-1, keepdims=True))
    dgain = jnp.sum(dxn * x * r, axis=0, keepdims=True)
    return dx, dgain


def _partner(t):
    lane = lax.broadcasted_iota(jnp.int32, t.shape, 1)
    return jnp.where(lane < ROT_HALF, pltpu.roll(t, HEAD_DIM - ROT_HALF, 1), pltpu.roll(t, ROT_HALF, 1))


def _rope(n, cos_t, sin_t):
    return n * cos_t + _partner(n) * sin_t


def _rope_bwd(d, cos_t, sin_t):
    lane = lax.broadcasted_iota(jnp.int32, d.shape, 1)
    return d * cos_t + jnp.where(lane < ROT_DIM, _partner(d * sin_t), 0.0)


def _resident(shape):
    return pl.BlockSpec(shape, lambda i: (0,) * len(shape), pipeline_mode=pl.Buffered(1))


def _chunks(n, cn):
    return [slice(j * cn, (j + 1) * cn) for j in range(n // cn)]


def _norm_linear(x, gain, w, *, cn, out_dtype, name, shards=None):
    s, k = x.shape
    n = w.shape[1]
    tm = _rows(s)
    names = tuple(shards or ())
    nr = len(names)

    def body(*refs):
        x_ref, g_ref, w_ref = refs[:3]
        y_ref, xn_ref = refs[3 + nr:5 + nr]
        if nr:
            start, finish = _gather_ops(names, refs[3:3 + nr], refs[5 + nr:5 + 2 * nr], refs[5 + 2 * nr:])
            pl.when(pl.program_id(0) == 0)(start)
        xv = x_ref[...]
        xn_ref[...] = (xv * _rms_r(xv) * g_ref[...]).astype(BF16)
        for c in _chunks(n, cn):
            y_ref[:, c] = _dot(xn_ref[...], w_ref[:, c]).astype(out_dtype)
        if nr:
            pl.when(pl.program_id(0) == s // tm - 1)(finish)

    row = lambda w_: pl.BlockSpec((tm, w_), lambda i: (i, 0))
    outs = _pcall(
        body, name=name, grid=(s // tm,),
        in_specs=[row(k), _resident((1, k)), _resident((k, n))] + [HBM_SPEC] * nr,
        out_specs=[row(n), row(k)] + [HBM_SPEC] * nr,
        out_shape=[SDS((s, n), out_dtype), SDS((s, k), BF16)] + [SDS(FULL[a][0], BF16) for a in names],
        scratch_shapes=_gather_scratch(names),
        compiler_params=_cp())(x, gain, w, *[shards[a] for a in names])
    return (outs[0], outs[1], dict(zip(names, outs[2:]))) if nr else tuple(outs)


def _norm_linear_swiglu(x, gain, wgu, *, name):
    s, k = x.shape
    tm = _rows(s)

    def body(x_ref, g_ref, w_ref, gu_ref, a_ref, xn_ref):
        xv = x_ref[...]
        xn_ref[...] = (xv * _rms_r(xv) * g_ref[...]).astype(BF16)
        for c in _chunks(FF_PAD, FF_TILE):
            g = _dot(xn_ref[...], w_ref[:, c])
            u = _dot(xn_ref[...], w_ref[:, slice(FF_PAD + c.start, FF_PAD + c.stop)])
            a_ref[:, c] = (g * jax.nn.sigmoid(g) * u).astype(BF16)
            gu_ref[0, :, c] = g.astype(BF16)
            gu_ref[1, :, c] = u.astype(BF16)

    row = lambda w_: pl.BlockSpec((tm, w_), lambda i: (i, 0))
    return _pcall(
        body, name=name, grid=(s // tm,),
        in_specs=[row(k), _resident((1, k)), _resident((k, 2 * FF_PAD))],
        out_specs=[pl.BlockSpec((2, tm, FF_PAD), lambda i: (0, i, 0)), row(FF_PAD), row(k)],
        out_shape=[SDS((2, s, FF_PAD), BF16), SDS((s, FF_PAD), BF16), SDS((s, k), BF16)],
        compiler_params=_cp())(x, gain, wgu)


def _linear_res(a, w, res, *, cn, name):
    s, k = a.shape
    n = w.shape[1]
    tm = _rows(s)

    def body(a_ref, w_ref, r_ref, y_ref):
        for c in _chunks(n, cn):
            y_ref[:, c] = r_ref[:, c] + _dot(a_ref[...], w_ref[:, c])

    row = lambda w_: pl.BlockSpec((tm, w_), lambda i: (i, 0))
    return _pcall(
        body, name=name, grid=(s // tm,),
        in_specs=[row(k), _resident((k, n)), row(n)], out_specs=row(n),
        out_shape=SDS((s, n), F32), compiler_params=_cp())(a, w, res)


def _linear_res_loss(a, w, res, tgt, *, name):
    s, k = a.shape
    n = w.shape[1]
    tm = _rows(s)

    def body(a_ref, w_ref, r_ref, t_ref, dy_ref, dyb_ref, sq_ref):
        e = r_ref[...] + _dot(a_ref[...], w_ref[...]) - t_ref[...]
        dy = e * (1.0 / n)
        dy_ref[...] = dy
        dyb_ref[...] = dy.astype(BF16)

        @pl.when(pl.program_id(0) == 0)
        def _():
            sq_ref[...] = jnp.zeros_like(sq_ref)
        sq_ref[...] += jnp.sum(e * e, axis=0, keepdims=True)

    row = lambda w_: pl.BlockSpec((tm, w_), lambda i: (i, 0))
    return _pcall(
        body, name=name, grid=(s // tm,),
        in_specs=[row(k), _resident((k, n)), row(n), row(n)],
        out_specs=[row(n), row(n), pl.BlockSpec((1, n), lambda i: (0, 0))],
        out_shape=[SDS((s, n), F32), SDS((s, n), BF16), SDS((1, n), F32)],
        compiler_params=_cp())(a, w, res, tgt)


def _linear_nt(g, w, *, cn, name):
    s, k = g.shape
    n = w.shape[0]
    tm = _rows(s)

    def body(g_ref, w_ref, y_ref):
        for c in _chunks(n, cn):
            y_ref[:, c] = _dot_nt(g_ref[...], w_ref[c, :]).astype(BF16)

    row = lambda w_: pl.BlockSpec((tm, w_), lambda i: (i, 0))
    return _pcall(
        body, name=name, grid=(s // tm,),
        in_specs=[row(k), _resident((n, k))], out_specs=row(n),
        out_shape=SDS((s, n), BF16), compiler_params=_cp())(g, w)


def _swiglu_bwd(dyb, wd, gu, *, name):
    s, n = dyb.shape
    tm = _rows(s)

    def body(dy_ref, wd_ref, gu_ref, dgu_ref):
        for c in _chunks(FF_PAD, FF_TILE):
            da = _dot_nt(dy_ref[...], wd_ref[c, :])
            g = gu_ref[0, :, c].astype(F32)
            u = gu_ref[1, :, c].astype(F32)
            sg = jax.nn.sigmoid(g)
            dgu_ref[0, :, c] = (da * u * (sg * (1.0 + g * (1.0 - sg)))).astype(BF16)
            dgu_ref[1, :, c] = (da * (g * sg)).astype(BF16)

    half = pl.BlockSpec((2, tm, FF_PAD), lambda i: (0, i, 0))
    return _pcall(
        body, name=name, grid=(s // tm,),
        in_specs=[pl.BlockSpec((tm, n), lambda i: (i, 0)), _resident((FF_PAD, n)), half],
        out_specs=half, out_shape=SDS((2, s, FF_PAD), BF16), compiler_params=_cp())(dyb, wd, gu)


def _linear_nt_normbwd(g, w, x, dres, gain, *, name, grads=None):
    d, k = w.shape
    s = x.shape[0]
    tm = _rows(s)
    names = tuple(grads or ())
    nr = len(names)

    def body(*refs):
        g_ref, w_ref, x_ref, dr_ref, gn_ref = refs[:5]
        dx_ref, dxb_ref, dg_ref = refs[5 + nr:8 + nr]
        if nr:
            copies = _exchange_copies(names, refs[5:5 + nr], refs[8 + nr:8 + 2 * nr], *refs[8 + 2 * nr:])

        @pl.when(pl.program_id(0) == 0)
        def _():
            dg_ref[...] = jnp.zeros_like(dg_ref)
            if nr:
                _start(copies)

        if g.ndim == 3:
            dxn = _dot_nt(g_ref[0], w_ref[:, :k // 2]) + _dot_nt(g_ref[1], w_ref[:, k // 2:])
        else:
            dxn = _dot_nt(g_ref[...], w_ref[...])
        xv = x_ref[...]
        dx, dgain = _norm_bwd(xv, _rms_r(xv), gn_ref[...], dxn)
        out = dr_ref[...] + dx
        dx_ref[...] = out
        dxb_ref[...] = out.astype(BF16)
        dg_ref[...] += dgain

        if nr:
            @pl.when(pl.program_id(0) == s // tm - 1)
            def _():
                _finish(copies)

    row = pl.BlockSpec((tm, d), lambda i: (i, 0))
    g_spec = (pl.BlockSpec((2, tm, k // 2), lambda i: (0, i, 0)) if g.ndim == 3
              else pl.BlockSpec((tm, k), lambda i: (i, 0)))
    outs = _pcall(
        body, name=name, grid=(s // tm,),
        in_specs=[g_spec, _resident((d, k)), row, row, _resident((1, d))] + [HBM_SPEC] * nr,
        out_specs=[row, row, pl.BlockSpec((1, d), lambda i: (0, 0))] + [HBM_SPEC] * nr,
        out_shape=[SDS((s, d), F32), SDS((s, d), BF16), SDS((1, d), F32)]
        + [SDS((N_DEV,) + _shard_shape(n), BF16) for n in names],
        scratch_shapes=_comm_sems(nr) if nr else [],
        compiler_params=_cp())(g, w, x, dres, gain, *[grads[n] for n in names])
    return (outs[0], outs[1], outs[2], dict(zip(names, outs[3:]))) if nr else tuple(outs)


def _dw_tn(x, g, *, tkw, tn, name):
    s, kw = x.shape
    halves = g.ndim == 3
    n = 2 * g.shape[2] if halves else g.shape[1]
    ts = _rows(s)
    ns = s // ts
    per_half = n // 2 // tn

    def body(x_ref, g_ref, o_ref, acc_ref):
        ss = pl.program_id(2)

        @pl.when(ss == 0)
        def _():
            acc_ref[...] = jnp.zeros_like(acc_ref)

        acc_ref[...] += _dot_tn(x_ref[...], g_ref[...])

        @pl.when(ss == ns - 1)
        def _():
            o_ref[...] = acc_ref[...].astype(BF16)

    g_spec = (pl.BlockSpec((None, ts, tn), lambda a, b, ss: (b // per_half, ss, b % per_half)) if halves
              else pl.BlockSpec((ts, tn), lambda a, b, ss: (ss, b)))
    return _pcall(
        body, name=name, grid=(kw // tkw, n // tn, ns),
        in_specs=[pl.BlockSpec((ts, tkw), lambda a, b, ss: (ss, a)), g_spec],
        out_specs=pl.BlockSpec((tkw, tn), lambda a, b, ss: (a, b)),
        out_shape=SDS((kw, n), BF16),
        scratch_shapes=[pltpu.VMEM((tkw, tn), F32)], compiler_params=_cp())(x, g)


def _qk_prep(proj, cos_t, sin_t, qg, kg, shards, *, name):
    s = proj.shape[0]
    tm = _rows(s)
    nqh = Q_W // HEAD_DIM
    names = tuple(shards)
    nr = len(names)

    def body(*refs):
        q_ref, k_ref, v_ref, c_ref, s_ref, qg_ref, kg_ref = refs[:7]
        qo_ref, ko_ref, vo_ref = refs[7 + nr:10 + nr]
        start, finish = _gather_ops(names, refs[7:7 + nr], refs[10 + nr:10 + 2 * nr], refs[10 + 2 * nr:])
        pl.when(pl.program_id(0) == 0)(start)
        c, sn = c_ref[...], s_ref[...]
        for h in range(nqh):
            sl = slice(h * HEAD_DIM, (h + 1) * HEAD_DIM)
            y = q_ref[:, sl]
            qo_ref[:, sl] = (_rope(y * _rms_r(y) * qg_ref[...], c, sn) * SCALE).astype(BF16)
        for h in range(ATT_HEADS):
            sl = slice(h * HEAD_DIM, (h + 1) * HEAD_DIM)
            y = k_ref[:, sl]
            ko_ref[:, sl] = _rope(y * _rms_r(y) * kg_ref[...], c, sn).astype(BF16)
        vo_ref[...] = v_ref[...].astype(BF16)
        pl.when(pl.program_id(0) == s // tm - 1)(finish)

    row = lambda w, j: pl.BlockSpec((tm, w), lambda i: (i, j))
    one = pl.BlockSpec((1, HEAD_DIM), lambda i: (0, 0))
    outs = _pcall(
        body, name=name, grid=(s // tm,),
        in_specs=[row(Q_W, 0), row(KV_W, 3), row(KV_W, 4), row(HEAD_DIM, 0), row(HEAD_DIM, 0), one, one]
        + [HBM_SPEC] * nr,
        out_specs=[row(Q_W, 0), row(KV_W, 0), row(KV_W, 0)] + [HBM_SPEC] * nr,
        out_shape=[SDS((s, Q_W), BF16), SDS((s, KV_W), BF16), SDS((s, KV_W), BF16)]
        + [SDS(FULL[a][0], BF16) for a in names],
        scratch_shapes=_gather_scratch(names),
        compiler_params=_cp())(proj, proj, proj, cos_t, sin_t, qg, kg, *[shards[a] for a in names])
    return outs[0], outs[1], outs[2], dict(zip(names, outs[3:]))


def _qk_prep_bwd(proj, dq, dk, dv, du, cos_t, sin_t, qg, kg, *, name):
    s = proj.shape[0]
    tm = _rows(s)

    def body(q_ref, k_ref, dq_ref, dk_ref, dv_ref, du_ref, c_ref, s_ref, qg_ref, kg_ref, dp_ref, dqg_ref, dkg_ref):
        c, sn = c_ref[...], s_ref[...]

        @pl.when(pl.program_id(0) == 0)
        def _():
            dqg_ref[...] = jnp.zeros_like(dqg_ref)
            dkg_ref[...] = jnp.zeros_like(dkg_ref)

        dqg = jnp.zeros((1, HEAD_DIM), F32)
        for g in range(N_GROUPS):
            for h in range(ATT_HEADS):
                sl = slice(h * HEAD_DIM, (h + 1) * HEAD_DIM)
                col = slice(g * KV_W + h * HEAD_DIM, g * KV_W + (h + 1) * HEAD_DIM)
                y = q_ref[:, col]
                dn = _rope_bwd(dq_ref[g, :, sl].astype(F32) * SCALE, c, sn)
                dy, dgain = _norm_bwd(y, _rms_r(y), qg_ref[...], dn)
                dp_ref[:, col] = dy.astype(BF16)
                dqg = dqg + dgain
        dqg_ref[...] += dqg

        dkg = jnp.zeros((1, HEAD_DIM), F32)
        for h in range(ATT_HEADS):
            sl = slice(h * HEAD_DIM, (h + 1) * HEAD_DIM)
            y = k_ref[:, sl]
            dn = _rope_bwd(dk_ref[:, sl], c, sn)
            dy, dgain = _norm_bwd(y, _rms_r(y), kg_ref[...], dn)
            dp_ref[:, Q_W + h * HEAD_DIM:Q_W + (h + 1) * HEAD_DIM] = dy.astype(BF16)
            dkg = dkg + dgain
        dkg_ref[...] += dkg

        dp_ref[:, Q_W + KV_W:Q_W + 2 * KV_W] = dv_ref[...].astype(BF16)
        dp_ref[:, Q_W + 2 * KV_W:] = du_ref[...].astype(BF16)

    row = lambda w, j: pl.BlockSpec((tm, w), lambda i: (i, j))
    one = pl.BlockSpec((1, HEAD_DIM), lambda i: (0, 0))
    return _pcall(
        body, name=name, grid=(s // tm,),
        in_specs=[row(Q_W, 0), row(KV_W, 3), pl.BlockSpec((N_GROUPS, tm, KV_W), lambda i: (0, i, 0))]
        + [row(KV_W, 0)] * 3 + [row(HEAD_DIM, 0), row(HEAD_DIM, 0), one, one],
        out_specs=[row(IN_W, 0), one, one],
        out_shape=[SDS((s, IN_W), BF16), SDS((1, HEAD_DIM), F32), SDS((1, HEAD_DIM), F32)],
        compiler_params=_cp())(proj, proj, dq, dk, dv, du, cos_t, sin_t, qg, kg)


ATT_ROWS = 16 * BLOCK


def _sub(ref, start, d, size=BLOCK):
    return ref[pl.ds(start, size, stride=d), :] if d > 1 else ref[pl.ds(start, size), :]


def _sub_set(ref, start, d, val):
    if d > 1:
        ref[pl.ds(start, BLOCK, stride=d), :] = val
    else:
        ref[pl.ds(start, BLOCK), :] = val


def _band_masks():
    row = lax.broadcasted_iota(jnp.int32, (BLOCK, BLOCK), 0)
    col = lax.broadcasted_iota(jnp.int32, (BLOCK, BLOCK), 1)
    return col <= row, col >= row


def _eye():
    row = lax.broadcasted_iota(jnp.int32, (BLOCK, BLOCK), 0)
    col = lax.broadcasted_iota(jnp.int32, (BLOCK, BLOCK), 1)
    return jnp.where(row == col, 1.0, 0.0).astype(BF16)


def _transpose_mxu(eye, a):
    return _dot_nt(eye, a).astype(BF16)


def _attn_fwd(q_rot, k_rot, v, shards, *, name):
    s = q_rot.shape[0]
    rr = ATT_ROWS
    nblk = s // rr
    names = tuple(shards)
    nr = len(names)

    def body(*refs):
        q0, q1, q2, kp, kc, vp, vc = refs[:7]
        mix_ref, lse_ref = refs[7 + nr:9 + nr]
        qs, ks, vs, os_, ls = refs[9 + 2 * nr:14 + 2 * nr]
        h, n = pl.program_id(0), pl.program_id(1)
        start, finish = _gather_ops(names, refs[7:7 + nr], refs[9 + nr:9 + 2 * nr], refs[14 + 2 * nr:])
        pl.when((h == 0) & (n == 0))(start)
        for g, q_ref in enumerate((q0, q1, q2)):
            qs[g] = q_ref[...].astype(F32)
        ks[:rr] = kp[...].astype(F32)
        ks[rr:] = kc[...].astype(F32)
        vs[:rr] = vp[...].astype(F32)
        vs[rr:] = vc[...].astype(F32)
        m_cur, m_band = _band_masks()
        mask_in = jnp.concatenate([m_band, m_cur], axis=1)
        mask_first = jnp.concatenate([m_band & (n > 0), m_cur], axis=1)
        ones = jnp.ones((2 * BLOCK, HEAD_DIM), BF16)
        pieces = [(g, d, j * BLOCK * d + r, j) for g, d in enumerate(DILATIONS) for r in range(d)
                  for j in range(rr // (BLOCK * d))]

        def scores(piece):
            g, d, base, j = piece
            q = _sub(qs.at[g], base, d).astype(BF16)
            k2 = _sub(ks, rr + base - BLOCK * d, d, 2 * BLOCK).astype(BF16)
            return jnp.where(mask_first if j == 0 else mask_in, _dot_nt(q, k2), NEG_INF)

        sc = scores(pieces[0])
        for i, (g, d, base, j) in enumerate(pieces):
            cur = sc
            if i + 1 < len(pieces):
                sc = scores(pieces[i + 1])
            m = jnp.max(cur, axis=-1, keepdims=True)
            p = jnp.exp(cur - m).astype(BF16)
            v2 = _sub(vs, rr + base - BLOCK * d, d, 2 * BLOCK).astype(BF16)
            acc_l = _dot(p, jnp.concatenate([v2, ones], axis=1))
            l = acc_l[:, HEAD_DIM:]
            _sub_set(os_.at[g], base, d, acc_l[:, :HEAD_DIM] / l)
            _sub_set(ls.at[g], base, d, m + jnp.log(l))
        for c in _chunks(rr, 2 * BLOCK):
            a, b, cc = ls[0, c, :], ls[1, c, :], ls[2, c, :]
            m = jnp.maximum(jnp.maximum(a, b), cc)
            wa, wb, wc = jnp.exp(a - m), jnp.exp(b - m), jnp.exp(cc - m)
            den = wa + wb + wc
            mix_ref[c, :] = ((wa * os_[0, c, :] + wb * os_[1, c, :] + wc * os_[2, c, :]) / den).astype(BF16)
            lse_ref[c, :] = m + jnp.log(den)

        pl.when((h == ATT_HEADS - 1) & (n == nblk - 1))(finish)

    blk = lambda f: pl.BlockSpec((rr, HEAD_DIM), f)
    prv = lambda n: jnp.maximum(n - 1, 0)
    outs = _pcall(
        body, name=name, grid=(ATT_HEADS, nblk),
        in_specs=[blk(lambda h, n, g=g: (n, g * ATT_HEADS + h)) for g in range(N_GROUPS)]
        + [blk(lambda h, n: (prv(n), h)), blk(lambda h, n: (n, h))] * 2 + [HBM_SPEC] * nr,
        out_specs=[blk(lambda h, n: (n, h)), blk(lambda h, n: (n, h))] + [HBM_SPEC] * nr,
        out_shape=[SDS((s, KV_W + POOL_W), BF16), SDS((s, KV_W), F32)] + [SDS(FULL[w][0], BF16) for w in names],
        scratch_shapes=[pltpu.VMEM((N_GROUPS, rr, HEAD_DIM), F32), pltpu.VMEM((2 * rr, HEAD_DIM), F32),
                        pltpu.VMEM((2 * rr, HEAD_DIM), F32), pltpu.VMEM((N_GROUPS, rr, HEAD_DIM), F32),
                        pltpu.VMEM((N_GROUPS, rr, HEAD_DIM), F32)] + _gather_scratch(names),
        compiler_params=_cp())(q_rot, q_rot, q_rot, k_rot, k_rot, v, v, *[shards[w] for w in names])
    return outs[0], outs[1], dict(zip(names, outs[2:]))


def _attn_bwd(q_rot, k_rot, v, mix, dmix, lse, grads, *, name):
    s = q_rot.shape[0]
    rr = ATT_ROWS
    nblk = s // rr
    names = tuple(grads)
    nr = len(names)

    def body(*refs):
        q0, q1, q2, qx0, qx1, qx2, kp, kc, vp, vc, do_c, do_x, o_c, o_x, l_c, l_x = refs[:16]
        dq_ref, dk_ref, dv_ref = refs[16 + nr:19 + nr]
        qs, ks, vs, dos, lss, dls, dqs, dks, dvs, send_sems, recv_sems, local_sems = refs[19 + 2 * nr:]
        h, n = pl.program_id(0), pl.program_id(1)
        copies = _exchange_copies(names, refs[16:16 + nr], refs[19 + nr:19 + 2 * nr], send_sems, recv_sems, local_sems)

        @pl.when((h == 0) & (n == 0))
        def _():
            _start(copies)

        for g, (qc_ref, qx_ref) in enumerate(((q0, qx0), (q1, qx1), (q2, qx2))):
            qs[g, :rr] = qc_ref[...].astype(F32)
            qs[g, rr:] = qx_ref[...].astype(F32)
        ks[:rr] = kp[...].astype(F32)
        ks[rr:] = kc[...].astype(F32)
        vs[:rr] = vp[...].astype(F32)
        vs[rr:] = vc[...].astype(F32)
        lss[:rr] = l_c[...]
        lss[rr:] = l_x[...]
        for half, (d_ref, o_ref) in enumerate(((do_c, o_c), (do_x, o_x))):
            for c in _chunks(rr, 2 * BLOCK):
                cs = slice(half * rr + c.start, half * rr + c.stop)
                dof = d_ref[c, :].astype(F32)
                dos[cs, :] = dof
                dls[cs, :] = jnp.broadcast_to(jnp.sum(dof * o_ref[c, :].astype(F32), axis=-1, keepdims=True),
                                              (2 * BLOCK, HEAD_DIM))
        m_cur, m_band = _band_masks()
        mask_in = jnp.concatenate([m_cur, m_band], axis=0)
        mask_last = jnp.concatenate([m_cur, m_band & (n + 1 < nblk)], axis=0)
        m_first = m_band & (n > 0)
        eye = _eye()
        pieces = [(g, d, j * BLOCK * d + r, j, rr // (BLOCK * d)) for g, d in enumerate(DILATIONS) for r in range(d)
                  for j in range(rr // (BLOCK * d))]

        def front(piece):
            g, d, base, _, _ = piece
            q2 = _sub(qs.at[g], base, d, 2 * BLOCK).astype(BF16)
            do2 = _sub(dos, base, d, 2 * BLOCK).astype(BF16)
            k = _sub(ks, rr + base, d).astype(BF16)
            vv = _sub(vs, rr + base, d).astype(BF16)
            return q2, do2, k, _dot_nt(q2, k), _dot_nt(do2, vv)

        nxt_front = front(pieces[0])
        dq_acc = None
        for i, (g, d, base, j, nsub) in enumerate(pieces):
            q2, do2, k, s2, dp2 = nxt_front
            if j == 0:
                kp_ = _sub(ks, rr + base - BLOCK * d, d).astype(BF16)
                p0 = jnp.where(m_first, jnp.exp(_dot_nt(q2[:BLOCK], kp_) - _sub(lss, base, d)), 0.0)
                ds0 = p0 * (_dot_nt(do2[:BLOCK], _sub(vs, rr + base - BLOCK * d, d).astype(BF16)) - _sub(dls, base, d))
                dq_acc = _dot(ds0.astype(BF16), kp_)
            if i + 1 < len(pieces):
                nxt_front = front(pieces[i + 1])
            p2 = jnp.where(mask_last if j + 1 == nsub else mask_in,
                           jnp.exp(s2 - _sub(lss, base, d, 2 * BLOCK)), 0.0)
            ds2 = (p2 * (dp2 - _sub(dls, base, d, 2 * BLOCK))).astype(BF16)
            dq2 = _dot(ds2, k)
            _sub_set(dqs.at[g], base, d, dq_acc + dq2[:BLOCK])
            dq_acc = dq2[BLOCK:]
            dk = _dot(_transpose_mxu(eye, ds2), q2)
            dv = _dot(_transpose_mxu(eye, p2.astype(BF16)), do2)
            if g == 0:
                _sub_set(dks, base, d, dk)
                _sub_set(dvs, base, d, dv)
            else:
                _sub_set(dks, base, d, _sub(dks, base, d) + dk)
                _sub_set(dvs, base, d, _sub(dvs, base, d) + dv)
        for g in range(N_GROUPS):
            dq_ref[g] = dqs[g].astype(BF16)
        dk_ref[...] = dks[...]
        dv_ref[...] = dvs[...]

        @pl.when((h == ATT_HEADS - 1) & (n == nblk - 1))
        def _():
            _finish(copies)

    blk = lambda f: pl.BlockSpec((rr, HEAD_DIM), f)
    prv = lambda n: jnp.maximum(n - 1, 0)
    nxt = lambda n: jnp.minimum(n + 1, nblk - 1)
    cur_kv = blk(lambda h, n: (n, h))
    outs = _pcall(
        body, name=name, grid=(ATT_HEADS, nblk),
        in_specs=[blk(lambda h, n, g=g: (n, g * ATT_HEADS + h)) for g in range(N_GROUPS)]
        + [blk(lambda h, n, g=g: (nxt(n), g * ATT_HEADS + h)) for g in range(N_GROUPS)]
        + [blk(lambda h, n: (prv(n), h)), cur_kv] * 2
        + [cur_kv, blk(lambda h, n: (nxt(n), h))] * 3 + [HBM_SPEC] * nr,
        out_specs=[pl.BlockSpec((N_GROUPS, rr, HEAD_DIM), lambda h, n: (0, n, h)), cur_kv, cur_kv] + [HBM_SPEC] * nr,
        out_shape=[SDS((N_GROUPS, s, KV_W), BF16), SDS((s, KV_W), F32), SDS((s, KV_W), F32)]
        + [SDS((N_DEV,) + _shard_shape(w), BF16) for w in names],
        scratch_shapes=[pltpu.VMEM((N_GROUPS, 2 * rr, HEAD_DIM), F32)] + [pltpu.VMEM((2 * rr, HEAD_DIM), F32)] * 5
        + [pltpu.VMEM((N_GROUPS, rr, HEAD_DIM), F32), pltpu.VMEM((rr, HEAD_DIM), F32), pltpu.VMEM((rr, HEAD_DIM), F32)]
        + _comm_sems(nr),
        compiler_params=_cp())(q_rot, q_rot, q_rot, q_rot, q_rot, q_rot, k_rot, k_rot, v, v, dmix, dmix, mix, mix,
                               lse, lse, *[grads[w] for w in names])
    return outs[0], outs[1], outs[2], dict(zip(names, outs[3:]))


def _pool_d(u_ref, halo_ref, i, tm):
    halo = jnp.where(i > 0, halo_ref[...], 0.0)
    t = i * tm + lax.broadcasted_iota(jnp.int32, (tm, 1), 0)
    out = []
    for g, w in enumerate(POOL_WINDOWS):
        sl = slice(g * HEAD_DIM, (g + 1) * HEAD_DIM)
        u = u_ref[:, sl]
        acc = jnp.concatenate([halo[:, sl], u], axis=0)
        sh = 1
        while sh < w:
            acc = acc + pltpu.roll(acc, sh, 0)
            sh *= 2
        cnt = jnp.minimum(t + 1, w).astype(F32)
        out.append(acc[HALO:, :] / cnt - u)
    return out


def _pool_fwd(proj, mix, pool_w, pool_scale, *, name):
    s = proj.shape[0]
    tm = _rows(s)
    ucol = (IN_W - POOL_W) // POOL_W

    def body(u_ref, halo_ref, mix_in, w_ref, sc_ref, o_ref):
        del mix_in
        dd = _pool_d(u_ref, halo_ref, pl.program_id(0), tm)
        for g in range(len(POOL_WINDOWS)):
            sl = slice(g * HEAD_DIM, (g + 1) * HEAD_DIM)
            y = _dot(dd[g].astype(BF16), w_ref[g].astype(BF16))
            o_ref[:, sl] = (y * sc_ref[:, sl]).astype(BF16)

    return _pcall(
        body, name=name, grid=(s // tm,),
        in_specs=[pl.BlockSpec((tm, POOL_W), lambda i: (i, ucol)),
                  pl.BlockSpec((HALO, POOL_W), lambda i: (jnp.maximum(i * (tm // HALO) - 1, 0), ucol)),
                  pl.BlockSpec(memory_space=pl.ANY),
                  pl.BlockSpec((len(POOL_WINDOWS), HEAD_DIM, HEAD_DIM), lambda i: (0, 0, 0)),
                  pl.BlockSpec((1, POOL_W), lambda i: (0, 0))],
        out_specs=pl.BlockSpec((tm, POOL_W), lambda i: (i, 1)),
        out_shape=SDS(mix.shape, BF16), input_output_aliases={2: 0},
        compiler_params=_cp())(proj, proj, mix, pool_w, pool_scale)


def _pool_bwd(proj, dmix, pool_w, pool_scale, *, name):
    s = proj.shape[0]
    tm = _rows(s)
    nblk = s // tm
    ucol = (IN_W - POOL_W) // POOL_W
    ng = len(POOL_WINDOWS)

    def body(u_ref, halo_ref, dp_ref, dpn_ref, w_ref, sc_ref, du_ref, dw_ref, dsc_ref):
        i = pl.program_id(0)

        @pl.when(i == 0)
        def _():
            dw_ref[...] = jnp.zeros_like(dw_ref)
            dsc_ref[...] = jnp.zeros_like(dsc_ref)

        dd = _pool_d(u_ref, halo_ref, i, tm)
        t = i * tm + lax.broadcasted_iota(jnp.int32, (tm, 1), 0)
        dpn = jnp.where(i + 1 < nblk, dpn_ref[...].astype(F32), 0.0)
        for g, w in enumerate(POOL_WINDOWS):
            sl = slice(g * HEAD_DIM, (g + 1) * HEAD_DIM)
            wg = w_ref[g].astype(BF16)
            db = dd[g].astype(BF16)
            dp = dp_ref[:, sl].astype(F32)
            dsc_ref[:, sl] += jnp.sum(dp * _dot(db, wg), axis=0, keepdims=True)
            dy = (dp * sc_ref[:, sl]).astype(BF16)
            dw_ref[g] += _dot_tn(db, dy)
            g_d = _dot_nt(dy, wg)
            g_dn = _dot_nt((dpn[:, sl] * sc_ref[:, sl]).astype(BF16), wg)
            cnt = jnp.minimum(t + 1, w).astype(F32)
            acc = jnp.concatenate([g_d / cnt, g_dn * (1.0 / w)], axis=0)
            sh = 1
            while sh < w:
                acc = acc + pltpu.roll(acc, tm + HALO - sh, 0)
                sh *= 2
            du_ref[:, sl] = acc[:tm, :] - g_d

    nh = s // HALO
    return _pcall(
        body, name=name, grid=(nblk,),
        in_specs=[pl.BlockSpec((tm, POOL_W), lambda i: (i, ucol)),
                  pl.BlockSpec((HALO, POOL_W), lambda i: (jnp.maximum(i * (tm // HALO) - 1, 0), ucol)),
                  pl.BlockSpec((tm, POOL_W), lambda i: (i, 1)),
                  pl.BlockSpec((HALO, POOL_W), lambda i: (jnp.minimum((i + 1) * (tm // HALO), nh - 1), 1)),
                  pl.BlockSpec((ng, HEAD_DIM, HEAD_DIM), lambda i: (0, 0, 0)),
                  pl.BlockSpec((1, POOL_W), lambda i: (0, 0))],
        out_specs=[pl.BlockSpec((tm, POOL_W), lambda i: (i, 0)),
                   pl.BlockSpec((ng, HEAD_DIM, HEAD_DIM), lambda i: (0, 0, 0)),
                   pl.BlockSpec((1, POOL_W), lambda i: (0, 0))],
        out_shape=[SDS((s, POOL_W), F32), SDS((ng, HEAD_DIM, HEAD_DIM), F32), SDS((1, POOL_W), F32)],
        compiler_params=_cp())(proj, proj, dmix, dmix, pool_w, pool_scale)


def _mem_fwd(mem, mem_g, wckv, ck_g, *, name):
    def body(m_ref, g_ref, w_ref, kg_ref, mn_ref, ckr_ref, ckn_ref, cv_ref):
        mv = m_ref[...]
        mn = (mv * _rms_r(mv) * g_ref[...]).astype(BF16)
        mn_ref[...] = mn
        ckv = _dot(mn, w_ref[...])
        ckr_ref[...] = ckv[:, :X_W]
        cv_ref[...] = ckv[:, X_W:].astype(BF16)
        for h in range(ATT_HEADS):
            sl = slice(h * HEAD_DIM, (h + 1) * HEAD_DIM)
            y = ckv[:, sl]
            ckn_ref[:, sl] = (y * _rms_r(y) * kg_ref[...]).astype(BF16)

    return _pcall(
        body, name=name,
        out_shape=[SDS((N_MEM, D_MODEL), BF16), SDS((N_MEM, X_W), F32), SDS((N_MEM, X_W), BF16),
                   SDS((N_MEM, X_W), BF16)],
        compiler_params=_cp())(mem, mem_g, wckv, ck_g)


def _cross_q(cq_ref, g_ref, sl):
    y = cq_ref[:, sl]
    r = _rms_r(y)
    return y, r, y * r * g_ref[...] * SCALE


def _cross_fwd(cq_raw, ck_n, cv, cq_g, *, name):
    s = cq_raw.shape[0]
    tm = _rows(s)

    def body(cq_ref, k_ref, v_ref, g_ref, o_ref):
        for h in range(ATT_HEADS):
            sl = slice(h * HEAD_DIM, (h + 1) * HEAD_DIM)
            _, _, qn = _cross_q(cq_ref, g_ref, sl)
            sc = _dot_nt(qn.astype(BF16), k_ref[:, sl])
            p = jnp.exp(sc - jnp.max(sc, axis=-1, keepdims=True))
            p = p / jnp.sum(p, axis=-1, keepdims=True)
            o_ref[:, sl] = _dot(p.astype(BF16), v_ref[:, sl]).astype(BF16)

    full = lambda a: pl.BlockSpec(a.shape, lambda i: (0, 0))
    return _pcall(
        body, name=name, grid=(s // tm,),
        in_specs=[pl.BlockSpec((tm, X_W), lambda i: (i, 0)), full(ck_n), full(cv), full(cq_g)],
        out_specs=pl.BlockSpec((tm, X_W), lambda i: (i, 0)),
        out_shape=SDS((s, X_W), BF16), compiler_params=_cp())(cq_raw, ck_n, cv, cq_g)


def _cross_bwd(d_o, cq_raw, ck_n, cv, cq_g, *, name):
    s = cq_raw.shape[0]
    tm = _rows(s)

    def body(do_ref, cq_ref, k_ref, v_ref, g_ref, dcq_ref, dk_ref, dv_ref, dg_ref):
        @pl.when(pl.program_id(0) == 0)
        def _():
            dk_ref[...] = jnp.zeros_like(dk_ref)
            dv_ref[...] = jnp.zeros_like(dv_ref)
            dg_ref[...] = jnp.zeros_like(dg_ref)

        dg = jnp.zeros((1, HEAD_DIM), F32)
        for h in range(ATT_HEADS):
            sl = slice(h * HEAD_DIM, (h + 1) * HEAD_DIM)
            y, r, qn = _cross_q(cq_ref, g_ref, sl)
            qb = qn.astype(BF16)
            do = do_ref[:, sl]
            sc = _dot_nt(qb, k_ref[:, sl])
            p = jnp.exp(sc - jnp.max(sc, axis=-1, keepdims=True))
            p = p / jnp.sum(p, axis=-1, keepdims=True)
            dv_ref[:, sl] += _dot_tn(p.astype(BF16), do)
            dp = _dot_nt(do, v_ref[:, sl])
            ds = (p * (dp - jnp.sum(dp * p, axis=-1, keepdims=True))).astype(BF16)
            dk_ref[:, sl] += _dot_tn(ds, qb)
            dn = _dot(ds, k_ref[:, sl]) * SCALE
            dy, dgain = _norm_bwd(y, r, g_ref[...], dn)
            dcq_ref[:, sl] = dy.astype(BF16)
            dg = dg + dgain
        dg_ref[...] += dg

    full = lambda a: pl.BlockSpec(a.shape, lambda i: (0, 0))
    row = pl.BlockSpec((tm, X_W), lambda i: (i, 0))
    return _pcall(
        body, name=name, grid=(s // tm,),
        in_specs=[row, row, full(ck_n), full(cv), full(cq_g)],
        out_specs=[row, pl.BlockSpec((N_MEM, X_W), lambda i: (0, 0)), pl.BlockSpec((N_MEM, X_W), lambda i: (0, 0)),
                   pl.BlockSpec((1, HEAD_DIM), lambda i: (0, 0))],
        out_shape=[SDS((s, X_W), BF16), SDS((N_MEM, X_W), F32), SDS((N_MEM, X_W), F32), SDS((1, HEAD_DIM), F32)],
        compiler_params=_cp())(d_o, cq_raw, ck_n, cv, cq_g)


def _mem_bwd(dck_n, dcv, ck_raw, memn, mem, wckv, mem_g, ck_g, *, name):
    def body(dk_ref, dv_ref, ckr_ref, mn_ref, m_ref, w_ref, mg_ref, kg_ref, dw_ref, dmg_ref, dkg_ref, dckv_s):
        dkg = jnp.zeros((1, HEAD_DIM), F32)
        for h in range(ATT_HEADS):
            sl = slice(h * HEAD_DIM, (h + 1) * HEAD_DIM)
            y = ckr_ref[:, sl]
            dy, dgain = _norm_bwd(y, _rms_r(y), kg_ref[...], dk_ref[:, sl])
            dckv_s[:, sl] = dy.astype(BF16)
            dkg = dkg + dgain
        dkg_ref[...] = dkg
        dckv_s[:, X_W:] = dv_ref[...].astype(BF16)
        dckv = dckv_s[...]
        dw_ref[...] = _dot_tn(mn_ref[...], dckv).astype(BF16)
        dmn = _dot_nt(dckv, w_ref[...])
        mv = m_ref[...]
        dmg_ref[...] = jnp.sum(dmn * mv * _rms_r(mv), axis=0, keepdims=True)

    return _pcall(
        body, name=name,
        out_shape=[SDS((D_MODEL, 2 * X_W), BF16), SDS((1, D_MODEL), F32), SDS((1, HEAD_DIM), F32)],
        scratch_shapes=[pltpu.VMEM((N_MEM, 2 * X_W), BF16)],
        compiler_params=_cp())(dck_n, dcv, ck_raw, memn, mem, wckv, mem_g, ck_g)


def _rope_tables(pos):
    inv_freq = ROPE_THETA ** (-jnp.arange(0, ROT_DIM, 2, dtype=F32) / ROT_DIM)
    ang = pos.astype(F32)[:, None] * inv_freq
    cos, sin = jnp.cos(ang), jnp.sin(ang)
    s = pos.shape[0]
    rest = HEAD_DIM - ROT_DIM
    cos_t = jnp.concatenate([cos, cos, jnp.ones((s, rest), F32)], axis=1)
    sin_t = jnp.concatenate([-sin, sin, jnp.zeros((s, rest), F32)], axis=1)
    return cos_t, sin_t


def _local_step(x, mem, pos, tgt, w_in, shards, sm):
    cos_t, sin_t = _rope_tables(pos)
    wb = {"w_in": w_in}

    pick = lambda *names: {a: shards[a] for a in names}
    proj, xn, got = _norm_linear(x, sm["mix_norm_g"], wb["w_in"], cn=768, out_dtype=F32, name="fwd_in_proj",
                                 shards=pick("w_out", "w_cq", "w_ckv", "w_co"))
    wb.update(got)
    q_rot, k_rot, v, got = _qk_prep(proj, cos_t, sin_t, sm["q_norm_g"], sm["k_norm_g"], pick("w_down"),
                                    name="fwd_qk_prep")
    wb.update(got)
    mix, lse_all, got = _attn_fwd(q_rot, k_rot, v, pick("w_gate_up"), name="fwd_attn")
    wb.update(got)
    mix = _pool_fwd(proj, mix, sm["pool_w"], sm["pool_scale"], name="fwd_pool")
    h1 = _linear_res(mix, wb["w_out"], x, cn=512, name="fwd_out_proj")
    cq_raw, hn = _norm_linear(h1, sm["cross_norm_g"], wb["w_cq"], cn=512, out_dtype=F32, name="fwd_cq_proj")
    memn, ck_raw, ck_n, cv = _mem_fwd(mem, sm["mem_norm_g"], wb["w_ckv"], sm["ck_norm_g"], name="fwd_mem")
    co = _cross_fwd(cq_raw, ck_n, cv, sm["cq_norm_g"], name="fwd_cross")
    h2 = _linear_res(co, wb["w_co"], h1, cn=512, name="fwd_co_proj")
    gu, act, fn = _norm_linear_swiglu(h2, sm["ffn_norm_g"], wb["w_gate_up"], name="fwd_gate_up")
    dy, dyb, sq = _linear_res_loss(act, wb["w_down"], h2, tgt, name="fwd_down_loss")

    gw = {}
    gs = {}
    dgu = _swiglu_bwd(dyb, wb["w_down"], gu, name="bwd_swiglu")
    gw["w_down"] = _dw_tn(act, dyb, tkw=1024, tn=1024, name="bwd_dw_down")
    dh2, dh2b, gs["ffn_norm_g"] = _linear_nt_normbwd(dgu, wb["w_gate_up"], h2, dy, sm["ffn_norm_g"],
                                                     name="bwd_ffn_in")
    gw["w_gate_up"] = _dw_tn(fn, dgu, tkw=1024, tn=1536, name="bwd_dw_gate_up")

    d_co = _linear_nt(dh2b, wb["w_co"], cn=512, name="bwd_co_proj")
    gw["w_co"] = _dw_tn(co, dh2b, tkw=512, tn=1024, name="bwd_dw_co")
    dcq, dck_n, dcv, gs["cq_norm_g"] = _cross_bwd(d_co, cq_raw, ck_n, cv, sm["cq_norm_g"], name="bwd_cross")
    gw["w_ckv"], gs["mem_norm_g"], gs["ck_norm_g"] = _mem_bwd(dck_n, dcv, ck_raw, memn, mem, wb["w_ckv"],
                                                             sm["mem_norm_g"], sm["ck_norm_g"], name="bwd_mem")
    dh1, dh1b, gs["cross_norm_g"] = _linear_nt_normbwd(dcq, wb["w_cq"], h1, dh2, sm["cross_norm_g"],
                                                       name="bwd_cq_in")
    gw["w_cq"] = _dw_tn(hn, dcq, tkw=1024, tn=512, name="bwd_dw_cq")

    dmix = _linear_nt(dh1b, wb["w_out"], cn=512, name="bwd_out_proj")
    gw["w_out"] = _dw_tn(mix, dh1b, tkw=1024, tn=1024, name="bwd_dw_out")
    du, gs["pool_w"], gs["pool_scale"] = _pool_bwd(proj, dmix, sm["pool_w"], sm["pool_scale"], name="bwd_pool")
    dq, dk, dv, parts = _attn_bwd(q_rot, k_rot, v, mix, dmix, lse_all, gw, name="bwd_attn")
    dproj, gs["q_norm_g"], gs["k_norm_g"] = _qk_prep_bwd(proj, dq, dk, dv, du, cos_t, sin_t, sm["q_norm_g"],
                                                         sm["k_norm_g"], name="bwd_qk_prep")
    gw_in = _dw_tn(xn, dproj, tkw=1024, tn=1536, name="bwd_dw_in")
    dx, _, gs["mix_norm_g"], last = _linear_nt_normbwd(dproj, wb["w_in"], x, dh1, sm["mix_norm_g"],
                                                       name="bwd_in_proj", grads={"w_in": gw_in})
    parts.update(last)
    return sq, dx, parts, gs


SHARDED = ("w_in", "w_out", "w_cq", "w_ckv", "w_co", "w_gate_up", "w_down")
SMALL = ("mix_norm_g", "q_norm_g", "k_norm_g", "pool_w", "pool_scale", "cross_norm_g", "mem_norm_g", "cq_norm_g",
         "ck_norm_g", "ffn_norm_g")
FULL = {
    "w_in": ((D_MODEL, IN_W), 1, IN_W // N_DEV),
    "w_out": ((D_MODEL, D_MODEL), 0, D_MODEL // N_DEV),
    "w_cq": ((D_MODEL, X_W), 0, D_MODEL // N_DEV),
    "w_ckv": ((D_MODEL, 2 * X_W), 0, D_MODEL // N_DEV),
    "w_co": ((X_W, D_MODEL), 1, D_MODEL // N_DEV),
    "w_gate_up": ((D_MODEL, 2 * FF_PAD), 1, FF_TILE),
    "w_down": ((FF_PAD, D_MODEL), 0, DOWN_SHARD),
}


def _shard_shape(name):
    shape, axis, width = FULL[name]
    return tuple(width if a == axis else n for a, n in enumerate(shape))


def _window(ref, name, dev):
    _, axis, width = FULL[name]
    if name == "w_down":
        start = pl.multiple_of((dev // 2) * FF_TILE + (dev % 2) * DOWN_SHARD, HALO)
    else:
        start = pl.multiple_of(dev * width, BLOCK)
    return ref.at[pl.ds(start, width), :] if axis == 0 else ref.at[:, pl.ds(start, width)]


def _mesh_place():
    x, y, c = lax.axis_index("x"), lax.axis_index("y"), lax.axis_index("c")
    return x, y, c, 4 * x + 2 * y + c


def _peer(x, y, c, k):
    px = 1 - x if k & 4 else x
    py = 1 - y if k & 2 else y
    pc = 1 - c if k & 1 else c
    return (px, py, pc), 4 * px + 2 * py + pc


HBM_SPEC = pl.BlockSpec(memory_space=pltpu.HBM)


def _comm_sems(n):
    return [pltpu.SemaphoreType.DMA((n, N_DEV - 1)), pltpu.SemaphoreType.DMA((n, N_DEV - 1)),
            pltpu.SemaphoreType.DMA((n,))]


def _gather_copies(names, ins, outs, send_sems, recv_sems, local_sems):
    x, y, c, me = _mesh_place()
    local, sent, recv = [], [], []
    for wi, name in enumerate(names):
        local.append(pltpu.make_async_copy(ins[wi], _window(outs[wi], name, me), local_sems.at[wi]))
        for k in range(1, N_DEV):
            peer, pidx = _peer(x, y, c, k)
            sems = dict(send_sem=send_sems.at[wi, k - 1], recv_sem=recv_sems.at[wi, k - 1], device_id=peer,
                        device_id_type=MESH)
            sent.append(pltpu.make_async_remote_copy(src_ref=ins[wi], dst_ref=_window(outs[wi], name, me), **sems))
            recv.append(pltpu.make_async_remote_copy(src_ref=ins[wi], dst_ref=_window(outs[wi], name, pidx), **sems))
    return local, sent, recv


def _exchange_copies(names, ins, outs, send_sems, recv_sems, local_sems):
    x, y, c, me = _mesh_place()
    local, sent, recv = [], [], []
    for wi, name in enumerate(names):
        local.append(pltpu.make_async_copy(_window(ins[wi], name, me), outs[wi].at[0], local_sems.at[wi]))
        for k in range(1, N_DEV):
            peer, pidx = _peer(x, y, c, k)
            sems = dict(send_sem=send_sems.at[wi, k - 1], recv_sem=recv_sems.at[wi, k - 1], device_id=peer,
                        device_id_type=MESH)
            sent.append(pltpu.make_async_remote_copy(src_ref=_window(ins[wi], name, pidx), dst_ref=outs[wi].at[k], **sems))
            recv.append(pltpu.make_async_remote_copy(src_ref=_window(ins[wi], name, me), dst_ref=outs[wi].at[k], **sems))
    return local, sent, recv


def _down_pads(down_ref, zero_ref, zero_sems):
    return [pltpu.make_async_copy(zero_ref, down_ref.at[pl.ds(t * FF_TILE + FF_SHARD, FF_TILE - FF_SHARD), :],
                                  zero_sems.at[t]) for t in range(FF_PAD // FF_TILE)]


def _gather_scratch(names):
    if not names:
        return []
    pad = [pltpu.VMEM((FF_TILE - FF_SHARD, D_MODEL), BF16), pltpu.SemaphoreType.DMA((FF_PAD // FF_TILE,))]
    return _comm_sems(len(names)) + (pad if "w_down" in names else [])


def _gather_ops(names, ins, outs, scratch):
    copies = _gather_copies(names, ins, outs, *scratch[:3])
    pads = _down_pads(outs[names.index("w_down")], scratch[3], scratch[4]) if "w_down" in names else []

    def start():
        if pads:
            scratch[3][...] = jnp.zeros_like(scratch[3])
        _start(copies)
        for cp in pads:
            cp.start()

    def finish():
        _finish(copies)
        for cp in pads:
            cp.wait()

    return start, finish


def _start(copies):
    local, sent, _ = copies
    for cp in local + sent:
        cp.start()


def _finish(copies):
    local, sent, recv = copies
    for cp in recv:
        cp.wait_recv()
    for cp in sent:
        cp.wait_send()
    for cp in local:
        cp.wait()


def _gather_weights(shards):
    names = tuple(shards)
    nw = len(names)

    def body(*refs):
        copies = _gather_copies(names, refs[:nw], refs[nw:2 * nw], *refs[2 * nw:])
        _start(copies)
        _finish(copies)

    outs = _pcall(
        body, name="gather_weights",
        in_specs=[HBM_SPEC] * nw, out_specs=[HBM_SPEC] * nw,
        out_shape=[SDS(FULL[n][0], BF16) for n in names],
        scratch_shapes=_comm_sems(nw))(*[shards[n] for n in names])
    return dict(zip(names, outs))


def _exchange_small(small):
    def body(small_ref, out_ref, send_sems, recv_sems, local_sems):
        x, y, c, me = _mesh_place()
        local = pltpu.make_async_copy(small_ref, out_ref.at[me], local_sems.at[0])
        local.start()
        sent = []
        for k in range(1, N_DEV):
            peer, _ = _peer(x, y, c, k)
            cp = pltpu.make_async_remote_copy(
                src_ref=small_ref, dst_ref=out_ref.at[me], send_sem=send_sems.at[0, k - 1],
                recv_sem=recv_sems.at[0, k - 1], device_id=peer, device_id_type=MESH)
            cp.start()
            sent.append(cp)
        for k in range(1, N_DEV):
            peer, pidx = _peer(x, y, c, k)
            pltpu.make_async_remote_copy(
                src_ref=small_ref, dst_ref=out_ref.at[pidx], send_sem=send_sems.at[0, k - 1],
                recv_sem=recv_sems.at[0, k - 1], device_id=peer, device_id_type=MESH).wait_recv()
        for cp in sent:
            cp.wait_send()
        local.wait()

    return _pcall(
        body, name="exchange_small", in_specs=[HBM_SPEC], out_specs=HBM_SPEC,
        out_shape=SDS((N_DEV,) + small.shape, F32), scratch_shapes=_comm_sems(1))(small)


def _adamw(parts, w, m, v, *, name):
    r, c = w.shape
    tr = r
    for cand in (256, 128, 88):
        if r % cand == 0:
            tr = cand
            break

    def body(p_ref, w_ref, m_ref, v_ref, g_ref, d_ref, mo_ref, vo_ref):
        g = p_ref[0].astype(F32)
        for k in range(1, N_DEV):
            g = g + p_ref[k].astype(F32)
        m_new = ADAM_B1 * m_ref[...] + (1.0 - ADAM_B1) * g
        v_new = ADAM_B2 * v_ref[...] + (1.0 - ADAM_B2) * (g * g)
        m_hat = m_new / (1.0 - ADAM_B1 ** ADAM_STEP)
        v_hat = v_new / (1.0 - ADAM_B2 ** ADAM_STEP)
        g_ref[...] = g
        d_ref[...] = -ADAM_LR * (m_hat / (jnp.sqrt(v_hat) + ADAM_EPS) + ADAM_WD * w_ref[...])
        mo_ref[...] = m_new
        vo_ref[...] = v_new

    row = pl.BlockSpec((tr, c), lambda i: (i, 0))
    return _pcall(
        body, name=name, grid=(r // tr,),
        in_specs=[pl.BlockSpec((N_DEV, tr, c), lambda i: (0, i, 0)), row, row, row],
        out_specs=[row] * 4, out_shape=[SDS((r, c), F32)] * 4, compiler_params=_cp())(parts, w, m, v)


def _pack_small(d):
    parts = []
    for n in SMALL:
        a = d[n].reshape(-1, HEAD_DIM)
        parts.append(jnp.pad(a, ((0, -a.shape[0] % 8), (0, 0))))
    return jnp.concatenate(parts, axis=0)


def _unpack_small(packed, like):
    out, row = {}, 0
    for n in SMALL:
        rows = like[n].size // HEAD_DIM
        out[n] = packed[row:row + rows].reshape(like[n].shape)
        row += rows + (-rows % 8)
    return out


def _pad_cols(a, width):
    return jnp.pad(a, ((0, 0), (0, width - a.shape[1])))


def kernel(x, mem, positions, mix_norm_g, w_in, q_norm_g, k_norm_g, pool_w, pool_scale, w_out, cross_norm_g, mem_norm_g, w_cq, w_ckv, cq_norm_g, ck_norm_g, w_co, ffn_norm_g, w_gate_up, w_down, loss_target, m_mix_norm_g, m_w_in, m_q_norm_g, m_k_norm_g, m_pool_w, m_pool_scale, m_w_out, m_cross_norm_g, m_mem_norm_g, m_w_cq, m_w_ckv, m_cq_norm_g, m_ck_norm_g, m_w_co, m_ffn_norm_g, m_w_gate_up, m_w_down, v_mix_norm_g, v_w_in, v_q_norm_g, v_k_norm_g, v_pool_w, v_pool_scale, v_w_out, v_cross_norm_g, v_mem_norm_g, v_w_cq, v_w_ckv, v_cq_norm_g, v_ck_norm_g, v_w_co, v_ffn_norm_g, v_w_gate_up, v_w_down):
    given = dict(locals())
    w_f32 = {n: given[n][0] for n in SHARDED + SMALL}
    m_f32 = {n: given["m_" + n][0] for n in SHARDED + SMALL}
    v_f32 = {n: given["v_" + n][0] for n in SHARDED + SMALL}
    for d in (w_f32, m_f32, v_f32):
        d["w_gate_up"] = _pad_cols(d["w_gate_up"], FF_TILE)

    shards = {n: w_f32[n].astype(BF16) for n in SHARDED}
    w_in_full = _gather_weights({"w_in": shards.pop("w_in")})["w_in"]
    sm = {n: w_f32[n] for n in SMALL}
    sm_rows = {n: (a if a.ndim == 3 else a.reshape(1, -1)) for n, a in sm.items()}
    sq, dx, parts, gs = _local_step(x[0], mem[0], positions[0], loss_target[0], w_in_full, shards, sm_rows)
    loss = lax.psum(0.5 / D_MODEL * jnp.sum(sq), ("x", "y", "c"))

    small_parts = _exchange_small(_pack_small(gs))
    res = {}
    for n in SHARDED:
        res[n] = _adamw(parts[n], w_f32[n], m_f32[n], v_f32[n], name="adamw_" + n)
    res["w_gate_up"] = [a[:, :FF_SHARD] for a in res["w_gate_up"]]
    small_res = _adamw(small_parts, _pack_small(sm), _pack_small({n: m_f32[n] for n in SMALL}),
                       _pack_small({n: v_f32[n] for n in SMALL}), name="adamw_small")
    small_res = [_unpack_small(a, sm) for a in small_res]
    order = ("mix_norm_g", "w_in", "q_norm_g", "k_norm_g", "pool_w", "pool_scale", "w_out", "cross_norm_g",
             "mem_norm_g", "w_cq", "w_ckv", "cq_norm_g", "ck_norm_g", "w_co", "ffn_norm_g", "w_gate_up", "w_down")
    outs = [loss, dx[None]]
    for which in range(4):
        for n in order:
            a = res[n][which] if n in SHARDED else small_res[which][n]
            outs.append(a[None])
    return tuple(outs)
```

```python
import functools

import jax
import jax.numpy as jnp
from jax import lax
from jax.experimental import pallas as pl
from jax.experimental.pallas import tpu as pltpu

F32 = jnp.float32
BF16 = jnp.bfloat16
SDS = jax.ShapeDtypeStruct

D_MODEL = 1024
HEAD_DIM = 128
N_GROUPS = 3
DILATIONS = (1, 4, 16)
ATT_HEADS = 4
Q_W = 1536
KV_W = 512
POOL_W = 512
POOL_WINDOWS = (2, 4, 8, 16)
IN_W = 3072
X_W = 512
N_MEM = 256
D_FF = 2816
FF_TILE = 768
FF_SHARD = 704
FF_PAD = 4 * FF_TILE
DOWN_SHARD = 352
ROT_DIM = 32
ROT_HALF = 16
ROPE_THETA = 500000.0
EPS = 1e-6
NEG_INF = -1e30
SCALE = HEAD_DIM ** -0.5
BLOCK = 128
HALO = 16

ADAM_LR = 0.001
ADAM_B1 = 0.9
ADAM_B2 = 0.999
ADAM_EPS = 1e-08
ADAM_WD = 0.01
ADAM_STEP = 10

N_DEV = 8
VMEM_LIMIT_BYTES = 56 * 1024 * 1024
MESH = pl.DeviceIdType.MESH


def _pcall(body, **kw):
    return pl.pallas_call(body, **kw)


def _cp():
    return pltpu.CompilerParams(vmem_limit_bytes=VMEM_LIMIT_BYTES)


def _dot(a, b):
    return lax.dot_general(a, b, (((1,), (0,)), ((), ())), preferred_element_type=F32)


def _dot_nt(a, b):
    return lax.dot_general(a, b, (((1,), (1,)), ((), ())), preferred_element_type=F32)


def _dot_tn(a, b):
    return lax.dot_general(a, b, (((0,), (0,)), ((), ())), preferred_element_type=F32)


def _rows(s):
    return min(512, s)


def _rms_r(x):
    return lax.rsqrt(jnp.mean(x * x, axis=-1, keepdims=True) + EPS)


def _norm_bwd(x, r, gain, dxn):
    z = dxn * gain
    dx = r * z - x * (r * r * r * jnp.mean(z * x, axis=-1, keepdims=True))
    dgain = jnp.sum(dxn * x * r, axis=0, keepdims=True)
    return dx, dgain


def _partner(t):
    lane = lax.broadcasted_iota(jnp.int32, t.shape, 1)
    return jnp.where(lane < ROT_HALF, pltpu.roll(t, HEAD_DIM - ROT_HALF, 1), pltpu.roll(t, ROT_HALF, 1))


def _rope(n, cos_t, sin_t):
    return n * cos_t + _partner(n) * sin_t


def _rope_bwd(d, cos_t, sin_t):
    lane = lax.broadcasted_iota(jnp.int32, d.shape, 1)
    return d * cos_t + jnp.where(lane < ROT_DIM, _partner(d * sin_t), 0.0)


def _resident(shape):
    return pl.BlockSpec(shape, lambda i: (0,) * len(shape), pipeline_mode=pl.Buffered(1))


def _chunks(n, cn):
    return [slice(j * cn, (j + 1) * cn) for j in range(n // cn)]


def _norm_linear(x, gain, w, *, cn, out_dtype, name, shards=None):
    s, k = x.shape
    n = w.shape[1]
    tm = _rows(s)
    names = tuple(shards or ())
    nr = len(names)

    def body(*refs):
        x_ref, g_ref, w_ref = refs[:3]
        y_ref, xn_ref = refs[3 + nr:5 + nr]
        if nr:
            start, finish = _gather_ops(names, refs[3:3 + nr], refs[5 + nr:5 + 2 * nr], refs[5 + 2 * nr:])
            pl.when(pl.program_id(0) == 0)(start)
        xv = x_ref[...]
        xn_ref[...] = (xv * _rms_r(xv) * g_ref[...]).astype(BF16)
        for c in _chunks(n, cn):
            y_ref[:, c] = _dot(xn_ref[...], w_ref[:, c]).astype(out_dtype)
        if nr:
            pl.when(pl.program_id(0) == s // tm - 1)(finish)

    row = lambda w_: pl.BlockSpec((tm, w_), lambda i: (i, 0))
    outs = _pcall(
        body, name=name, grid=(s // tm,),
        in_specs=[row(k), _resident((1, k)), _resident((k, n))] + [HBM_SPEC] * nr,
        out_specs=[row(n), row(k)] + [HBM_SPEC] * nr,
        out_shape=[SDS((s, n), out_dtype), SDS((s, k), BF16)] + [SDS(FULL[a][0], BF16) for a in names],
        scratch_shapes=_gather_scratch(names),
        compiler_params=_cp())(x, gain, w, *[shards[a] for a in names])
    return (outs[0], outs[1], dict(zip(names, outs[2:]))) if nr else tuple(outs)


def _norm_linear_swiglu(x, gain, wgu, *, name):
    s, k = x.shape
    tm = _rows(s)

    def body(x_ref, g_ref, w_ref, gu_ref, a_ref, xn_ref):
        xv = x_ref[...]
        xn_ref[...] = (xv * _rms_r(xv) * g_ref[...]).astype(BF16)
        for c in _chunks(FF_PAD, FF_TILE):
            g = _dot(xn_ref[...], w_ref[:, c])
            u = _dot(xn_ref[...], w_ref[:, slice(FF_PAD + c.start, FF_PAD + c.stop)])
            a_ref[:, c] = (g * jax.nn.sigmoid(g) * u).astype(BF16)
            gu_ref[0, :, c] = g.astype(BF16)
            gu_ref[1, :, c] = u.astype(BF16)

    row = lambda w_: pl.BlockSpec((tm, w_), lambda i: (i, 0))
    return _pcall(
        body, name=name, grid=(s // tm,),
        in_specs=[row(k), _resident((1, k)), _resident((k, 2 * FF_PAD))],
        out_specs=[pl.BlockSpec((2, tm, FF_PAD), lambda i: (0, i, 0)), row(FF_PAD), row(k)],
        out_shape=[SDS((2, s, FF_PAD), BF16), SDS((s, FF_PAD), BF16), SDS((s, k), BF16)],
        compiler_params=_cp())(x, gain, wgu)


def _linear_res(a, w, res, *, cn, name):
    s, k = a.shape
    n = w.shape[1]
    tm = _rows(s)

    def body(a_ref, w_ref, r_ref, y_ref):
        for c in _chunks(n, cn):
            y_ref[:, c] = r_ref[:, c] + _dot(a_ref[...], w_ref[:, c])

    row = lambda w_: pl.BlockSpec((tm, w_), lambda i: (i, 0))
    return _pcall(
        body, name=name, grid=(s // tm,),
        in_specs=[row(k), _resident((k, n)), row(n)], out_specs=row(n),
        out_shape=SDS((s, n), F32), compiler_params=_cp())(a, w, res)


def _linear_res_loss(a, w, res, tgt, *, name):
    s, k = a.shape
    n = w.shape[1]
    tm = _rows(s)

    def body(a_ref, w_ref, r_ref, t_ref, dy_ref, dyb_ref, sq_ref):
        e = r_ref[...] + _dot(a_ref[...], w_ref[...]) - t_ref[...]
        dy = e * (1.0 / n)
        dy_ref[...] = dy
        dyb_ref[...] = dy.astype(BF16)

        @pl.when(pl.program_id(0) == 0)
        def _():
            sq_ref[...] = jnp.zeros_like(sq_ref)
        sq_ref[...] += jnp.sum(e * e, axis=0, keepdims=True)

    row = lambda w_: pl.BlockSpec((tm, w_), lambda i: (i, 0))
    return _pcall(
        body, name=name, grid=(s // tm,),
        in_specs=[row(k), _resident((k, n)), row(n), row(n)],
        out_specs=[row(n), row(n), pl.BlockSpec((1, n), lambda i: (0, 0))],
        out_shape=[SDS((s, n), F32), SDS((s, n), BF16), SDS((1, n), F32)],
        compiler_params=_cp())(a, w, res, tgt)


def _linear_nt(g, w, *, cn, name):
    s, k = g.shape
    n = w.shape[0]
    tm = _rows(s)

    def body(g_ref, w_ref, y_ref):
        for c in _chunks(n, cn):
            y_ref[:, c] = _dot_nt(g_ref[...], w_ref[c, :]).astype(BF16)

    row = lambda w_: pl.BlockSpec((tm, w_), lambda i: (i, 0))
    return _pcall(
        body, name=name, grid=(s // tm,),
        in_specs=[row(k), _resident((n, k))], out_specs=row(n),
        out_shape=SDS((s, n), BF16), compiler_params=_cp())(g, w)


def _swiglu_bwd(dyb, wd, gu, *, name):
    s, n = dyb.shape
    tm = _rows(s)

    def body(dy_ref, wd_ref, gu_ref, dgu_ref):
        for c in _chunks(FF_PAD, FF_TILE):
            da = _dot_nt(dy_ref[...], wd_ref[c, :])
            g = gu_ref[0, :, c].astype(F32)
            u = gu_ref[1, :, c].astype(F32)
            sg = jax.nn.sigmoid(g)
            dgu_ref[0, :, c] = (da * u * (sg * (1.0 + g * (1.0 - sg)))).astype(BF16)
            dgu_ref[1, :, c] = (da * (g * sg)).astype(BF16)

    half = pl.BlockSpec((2, tm, FF_PAD), lambda i: (0, i, 0))
    return _pcall(
        body, name=name, grid=(s // tm,),
        in_specs=[pl.BlockSpec((tm, n), lambda i: (i, 0)), _resident((FF_PAD, n)), half],
        out_specs=half, out_shape=SDS((2, s, FF_PAD), BF16), compiler_params=_cp())(dyb, wd, gu)


def _linear_nt_normbwd(g, w, x, dres, gain, *, name, grads=None):
    d, k = w.shape
    s = x.shape[0]
    tm = _rows(s)
    names = tuple(grads or ())
    nr = len(names)

    def body(*refs):
        g_ref, w_ref, x_ref, dr_ref, gn_ref = refs[:5]
        dx_ref, dxb_ref, dg_ref = refs[5 + nr:8 + nr]
        if nr:
            copies = _exchange_copies(names, refs[5:5 + nr], refs[8 + nr:8 + 2 * nr], *refs[8 + 2 * nr:])

        @pl.when(pl.program_id(0) == 0)
        def _():
            dg_ref[...] = jnp.zeros_like(dg_ref)
            if nr:
                _start(copies)

        if g.ndim == 3:
            dxn = _dot_nt(g_ref[0], w_ref[:, :k // 2]) + _dot_nt(g_ref[1], w_ref[:, k // 2:])
        else:
            dxn = _dot_nt(g_ref[...], w_ref[...])
        xv = x_ref[...]
        dx, dgain = _norm_bwd(xv, _rms_r(xv), gn_ref[...], dxn)
        out = dr_ref[...] + dx
        dx_ref[...] = out
        dxb_ref[...] = out.astype(BF16)
        dg_ref[...] += dgain

        if nr:
            @pl.when(pl.program_id(0) == s // tm - 1)
            def _():
                _finish(copies)

    row = pl.BlockSpec((tm, d), lambda i: (i, 0))
    g_spec = (pl.BlockSpec((2, tm, k // 2), lambda i: (0, i, 0)) if g.ndim == 3
              else pl.BlockSpec((tm, k), lambda i: (i, 0)))
    outs = _pcall(
        body, name=name, grid=(s // tm,),
        in_specs=[g_spec, _resident((d, k)), row, row, _resident((1, d))] + [HBM_SPEC] * nr,
        out_specs=[row, row, pl.BlockSpec((1, d), lambda i: (0, 0))] + [HBM_SPEC] * nr,
        out_shape=[SDS((s, d), F32), SDS((s, d), BF16), SDS((1, d), F32)]
        + [SDS((N_DEV,) + _shard_shape(n), BF16) for n in names],
        scratch_shapes=_comm_sems(nr) if nr else [],
        compiler_params=_cp())(g, w, x, dres, gain, *[grads[n] for n in names])
    return (outs[0], outs[1], outs[2], dict(zip(names, outs[3:]))) if nr else tuple(outs)


def _dw_tn(x, g, *, tkw, tn, name):
    s, kw = x.shape
    halves = g.ndim == 3
    n = 2 * g.shape[2] if halves else g.shape[1]
    ts = _rows(s)
    ns = s // ts
    per_half = n // 2 // tn

    def body(x_ref, g_ref, o_ref, acc_ref):
        ss = pl.program_id(2)

        @pl.when(ss == 0)
        def _():
            acc_ref[...] = jnp.zeros_like(acc_ref)

        acc_ref[...] += _dot_tn(x_ref[...], g_ref[...])

        @pl.when(ss == ns - 1)
        def _():
            o_ref[...] = acc_ref[...].astype(BF16)

    g_spec = (pl.BlockSpec((None, ts, tn), lambda a, b, ss: (b // per_half, ss, b % per_half)) if halves
              else pl.BlockSpec((ts, tn), lambda a, b, ss: (ss, b)))
    return _pcall(
        body, name=name, grid=(kw // tkw, n // tn, ns),
        in_specs=[pl.BlockSpec((ts, tkw), lambda a, b, ss: (ss, a)), g_spec],
        out_specs=pl.BlockSpec((tkw, tn), lambda a, b, ss: (a, b)),
        out_shape=SDS((kw, n), BF16),
        scratch_shapes=[pltpu.VMEM((tkw, tn), F32)], compiler_params=_cp())(x, g)


def _qk_prep(proj, cos_t, sin_t, qg, kg, shards, *, name):
    s = proj.shape[0]
    tm = _rows(s)
    nqh = Q_W // HEAD_DIM
    names = tuple(shards)
    nr = len(names)

    def body(*refs):
        q_ref, k_ref, v_ref, c_ref, s_ref, qg_ref, kg_ref = refs[:7]
        qo_ref, ko_ref, vo_ref = refs[7 + nr:10 + nr]
        start, finish = _gather_ops(names, refs[7:7 + nr], refs[10 + nr:10 + 2 * nr], refs[10 + 2 * nr:])
        pl.when(pl.program_id(0) == 0)(start)
        c, sn = c_ref[...], s_ref[...]
        for h in range(nqh):
            sl = slice(h * HEAD_DIM, (h + 1) * HEAD_DIM)
            y = q_ref[:, sl]
            qo_ref[:, sl] = (_rope(y * _rms_r(y) * qg_ref[...], c, sn) * SCALE).astype(BF16)
        for h in range(ATT_HEADS):
            sl = slice(h * HEAD_DIM, (h + 1) * HEAD_DIM)
            y = k_ref[:, sl]
            ko_ref[:, sl] = _rope(y * _rms_r(y) * kg_ref[...], c, sn).astype(BF16)
        vo_ref[...] = v_ref[...].astype(BF16)
        pl.when(pl.program_id(0) == s // tm - 1)(finish)

    row = lambda w, j: pl.BlockSpec((tm, w), lambda i: (i, j))
    one = pl.BlockSpec((1, HEAD_DIM), lambda i: (0, 0))
    outs = _pcall(
        body, name=name, grid=(s // tm,),
        in_specs=[row(Q_W, 0), row(KV_W, 3), row(KV_W, 4), row(HEAD_DIM, 0), row(HEAD_DIM, 0), one, one]
        + [HBM_SPEC] * nr,
        out_specs=[row(Q_W, 0), row(KV_W, 0), row(KV_W, 0)] + [HBM_SPEC] * nr,
        out_shape=[SDS((s, Q_W), BF16), SDS((s, KV_W), BF16), SDS((s, KV_W), BF16)]
        + [SDS(FULL[a][0], BF16) for a in names],
        scratch_shapes=_gather_scratch(names),
        compiler_params=_cp())(proj, proj, proj, cos_t, sin_t, qg, kg, *[shards[a] for a in names])
    return outs[0], outs[1], outs[2], dict(zip(names, outs[3:]))


def _qk_prep_bwd(proj, dq, dk, dv, du, cos_t, sin_t, qg, kg, *, name):
    s = proj.shape[0]
    tm = _rows(s)

    def body(q_ref, k_ref, dq_ref, dk_ref, dv_ref, du_ref, c_ref, s_ref, qg_ref, kg_ref, dp_ref, dqg_ref, dkg_ref):
        c, sn = c_ref[...], s_ref[...]

        @pl.when(pl.program_id(0) == 0)
        def _():
            dqg_ref[...] = jnp.zeros_like(dqg_ref)
            dkg_ref[...] = jnp.zeros_like(dkg_ref)

        dqg = jnp.zeros((1, HEAD_DIM), F32)
        for g in range(N_GROUPS):
            for h in range(ATT_HEADS):
                sl = slice(h * HEAD_DIM, (h + 1) * HEAD_DIM)
                col = slice(g * KV_W + h * HEAD_DIM, g * KV_W + (h + 1) * HEAD_DIM)
                y = q_ref[:, col]
                dn = _rope_bwd(dq_ref[g, :, sl].astype(F32) * SCALE, c, sn)
                dy, dgain = _norm_bwd(y, _rms_r(y), qg_ref[...], dn)
                dp_ref[:, col] = dy.astype(BF16)
                dqg = dqg + dgain
        dqg_ref[...] += dqg

        dkg = jnp.zeros((1, HEAD_DIM), F32)
        for h in range(ATT_HEADS):
            sl = slice(h * HEAD_DIM, (h + 1) * HEAD_DIM)
            y = k_ref[:, sl]
            dn = _rope_bwd(dk_ref[:, sl], c, sn)
            dy, dgain = _norm_bwd(y, _rms_r(y), kg_ref[...], dn)
            dp_ref[:, Q_W + h * HEAD_DIM:Q_W + (h + 1) * HEAD_DIM] = dy.astype(BF16)
            dkg = dkg + dgain
        dkg_ref[...] += dkg

        dp_ref[:, Q_W + KV_W:Q_W + 2 * KV_W] = dv_ref[...].astype(BF16)
        dp_ref[:, Q_W + 2 * KV_W:] = du_ref[...].astype(BF16)

    row = lambda w, j: pl.BlockSpec((tm, w), lambda i: (i, j))
    one = pl.BlockSpec((1, HEAD_DIM), lambda i: (0, 0))
    return _pcall(
        body, name=name, grid=(s // tm,),
        in_specs=[row(Q_W, 0), row(KV_W, 3), pl.BlockSpec((N_GROUPS, tm, KV_W), lambda i: (0, i, 0))]
        + [row(KV_W, 0)] * 3 + [row(HEAD_DIM, 0), row(HEAD_DIM, 0), one, one],
        out_specs=[row(IN_W, 0), one, one],
        out_shape=[SDS((s, IN_W), BF16), SDS((1, HEAD_DIM), F32), SDS((1, HEAD_DIM), F32)],
        compiler_params=_cp())(proj, proj, dq, dk, dv, du, cos_t, sin_t, qg, kg)


ATT_ROWS = 16 * BLOCK


def _sub(ref, start, d, size=BLOCK):
    return ref[pl.ds(start, size, stride=d), :] if d > 1 else ref[pl.ds(start, size), :]


def _sub_set(ref, start, d, val):
    if d > 1:
        ref[pl.ds(start, BLOCK, stride=d), :] = val
    else:
        ref[pl.ds(start, BLOCK), :] = val


def _band_masks():
    row = lax.broadcasted_iota(jnp.int32, (BLOCK, BLOCK), 0)
    col = lax.broadcasted_iota(jnp.int32, (BLOCK, BLOCK), 1)
    return col <= row, col >= row


def _eye(n=BLOCK):
    row = lax.broadcasted_iota(jnp.int32, (n, n), 0)
    col = lax.broadcasted_iota(jnp.int32, (n, n), 1)
    return jnp.where(row == col, 1.0, 0.0).astype(BF16)


def _transpose_mxu(eye, a):
    return _dot_nt(eye, a).astype(BF16)


def _attn_fwd(q_rot, k_rot, v, shards, *, name):
    s = q_rot.shape[0]
    rr = ATT_ROWS
    nblk = s // rr
    names = tuple(shards)
    nr = len(names)

    def body(*refs):
        q0, q1, q2, kp, kc, vp, vc = refs[:7]
        mix_ref, lse_ref = refs[7 + nr:9 + nr]
        qs, ks, vs, os_, ls = refs[9 + 2 * nr:14 + 2 * nr]
        h, n = pl.program_id(0), pl.program_id(1)
        start, finish = _gather_ops(names, refs[7:7 + nr], refs[9 + nr:9 + 2 * nr], refs[14 + 2 * nr:])
        pl.when((h == 0) & (n == 0))(start)
        for g, q_ref in enumerate((q0, q1, q2)):
            qs[g] = q_ref[...].astype(F32)
        ks[:rr] = kp[...].astype(F32)
        ks[rr:] = kc[...].astype(F32)
        vs[:rr] = vp[...].astype(F32)
        vs[rr:] = vc[...].astype(F32)
        m_cur, m_band = _band_masks()
        mask_in = jnp.concatenate([m_band, m_cur], axis=1)
        mask_first = jnp.concatenate([m_band & (n > 0), m_cur], axis=1)
        ones = jnp.ones((2 * BLOCK, HEAD_DIM), BF16)
        pieces = [(g, d, j * BLOCK * d + r, j) for g, d in enumerate(DILATIONS) for r in range(d)
                  for j in range(rr // (BLOCK * d))]

        def scores(piece):
            g, d, base, j = piece
            q = _sub(qs.at[g], base, d).astype(BF16)
            k2 = _sub(ks, rr + base - BLOCK * d, d, 2 * BLOCK).astype(BF16)
            return jnp.where(mask_first if j == 0 else mask_in, _dot_nt(q, k2), NEG_INF)

        sc = scores(pieces[0])
        for i, (g, d, base, j) in enumerate(pieces):
            cur = sc
            if i + 1 < len(pieces):
                sc = scores(pieces[i + 1])
            m = jnp.max(cur, axis=-1, keepdims=True)
            p = jnp.exp(cur - m).astype(BF16)
            v2 = _sub(vs, rr + base - BLOCK * d, d, 2 * BLOCK).astype(BF16)
            acc_l = _dot(p, jnp.concatenate([v2, ones], axis=1))
            l = acc_l[:, HEAD_DIM:]
            _sub_set(os_.at[g], base, d, acc_l[:, :HEAD_DIM] / l)
            _sub_set(ls.at[g], base, d, m + jnp.log(l))
        for c in _chunks(rr, 2 * BLOCK):
            a, b, cc = ls[0, c, :], ls[1, c, :], ls[2, c, :]
            m = jnp.maximum(jnp.maximum(a, b), cc)
            wa, wb, wc = jnp.exp(a - m), jnp.exp(b - m), jnp.exp(cc - m)
            den = wa + wb + wc
            mix_ref[c, :] = ((wa * os_[0, c, :] + wb * os_[1, c, :] + wc * os_[2, c, :]) / den).astype(BF16)
            lse_ref[c, :] = m + jnp.log(den)

        pl.when((h == ATT_HEADS - 1) & (n == nblk - 1))(finish)

    blk = lambda f: pl.BlockSpec((rr, HEAD_DIM), f)
    prv = lambda n: jnp.maximum(n - 1, 0)
    outs = _pcall(
        body, name=name, grid=(ATT_HEADS, nblk),
        in_specs=[blk(lambda h, n, g=g: (n, g * ATT_HEADS + h)) for g in range(N_GROUPS)]
        + [blk(lambda h, n: (prv(n), h)), blk(lambda h, n: (n, h))] * 2 + [HBM_SPEC] * nr,
        out_specs=[blk(lambda h, n: (n, h)), blk(lambda h, n: (n, h))] + [HBM_SPEC] * nr,
        out_shape=[SDS((s, KV_W + POOL_W), BF16), SDS((s, KV_W), F32)] + [SDS(FULL[w][0], BF16) for w in names],
        scratch_shapes=[pltpu.VMEM((N_GROUPS, rr, HEAD_DIM), F32), pltpu.VMEM((2 * rr, HEAD_DIM), F32),
                        pltpu.VMEM((2 * rr, HEAD_DIM), F32), pltpu.VMEM((N_GROUPS, rr, HEAD_DIM), F32),
                        pltpu.VMEM((N_GROUPS, rr, HEAD_DIM), F32)] + _gather_scratch(names),
        compiler_params=_cp())(q_rot, q_rot, q_rot, k_rot, k_rot, v, v, *[shards[w] for w in names])
    return outs[0], outs[1], dict(zip(names, outs[2:]))


def _attn_bwd(q_rot, k_rot, v, mix, dmix, lse, grads, *, name):
    s = q_rot.shape[0]
    rr = ATT_ROWS
    nblk = s // rr
    names = tuple(grads)
    nr = len(names)

    def body(*refs):
        q0, q1, q2, qx0, qx1, qx2, kp, kc, vp, vc, do_c, do_x, o_c, o_x, l_c, l_x = refs[:16]
        dq_ref, dk_ref, dv_ref = refs[16 + nr:19 + nr]
        qs, ks, vs, dos, lss, dls, dqs, dks, dvs, send_sems, recv_sems, local_sems = refs[19 + 2 * nr:]
        h, n = pl.program_id(0), pl.program_id(1)
        copies = _exchange_copies(names, refs[16:16 + nr], refs[19 + nr:19 + 2 * nr], send_sems, recv_sems, local_sems)

        @pl.when((h == 0) & (n == 0))
        def _():
            _start(copies)

        for g, (qc_ref, qx_ref) in enumerate(((q0, qx0), (q1, qx1), (q2, qx2))):
            qs[g, :rr] = qc_ref[...].astype(F32)
            qs[g, rr:] = qx_ref[...].astype(F32)
        ks[:rr] = kp[...].astype(F32)
        ks[rr:] = kc[...].astype(F32)
        vs[:rr] = vp[...].astype(F32)
        vs[rr:] = vc[...].astype(F32)
        lss[:rr] = l_c[...]
        lss[rr:] = l_x[...]
        for half, (d_ref, o_ref) in enumerate(((do_c, o_c), (do_x, o_x))):
            for c in _chunks(rr, 2 * BLOCK):
                cs = slice(half * rr + c.start, half * rr + c.stop)
                dof = d_ref[c, :].astype(F32)
                dos[cs, :] = dof
                dls[cs, :] = jnp.broadcast_to(jnp.sum(dof * o_ref[c, :].astype(F32), axis=-1, keepdims=True),
                                              (2 * BLOCK, HEAD_DIM))
        m_cur, m_band = _band_masks()
        mask_in = jnp.concatenate([m_cur, m_band], axis=0)
        mask_last = jnp.concatenate([m_cur, m_band & (n + 1 < nblk)], axis=0)
        m_first = m_band & (n > 0)
        eye = _eye()
        pieces = [(g, d, j * BLOCK * d + r, j, rr // (BLOCK * d)) for g, d in enumerate(DILATIONS) for r in range(d)
                  for j in range(rr // (BLOCK * d))]

        def front(piece):
            g, d, base, _, _ = piece
            q2 = _sub(qs.at[g], base, d, 2 * BLOCK).astype(BF16)
            do2 = _sub(dos, base, d, 2 * BLOCK).astype(BF16)
            k = _sub(ks, rr + base, d).astype(BF16)
            vv = _sub(vs, rr + base, d).astype(BF16)
            return q2, do2, k, _dot_nt(q2, k), _dot_nt(do2, vv)

        nxt_front = front(pieces[0])
        dq_acc = None
        for i, (g, d, base, j, nsub) in enumerate(pieces):
            q2, do2, k, s2, dp2 = nxt_front
            if j == 0:
                kp_ = _sub(ks, rr + base - BLOCK * d, d).astype(BF16)
                p0 = jnp.where(m_first, jnp.exp(_dot_nt(q2[:BLOCK], kp_) - _sub(lss, base, d)), 0.0)
                ds0 = p0 * (_dot_nt(do2[:BLOCK], _sub(vs, rr + base - BLOCK * d, d).astype(BF16)) - _sub(dls, base, d))
                dq_acc = _dot(ds0.astype(BF16), kp_)
            if i + 1 < len(pieces):
                nxt_front = front(pieces[i + 1])
            p2 = jnp.where(mask_last if j + 1 == nsub else mask_in,
                           jnp.exp(s2 - _sub(lss, base, d, 2 * BLOCK)), 0.0)
            ds2 = (p2 * (dp2 - _sub(dls, base, d, 2 * BLOCK))).astype(BF16)
            dq2 = _dot(ds2, k)
            _sub_set(dqs.at[g], base, d, dq_acc + dq2[:BLOCK])
            dq_acc = dq2[BLOCK:]
            dk = _dot(_transpose_mxu(eye, ds2), q2)
            dv = _dot(_transpose_mxu(eye, p2.astype(BF16)), do2)
            if g == 0:
                _sub_set(dks, base, d, dk)
                _sub_set(dvs, base, d, dv)
            else:
                _sub_set(dks, base, d, _sub(dks, base, d) + dk)
                _sub_set(dvs, base, d, _sub(dvs, base, d) + dv)
        for g in range(N_GROUPS):
            dq_ref[g] = dqs[g].astype(BF16)
        dk_ref[...] = dks[...]
        dv_ref[...] = dvs[...]

        @pl.when((h == ATT_HEADS - 1) & (n == nblk - 1))
        def _():
            _finish(copies)

    blk = lambda f: pl.BlockSpec((rr, HEAD_DIM), f)
    prv = lambda n: jnp.maximum(n - 1, 0)
    nxt = lambda n: jnp.minimum(n + 1, nblk - 1)
    cur_kv = blk(lambda h, n: (n, h))
    outs = _pcall(
        body, name=name, grid=(ATT_HEADS, nblk),
        in_specs=[blk(lambda h, n, g=g: (n, g * ATT_HEADS + h)) for g in range(N_GROUPS)]
        + [blk(lambda h, n, g=g: (nxt(n), g * ATT_HEADS + h)) for g in range(N_GROUPS)]
        + [blk(lambda h, n: (prv(n), h)), cur_kv] * 2
        + [cur_kv, blk(lambda h, n: (nxt(n), h))] * 3 + [HBM_SPEC] * nr,
        out_specs=[pl.BlockSpec((N_GROUPS, rr, HEAD_DIM), lambda h, n: (0, n, h)), cur_kv, cur_kv] + [HBM_SPEC] * nr,
        out_shape=[SDS((N_GROUPS, s, KV_W), BF16), SDS((s, KV_W), F32), SDS((s, KV_W), F32)]
        + [SDS((N_DEV,) + _shard_shape(w), BF16) for w in names],
        scratch_shapes=[pltpu.VMEM((N_GROUPS, 2 * rr, HEAD_DIM), F32)] + [pltpu.VMEM((2 * rr, HEAD_DIM), F32)] * 5
        + [pltpu.VMEM((N_GROUPS, rr, HEAD_DIM), F32), pltpu.VMEM((rr, HEAD_DIM), F32), pltpu.VMEM((rr, HEAD_DIM), F32)]
        + _comm_sems(nr),
        compiler_params=_cp())(q_rot, q_rot, q_rot, q_rot, q_rot, q_rot, k_rot, k_rot, v, v, dmix, dmix, mix, mix,
                               lse, lse, *[grads[w] for w in names])
    return outs[0], outs[1], outs[2], dict(zip(names, outs[3:]))


def _pool_d(u_ref, halo_ref, i, tm):
    halo = jnp.where(i > 0, halo_ref[...], 0.0)
    t = i * tm + lax.broadcasted_iota(jnp.int32, (tm, 1), 0)
    out = []
    for g, w in enumerate(POOL_WINDOWS):
        sl = slice(g * HEAD_DIM, (g + 1) * HEAD_DIM)
        u = u_ref[:, sl]
        acc = jnp.concatenate([halo[:, sl], u], axis=0)
        sh = 1
        while sh < w:
            acc = acc + pltpu.roll(acc, sh, 0)
            sh *= 2
        cnt = jnp.minimum(t + 1, w).astype(F32)
        out.append(acc[HALO:, :] / cnt - u)
    return out


def _pool_fwd(proj, mix, pool_w, pool_scale, *, name):
    s = proj.shape[0]
    tm = _rows(s)
    ucol = (IN_W - POOL_W) // POOL_W

    def body(u_ref, halo_ref, mix_in, w_ref, sc_ref, o_ref):
        del mix_in
        dd = _pool_d(u_ref, halo_ref, pl.program_id(0), tm)
        for g in range(len(POOL_WINDOWS)):
            sl = slice(g * HEAD_DIM, (g + 1) * HEAD_DIM)
            y = _dot(dd[g].astype(BF16), w_ref[g].astype(BF16))
            o_ref[:, sl] = (y * sc_ref[:, sl]).astype(BF16)

    return _pcall(
        body, name=name, grid=(s // tm,),
        in_specs=[pl.BlockSpec((tm, POOL_W), lambda i: (i, ucol)),
                  pl.BlockSpec((HALO, POOL_W), lambda i: (jnp.maximum(i * (tm // HALO) - 1, 0), ucol)),
                  pl.BlockSpec(memory_space=pl.ANY),
                  pl.BlockSpec((len(POOL_WINDOWS), HEAD_DIM, HEAD_DIM), lambda i: (0, 0, 0)),
                  pl.BlockSpec((1, POOL_W), lambda i: (0, 0))],
        out_specs=pl.BlockSpec((tm, POOL_W), lambda i: (i, 1)),
        out_shape=SDS(mix.shape, BF16), input_output_aliases={2: 0},
        compiler_params=_cp())(proj, proj, mix, pool_w, pool_scale)


def _pool_bwd(proj, dmix, pool_w, pool_scale, *, name):
    s = proj.shape[0]
    tm = _rows(s)
    nblk = s // tm
    ucol = (IN_W - POOL_W) // POOL_W
    ng = len(POOL_WINDOWS)

    def body(u_ref, halo_ref, dp_ref, dpn_ref, w_ref, sc_ref, du_ref, dw_ref, dsc_ref):
        i = pl.program_id(0)

        @pl.when(i == 0)
        def _():
            dw_ref[...] = jnp.zeros_like(dw_ref)
            dsc_ref[...] = jnp.zeros_like(dsc_ref)

        dd = _pool_d(u_ref, halo_ref, i, tm)
        t = i * tm + lax.broadcasted_iota(jnp.int32, (tm, 1), 0)
        dpn = jnp.where(i + 1 < nblk, dpn_ref[...].astype(F32), 0.0)
        for g, w in enumerate(POOL_WINDOWS):
            sl = slice(g * HEAD_DIM, (g + 1) * HEAD_DIM)
            wg = w_ref[g].astype(BF16)
            db = dd[g].astype(BF16)
            dp = dp_ref[:, sl].astype(F32)
            dsc_ref[:, sl] += jnp.sum(dp * _dot(db, wg), axis=0, keepdims=True)
            dy = (dp * sc_ref[:, sl]).astype(BF16)
            dw_ref[g] += _dot_tn(db, dy)
            g_d = _dot_nt(dy, wg)
            g_dn = _dot_nt((dpn[:, sl] * sc_ref[:, sl]).astype(BF16), wg)
            cnt = jnp.minimum(t + 1, w).astype(F32)
            acc = jnp.concatenate([g_d / cnt, g_dn * (1.0 / w)], axis=0)
            sh = 1
            while sh < w:
                acc = acc + pltpu.roll(acc, tm + HALO - sh, 0)
                sh *= 2
            du_ref[:, sl] = acc[:tm, :] - g_d

    nh = s // HALO
    return _pcall(
        body, name=name, grid=(nblk,),
        in_specs=[pl.BlockSpec((tm, POOL_W), lambda i: (i, ucol)),
                  pl.BlockSpec((HALO, POOL_W), lambda i: (jnp.maximum(i * (tm // HALO) - 1, 0), ucol)),
                  pl.BlockSpec((tm, POOL_W), lambda i: (i, 1)),
                  pl.BlockSpec((HALO, POOL_W), lambda i: (jnp.minimum((i + 1) * (tm // HALO), nh - 1), 1)),
                  pl.BlockSpec((ng, HEAD_DIM, HEAD_DIM), lambda i: (0, 0, 0)),
                  pl.BlockSpec((1, POOL_W), lambda i: (0, 0))],
        out_specs=[pl.BlockSpec((tm, POOL_W), lambda i: (i, 0)),
                   pl.BlockSpec((ng, HEAD_DIM, HEAD_DIM), lambda i: (0, 0, 0)),
                   pl.BlockSpec((1, POOL_W), lambda i: (0, 0))],
        out_shape=[SDS((s, POOL_W), F32), SDS((ng, HEAD_DIM, HEAD_DIM), F32), SDS((1, POOL_W), F32)],
        compiler_params=_cp())(proj, proj, dmix, dmix, pool_w, pool_scale)


def _mem_fwd(mem, mem_g, wckv, ck_g, *, name):
    def body(m_ref, g_ref, w_ref, kg_ref, mn_ref, ckr_ref, ckn_ref, cv_ref):
        mv = m_ref[...]
        mn = (mv * _rms_r(mv) * g_ref[...]).astype(BF16)
        mn_ref[...] = mn
        ckv = _dot(mn, w_ref[...])
        ckr_ref[...] = ckv[:, :X_W]
        cv_ref[...] = ckv[:, X_W:].astype(BF16)
        for h in range(ATT_HEADS):
            sl = slice(h * HEAD_DIM, (h + 1) * HEAD_DIM)
            y = ckv[:, sl]
            ckn_ref[:, sl] = (y * _rms_r(y) * kg_ref[...]).astype(BF16)

    return _pcall(
        body, name=name,
        out_shape=[SDS((N_MEM, D_MODEL), BF16), SDS((N_MEM, X_W), F32), SDS((N_MEM, X_W), BF16),
                   SDS((N_MEM, X_W), BF16)],
        compiler_params=_cp())(mem, mem_g, wckv, ck_g)


def _cross_q(cq_ref, g_ref, sl):
    y = cq_ref[:, sl]
    r = _rms_r(y)
    return y, r, y * r * g_ref[...] * SCALE


def _cross_fwd(cq_raw, ck_n, cv, cq_g, *, name):
    s = cq_raw.shape[0]
    tm = _rows(s)

    def body(cq_ref, k_ref, v_ref, g_ref, o_ref):
        for h in range(ATT_HEADS):
            sl = slice(h * HEAD_DIM, (h + 1) * HEAD_DIM)
            _, _, qn = _cross_q(cq_ref, g_ref, sl)
            sc = _dot_nt(qn.astype(BF16), k_ref[:, sl])
            p = jnp.exp(sc - jnp.max(sc, axis=-1, keepdims=True))
            p = p / jnp.sum(p, axis=-1, keepdims=True)
            o_ref[:, sl] = _dot(p.astype(BF16), v_ref[:, sl]).astype(BF16)

    full = lambda a: pl.BlockSpec(a.shape, lambda i: (0, 0))
    return _pcall(
        body, name=name, grid=(s // tm,),
        in_specs=[pl.BlockSpec((tm, X_W), lambda i: (i, 0)), full(ck_n), full(cv), full(cq_g)],
        out_specs=pl.BlockSpec((tm, X_W), lambda i: (i, 0)),
        out_shape=SDS((s, X_W), BF16), compiler_params=_cp())(cq_raw, ck_n, cv, cq_g)


def _cross_bwd(d_o, cq_raw, ck_n, cv, cq_g, *, name):
    s = cq_raw.shape[0]
    tm = _rows(s)

    def body(do_ref, cq_ref, k_ref, v_ref, g_ref, dcq_ref, dk_ref, dv_ref, dg_ref):
        @pl.when(pl.program_id(0) == 0)
        def _():
            dk_ref[...] = jnp.zeros_like(dk_ref)
            dv_ref[...] = jnp.zeros_like(dv_ref)
            dg_ref[...] = jnp.zeros_like(dg_ref)

        dg = jnp.zeros((1, HEAD_DIM), F32)
        for h in range(ATT_HEADS):
            sl = slice(h * HEAD_DIM, (h + 1) * HEAD_DIM)
            y, r, qn = _cross_q(cq_ref, g_ref, sl)
            qb = qn.astype(BF16)
            do = do_ref[:, sl]
            sc = _dot_nt(qb, k_ref[:, sl])
            p = jnp.exp(sc - jnp.max(sc, axis=-1, keepdims=True))
            p = p / jnp.sum(p, axis=-1, keepdims=True)
            dv_ref[:, sl] += _dot_tn(p.astype(BF16), do)
            dp = _dot_nt(do, v_ref[:, sl])
            ds = (p * (dp - jnp.sum(dp * p, axis=-1, keepdims=True))).astype(BF16)
            dk_ref[:, sl] += _dot_tn(ds, qb)
            dn = _dot(ds, k_ref[:, sl]) * SCALE
            dy, dgain = _norm_bwd(y, r, g_ref[...], dn)
            dcq_ref[:, sl] = dy.astype(BF16)
            dg = dg + dgain
        dg_ref[...] += dg

    full = lambda a: pl.BlockSpec(a.shape, lambda i: (0, 0))
    row = pl.BlockSpec((tm, X_W), lambda i: (i, 0))
    return _pcall(
        body, name=name, grid=(s // tm,),
        in_specs=[row, row, full(ck_n), full(cv), full(cq_g)],
        out_specs=[row, pl.BlockSpec((N_MEM, X_W), lambda i: (0, 0)), pl.BlockSpec((N_MEM, X_W), lambda i: (0, 0)),
                   pl.BlockSpec((1, HEAD_DIM), lambda i: (0, 0))],
        out_shape=[SDS((s, X_W), BF16), SDS((N_MEM, X_W), F32), SDS((N_MEM, X_W), F32), SDS((1, HEAD_DIM), F32)],
        compiler_params=_cp())(d_o, cq_raw, ck_n, cv, cq_g)


def _mem_bwd(dck_n, dcv, ck_raw, memn, mem, wckv, mem_g, ck_g, *, name):
    def body(dk_ref, dv_ref, ckr_ref, mn_ref, m_ref, w_ref, mg_ref, kg_ref, dw_ref, dmg_ref, dkg_ref, dckv_s):
        dkg = jnp.zeros((1, HEAD_DIM), F32)
        for h in range(ATT_HEADS):
            sl = slice(h * HEAD_DIM, (h + 1) * HEAD_DIM)
            y = ckr_ref[:, sl]
            dy, dgain = _norm_bwd(y, _rms_r(y), kg_ref[...], dk_ref[:, sl])
            dckv_s[:, sl] = dy.astype(BF16)
            dkg = dkg + dgain
        dkg_ref[...] = dkg
        dckv_s[:, X_W:] = dv_ref[...].astype(BF16)
        dckv = dckv_s[...]
        dw_ref[...] = _dot_tn(mn_ref[...], dckv).astype(BF16)
        dmn = _dot_nt(dckv, w_ref[...])
        mv = m_ref[...]
        dmg_ref[...] = jnp.sum(dmn * mv * _rms_r(mv), axis=0, keepdims=True)

    return _pcall(
        body, name=name,
        out_shape=[SDS((D_MODEL, 2 * X_W), BF16), SDS((1, D_MODEL), F32), SDS((1, HEAD_DIM), F32)],
        scratch_shapes=[pltpu.VMEM((N_MEM, 2 * X_W), BF16)],
        compiler_params=_cp())(dck_n, dcv, ck_raw, memn, mem, wckv, mem_g, ck_g)


def _rope_tables(pos):
    inv_freq = ROPE_THETA ** (-jnp.arange(0, ROT_DIM, 2, dtype=F32) / ROT_DIM)
    ang = pos.astype(F32)[:, None] * inv_freq
    cos, sin = jnp.cos(ang), jnp.sin(ang)
    s = pos.shape[0]
    rest = HEAD_DIM - ROT_DIM
    cos_t = jnp.concatenate([cos, cos, jnp.ones((s, rest), F32)], axis=1)
    sin_t = jnp.concatenate([-sin, sin, jnp.zeros((s, rest), F32)], axis=1)
    return cos_t, sin_t


def _local_step(x, mem, pos, tgt, w_in, shards, sm):
    cos_t, sin_t = _rope_tables(pos)
    wb = {"w_in": w_in}

    pick = lambda *names: {a: shards[a] for a in names}
    proj, xn, got = _norm_linear(x, sm["mix_norm_g"], wb["w_in"], cn=768, out_dtype=F32, name="fwd_in_proj",
                                 shards=pick("w_out", "w_cq", "w_ckv", "w_co"))
    wb.update(got)
    q_rot, k_rot, v, got = _qk_prep(proj, cos_t, sin_t, sm["q_norm_g"], sm["k_norm_g"], pick("w_down"),
                                    name="fwd_qk_prep")
    wb.update(got)
    mix, lse_all, got = _attn_fwd(q_rot, k_rot, v, pick("w_gate_up"), name="fwd_attn")
    wb.update(got)
    mix = _pool_fwd(proj, mix, sm["pool_w"], sm["pool_scale"], name="fwd_pool")
    h1 = _linear_res(mix, wb["w_out"], x, cn=512, name="fwd_out_proj")
    cq_raw, hn = _norm_linear(h1, sm["cross_norm_g"], wb["w_cq"], cn=512, out_dtype=F32, name="fwd_cq_proj")
    memn, ck_raw, ck_n, cv = _mem_fwd(mem, sm["mem_norm_g"], wb["w_ckv"], sm["ck_norm_g"], name="fwd_mem")
    co = _cross_fwd(cq_raw, ck_n, cv, sm["cq_norm_g"], name="fwd_cross")
    h2 = _linear_res(co, wb["w_co"], h1, cn=512, name="fwd_co_proj")
    gu, act, fn = _norm_linear_swiglu(h2, sm["ffn_norm_g"], wb["w_gate_up"], name="fwd_gate_up")
    dy, dyb, sq = _linear_res_loss(act, wb["w_down"], h2, tgt, name="fwd_down_loss")

    gw = {}
    gs = {}
    dgu = _swiglu_bwd(dyb, wb["w_down"], gu, name="bwd_swiglu")
    gw["w_down"] = _dw_tn(act, dyb, tkw=1024, tn=1024, name="bwd_dw_down")
    dh2, dh2b, gs["ffn_norm_g"] = _linear_nt_normbwd(dgu, wb["w_gate_up"], h2, dy, sm["ffn_norm_g"],
                                                     name="bwd_ffn_in")
    gw["w_gate_up"] = _dw_tn(fn, dgu, tkw=1024, tn=1536, name="bwd_dw_gate_up")

    d_co = _linear_nt(dh2b, wb["w_co"], cn=512, name="bwd_co_proj")
    gw["w_co"] = _dw_tn(co, dh2b, tkw=512, tn=1024, name="bwd_dw_co")
    dcq, dck_n, dcv, gs["cq_norm_g"] = _cross_bwd(d_co, cq_raw, ck_n, cv, sm["cq_norm_g"], name="bwd_cross")
    gw["w_ckv"], gs["mem_norm_g"], gs["ck_norm_g"] = _mem_bwd(dck_n, dcv, ck_raw, memn, mem, wb["w_ckv"],
                                                             sm["mem_norm_g"], sm["ck_norm_g"], name="bwd_mem")
    dh1, dh1b, gs["cross_norm_g"] = _linear_nt_normbwd(dcq, wb["w_cq"], h1, dh2, sm["cross_norm_g"],
                                                       name="bwd_cq_in")
    gw["w_cq"] = _dw_tn(hn, dcq, tkw=1024, tn=512, name="bwd_dw_cq")

    dmix = _linear_nt(dh1b, wb["w_out"], cn=512, name="bwd_out_proj")
    gw["w_out"] = _dw_tn(mix, dh1b, tkw=1024, tn=1024, name="bwd_dw_out")
    du, gs["pool_w"], gs["pool_scale"] = _pool_bwd(proj, dmix, sm["pool_w"], sm["pool_scale"], name="bwd_pool")
    dq, dk, dv, parts = _attn_bwd(q_rot, k_rot, v, mix, dmix, lse_all, gw, name="bwd_attn")
    dproj, gs["q_norm_g"], gs["k_norm_g"] = _qk_prep_bwd(proj, dq, dk, dv, du, cos_t, sin_t, sm["q_norm_g"],
                                                         sm["k_norm_g"], name="bwd_qk_prep")
    gw_in = _dw_tn(xn, dproj, tkw=1024, tn=1536, name="bwd_dw_in")
    dx, _, gs["mix_norm_g"], last = _linear_nt_normbwd(dproj, wb["w_in"], x, dh1, sm["mix_norm_g"],
                                                       name="bwd_in_proj", grads={"w_in": gw_in})
    parts.update(last)
    return sq, dx, parts, gs


SHARDED = ("w_in", "w_out", "w_cq", "w_ckv", "w_co", "w_gate_up", "w_down")
SMALL = ("mix_norm_g", "q_norm_g", "k_norm_g", "pool_w", "pool_scale", "cross_norm_g", "mem_norm_g", "cq_norm_g",
         "ck_norm_g", "ffn_norm_g")
FULL = {
    "w_in": ((D_MODEL, IN_W), 1, IN_W // N_DEV),
    "w_out": ((D_MODEL, D_MODEL), 0, D_MODEL // N_DEV),
    "w_cq": ((D_MODEL, X_W), 0, D_MODEL // N_DEV),
    "w_ckv": ((D_MODEL, 2 * X_W), 0, D_MODEL // N_DEV),
    "w_co": ((X_W, D_MODEL), 1, D_MODEL // N_DEV),
    "w_gate_up": ((D_MODEL, 2 * FF_PAD), 1, FF_TILE),
    "w_down": ((FF_PAD, D_MODEL), 0, DOWN_SHARD),
}


def _shard_shape(name):
    shape, axis, width = FULL[name]
    return tuple(width if a == axis else n for a, n in enumerate(shape))


def _window(ref, name, dev):
    _, axis, width = FULL[name]
    if name == "w_down":
        start = pl.multiple_of((dev // 2) * FF_TILE + (dev % 2) * DOWN_SHARD, HALO)
    else:
        start = pl.multiple_of(dev * width, BLOCK)
    return ref.at[pl.ds(start, width), :] if axis == 0 else ref.at[:, pl.ds(start, width)]


def _mesh_place():
    x, y, c = lax.axis_index("x"), lax.axis_index("y"), lax.axis_index("c")
    return x, y, c, 4 * x + 2 * y + c


def _peer(x, y, c, k):
    px = 1 - x if k & 4 else x
    py = 1 - y if k & 2 else y
    pc = 1 - c if k & 1 else c
    return (px, py, pc), 4 * px + 2 * py + pc


HBM_SPEC = pl.BlockSpec(memory_space=pltpu.HBM)


def _comm_sems(n):
    return [pltpu.SemaphoreType.DMA((n, N_DEV - 1)), pltpu.SemaphoreType.DMA((n, N_DEV - 1)),
            pltpu.SemaphoreType.DMA((n,))]


def _gather_copies(names, ins, outs, send_sems, recv_sems, local_sems):
    x, y, c, me = _mesh_place()
    local, sent, recv = [], [], []
    for wi, name in enumerate(names):
        local.append(pltpu.make_async_copy(ins[wi], _window(outs[wi], name, me), local_sems.at[wi]))
        for k in range(1, N_DEV):
            peer, pidx = _peer(x, y, c, k)
            sems = dict(send_sem=send_sems.at[wi, k - 1], recv_sem=recv_sems.at[wi, k - 1], device_id=peer,
                        device_id_type=MESH)
            sent.append(pltpu.make_async_remote_copy(src_ref=ins[wi], dst_ref=_window(outs[wi], name, me), **sems))
            recv.append(pltpu.make_async_remote_copy(src_ref=ins[wi], dst_ref=_window(outs[wi], name, pidx), **sems))
    return local, sent, recv


def _exchange_copies(names, ins, outs, send_sems, recv_sems, local_sems):
    x, y, c, me = _mesh_place()
    local, sent, recv = [], [], []
    for wi, name in enumerate(names):
        local.append(pltpu.make_async_copy(_window(ins[wi], name, me), outs[wi].at[0], local_sems.at[wi]))
        for k in range(1, N_DEV):
            peer, pidx = _peer(x, y, c, k)
            sems = dict(send_sem=send_sems.at[wi, k - 1], recv_sem=recv_sems.at[wi, k - 1], device_id=peer,
                        device_id_type=MESH)
            sent.append(pltpu.make_async_remote_copy(src_ref=_window(ins[wi], name, pidx), dst_ref=outs[wi].at[k], **sems))
            recv.append(pltpu.make_async_remote_copy(src_ref=_window(ins[wi], name, me), dst_ref=outs[wi].at[k], **sems))
    return local, sent, recv


def _down_pads(down_ref, zero_ref, zero_sems):
    return [pltpu.make_async_copy(zero_ref, down_ref.at[pl.ds(t * FF_TILE + FF_SHARD, FF_TILE - FF_SHARD), :],
                                  zero_sems.at[t]) for t in range(FF_PAD // FF_TILE)]


def _gather_scratch(names):
    if not names:
        return []
    pad = [pltpu.VMEM((FF_TILE - FF_SHARD, D_MODEL), BF16), pltpu.SemaphoreType.DMA((FF_PAD // FF_TILE,))]
    return _comm_sems(len(names)) + (pad if "w_down" in names else [])


def _gather_ops(names, ins, outs, scratch):
    copies = _gather_copies(names, ins, outs, *scratch[:3])
    pads = _down_pads(outs[names.index("w_down")], scratch[3], scratch[4]) if "w_down" in names else []

    def start():
        if pads:
            scratch[3][...] = jnp.zeros_like(scratch[3])
        _start(copies)
        for cp in pads:
            cp.start()

    def finish():
        _finish(copies)
        for cp in pads:
            cp.wait()

    return start, finish


def _start(copies):
    local, sent, _ = copies
    for cp in local + sent:
        cp.start()


def _finish(copies):
    local, sent, recv = copies
    for cp in recv:
        cp.wait_recv()
    for cp in sent:
        cp.wait_send()
    for cp in local:
        cp.wait()


def _gather_weights(shards):
    names = tuple(shards)
    nw = len(names)

    def body(*refs):
        copies = _gather_copies(names, refs[:nw], refs[nw:2 * nw], *refs[2 * nw:])
        _start(copies)
        _finish(copies)

    outs = _pcall(
        body, name="gather_weights",
        in_specs=[HBM_SPEC] * nw, out_specs=[HBM_SPEC] * nw,
        out_shape=[SDS(FULL[n][0], BF16) for n in names],
        scratch_shapes=_comm_sems(nw))(*[shards[n] for n in names])
    return dict(zip(names, outs))


def _exchange_small(blocks):
    nb = len(blocks)

    def body(*refs):
        ins, outs = refs[:nb], refs[nb:2 * nb]
        send_sems, recv_sems, local_sems = refs[2 * nb:]
        x, y, c, me = _mesh_place()
        local, sent, recv = [], [], []
        for bi in range(nb):
            local.append(pltpu.make_async_copy(ins[bi], outs[bi].at[me], local_sems.at[bi]))
            for k in range(1, N_DEV):
                peer, pidx = _peer(x, y, c, k)
                sems = dict(send_sem=send_sems.at[bi, k - 1], recv_sem=recv_sems.at[bi, k - 1], device_id=peer,
                            device_id_type=MESH)
                sent.append(pltpu.make_async_remote_copy(src_ref=ins[bi], dst_ref=outs[bi].at[me], **sems))
                recv.append(pltpu.make_async_remote_copy(src_ref=ins[bi], dst_ref=outs[bi].at[pidx], **sems))
        _start((local, sent, recv))
        _finish((local, sent, recv))

    return _pcall(
        body, name="exchange_small", in_specs=[HBM_SPEC] * nb, out_specs=[HBM_SPEC] * nb,
        out_shape=[SDS((N_DEV,) + a.shape, F32) for a in blocks], scratch_shapes=_comm_sems(nb))(*blocks)


def _adam_math(g, w, m, v):
    m_new = ADAM_B1 * m + (1.0 - ADAM_B1) * g
    v_new = ADAM_B2 * v + (1.0 - ADAM_B2) * (g * g)
    m_hat = m_new / (1.0 - ADAM_B1 ** ADAM_STEP)
    v_hat = v_new / (1.0 - ADAM_B2 ** ADAM_STEP)
    return -ADAM_LR * (m_hat / (jnp.sqrt(v_hat) + ADAM_EPS) + ADAM_WD * w), m_new, v_new


def _adamw_small(parts, w, m, v, sq_parts, *, name):
    n = len(parts)

    def body(*refs):
        p_refs, w_refs, m_refs, v_refs = refs[:n], refs[n:2 * n], refs[2 * n:3 * n], refs[3 * n:4 * n]
        sq_ref, outs = refs[4 * n], refs[4 * n + 1:]
        for i in range(n):
            g = p_refs[i][0]
            for k in range(1, N_DEV):
                g = g + p_refs[i][k]
            delta, m_new, v_new = _adam_math(g, w_refs[i][...], m_refs[i][...], v_refs[i][...])
            outs[4 * i][...] = g
            outs[4 * i + 1][...] = delta
            outs[4 * i + 2][...] = m_new
            outs[4 * i + 3][...] = v_new
        tot = sq_ref[0]
        for k in range(1, N_DEV):
            tot = tot + sq_ref[k]
        outs[4 * n][...] = (0.5 / D_MODEL) * jnp.sum(tot, axis=1, keepdims=True)

    out_shape = [SDS(a.shape, F32) for a in w for _ in range(4)] + [SDS((1, 1), F32)]
    outs = _pcall(body, name=name, out_shape=out_shape, compiler_params=_cp())(*parts, *w, *m, *v, sq_parts)
    return [outs[4 * i:4 * i + 4] for i in range(n)], outs[4 * n][0, 0]


def _adamw(parts, w, m, v, *, name):
    r, c = w.shape
    tr = r
    for cand in (256, 128, 88):
        if r % cand == 0:
            tr = cand
            break

    def body(p_ref, w_ref, m_ref, v_ref, g_ref, d_ref, mo_ref, vo_ref):
        g = p_ref[0].astype(F32)
        for k in range(1, N_DEV):
            g = g + p_ref[k].astype(F32)
        g_ref[...] = g
        d_ref[...], mo_ref[...], vo_ref[...] = _adam_math(g, w_ref[...], m_ref[...], v_ref[...])

    row = pl.BlockSpec((tr, c), lambda i: (i, 0))
    return _pcall(
        body, name=name, grid=(r // tr,),
        in_specs=[pl.BlockSpec((N_DEV, tr, c), lambda i: (0, i, 0)), row, row, row],
        out_specs=[row] * 4, out_shape=[SDS((r, c), F32)] * 4, compiler_params=_cp())(parts, w, m, v)


def _pad_cols(a, width):
    return jnp.pad(a, ((0, 0), (0, width - a.shape[1])))


def kernel(x, mem, positions, mix_norm_g, w_in, q_norm_g, k_norm_g, pool_w, pool_scale, w_out, cross_norm_g, mem_norm_g, w_cq, w_ckv, cq_norm_g, ck_norm_g, w_co, ffn_norm_g, w_gate_up, w_down, loss_target, m_mix_norm_g, m_w_in, m_q_norm_g, m_k_norm_g, m_pool_w, m_pool_scale, m_w_out, m_cross_norm_g, m_mem_norm_g, m_w_cq, m_w_ckv, m_cq_norm_g, m_ck_norm_g, m_w_co, m_ffn_norm_g, m_w_gate_up, m_w_down, v_mix_norm_g, v_w_in, v_q_norm_g, v_k_norm_g, v_pool_w, v_pool_scale, v_w_out, v_cross_norm_g, v_mem_norm_g, v_w_cq, v_w_ckv, v_cq_norm_g, v_ck_norm_g, v_w_co, v_ffn_norm_g, v_w_gate_up, v_w_down):
    given = dict(locals())
    w_f32 = {n: given[n][0] for n in SHARDED + SMALL}
    m_f32 = {n: given["m_" + n][0] for n in SHARDED + SMALL}
    v_f32 = {n: given["v_" + n][0] for n in SHARDED + SMALL}
    for d in (w_f32, m_f32, v_f32):
        d["w_gate_up"] = _pad_cols(d["w_gate_up"], FF_TILE)

    shards = {n: w_f32[n].astype(BF16) for n in SHARDED}
    w_in_full = _gather_weights({"w_in": shards.pop("w_in")})["w_in"]
    sm_rows = {n: (w_f32[n] if w_f32[n].ndim == 3 else w_f32[n].reshape(1, -1)) for n in SMALL}
    sq, dx, parts, gs = _local_step(x[0], mem[0], positions[0], loss_target[0], w_in_full, shards, sm_rows)

    flat = lambda a: a.reshape(-1, a.shape[-1])
    got = _exchange_small([flat(gs[n]) for n in SMALL] + [sq])
    small_res, loss = _adamw_small(got[:-1], [flat(sm_rows[n]) for n in SMALL],
                                   [flat(m_f32[n].reshape(sm_rows[n].shape)) for n in SMALL],
                                   [flat(v_f32[n].reshape(sm_rows[n].shape)) for n in SMALL], got[-1],
                                   name="adamw_small")
    res = {n: [a.reshape(w_f32[n].shape) for a in small_res[i]] for i, n in enumerate(SMALL)}
    for n in SHARDED:
        res[n] = _adamw(parts[n], w_f32[n], m_f32[n], v_f32[n], name="adamw_" + n)
    res["w_gate_up"] = [a[:, :FF_SHARD] for a in res["w_gate_up"]]
    order = ("mix_norm_g", "w_in", "q_norm_g", "k_norm_g", "pool_w", "pool_scale", "w_out", "cross_norm_g",
             "mem_norm_g", "w_cq", "w_ckv", "cq_norm_g", "ck_norm_g", "w_co", "ffn_norm_g", "w_gate_up", "w_down")
    outs = [loss, dx[None]]
    for which in range(4):
        outs += [res[n][which][None] for n in order]
    return tuple(outs)
```

```python
import functools

import jax
import jax.numpy as jnp
from jax import lax
from jax.experimental import pallas as pl
from jax.experimental.pallas import tpu as pltpu

F32 = jnp.float32
BF16 = jnp.bfloat16
SDS = jax.ShapeDtypeStruct

D_MODEL = 1024
HEAD_DIM = 128
N_GROUPS = 3
DILATIONS = (1, 4, 16)
ATT_HEADS = 4
Q_W = 1536
KV_W = 512
POOL_W = 512
POOL_WINDOWS = (2, 4, 8, 16)
IN_W = 3072
X_W = 512
N_MEM = 256
D_FF = 2816
FF_TILE = 768
FF_SHARD = 704
FF_PAD = 4 * FF_TILE
DOWN_SHARD = 352
ROT_DIM = 32
ROT_HALF = 16
ROPE_THETA = 500000.0
EPS = 1e-6
NEG_INF = -1e30
SCALE = HEAD_DIM ** -0.5
BLOCK = 128
HALO = 16

ADAM_LR = 0.001
ADAM_B1 = 0.9
ADAM_B2 = 0.999
ADAM_EPS = 1e-08
ADAM_WD = 0.01
ADAM_STEP = 10

N_DEV = 8
VMEM_LIMIT_BYTES = 56 * 1024 * 1024
MESH = pl.DeviceIdType.MESH


def _pcall(body, **kw):
    return pl.pallas_call(body, **kw)


def _cp():
    return pltpu.CompilerParams(vmem_limit_bytes=VMEM_LIMIT_BYTES)


def _dot(a, b):
    return lax.dot_general(a, b, (((1,), (0,)), ((), ())), preferred_element_type=F32)


def _dot_nt(a, b):
    return lax.dot_general(a, b, (((1,), (1,)), ((), ())), preferred_element_type=F32)


def _dot_tn(a, b):
    return lax.dot_general(a, b, (((0,), (0,)), ((), ())), preferred_element_type=F32)


def _rows(s):
    return min(512, s)


def _rms_r(x):
    return lax.rsqrt(jnp.mean(x * x, axis=-1, keepdims=True) + EPS)


def _norm_bwd(x, r, gain, dxn):
    z = dxn * gain
    dx = r * z - x * (r * r * r * jnp.mean(z * x, axis=-1, keepdims=True))
    dgain = jnp.sum(dxn * x * r, axis=0, keepdims=True)
    return dx, dgain


def _split_bf16(t):
    hi = t.astype(BF16)
    return hi, (t - hi.astype(F32)).astype(BF16)


def _lane_sums(t, ones):
    hi, lo = _split_bf16(t)
    return _dot(hi, ones) + _dot(lo, ones)


def _head_r(y, ones):
    return lax.rsqrt(_lane_sums(y * y, ones) * (1.0 / HEAD_DIM) + EPS)


def _head_norm_bwd(y, r, gain, dn, ones):
    z = dn * gain
    dy = r * z - y * (r * r * r * (_dot((z * y).astype(BF16), ones) * (1.0 / HEAD_DIM)))
    return dy, jnp.sum(dn * y * r, axis=0, keepdims=True)


def _swap_matrix():
    src = lax.broadcasted_iota(jnp.int32, (HEAD_DIM, HEAD_DIM), 0)
    dst = lax.broadcasted_iota(jnp.int32, (HEAD_DIM, HEAD_DIM), 1)
    hit = ((dst < ROT_HALF) & (src == dst + ROT_HALF)) | ((dst >= ROT_HALF) & (dst < ROT_DIM) & (src == dst - ROT_HALF))
    return jnp.where(hit, 1.0, 0.0).astype(BF16)


def _partner(t, swap):
    hi, lo = _split_bf16(t)
    return _dot(hi, swap) + _dot(lo, swap)


def _rope(n, cos_t, sin_t, swap):
    return n * cos_t + _partner(n, swap) * sin_t


def _rope_bwd(d, cos_t, sin_t, swap):
    return d * cos_t + _dot((d * sin_t).astype(BF16), swap)


def _resident(shape):
    return pl.BlockSpec(shape, lambda i: (0,) * len(shape), pipeline_mode=pl.Buffered(1))


def _chunks(n, cn):
    return [slice(j * cn, (j + 1) * cn) for j in range(n // cn)]


def _norm_linear(x, gain, w, *, cn, out_dtype, name, shards=None):
    s, k = x.shape
    n = w.shape[1]
    tm = _rows(s)
    names = tuple(shards or ())
    nr = len(names)

    def body(*refs):
        x_ref, g_ref, w_ref = refs[:3]
        y_ref, xn_ref = refs[3 + nr:5 + nr]
        if nr:
            start, finish = _gather_ops(names, refs[3:3 + nr], refs[5 + nr:5 + 2 * nr], refs[5 + 2 * nr:])
            pl.when(pl.program_id(0) == 0)(start)
        xv = x_ref[...]
        xn_ref[...] = (xv * _rms_r(xv) * g_ref[...]).astype(BF16)
        for c in _chunks(n, cn):
            y_ref[:, c] = _dot(xn_ref[...], w_ref[:, c]).astype(out_dtype)
        if nr:
            pl.when(pl.program_id(0) == s // tm - 1)(finish)

    row = lambda w_: pl.BlockSpec((tm, w_), lambda i: (i, 0))
    outs = _pcall(
        body, name=name, grid=(s // tm,),
        in_specs=[row(k), _resident((1, k)), _resident((k, n))] + [HBM_SPEC] * nr,
        out_specs=[row(n), row(k)] + [HBM_SPEC] * nr,
        out_shape=[SDS((s, n), out_dtype), SDS((s, k), BF16)] + [SDS(FULL[a][0], BF16) for a in names],
        scratch_shapes=_gather_scratch(names),
        compiler_params=_cp())(x, gain, w, *[shards[a] for a in names])
    return (outs[0], outs[1], dict(zip(names, outs[2:]))) if nr else tuple(outs)


def _norm_linear_swiglu(x, gain, wgu, *, name):
    s, k = x.shape
    tm = _rows(s)

    def body(x_ref, g_ref, w_ref, gu_ref, a_ref, xn_ref):
        xv = x_ref[...]
        xn_ref[...] = (xv * _rms_r(xv) * g_ref[...]).astype(BF16)
        for c in _chunks(FF_PAD, FF_TILE):
            g = _dot(xn_ref[...], w_ref[:, c])
            u = _dot(xn_ref[...], w_ref[:, slice(FF_PAD + c.start, FF_PAD + c.stop)])
            a_ref[:, c] = (g * jax.nn.sigmoid(g) * u).astype(BF16)
            gu_ref[0, :, c] = g.astype(BF16)
            gu_ref[1, :, c] = u.astype(BF16)

    row = lambda w_: pl.BlockSpec((tm, w_), lambda i: (i, 0))
    return _pcall(
        body, name=name, grid=(s // tm,),
        in_specs=[row(k), _resident((1, k)), _resident((k, 2 * FF_PAD))],
        out_specs=[pl.BlockSpec((2, tm, FF_PAD), lambda i: (0, i, 0)), row(FF_PAD), row(k)],
        out_shape=[SDS((2, s, FF_PAD), BF16), SDS((s, FF_PAD), BF16), SDS((s, k), BF16)],
        compiler_params=_cp())(x, gain, wgu)


def _linear_res(a, w, res, *, cn, name):
    s, k = a.shape
    n = w.shape[1]
    tm = _rows(s)

    def body(a_ref, w_ref, r_ref, y_ref):
        for c in _chunks(n, cn):
            y_ref[:, c] = r_ref[:, c] + _dot(a_ref[...], w_ref[:, c])

    row = lambda w_: pl.BlockSpec((tm, w_), lambda i: (i, 0))
    return _pcall(
        body, name=name, grid=(s // tm,),
        in_specs=[row(k), _resident((k, n)), row(n)], out_specs=row(n),
        out_shape=SDS((s, n), F32), compiler_params=_cp())(a, w, res)


def _linear_res_loss(a, w, res, tgt, *, name):
    s, k = a.shape
    n = w.shape[1]
    tm = _rows(s)

    def body(a_ref, w_ref, r_ref, t_ref, dy_ref, dyb_ref, sq_ref):
        e = r_ref[...] + _dot(a_ref[...], w_ref[...]) - t_ref[...]
        dy = e * (1.0 / n)
        dy_ref[...] = dy
        dyb_ref[...] = dy.astype(BF16)

        @pl.when(pl.program_id(0) == 0)
        def _():
            sq_ref[...] = jnp.zeros_like(sq_ref)
        sq_ref[...] += jnp.sum(e * e, axis=0, keepdims=True)

    row = lambda w_: pl.BlockSpec((tm, w_), lambda i: (i, 0))
    return _pcall(
        body, name=name, grid=(s // tm,),
        in_specs=[row(k), _resident((k, n)), row(n), row(n)],
        out_specs=[row(n), row(n), pl.BlockSpec((1, n), lambda i: (0, 0))],
        out_shape=[SDS((s, n), F32), SDS((s, n), BF16), SDS((1, n), F32)],
        compiler_params=_cp())(a, w, res, tgt)


def _linear_nt(g, w, *, cn, name):
    s, k = g.shape
    n = w.shape[0]
    tm = _rows(s)

    def body(g_ref, w_ref, y_ref):
        for c in _chunks(n, cn):
            y_ref[:, c] = _dot_nt(g_ref[...], w_ref[c, :]).astype(BF16)

    row = lambda w_: pl.BlockSpec((tm, w_), lambda i: (i, 0))
    return _pcall(
        body, name=name, grid=(s // tm,),
        in_specs=[row(k), _resident((n, k))], out_specs=row(n),
        out_shape=SDS((s, n), BF16), compiler_params=_cp())(g, w)


def _swiglu_bwd(dyb, wd, gu, *, name):
    s, n = dyb.shape
    tm = _rows(s)

    def body(dy_ref, wd_ref, gu_ref, dgu_ref):
        for c in _chunks(FF_PAD, FF_TILE):
            da = _dot_nt(dy_ref[...], wd_ref[c, :])
            g = gu_ref[0, :, c].astype(F32)
            u = gu_ref[1, :, c].astype(F32)
            sg = jax.nn.sigmoid(g)
            dgu_ref[0, :, c] = (da * u * (sg * (1.0 + g * (1.0 - sg)))).astype(BF16)
            dgu_ref[1, :, c] = (da * (g * sg)).astype(BF16)

    half = pl.BlockSpec((2, tm, FF_PAD), lambda i: (0, i, 0))
    return _pcall(
        body, name=name, grid=(s // tm,),
        in_specs=[pl.BlockSpec((tm, n), lambda i: (i, 0)), _resident((FF_PAD, n)), half],
        out_specs=half, out_shape=SDS((2, s, FF_PAD), BF16), compiler_params=_cp())(dyb, wd, gu)


def _linear_nt_normbwd(g, w, x, dres, gain, *, name, grads=None):
    d, k = w.shape
    s = x.shape[0]
    tm = _rows(s)
    names = tuple(grads or ())
    nr = len(names)

    def body(*refs):
        g_ref, w_ref, x_ref, dr_ref, gn_ref = refs[:5]
        dx_ref, dxb_ref, dg_ref = refs[5 + nr:8 + nr]
        if nr:
            copies = _exchange_copies(names, refs[5:5 + nr], refs[8 + nr:8 + 2 * nr], *refs[8 + 2 * nr:])

        @pl.when(pl.program_id(0) == 0)
        def _():
            dg_ref[...] = jnp.zeros_like(dg_ref)
            if nr:
                _start(copies)

        if g.ndim == 3:
            dxn = _dot_nt(g_ref[0], w_ref[:, :k // 2]) + _dot_nt(g_ref[1], w_ref[:, k // 2:])
        else:
            dxn = _dot_nt(g_ref[...], w_ref[...])
        xv = x_ref[...]
        dx, dgain = _norm_bwd(xv, _rms_r(xv), gn_ref[...], dxn)
        out = dr_ref[...] + dx
        dx_ref[...] = out
        dxb_ref[...] = out.astype(BF16)
        dg_ref[...] += dgain

        if nr:
            @pl.when(pl.program_id(0) == s // tm - 1)
            def _():
                _finish(copies)

    row = pl.BlockSpec((tm, d), lambda i: (i, 0))
    g_spec = (pl.BlockSpec((2, tm, k // 2), lambda i: (0, i, 0)) if g.ndim == 3
              else pl.BlockSpec((tm, k), lambda i: (i, 0)))
    outs = _pcall(
        body, name=name, grid=(s // tm,),
        in_specs=[g_spec, _resident((d, k)), row, row, _resident((1, d))] + [HBM_SPEC] * nr,
        out_specs=[row, row, pl.BlockSpec((1, d), lambda i: (0, 0))] + [HBM_SPEC] * nr,
        out_shape=[SDS((s, d), F32), SDS((s, d), BF16), SDS((1, d), F32)]
        + [SDS((N_DEV,) + _shard_shape(n), BF16) for n in names],
        scratch_shapes=_comm_sems(nr) if nr else [],
        compiler_params=_cp())(g, w, x, dres, gain, *[grads[n] for n in names])
    return (outs[0], outs[1], outs[2], dict(zip(names, outs[3:]))) if nr else tuple(outs)


def _dw_tn(x, g, *, tkw, tn, name):
    s, kw = x.shape
    halves = g.ndim == 3
    n = 2 * g.shape[2] if halves else g.shape[1]
    ts = min(1024, s)
    ns = s // ts
    per_half = n // 2 // tn

    def body(x_ref, g_ref, o_ref, acc_ref):
        ss = pl.program_id(2)

        @pl.when(ss == 0)
        def _():
            acc_ref[...] = jnp.zeros_like(acc_ref)

        acc_ref[...] += _dot_tn(x_ref[...], g_ref[...])

        @pl.when(ss == ns - 1)
        def _():
            o_ref[...] = acc_ref[...].astype(BF16)

    g_spec = (pl.BlockSpec((None, ts, tn), lambda a, b, ss: (b // per_half, ss, b % per_half)) if halves
              else pl.BlockSpec((ts, tn), lambda a, b, ss: (ss, b)))
    return _pcall(
        body, name=name, grid=(kw // tkw, n // tn, ns),
        in_specs=[pl.BlockSpec((ts, tkw), lambda a, b, ss: (ss, a)), g_spec],
        out_specs=pl.BlockSpec((tkw, tn), lambda a, b, ss: (a, b)),
        out_shape=SDS((kw, n), BF16),
        scratch_shapes=[pltpu.VMEM((tkw, tn), F32)], compiler_params=_cp())(x, g)


def _qk_prep(proj, cos_t, sin_t, qg, kg, shards, *, name):
    s = proj.shape[0]
    tm = _rows(s)
    nqh = Q_W // HEAD_DIM
    names = tuple(shards)
    nr = len(names)

    def body(*refs):
        q_ref, k_ref, v_ref, c_ref, s_ref, qg_ref, kg_ref = refs[:7]
        qo_ref, ko_ref, vo_ref = refs[7 + nr:10 + nr]
        start, finish = _gather_ops(names, refs[7:7 + nr], refs[10 + nr:10 + 2 * nr], refs[10 + 2 * nr:])
        pl.when(pl.program_id(0) == 0)(start)
        c, sn = c_ref[...], s_ref[...]
        ones, swap = jnp.ones((HEAD_DIM, HEAD_DIM), BF16), _swap_matrix()
        for h in range(nqh):
            sl = slice(h * HEAD_DIM, (h + 1) * HEAD_DIM)
            y = q_ref[:, sl]
            qo_ref[:, sl] = (_rope(y * _head_r(y, ones) * qg_ref[...], c, sn, swap) * SCALE).astype(BF16)
        for h in range(ATT_HEADS):
            sl = slice(h * HEAD_DIM, (h + 1) * HEAD_DIM)
            y = k_ref[:, sl]
            ko_ref[:, sl] = _rope(y * _head_r(y, ones) * kg_ref[...], c, sn, swap).astype(BF16)
        vo_ref[...] = v_ref[...].astype(BF16)
        pl.when(pl.program_id(0) == s // tm - 1)(finish)

    row = lambda w, j: pl.BlockSpec((tm, w), lambda i: (i, j))
    one = pl.BlockSpec((1, HEAD_DIM), lambda i: (0, 0))
    outs = _pcall(
        body, name=name, grid=(s // tm,),
        in_specs=[row(Q_W, 0), row(KV_W, 3), row(KV_W, 4), row(HEAD_DIM, 0), row(HEAD_DIM, 0), one, one]
        + [HBM_SPEC] * nr,
        out_specs=[row(Q_W, 0), row(KV_W, 0), row(KV_W, 0)] + [HBM_SPEC] * nr,
        out_shape=[SDS((s, Q_W), BF16), SDS((s, KV_W), BF16), SDS((s, KV_W), BF16)]
        + [SDS(FULL[a][0], BF16) for a in names],
        scratch_shapes=_gather_scratch(names),
        compiler_params=_cp())(proj, proj, proj, cos_t, sin_t, qg, kg, *[shards[a] for a in names])
    return outs[0], outs[1], outs[2], dict(zip(names, outs[3:]))


def _qk_prep_bwd(proj, dq, dk, dv, du, cos_t, sin_t, qg, kg, *, name):
    s = proj.shape[0]
    tm = _rows(s)

    def body(q_ref, k_ref, dq_ref, dk_ref, dv_ref, du_ref, c_ref, s_ref, qg_ref, kg_ref, dp_ref, dqg_ref, dkg_ref):
        c, sn = c_ref[...], s_ref[...]

        @pl.when(pl.program_id(0) == 0)
        def _():
            dqg_ref[...] = jnp.zeros_like(dqg_ref)
            dkg_ref[...] = jnp.zeros_like(dkg_ref)

        ones, swap = jnp.ones((HEAD_DIM, HEAD_DIM), BF16), _swap_matrix()
        dqg = jnp.zeros((1, HEAD_DIM), F32)
        for g in range(N_GROUPS):
            for h in range(ATT_HEADS):
                sl = slice(h * HEAD_DIM, (h + 1) * HEAD_DIM)
                col = slice(g * KV_W + h * HEAD_DIM, g * KV_W + (h + 1) * HEAD_DIM)
                y = q_ref[:, col]
                dn = _rope_bwd(dq_ref[g, :, sl].astype(F32) * SCALE, c, sn, swap)
                dy, dgain = _head_norm_bwd(y, _head_r(y, ones), qg_ref[...], dn, ones)
                dp_ref[:, col] = dy.astype(BF16)
                dqg = dqg + dgain
        dqg_ref[...] += dqg

        dkg = jnp.zeros((1, HEAD_DIM), F32)
        for h in range(ATT_HEADS):
            sl = slice(h * HEAD_DIM, (h + 1) * HEAD_DIM)
            y = k_ref[:, sl]
            dn = _rope_bwd(dk_ref[:, sl], c, sn, swap)
            dy, dgain = _head_norm_bwd(y, _head_r(y, ones), kg_ref[...], dn, ones)
            dp_ref[:, Q_W + h * HEAD_DIM:Q_W + (h + 1) * HEAD_DIM] = dy.astype(BF16)
            dkg = dkg + dgain
        dkg_ref[...] += dkg

        dp_ref[:, Q_W + KV_W:Q_W + 2 * KV_W] = dv_ref[...].astype(BF16)
        dp_ref[:, Q_W + 2 * KV_W:] = du_ref[...].astype(BF16)

    row = lambda w, j: pl.BlockSpec((tm, w), lambda i: (i, j))
    one = pl.BlockSpec((1, HEAD_DIM), lambda i: (0, 0))
    return _pcall(
        body, name=name, grid=(s // tm,),
        in_specs=[row(Q_W, 0), row(KV_W, 3), pl.BlockSpec((N_GROUPS, tm, KV_W), lambda i: (0, i, 0))]
        + [row(KV_W, 0)] * 3 + [row(HEAD_DIM, 0), row(HEAD_DIM, 0), one, one],
        out_specs=[row(IN_W, 0), one, one],
        out_shape=[SDS((s, IN_W), BF16), SDS((1, HEAD_DIM), F32), SDS((1, HEAD_DIM), F32)],
        compiler_params=_cp())(proj, proj, dq, dk, dv, du, cos_t, sin_t, qg, kg)


ATT_ROWS = 16 * BLOCK


def _sub(ref, start, d, size=BLOCK):
    return ref[pl.ds(start, size, stride=d), :] if d > 1 else ref[pl.ds(start, size), :]


def _sub_set(ref, start, d, val):
    if d > 1:
        ref[pl.ds(start, BLOCK, stride=d), :] = val
    else:
        ref[pl.ds(start, BLOCK), :] = val


def _band_masks():
    row = lax.broadcasted_iota(jnp.int32, (BLOCK, BLOCK), 0)
    col = lax.broadcasted_iota(jnp.int32, (BLOCK, BLOCK), 1)
    return col <= row, col >= row


def _eye(n=BLOCK):
    row = lax.broadcasted_iota(jnp.int32, (n, n), 0)
    col = lax.broadcasted_iota(jnp.int32, (n, n), 1)
    return jnp.where(row == col, 1.0, 0.0).astype(BF16)


def _transpose_mxu(eye, a):
    return _dot_nt(eye, a).astype(BF16)


def _attn_fwd(q_rot, k_rot, v, shards, *, name):
    s = q_rot.shape[0]
    rr = ATT_ROWS
    nblk = s // rr
    names = tuple(shards)
    nr = len(names)

    def body(*refs):
        q0, q1, q2, kp, kc, vp, vc = refs[:7]
        mix_ref, lse_ref = refs[7 + nr:9 + nr]
        qs, ks, vs, os_, ls = refs[9 + 2 * nr:14 + 2 * nr]
        h, n = pl.program_id(0), pl.program_id(1)
        start, finish = _gather_ops(names, refs[7:7 + nr], refs[9 + nr:9 + 2 * nr], refs[14 + 2 * nr:])
        pl.when((h == 0) & (n == 0))(start)
        for g, q_ref in enumerate((q0, q1, q2)):
            qs[g] = q_ref[...].astype(F32)
        ks[:rr] = kp[...].astype(F32)
        ks[rr:] = kc[...].astype(F32)
        vs[:rr] = vp[...].astype(F32)
        vs[rr:] = vc[...].astype(F32)
        m_cur, m_band = _band_masks()
        mask_in = jnp.concatenate([m_band, m_cur], axis=1)
        mask_first = jnp.concatenate([m_band & (n > 0), m_cur], axis=1)
        ones = jnp.ones((2 * BLOCK, HEAD_DIM), BF16)
        pieces = [(g, d, j * BLOCK * d + r, j) for g, d in enumerate(DILATIONS) for r in range(d)
                  for j in range(rr // (BLOCK * d))]

        def scores(piece):
            g, d, base, j = piece
            q = _sub(qs.at[g], base, d).astype(BF16)
            k2 = _sub(ks, rr + base - BLOCK * d, d, 2 * BLOCK).astype(BF16)
            return jnp.where(mask_first if j == 0 else mask_in, _dot_nt(q, k2), NEG_INF)

        sc = scores(pieces[0])
        for i, (g, d, base, j) in enumerate(pieces):
            cur = sc
            if i + 1 < len(pieces):
                sc = scores(pieces[i + 1])
            m = jnp.max(cur, axis=-1, keepdims=True)
            p = jnp.exp(cur - m).astype(BF16)
            v2 = _sub(vs, rr + base - BLOCK * d, d, 2 * BLOCK).astype(BF16)
            acc_l = _dot(p, jnp.concatenate([v2, ones], axis=1))
            l = acc_l[:, HEAD_DIM:]
            _sub_set(os_.at[g], base, d, acc_l[:, :HEAD_DIM] / l)
            _sub_set(ls.at[g], base, d, m + jnp.log(l))
        for c in _chunks(rr, 2 * BLOCK):
            a, b, cc = ls[0, c, :], ls[1, c, :], ls[2, c, :]
            m = jnp.maximum(jnp.maximum(a, b), cc)
            wa, wb, wc = jnp.exp(a - m), jnp.exp(b - m), jnp.exp(cc - m)
            den = wa + wb + wc
            mix_ref[c, :] = ((wa * os_[0, c, :] + wb * os_[1, c, :] + wc * os_[2, c, :]) / den).astype(BF16)
            lse_ref[c, :] = m + jnp.log(den)

        pl.when((h == ATT_HEADS - 1) & (n == nblk - 1))(finish)

    blk = lambda f: pl.BlockSpec((rr, HEAD_DIM), f)
    prv = lambda n: jnp.maximum(n - 1, 0)
    outs = _pcall(
        body, name=name, grid=(ATT_HEADS, nblk),
        in_specs=[blk(lambda h, n, g=g: (n, g * ATT_HEADS + h)) for g in range(N_GROUPS)]
        + [blk(lambda h, n: (prv(n), h)), blk(lambda h, n: (n, h))] * 2 + [HBM_SPEC] * nr,
        out_specs=[blk(lambda h, n: (n, h)), blk(lambda h, n: (n, h))] + [HBM_SPEC] * nr,
        out_shape=[SDS((s, KV_W + POOL_W), BF16), SDS((s, KV_W), F32)] + [SDS(FULL[w][0], BF16) for w in names],
        scratch_shapes=[pltpu.VMEM((N_GROUPS, rr, HEAD_DIM), F32), pltpu.VMEM((2 * rr, HEAD_DIM), F32),
                        pltpu.VMEM((2 * rr, HEAD_DIM), F32), pltpu.VMEM((N_GROUPS, rr, HEAD_DIM), F32),
                        pltpu.VMEM((N_GROUPS, rr, HEAD_DIM), F32)] + _gather_scratch(names),
        compiler_params=_cp())(q_rot, q_rot, q_rot, k_rot, k_rot, v, v, *[shards[w] for w in names])
    return outs[0], outs[1], dict(zip(names, outs[2:]))


def _attn_bwd(q_rot, k_rot, v, mix, dmix, lse, grads, *, name):
    s = q_rot.shape[0]
    rr = ATT_ROWS
    nblk = s // rr
    names = tuple(grads)
    nr = len(names)

    def body(*refs):
        q0, q1, q2, qx0, qx1, qx2, kp, kc, vp, vc, do_c, do_x, o_c, o_x, l_c, l_x = refs[:16]
        dq_ref, dk_ref, dv_ref = refs[16 + nr:19 + nr]
        qs, ks, vs, dos, lss, dls, dqs, dks, dvs, send_sems, recv_sems, local_sems = refs[19 + 2 * nr:]
        h, n = pl.program_id(0), pl.program_id(1)
        copies = _exchange_copies(names, refs[16:16 + nr], refs[19 + nr:19 + 2 * nr], send_sems, recv_sems, local_sems)

        @pl.when((h == 0) & (n == 0))
        def _():
            _start(copies)

        for g, (qc_ref, qx_ref) in enumerate(((q0, qx0), (q1, qx1), (q2, qx2))):
            qs[g, :rr] = qc_ref[...].astype(F32)
            qs[g, rr:] = qx_ref[...].astype(F32)
        ks[:rr] = kp[...].astype(F32)
        ks[rr:] = kc[...].astype(F32)
        vs[:rr] = vp[...].astype(F32)
        vs[rr:] = vc[...].astype(F32)
        lss[:rr] = l_c[...]
        lss[rr:] = l_x[...]
        for half, (d_ref, o_ref) in enumerate(((do_c, o_c), (do_x, o_x))):
            for c in _chunks(rr, 2 * BLOCK):
                cs = slice(half * rr + c.start, half * rr + c.stop)
                dof = d_ref[c, :].astype(F32)
                dos[cs, :] = dof
                dls[cs, :] = jnp.broadcast_to(jnp.sum(dof * o_ref[c, :].astype(F32), axis=-1, keepdims=True),
                                              (2 * BLOCK, HEAD_DIM))
        m_cur, m_band = _band_masks()
        mask_in = jnp.concatenate([m_cur, m_band], axis=0)
        mask_last = jnp.concatenate([m_cur, m_band & (n + 1 < nblk)], axis=0)
        m_first = m_band & (n > 0)
        eye = _eye()
        pieces = [(g, d, j * BLOCK * d + r, j, rr // (BLOCK * d)) for g, d in enumerate(DILATIONS) for r in range(d)
                  for j in range(rr // (BLOCK * d))]

        def front(piece):
            g, d, base, _, _ = piece
            q2 = _sub(qs.at[g], base, d, 2 * BLOCK).astype(BF16)
            do2 = _sub(dos, base, d, 2 * BLOCK).astype(BF16)
            k = _sub(ks, rr + base, d).astype(BF16)
            vv = _sub(vs, rr + base, d).astype(BF16)
            return q2, do2, k, _dot_nt(q2, k), _dot_nt(do2, vv)

        nxt_front = front(pieces[0])
        dq_acc = None
        for i, (g, d, base, j, nsub) in enumerate(pieces):
            q2, do2, k, s2, dp2 = nxt_front
            if j == 0:
                kp_ = _sub(ks, rr + base - BLOCK * d, d).astype(BF16)
                p0 = jnp.where(m_first, jnp.exp(_dot_nt(q2[:BLOCK], kp_) - _sub(lss, base, d)), 0.0)
                ds0 = p0 * (_dot_nt(do2[:BLOCK], _sub(vs, rr + base - BLOCK * d, d).astype(BF16)) - _sub(dls, base, d))
                dq_acc = _dot(ds0.astype(BF16), kp_)
            if i + 1 < len(pieces):
                nxt_front = front(pieces[i + 1])
            p2 = jnp.where(mask_last if j + 1 == nsub else mask_in,
                           jnp.exp(s2 - _sub(lss, base, d, 2 * BLOCK)), 0.0)
            ds2 = (p2 * (dp2 - _sub(dls, base, d, 2 * BLOCK))).astype(BF16)
            dq2 = _dot(ds2, k)
            _sub_set(dqs.at[g], base, d, dq_acc + dq2[:BLOCK])
            dq_acc = dq2[BLOCK:]
            dk = _dot(_transpose_mxu(eye, ds2), q2)
            dv = _dot(_transpose_mxu(eye, p2.astype(BF16)), do2)
            if g == 0:
                _sub_set(dks, base, d, dk)
                _sub_set(dvs, base, d, dv)
            else:
                _sub_set(dks, base, d, _sub(dks, base, d) + dk)
                _sub_set(dvs, base, d, _sub(dvs, base, d) + dv)
        for g in range(N_GROUPS):
            dq_ref[g] = dqs[g].astype(BF16)
        dk_ref[...] = dks[...]
        dv_ref[...] = dvs[...]

        @pl.when((h == ATT_HEADS - 1) & (n == nblk - 1))
        def _():
            _finish(copies)

    blk = lambda f: pl.BlockSpec((rr, HEAD_DIM), f)
    prv = lambda n: jnp.maximum(n - 1, 0)
    nxt = lambda n: jnp.minimum(n + 1, nblk - 1)
    cur_kv = blk(lambda h, n: (n, h))
    outs = _pcall(
        body, name=name, grid=(ATT_HEADS, nblk),
        in_specs=[blk(lambda h, n, g=g: (n, g * ATT_HEADS + h)) for g in range(N_GROUPS)]
        + [blk(lambda h, n, g=g: (nxt(n), g * ATT_HEADS + h)) for g in range(N_GROUPS)]
        + [blk(lambda h, n: (prv(n), h)), cur_kv] * 2
        + [cur_kv, blk(lambda h, n: (nxt(n), h))] * 3 + [HBM_SPEC] * nr,
        out_specs=[pl.BlockSpec((N_GROUPS, rr, HEAD_DIM), lambda h, n: (0, n, h)), cur_kv, cur_kv] + [HBM_SPEC] * nr,
        out_shape=[SDS((N_GROUPS, s, KV_W), BF16), SDS((s, KV_W), F32), SDS((s, KV_W), F32)]
        + [SDS((N_DEV,) + _shard_shape(w), BF16) for w in names],
        scratch_shapes=[pltpu.VMEM((N_GROUPS, 2 * rr, HEAD_DIM), F32)] + [pltpu.VMEM((2 * rr, HEAD_DIM), F32)] * 5
        + [pltpu.VMEM((N_GROUPS, rr, HEAD_DIM), F32), pltpu.VMEM((rr, HEAD_DIM), F32), pltpu.VMEM((rr, HEAD_DIM), F32)]
        + _comm_sems(nr),
        compiler_params=_cp())(q_rot, q_rot, q_rot, q_rot, q_rot, q_rot, k_rot, k_rot, v, v, dmix, dmix, mix, mix,
                               lse, lse, *[grads[w] for w in names])
    return outs[0], outs[1], outs[2], dict(zip(names, outs[3:]))


def _pool_d(u_ref, halo_ref, i, tm):
    halo = jnp.where(i > 0, halo_ref[...], 0.0)
    t = i * tm + lax.broadcasted_iota(jnp.int32, (tm, 1), 0)
    out = []
    for g, w in enumerate(POOL_WINDOWS):
        sl = slice(g * HEAD_DIM, (g + 1) * HEAD_DIM)
        u = u_ref[:, sl]
        acc = jnp.concatenate([halo[:, sl], u], axis=0)
        sh = 1
        while sh < w:
            acc = acc + pltpu.roll(acc, sh, 0)
            sh *= 2
        cnt = jnp.minimum(t + 1, w).astype(F32)
        out.append(acc[HALO:, :] / cnt - u)
    return out


def _pool_fwd(proj, mix, pool_w, pool_scale, *, name):
    s = proj.shape[0]
    tm = _rows(s)
    ucol = (IN_W - POOL_W) // POOL_W

    def body(u_ref, halo_ref, mix_in, w_ref, sc_ref, o_ref):
        del mix_in
        dd = _pool_d(u_ref, halo_ref, pl.program_id(0), tm)
        for g in range(len(POOL_WINDOWS)):
            sl = slice(g * HEAD_DIM, (g + 1) * HEAD_DIM)
            y = _dot(dd[g].astype(BF16), w_ref[g].astype(BF16))
            o_ref[:, sl] = (y * sc_ref[:, sl]).astype(BF16)

    return _pcall(
        body, name=name, grid=(s // tm,),
        in_specs=[pl.BlockSpec((tm, POOL_W), lambda i: (i, ucol)),
                  pl.BlockSpec((HALO, POOL_W), lambda i: (jnp.maximum(i * (tm // HALO) - 1, 0), ucol)),
                  pl.BlockSpec(memory_space=pl.ANY),
                  pl.BlockSpec((len(POOL_WINDOWS), HEAD_DIM, HEAD_DIM), lambda i: (0, 0, 0)),
                  pl.BlockSpec((1, POOL_W), lambda i: (0, 0))],
        out_specs=pl.BlockSpec((tm, POOL_W), lambda i: (i, 1)),
        out_shape=SDS(mix.shape, BF16), input_output_aliases={2: 0},
        compiler_params=_cp())(proj, proj, mix, pool_w, pool_scale)


def _pool_bwd(proj, dmix, pool_w, pool_scale, *, name):
    s = proj.shape[0]
    tm = _rows(s)
    nblk = s // tm
    ucol = (IN_W - POOL_W) // POOL_W
    ng = len(POOL_WINDOWS)

    def body(u_ref, halo_ref, dp_ref, dpn_ref, w_ref, sc_ref, du_ref, dw_ref, dsc_ref):
        i = pl.program_id(0)

        @pl.when(i == 0)
        def _():
            dw_ref[...] = jnp.zeros_like(dw_ref)
            dsc_ref[...] = jnp.zeros_like(dsc_ref)

        dd = _pool_d(u_ref, halo_ref, i, tm)
        t = i * tm + lax.broadcasted_iota(jnp.int32, (tm, 1), 0)
        dpn = jnp.where(i + 1 < nblk, dpn_ref[...].astype(F32), 0.0)
        for g, w in enumerate(POOL_WINDOWS):
            sl = slice(g * HEAD_DIM, (g + 1) * HEAD_DIM)
            wg = w_ref[g].astype(BF16)
            db = dd[g].astype(BF16)
            dp = dp_ref[:, sl].astype(F32)
            dsc_ref[:, sl] += jnp.sum(dp * _dot(db, wg), axis=0, keepdims=True)
            dy = (dp * sc_ref[:, sl]).astype(BF16)
            dw_ref[g] += _dot_tn(db, dy)
            g_d = _dot_nt(dy, wg)
            g_dn = _dot_nt((dpn[:, sl] * sc_ref[:, sl]).astype(BF16), wg)
            cnt = jnp.minimum(t + 1, w).astype(F32)
            acc = jnp.concatenate([g_d / cnt, g_dn * (1.0 / w)], axis=0)
            sh = 1
            while sh < w:
                acc = acc + pltpu.roll(acc, tm + HALO - sh, 0)
                sh *= 2
            du_ref[:, sl] = acc[:tm, :] - g_d

    nh = s // HALO
    return _pcall(
        body, name=name, grid=(nblk,),
        in_specs=[pl.BlockSpec((tm, POOL_W), lambda i: (i, ucol)),
                  pl.BlockSpec((HALO, POOL_W), lambda i: (jnp.maximum(i * (tm // HALO) - 1, 0), ucol)),
                  pl.BlockSpec((tm, POOL_W), lambda i: (i, 1)),
                  pl.BlockSpec((HALO, POOL_W), lambda i: (jnp.minimum((i + 1) * (tm // HALO), nh - 1), 1)),
                  pl.BlockSpec((ng, HEAD_DIM, HEAD_DIM), lambda i: (0, 0, 0)),
                  pl.BlockSpec((1, POOL_W), lambda i: (0, 0))],
        out_specs=[pl.BlockSpec((tm, POOL_W), lambda i: (i, 0)),
                   pl.BlockSpec((ng, HEAD_DIM, HEAD_DIM), lambda i: (0, 0, 0)),
                   pl.BlockSpec((1, POOL_W), lambda i: (0, 0))],
        out_shape=[SDS((s, POOL_W), F32), SDS((ng, HEAD_DIM, HEAD_DIM), F32), SDS((1, POOL_W), F32)],
        compiler_params=_cp())(proj, proj, dmix, dmix, pool_w, pool_scale)


def _mem_fwd(mem, mem_g, wckv, ck_g, *, name):
    def body(m_ref, g_ref, w_ref, kg_ref, mn_ref, ckr_ref, ckn_ref, cv_ref):
        mv = m_ref[...]
        mn = (mv * _rms_r(mv) * g_ref[...]).astype(BF16)
        mn_ref[...] = mn
        ckv = _dot(mn, w_ref[...])
        ckr_ref[...] = ckv[:, :X_W]
        cv_ref[...] = ckv[:, X_W:].astype(BF16)
        for h in range(ATT_HEADS):
            sl = slice(h * HEAD_DIM, (h + 1) * HEAD_DIM)
            y = ckv[:, sl]
            ckn_ref[:, sl] = (y * _rms_r(y) * kg_ref[...]).astype(BF16)

    return _pcall(
        body, name=name,
        out_shape=[SDS((N_MEM, D_MODEL), BF16), SDS((N_MEM, X_W), F32), SDS((N_MEM, X_W), BF16),
                   SDS((N_MEM, X_W), BF16)],
        compiler_params=_cp())(mem, mem_g, wckv, ck_g)


def _cross_q(cq_ref, g_ref, sl):
    y = cq_ref[:, sl]
    r = _rms_r(y)
    return y, r, y * r * g_ref[...] * SCALE


def _cross_fwd(cq_raw, ck_n, cv, cq_g, *, name):
    s = cq_raw.shape[0]
    tm = _rows(s)

    def body(cq_ref, k_ref, v_ref, g_ref, o_ref):
        for h in range(ATT_HEADS):
            sl = slice(h * HEAD_DIM, (h + 1) * HEAD_DIM)
            _, _, qn = _cross_q(cq_ref, g_ref, sl)
            sc = _dot_nt(qn.astype(BF16), k_ref[:, sl])
            p = jnp.exp(sc - jnp.max(sc, axis=-1, keepdims=True))
            p = p / jnp.sum(p, axis=-1, keepdims=True)
            o_ref[:, sl] = _dot(p.astype(BF16), v_ref[:, sl]).astype(BF16)

    full = lambda a: pl.BlockSpec(a.shape, lambda i: (0, 0))
    return _pcall(
        body, name=name, grid=(s // tm,),
        in_specs=[pl.BlockSpec((tm, X_W), lambda i: (i, 0)), full(ck_n), full(cv), full(cq_g)],
        out_specs=pl.BlockSpec((tm, X_W), lambda i: (i, 0)),
        out_shape=SDS((s, X_W), BF16), compiler_params=_cp())(cq_raw, ck_n, cv, cq_g)


def _cross_bwd(d_o, cq_raw, ck_n, cv, cq_g, *, name):
    s = cq_raw.shape[0]
    tm = _rows(s)

    def body(do_ref, cq_ref, k_ref, v_ref, g_ref, dcq_ref, dk_ref, dv_ref, dg_ref):
        @pl.when(pl.program_id(0) == 0)
        def _():
            dk_ref[...] = jnp.zeros_like(dk_ref)
            dv_ref[...] = jnp.zeros_like(dv_ref)
            dg_ref[...] = jnp.zeros_like(dg_ref)

        dg = jnp.zeros((1, HEAD_DIM), F32)
        for h in range(ATT_HEADS):
            sl = slice(h * HEAD_DIM, (h + 1) * HEAD_DIM)
            y, r, qn = _cross_q(cq_ref, g_ref, sl)
            qb = qn.astype(BF16)
            do = do_ref[:, sl]
            sc = _dot_nt(qb, k_ref[:, sl])
            p = jnp.exp(sc - jnp.max(sc, axis=-1, keepdims=True))
            p = p / jnp.sum(p, axis=-1, keepdims=True)
            dv_ref[:, sl] += _dot_tn(p.astype(BF16), do)
            dp = _dot_nt(do, v_ref[:, sl])
            ds = (p * (dp - jnp.sum(dp * p, axis=-1, keepdims=True))).astype(BF16)
            dk_ref[:, sl] += _dot_tn(ds, qb)
            dn = _dot(ds, k_ref[:, sl]) * SCALE
            dy, dgain = _norm_bwd(y, r, g_ref[...], dn)
            dcq_ref[:, sl] = dy.astype(BF16)
            dg = dg + dgain
        dg_ref[...] += dg

    full = lambda a: pl.BlockSpec(a.shape, lambda i: (0, 0))
    row = pl.BlockSpec((tm, X_W), lambda i: (i, 0))
    return _pcall(
        body, name=name, grid=(s // tm,),
        in_specs=[row, row, full(ck_n), full(cv), full(cq_g)],
        out_specs=[row, pl.BlockSpec((N_MEM, X_W), lambda i: (0, 0)), pl.BlockSpec((N_MEM, X_W), lambda i: (0, 0)),
                   pl.BlockSpec((1, HEAD_DIM), lambda i: (0, 0))],
        out_shape=[SDS((s, X_W), BF16), SDS((N_MEM, X_W), F32), SDS((N_MEM, X_W), F32), SDS((1, HEAD_DIM), F32)],
        compiler_params=_cp())(d_o, cq_raw, ck_n, cv, cq_g)


def _mem_bwd(dck_n, dcv, ck_raw, memn, mem, wckv, mem_g, ck_g, *, name):
    def body(dk_ref, dv_ref, ckr_ref, mn_ref, m_ref, w_ref, mg_ref, kg_ref, dw_ref, dmg_ref, dkg_ref, dckv_s):
        dkg = jnp.zeros((1, HEAD_DIM), F32)
        for h in range(ATT_HEADS):
            sl = slice(h * HEAD_DIM, (h + 1) * HEAD_DIM)
            y = ckr_ref[:, sl]
            dy, dgain = _norm_bwd(y, _rms_r(y), kg_ref[...], dk_ref[:, sl])
            dckv_s[:, sl] = dy.astype(BF16)
            dkg = dkg + dgain
        dkg_ref[...] = dkg
        dckv_s[:, X_W:] = dv_ref[...].astype(BF16)
        dckv = dckv_s[...]
        dw_ref[...] = _dot_tn(mn_ref[...], dckv).astype(BF16)
        dmn = _dot_nt(dckv, w_ref[...])
        mv = m_ref[...]
        dmg_ref[...] = jnp.sum(dmn * mv * _rms_r(mv), axis=0, keepdims=True)

    return _pcall(
        body, name=name,
        out_shape=[SDS((D_MODEL, 2 * X_W), BF16), SDS((1, D_MODEL), F32), SDS((1, HEAD_DIM), F32)],
        scratch_shapes=[pltpu.VMEM((N_MEM, 2 * X_W), BF16)],
        compiler_params=_cp())(dck_n, dcv, ck_raw, memn, mem, wckv, mem_g, ck_g)


def _rope_tables(pos):
    inv_freq = ROPE_THETA ** (-jnp.arange(0, ROT_DIM, 2, dtype=F32) / ROT_DIM)
    ang = pos.astype(F32)[:, None] * inv_freq
    cos, sin = jnp.cos(ang), jnp.sin(ang)
    s = pos.shape[0]
    rest = HEAD_DIM - ROT_DIM
    cos_t = jnp.concatenate([cos, cos, jnp.ones((s, rest), F32)], axis=1)
    sin_t = jnp.concatenate([-sin, sin, jnp.zeros((s, rest), F32)], axis=1)
    return cos_t, sin_t


def _local_step(x, mem, pos, tgt, w_in, shards, sm):
    cos_t, sin_t = _rope_tables(pos)
    wb = {"w_in": w_in}

    pick = lambda *names: {a: shards[a] for a in names}
    proj, xn, got = _norm_linear(x, sm["mix_norm_g"], wb["w_in"], cn=768, out_dtype=F32, name="fwd_in_proj",
                                 shards=pick("w_out", "w_cq", "w_ckv", "w_co"))
    wb.update(got)
    q_rot, k_rot, v, got = _qk_prep(proj, cos_t, sin_t, sm["q_norm_g"], sm["k_norm_g"], pick("w_down"),
                                    name="fwd_qk_prep")
    wb.update(got)
    mix, lse_all, got = _attn_fwd(q_rot, k_rot, v, pick("w_gate_up"), name="fwd_attn")
    wb.update(got)
    mix = _pool_fwd(proj, mix, sm["pool_w"], sm["pool_scale"], name="fwd_pool")
    h1 = _linear_res(mix, wb["w_out"], x, cn=512, name="fwd_out_proj")
    cq_raw, hn = _norm_linear(h1, sm["cross_norm_g"], wb["w_cq"], cn=512, out_dtype=F32, name="fwd_cq_proj")
    memn, ck_raw, ck_n, cv = _mem_fwd(mem, sm["mem_norm_g"], wb["w_ckv"], sm["ck_norm_g"], name="fwd_mem")
    co = _cross_fwd(cq_raw, ck_n, cv, sm["cq_norm_g"], name="fwd_cross")
    h2 = _linear_res(co, wb["w_co"], h1, cn=512, name="fwd_co_proj")
    gu, act, fn = _norm_linear_swiglu(h2, sm["ffn_norm_g"], wb["w_gate_up"], name="fwd_gate_up")
    dy, dyb, sq = _linear_res_loss(act, wb["w_down"], h2, tgt, name="fwd_down_loss")

    gw = {}
    gs = {}
    dgu = _swiglu_bwd(dyb, wb["w_down"], gu, name="bwd_swiglu")
    gw["w_down"] = _dw_tn(act, dyb, tkw=1024, tn=1024, name="bwd_dw_down")
    dh2, dh2b, gs["ffn_norm_g"] = _linear_nt_normbwd(dgu, wb["w_gate_up"], h2, dy, sm["ffn_norm_g"],
                                                     name="bwd_ffn_in")
    gw["w_gate_up"] = _dw_tn(fn, dgu, tkw=1024, tn=1536, name="bwd_dw_gate_up")

    d_co = _linear_nt(dh2b, wb["w_co"], cn=512, name="bwd_co_proj")
    gw["w_co"] = _dw_tn(co, dh2b, tkw=512, tn=1024, name="bwd_dw_co")
    dcq, dck_n, dcv, gs["cq_norm_g"] = _cross_bwd(d_co, cq_raw, ck_n, cv, sm["cq_norm_g"], name="bwd_cross")
    gw["w_ckv"], gs["mem_norm_g"], gs["ck_norm_g"] = _mem_bwd(dck_n, dcv, ck_raw, memn, mem, wb["w_ckv"],
                                                             sm["mem_norm_g"], sm["ck_norm_g"], name="bwd_mem")
    dh1, dh1b, gs["cross_norm_g"] = _linear_nt_normbwd(dcq, wb["w_cq"], h1, dh2, sm["cross_norm_g"],
                                                       name="bwd_cq_in")
    gw["w_cq"] = _dw_tn(hn, dcq, tkw=1024, tn=512, name="bwd_dw_cq")

    dmix = _linear_nt(dh1b, wb["w_out"], cn=512, name="bwd_out_proj")
    gw["w_out"] = _dw_tn(mix, dh1b, tkw=1024, tn=1024, name="bwd_dw_out")
    du, gs["pool_w"], gs["pool_scale"] = _pool_bwd(proj, dmix, sm["pool_w"], sm["pool_scale"], name="bwd_pool")
    dq, dk, dv, parts = _attn_bwd(q_rot, k_rot, v, mix, dmix, lse_all, gw, name="bwd_attn")
    dproj, gs["q_norm_g"], gs["k_norm_g"] = _qk_prep_bwd(proj, dq, dk, dv, du, cos_t, sin_t, sm["q_norm_g"],
                                                         sm["k_norm_g"], name="bwd_qk_prep")
    gw_in = _dw_tn(xn, dproj, tkw=1024, tn=1536, name="bwd_dw_in")
    dx, _, gs["mix_norm_g"], last = _linear_nt_normbwd(dproj, wb["w_in"], x, dh1, sm["mix_norm_g"],
                                                       name="bwd_in_proj", grads={"w_in": gw_in})
    parts.update(last)
    return sq, dx, parts, gs


SHARDED = ("w_in", "w_out", "w_cq", "w_ckv", "w_co", "w_gate_up", "w_down")
SMALL = ("mix_norm_g", "q_norm_g", "k_norm_g", "pool_w", "pool_scale", "cross_norm_g", "mem_norm_g", "cq_norm_g",
         "ck_norm_g", "ffn_norm_g")
FULL = {
    "w_in": ((D_MODEL, IN_W), 1, IN_W // N_DEV),
    "w_out": ((D_MODEL, D_MODEL), 0, D_MODEL // N_DEV),
    "w_cq": ((D_MODEL, X_W), 0, D_MODEL // N_DEV),
    "w_ckv": ((D_MODEL, 2 * X_W), 0, D_MODEL // N_DEV),
    "w_co": ((X_W, D_MODEL), 1, D_MODEL // N_DEV),
    "w_gate_up": ((D_MODEL, 2 * FF_PAD), 1, FF_TILE),
    "w_down": ((FF_PAD, D_MODEL), 0, DOWN_SHARD),
}


def _shard_shape(name):
    shape, axis, width = FULL[name]
    return tuple(width if a == axis else n for a, n in enumerate(shape))


def _window(ref, name, dev):
    _, axis, width = FULL[name]
    if name == "w_down":
        start = pl.multiple_of((dev // 2) * FF_TILE + (dev % 2) * DOWN_SHARD, HALO)
    else:
        start = pl.multiple_of(dev * width, BLOCK)
    return ref.at[pl.ds(start, width), :] if axis == 0 else ref.at[:, pl.ds(start, width)]


def _mesh_place():
    x, y, c = lax.axis_index("x"), lax.axis_index("y"), lax.axis_index("c")
    return x, y, c, 4 * x + 2 * y + c


def _peer(x, y, c, k):
    px = 1 - x if k & 4 else x
    py = 1 - y if k & 2 else y
    pc = 1 - c if k & 1 else c
    return (px, py, pc), 4 * px + 2 * py + pc


HBM_SPEC = pl.BlockSpec(memory_space=pltpu.HBM)


def _comm_sems(n):
    return [pltpu.SemaphoreType.DMA((n, N_DEV - 1)), pltpu.SemaphoreType.DMA((n, N_DEV - 1)),
            pltpu.SemaphoreType.DMA((n,))]


def _gather_copies(names, ins, outs, send_sems, recv_sems, local_sems):
    x, y, c, me = _mesh_place()
    local, sent, recv = [], [], []
    for wi, name in enumerate(names):
        local.append(pltpu.make_async_copy(ins[wi], _window(outs[wi], name, me), local_sems.at[wi]))
        for k in range(1, N_DEV):
            peer, pidx = _peer(x, y, c, k)
            sems = dict(send_sem=send_sems.at[wi, k - 1], recv_sem=recv_sems.at[wi, k - 1], device_id=peer,
                        device_id_type=MESH)
            sent.append(pltpu.make_async_remote_copy(src_ref=ins[wi], dst_ref=_window(outs[wi], name, me), **sems))
            recv.append(pltpu.make_async_remote_copy(src_ref=ins[wi], dst_ref=_window(outs[wi], name, pidx), **sems))
    return local, sent, recv


def _exchange_copies(names, ins, outs, send_sems, recv_sems, local_sems):
    x, y, c, me = _mesh_place()
    local, sent, recv = [], [], []
    for wi, name in enumerate(names):
        local.append(pltpu.make_async_copy(_window(ins[wi], name, me), outs[wi].at[0], local_sems.at[wi]))
        for k in range(1, N_DEV):
            peer, pidx = _peer(x, y, c, k)
            sems = dict(send_sem=send_sems.at[wi, k - 1], recv_sem=recv_sems.at[wi, k - 1], device_id=peer,
                        device_id_type=MESH)
            sent.append(pltpu.make_async_remote_copy(src_ref=_window(ins[wi], name, pidx), dst_ref=outs[wi].at[k], **sems))
            recv.append(pltpu.make_async_remote_copy(src_ref=_window(ins[wi], name, me), dst_ref=outs[wi].at[k], **sems))
    return local, sent, recv


def _down_pads(down_ref, zero_ref, zero_sems):
    return [pltpu.make_async_copy(zero_ref, down_ref.at[pl.ds(t * FF_TILE + FF_SHARD, FF_TILE - FF_SHARD), :],
                                  zero_sems.at[t]) for t in range(FF_PAD // FF_TILE)]


def _gather_scratch(names):
    if not names:
        return []
    pad = [pltpu.VMEM((FF_TILE - FF_SHARD, D_MODEL), BF16), pltpu.SemaphoreType.DMA((FF_PAD // FF_TILE,))]
    return _comm_sems(len(names)) + (pad if "w_down" in names else [])


def _gather_ops(names, ins, outs, scratch):
    copies = _gather_copies(names, ins, outs, *scratch[:3])
    pads = _down_pads(outs[names.index("w_down")], scratch[3], scratch[4]) if "w_down" in names else []

    def start():
        if pads:
            scratch[3][...] = jnp.zeros_like(scratch[3])
        _start(copies)
        for cp in pads:
            cp.start()

    def finish():
        _finish(copies)
        for cp in pads:
            cp.wait()

    return start, finish


def _start(copies):
    local, sent, _ = copies
    for cp in local + sent:
        cp.start()


def _finish(copies):
    local, sent, recv = copies
    for cp in recv:
        cp.wait_recv()
    for cp in sent:
        cp.wait_send()
    for cp in local:
        cp.wait()


def _gather_weights(shards):
    names = tuple(shards)
    nw = len(names)

    def body(*refs):
        copies = _gather_copies(names, refs[:nw], refs[nw:2 * nw], *refs[2 * nw:])
        _start(copies)
        _finish(copies)

    outs = _pcall(
        body, name="gather_weights",
        in_specs=[HBM_SPEC] * nw, out_specs=[HBM_SPEC] * nw,
        out_shape=[SDS(FULL[n][0], BF16) for n in names],
        scratch_shapes=_comm_sems(nw))(*[shards[n] for n in names])
    return dict(zip(names, outs))


def _exchange_small(blocks):
    nb = len(blocks)

    def body(*refs):
        ins, outs = refs[:nb], refs[nb:2 * nb]
        send_sems, recv_sems, local_sems = refs[2 * nb:]
        x, y, c, me = _mesh_place()
        local, sent, recv = [], [], []
        for bi in range(nb):
            local.append(pltpu.make_async_copy(ins[bi], outs[bi].at[me], local_sems.at[bi]))
            for k in range(1, N_DEV):
                peer, pidx = _peer(x, y, c, k)
                sems = dict(send_sem=send_sems.at[bi, k - 1], recv_sem=recv_sems.at[bi, k - 1], device_id=peer,
                            device_id_type=MESH)
                sent.append(pltpu.make_async_remote_copy(src_ref=ins[bi], dst_ref=outs[bi].at[me], **sems))
                recv.append(pltpu.make_async_remote_copy(src_ref=ins[bi], dst_ref=outs[bi].at[pidx], **sems))
        _start((local, sent, recv))
        _finish((local, sent, recv))

    return _pcall(
        body, name="exchange_small", in_specs=[HBM_SPEC] * nb, out_specs=[HBM_SPEC] * nb,
        out_shape=[SDS((N_DEV,) + a.shape, F32) for a in blocks], scratch_shapes=_comm_sems(nb))(*blocks)


def _adam_math(g, w, m, v):
    m_new = ADAM_B1 * m + (1.0 - ADAM_B1) * g
    v_new = ADAM_B2 * v + (1.0 - ADAM_B2) * (g * g)
    m_hat = m_new / (1.0 - ADAM_B1 ** ADAM_STEP)
    v_hat = v_new / (1.0 - ADAM_B2 ** ADAM_STEP)
    return -ADAM_LR * (m_hat / (jnp.sqrt(v_hat) + ADAM_EPS) + ADAM_WD * w), m_new, v_new


def _adamw_small(parts, w, m, v, sq_parts, *, name):
    n = len(parts)

    def body(*refs):
        p_refs, w_refs, m_refs, v_refs = refs[:n], refs[n:2 * n], refs[2 * n:3 * n], refs[3 * n:4 * n]
        sq_ref, outs = refs[4 * n], refs[4 * n + 1:]
        for i in range(n):
            g = p_refs[i][0]
            for k in range(1, N_DEV):
                g = g + p_refs[i][k]
            delta, m_new, v_new = _adam_math(g, w_refs[i][...], m_refs[i][...], v_refs[i][...])
            outs[4 * i][...] = g
            outs[4 * i + 1][...] = delta
            outs[4 * i + 2][...] = m_new
            outs[4 * i + 3][...] = v_new
        tot = sq_ref[0]
        for k in range(1, N_DEV):
            tot = tot + sq_ref[k]
        outs[4 * n][...] = (0.5 / D_MODEL) * jnp.sum(tot, axis=1, keepdims=True)

    out_shape = [SDS(a.shape, F32) for a in w for _ in range(4)] + [SDS((1, 1), F32)]
    outs = _pcall(body, name=name, out_shape=out_shape, compiler_params=_cp())(*parts, *w, *m, *v, sq_parts)
    return [outs[4 * i:4 * i + 4] for i in range(n)], outs[4 * n][0, 0]


def _adamw(parts, w, m, v, *, name):
    r, c = w.shape
    tr = r
    for cand in (256, 128, 88):
        if r % cand == 0:
            tr = cand
            break

    def body(p_ref, w_ref, m_ref, v_ref, g_ref, d_ref, mo_ref, vo_ref):
        g = p_ref[0].astype(F32)
        for k in range(1, N_DEV):
            g = g + p_ref[k].astype(F32)
        g_ref[...] = g
        d_ref[...], mo_ref[...], vo_ref[...] = _adam_math(g, w_ref[...], m_ref[...], v_ref[...])

    row = pl.BlockSpec((tr, c), lambda i: (i, 0))
    return _pcall(
        body, name=name, grid=(r // tr,),
        in_specs=[pl.BlockSpec((N_DEV, tr, c), lambda i: (0, i, 0)), row, row, row],
        out_specs=[row] * 4, out_shape=[SDS((r, c), F32)] * 4, compiler_params=_cp())(parts, w, m, v)


def _pad_cols(a, width):
    return jnp.pad(a, ((0, 0), (0, width - a.shape[1])))


def kernel(x, mem, positions, mix_norm_g, w_in, q_norm_g, k_norm_g, pool_w, pool_scale, w_out, cross_norm_g, mem_norm_g, w_cq, w_ckv, cq_norm_g, ck_norm_g, w_co, ffn_norm_g, w_gate_up, w_down, loss_target, m_mix_norm_g, m_w_in, m_q_norm_g, m_k_norm_g, m_pool_w, m_pool_scale, m_w_out, m_cross_norm_g, m_mem_norm_g, m_w_cq, m_w_ckv, m_cq_norm_g, m_ck_norm_g, m_w_co, m_ffn_norm_g, m_w_gate_up, m_w_down, v_mix_norm_g, v_w_in, v_q_norm_g, v_k_norm_g, v_pool_w, v_pool_scale, v_w_out, v_cross_norm_g, v_mem_norm_g, v_w_cq, v_w_ckv, v_cq_norm_g, v_ck_norm_g, v_w_co, v_ffn_norm_g, v_w_gate_up, v_w_down):
    given = dict(locals())
    w_f32 = {n: given[n][0] for n in SHARDED + SMALL}
    m_f32 = {n: given["m_" + n][0] for n in SHARDED + SMALL}
    v_f32 = {n: given["v_" + n][0] for n in SHARDED + SMALL}
    for d in (w_f32, m_f32, v_f32):
        d["w_gate_up"] = _pad_cols(d["w_gate_up"], FF_TILE)

    shards = {n: w_f32[n].astype(BF16) for n in SHARDED}
    w_in_full = _gather_weights({"w_in": shards.pop("w_in")})["w_in"]
    sm_rows = {n: (w_f32[n] if w_f32[n].ndim == 3 else w_f32[n].reshape(1, -1)) for n in SMALL}
    sq, dx, parts, gs = _local_step(x[0], mem[0], positions[0], loss_target[0], w_in_full, shards, sm_rows)

    flat = lambda a: a.reshape(-1, a.shape[-1])
    got = _exchange_small([flat(gs[n]) for n in SMALL] + [sq])
    small_res, loss = _adamw_small(got[:-1], [flat(sm_rows[n]) for n in SMALL],
                                   [flat(m_f32[n].reshape(sm_rows[n].shape)) for n in SMALL],
                                   [flat(v_f32[n].reshape(sm_rows[n].shape)) for n in SMALL], got[-1],
                                   name="adamw_small")
    res = {n: [a.reshape(w_f32[n].shape) for a in small_res[i]] for i, n in enumerate(SMALL)}
    for n in SHARDED:
        res[n] = _adamw(parts[n], w_f32[n], m_f32[n], v_f32[n], name="adamw_" + n)
    res["w_gate_up"] = [a[:, :FF_SHARD] for a in res["w_gate_up"]]
    order = ("mix_norm_g", "w_in", "q_norm_g", "k_norm_g", "pool_w", "pool_scale", "w_out", "cross_norm_g",
             "mem_norm_g", "w_cq", "w_ckv", "cq_norm_g", "ck_norm_g", "w_co", "ffn_norm_g", "w_gate_up", "w_down")
    outs = [loss, dx[None]]
    for which in range(4):
        outs += [res[n][which][None] for n in order]
    return tuple(outs)
```

```python
import functools

import jax
import jax.numpy as jnp
from jax import lax
from jax.experimental import pallas as pl
from jax.experimental.pallas import tpu as pltpu

F32 = jnp.float32
BF16 = jnp.bfloat16
SDS = jax.ShapeDtypeStruct

D_MODEL = 1024
HEAD_DIM = 128
N_GROUPS = 3
DILATIONS = (1, 4, 16)
ATT_HEADS = 4
Q_W = 1536
KV_W = 512
POOL_W = 512
POOL_WINDOWS = (2, 4, 8, 16)
IN_W = 3072
X_W = 512
N_MEM = 256
D_FF = 2816
FF_TILE = 768
FF_SHARD = 704
FF_PAD = 4 * FF_TILE
DOWN_SHARD = 352
ROT_DIM = 32
ROT_HALF = 16
ROPE_THETA = 500000.0
EPS = 1e-6
NEG_INF = -1e30
SCALE = HEAD_DIM ** -0.5
BLOCK = 128
HALO = 16

ADAM_LR = 0.001
ADAM_B1 = 0.9
ADAM_B2 = 0.999
ADAM_EPS = 1e-08
ADAM_WD = 0.01
ADAM_STEP = 10

N_DEV = 8
VMEM_LIMIT_BYTES = 56 * 1024 * 1024
MESH = pl.DeviceIdType.MESH


def _pcall(body, **kw):
    return pl.pallas_call(body, **kw)


def _cp():
    return pltpu.CompilerParams(vmem_limit_bytes=VMEM_LIMIT_BYTES)


def _dot(a, b):
    return lax.dot_general(a, b, (((1,), (0,)), ((), ())), preferred_element_type=F32)


def _dot_nt(a, b):
    return lax.dot_general(a, b, (((1,), (1,)), ((), ())), preferred_element_type=F32)


def _dot_tn(a, b):
    return lax.dot_general(a, b, (((0,), (0,)), ((), ())), preferred_element_type=F32)


def _rows(s):
    return min(512, s)


def _rms_r(x):
    return lax.rsqrt(jnp.mean(x * x, axis=-1, keepdims=True) + EPS)


def _norm_bwd(x, r, gain, dxn):
    z = dxn * gain
    dx = r * z - x * (r * r * r * jnp.mean(z * x, axis=-1, keepdims=True))
    dgain = jnp.sum(dxn * x * r, axis=0, keepdims=True)
    return dx, dgain


def _split_bf16(t):
    hi = t.astype(BF16)
    return hi, (t - hi.astype(F32)).astype(BF16)


def _lane_sums(t, ones):
    hi, lo = _split_bf16(t)
    return _dot(hi, ones) + _dot(lo, ones)


def _head_r(y, ones):
    return lax.rsqrt(_lane_sums(y * y, ones) * (1.0 / HEAD_DIM) + EPS)


def _head_norm_bwd(y, r, gain, dn, ones):
    z = dn * gain
    dy = r * z - y * (r * r * r * (_dot((z * y).astype(BF16), ones) * (1.0 / HEAD_DIM)))
    return dy, jnp.sum(dn * y * r, axis=0, keepdims=True)


def _swap_matrix():
    src = lax.broadcasted_iota(jnp.int32, (HEAD_DIM, HEAD_DIM), 0)
    dst = lax.broadcasted_iota(jnp.int32, (HEAD_DIM, HEAD_DIM), 1)
    hit = ((dst < ROT_HALF) & (src == dst + ROT_HALF)) | ((dst >= ROT_HALF) & (dst < ROT_DIM) & (src == dst - ROT_HALF))
    return jnp.where(hit, 1.0, 0.0).astype(BF16)


def _partner(t, swap):
    hi, lo = _split_bf16(t)
    return _dot(hi, swap) + _dot(lo, swap)


def _rope(n, cos_t, sin_t, swap):
    return n * cos_t + _partner(n, swap) * sin_t


def _rope_bwd(d, cos_t, sin_t, swap):
    return d * cos_t + _dot((d * sin_t).astype(BF16), swap)


def _resident(shape):
    return pl.BlockSpec(shape, lambda i: (0,) * len(shape), pipeline_mode=pl.Buffered(1))


def _chunks(n, cn):
    return [slice(j * cn, (j + 1) * cn) for j in range(n // cn)]


def _norm_linear(x, gain, w, *, cn, out_dtype, name, shards=None):
    s, k = x.shape
    n = w.shape[1]
    tm = _rows(s)
    names = tuple(shards or ())
    nr = len(names)

    def body(*refs):
        x_ref, g_ref, w_ref = refs[:3]
        y_ref, xn_ref = refs[3 + nr:5 + nr]
        if nr:
            start, relay, finish = _gather_ops(names, refs[3:3 + nr], refs[5 + nr:5 + 2 * nr], refs[5 + 2 * nr:])
            pl.when(pl.program_id(0) == 0)(start)
            pl.when(pl.program_id(0) == _relay_step(s // tm))(relay)
        xv = x_ref[...]
        xn_ref[...] = (xv * _rms_r(xv) * g_ref[...]).astype(BF16)
        for c in _chunks(n, cn):
            y_ref[:, c] = _dot(xn_ref[...], w_ref[:, c]).astype(out_dtype)
        if nr:
            pl.when(pl.program_id(0) == s // tm - 1)(finish)

    row = lambda w_: pl.BlockSpec((tm, w_), lambda i: (i, 0))
    outs = _pcall(
        body, name=name, grid=(s // tm,),
        in_specs=[row(k), _resident((1, k)), _resident((k, n))] + [HBM_SPEC] * nr,
        out_specs=[row(n), row(k)] + [HBM_SPEC] * nr,
        out_shape=[SDS((s, n), out_dtype), SDS((s, k), BF16)] + [SDS(FULL[a][0], BF16) for a in names],
        scratch_shapes=_gather_scratch(names),
        compiler_params=_cp())(x, gain, w, *[shards[a] for a in names])
    return (outs[0], outs[1], dict(zip(names, outs[2:]))) if nr else tuple(outs)


def _norm_linear_swiglu(x, gain, wgu, *, name):
    s, k = x.shape
    tm = _rows(s)

    def body(x_ref, g_ref, w_ref, gu_ref, a_ref, xn_ref):
        xv = x_ref[...]
        xn_ref[...] = (xv * _rms_r(xv) * g_ref[...]).astype(BF16)
        for c in _chunks(FF_PAD, FF_TILE):
            g = _dot(xn_ref[...], w_ref[:, c])
            u = _dot(xn_ref[...], w_ref[:, slice(FF_PAD + c.start, FF_PAD + c.stop)])
            a_ref[:, c] = (g * jax.nn.sigmoid(g) * u).astype(BF16)
            gu_ref[0, :, c] = g.astype(BF16)
            gu_ref[1, :, c] = u.astype(BF16)

    row = lambda w_: pl.BlockSpec((tm, w_), lambda i: (i, 0))
    return _pcall(
        body, name=name, grid=(s // tm,),
        in_specs=[row(k), _resident((1, k)), _resident((k, 2 * FF_PAD))],
        out_specs=[pl.BlockSpec((2, tm, FF_PAD), lambda i: (0, i, 0)), row(FF_PAD), row(k)],
        out_shape=[SDS((2, s, FF_PAD), BF16), SDS((s, FF_PAD), BF16), SDS((s, k), BF16)],
        compiler_params=_cp())(x, gain, wgu)


def _linear_res(a, w, res, *, cn, name):
    s, k = a.shape
    n = w.shape[1]
    tm = _rows(s)

    def body(a_ref, w_ref, r_ref, y_ref):
        for c in _chunks(n, cn):
            y_ref[:, c] = r_ref[:, c] + _dot(a_ref[...], w_ref[:, c])

    row = lambda w_: pl.BlockSpec((tm, w_), lambda i: (i, 0))
    return _pcall(
        body, name=name, grid=(s // tm,),
        in_specs=[row(k), _resident((k, n)), row(n)], out_specs=row(n),
        out_shape=SDS((s, n), F32), compiler_params=_cp())(a, w, res)


def _linear_res_loss(a, w, res, tgt, *, name):
    s, k = a.shape
    n = w.shape[1]
    tm = _rows(s)

    def body(a_ref, w_ref, r_ref, t_ref, dy_ref, dyb_ref, sq_ref):
        e = r_ref[...] + _dot(a_ref[...], w_ref[...]) - t_ref[...]
        dy = e * (1.0 / n)
        dy_ref[...] = dy
        dyb_ref[...] = dy.astype(BF16)

        @pl.when(pl.program_id(0) == 0)
        def _():
            sq_ref[...] = jnp.zeros_like(sq_ref)
        sq_ref[...] += jnp.sum(e * e, axis=0, keepdims=True)

    row = lambda w_: pl.BlockSpec((tm, w_), lambda i: (i, 0))
    return _pcall(
        body, name=name, grid=(s // tm,),
        in_specs=[row(k), _resident((k, n)), row(n), row(n)],
        out_specs=[row(n), row(n), pl.BlockSpec((1, n), lambda i: (0, 0))],
        out_shape=[SDS((s, n), F32), SDS((s, n), BF16), SDS((1, n), F32)],
        compiler_params=_cp())(a, w, res, tgt)


def _linear_nt(g, w, *, cn, name):
    s, k = g.shape
    n = w.shape[0]
    tm = _rows(s)

    def body(g_ref, w_ref, y_ref):
        for c in _chunks(n, cn):
            y_ref[:, c] = _dot_nt(g_ref[...], w_ref[c, :]).astype(BF16)

    row = lambda w_: pl.BlockSpec((tm, w_), lambda i: (i, 0))
    return _pcall(
        body, name=name, grid=(s // tm,),
        in_specs=[row(k), _resident((n, k))], out_specs=row(n),
        out_shape=SDS((s, n), BF16), compiler_params=_cp())(g, w)


def _swiglu_bwd(dyb, wd, gu, *, name):
    s, n = dyb.shape
    tm = _rows(s)

    def body(dy_ref, wd_ref, gu_ref, dgu_ref):
        for c in _chunks(FF_PAD, FF_TILE):
            da = _dot_nt(dy_ref[...], wd_ref[c, :])
            g = gu_ref[0, :, c].astype(F32)
            u = gu_ref[1, :, c].astype(F32)
            sg = jax.nn.sigmoid(g)
            dgu_ref[0, :, c] = (da * u * (sg * (1.0 + g * (1.0 - sg)))).astype(BF16)
            dgu_ref[1, :, c] = (da * (g * sg)).astype(BF16)

    half = pl.BlockSpec((2, tm, FF_PAD), lambda i: (0, i, 0))
    return _pcall(
        body, name=name, grid=(s // tm,),
        in_specs=[pl.BlockSpec((tm, n), lambda i: (i, 0)), _resident((FF_PAD, n)), half],
        out_specs=half, out_shape=SDS((2, s, FF_PAD), BF16), compiler_params=_cp())(dyb, wd, gu)


def _linear_nt_normbwd(g, w, x, dres, gain, *, name, grads=None):
    d, k = w.shape
    s = x.shape[0]
    tm = _rows(s)
    names = tuple(grads or ())
    nr = len(names)

    def body(*refs):
        g_ref, w_ref, x_ref, dr_ref, gn_ref = refs[:5]
        dx_ref, dxb_ref, dg_ref = refs[5 + nr:8 + nr]
        if nr:
            copies = _exchange_copies(names, refs[5:5 + nr], refs[8 + nr:8 + 2 * nr], *refs[8 + 2 * nr:])

        @pl.when(pl.program_id(0) == 0)
        def _():
            dg_ref[...] = jnp.zeros_like(dg_ref)
            if nr:
                _start(copies)

        if g.ndim == 3:
            dxn = _dot_nt(g_ref[0], w_ref[:, :k // 2]) + _dot_nt(g_ref[1], w_ref[:, k // 2:])
        else:
            dxn = _dot_nt(g_ref[...], w_ref[...])
        xv = x_ref[...]
        dx, dgain = _norm_bwd(xv, _rms_r(xv), gn_ref[...], dxn)
        out = dr_ref[...] + dx
        dx_ref[...] = out
        dxb_ref[...] = out.astype(BF16)
        dg_ref[...] += dgain

        if nr:
            @pl.when(pl.program_id(0) == s // tm - 1)
            def _():
                _finish(copies)

    row = pl.BlockSpec((tm, d), lambda i: (i, 0))
    g_spec = (pl.BlockSpec((2, tm, k // 2), lambda i: (0, i, 0)) if g.ndim == 3
              else pl.BlockSpec((tm, k), lambda i: (i, 0)))
    outs = _pcall(
        body, name=name, grid=(s // tm,),
        in_specs=[g_spec, _resident((d, k)), row, row, _resident((1, d))] + [HBM_SPEC] * nr,
        out_specs=[row, row, pl.BlockSpec((1, d), lambda i: (0, 0))] + [HBM_SPEC] * nr,
        out_shape=[SDS((s, d), F32), SDS((s, d), BF16), SDS((1, d), F32)]
        + [SDS((N_DEV,) + _shard_shape(n), BF16) for n in names],
        scratch_shapes=_comm_sems(nr) if nr else [],
        compiler_params=_cp())(g, w, x, dres, gain, *[grads[n] for n in names])
    return (outs[0], outs[1], outs[2], dict(zip(names, outs[3:]))) if nr else tuple(outs)


def _dw_tn(x, g, *, tkw, tn, name):
    s, kw = x.shape
    halves = g.ndim == 3
    n = 2 * g.shape[2] if halves else g.shape[1]
    ts = min(1024, s)
    ns = s // ts
    per_half = n // 2 // tn

    def body(x_ref, g_ref, o_ref, acc_ref):
        ss = pl.program_id(2)

        @pl.when(ss == 0)
        def _():
            acc_ref[...] = jnp.zeros_like(acc_ref)

        acc_ref[...] += _dot_tn(x_ref[...], g_ref[...])

        @pl.when(ss == ns - 1)
        def _():
            o_ref[...] = acc_ref[...].astype(BF16)

    g_spec = (pl.BlockSpec((None, ts, tn), lambda a, b, ss: (b // per_half, ss, b % per_half)) if halves
              else pl.BlockSpec((ts, tn), lambda a, b, ss: (ss, b)))
    return _pcall(
        body, name=name, grid=(kw // tkw, n // tn, ns),
        in_specs=[pl.BlockSpec((ts, tkw), lambda a, b, ss: (ss, a)), g_spec],
        out_specs=pl.BlockSpec((tkw, tn), lambda a, b, ss: (a, b)),
        out_shape=SDS((kw, n), BF16),
        scratch_shapes=[pltpu.VMEM((tkw, tn), F32)], compiler_params=_cp())(x, g)


def _qk_prep(proj, cos_t, sin_t, qg, kg, shards, *, name):
    s = proj.shape[0]
    tm = _rows(s)
    nqh = Q_W // HEAD_DIM
    names = tuple(shards)
    nr = len(names)

    def body(*refs):
        q_ref, k_ref, v_ref, c_ref, s_ref, qg_ref, kg_ref = refs[:7]
        qo_ref, ko_ref, vo_ref = refs[7 + nr:10 + nr]
        start, relay, finish = _gather_ops(names, refs[7:7 + nr], refs[10 + nr:10 + 2 * nr], refs[10 + 2 * nr:])
        pl.when(pl.program_id(0) == 0)(start)
        pl.when(pl.program_id(0) == _relay_step(s // tm))(relay)
        c, sn = c_ref[...], s_ref[...]
        ones, swap = jnp.ones((HEAD_DIM, HEAD_DIM), BF16), _swap_matrix()
        for h in range(nqh):
            sl = slice(h * HEAD_DIM, (h + 1) * HEAD_DIM)
            y = q_ref[:, sl]
            qo_ref[:, sl] = (_rope(y * _head_r(y, ones) * qg_ref[...], c, sn, swap) * SCALE).astype(BF16)
        for h in range(ATT_HEADS):
            sl = slice(h * HEAD_DIM, (h + 1) * HEAD_DIM)
            y = k_ref[:, sl]
            ko_ref[:, sl] = _rope(y * _head_r(y, ones) * kg_ref[...], c, sn, swap).astype(BF16)
        vo_ref[...] = v_ref[...].astype(BF16)
        pl.when(pl.program_id(0) == s // tm - 1)(finish)

    row = lambda w, j: pl.BlockSpec((tm, w), lambda i: (i, j))
    one = pl.BlockSpec((1, HEAD_DIM), lambda i: (0, 0))
    outs = _pcall(
        body, name=name, grid=(s // tm,),
        in_specs=[row(Q_W, 0), row(KV_W, 3), row(KV_W, 4), row(HEAD_DIM, 0), row(HEAD_DIM, 0), one, one]
        + [HBM_SPEC] * nr,
        out_specs=[row(Q_W, 0), row(KV_W, 0), row(KV_W, 0)] + [HBM_SPEC] * nr,
        out_shape=[SDS((s, Q_W), BF16), SDS((s, KV_W), BF16), SDS((s, KV_W), BF16)]
        + [SDS(FULL[a][0], BF16) for a in names],
        scratch_shapes=_gather_scratch(names),
        compiler_params=_cp())(proj, proj, proj, cos_t, sin_t, qg, kg, *[shards[a] for a in names])
    return outs[0], outs[1], outs[2], dict(zip(names, outs[3:]))


def _qk_prep_bwd(proj, dq, dk, dv, du, cos_t, sin_t, qg, kg, *, name):
    s = proj.shape[0]
    tm = _rows(s)

    def body(q_ref, k_ref, dq_ref, dk_ref, dv_ref, du_ref, c_ref, s_ref, qg_ref, kg_ref, dp_ref, dqg_ref, dkg_ref):
        c, sn = c_ref[...], s_ref[...]

        @pl.when(pl.program_id(0) == 0)
        def _():
            dqg_ref[...] = jnp.zeros_like(dqg_ref)
            dkg_ref[...] = jnp.zeros_like(dkg_ref)

        ones, swap = jnp.ones((HEAD_DIM, HEAD_DIM), BF16), _swap_matrix()
        dqg = jnp.zeros((1, HEAD_DIM), F32)
        for g in range(N_GROUPS):
            for h in range(ATT_HEADS):
                sl = slice(h * HEAD_DIM, (h + 1) * HEAD_DIM)
                col = slice(g * KV_W + h * HEAD_DIM, g * KV_W + (h + 1) * HEAD_DIM)
                y = q_ref[:, col]
                dn = _rope_bwd(dq_ref[g, :, sl].astype(F32) * SCALE, c, sn, swap)
                dy, dgain = _head_norm_bwd(y, _head_r(y, ones), qg_ref[...], dn, ones)
                dp_ref[:, col] = dy.astype(BF16)
                dqg = dqg + dgain
        dqg_ref[...] += dqg

        dkg = jnp.zeros((1, HEAD_DIM), F32)
        for h in range(ATT_HEADS):
            sl = slice(h * HEAD_DIM, (h + 1) * HEAD_DIM)
            y = k_ref[:, sl]
            dn = _rope_bwd(dk_ref[:, sl], c, sn, swap)
            dy, dgain = _head_norm_bwd(y, _head_r(y, ones), kg_ref[...], dn, ones)
            dp_ref[:, Q_W + h * HEAD_DIM:Q_W + (h + 1) * HEAD_DIM] = dy.astype(BF16)
            dkg = dkg + dgain
        dkg_ref[...] += dkg

        dp_ref[:, Q_W + KV_W:Q_W + 2 * KV_W] = dv_ref[...].astype(BF16)
        dp_ref[:, Q_W + 2 * KV_W:] = du_ref[...].astype(BF16)

    row = lambda w, j: pl.BlockSpec((tm, w), lambda i: (i, j))
    one = pl.BlockSpec((1, HEAD_DIM), lambda i: (0, 0))
    return _pcall(
        body, name=name, grid=(s // tm,),
        in_specs=[row(Q_W, 0), row(KV_W, 3), pl.BlockSpec((N_GROUPS, tm, KV_W), lambda i: (0, i, 0))]
        + [row(KV_W, 0)] * 3 + [row(HEAD_DIM, 0), row(HEAD_DIM, 0), one, one],
        out_specs=[row(IN_W, 0), one, one],
        out_shape=[SDS((s, IN_W), BF16), SDS((1, HEAD_DIM), F32), SDS((1, HEAD_DIM), F32)],
        compiler_params=_cp())(proj, proj, dq, dk, dv, du, cos_t, sin_t, qg, kg)


ATT_ROWS = 16 * BLOCK


def _sub(ref, start, d, size=BLOCK):
    return ref[pl.ds(start, size, stride=d), :] if d > 1 else ref[pl.ds(start, size), :]


def _sub_set(ref, start, d, val):
    if d > 1:
        ref[pl.ds(start, BLOCK, stride=d), :] = val
    else:
        ref[pl.ds(start, BLOCK), :] = val


def _band_masks():
    row = lax.broadcasted_iota(jnp.int32, (BLOCK, BLOCK), 0)
    col = lax.broadcasted_iota(jnp.int32, (BLOCK, BLOCK), 1)
    return col <= row, col >= row


def _eye(n=BLOCK):
    row = lax.broadcasted_iota(jnp.int32, (n, n), 0)
    col = lax.broadcasted_iota(jnp.int32, (n, n), 1)
    return jnp.where(row == col, 1.0, 0.0).astype(BF16)


def _transpose_mxu(eye, a):
    return _dot_nt(eye, a).astype(BF16)


def _attn_fwd(q_rot, k_rot, v, shards, *, name):
    s = q_rot.shape[0]
    rr = ATT_ROWS
    nblk = s // rr
    names = tuple(shards)
    nr = len(names)

    def body(*refs):
        q0, q1, q2, kp, kc, vp, vc = refs[:7]
        mix_ref, lse_ref = refs[7 + nr:9 + nr]
        qs, ks, vs, os_, ls = refs[9 + 2 * nr:14 + 2 * nr]
        h, n = pl.program_id(0), pl.program_id(1)
        start, relay, finish = _gather_ops(names, refs[7:7 + nr], refs[9 + nr:9 + 2 * nr], refs[14 + 2 * nr:])
        pl.when((h == 0) & (n == 0))(start)
        pl.when(h * nblk + n == _relay_step(ATT_HEADS * nblk))(relay)
        for g, q_ref in enumerate((q0, q1, q2)):
            qs[g] = q_ref[...].astype(F32)
        ks[:rr] = kp[...].astype(F32)
        ks[rr:] = kc[...].astype(F32)
        vs[:rr] = vp[...].astype(F32)
        vs[rr:] = vc[...].astype(F32)
        m_cur, m_band = _band_masks()
        mask_in = jnp.concatenate([m_band, m_cur], axis=1)
        mask_first = jnp.concatenate([m_band & (n > 0), m_cur], axis=1)
        ones = jnp.ones((2 * BLOCK, HEAD_DIM), BF16)
        pieces = [(g, d, j * BLOCK * d + r, j) for g, d in enumerate(DILATIONS) for r in range(d)
                  for j in range(rr // (BLOCK * d))]

        def scores(piece):
            g, d, base, j = piece
            q = _sub(qs.at[g], base, d).astype(BF16)
            k2 = _sub(ks, rr + base - BLOCK * d, d, 2 * BLOCK).astype(BF16)
            return jnp.where(mask_first if j == 0 else mask_in, _dot_nt(q, k2), NEG_INF)

        sc = scores(pieces[0])
        for i, (g, d, base, j) in enumerate(pieces):
            cur = sc
            if i + 1 < len(pieces):
                sc = scores(pieces[i + 1])
            m = jnp.max(cur, axis=-1, keepdims=True)
            p = jnp.exp(cur - m).astype(BF16)
            v2 = _sub(vs, rr + base - BLOCK * d, d, 2 * BLOCK).astype(BF16)
            acc_l = _dot(p, jnp.concatenate([v2, ones], axis=1))
            l = acc_l[:, HEAD_DIM:]
            _sub_set(os_.at[g], base, d, acc_l[:, :HEAD_DIM] / l)
            _sub_set(ls.at[g], base, d, m + jnp.log(l))
        for c in _chunks(rr, 2 * BLOCK):
            a, b, cc = ls[0, c, :], ls[1, c, :], ls[2, c, :]
            m = jnp.maximum(jnp.maximum(a, b), cc)
            wa, wb, wc = jnp.exp(a - m), jnp.exp(b - m), jnp.exp(cc - m)
            den = wa + wb + wc
            mix_ref[c, :] = ((wa * os_[0, c, :] + wb * os_[1, c, :] + wc * os_[2, c, :]) / den).astype(BF16)
            lse_ref[c, :] = m + jnp.log(den)

        pl.when((h == ATT_HEADS - 1) & (n == nblk - 1))(finish)

    blk = lambda f: pl.BlockSpec((rr, HEAD_DIM), f)
    prv = lambda n: jnp.maximum(n - 1, 0)
    outs = _pcall(
        body, name=name, grid=(ATT_HEADS, nblk),
        in_specs=[blk(lambda h, n, g=g: (n, g * ATT_HEADS + h)) for g in range(N_GROUPS)]
        + [blk(lambda h, n: (prv(n), h)), blk(lambda h, n: (n, h))] * 2 + [HBM_SPEC] * nr,
        out_specs=[blk(lambda h, n: (n, h)), blk(lambda h, n: (n, h))] + [HBM_SPEC] * nr,
        out_shape=[SDS((s, KV_W + POOL_W), BF16), SDS((s, KV_W), F32)] + [SDS(FULL[w][0], BF16) for w in names],
        scratch_shapes=[pltpu.VMEM((N_GROUPS, rr, HEAD_DIM), F32), pltpu.VMEM((2 * rr, HEAD_DIM), F32),
                        pltpu.VMEM((2 * rr, HEAD_DIM), F32), pltpu.VMEM((N_GROUPS, rr, HEAD_DIM), F32),
                        pltpu.VMEM((N_GROUPS, rr, HEAD_DIM), F32)] + _gather_scratch(names),
        compiler_params=_cp())(q_rot, q_rot, q_rot, k_rot, k_rot, v, v, *[shards[w] for w in names])
    return outs[0], outs[1], dict(zip(names, outs[2:]))


def _attn_bwd(q_rot, k_rot, v, mix, dmix, lse, grads, *, name):
    s = q_rot.shape[0]
    rr = ATT_ROWS
    nblk = s // rr
    names = tuple(grads)
    nr = len(names)

    def body(*refs):
        q0, q1, q2, qx0, qx1, qx2, kp, kc, vp, vc, do_c, do_x, o_c, o_x, l_c, l_x = refs[:16]
        dq_ref, dk_ref, dv_ref = refs[16 + nr:19 + nr]
        qs, ks, vs, dos, lss, dls, dqs, dks, dvs, send_sems, recv_sems, local_sems = refs[19 + 2 * nr:]
        h, n = pl.program_id(0), pl.program_id(1)
        copies = _exchange_copies(names, refs[16:16 + nr], refs[19 + nr:19 + 2 * nr], send_sems, recv_sems, local_sems)

        @pl.when((h == 0) & (n == 0))
        def _():
            _start(copies)

        for g, (qc_ref, qx_ref) in enumerate(((q0, qx0), (q1, qx1), (q2, qx2))):
            qs[g, :rr] = qc_ref[...].astype(F32)
            qs[g, rr:] = qx_ref[...].astype(F32)
        ks[:rr] = kp[...].astype(F32)
        ks[rr:] = kc[...].astype(F32)
        vs[:rr] = vp[...].astype(F32)
        vs[rr:] = vc[...].astype(F32)
        lss[:rr] = l_c[...]
        lss[rr:] = l_x[...]
        for half, (d_ref, o_ref) in enumerate(((do_c, o_c), (do_x, o_x))):
            for c in _chunks(rr, 2 * BLOCK):
                cs = slice(half * rr + c.start, half * rr + c.stop)
                dof = d_ref[c, :].astype(F32)
                dos[cs, :] = dof
                dls[cs, :] = jnp.broadcast_to(jnp.sum(dof * o_ref[c, :].astype(F32), axis=-1, keepdims=True),
                                              (2 * BLOCK, HEAD_DIM))
        m_cur, m_band = _band_masks()
        mask_in = jnp.concatenate([m_cur, m_band], axis=0)
        mask_last = jnp.concatenate([m_cur, m_band & (n + 1 < nblk)], axis=0)
        m_first = m_band & (n > 0)
        eye = _eye()
        pieces = [(g, d, j * BLOCK * d + r, j, rr // (BLOCK * d)) for g, d in enumerate(DILATIONS) for r in range(d)
                  for j in range(rr // (BLOCK * d))]

        def front(piece):
            g, d, base, _, _ = piece
            q2 = _sub(qs.at[g], base, d, 2 * BLOCK).astype(BF16)
            do2 = _sub(dos, base, d, 2 * BLOCK).astype(BF16)
            k = _sub(ks, rr + base, d).astype(BF16)
            vv = _sub(vs, rr + base, d).astype(BF16)
            return q2, do2, k, _dot_nt(q2, k), _dot_nt(do2, vv)

        nxt_front = front(pieces[0])
        dq_acc = None
        for i, (g, d, base, j, nsub) in enumerate(pieces):
            q2, do2, k, s2, dp2 = nxt_front
            if j == 0:
                kp_ = _sub(ks, rr + base - BLOCK * d, d).astype(BF16)
                p0 = jnp.where(m_first, jnp.exp(_dot_nt(q2[:BLOCK], kp_) - _sub(lss, base, d)), 0.0)
                ds0 = p0 * (_dot_nt(do2[:BLOCK], _sub(vs, rr + base - BLOCK * d, d).astype(BF16)) - _sub(dls, base, d))
                dq_acc = _dot(ds0.astype(BF16), kp_)
            if i + 1 < len(pieces):
                nxt_front = front(pieces[i + 1])
            p2 = jnp.where(mask_last if j + 1 == nsub else mask_in,
                           jnp.exp(s2 - _sub(lss, base, d, 2 * BLOCK)), 0.0)
            ds2 = (p2 * (dp2 - _sub(dls, base, d, 2 * BLOCK))).astype(BF16)
            dq2 = _dot(ds2, k)
            _sub_set(dqs.at[g], base, d, dq_acc + dq2[:BLOCK])
            dq_acc = dq2[BLOCK:]
            dk = _dot(_transpose_mxu(eye, ds2), q2)
            dv = _dot(_transpose_mxu(eye, p2.astype(BF16)), do2)
            if g == 0:
                _sub_set(dks, base, d, dk)
                _sub_set(dvs, base, d, dv)
            else:
                _sub_set(dks, base, d, _sub(dks, base, d) + dk)
                _sub_set(dvs, base, d, _sub(dvs, base, d) + dv)
        for g in range(N_GROUPS):
            dq_ref[g] = dqs[g].astype(BF16)
        dk_ref[...] = dks[...]
        dv_ref[...] = dvs[...]

        @pl.when((h == ATT_HEADS - 1) & (n == nblk - 1))
        def _():
            _finish(copies)

    blk = lambda f: pl.BlockSpec((rr, HEAD_DIM), f)
    prv = lambda n: jnp.maximum(n - 1, 0)
    nxt = lambda n: jnp.minimum(n + 1, nblk - 1)
    cur_kv = blk(lambda h, n: (n, h))
    outs = _pcall(
        body, name=name, grid=(ATT_HEADS, nblk),
        in_specs=[blk(lambda h, n, g=g: (n, g * ATT_HEADS + h)) for g in range(N_GROUPS)]
        + [blk(lambda h, n, g=g: (nxt(n), g * ATT_HEADS + h)) for g in range(N_GROUPS)]
        + [blk(lambda h, n: (prv(n), h)), cur_kv] * 2
        + [cur_kv, blk(lambda h, n: (nxt(n), h))] * 3 + [HBM_SPEC] * nr,
        out_specs=[pl.BlockSpec((N_GROUPS, rr, HEAD_DIM), lambda h, n: (0, n, h)), cur_kv, cur_kv] + [HBM_SPEC] * nr,
        out_shape=[SDS((N_GROUPS, s, KV_W), BF16), SDS((s, KV_W), F32), SDS((s, KV_W), F32)]
        + [SDS((N_DEV,) + _shard_shape(w), BF16) for w in names],
        scratch_shapes=[pltpu.VMEM((N_GROUPS, 2 * rr, HEAD_DIM), F32)] + [pltpu.VMEM((2 * rr, HEAD_DIM), F32)] * 5
        + [pltpu.VMEM((N_GROUPS, rr, HEAD_DIM), F32), pltpu.VMEM((rr, HEAD_DIM), F32), pltpu.VMEM((rr, HEAD_DIM), F32)]
        + _comm_sems(nr),
        compiler_params=_cp())(q_rot, q_rot, q_rot, q_rot, q_rot, q_rot, k_rot, k_rot, v, v, dmix, dmix, mix, mix,
                               lse, lse, *[grads[w] for w in names])
    return outs[0], outs[1], outs[2], dict(zip(names, outs[3:]))


def _pool_d(u_ref, halo_ref, i, tm):
    halo = jnp.where(i > 0, halo_ref[...], 0.0)
    t = i * tm + lax.broadcasted_iota(jnp.int32, (tm, 1), 0)
    out = []
    for g, w in enumerate(POOL_WINDOWS):
        sl = slice(g * HEAD_DIM, (g + 1) * HEAD_DIM)
        u = u_ref[:, sl]
        acc = jnp.concatenate([halo[:, sl], u], axis=0)
        sh = 1
        while sh < w:
            acc = acc + pltpu.roll(acc, sh, 0)
            sh *= 2
        cnt = jnp.minimum(t + 1, w).astype(F32)
        out.append(acc[HALO:, :] / cnt - u)
    return out


def _pool_fwd(proj, mix, pool_w, pool_scale, *, name):
    s = proj.shape[0]
    tm = _rows(s)
    ucol = (IN_W - POOL_W) // POOL_W

    def body(u_ref, halo_ref, mix_in, w_ref, sc_ref, o_ref):
        del mix_in
        dd = _pool_d(u_ref, halo_ref, pl.program_id(0), tm)
        for g in range(len(POOL_WINDOWS)):
            sl = slice(g * HEAD_DIM, (g + 1) * HEAD_DIM)
            y = _dot(dd[g].astype(BF16), w_ref[g].astype(BF16))
            o_ref[:, sl] = (y * sc_ref[:, sl]).astype(BF16)

    return _pcall(
        body, name=name, grid=(s // tm,),
        in_specs=[pl.BlockSpec((tm, POOL_W), lambda i: (i, ucol)),
                  pl.BlockSpec((HALO, POOL_W), lambda i: (jnp.maximum(i * (tm // HALO) - 1, 0), ucol)),
                  pl.BlockSpec(memory_space=pl.ANY),
                  pl.BlockSpec((len(POOL_WINDOWS), HEAD_DIM, HEAD_DIM), lambda i: (0, 0, 0)),
                  pl.BlockSpec((1, POOL_W), lambda i: (0, 0))],
        out_specs=pl.BlockSpec((tm, POOL_W), lambda i: (i, 1)),
        out_shape=SDS(mix.shape, BF16), input_output_aliases={2: 0},
        compiler_params=_cp())(proj, proj, mix, pool_w, pool_scale)


def _pool_bwd(proj, dmix, pool_w, pool_scale, *, name):
    s = proj.shape[0]
    tm = _rows(s)
    nblk = s // tm
    ucol = (IN_W - POOL_W) // POOL_W
    ng = len(POOL_WINDOWS)

    def body(u_ref, halo_ref, dp_ref, dpn_ref, w_ref, sc_ref, du_ref, dw_ref, dsc_ref):
        i = pl.program_id(0)

        @pl.when(i == 0)
        def _():
            dw_ref[...] = jnp.zeros_like(dw_ref)
            dsc_ref[...] = jnp.zeros_like(dsc_ref)

        dd = _pool_d(u_ref, halo_ref, i, tm)
        t = i * tm + lax.broadcasted_iota(jnp.int32, (tm, 1), 0)
        dpn = jnp.where(i + 1 < nblk, dpn_ref[...].astype(F32), 0.0)
        for g, w in enumerate(POOL_WINDOWS):
            sl = slice(g * HEAD_DIM, (g + 1) * HEAD_DIM)
            wg = w_ref[g].astype(BF16)
            db = dd[g].astype(BF16)
            dp = dp_ref[:, sl].astype(F32)
            dsc_ref[:, sl] += jnp.sum(dp * _dot(db, wg), axis=0, keepdims=True)
            dy = (dp * sc_ref[:, sl]).astype(BF16)
            dw_ref[g] += _dot_tn(db, dy)
            g_d = _dot_nt(dy, wg)
            g_dn = _dot_nt((dpn[:, sl] * sc_ref[:, sl]).astype(BF16), wg)
            cnt = jnp.minimum(t + 1, w).astype(F32)
            acc = jnp.concatenate([g_d / cnt, g_dn * (1.0 / w)], axis=0)
            sh = 1
            while sh < w:
                acc = acc + pltpu.roll(acc, tm + HALO - sh, 0)
                sh *= 2
            du_ref[:, sl] = acc[:tm, :] - g_d

    nh = s // HALO
    return _pcall(
        body, name=name, grid=(nblk,),
        in_specs=[pl.BlockSpec((tm, POOL_W), lambda i: (i, ucol)),
                  pl.BlockSpec((HALO, POOL_W), lambda i: (jnp.maximum(i * (tm // HALO) - 1, 0), ucol)),
                  pl.BlockSpec((tm, POOL_W), lambda i: (i, 1)),
                  pl.BlockSpec((HALO, POOL_W), lambda i: (jnp.minimum((i + 1) * (tm // HALO), nh - 1), 1)),
                  pl.BlockSpec((ng, HEAD_DIM, HEAD_DIM), lambda i: (0, 0, 0)),
                  pl.BlockSpec((1, POOL_W), lambda i: (0, 0))],
        out_specs=[pl.BlockSpec((tm, POOL_W), lambda i: (i, 0)),
                   pl.BlockSpec((ng, HEAD_DIM, HEAD_DIM), lambda i: (0, 0, 0)),
                   pl.BlockSpec((1, POOL_W), lambda i: (0, 0))],
        out_shape=[SDS((s, POOL_W), F32), SDS((ng, HEAD_DIM, HEAD_DIM), F32), SDS((1, POOL_W), F32)],
        compiler_params=_cp())(proj, proj, dmix, dmix, pool_w, pool_scale)


def _mem_fwd(mem, mem_g, wckv, ck_g, *, name):
    def body(m_ref, g_ref, w_ref, kg_ref, mn_ref, ckr_ref, ckn_ref, cv_ref):
        mv = m_ref[...]
        mn = (mv * _rms_r(mv) * g_ref[...]).astype(BF16)
        mn_ref[...] = mn
        ckv = _dot(mn, w_ref[...])
        ckr_ref[...] = ckv[:, :X_W]
        cv_ref[...] = ckv[:, X_W:].astype(BF16)
        for h in range(ATT_HEADS):
            sl = slice(h * HEAD_DIM, (h + 1) * HEAD_DIM)
            y = ckv[:, sl]
            ckn_ref[:, sl] = (y * _rms_r(y) * kg_ref[...]).astype(BF16)

    return _pcall(
        body, name=name,
        out_shape=[SDS((N_MEM, D_MODEL), BF16), SDS((N_MEM, X_W), F32), SDS((N_MEM, X_W), BF16),
                   SDS((N_MEM, X_W), BF16)],
        compiler_params=_cp())(mem, mem_g, wckv, ck_g)


def _cross_q(cq_ref, g_ref, sl):
    y = cq_ref[:, sl]
    r = _rms_r(y)
    return y, r, y * r * g_ref[...] * SCALE


def _cross_fwd(cq_raw, ck_n, cv, cq_g, *, name):
    s = cq_raw.shape[0]
    tm = _rows(s)

    def body(cq_ref, k_ref, v_ref, g_ref, o_ref):
        for h in range(ATT_HEADS):
            sl = slice(h * HEAD_DIM, (h + 1) * HEAD_DIM)
            _, _, qn = _cross_q(cq_ref, g_ref, sl)
            sc = _dot_nt(qn.astype(BF16), k_ref[:, sl])
            p = jnp.exp(sc - jnp.max(sc, axis=-1, keepdims=True))
            p = p / jnp.sum(p, axis=-1, keepdims=True)
            o_ref[:, sl] = _dot(p.astype(BF16), v_ref[:, sl]).astype(BF16)

    full = lambda a: pl.BlockSpec(a.shape, lambda i: (0, 0))
    return _pcall(
        body, name=name, grid=(s // tm,),
        in_specs=[pl.BlockSpec((tm, X_W), lambda i: (i, 0)), full(ck_n), full(cv), full(cq_g)],
        out_specs=pl.BlockSpec((tm, X_W), lambda i: (i, 0)),
        out_shape=SDS((s, X_W), BF16), compiler_params=_cp())(cq_raw, ck_n, cv, cq_g)


def _cross_bwd(d_o, cq_raw, ck_n, cv, cq_g, *, name):
    s = cq_raw.shape[0]
    tm = _rows(s)

    def body(do_ref, cq_ref, k_ref, v_ref, g_ref, dcq_ref, dk_ref, dv_ref, dg_ref):
        @pl.when(pl.program_id(0) == 0)
        def _():
            dk_ref[...] = jnp.zeros_like(dk_ref)
            dv_ref[...] = jnp.zeros_like(dv_ref)
            dg_ref[...] = jnp.zeros_like(dg_ref)

        dg = jnp.zeros((1, HEAD_DIM), F32)
        for h in range(ATT_HEADS):
            sl = slice(h * HEAD_DIM, (h + 1) * HEAD_DIM)
            y, r, qn = _cross_q(cq_ref, g_ref, sl)
            qb = qn.astype(BF16)
            do = do_ref[:, sl]
            sc = _dot_nt(qb, k_ref[:, sl])
            p = jnp.exp(sc - jnp.max(sc, axis=-1, keepdims=True))
            p = p / jnp.sum(p, axis=-1, keepdims=True)
            dv_ref[:, sl] += _dot_tn(p.astype(BF16), do)
            dp = _dot_nt(do, v_ref[:, sl])
            ds = (p * (dp - jnp.sum(dp * p, axis=-1, keepdims=True))).astype(BF16)
            dk_ref[:, sl] += _dot_tn(ds, qb)
            dn = _dot(ds, k_ref[:, sl]) * SCALE
            dy, dgain = _norm_bwd(y, r, g_ref[...], dn)
            dcq_ref[:, sl] = dy.astype(BF16)
            dg = dg + dgain
        dg_ref[...] += dg

    full = lambda a: pl.BlockSpec(a.shape, lambda i: (0, 0))
    row = pl.BlockSpec((tm, X_W), lambda i: (i, 0))
    return _pcall(
        body, name=name, grid=(s // tm,),
        in_specs=[row, row, full(ck_n), full(cv), full(cq_g)],
        out_specs=[row, pl.BlockSpec((N_MEM, X_W), lambda i: (0, 0)), pl.BlockSpec((N_MEM, X_W), lambda i: (0, 0)),
                   pl.BlockSpec((1, HEAD_DIM), lambda i: (0, 0))],
        out_shape=[SDS((s, X_W), BF16), SDS((N_MEM, X_W), F32), SDS((N_MEM, X_W), F32), SDS((1, HEAD_DIM), F32)],
        compiler_params=_cp())(d_o, cq_raw, ck_n, cv, cq_g)


def _mem_bwd(dck_n, dcv, ck_raw, memn, mem, wckv, mem_g, ck_g, *, name):
    def body(dk_ref, dv_ref, ckr_ref, mn_ref, m_ref, w_ref, mg_ref, kg_ref, dw_ref, dmg_ref, dkg_ref, dckv_s):
        dkg = jnp.zeros((1, HEAD_DIM), F32)
        for h in range(ATT_HEADS):
            sl = slice(h * HEAD_DIM, (h + 1) * HEAD_DIM)
            y = ckr_ref[:, sl]
            dy, dgain = _norm_bwd(y, _rms_r(y), kg_ref[...], dk_ref[:, sl])
            dckv_s[:, sl] = dy.astype(BF16)
            dkg = dkg + dgain
        dkg_ref[...] = dkg
        dckv_s[:, X_W:] = dv_ref[...].astype(BF16)
        dckv = dckv_s[...]
        dw_ref[...] = _dot_tn(mn_ref[...], dckv).astype(BF16)
        dmn = _dot_nt(dckv, w_ref[...])
        mv = m_ref[...]
        dmg_ref[...] = jnp.sum(dmn * mv * _rms_r(mv), axis=0, keepdims=True)

    return _pcall(
        body, name=name,
        out_shape=[SDS((D_MODEL, 2 * X_W), BF16), SDS((1, D_MODEL), F32), SDS((1, HEAD_DIM), F32)],
        scratch_shapes=[pltpu.VMEM((N_MEM, 2 * X_W), BF16)],
        compiler_params=_cp())(dck_n, dcv, ck_raw, memn, mem, wckv, mem_g, ck_g)


def _rope_tables(pos):
    inv_freq = ROPE_THETA ** (-jnp.arange(0, ROT_DIM, 2, dtype=F32) / ROT_DIM)
    ang = pos.astype(F32)[:, None] * inv_freq
    cos, sin = jnp.cos(ang), jnp.sin(ang)
    s = pos.shape[0]
    rest = HEAD_DIM - ROT_DIM
    cos_t = jnp.concatenate([cos, cos, jnp.ones((s, rest), F32)], axis=1)
    sin_t = jnp.concatenate([-sin, sin, jnp.zeros((s, rest), F32)], axis=1)
    return cos_t, sin_t


def _local_step(x, mem, pos, tgt, w_in, shards, sm):
    cos_t, sin_t = _rope_tables(pos)
    wb = {"w_in": w_in}

    pick = lambda *names: {a: shards[a] for a in names}
    proj, xn, got = _norm_linear(x, sm["mix_norm_g"], wb["w_in"], cn=768, out_dtype=F32, name="fwd_in_proj",
                                 shards=pick("w_out", "w_cq", "w_ckv", "w_co"))
    wb.update(got)
    q_rot, k_rot, v, got = _qk_prep(proj, cos_t, sin_t, sm["q_norm_g"], sm["k_norm_g"], pick("w_down"),
                                    name="fwd_qk_prep")
    wb.update(got)
    mix, lse_all, got = _attn_fwd(q_rot, k_rot, v, pick("w_gate_up"), name="fwd_attn")
    wb.update(got)
    mix = _pool_fwd(proj, mix, sm["pool_w"], sm["pool_scale"], name="fwd_pool")
    h1 = _linear_res(mix, wb["w_out"], x, cn=512, name="fwd_out_proj")
    cq_raw, hn = _norm_linear(h1, sm["cross_norm_g"], wb["w_cq"], cn=512, out_dtype=F32, name="fwd_cq_proj")
    memn, ck_raw, ck_n, cv = _mem_fwd(mem, sm["mem_norm_g"], wb["w_ckv"], sm["ck_norm_g"], name="fwd_mem")
    co = _cross_fwd(cq_raw, ck_n, cv, sm["cq_norm_g"], name="fwd_cross")
    h2 = _linear_res(co, wb["w_co"], h1, cn=512, name="fwd_co_proj")
    gu, act, fn = _norm_linear_swiglu(h2, sm["ffn_norm_g"], wb["w_gate_up"], name="fwd_gate_up")
    dy, dyb, sq = _linear_res_loss(act, wb["w_down"], h2, tgt, name="fwd_down_loss")

    gw = {}
    gs = {}
    dgu = _swiglu_bwd(dyb, wb["w_down"], gu, name="bwd_swiglu")
    gw["w_down"] = _dw_tn(act, dyb, tkw=1024, tn=1024, name="bwd_dw_down")
    dh2, dh2b, gs["ffn_norm_g"] = _linear_nt_normbwd(dgu, wb["w_gate_up"], h2, dy, sm["ffn_norm_g"],
                                                     name="bwd_ffn_in")
    gw["w_gate_up"] = _dw_tn(fn, dgu, tkw=1024, tn=1536, name="bwd_dw_gate_up")

    d_co = _linear_nt(dh2b, wb["w_co"], cn=512, name="bwd_co_proj")
    gw["w_co"] = _dw_tn(co, dh2b, tkw=512, tn=1024, name="bwd_dw_co")
    dcq, dck_n, dcv, gs["cq_norm_g"] = _cross_bwd(d_co, cq_raw, ck_n, cv, sm["cq_norm_g"], name="bwd_cross")
    gw["w_ckv"], gs["mem_norm_g"], gs["ck_norm_g"] = _mem_bwd(dck_n, dcv, ck_raw, memn, mem, wb["w_ckv"],
                                                             sm["mem_norm_g"], sm["ck_norm_g"], name="bwd_mem")
    dh1, dh1b, gs["cross_norm_g"] = _linear_nt_normbwd(dcq, wb["w_cq"], h1, dh2, sm["cross_norm_g"],
                                                       name="bwd_cq_in")
    gw["w_cq"] = _dw_tn(hn, dcq, tkw=1024, tn=512, name="bwd_dw_cq")

    dmix = _linear_nt(dh1b, wb["w_out"], cn=512, name="bwd_out_proj")
    gw["w_out"] = _dw_tn(mix, dh1b, tkw=1024, tn=1024, name="bwd_dw_out")
    du, gs["pool_w"], gs["pool_scale"] = _pool_bwd(proj, dmix, sm["pool_w"], sm["pool_scale"], name="bwd_pool")
    dq, dk, dv, parts = _attn_bwd(q_rot, k_rot, v, mix, dmix, lse_all, gw, name="bwd_attn")
    dproj, gs["q_norm_g"], gs["k_norm_g"] = _qk_prep_bwd(proj, dq, dk, dv, du, cos_t, sin_t, sm["q_norm_g"],
                                                         sm["k_norm_g"], name="bwd_qk_prep")
    gw_in = _dw_tn(xn, dproj, tkw=1024, tn=1536, name="bwd_dw_in")
    dx, _, gs["mix_norm_g"], last = _linear_nt_normbwd(dproj, wb["w_in"], x, dh1, sm["mix_norm_g"],
                                                       name="bwd_in_proj", grads={"w_in": gw_in})
    parts.update(last)
    return sq, dx, parts, gs


SHARDED = ("w_in", "w_out", "w_cq", "w_ckv", "w_co", "w_gate_up", "w_down")
SMALL = ("mix_norm_g", "q_norm_g", "k_norm_g", "pool_w", "pool_scale", "cross_norm_g", "mem_norm_g", "cq_norm_g",
         "ck_norm_g", "ffn_norm_g")
FULL = {
    "w_in": ((D_MODEL, IN_W), 1, IN_W // N_DEV),
    "w_out": ((D_MODEL, D_MODEL), 0, D_MODEL // N_DEV),
    "w_cq": ((D_MODEL, X_W), 0, D_MODEL // N_DEV),
    "w_ckv": ((D_MODEL, 2 * X_W), 0, D_MODEL // N_DEV),
    "w_co": ((X_W, D_MODEL), 1, D_MODEL // N_DEV),
    "w_gate_up": ((D_MODEL, 2 * FF_PAD), 1, FF_TILE),
    "w_down": ((FF_PAD, D_MODEL), 0, DOWN_SHARD),
}


def _shard_shape(name):
    shape, axis, width = FULL[name]
    return tuple(width if a == axis else n for a, n in enumerate(shape))


def _window(ref, name, dev):
    _, axis, width = FULL[name]
    if name == "w_down":
        start = pl.multiple_of((dev // 2) * FF_TILE + (dev % 2) * DOWN_SHARD, HALO)
    else:
        start = pl.multiple_of(dev * width, BLOCK)
    return ref.at[pl.ds(start, width), :] if axis == 0 else ref.at[:, pl.ds(start, width)]


def _mesh_place():
    x, y, c = lax.axis_index("x"), lax.axis_index("y"), lax.axis_index("c")
    return x, y, c, 4 * x + 2 * y + c


def _peer(x, y, c, k):
    px = 1 - x if k & 4 else x
    py = 1 - y if k & 2 else y
    pc = 1 - c if k & 1 else c
    return (px, py, pc), 4 * px + 2 * py + pc


HBM_SPEC = pl.BlockSpec(memory_space=pltpu.HBM)


def _comm_sems(n):
    return [pltpu.SemaphoreType.DMA((n, N_DEV - 1)), pltpu.SemaphoreType.DMA((n, N_DEV - 1)),
            pltpu.SemaphoreType.DMA((n,))]


DIRECT = (1, 2, 4, 6)
RELAYED = (2, 4, 6)


def _relay_step(steps):
    return max(1, (steps * 11) // 16) if steps > 1 else 0


def _gather_copies(names, ins, outs, send_sems, recv_sems, local_sems):
    x, y, c, me = _mesh_place()
    sibling, _ = _peer(x, y, c, 1)
    local, direct, relays, recv = [], [], {}, {}
    for wi, name in enumerate(names):
        def sems(k, to):
            return dict(send_sem=send_sems.at[wi, k - 1], recv_sem=recv_sems.at[wi, k - 1], device_id=to,
                        device_id_type=MESH)
        local.append(pltpu.make_async_copy(ins[wi], _window(outs[wi], name, me), local_sems.at[wi]))
        for k in range(1, N_DEV):
            peer, pidx = _peer(x, y, c, k)
            win = _window(outs[wi], name, pidx)
            recv[wi, k] = pltpu.make_async_remote_copy(src_ref=ins[wi], dst_ref=win, **sems(k, peer))
            if k in DIRECT:
                direct.append(pltpu.make_async_remote_copy(src_ref=ins[wi], dst_ref=_window(outs[wi], name, me),
                                                           **sems(k, peer)))
            if k in RELAYED:
                relays[wi, k] = pltpu.make_async_remote_copy(src_ref=win, dst_ref=win, **sems(k + 1, sibling))
    return local, direct, relays, recv


def _exchange_copies(names, ins, outs, send_sems, recv_sems, local_sems):
    x, y, c, me = _mesh_place()
    local, sent, recv = [], [], []
    for wi, name in enumerate(names):
        local.append(pltpu.make_async_copy(_window(ins[wi], name, me), outs[wi].at[0], local_sems.at[wi]))
        for k in range(1, N_DEV):
            peer, pidx = _peer(x, y, c, k)
            sems = dict(send_sem=send_sems.at[wi, k - 1], recv_sem=recv_sems.at[wi, k - 1], device_id=peer,
                        device_id_type=MESH)
            sent.append(pltpu.make_async_remote_copy(src_ref=_window(ins[wi], name, pidx), dst_ref=outs[wi].at[k], **sems))
            recv.append(pltpu.make_async_remote_copy(src_ref=_window(ins[wi], name, me), dst_ref=outs[wi].at[k], **sems))
    return local, sent, recv


def _down_pads(down_ref, zero_ref, zero_sems):
    return [pltpu.make_async_copy(zero_ref, down_ref.at[pl.ds(t * FF_TILE + FF_SHARD, FF_TILE - FF_SHARD), :],
                                  zero_sems.at[t]) for t in range(FF_PAD // FF_TILE)]


def _gather_scratch(names):
    if not names:
        return []
    pad = [pltpu.VMEM((FF_TILE - FF_SHARD, D_MODEL), BF16), pltpu.SemaphoreType.DMA((FF_PAD // FF_TILE,))]
    return _comm_sems(len(names)) + (pad if "w_down" in names else [])


def _gather_ops(names, ins, outs, scratch):
    local, direct, relays, recv = _gather_copies(names, ins, outs, *scratch[:3])
    pads = _down_pads(outs[names.index("w_down")], scratch[3], scratch[4]) if "w_down" in names else []

    def start():
        if pads:
            scratch[3][...] = jnp.zeros_like(scratch[3])
        for cp in local + direct + pads:
            cp.start()

    def relay():
        for (wi, k), cp in relays.items():
            recv[wi, k].wait_recv()
            cp.start()

    def finish():
        for (wi, k), cp in recv.items():
            if k not in RELAYED:
                cp.wait_recv()
        for cp in direct + list(relays.values()):
            cp.wait_send()
        for cp in local + pads:
            cp.wait()

    return start, relay, finish


def _start(copies):
    local, sent, _ = copies
    for cp in local + sent:
        cp.start()


def _finish(copies):
    local, sent, recv = copies
    for cp in recv:
        cp.wait_recv()
    for cp in sent:
        cp.wait_send()
    for cp in local:
        cp.wait()


def _gather_weights(shards):
    names = tuple(shards)
    nw = len(names)

    def body(*refs):
        start, relay, finish = _gather_ops(names, refs[:nw], refs[nw:2 * nw], refs[2 * nw:])
        start()
        relay()
        finish()

    outs = _pcall(
        body, name="gather_weights",
        in_specs=[HBM_SPEC] * nw, out_specs=[HBM_SPEC] * nw,
        out_shape=[SDS(FULL[n][0], BF16) for n in names],
        scratch_shapes=_comm_sems(nw))(*[shards[n] for n in names])
    return dict(zip(names, outs))


def _exchange_small(blocks):
    nb = len(blocks)

    def body(*refs):
        ins, outs = refs[:nb], refs[nb:2 * nb]
        send_sems, recv_sems, local_sems = refs[2 * nb:]
        x, y, c, me = _mesh_place()
        local, sent, recv = [], [], []
        for bi in range(nb):
            local.append(pltpu.make_async_copy(ins[bi], outs[bi].at[me], local_sems.at[bi]))
            for k in range(1, N_DEV):
                peer, pidx = _peer(x, y, c, k)
                sems = dict(send_sem=send_sems.at[bi, k - 1], recv_sem=recv_sems.at[bi, k - 1], device_id=peer,
                            device_id_type=MESH)
                sent.append(pltpu.make_async_remote_copy(src_ref=ins[bi], dst_ref=outs[bi].at[me], **sems))
                recv.append(pltpu.make_async_remote_copy(src_ref=ins[bi], dst_ref=outs[bi].at[pidx], **sems))
        _start((local, sent, recv))
        _finish((local, sent, recv))

    return _pcall(
        body, name="exchange_small", in_specs=[HBM_SPEC] * nb, out_specs=[HBM_SPEC] * nb,
        out_shape=[SDS((N_DEV,) + a.shape, F32) for a in blocks], scratch_shapes=_comm_sems(nb))(*blocks)


def _adam_math(g, w, m, v):
    m_new = ADAM_B1 * m + (1.0 - ADAM_B1) * g
    v_new = ADAM_B2 * v + (1.0 - ADAM_B2) * (g * g)
    m_hat = m_new / (1.0 - ADAM_B1 ** ADAM_STEP)
    v_hat = v_new / (1.0 - ADAM_B2 ** ADAM_STEP)
    return -ADAM_LR * (m_hat / (jnp.sqrt(v_hat) + ADAM_EPS) + ADAM_WD * w), m_new, v_new


def _adamw_small(parts, w, m, v, sq_parts, *, name):
    n = len(parts)

    def body(*refs):
        p_refs, w_refs, m_refs, v_refs = refs[:n], refs[n:2 * n], refs[2 * n:3 * n], refs[3 * n:4 * n]
        sq_ref, outs = refs[4 * n], refs[4 * n + 1:]
        for i in range(n):
            g = p_refs[i][0]
            for k in range(1, N_DEV):
                g = g + p_refs[i][k]
            delta, m_new, v_new = _adam_math(g, w_refs[i][...], m_refs[i][...], v_refs[i][...])
            outs[4 * i][...] = g
            outs[4 * i + 1][...] = delta
            outs[4 * i + 2][...] = m_new
            outs[4 * i + 3][...] = v_new
        tot = sq_ref[0]
        for k in range(1, N_DEV):
            tot = tot + sq_ref[k]
        outs[4 * n][...] = (0.5 / D_MODEL) * jnp.sum(tot, axis=1, keepdims=True)

    out_shape = [SDS(a.shape, F32) for a in w for _ in range(4)] + [SDS((1, 1), F32)]
    outs = _pcall(body, name=name, out_shape=out_shape, compiler_params=_cp())(*parts, *w, *m, *v, sq_parts)
    return [outs[4 * i:4 * i + 4] for i in range(n)], outs[4 * n][0, 0]


def _adamw(parts, w, m, v, *, name):
    r, c = w.shape
    tr = r
    for cand in (256, 128, 88):
        if r % cand == 0:
            tr = cand
            break

    def body(p_ref, w_ref, m_ref, v_ref, g_ref, d_ref, mo_ref, vo_ref):
        g = p_ref[0].astype(F32)
        for k in range(1, N_DEV):
            g = g + p_ref[k].astype(F32)
        g_ref[...] = g
        d_ref[...], mo_ref[...], vo_ref[...] = _adam_math(g, w_ref[...], m_ref[...], v_ref[...])

    row = pl.BlockSpec((tr, c), lambda i: (i, 0))
    return _pcall(
        body, name=name, grid=(r // tr,),
        in_specs=[pl.BlockSpec((N_DEV, tr, c), lambda i: (0, i, 0)), row, row, row],
        out_specs=[row] * 4, out_shape=[SDS((r, c), F32)] * 4, compiler_params=_cp())(parts, w, m, v)


def _pad_cols(a, width):
    return jnp.pad(a, ((0, 0), (0, width - a.shape[1])))


def kernel(x, mem, positions, mix_norm_g, w_in, q_norm_g, k_norm_g, pool_w, pool_scale, w_out, cross_norm_g, mem_norm_g, w_cq, w_ckv, cq_norm_g, ck_norm_g, w_co, ffn_norm_g, w_gate_up, w_down, loss_target, m_mix_norm_g, m_w_in, m_q_norm_g, m_k_norm_g, m_pool_w, m_pool_scale, m_w_out, m_cross_norm_g, m_mem_norm_g, m_w_cq, m_w_ckv, m_cq_norm_g, m_ck_norm_g, m_w_co, m_ffn_norm_g, m_w_gate_up, m_w_down, v_mix_norm_g, v_w_in, v_q_norm_g, v_k_norm_g, v_pool_w, v_pool_scale, v_w_out, v_cross_norm_g, v_mem_norm_g, v_w_cq, v_w_ckv, v_cq_norm_g, v_ck_norm_g, v_w_co, v_ffn_norm_g, v_w_gate_up, v_w_down):
    given = dict(locals())
    w_f32 = {n: given[n][0] for n in SHARDED + SMALL}
    m_f32 = {n: given["m_" + n][0] for n in SHARDED + SMALL}
    v_f32 = {n: given["v_" + n][0] for n in SHARDED + SMALL}
    for d in (w_f32, m_f32, v_f32):
        d["w_gate_up"] = _pad_cols(d["w_gate_up"], FF_TILE)

    shards = {n: w_f32[n].astype(BF16) for n in SHARDED}
    w_in_full = _gather_weights({"w_in": shards.pop("w_in")})["w_in"]
    sm_rows = {n: (w_f32[n] if w_f32[n].ndim == 3 else w_f32[n].reshape(1, -1)) for n in SMALL}
    sq, dx, parts, gs = _local_step(x[0], mem[0], positions[0], loss_target[0], w_in_full, shards, sm_rows)

    flat = lambda a: a.reshape(-1, a.shape[-1])
    got = _exchange_small([flat(gs[n]) for n in SMALL] + [sq])
    small_res, loss = _adamw_small(got[:-1], [flat(sm_rows[n]) for n in SMALL],
                                   [flat(m_f32[n].reshape(sm_rows[n].shape)) for n in SMALL],
                                   [flat(v_f32[n].reshape(sm_rows[n].shape)) for n in SMALL], got[-1],
                                   name="adamw_small")
    res = {n: [a.reshape(w_f32[n].shape) for a in small_res[i]] for i, n in enumerate(SMALL)}
    for n in SHARDED:
        res[n] = _adamw(parts[n], w_f32[n], m_f32[n], v_f32[n], name="adamw_" + n)
    res["w_gate_up"] = [a[:, :FF_SHARD] for a in res["w_gate_up"]]
    order = ("mix_norm_g", "w_in", "q_norm_g", "k_norm_g", "pool_w", "pool_scale", "w_out", "cross_norm_g",
             "mem_norm_g", "w_cq", "w_ckv", "cq_norm_g", "ck_norm_g", "w_co", "ffn_norm_g", "w_gate_up", "w_down")
    outs = [loss, dx[None]]
    for which in range(4):
        outs += [res[n][which][None] for n in order]
    return tuple(outs)
```

```python
import functools

import jax
import jax.numpy as jnp
from jax import lax
from jax.experimental import pallas as pl
from jax.experimental.pallas import tpu as pltpu

F32 = jnp.float32
BF16 = jnp.bfloat16
SDS = jax.ShapeDtypeStruct

D_MODEL = 1024
HEAD_DIM = 128
N_GROUPS = 3
DILATIONS = (1, 4, 16)
ATT_HEADS = 4
Q_W = 1536
KV_W = 512
POOL_W = 512
POOL_WINDOWS = (2, 4, 8, 16)
IN_W = 3072
X_W = 512
N_MEM = 256
D_FF = 2816
FF_TILE = 768
FF_SHARD = 704
FF_PAD = 4 * FF_TILE
DOWN_SHARD = 352
ROT_DIM = 32
ROT_HALF = 16
ROPE_THETA = 500000.0
EPS = 1e-6
NEG_INF = -1e30
SCALE = HEAD_DIM ** -0.5
BLOCK = 128
HALO = 16

ADAM_LR = 0.001
ADAM_B1 = 0.9
ADAM_B2 = 0.999
ADAM_EPS = 1e-08
ADAM_WD = 0.01
ADAM_STEP = 10

N_DEV = 8
VMEM_LIMIT_BYTES = 56 * 1024 * 1024
MESH = pl.DeviceIdType.MESH


def _pcall(body, **kw):
    return pl.pallas_call(body, **kw)


def _cp():
    return pltpu.CompilerParams(vmem_limit_bytes=VMEM_LIMIT_BYTES)


def _dot(a, b):
    return lax.dot_general(a, b, (((1,), (0,)), ((), ())), preferred_element_type=F32)


def _dot_nt(a, b):
    return lax.dot_general(a, b, (((1,), (1,)), ((), ())), preferred_element_type=F32)


def _dot_tn(a, b):
    return lax.dot_general(a, b, (((0,), (0,)), ((), ())), preferred_element_type=F32)


def _rows(s):
    return min(512, s)


def _rms_r(x):
    return lax.rsqrt(jnp.mean(x * x, axis=-1, keepdims=True) + EPS)


def _norm_bwd(x, r, gain, dxn):
    z = dxn * gain
    dx = r * z - x * (r * r * r * jnp.mean(z * x, axis=-1, keepdims=True))
    dgain = jnp.sum(dxn * x * r, axis=0, keepdims=True)
    return dx, dgain


def _split_bf16(t):
    hi = t.astype(BF16)
    return hi, (t - hi.astype(F32)).astype(BF16)


def _lane_sums(t, ones):
    hi, lo = _split_bf16(t)
    return _dot(hi, ones) + _dot(lo, ones)


def _head_r(y, ones):
    return lax.rsqrt(_lane_sums(y * y, ones) * (1.0 / HEAD_DIM) + EPS)


def _head_norm_bwd(y, r, gain, dn, ones):
    z = dn * gain
    dy = r * z - y * (r * r * r * (_dot((z * y).astype(BF16), ones) * (1.0 / HEAD_DIM)))
    return dy, jnp.sum(dn * y * r, axis=0, keepdims=True)


def _swap_matrix():
    src = lax.broadcasted_iota(jnp.int32, (HEAD_DIM, HEAD_DIM), 0)
    dst = lax.broadcasted_iota(jnp.int32, (HEAD_DIM, HEAD_DIM), 1)
    hit = ((dst < ROT_HALF) & (src == dst + ROT_HALF)) | ((dst >= ROT_HALF) & (dst < ROT_DIM) & (src == dst - ROT_HALF))
    return jnp.where(hit, 1.0, 0.0).astype(BF16)


def _partner(t, swap):
    hi, lo = _split_bf16(t)
    return _dot(hi, swap) + _dot(lo, swap)


def _rope(n, cos_t, sin_t, swap):
    return n * cos_t + _partner(n, swap) * sin_t


def _rope_bwd(d, cos_t, sin_t, swap):
    return d * cos_t + _dot((d * sin_t).astype(BF16), swap)


def _resident(shape):
    return pl.BlockSpec(shape, lambda i: (0,) * len(shape), pipeline_mode=pl.Buffered(1))


def _chunks(n, cn):
    return [slice(j * cn, (j + 1) * cn) for j in range(n // cn)]


def _norm_linear(x, gain, w, *, cn, out_dtype, name, shards=None):
    s, k = x.shape
    n = w.shape[1]
    tm = _rows(s)
    names = tuple(shards or ())
    nr = len(names)

    def body(*refs):
        x_ref, g_ref, w_ref = refs[:3]
        y_ref, xn_ref = refs[3 + nr:5 + nr]
        if nr:
            start, relay, finish = _gather_ops(names, refs[3:3 + nr], refs[5 + nr:5 + 2 * nr], refs[5 + 2 * nr:])
            pl.when(pl.program_id(0) == 0)(start)
            pl.when(pl.program_id(0) == _relay_step(s // tm))(relay)
        xv = x_ref[...]
        xn_ref[...] = (xv * _rms_r(xv) * g_ref[...]).astype(BF16)
        for c in _chunks(n, cn):
            y_ref[:, c] = _dot(xn_ref[...], w_ref[:, c]).astype(out_dtype)
        if nr:
            pl.when(pl.program_id(0) == s // tm - 1)(finish)

    row = lambda w_: pl.BlockSpec((tm, w_), lambda i: (i, 0))
    outs = _pcall(
        body, name=name, grid=(s // tm,),
        in_specs=[row(k), _resident((1, k)), _resident((k, n))] + [HBM_SPEC] * nr,
        out_specs=[row(n), row(k)] + [HBM_SPEC] * nr,
        out_shape=[SDS((s, n), out_dtype), SDS((s, k), BF16)] + [SDS(FULL[a][0], BF16) for a in names],
        scratch_shapes=_gather_scratch(names),
        compiler_params=_cp())(x, gain, w, *[shards[a] for a in names])
    return (outs[0], outs[1], dict(zip(names, outs[2:]))) if nr else tuple(outs)


def _norm_linear_swiglu(x, gain, wgu, *, name):
    s, k = x.shape
    tm = _rows(s)

    def body(x_ref, g_ref, w_ref, gu_ref, a_ref, xn_ref):
        xv = x_ref[...]
        xn_ref[...] = (xv * _rms_r(xv) * g_ref[...]).astype(BF16)
        for c in _chunks(FF_PAD, FF_TILE):
            g = _dot(xn_ref[...], w_ref[:, c])
            u = _dot(xn_ref[...], w_ref[:, slice(FF_PAD + c.start, FF_PAD + c.stop)])
            a_ref[:, c] = (g * jax.nn.sigmoid(g) * u).astype(BF16)
            gu_ref[0, :, c] = g.astype(BF16)
            gu_ref[1, :, c] = u.astype(BF16)

    row = lambda w_: pl.BlockSpec((tm, w_), lambda i: (i, 0))
    return _pcall(
        body, name=name, grid=(s // tm,),
        in_specs=[row(k), _resident((1, k)), _resident((k, 2 * FF_PAD))],
        out_specs=[pl.BlockSpec((2, tm, FF_PAD), lambda i: (0, i, 0)), row(FF_PAD), row(k)],
        out_shape=[SDS((2, s, FF_PAD), BF16), SDS((s, FF_PAD), BF16), SDS((s, k), BF16)],
        compiler_params=_cp())(x, gain, wgu)


def _linear_res(a, w, res, *, cn, name):
    s, k = a.shape
    n = w.shape[1]
    tm = _rows(s)

    def body(a_ref, w_ref, r_ref, y_ref):
        for c in _chunks(n, cn):
            y_ref[:, c] = r_ref[:, c] + _dot(a_ref[...], w_ref[:, c])

    row = lambda w_: pl.BlockSpec((tm, w_), lambda i: (i, 0))
    return _pcall(
        body, name=name, grid=(s // tm,),
        in_specs=[row(k), _resident((k, n)), row(n)], out_specs=row(n),
        out_shape=SDS((s, n), F32), compiler_params=_cp())(a, w, res)


def _linear_res_loss(a, w, res, tgt, *, name):
    s, k = a.shape
    n = w.shape[1]
    tm = _rows(s)

    def body(a_ref, w_ref, r_ref, t_ref, dy_ref, dyb_ref, sq_ref):
        e = r_ref[...] + _dot(a_ref[...], w_ref[...]) - t_ref[...]
        dy = e * (1.0 / n)
        dy_ref[...] = dy
        dyb_ref[...] = dy.astype(BF16)

        @pl.when(pl.program_id(0) == 0)
        def _():
            sq_ref[...] = jnp.zeros_like(sq_ref)
        sq_ref[...] += jnp.sum(e * e, axis=0, keepdims=True)

    row = lambda w_: pl.BlockSpec((tm, w_), lambda i: (i, 0))
    return _pcall(
        body, name=name, grid=(s // tm,),
        in_specs=[row(k), _resident((k, n)), row(n), row(n)],
        out_specs=[row(n), row(n), pl.BlockSpec((1, n), lambda i: (0, 0))],
        out_shape=[SDS((s, n), F32), SDS((s, n), BF16), SDS((1, n), F32)],
        compiler_params=_cp())(a, w, res, tgt)


def _linear_nt(g, w, *, cn, name):
    s, k = g.shape
    n = w.shape[0]
    tm = _rows(s)

    def body(g_ref, w_ref, y_ref):
        for c in _chunks(n, cn):
            y_ref[:, c] = _dot_nt(g_ref[...], w_ref[c, :]).astype(BF16)

    row = lambda w_: pl.BlockSpec((tm, w_), lambda i: (i, 0))
    return _pcall(
        body, name=name, grid=(s // tm,),
        in_specs=[row(k), _resident((n, k))], out_specs=row(n),
        out_shape=SDS((s, n), BF16), compiler_params=_cp())(g, w)


def _swiglu_bwd(dyb, wd, gu, *, name):
    s, n = dyb.shape
    tm = _rows(s)

    def body(dy_ref, wd_ref, gu_ref, dgu_ref):
        for c in _chunks(FF_PAD, FF_TILE):
            da = _dot_nt(dy_ref[...], wd_ref[c, :])
            g = gu_ref[0, :, c].astype(F32)
            u = gu_ref[1, :, c].astype(F32)
            sg = jax.nn.sigmoid(g)
            dgu_ref[0, :, c] = (da * u * (sg * (1.0 + g * (1.0 - sg)))).astype(BF16)
            dgu_ref[1, :, c] = (da * (g * sg)).astype(BF16)

    half = pl.BlockSpec((2, tm, FF_PAD), lambda i: (0, i, 0))
    return _pcall(
        body, name=name, grid=(s // tm,),
        in_specs=[pl.BlockSpec((tm, n), lambda i: (i, 0)), _resident((FF_PAD, n)), half],
        out_specs=half, out_shape=SDS((2, s, FF_PAD), BF16), compiler_params=_cp())(dyb, wd, gu)


def _linear_nt_normbwd(g, w, x, dres, gain, *, name, grads=None):
    d, k = w.shape
    s = x.shape[0]
    tm = _rows(s)
    names = tuple(grads or ())
    nr = len(names)

    def body(*refs):
        g_ref, w_ref, x_ref, dr_ref, gn_ref = refs[:5]
        dx_ref, dxb_ref, dg_ref = refs[5 + nr:8 + nr]
        if nr:
            copies = _exchange_copies(names, refs[5:5 + nr], refs[8 + nr:8 + 2 * nr], *refs[8 + 2 * nr:])

        @pl.when(pl.program_id(0) == 0)
        def _():
            dg_ref[...] = jnp.zeros_like(dg_ref)
            if nr:
                _start(copies)

        if g.ndim == 3:
            dxn = _dot_nt(g_ref[0], w_ref[:, :k // 2]) + _dot_nt(g_ref[1], w_ref[:, k // 2:])
        else:
            dxn = _dot_nt(g_ref[...], w_ref[...])
        xv = x_ref[...]
        dx, dgain = _norm_bwd(xv, _rms_r(xv), gn_ref[...], dxn)
        out = dr_ref[...] + dx
        dx_ref[...] = out
        dxb_ref[...] = out.astype(BF16)
        dg_ref[...] += dgain

        if nr:
            @pl.when(pl.program_id(0) == s // tm - 1)
            def _():
                _finish(copies)

    row = pl.BlockSpec((tm, d), lambda i: (i, 0))
    g_spec = (pl.BlockSpec((2, tm, k // 2), lambda i: (0, i, 0)) if g.ndim == 3
              else pl.BlockSpec((tm, k), lambda i: (i, 0)))
    outs = _pcall(
        body, name=name, grid=(s // tm,),
        in_specs=[g_spec, _resident((d, k)), row, row, _resident((1, d))] + [HBM_SPEC] * nr,
        out_specs=[row, row, pl.BlockSpec((1, d), lambda i: (0, 0))] + [HBM_SPEC] * nr,
        out_shape=[SDS((s, d), F32), SDS((s, d), BF16), SDS((1, d), F32)]
        + [SDS((N_DEV,) + _shard_shape(n), BF16) for n in names],
        scratch_shapes=_comm_sems(nr) if nr else [],
        compiler_params=_cp())(g, w, x, dres, gain, *[grads[n] for n in names])
    return (outs[0], outs[1], outs[2], dict(zip(names, outs[3:]))) if nr else tuple(outs)


def _dw_tn(x, g, *, tkw, tn, name):
    s, kw = x.shape
    halves = g.ndim == 3
    n = 2 * g.shape[2] if halves else g.shape[1]
    ts = min(1024, s)
    ns = s // ts
    per_half = n // 2 // tn

    def body(x_ref, g_ref, o_ref, acc_ref):
        ss = pl.program_id(2)

        @pl.when(ss == 0)
        def _():
            acc_ref[...] = jnp.zeros_like(acc_ref)

        acc_ref[...] += _dot_tn(x_ref[...], g_ref[...])

        @pl.when(ss == ns - 1)
        def _():
            o_ref[...] = acc_ref[...].astype(BF16)

    g_spec = (pl.BlockSpec((None, ts, tn), lambda a, b, ss: (b // per_half, ss, b % per_half)) if halves
              else pl.BlockSpec((ts, tn), lambda a, b, ss: (ss, b)))
    return _pcall(
        body, name=name, grid=(kw // tkw, n // tn, ns),
        in_specs=[pl.BlockSpec((ts, tkw), lambda a, b, ss: (ss, a)), g_spec],
        out_specs=pl.BlockSpec((tkw, tn), lambda a, b, ss: (a, b)),
        out_shape=SDS((kw, n), BF16),
        scratch_shapes=[pltpu.VMEM((tkw, tn), F32)], compiler_params=_cp())(x, g)


def _qk_prep(proj, cos_t, sin_t, qg, kg, shards, *, name):
    s = proj.shape[0]
    tm = _rows(s)
    nqh = Q_W // HEAD_DIM
    names = tuple(shards)
    nr = len(names)

    def body(*refs):
        q_ref, k_ref, v_ref, c_ref, s_ref, qg_ref, kg_ref = refs[:7]
        qo_ref, ko_ref, vo_ref = refs[7 + nr:10 + nr]
        start, relay, finish = _gather_ops(names, refs[7:7 + nr], refs[10 + nr:10 + 2 * nr], refs[10 + 2 * nr:])
        pl.when(pl.program_id(0) == 0)(start)
        pl.when(pl.program_id(0) == _relay_step(s // tm))(relay)
        c, sn = c_ref[...], s_ref[...]
        ones, swap = jnp.ones((HEAD_DIM, HEAD_DIM), BF16), _swap_matrix()
        for h in range(nqh):
            sl = slice(h * HEAD_DIM, (h + 1) * HEAD_DIM)
            y = q_ref[:, sl]
            qo_ref[:, sl] = (_rope(y * _head_r(y, ones) * qg_ref[...], c, sn, swap) * SCALE).astype(BF16)
        for h in range(ATT_HEADS):
            sl = slice(h * HEAD_DIM, (h + 1) * HEAD_DIM)
            y = k_ref[:, sl]
            ko_ref[:, sl] = _rope(y * _head_r(y, ones) * kg_ref[...], c, sn, swap).astype(BF16)
        vo_ref[...] = v_ref[...].astype(BF16)
        pl.when(pl.program_id(0) == s // tm - 1)(finish)

    row = lambda w, j: pl.BlockSpec((tm, w), lambda i: (i, j))
    one = pl.BlockSpec((1, HEAD_DIM), lambda i: (0, 0))
    outs = _pcall(
        body, name=name, grid=(s // tm,),
        in_specs=[row(Q_W, 0), row(KV_W, 3), row(KV_W, 4), row(HEAD_DIM, 0), row(HEAD_DIM, 0), one, one]
        + [HBM_SPEC] * nr,
        out_specs=[row(Q_W, 0), row(KV_W, 0), row(KV_W, 0)] + [HBM_SPEC] * nr,
        out_shape=[SDS((s, Q_W), BF16), SDS((s, KV_W), BF16), SDS((s, KV_W), BF16)]
        + [SDS(FULL[a][0], BF16) for a in names],
        scratch_shapes=_gather_scratch(names),
        compiler_params=_cp())(proj, proj, proj, cos_t, sin_t, qg, kg, *[shards[a] for a in names])
    return outs[0], outs[1], outs[2], dict(zip(names, outs[3:]))


def _qk_prep_bwd(proj, dq, dk, dv, du, cos_t, sin_t, qg, kg, *, name):
    s = proj.shape[0]
    tm = _rows(s)

    def body(q_ref, k_ref, dq_ref, dk_ref, dv_ref, du_ref, c_ref, s_ref, qg_ref, kg_ref, dp_ref, dqg_ref, dkg_ref):
        c, sn = c_ref[...], s_ref[...]

        @pl.when(pl.program_id(0) == 0)
        def _():
            dqg_ref[...] = jnp.zeros_like(dqg_ref)
            dkg_ref[...] = jnp.zeros_like(dkg_ref)

        ones, swap = jnp.ones((HEAD_DIM, HEAD_DIM), BF16), _swap_matrix()
        dqg = jnp.zeros((1, HEAD_DIM), F32)
        for g in range(N_GROUPS):
            for h in range(ATT_HEADS):
                sl = slice(h * HEAD_DIM, (h + 1) * HEAD_DIM)
                col = slice(g * KV_W + h * HEAD_DIM, g * KV_W + (h + 1) * HEAD_DIM)
                y = q_ref[:, col]
                dn = _rope_bwd(dq_ref[g, :, sl].astype(F32) * SCALE, c, sn, swap)
                dy, dgain = _head_norm_bwd(y, _head_r(y, ones), qg_ref[...], dn, ones)
                dp_ref[:, col] = dy.astype(BF16)
                dqg = dqg + dgain
        dqg_ref[...] += dqg

        dkg = jnp.zeros((1, HEAD_DIM), F32)
        for h in range(ATT_HEADS):
            sl = slice(h * HEAD_DIM, (h + 1) * HEAD_DIM)
            y = k_ref[:, sl]
            dn = _rope_bwd(dk_ref[:, sl], c, sn, swap)
            dy, dgain = _head_norm_bwd(y, _head_r(y, ones), kg_ref[...], dn, ones)
            dp_ref[:, Q_W + h * HEAD_DIM:Q_W + (h + 1) * HEAD_DIM] = dy.astype(BF16)
            dkg = dkg + dgain
        dkg_ref[...] += dkg

        dp_ref[:, Q_W + KV_W:Q_W + 2 * KV_W] = dv_ref[...].astype(BF16)
        dp_ref[:, Q_W + 2 * KV_W:] = du_ref[...].astype(BF16)

    row = lambda w, j: pl.BlockSpec((tm, w), lambda i: (i, j))
    one = pl.BlockSpec((1, HEAD_DIM), lambda i: (0, 0))
    return _pcall(
        body, name=name, grid=(s // tm,),
        in_specs=[row(Q_W, 0), row(KV_W, 3), pl.BlockSpec((N_GROUPS, tm, KV_W), lambda i: (0, i, 0))]
        + [row(KV_W, 0)] * 3 + [row(HEAD_DIM, 0), row(HEAD_DIM, 0), one, one],
        out_specs=[row(IN_W, 0), one, one],
        out_shape=[SDS((s, IN_W), BF16), SDS((1, HEAD_DIM), F32), SDS((1, HEAD_DIM), F32)],
        compiler_params=_cp())(proj, proj, dq, dk, dv, du, cos_t, sin_t, qg, kg)


ATT_ROWS = 16 * BLOCK


def _sub(ref, start, d, size=BLOCK):
    return ref[pl.ds(start, size, stride=d), :] if d > 1 else ref[pl.ds(start, size), :]


def _sub_set(ref, start, d, val):
    if d > 1:
        ref[pl.ds(start, BLOCK, stride=d), :] = val
    else:
        ref[pl.ds(start, BLOCK), :] = val


def _band_masks():
    row = lax.broadcasted_iota(jnp.int32, (BLOCK, BLOCK), 0)
    col = lax.broadcasted_iota(jnp.int32, (BLOCK, BLOCK), 1)
    return col <= row, col >= row


def _eye(n=BLOCK):
    row = lax.broadcasted_iota(jnp.int32, (n, n), 0)
    col = lax.broadcasted_iota(jnp.int32, (n, n), 1)
    return jnp.where(row == col, 1.0, 0.0).astype(BF16)


def _attn_fwd(q_rot, k_rot, v, shards, *, name):
    s = q_rot.shape[0]
    rr = ATT_ROWS
    nblk = s // rr
    names = tuple(shards)
    nr = len(names)

    def body(*refs):
        q0, q1, q2, kp, kc, vp, vc = refs[:7]
        mix_ref, lse_ref = refs[7 + nr:9 + nr]
        qs, ks, vs, os_, ls = refs[9 + 2 * nr:14 + 2 * nr]
        h, n = pl.program_id(0), pl.program_id(1)
        start, relay, finish = _gather_ops(names, refs[7:7 + nr], refs[9 + nr:9 + 2 * nr], refs[14 + 2 * nr:])
        pl.when((h == 0) & (n == 0))(start)
        pl.when(h * nblk + n == _relay_step(ATT_HEADS * nblk))(relay)
        for g, q_ref in enumerate((q0, q1, q2)):
            qs[g] = q_ref[...].astype(F32)
        ks[:rr] = kp[...].astype(F32)
        ks[rr:] = kc[...].astype(F32)
        vs[:rr] = vp[...].astype(F32)
        vs[rr:] = vc[...].astype(F32)
        m_cur, m_band = _band_masks()
        mask_in = jnp.concatenate([m_band, m_cur], axis=1)
        mask_first = jnp.concatenate([m_band & (n > 0), m_cur], axis=1)
        ones = jnp.ones((2 * BLOCK, HEAD_DIM), BF16)
        pieces = [(g, d, j * BLOCK * d + r, j) for g, d in enumerate(DILATIONS) for r in range(d)
                  for j in range(rr // (BLOCK * d))]

        def scores(piece):
            g, d, base, j = piece
            q = _sub(qs.at[g], base, d).astype(BF16)
            k2 = _sub(ks, rr + base - BLOCK * d, d, 2 * BLOCK).astype(BF16)
            return jnp.where(mask_first if j == 0 else mask_in, _dot_nt(q, k2), NEG_INF)

        sc = scores(pieces[0])
        for i, (g, d, base, j) in enumerate(pieces):
            cur = sc
            if i + 1 < len(pieces):
                sc = scores(pieces[i + 1])
            m = jnp.max(cur, axis=-1, keepdims=True)
            p = jnp.exp(cur - m).astype(BF16)
            v2 = _sub(vs, rr + base - BLOCK * d, d, 2 * BLOCK).astype(BF16)
            acc_l = _dot(p, jnp.concatenate([v2, ones], axis=1))
            l = acc_l[:, HEAD_DIM:]
            _sub_set(os_.at[g], base, d, acc_l[:, :HEAD_DIM] / l)
            _sub_set(ls.at[g], base, d, m + jnp.log(l))
        for c in _chunks(rr, 2 * BLOCK):
            a, b, cc = ls[0, c, :], ls[1, c, :], ls[2, c, :]
            m = jnp.maximum(jnp.maximum(a, b), cc)
            wa, wb, wc = jnp.exp(a - m), jnp.exp(b - m), jnp.exp(cc - m)
            den = wa + wb + wc
            mix_ref[c, :] = ((wa * os_[0, c, :] + wb * os_[1, c, :] + wc * os_[2, c, :]) / den).astype(BF16)
            lse_ref[c, :] = m + jnp.log(den)

        pl.when((h == ATT_HEADS - 1) & (n == nblk - 1))(finish)

    blk = lambda f: pl.BlockSpec((rr, HEAD_DIM), f)
    prv = lambda n: jnp.maximum(n - 1, 0)
    outs = _pcall(
        body, name=name, grid=(ATT_HEADS, nblk),
        in_specs=[blk(lambda h, n, g=g: (n, g * ATT_HEADS + h)) for g in range(N_GROUPS)]
        + [blk(lambda h, n: (prv(n), h)), blk(lambda h, n: (n, h))] * 2 + [HBM_SPEC] * nr,
        out_specs=[blk(lambda h, n: (n, h)), blk(lambda h, n: (n, h))] + [HBM_SPEC] * nr,
        out_shape=[SDS((s, KV_W + POOL_W), BF16), SDS((s, KV_W), F32)] + [SDS(FULL[w][0], BF16) for w in names],
        scratch_shapes=[pltpu.VMEM((N_GROUPS, rr, HEAD_DIM), F32), pltpu.VMEM((2 * rr, HEAD_DIM), F32),
                        pltpu.VMEM((2 * rr, HEAD_DIM), F32), pltpu.VMEM((N_GROUPS, rr, HEAD_DIM), F32),
                        pltpu.VMEM((N_GROUPS, rr, HEAD_DIM), F32)] + _gather_scratch(names),
        compiler_params=_cp())(q_rot, q_rot, q_rot, k_rot, k_rot, v, v, *[shards[w] for w in names])
    return outs[0], outs[1], dict(zip(names, outs[2:]))


def _attn_bwd(q_rot, k_rot, v, mix, dmix, lse, grads, *, name):
    s = q_rot.shape[0]
    rr = ATT_ROWS
    nblk = s // rr
    names = tuple(grads)
    nr = len(names)

    def body(*refs):
        q0, q1, q2, qx0, qx1, qx2, kp, kc, vp, vc, do_c, do_x, o_c, o_x, l_c, l_x = refs[:16]
        dq_ref, dk_ref, dv_ref = refs[16 + nr:19 + nr]
        qs, ks, vs, dos, lss, dls, dqs, dks, dvs, send_sems, recv_sems, local_sems = refs[19 + 2 * nr:]
        h, n = pl.program_id(0), pl.program_id(1)
        copies = _exchange_copies(names, refs[16:16 + nr], refs[19 + nr:19 + 2 * nr], send_sems, recv_sems, local_sems)

        @pl.when((h == 0) & (n == 0))
        def _():
            _start(copies)

        for g, (qc_ref, qx_ref) in enumerate(((q0, qx0), (q1, qx1), (q2, qx2))):
            qs[g, :rr] = qc_ref[...].astype(F32)
            qs[g, rr:] = qx_ref[...].astype(F32)
        ks[:rr] = kp[...].astype(F32)
        ks[rr:] = kc[...].astype(F32)
        vs[:rr] = vp[...].astype(F32)
        vs[rr:] = vc[...].astype(F32)
        lss[:rr] = l_c[...]
        lss[rr:] = l_x[...]
        for half, (d_ref, o_ref) in enumerate(((do_c, o_c), (do_x, o_x))):
            for c in _chunks(rr, 2 * BLOCK):
                cs = slice(half * rr + c.start, half * rr + c.stop)
                dof = d_ref[c, :].astype(F32)
                dos[cs, :] = dof
                dls[cs, :] = jnp.broadcast_to(jnp.sum(dof * o_ref[c, :].astype(F32), axis=-1, keepdims=True),
                                              (2 * BLOCK, HEAD_DIM))
        m_cur, m_band = _band_masks()
        mask_in = jnp.concatenate([m_cur, m_band], axis=0)
        mask_last = jnp.concatenate([m_cur, m_band & (n + 1 < nblk)], axis=0)
        m_first = m_band & (n > 0)
        eye = _eye()
        pieces = [(g, d, j * BLOCK * d + r, j, rr // (BLOCK * d)) for g, d in enumerate(DILATIONS) for r in range(d)
                  for j in range(rr // (BLOCK * d))]

        def front(piece):
            g, d, base, _, _ = piece
            q2 = _sub(qs.at[g], base, d, 2 * BLOCK).astype(BF16)
            do2 = _sub(dos, base, d, 2 * BLOCK).astype(BF16)
            k = _sub(ks, rr + base, d).astype(BF16)
            vv = _sub(vs, rr + base, d).astype(BF16)
            return q2, do2, k, _dot_nt(q2, k), _dot_nt(do2, vv)

        def middle(piece, fr, dq_acc):
            g, d, base, j, nsub = piece
            q2, do2, k, s2, dp2 = fr
            if j == 0:
                kp_ = _sub(ks, rr + base - BLOCK * d, d).astype(BF16)
                p0 = jnp.where(m_first, jnp.exp(_dot_nt(q2[:BLOCK], kp_) - _sub(lss, base, d)), 0.0)
                ds0 = p0 * (_dot_nt(do2[:BLOCK], _sub(vs, rr + base - BLOCK * d, d).astype(BF16)) - _sub(dls, base, d))
                dq_acc = _dot(ds0.astype(BF16), kp_)
            p2 = jnp.where(mask_last if j + 1 == nsub else mask_in,
                           jnp.exp(s2 - _sub(lss, base, d, 2 * BLOCK)), 0.0)
            ds2 = (p2 * (dp2 - _sub(dls, base, d, 2 * BLOCK))).astype(BF16)
            dq2 = _dot(ds2, k)
            return dq2[BLOCK:], (_dot_nt(eye, ds2), _dot_nt(eye, p2.astype(BF16)), q2, do2, dq_acc + dq2[:BLOCK])

        def back(piece, tr):
            g, d, base, _, _ = piece
            ds_t, p_t, q2, do2, dq = tr
            _sub_set(dqs.at[g], base, d, dq)
            dk, dv = _dot(ds_t.astype(BF16), q2), _dot(p_t.astype(BF16), do2)
            if g == 0:
                _sub_set(dks, base, d, dk)
                _sub_set(dvs, base, d, dv)
            else:
                _sub_set(dks, base, d, _sub(dks, base, d) + dk)
                _sub_set(dvs, base, d, _sub(dvs, base, d) + dv)

        fr, held, dq_acc = front(pieces[0]), None, None
        for i, piece in enumerate(pieces):
            cur = fr
            if i + 1 < len(pieces):
                fr = front(pieces[i + 1])
            dq_acc, now = middle(piece, cur, dq_acc)
            if held is not None:
                back(pieces[i - 1], held)
            held = now
        back(pieces[-1], held)
        for g in range(N_GROUPS):
            dq_ref[g] = dqs[g].astype(BF16)
        dk_ref[...] = dks[...]
        dv_ref[...] = dvs[...]

        @pl.when((h == ATT_HEADS - 1) & (n == nblk - 1))
        def _():
            _finish(copies)

    blk = lambda f: pl.BlockSpec((rr, HEAD_DIM), f)
    prv = lambda n: jnp.maximum(n - 1, 0)
    nxt = lambda n: jnp.minimum(n + 1, nblk - 1)
    cur_kv = blk(lambda h, n: (n, h))
    outs = _pcall(
        body, name=name, grid=(ATT_HEADS, nblk),
        in_specs=[blk(lambda h, n, g=g: (n, g * ATT_HEADS + h)) for g in range(N_GROUPS)]
        + [blk(lambda h, n, g=g: (nxt(n), g * ATT_HEADS + h)) for g in range(N_GROUPS)]
        + [blk(lambda h, n: (prv(n), h)), cur_kv] * 2
        + [cur_kv, blk(lambda h, n: (nxt(n), h))] * 3 + [HBM_SPEC] * nr,
        out_specs=[pl.BlockSpec((N_GROUPS, rr, HEAD_DIM), lambda h, n: (0, n, h)), cur_kv, cur_kv] + [HBM_SPEC] * nr,
        out_shape=[SDS((N_GROUPS, s, KV_W), BF16), SDS((s, KV_W), F32), SDS((s, KV_W), F32)]
        + [SDS((N_DEV,) + _shard_shape(w), BF16) for w in names],
        scratch_shapes=[pltpu.VMEM((N_GROUPS, 2 * rr, HEAD_DIM), F32)] + [pltpu.VMEM((2 * rr, HEAD_DIM), F32)] * 5
        + [pltpu.VMEM((N_GROUPS, rr, HEAD_DIM), F32), pltpu.VMEM((rr, HEAD_DIM), F32), pltpu.VMEM((rr, HEAD_DIM), F32)]
        + _comm_sems(nr),
        compiler_params=_cp())(q_rot, q_rot, q_rot, q_rot, q_rot, q_rot, k_rot, k_rot, v, v, dmix, dmix, mix, mix,
                               lse, lse, *[grads[w] for w in names])
    return outs[0], outs[1], outs[2], dict(zip(names, outs[3:]))


def _pool_d(u_ref, halo_ref, i, tm):
    halo = jnp.where(i > 0, halo_ref[...], 0.0)
    t = i * tm + lax.broadcasted_iota(jnp.int32, (tm, 1), 0)
    out = []
    for g, w in enumerate(POOL_WINDOWS):
        sl = slice(g * HEAD_DIM, (g + 1) * HEAD_DIM)
        u = u_ref[:, sl]
        acc = jnp.concatenate([halo[:, sl], u], axis=0)
        sh = 1
        while sh < w:
            acc = acc + pltpu.roll(acc, sh, 0)
            sh *= 2
        cnt = jnp.minimum(t + 1, w).astype(F32)
        out.append(acc[HALO:, :] / cnt - u)
    return out


def _pool_fwd(proj, mix, pool_w, pool_scale, *, name):
    s = proj.shape[0]
    tm = _rows(s)
    ucol = (IN_W - POOL_W) // POOL_W

    def body(u_ref, halo_ref, mix_in, w_ref, sc_ref, o_ref):
        del mix_in
        dd = _pool_d(u_ref, halo_ref, pl.program_id(0), tm)
        for g in range(len(POOL_WINDOWS)):
            sl = slice(g * HEAD_DIM, (g + 1) * HEAD_DIM)
            y = _dot(dd[g].astype(BF16), w_ref[g].astype(BF16))
            o_ref[:, sl] = (y * sc_ref[:, sl]).astype(BF16)

    return _pcall(
        body, name=name, grid=(s // tm,),
        in_specs=[pl.BlockSpec((tm, POOL_W), lambda i: (i, ucol)),
                  pl.BlockSpec((HALO, POOL_W), lambda i: (jnp.maximum(i * (tm // HALO) - 1, 0), ucol)),
                  pl.BlockSpec(memory_space=pl.ANY),
                  pl.BlockSpec((len(POOL_WINDOWS), HEAD_DIM, HEAD_DIM), lambda i: (0, 0, 0)),
                  pl.BlockSpec((1, POOL_W), lambda i: (0, 0))],
        out_specs=pl.BlockSpec((tm, POOL_W), lambda i: (i, 1)),
        out_shape=SDS(mix.shape, BF16), input_output_aliases={2: 0},
        compiler_params=_cp())(proj, proj, mix, pool_w, pool_scale)


def _pool_bwd(proj, dmix, pool_w, pool_scale, *, name):
    s = proj.shape[0]
    tm = _rows(s)
    nblk = s // tm
    ucol = (IN_W - POOL_W) // POOL_W
    ng = len(POOL_WINDOWS)

    def body(u_ref, halo_ref, dp_ref, dpn_ref, w_ref, sc_ref, du_ref, dw_ref, dsc_ref):
        i = pl.program_id(0)

        @pl.when(i == 0)
        def _():
            dw_ref[...] = jnp.zeros_like(dw_ref)
            dsc_ref[...] = jnp.zeros_like(dsc_ref)

        dd = _pool_d(u_ref, halo_ref, i, tm)
        t = i * tm + lax.broadcasted_iota(jnp.int32, (tm, 1), 0)
        dpn = jnp.where(i + 1 < nblk, dpn_ref[...].astype(F32), 0.0)
        for g, w in enumerate(POOL_WINDOWS):
            sl = slice(g * HEAD_DIM, (g + 1) * HEAD_DIM)
            wg = w_ref[g].astype(BF16)
            db = dd[g].astype(BF16)
            dp = dp_ref[:, sl].astype(F32)
            dsc_ref[:, sl] += jnp.sum(dp * _dot(db, wg), axis=0, keepdims=True)
            dy = (dp * sc_ref[:, sl]).astype(BF16)
            dw_ref[g] += _dot_tn(db, dy)
            g_d = _dot_nt(dy, wg)
            g_dn = _dot_nt((dpn[:, sl] * sc_ref[:, sl]).astype(BF16), wg)
            cnt = jnp.minimum(t + 1, w).astype(F32)
            acc = jnp.concatenate([g_d / cnt, g_dn * (1.0 / w)], axis=0)
            sh = 1
            while sh < w:
                acc = acc + pltpu.roll(acc, tm + HALO - sh, 0)
                sh *= 2
            du_ref[:, sl] = acc[:tm, :] - g_d

    nh = s // HALO
    return _pcall(
        body, name=name, grid=(nblk,),
        in_specs=[pl.BlockSpec((tm, POOL_W), lambda i: (i, ucol)),
                  pl.BlockSpec((HALO, POOL_W), lambda i: (jnp.maximum(i * (tm // HALO) - 1, 0), ucol)),
                  pl.BlockSpec((tm, POOL_W), lambda i: (i, 1)),
                  pl.BlockSpec((HALO, POOL_W), lambda i: (jnp.minimum((i + 1) * (tm // HALO), nh - 1), 1)),
                  pl.BlockSpec((ng, HEAD_DIM, HEAD_DIM), lambda i: (0, 0, 0)),
                  pl.BlockSpec((1, POOL_W), lambda i: (0, 0))],
        out_specs=[pl.BlockSpec((tm, POOL_W), lambda i: (i, 0)),
                   pl.BlockSpec((ng, HEAD_DIM, HEAD_DIM), lambda i: (0, 0, 0)),
                   pl.BlockSpec((1, POOL_W), lambda i: (0, 0))],
        out_shape=[SDS((s, POOL_W), F32), SDS((ng, HEAD_DIM, HEAD_DIM), F32), SDS((1, POOL_W), F32)],
        compiler_params=_cp())(proj, proj, dmix, dmix, pool_w, pool_scale)


def _mem_fwd(mem, mem_g, wckv, ck_g, *, name):
    def body(m_ref, g_ref, w_ref, kg_ref, mn_ref, ckr_ref, ckn_ref, cv_ref):
        mv = m_ref[...]
        mn = (mv * _rms_r(mv) * g_ref[...]).astype(BF16)
        mn_ref[...] = mn
        ckv = _dot(mn, w_ref[...])
        ckr_ref[...] = ckv[:, :X_W]
        cv_ref[...] = ckv[:, X_W:].astype(BF16)
        for h in range(ATT_HEADS):
            sl = slice(h * HEAD_DIM, (h + 1) * HEAD_DIM)
            y = ckv[:, sl]
            ckn_ref[:, sl] = (y * _rms_r(y) * kg_ref[...]).astype(BF16)

    return _pcall(
        body, name=name,
        out_shape=[SDS((N_MEM, D_MODEL), BF16), SDS((N_MEM, X_W), F32), SDS((N_MEM, X_W), BF16),
                   SDS((N_MEM, X_W), BF16)],
        compiler_params=_cp())(mem, mem_g, wckv, ck_g)


def _cross_q(cq_ref, g_ref, sl):
    y = cq_ref[:, sl]
    r = _rms_r(y)
    return y, r, y * r * g_ref[...] * SCALE


def _cross_fwd(cq_raw, ck_n, cv, cq_g, *, name):
    s = cq_raw.shape[0]
    tm = _rows(s)

    def body(cq_ref, k_ref, v_ref, g_ref, o_ref):
        for h in range(ATT_HEADS):
            sl = slice(h * HEAD_DIM, (h + 1) * HEAD_DIM)
            _, _, qn = _cross_q(cq_ref, g_ref, sl)
            sc = _dot_nt(qn.astype(BF16), k_ref[:, sl])
            p = jnp.exp(sc - jnp.max(sc, axis=-1, keepdims=True))
            p = p / jnp.sum(p, axis=-1, keepdims=True)
            o_ref[:, sl] = _dot(p.astype(BF16), v_ref[:, sl]).astype(BF16)

    full = lambda a: pl.BlockSpec(a.shape, lambda i: (0, 0))
    return _pcall(
        body, name=name, grid=(s // tm,),
        in_specs=[pl.BlockSpec((tm, X_W), lambda i: (i, 0)), full(ck_n), full(cv), full(cq_g)],
        out_specs=pl.BlockSpec((tm, X_W), lambda i: (i, 0)),
        out_shape=SDS((s, X_W), BF16), compiler_params=_cp())(cq_raw, ck_n, cv, cq_g)


def _cross_bwd(d_o, cq_raw, ck_n, cv, cq_g, *, name):
    s = cq_raw.shape[0]
    tm = _rows(s)

    def body(do_ref, cq_ref, k_ref, v_ref, g_ref, dcq_ref, dk_ref, dv_ref, dg_ref):
        @pl.when(pl.program_id(0) == 0)
        def _():
            dk_ref[...] = jnp.zeros_like(dk_ref)
            dv_ref[...] = jnp.zeros_like(dv_ref)
            dg_ref[...] = jnp.zeros_like(dg_ref)

        dg = jnp.zeros((1, HEAD_DIM), F32)
        for h in range(ATT_HEADS):
            sl = slice(h * HEAD_DIM, (h + 1) * HEAD_DIM)
            y, r, qn = _cross_q(cq_ref, g_ref, sl)
            qb = qn.astype(BF16)
            do = do_ref[:, sl]
            sc = _dot_nt(qb, k_ref[:, sl])
            p = jnp.exp(sc - jnp.max(sc, axis=-1, keepdims=True))
            p = p / jnp.sum(p, axis=-1, keepdims=True)
            dv_ref[:, sl] += _dot_tn(p.astype(BF16), do)
            dp = _dot_nt(do, v_ref[:, sl])
            ds = (p * (dp - jnp.sum(dp * p, axis=-1, keepdims=True))).astype(BF16)
            dk_ref[:, sl] += _dot_tn(ds, qb)
            dn = _dot(ds, k_ref[:, sl]) * SCALE
            dy, dgain = _norm_bwd(y, r, g_ref[...], dn)
            dcq_ref[:, sl] = dy.astype(BF16)
            dg = dg + dgain
        dg_ref[...] += dg

    full = lambda a: pl.BlockSpec(a.shape, lambda i: (0, 0))
    row = pl.BlockSpec((tm, X_W), lambda i: (i, 0))
    return _pcall(
        body, name=name, grid=(s // tm,),
        in_specs=[row, row, full(ck_n), full(cv), full(cq_g)],
        out_specs=[row, pl.BlockSpec((N_MEM, X_W), lambda i: (0, 0)), pl.BlockSpec((N_MEM, X_W), lambda i: (0, 0)),
                   pl.BlockSpec((1, HEAD_DIM), lambda i: (0, 0))],
        out_shape=[SDS((s, X_W), BF16), SDS((N_MEM, X_W), F32), SDS((N_MEM, X_W), F32), SDS((1, HEAD_DIM), F32)],
        compiler_params=_cp())(d_o, cq_raw, ck_n, cv, cq_g)


def _mem_bwd(dck_n, dcv, ck_raw, memn, mem, wckv, mem_g, ck_g, *, name):
    def body(dk_ref, dv_ref, ckr_ref, mn_ref, m_ref, w_ref, mg_ref, kg_ref, dw_ref, dmg_ref, dkg_ref, dckv_s):
        dkg = jnp.zeros((1, HEAD_DIM), F32)
        for h in range(ATT_HEADS):
            sl = slice(h * HEAD_DIM, (h + 1) * HEAD_DIM)
            y = ckr_ref[:, sl]
            dy, dgain = _norm_bwd(y, _rms_r(y), kg_ref[...], dk_ref[:, sl])
            dckv_s[:, sl] = dy.astype(BF16)
            dkg = dkg + dgain
        dkg_ref[...] = dkg
        dckv_s[:, X_W:] = dv_ref[...].astype(BF16)
        dckv = dckv_s[...]
        dw_ref[...] = _dot_tn(mn_ref[...], dckv).astype(BF16)
        dmn = _dot_nt(dckv, w_ref[...])
        mv = m_ref[...]
        dmg_ref[...] = jnp.sum(dmn * mv * _rms_r(mv), axis=0, keepdims=True)

    return _pcall(
        body, name=name,
        out_shape=[SDS((D_MODEL, 2 * X_W), BF16), SDS((1, D_MODEL), F32), SDS((1, HEAD_DIM), F32)],
        scratch_shapes=[pltpu.VMEM((N_MEM, 2 * X_W), BF16)],
        compiler_params=_cp())(dck_n, dcv, ck_raw, memn, mem, wckv, mem_g, ck_g)


def _rope_tables(pos):
    inv_freq = ROPE_THETA ** (-jnp.arange(0, ROT_DIM, 2, dtype=F32) / ROT_DIM)
    ang = pos.astype(F32)[:, None] * inv_freq
    cos, sin = jnp.cos(ang), jnp.sin(ang)
    s = pos.shape[0]
    rest = HEAD_DIM - ROT_DIM
    cos_t = jnp.concatenate([cos, cos, jnp.ones((s, rest), F32)], axis=1)
    sin_t = jnp.concatenate([-sin, sin, jnp.zeros((s, rest), F32)], axis=1)
    return cos_t, sin_t


def _local_step(x, mem, pos, tgt, w_in, shards, sm):
    cos_t, sin_t = _rope_tables(pos)
    wb = {"w_in": w_in}

    pick = lambda *names: {a: shards[a] for a in names}
    proj, xn, got = _norm_linear(x, sm["mix_norm_g"], wb["w_in"], cn=768, out_dtype=F32, name="fwd_in_proj",
                                 shards=pick("w_out", "w_cq", "w_ckv", "w_co"))
    wb.update(got)
    q_rot, k_rot, v, got = _qk_prep(proj, cos_t, sin_t, sm["q_norm_g"], sm["k_norm_g"], pick("w_down"),
                                    name="fwd_qk_prep")
    wb.update(got)
    mix, lse_all, got = _attn_fwd(q_rot, k_rot, v, pick("w_gate_up"), name="fwd_attn")
    wb.update(got)
    mix = _pool_fwd(proj, mix, sm["pool_w"], sm["pool_scale"], name="fwd_pool")
    h1 = _linear_res(mix, wb["w_out"], x, cn=512, name="fwd_out_proj")
    cq_raw, hn = _norm_linear(h1, sm["cross_norm_g"], wb["w_cq"], cn=512, out_dtype=F32, name="fwd_cq_proj")
    memn, ck_raw, ck_n, cv = _mem_fwd(mem, sm["mem_norm_g"], wb["w_ckv"], sm["ck_norm_g"], name="fwd_mem")
    co = _cross_fwd(cq_raw, ck_n, cv, sm["cq_norm_g"], name="fwd_cross")
    h2 = _linear_res(co, wb["w_co"], h1, cn=512, name="fwd_co_proj")
    gu, act, fn = _norm_linear_swiglu(h2, sm["ffn_norm_g"], wb["w_gate_up"], name="fwd_gate_up")
    dy, dyb, sq = _linear_res_loss(act, wb["w_down"], h2, tgt, name="fwd_down_loss")

    gw = {}
    gs = {}
    dgu = _swiglu_bwd(dyb, wb["w_down"], gu, name="bwd_swiglu")
    gw_down = _dw_tn(act, dyb, tkw=1024, tn=1024, name="bwd_dw_down")
    dh2, dh2b, gs["ffn_norm_g"], parts = _linear_nt_normbwd(dgu, wb["w_gate_up"], h2, dy, sm["ffn_norm_g"],
                                                            name="bwd_ffn_in", grads={"w_down": gw_down})
    gw["w_gate_up"] = _dw_tn(fn, dgu, tkw=1024, tn=1536, name="bwd_dw_gate_up")

    d_co = _linear_nt(dh2b, wb["w_co"], cn=512, name="bwd_co_proj")
    gw["w_co"] = _dw_tn(co, dh2b, tkw=512, tn=1024, name="bwd_dw_co")
    dcq, dck_n, dcv, gs["cq_norm_g"] = _cross_bwd(d_co, cq_raw, ck_n, cv, sm["cq_norm_g"], name="bwd_cross")
    gw["w_ckv"], gs["mem_norm_g"], gs["ck_norm_g"] = _mem_bwd(dck_n, dcv, ck_raw, memn, mem, wb["w_ckv"],
                                                             sm["mem_norm_g"], sm["ck_norm_g"], name="bwd_mem")
    dh1, dh1b, gs["cross_norm_g"] = _linear_nt_normbwd(dcq, wb["w_cq"], h1, dh2, sm["cross_norm_g"],
                                                       name="bwd_cq_in")
    gw["w_cq"] = _dw_tn(hn, dcq, tkw=1024, tn=512, name="bwd_dw_cq")

    dmix = _linear_nt(dh1b, wb["w_out"], cn=512, name="bwd_out_proj")
    gw["w_out"] = _dw_tn(mix, dh1b, tkw=1024, tn=1024, name="bwd_dw_out")
    du, gs["pool_w"], gs["pool_scale"] = _pool_bwd(proj, dmix, sm["pool_w"], sm["pool_scale"], name="bwd_pool")
    dq, dk, dv, got = _attn_bwd(q_rot, k_rot, v, mix, dmix, lse_all, gw, name="bwd_attn")
    parts.update(got)
    dproj, gs["q_norm_g"], gs["k_norm_g"] = _qk_prep_bwd(proj, dq, dk, dv, du, cos_t, sin_t, sm["q_norm_g"],
                                                         sm["k_norm_g"], name="bwd_qk_prep")
    gw_in = _dw_tn(xn, dproj, tkw=1024, tn=1536, name="bwd_dw_in")
    dx, _, gs["mix_norm_g"], last = _linear_nt_normbwd(dproj, wb["w_in"], x, dh1, sm["mix_norm_g"],
                                                       name="bwd_in_proj", grads={"w_in": gw_in})
    parts.update(last)
    return sq, dx, parts, gs


SHARDED = ("w_in", "w_out", "w_cq", "w_ckv", "w_co", "w_gate_up", "w_down")
SMALL = ("mix_norm_g", "q_norm_g", "k_norm_g", "pool_w", "pool_scale", "cross_norm_g", "mem_norm_g", "cq_norm_g",
         "ck_norm_g", "ffn_norm_g")
FULL = {
    "w_in": ((D_MODEL, IN_W), 1, IN_W // N_DEV),
    "w_out": ((D_MODEL, D_MODEL), 0, D_MODEL // N_DEV),
    "w_cq": ((D_MODEL, X_W), 0, D_MODEL // N_DEV),
    "w_ckv": ((D_MODEL, 2 * X_W), 0, D_MODEL // N_DEV),
    "w_co": ((X_W, D_MODEL), 1, D_MODEL // N_DEV),
    "w_gate_up": ((D_MODEL, 2 * FF_PAD), 1, FF_TILE),
    "w_down": ((FF_PAD, D_MODEL), 0, DOWN_SHARD),
}


def _shard_shape(name):
    shape, axis, width = FULL[name]
    return tuple(width if a == axis else n for a, n in enumerate(shape))


def _window(ref, name, dev):
    _, axis, width = FULL[name]
    if name == "w_down":
        start = pl.multiple_of((dev // 2) * FF_TILE + (dev % 2) * DOWN_SHARD, HALO)
    else:
        start = pl.multiple_of(dev * width, BLOCK)
    return ref.at[pl.ds(start, width), :] if axis == 0 else ref.at[:, pl.ds(start, width)]


def _mesh_place():
    x, y, c = lax.axis_index("x"), lax.axis_index("y"), lax.axis_index("c")
    return x, y, c, 4 * x + 2 * y + c


def _peer(x, y, c, k):
    px = 1 - x if k & 4 else x
    py = 1 - y if k & 2 else y
    pc = 1 - c if k & 1 else c
    return (px, py, pc), 4 * px + 2 * py + pc


HBM_SPEC = pl.BlockSpec(memory_space=pltpu.HBM)


def _comm_sems(n):
    return [pltpu.SemaphoreType.DMA((n, N_DEV - 1)), pltpu.SemaphoreType.DMA((n, N_DEV - 1)),
            pltpu.SemaphoreType.DMA((n,))]


DIRECT = (1, 2, 4, 6)
RELAYED = (2, 4, 6)


def _relay_step(steps):
    return max(1, (steps * 11) // 16) if steps > 1 else 0


def _gather_copies(names, ins, outs, send_sems, recv_sems, local_sems):
    x, y, c, me = _mesh_place()
    sibling, _ = _peer(x, y, c, 1)
    local, direct, relays, recv = [], [], {}, {}
    for wi, name in enumerate(names):
        def sems(k, to):
            return dict(send_sem=send_sems.at[wi, k - 1], recv_sem=recv_sems.at[wi, k - 1], device_id=to,
                        device_id_type=MESH)
        local.append(pltpu.make_async_copy(ins[wi], _window(outs[wi], name, me), local_sems.at[wi]))
        for k in range(1, N_DEV):
            peer, pidx = _peer(x, y, c, k)
            win = _window(outs[wi], name, pidx)
            recv[wi, k] = pltpu.make_async_remote_copy(src_ref=ins[wi], dst_ref=win, **sems(k, peer))
            if k in DIRECT:
                direct.append(pltpu.make_async_remote_copy(src_ref=ins[wi], dst_ref=_window(outs[wi], name, me),
                                                           **sems(k, peer)))
            if k in RELAYED:
                relays[wi, k] = pltpu.make_async_remote_copy(src_ref=win, dst_ref=win, **sems(k + 1, sibling))
    return local, direct, relays, recv


def _exchange_copies(names, ins, outs, send_sems, recv_sems, local_sems):
    x, y, c, me = _mesh_place()
    local, sent, recv = [], [], []
    for wi, name in enumerate(names):
        local.append(pltpu.make_async_copy(_window(ins[wi], name, me), outs[wi].at[0], local_sems.at[wi]))
        for k in range(1, N_DEV):
            peer, pidx = _peer(x, y, c, k)
            sems = dict(send_sem=send_sems.at[wi, k - 1], recv_sem=recv_sems.at[wi, k - 1], device_id=peer,
                        device_id_type=MESH)
            sent.append(pltpu.make_async_remote_copy(src_ref=_window(ins[wi], name, pidx), dst_ref=outs[wi].at[k], **sems))
            recv.append(pltpu.make_async_remote_copy(src_ref=_window(ins[wi], name, me), dst_ref=outs[wi].at[k], **sems))
    return local, sent, recv


def _down_pads(down_ref, zero_ref, zero_sems):
    return [pltpu.make_async_copy(zero_ref, down_ref.at[pl.ds(t * FF_TILE + FF_SHARD, FF_TILE - FF_SHARD), :],
                                  zero_sems.at[t]) for t in range(FF_PAD // FF_TILE)]


def _gather_scratch(names):
    if not names:
        return []
    pad = [pltpu.VMEM((FF_TILE - FF_SHARD, D_MODEL), BF16), pltpu.SemaphoreType.DMA((FF_PAD // FF_TILE,))]
    return _comm_sems(len(names)) + (pad if "w_down" in names else [])


def _gather_ops(names, ins, outs, scratch):
    local, direct, relays, recv = _gather_copies(names, ins, outs, *scratch[:3])
    pads = _down_pads(outs[names.index("w_down")], scratch[3], scratch[4]) if "w_down" in names else []

    def start():
        if pads:
            scratch[3][...] = jnp.zeros_like(scratch[3])
        for cp in local + direct + pads:
            cp.start()

    def relay():
        for (wi, k), cp in relays.items():
            recv[wi, k].wait_recv()
            cp.start()

    def finish():
        for (wi, k), cp in recv.items():
            if k not in RELAYED:
                cp.wait_recv()
        for cp in direct + list(relays.values()):
            cp.wait_send()
        for cp in local + pads:
            cp.wait()

    return start, relay, finish


def _start(copies):
    local, sent, _ = copies
    for cp in local + sent:
        cp.start()


def _finish(copies):
    local, sent, recv = copies
    for cp in recv:
        cp.wait_recv()
    for cp in sent:
        cp.wait_send()
    for cp in local:
        cp.wait()


def _gather_weights(shards):
    names = tuple(shards)
    nw = len(names)

    def body(*refs):
        start, relay, finish = _gather_ops(names, refs[:nw], refs[nw:2 * nw], refs[2 * nw:])
        start()
        relay()
        finish()

    outs = _pcall(
        body, name="gather_weights",
        in_specs=[HBM_SPEC] * nw, out_specs=[HBM_SPEC] * nw,
        out_shape=[SDS(FULL[n][0], BF16) for n in names],
        scratch_shapes=_comm_sems(nw))(*[shards[n] for n in names])
    return dict(zip(names, outs))


def _exchange_small(blocks):
    nb = len(blocks)

    def body(*refs):
        ins, outs = refs[:nb], refs[nb:2 * nb]
        send_sems, recv_sems, local_sems = refs[2 * nb:]
        x, y, c, me = _mesh_place()
        local, sent, recv = [], [], []
        for bi in range(nb):
            local.append(pltpu.make_async_copy(ins[bi], outs[bi].at[me], local_sems.at[bi]))
            for k in range(1, N_DEV):
                peer, pidx = _peer(x, y, c, k)
                sems = dict(send_sem=send_sems.at[bi, k - 1], recv_sem=recv_sems.at[bi, k - 1], device_id=peer,
                            device_id_type=MESH)
                sent.append(pltpu.make_async_remote_copy(src_ref=ins[bi], dst_ref=outs[bi].at[me], **sems))
                recv.append(pltpu.make_async_remote_copy(src_ref=ins[bi], dst_ref=outs[bi].at[pidx], **sems))
        _start((local, sent, recv))
        _finish((local, sent, recv))

    return _pcall(
        body, name="exchange_small", in_specs=[HBM_SPEC] * nb, out_specs=[HBM_SPEC] * nb,
        out_shape=[SDS((N_DEV,) + a.shape, F32) for a in blocks], scratch_shapes=_comm_sems(nb))(*blocks)


def _adam_math(g, w, m, v):
    m_new = ADAM_B1 * m + (1.0 - ADAM_B1) * g
    v_new = ADAM_B2 * v + (1.0 - ADAM_B2) * (g * g)
    m_hat = m_new / (1.0 - ADAM_B1 ** ADAM_STEP)
    v_hat = v_new / (1.0 - ADAM_B2 ** ADAM_STEP)
    return -ADAM_LR * (m_hat / (jnp.sqrt(v_hat) + ADAM_EPS) + ADAM_WD * w), m_new, v_new


def _adamw_small(parts, w, m, v, sq_parts, *, name):
    n = len(parts)

    def body(*refs):
        p_refs, w_refs, m_refs, v_refs = refs[:n], refs[n:2 * n], refs[2 * n:3 * n], refs[3 * n:4 * n]
        sq_ref, outs = refs[4 * n], refs[4 * n + 1:]
        for i in range(n):
            g = p_refs[i][0]
            for k in range(1, N_DEV):
                g = g + p_refs[i][k]
            delta, m_new, v_new = _adam_math(g, w_refs[i][...], m_refs[i][...], v_refs[i][...])
            outs[4 * i][...] = g
            outs[4 * i + 1][...] = delta
            outs[4 * i + 2][...] = m_new
            outs[4 * i + 3][...] = v_new
        tot = sq_ref[0]
        for k in range(1, N_DEV):
            tot = tot + sq_ref[k]
        outs[4 * n][...] = (0.5 / D_MODEL) * jnp.sum(tot, axis=1, keepdims=True)

    out_shape = [SDS(a.shape, F32) for a in w for _ in range(4)] + [SDS((1, 1), F32)]
    outs = _pcall(body, name=name, out_shape=out_shape, compiler_params=_cp())(*parts, *w, *m, *v, sq_parts)
    return [outs[4 * i:4 * i + 4] for i in range(n)], outs[4 * n][0, 0]


def _adamw(parts, w, m, v, *, name):
    r, c = w.shape
    tr = r
    for cand in (256, 128, 88):
        if r % cand == 0:
            tr = cand
            break

    def body(p_ref, w_ref, m_ref, v_ref, g_ref, d_ref, mo_ref, vo_ref):
        g = p_ref[0].astype(F32)
        for k in range(1, N_DEV):
            g = g + p_ref[k].astype(F32)
        g_ref[...] = g
        d_ref[...], mo_ref[...], vo_ref[...] = _adam_math(g, w_ref[...], m_ref[...], v_ref[...])

    row = pl.BlockSpec((tr, c), lambda i: (i, 0))
    return _pcall(
        body, name=name, grid=(r // tr,),
        in_specs=[pl.BlockSpec((N_DEV, tr, c), lambda i: (0, i, 0)), row, row, row],
        out_specs=[row] * 4, out_shape=[SDS((r, c), F32)] * 4, compiler_params=_cp())(parts, w, m, v)


def _pad_cols(a, width):
    return jnp.pad(a, ((0, 0), (0, width - a.shape[1])))


def kernel(x, mem, positions, mix_norm_g, w_in, q_norm_g, k_norm_g, pool_w, pool_scale, w_out, cross_norm_g, mem_norm_g, w_cq, w_ckv, cq_norm_g, ck_norm_g, w_co, ffn_norm_g, w_gate_up, w_down, loss_target, m_mix_norm_g, m_w_in, m_q_norm_g, m_k_norm_g, m_pool_w, m_pool_scale, m_w_out, m_cross_norm_g, m_mem_norm_g, m_w_cq, m_w_ckv, m_cq_norm_g, m_ck_norm_g, m_w_co, m_ffn_norm_g, m_w_gate_up, m_w_down, v_mix_norm_g, v_w_in, v_q_norm_g, v_k_norm_g, v_pool_w, v_pool_scale, v_w_out, v_cross_norm_g, v_mem_norm_g, v_w_cq, v_w_ckv, v_cq_norm_g, v_ck_norm_g, v_w_co, v_ffn_norm_g, v_w_gate_up, v_w_down):
    given = dict(locals())
    w_f32 = {n: given[n][0] for n in SHARDED + SMALL}
    m_f32 = {n: given["m_" + n][0] for n in SHARDED + SMALL}
    v_f32 = {n: given["v_" + n][0] for n in SHARDED + SMALL}
    for d in (w_f32, m_f32, v_f32):
        d["w_gate_up"] = _pad_cols(d["w_gate_up"], FF_TILE)

    shards = {n: w_f32[n].astype(BF16) for n in SHARDED}
    w_in_full = _gather_weights({"w_in": shards.pop("w_in")})["w_in"]
    sm_rows = {n: (w_f32[n] if w_f32[n].ndim == 3 else w_f32[n].reshape(1, -1)) for n in SMALL}
    sq, dx, parts, gs = _local_step(x[0], mem[0], positions[0], loss_target[0], w_in_full, shards, sm_rows)

    flat = lambda a: a.reshape(-1, a.shape[-1])
    got = _exchange_small([flat(gs[n]) for n in SMALL] + [sq])
    small_res, loss = _adamw_small(got[:-1], [flat(sm_rows[n]) for n in SMALL],
                                   [flat(m_f32[n].reshape(sm_rows[n].shape)) for n in SMALL],
                                   [flat(v_f32[n].reshape(sm_rows[n].shape)) for n in SMALL], got[-1],
                                   name="adamw_small")
    res = {n: [a.reshape(w_f32[n].shape) for a in small_res[i]] for i, n in enumerate(SMALL)}
    for n in SHARDED:
        res[n] = _adamw(parts[n], w_f32[n], m_f32[n], v_f32[n], name="adamw_" + n)
    res["w_gate_up"] = [a[:, :FF_SHARD] for a in res["w_gate_up"]]
    order = ("mix_norm_g", "w_in", "q_norm_g", "k_norm_g", "pool_w", "pool_scale", "w_out", "cross_norm_g",
             "mem_norm_g", "w_cq", "w_ckv", "cq_norm_g", "ck_norm_g", "w_co", "ffn_norm_g", "w_gate_up", "w_down")
    outs = [loss, dx[None]]
    for which in range(4):
        outs += [res[n][which][None] for n in order]
    return tuple(outs)
```

```python
import functools

import jax
import jax.numpy as jnp
from jax import lax
from jax.experimental import pallas as pl
from jax.experimental.pallas import tpu as pltpu

F32 = jnp.float32
BF16 = jnp.bfloat16
SDS = jax.ShapeDtypeStruct

D_MODEL = 1024
HEAD_DIM = 128
N_GROUPS = 3
DILATIONS = (1, 4, 16)
ATT_HEADS = 4
Q_W = 1536
KV_W = 512
POOL_W = 512
POOL_WINDOWS = (2, 4, 8, 16)
IN_W = 3072
X_W = 512
N_MEM = 256
D_FF = 2816
FF_TILE = 768
FF_SHARD = 704
FF_PAD = 4 * FF_TILE
DOWN_SHARD = 352
ROT_DIM = 32
ROT_HALF = 16
ROPE_THETA = 500000.0
EPS = 1e-6
NEG_INF = -1e30
SCALE = HEAD_DIM ** -0.5
BLOCK = 128
HALO = 16

ADAM_LR = 0.001
ADAM_B1 = 0.9
ADAM_B2 = 0.999
ADAM_EPS = 1e-08
ADAM_WD = 0.01
ADAM_STEP = 10

N_DEV = 8
VMEM_LIMIT_BYTES = 56 * 1024 * 1024
MESH = pl.DeviceIdType.MESH


def _pcall(body, **kw):
    return pl.pallas_call(body, **kw)


def _cp():
    return pltpu.CompilerParams(vmem_limit_bytes=VMEM_LIMIT_BYTES)


def _dot(a, b):
    return lax.dot_general(a, b, (((1,), (0,)), ((), ())), preferred_element_type=F32)


def _dot_nt(a, b):
    return lax.dot_general(a, b, (((1,), (1,)), ((), ())), preferred_element_type=F32)


def _dot_tn(a, b):
    return lax.dot_general(a, b, (((0,), (0,)), ((), ())), preferred_element_type=F32)


def _rows(s):
    return min(512, s)


def _rms_r(x):
    return lax.rsqrt(jnp.mean(x * x, axis=-1, keepdims=True) + EPS)


def _norm_bwd(x, r, gain, dxn):
    z = dxn * gain
    dx = r * z - x * (r * r * r * jnp.mean(z * x, axis=-1, keepdims=True))
    dgain = jnp.sum(dxn * x * r, axis=0, keepdims=True)
    return dx, dgain


def _split_bf16(t):
    hi = t.astype(BF16)
    return hi, (t - hi.astype(F32)).astype(BF16)


def _lane_sums(t, ones):
    hi, lo = _split_bf16(t)
    return _dot(hi, ones) + _dot(lo, ones)


def _head_r(y, ones):
    return lax.rsqrt(_lane_sums(y * y, ones) * (1.0 / HEAD_DIM) + EPS)


def _head_norm_bwd(y, r, gain, dn, ones):
    z = dn * gain
    dy = r * z - y * (r * r * r * (_dot((z * y).astype(BF16), ones) * (1.0 / HEAD_DIM)))
    return dy, jnp.sum(dn * y * r, axis=0, keepdims=True)


def _swap_matrix():
    src = lax.broadcasted_iota(jnp.int32, (HEAD_DIM, HEAD_DIM), 0)
    dst = lax.broadcasted_iota(jnp.int32, (HEAD_DIM, HEAD_DIM), 1)
    hit = ((dst < ROT_HALF) & (src == dst + ROT_HALF)) | ((dst >= ROT_HALF) & (dst < ROT_DIM) & (src == dst - ROT_HALF))
    return jnp.where(hit, 1.0, 0.0).astype(BF16)


def _partner(t, swap):
    hi, lo = _split_bf16(t)
    return _dot(hi, swap) + _dot(lo, swap)


def _rope(n, cos_t, sin_t, swap):
    return n * cos_t + _partner(n, swap) * sin_t


def _rope_bwd(d, cos_t, sin_t, swap):
    return d * cos_t + _dot((d * sin_t).astype(BF16), swap)


def _resident(shape):
    return pl.BlockSpec(shape, lambda i: (0,) * len(shape), pipeline_mode=pl.Buffered(1))


def _chunks(n, cn):
    return [slice(j * cn, (j + 1) * cn) for j in range(n // cn)]


def _norm_linear(x, gain, w, *, cn, out_dtype, name, shards=None):
    s, k = x.shape
    n = w.shape[1]
    tm = _rows(s)
    names = tuple(shards or ())
    nr = len(names)

    def body(*refs):
        x_ref, g_ref, w_ref = refs[:3]
        y_ref, xn_ref = refs[3 + nr:5 + nr]
        if nr:
            start, relay, finish = _gather_ops(names, refs[3:3 + nr], refs[5 + nr:5 + 2 * nr], refs[5 + 2 * nr:])
            pl.when(pl.program_id(0) == 0)(start)
            pl.when(pl.program_id(0) == _relay_step(s // tm))(relay)
        xv = x_ref[...]
        xn_ref[...] = (xv * _rms_r(xv) * g_ref[...]).astype(BF16)
        for c in _chunks(n, cn):
            y_ref[:, c] = _dot(xn_ref[...], w_ref[:, c]).astype(out_dtype)
        if nr:
            pl.when(pl.program_id(0) == s // tm - 1)(finish)

    row = lambda w_: pl.BlockSpec((tm, w_), lambda i: (i, 0))
    outs = _pcall(
        body, name=name, grid=(s // tm,),
        in_specs=[row(k), _resident((1, k)), _resident((k, n))] + [HBM_SPEC] * nr,
        out_specs=[row(n), row(k)] + [HBM_SPEC] * nr,
        out_shape=[SDS((s, n), out_dtype), SDS((s, k), BF16)] + [SDS(FULL[a][0], BF16) for a in names],
        scratch_shapes=_gather_scratch(names),
        compiler_params=_cp())(x, gain, w, *[shards[a] for a in names])
    return (outs[0], outs[1], dict(zip(names, outs[2:]))) if nr else tuple(outs)


def _norm_linear_swiglu(x, gain, wgu, *, name):
    s, k = x.shape
    tm = _rows(s)

    def body(x_ref, g_ref, w_ref, gu_ref, a_ref, xn_ref):
        xv = x_ref[...]
        xn_ref[...] = (xv * _rms_r(xv) * g_ref[...]).astype(BF16)
        for c in _chunks(FF_PAD, FF_TILE):
            g = _dot(xn_ref[...], w_ref[:, c])
            u = _dot(xn_ref[...], w_ref[:, slice(FF_PAD + c.start, FF_PAD + c.stop)])
            a_ref[:, c] = (g * jax.nn.sigmoid(g) * u).astype(BF16)
            gu_ref[0, :, c] = g.astype(BF16)
            gu_ref[1, :, c] = u.astype(BF16)

    row = lambda w_: pl.BlockSpec((tm, w_), lambda i: (i, 0))
    return _pcall(
        body, name=name, grid=(s // tm,),
        in_specs=[row(k), _resident((1, k)), _resident((k, 2 * FF_PAD))],
        out_specs=[pl.BlockSpec((2, tm, FF_PAD), lambda i: (0, i, 0)), row(FF_PAD), row(k)],
        out_shape=[SDS((2, s, FF_PAD), BF16), SDS((s, FF_PAD), BF16), SDS((s, k), BF16)],
        compiler_params=_cp())(x, gain, wgu)


def _linear_res(a, w, res, *, cn, name):
    s, k = a.shape
    n = w.shape[1]
    tm = _rows(s)

    def body(a_ref, w_ref, r_ref, y_ref):
        for c in _chunks(n, cn):
            y_ref[:, c] = r_ref[:, c] + _dot(a_ref[...], w_ref[:, c])

    row = lambda w_: pl.BlockSpec((tm, w_), lambda i: (i, 0))
    return _pcall(
        body, name=name, grid=(s // tm,),
        in_specs=[row(k), _resident((k, n)), row(n)], out_specs=row(n),
        out_shape=SDS((s, n), F32), compiler_params=_cp())(a, w, res)


def _linear_res_loss(a, w, res, tgt, *, name):
    s, k = a.shape
    n = w.shape[1]
    tm = _rows(s)

    def body(a_ref, w_ref, r_ref, t_ref, dy_ref, dyb_ref, sq_ref):
        e = r_ref[...] + _dot(a_ref[...], w_ref[...]) - t_ref[...]
        dy = e * (1.0 / n)
        dy_ref[...] = dy
        dyb_ref[...] = dy.astype(BF16)

        @pl.when(pl.program_id(0) == 0)
        def _():
            sq_ref[...] = jnp.zeros_like(sq_ref)
        sq_ref[...] += jnp.sum(e * e, axis=0, keepdims=True)

    row = lambda w_: pl.BlockSpec((tm, w_), lambda i: (i, 0))
    return _pcall(
        body, name=name, grid=(s // tm,),
        in_specs=[row(k), _resident((k, n)), row(n), row(n)],
        out_specs=[row(n), row(n), pl.BlockSpec((1, n), lambda i: (0, 0))],
        out_shape=[SDS((s, n), F32), SDS((s, n), BF16), SDS((1, n), F32)],
        compiler_params=_cp())(a, w, res, tgt)


def _linear_nt(g, w, *, cn, name):
    s, k = g.shape
    n = w.shape[0]
    tm = _rows(s)

    def body(g_ref, w_ref, y_ref):
        for c in _chunks(n, cn):
            y_ref[:, c] = _dot_nt(g_ref[...], w_ref[c, :]).astype(BF16)

    row = lambda w_: pl.BlockSpec((tm, w_), lambda i: (i, 0))
    return _pcall(
        body, name=name, grid=(s // tm,),
        in_specs=[row(k), _resident((n, k))], out_specs=row(n),
        out_shape=SDS((s, n), BF16), compiler_params=_cp())(g, w)


def _swiglu_bwd(dyb, wd, gu, *, name):
    s, n = dyb.shape
    tm = _rows(s)

    def body(dy_ref, wd_ref, gu_ref, dgu_ref):
        for c in _chunks(FF_PAD, FF_TILE):
            da = _dot_nt(dy_ref[...], wd_ref[c, :])
            g = gu_ref[0, :, c].astype(F32)
            u = gu_ref[1, :, c].astype(F32)
            sg = jax.nn.sigmoid(g)
            dgu_ref[0, :, c] = (da * u * (sg * (1.0 + g * (1.0 - sg)))).astype(BF16)
            dgu_ref[1, :, c] = (da * (g * sg)).astype(BF16)

    half = pl.BlockSpec((2, tm, FF_PAD), lambda i: (0, i, 0))
    return _pcall(
        body, name=name, grid=(s // tm,),
        in_specs=[pl.BlockSpec((tm, n), lambda i: (i, 0)), _resident((FF_PAD, n)), half],
        out_specs=half, out_shape=SDS((2, s, FF_PAD), BF16), compiler_params=_cp())(dyb, wd, gu)


def _linear_nt_normbwd(g, w, x, dres, gain, *, name, grads=None):
    d, k = w.shape
    s = x.shape[0]
    tm = _rows(s)
    names = tuple(grads or ())
    nr = len(names)

    def body(*refs):
        g_ref, w_ref, x_ref, dr_ref, gn_ref = refs[:5]
        dx_ref, dxb_ref, dg_ref = refs[5 + nr:8 + nr]
        if nr:
            copies = _exchange_copies(names, refs[5:5 + nr], refs[8 + nr:8 + 2 * nr], *refs[8 + 2 * nr:])

        @pl.when(pl.program_id(0) == 0)
        def _():
            dg_ref[...] = jnp.zeros_like(dg_ref)
            if nr:
                _start(copies)

        if g.ndim == 3:
            dxn = _dot_nt(g_ref[0], w_ref[:, :k // 2]) + _dot_nt(g_ref[1], w_ref[:, k // 2:])
        else:
            dxn = _dot_nt(g_ref[...], w_ref[...])
        xv = x_ref[...]
        dx, dgain = _norm_bwd(xv, _rms_r(xv), gn_ref[...], dxn)
        out = dr_ref[...] + dx
        dx_ref[...] = out
        dxb_ref[...] = out.astype(BF16)
        dg_ref[...] += dgain

        if nr:
            @pl.when(pl.program_id(0) == s // tm - 1)
            def _():
                _finish(copies)

    row = pl.BlockSpec((tm, d), lambda i: (i, 0))
    g_spec = (pl.BlockSpec((2, tm, k // 2), lambda i: (0, i, 0)) if g.ndim == 3
              else pl.BlockSpec((tm, k), lambda i: (i, 0)))
    outs = _pcall(
        body, name=name, grid=(s // tm,),
        in_specs=[g_spec, _resident((d, k)), row, row, _resident((1, d))] + [HBM_SPEC] * nr,
        out_specs=[row, row, pl.BlockSpec((1, d), lambda i: (0, 0))] + [HBM_SPEC] * nr,
        out_shape=[SDS((s, d), F32), SDS((s, d), BF16), SDS((1, d), F32)]
        + [SDS((N_DEV,) + _shard_shape(n), BF16) for n in names],
        scratch_shapes=_comm_sems(nr) if nr else [],
        compiler_params=_cp())(g, w, x, dres, gain, *[grads[n] for n in names])
    return (outs[0], outs[1], outs[2], dict(zip(names, outs[3:]))) if nr else tuple(outs)


def _dw_tn(x, g, *, tkw, tn, name):
    s, kw = x.shape
    halves = g.ndim == 3
    n = 2 * g.shape[2] if halves else g.shape[1]
    ts = min(2048, s)
    ns = s // ts
    per_half = n // 2 // tn

    def body(x_ref, g_ref, o_ref, acc_ref):
        ss = pl.program_id(2)

        @pl.when(ss == 0)
        def _():
            acc_ref[...] = jnp.zeros_like(acc_ref)

        acc_ref[...] += _dot_tn(x_ref[...], g_ref[...])

        @pl.when(ss == ns - 1)
        def _():
            o_ref[...] = acc_ref[...].astype(BF16)

    g_spec = (pl.BlockSpec((None, ts, tn), lambda a, b, ss: (b // per_half, ss, b % per_half)) if halves
              else pl.BlockSpec((ts, tn), lambda a, b, ss: (ss, b)))
    return _pcall(
        body, name=name, grid=(kw // tkw, n // tn, ns),
        in_specs=[pl.BlockSpec((ts, tkw), lambda a, b, ss: (ss, a)), g_spec],
        out_specs=pl.BlockSpec((tkw, tn), lambda a, b, ss: (a, b)),
        out_shape=SDS((kw, n), BF16),
        scratch_shapes=[pltpu.VMEM((tkw, tn), F32)], compiler_params=_cp())(x, g)


def _qk_prep(proj, cos_t, sin_t, qg, kg, *, name):
    s = proj.shape[0]
    tm = _rows(s)
    nqh = Q_W // HEAD_DIM

    def body(q_ref, k_ref, v_ref, c_ref, s_ref, qg_ref, kg_ref, qo_ref, ko_ref, vo_ref):
        c, sn = c_ref[...], s_ref[...]
        ones, swap = jnp.ones((HEAD_DIM, HEAD_DIM), BF16), _swap_matrix()
        for h in range(nqh):
            sl = slice(h * HEAD_DIM, (h + 1) * HEAD_DIM)
            y = q_ref[:, sl]
            qo_ref[:, sl] = (_rope(y * _head_r(y, ones) * qg_ref[...], c, sn, swap) * SCALE).astype(BF16)
        for h in range(ATT_HEADS):
            sl = slice(h * HEAD_DIM, (h + 1) * HEAD_DIM)
            y = k_ref[:, sl]
            ko_ref[:, sl] = _rope(y * _head_r(y, ones) * kg_ref[...], c, sn, swap).astype(BF16)
        vo_ref[...] = v_ref[...].astype(BF16)

    row = lambda w, j: pl.BlockSpec((tm, w), lambda i: (i, j))
    one = pl.BlockSpec((1, HEAD_DIM), lambda i: (0, 0))
    return _pcall(
        body, name=name, grid=(s // tm,),
        in_specs=[row(Q_W, 0), row(KV_W, 3), row(KV_W, 4), row(HEAD_DIM, 0), row(HEAD_DIM, 0), one, one],
        out_specs=[row(Q_W, 0), row(KV_W, 0), row(KV_W, 0)],
        out_shape=[SDS((s, Q_W), BF16), SDS((s, KV_W), BF16), SDS((s, KV_W), BF16)],
        compiler_params=_cp())(proj, proj, proj, cos_t, sin_t, qg, kg)


def _qk_prep_bwd(proj, dq, dk, dv, du, cos_t, sin_t, qg, kg, *, name):
    s = proj.shape[0]
    tm = _rows(s)

    def body(q_ref, k_ref, dq_ref, dk_ref, dv_ref, du_ref, c_ref, s_ref, qg_ref, kg_ref, dp_ref, dqg_ref, dkg_ref):
        c, sn = c_ref[...], s_ref[...]

        @pl.when(pl.program_id(0) == 0)
        def _():
            dqg_ref[...] = jnp.zeros_like(dqg_ref)
            dkg_ref[...] = jnp.zeros_like(dkg_ref)

        ones, swap = jnp.ones((HEAD_DIM, HEAD_DIM), BF16), _swap_matrix()
        dqg = jnp.zeros((1, HEAD_DIM), F32)
        for g in range(N_GROUPS):
            for h in range(ATT_HEADS):
                sl = slice(h * HEAD_DIM, (h + 1) * HEAD_DIM)
                col = slice(g * KV_W + h * HEAD_DIM, g * KV_W + (h + 1) * HEAD_DIM)
                y = q_ref[:, col]
                dn = _rope_bwd(dq_ref[g, :, sl].astype(F32) * SCALE, c, sn, swap)
                dy, dgain = _head_norm_bwd(y, _head_r(y, ones), qg_ref[...], dn, ones)
                dp_ref[:, col] = dy.astype(BF16)
                dqg = dqg + dgain
        dqg_ref[...] += dqg

        dkg = jnp.zeros((1, HEAD_DIM), F32)
        for h in range(ATT_HEADS):
            sl = slice(h * HEAD_DIM, (h + 1) * HEAD_DIM)
            y = k_ref[:, sl]
            dn = _rope_bwd(dk_ref[:, sl], c, sn, swap)
            dy, dgain = _head_norm_bwd(y, _head_r(y, ones), kg_ref[...], dn, ones)
            dp_ref[:, Q_W + h * HEAD_DIM:Q_W + (h + 1) * HEAD_DIM] = dy.astype(BF16)
            dkg = dkg + dgain
        dkg_ref[...] += dkg

        dp_ref[:, Q_W + KV_W:Q_W + 2 * KV_W] = dv_ref[...].astype(BF16)
        dp_ref[:, Q_W + 2 * KV_W:] = du_ref[...].astype(BF16)

    row = lambda w, j: pl.BlockSpec((tm, w), lambda i: (i, j))
    one = pl.BlockSpec((1, HEAD_DIM), lambda i: (0, 0))
    return _pcall(
        body, name=name, grid=(s // tm,),
        in_specs=[row(Q_W, 0), row(KV_W, 3), pl.BlockSpec((N_GROUPS, tm, KV_W), lambda i: (0, i, 0))]
        + [row(KV_W, 0)] * 3 + [row(HEAD_DIM, 0), row(HEAD_DIM, 0), one, one],
        out_specs=[row(IN_W, 0), one, one],
        out_shape=[SDS((s, IN_W), BF16), SDS((1, HEAD_DIM), F32), SDS((1, HEAD_DIM), F32)],
        compiler_params=_cp())(proj, proj, dq, dk, dv, du, cos_t, sin_t, qg, kg)


ATT_ROWS = 16 * BLOCK


def _sub(ref, start, d, size=BLOCK):
    return ref[pl.ds(start, size, stride=d), :] if d > 1 else ref[pl.ds(start, size), :]


def _sub_set(ref, start, d, val):
    if d > 1:
        ref[pl.ds(start, BLOCK, stride=d), :] = val
    else:
        ref[pl.ds(start, BLOCK), :] = val


def _band_masks():
    row = lax.broadcasted_iota(jnp.int32, (BLOCK, BLOCK), 0)
    col = lax.broadcasted_iota(jnp.int32, (BLOCK, BLOCK), 1)
    return col <= row, col >= row


def _eye(n=BLOCK):
    row = lax.broadcasted_iota(jnp.int32, (n, n), 0)
    col = lax.broadcasted_iota(jnp.int32, (n, n), 1)
    return jnp.where(row == col, 1.0, 0.0).astype(BF16)


def _attn_fwd(q_rot, k_rot, v, shards, *, name):
    s = q_rot.shape[0]
    rr = ATT_ROWS
    nblk = s // rr
    names = tuple(shards)
    nr = len(names)

    def body(*refs):
        q0, q1, q2, kp, kc, vp, vc = refs[:7]
        mix_ref, lse_ref = refs[7 + nr:9 + nr]
        qs, ks, vs, os_, ls = refs[9 + 2 * nr:14 + 2 * nr]
        h, n = pl.program_id(0), pl.program_id(1)
        start, relay, finish = _gather_ops(names, refs[7:7 + nr], refs[9 + nr:9 + 2 * nr], refs[14 + 2 * nr:])
        pl.when((h == 0) & (n == 0))(start)
        steps = ATT_HEADS * nblk
        early, late = max(1, (steps * 6) // 16), max(1, (steps * 14) // 16)
        pl.when(h * nblk + n == early)(functools.partial(relay, names[:1]))
        pl.when(h * nblk + n == late)(functools.partial(relay, names[1:]))
        for g, q_ref in enumerate((q0, q1, q2)):
            qs[g] = q_ref[...].astype(F32)
        ks[:rr] = kp[...].astype(F32)
        ks[rr:] = kc[...].astype(F32)
        vs[:rr] = vp[...].astype(F32)
        vs[rr:] = vc[...].astype(F32)
        m_cur, m_band = _band_masks()
        mask_in = jnp.concatenate([m_band, m_cur], axis=1)
        mask_first = jnp.concatenate([m_band & (n > 0), m_cur], axis=1)
        ones = jnp.ones((2 * BLOCK, HEAD_DIM), BF16)
        pieces = [(g, d, j * BLOCK * d + r, j) for g, d in enumerate(DILATIONS) for r in range(d)
                  for j in range(rr // (BLOCK * d))]

        def scores(piece):
            g, d, base, j = piece
            q = _sub(qs.at[g], base, d).astype(BF16)
            k2 = _sub(ks, rr + base - BLOCK * d, d, 2 * BLOCK).astype(BF16)
            return jnp.where(mask_first if j == 0 else mask_in, _dot_nt(q, k2), NEG_INF)

        sc = scores(pieces[0])
        for i, (g, d, base, j) in enumerate(pieces):
            cur = sc
            if i + 1 < len(pieces):
                sc = scores(pieces[i + 1])
            m = jnp.max(cur, axis=-1, keepdims=True)
            p = jnp.exp(cur - m).astype(BF16)
            v2 = _sub(vs, rr + base - BLOCK * d, d, 2 * BLOCK).astype(BF16)
            acc_l = _dot(p, jnp.concatenate([v2, ones], axis=1))
            l = acc_l[:, HEAD_DIM:]
            _sub_set(os_.at[g], base, d, acc_l[:, :HEAD_DIM] / l)
            _sub_set(ls.at[g], base, d, m + jnp.log(l))
        for c in _chunks(rr, 2 * BLOCK):
            a, b, cc = ls[0, c, :], ls[1, c, :], ls[2, c, :]
            m = jnp.maximum(jnp.maximum(a, b), cc)
            wa, wb, wc = jnp.exp(a - m), jnp.exp(b - m), jnp.exp(cc - m)
            den = wa + wb + wc
            mix_ref[c, :] = ((wa * os_[0, c, :] + wb * os_[1, c, :] + wc * os_[2, c, :]) / den).astype(BF16)
            lse_ref[c, :] = m + jnp.log(den)

        pl.when((h == ATT_HEADS - 1) & (n == nblk - 1))(finish)

    blk = lambda f: pl.BlockSpec((rr, HEAD_DIM), f)
    prv = lambda n: jnp.maximum(n - 1, 0)
    outs = _pcall(
        body, name=name, grid=(ATT_HEADS, nblk),
        in_specs=[blk(lambda h, n, g=g: (n, g * ATT_HEADS + h)) for g in range(N_GROUPS)]
        + [blk(lambda h, n: (prv(n), h)), blk(lambda h, n: (n, h))] * 2 + [HBM_SPEC] * nr,
        out_specs=[blk(lambda h, n: (n, h)), blk(lambda h, n: (n, h))] + [HBM_SPEC] * nr,
        out_shape=[SDS((s, KV_W + POOL_W), BF16), SDS((s, KV_W), F32)] + [SDS(FULL[w][0], BF16) for w in names],
        scratch_shapes=[pltpu.VMEM((N_GROUPS, rr, HEAD_DIM), F32), pltpu.VMEM((2 * rr, HEAD_DIM), F32),
                        pltpu.VMEM((2 * rr, HEAD_DIM), F32), pltpu.VMEM((N_GROUPS, rr, HEAD_DIM), F32),
                        pltpu.VMEM((N_GROUPS, rr, HEAD_DIM), F32)] + _gather_scratch(names),
        compiler_params=_cp())(q_rot, q_rot, q_rot, k_rot, k_rot, v, v, *[shards[w] for w in names])
    return outs[0], outs[1], dict(zip(names, outs[2:]))


def _attn_bwd(q_rot, k_rot, v, mix, dmix, lse, grads, *, name):
    s = q_rot.shape[0]
    rr = ATT_ROWS
    nblk = s // rr
    names = tuple(grads)
    nr = len(names)

    def body(*refs):
        q0, q1, q2, qx0, qx1, qx2, kp, kc, vp, vc, do_c, do_x, o_c, o_x, l_c, l_x = refs[:16]
        dq_ref, dk_ref, dv_ref = refs[16 + nr:19 + nr]
        qs, ks, vs, dos, lss, dls, dqs, dks, dvs, send_sems, recv_sems, local_sems = refs[19 + 2 * nr:]
        h, n = pl.program_id(0), pl.program_id(1)
        copies = _exchange_copies(names, refs[16:16 + nr], refs[19 + nr:19 + 2 * nr], send_sems, recv_sems, local_sems)

        @pl.when((h == 0) & (n == 0))
        def _():
            _start(copies)

        for g, (qc_ref, qx_ref) in enumerate(((q0, qx0), (q1, qx1), (q2, qx2))):
            qs[g, :rr] = qc_ref[...].astype(F32)
            qs[g, rr:] = qx_ref[...].astype(F32)
        ks[:rr] = kp[...].astype(F32)
        ks[rr:] = kc[...].astype(F32)
        vs[:rr] = vp[...].astype(F32)
        vs[rr:] = vc[...].astype(F32)
        lss[:rr] = l_c[...]
        lss[rr:] = l_x[...]
        for half, (d_ref, o_ref) in enumerate(((do_c, o_c), (do_x, o_x))):
            for c in _chunks(rr, 2 * BLOCK):
                cs = slice(half * rr + c.start, half * rr + c.stop)
                dof = d_ref[c, :].astype(F32)
                dos[cs, :] = dof
                dls[cs, :] = jnp.broadcast_to(jnp.sum(dof * o_ref[c, :].astype(F32), axis=-1, keepdims=True),
                                              (2 * BLOCK, HEAD_DIM))
        m_cur, m_band = _band_masks()
        mask_in = jnp.concatenate([m_cur, m_band], axis=0)
        mask_last = jnp.concatenate([m_cur, m_band & (n + 1 < nblk)], axis=0)
        m_first = m_band & (n > 0)
        eye = _eye()
        pieces = [(g, d, j * BLOCK * d + r, j, rr // (BLOCK * d)) for g, d in enumerate(DILATIONS) for r in range(d)
                  for j in range(rr // (BLOCK * d))]

        def front(piece):
            g, d, base, _, _ = piece
            q2 = _sub(qs.at[g], base, d, 2 * BLOCK).astype(BF16)
            do2 = _sub(dos, base, d, 2 * BLOCK).astype(BF16)
            k = _sub(ks, rr + base, d).astype(BF16)
            vv = _sub(vs, rr + base, d).astype(BF16)
            return q2, do2, k, _dot_nt(q2, k), _dot_nt(do2, vv)

        def middle(piece, fr, dq_acc):
            g, d, base, j, nsub = piece
            q2, do2, k, s2, dp2 = fr
            if j == 0:
                kp_ = _sub(ks, rr + base - BLOCK * d, d).astype(BF16)
                p0 = jnp.where(m_first, jnp.exp(_dot_nt(q2[:BLOCK], kp_) - _sub(lss, base, d)), 0.0)
                ds0 = p0 * (_dot_nt(do2[:BLOCK], _sub(vs, rr + base - BLOCK * d, d).astype(BF16)) - _sub(dls, base, d))
                dq_acc = _dot(ds0.astype(BF16), kp_)
            p2 = jnp.where(mask_last if j + 1 == nsub else mask_in,
                           jnp.exp(s2 - _sub(lss, base, d, 2 * BLOCK)), 0.0)
            ds2 = (p2 * (dp2 - _sub(dls, base, d, 2 * BLOCK))).astype(BF16)
            dq2 = _dot(ds2, k)
            return dq2[BLOCK:], (_dot_nt(eye, ds2), _dot_nt(eye, p2.astype(BF16)), q2, do2, dq_acc + dq2[:BLOCK])

        def back(piece, tr):
            g, d, base, _, _ = piece
            ds_t, p_t, q2, do2, dq = tr
            _sub_set(dqs.at[g], base, d, dq)
            dk, dv = _dot(ds_t.astype(BF16), q2), _dot(p_t.astype(BF16), do2)
            if g == 0:
                _sub_set(dks, base, d, dk)
                _sub_set(dvs, base, d, dv)
            else:
                _sub_set(dks, base, d, _sub(dks, base, d) + dk)
                _sub_set(dvs, base, d, _sub(dvs, base, d) + dv)

        fr, held, dq_acc = front(pieces[0]), None, None
        for i, piece in enumerate(pieces):
            cur = fr
            if i + 1 < len(pieces):
                fr = front(pieces[i + 1])
            dq_acc, now = middle(piece, cur, dq_acc)
            if held is not None:
                back(pieces[i - 1], held)
            held = now
        back(pieces[-1], held)
        for g in range(N_GROUPS):
            dq_ref[g] = dqs[g].astype(BF16)
        dk_ref[...] = dks[...]
        dv_ref[...] = dvs[...]

        @pl.when((h == ATT_HEADS - 1) & (n == nblk - 1))
        def _():
            _finish(copies)

    blk = lambda f: pl.BlockSpec((rr, HEAD_DIM), f)
    prv = lambda n: jnp.maximum(n - 1, 0)
    nxt = lambda n: jnp.minimum(n + 1, nblk - 1)
    cur_kv = blk(lambda h, n: (n, h))
    outs = _pcall(
        body, name=name, grid=(ATT_HEADS, nblk),
        in_specs=[blk(lambda h, n, g=g: (n, g * ATT_HEADS + h)) for g in range(N_GROUPS)]
        + [blk(lambda h, n, g=g: (nxt(n), g * ATT_HEADS + h)) for g in range(N_GROUPS)]
        + [blk(lambda h, n: (prv(n), h)), cur_kv] * 2
        + [cur_kv, blk(lambda h, n: (nxt(n), h))] * 3 + [HBM_SPEC] * nr,
        out_specs=[pl.BlockSpec((N_GROUPS, rr, HEAD_DIM), lambda h, n: (0, n, h)), cur_kv, cur_kv] + [HBM_SPEC] * nr,
        out_shape=[SDS((N_GROUPS, s, KV_W), BF16), SDS((s, KV_W), F32), SDS((s, KV_W), F32)]
        + [SDS((N_DEV,) + _shard_shape(w), BF16) for w in names],
        scratch_shapes=[pltpu.VMEM((N_GROUPS, 2 * rr, HEAD_DIM), F32)] + [pltpu.VMEM((2 * rr, HEAD_DIM), F32)] * 5
        + [pltpu.VMEM((N_GROUPS, rr, HEAD_DIM), F32), pltpu.VMEM((rr, HEAD_DIM), F32), pltpu.VMEM((rr, HEAD_DIM), F32)]
        + _comm_sems(nr),
        compiler_params=_cp())(q_rot, q_rot, q_rot, q_rot, q_rot, q_rot, k_rot, k_rot, v, v, dmix, dmix, mix, mix,
                               lse, lse, *[grads[w] for w in names])
    return outs[0], outs[1], outs[2], dict(zip(names, outs[3:]))


def _pool_d(u_ref, halo_ref, i, tm):
    halo = jnp.where(i > 0, halo_ref[...], 0.0)
    t = i * tm + lax.broadcasted_iota(jnp.int32, (tm, 1), 0)
    out = []
    for g, w in enumerate(POOL_WINDOWS):
        sl = slice(g * HEAD_DIM, (g + 1) * HEAD_DIM)
        u = u_ref[:, sl]
        acc = jnp.concatenate([halo[:, sl], u], axis=0)
        sh = 1
        while sh < w:
            acc = acc + pltpu.roll(acc, sh, 0)
            sh *= 2
        cnt = jnp.minimum(t + 1, w).astype(F32)
        out.append(acc[HALO:, :] / cnt - u)
    return out


def _pool_fwd(proj, mix, pool_w, pool_scale, *, name):
    s = proj.shape[0]
    tm = _rows(s)
    ucol = (IN_W - POOL_W) // POOL_W

    def body(u_ref, halo_ref, mix_in, w_ref, sc_ref, o_ref):
        del mix_in
        dd = _pool_d(u_ref, halo_ref, pl.program_id(0), tm)
        for g in range(len(POOL_WINDOWS)):
            sl = slice(g * HEAD_DIM, (g + 1) * HEAD_DIM)
            y = _dot(dd[g].astype(BF16), w_ref[g].astype(BF16))
            o_ref[:, sl] = (y * sc_ref[:, sl]).astype(BF16)

    return _pcall(
        body, name=name, grid=(s // tm,),
        in_specs=[pl.BlockSpec((tm, POOL_W), lambda i: (i, ucol)),
                  pl.BlockSpec((HALO, POOL_W), lambda i: (jnp.maximum(i * (tm // HALO) - 1, 0), ucol)),
                  pl.BlockSpec(memory_space=pl.ANY),
                  pl.BlockSpec((len(POOL_WINDOWS), HEAD_DIM, HEAD_DIM), lambda i: (0, 0, 0)),
                  pl.BlockSpec((1, POOL_W), lambda i: (0, 0))],
        out_specs=pl.BlockSpec((tm, POOL_W), lambda i: (i, 1)),
        out_shape=SDS(mix.shape, BF16), input_output_aliases={2: 0},
        compiler_params=_cp())(proj, proj, mix, pool_w, pool_scale)


def _pool_bwd(proj, dmix, pool_w, pool_scale, *, name):
    s = proj.shape[0]
    tm = _rows(s)
    nblk = s // tm
    ucol = (IN_W - POOL_W) // POOL_W
    ng = len(POOL_WINDOWS)

    def body(u_ref, halo_ref, dp_ref, dpn_ref, w_ref, sc_ref, du_ref, dw_ref, dsc_ref):
        i = pl.program_id(0)

        @pl.when(i == 0)
        def _():
            dw_ref[...] = jnp.zeros_like(dw_ref)
            dsc_ref[...] = jnp.zeros_like(dsc_ref)

        dd = _pool_d(u_ref, halo_ref, i, tm)
        t = i * tm + lax.broadcasted_iota(jnp.int32, (tm, 1), 0)
        dpn = jnp.where(i + 1 < nblk, dpn_ref[...].astype(F32), 0.0)
        for g, w in enumerate(POOL_WINDOWS):
            sl = slice(g * HEAD_DIM, (g + 1) * HEAD_DIM)
            wg = w_ref[g].astype(BF16)
            db = dd[g].astype(BF16)
            dp = dp_ref[:, sl].astype(F32)
            dsc_ref[:, sl] += jnp.sum(dp * _dot(db, wg), axis=0, keepdims=True)
            dy = (dp * sc_ref[:, sl]).astype(BF16)
            dw_ref[g] += _dot_tn(db, dy)
            g_d = _dot_nt(dy, wg)
            g_dn = _dot_nt((dpn[:, sl] * sc_ref[:, sl]).astype(BF16), wg)
            cnt = jnp.minimum(t + 1, w).astype(F32)
            acc = jnp.concatenate([g_d / cnt, g_dn * (1.0 / w)], axis=0)
            sh = 1
            while sh < w:
                acc = acc + pltpu.roll(acc, tm + HALO - sh, 0)
                sh *= 2
            du_ref[:, sl] = acc[:tm, :] - g_d

    nh = s // HALO
    return _pcall(
        body, name=name, grid=(nblk,),
        in_specs=[pl.BlockSpec((tm, POOL_W), lambda i: (i, ucol)),
                  pl.BlockSpec((HALO, POOL_W), lambda i: (jnp.maximum(i * (tm // HALO) - 1, 0), ucol)),
                  pl.BlockSpec((tm, POOL_W), lambda i: (i, 1)),
                  pl.BlockSpec((HALO, POOL_W), lambda i: (jnp.minimum((i + 1) * (tm // HALO), nh - 1), 1)),
                  pl.BlockSpec((ng, HEAD_DIM, HEAD_DIM), lambda i: (0, 0, 0)),
                  pl.BlockSpec((1, POOL_W), lambda i: (0, 0))],
        out_specs=[pl.BlockSpec((tm, POOL_W), lambda i: (i, 0)),
                   pl.BlockSpec((ng, HEAD_DIM, HEAD_DIM), lambda i: (0, 0, 0)),
                   pl.BlockSpec((1, POOL_W), lambda i: (0, 0))],
        out_shape=[SDS((s, POOL_W), F32), SDS((ng, HEAD_DIM, HEAD_DIM), F32), SDS((1, POOL_W), F32)],
        compiler_params=_cp())(proj, proj, dmix, dmix, pool_w, pool_scale)


def _mem_fwd(mem, mem_g, wckv, ck_g, *, name):
    def body(m_ref, g_ref, w_ref, kg_ref, mn_ref, ckr_ref, ckn_ref, cv_ref):
        mv = m_ref[...]
        mn = (mv * _rms_r(mv) * g_ref[...]).astype(BF16)
        mn_ref[...] = mn
        ckv = _dot(mn, w_ref[...])
        ckr_ref[...] = ckv[:, :X_W]
        cv_ref[...] = ckv[:, X_W:].astype(BF16)
        for h in range(ATT_HEADS):
            sl = slice(h * HEAD_DIM, (h + 1) * HEAD_DIM)
            y = ckv[:, sl]
            ckn_ref[:, sl] = (y * _rms_r(y) * kg_ref[...]).astype(BF16)

    return _pcall(
        body, name=name,
        out_shape=[SDS((N_MEM, D_MODEL), BF16), SDS((N_MEM, X_W), F32), SDS((N_MEM, X_W), BF16),
                   SDS((N_MEM, X_W), BF16)],
        compiler_params=_cp())(mem, mem_g, wckv, ck_g)


def _cross_q(cq_ref, g_ref, sl):
    y = cq_ref[:, sl]
    r = _rms_r(y)
    return y, r, y * r * g_ref[...] * SCALE


def _cross_fwd(cq_raw, ck_n, cv, cq_g, *, name):
    s = cq_raw.shape[0]
    tm = _rows(s)

    def body(cq_ref, k_ref, v_ref, g_ref, o_ref):
        for h in range(ATT_HEADS):
            sl = slice(h * HEAD_DIM, (h + 1) * HEAD_DIM)
            _, _, qn = _cross_q(cq_ref, g_ref, sl)
            sc = _dot_nt(qn.astype(BF16), k_ref[:, sl])
            p = jnp.exp(sc - jnp.max(sc, axis=-1, keepdims=True))
            p = p / jnp.sum(p, axis=-1, keepdims=True)
            o_ref[:, sl] = _dot(p.astype(BF16), v_ref[:, sl]).astype(BF16)

    full = lambda a: pl.BlockSpec(a.shape, lambda i: (0, 0))
    return _pcall(
        body, name=name, grid=(s // tm,),
        in_specs=[pl.BlockSpec((tm, X_W), lambda i: (i, 0)), full(ck_n), full(cv), full(cq_g)],
        out_specs=pl.BlockSpec((tm, X_W), lambda i: (i, 0)),
        out_shape=SDS((s, X_W), BF16), compiler_params=_cp())(cq_raw, ck_n, cv, cq_g)


def _cross_bwd(d_o, cq_raw, ck_n, cv, cq_g, *, name):
    s = cq_raw.shape[0]
    tm = _rows(s)

    def body(do_ref, cq_ref, k_ref, v_ref, g_ref, dcq_ref, dk_ref, dv_ref, dg_ref):
        @pl.when(pl.program_id(0) == 0)
        def _():
            dk_ref[...] = jnp.zeros_like(dk_ref)
            dv_ref[...] = jnp.zeros_like(dv_ref)
            dg_ref[...] = jnp.zeros_like(dg_ref)

        def front(h):
            sl = slice(h * HEAD_DIM, (h + 1) * HEAD_DIM)
            y, r, qn = _cross_q(cq_ref, g_ref, sl)
            qb = qn.astype(BF16)
            do = do_ref[:, sl]
            return y, r, qb, do, _dot_nt(qb, k_ref[:, sl]), _dot_nt(do, v_ref[:, sl])

        dg = jnp.zeros((1, HEAD_DIM), F32)
        nxt = front(0)
        for h in range(ATT_HEADS):
            sl = slice(h * HEAD_DIM, (h + 1) * HEAD_DIM)
            y, r, qb, do, sc, dp = nxt
            if h + 1 < ATT_HEADS:
                nxt = front(h + 1)
            p = jnp.exp(sc - jnp.max(sc, axis=-1, keepdims=True))
            p = p / jnp.sum(p, axis=-1, keepdims=True)
            dv_ref[:, sl] += _dot_tn(p.astype(BF16), do)
            ds = (p * (dp - jnp.sum(dp * p, axis=-1, keepdims=True))).astype(BF16)
            dk_ref[:, sl] += _dot_tn(ds, qb)
            dn = _dot(ds, k_ref[:, sl]) * SCALE
            dy, dgain = _norm_bwd(y, r, g_ref[...], dn)
            dcq_ref[:, sl] = dy.astype(BF16)
            dg = dg + dgain
        dg_ref[...] += dg

    full = lambda a: pl.BlockSpec(a.shape, lambda i: (0, 0))
    row = pl.BlockSpec((tm, X_W), lambda i: (i, 0))
    return _pcall(
        body, name=name, grid=(s // tm,),
        in_specs=[row, row, full(ck_n), full(cv), full(cq_g)],
        out_specs=[row, pl.BlockSpec((N_MEM, X_W), lambda i: (0, 0)), pl.BlockSpec((N_MEM, X_W), lambda i: (0, 0)),
                   pl.BlockSpec((1, HEAD_DIM), lambda i: (0, 0))],
        out_shape=[SDS((s, X_W), BF16), SDS((N_MEM, X_W), F32), SDS((N_MEM, X_W), F32), SDS((1, HEAD_DIM), F32)],
        compiler_params=_cp())(d_o, cq_raw, ck_n, cv, cq_g)


def _mem_bwd(dck_n, dcv, ck_raw, memn, mem, wckv, mem_g, ck_g, *, name):
    def body(dk_ref, dv_ref, ckr_ref, mn_ref, m_ref, w_ref, mg_ref, kg_ref, dw_ref, dmg_ref, dkg_ref, dckv_s):
        dkg = jnp.zeros((1, HEAD_DIM), F32)
        for h in range(ATT_HEADS):
            sl = slice(h * HEAD_DIM, (h + 1) * HEAD_DIM)
            y = ckr_ref[:, sl]
            dy, dgain = _norm_bwd(y, _rms_r(y), kg_ref[...], dk_ref[:, sl])
            dckv_s[:, sl] = dy.astype(BF16)
            dkg = dkg + dgain
        dkg_ref[...] = dkg
        dckv_s[:, X_W:] = dv_ref[...].astype(BF16)
        dckv = dckv_s[...]
        dw_ref[...] = _dot_tn(mn_ref[...], dckv).astype(BF16)
        dmn = _dot_nt(dckv, w_ref[...])
        mv = m_ref[...]
        dmg_ref[...] = jnp.sum(dmn * mv * _rms_r(mv), axis=0, keepdims=True)

    return _pcall(
        body, name=name,
        out_shape=[SDS((D_MODEL, 2 * X_W), BF16), SDS((1, D_MODEL), F32), SDS((1, HEAD_DIM), F32)],
        scratch_shapes=[pltpu.VMEM((N_MEM, 2 * X_W), BF16)],
        compiler_params=_cp())(dck_n, dcv, ck_raw, memn, mem, wckv, mem_g, ck_g)


def _rope_tables(pos):
    inv_freq = ROPE_THETA ** (-jnp.arange(0, ROT_DIM, 2, dtype=F32) / ROT_DIM)
    ang = pos.astype(F32)[:, None] * inv_freq
    cos, sin = jnp.cos(ang), jnp.sin(ang)
    s = pos.shape[0]
    rest = HEAD_DIM - ROT_DIM
    cos_t = jnp.concatenate([cos, cos, jnp.ones((s, rest), F32)], axis=1)
    sin_t = jnp.concatenate([-sin, sin, jnp.zeros((s, rest), F32)], axis=1)
    return cos_t, sin_t


def _local_step(x, mem, pos, tgt, w_in, shards, sm):
    cos_t, sin_t = _rope_tables(pos)
    wb = {"w_in": w_in}

    pick = lambda *names: {a: shards[a] for a in names}
    proj, xn, got = _norm_linear(x, sm["mix_norm_g"], wb["w_in"], cn=768, out_dtype=F32, name="fwd_in_proj",
                                 shards=pick("w_out", "w_cq", "w_ckv", "w_co"))
    wb.update(got)
    q_rot, k_rot, v = _qk_prep(proj, cos_t, sin_t, sm["q_norm_g"], sm["k_norm_g"], name="fwd_qk_prep")
    mix, lse_all, got = _attn_fwd(q_rot, k_rot, v, pick("w_down", "w_gate_up"), name="fwd_attn")
    wb.update(got)
    mix = _pool_fwd(proj, mix, sm["pool_w"], sm["pool_scale"], name="fwd_pool")
    h1 = _linear_res(mix, wb["w_out"], x, cn=512, name="fwd_out_proj")
    cq_raw, hn = _norm_linear(h1, sm["cross_norm_g"], wb["w_cq"], cn=512, out_dtype=F32, name="fwd_cq_proj")
    memn, ck_raw, ck_n, cv = _mem_fwd(mem, sm["mem_norm_g"], wb["w_ckv"], sm["ck_norm_g"], name="fwd_mem")
    co = _cross_fwd(cq_raw, ck_n, cv, sm["cq_norm_g"], name="fwd_cross")
    h2 = _linear_res(co, wb["w_co"], h1, cn=512, name="fwd_co_proj")
    gu, act, fn = _norm_linear_swiglu(h2, sm["ffn_norm_g"], wb["w_gate_up"], name="fwd_gate_up")
    dy, dyb, sq = _linear_res_loss(act, wb["w_down"], h2, tgt, name="fwd_down_loss")

    gw = {}
    gs = {}
    dgu = _swiglu_bwd(dyb, wb["w_down"], gu, name="bwd_swiglu")
    gw_down = _dw_tn(act, dyb, tkw=1024, tn=1024, name="bwd_dw_down")
    dh2, dh2b, gs["ffn_norm_g"], parts = _linear_nt_normbwd(dgu, wb["w_gate_up"], h2, dy, sm["ffn_norm_g"],
                                                            name="bwd_ffn_in", grads={"w_down": gw_down})
    gw["w_gate_up"] = _dw_tn(fn, dgu, tkw=1024, tn=1536, name="bwd_dw_gate_up")

    d_co = _linear_nt(dh2b, wb["w_co"], cn=512, name="bwd_co_proj")
    gw["w_co"] = _dw_tn(co, dh2b, tkw=512, tn=1024, name="bwd_dw_co")
    dcq, dck_n, dcv, gs["cq_norm_g"] = _cross_bwd(d_co, cq_raw, ck_n, cv, sm["cq_norm_g"], name="bwd_cross")
    gw["w_ckv"], gs["mem_norm_g"], gs["ck_norm_g"] = _mem_bwd(dck_n, dcv, ck_raw, memn, mem, wb["w_ckv"],
                                                             sm["mem_norm_g"], sm["ck_norm_g"], name="bwd_mem")
    dh1, dh1b, gs["cross_norm_g"] = _linear_nt_normbwd(dcq, wb["w_cq"], h1, dh2, sm["cross_norm_g"],
                                                       name="bwd_cq_in")
    gw["w_cq"] = _dw_tn(hn, dcq, tkw=1024, tn=512, name="bwd_dw_cq")

    dmix = _linear_nt(dh1b, wb["w_out"], cn=512, name="bwd_out_proj")
    gw["w_out"] = _dw_tn(mix, dh1b, tkw=1024, tn=1024, name="bwd_dw_out")
    du, gs["pool_w"], gs["pool_scale"] = _pool_bwd(proj, dmix, sm["pool_w"], sm["pool_scale"], name="bwd_pool")
    dq, dk, dv, got = _attn_bwd(q_rot, k_rot, v, mix, dmix, lse_all, gw, name="bwd_attn")
    parts.update(got)
    dproj, gs["q_norm_g"], gs["k_norm_g"] = _qk_prep_bwd(proj, dq, dk, dv, du, cos_t, sin_t, sm["q_norm_g"],
                                                         sm["k_norm_g"], name="bwd_qk_prep")
    gw_in = _dw_tn(xn, dproj, tkw=1024, tn=1536, name="bwd_dw_in")
    dx, _, gs["mix_norm_g"], last = _linear_nt_normbwd(dproj, wb["w_in"], x, dh1, sm["mix_norm_g"],
                                                       name="bwd_in_proj", grads={"w_in": gw_in})
    parts.update(last)
    return sq, dx, parts, gs


SHARDED = ("w_in", "w_out", "w_cq", "w_ckv", "w_co", "w_gate_up", "w_down")
SMALL = ("mix_norm_g", "q_norm_g", "k_norm_g", "pool_w", "pool_scale", "cross_norm_g", "mem_norm_g", "cq_norm_g",
         "ck_norm_g", "ffn_norm_g")
FULL = {
    "w_in": ((D_MODEL, IN_W), 1, IN_W // N_DEV),
    "w_out": ((D_MODEL, D_MODEL), 0, D_MODEL // N_DEV),
    "w_cq": ((D_MODEL, X_W), 0, D_MODEL // N_DEV),
    "w_ckv": ((D_MODEL, 2 * X_W), 0, D_MODEL // N_DEV),
    "w_co": ((X_W, D_MODEL), 1, D_MODEL // N_DEV),
    "w_gate_up": ((D_MODEL, 2 * FF_PAD), 1, FF_TILE),
    "w_down": ((FF_PAD, D_MODEL), 0, DOWN_SHARD),
}


def _shard_shape(name):
    shape, axis, width = FULL[name]
    return tuple(width if a == axis else n for a, n in enumerate(shape))


def _window(ref, name, dev):
    _, axis, width = FULL[name]
    if name == "w_down":
        start = pl.multiple_of((dev // 2) * FF_TILE + (dev % 2) * DOWN_SHARD, HALO)
    else:
        start = pl.multiple_of(dev * width, BLOCK)
    return ref.at[pl.ds(start, width), :] if axis == 0 else ref.at[:, pl.ds(start, width)]


def _mesh_place():
    x, y, c = lax.axis_index("x"), lax.axis_index("y"), lax.axis_index("c")
    return x, y, c, 4 * x + 2 * y + c


def _peer(x, y, c, k):
    px = 1 - x if k & 4 else x
    py = 1 - y if k & 2 else y
    pc = 1 - c if k & 1 else c
    return (px, py, pc), 4 * px + 2 * py + pc


HBM_SPEC = pl.BlockSpec(memory_space=pltpu.HBM)


def _comm_sems(n):
    return [pltpu.SemaphoreType.DMA((n, N_DEV - 1)), pltpu.SemaphoreType.DMA((n, N_DEV - 1)),
            pltpu.SemaphoreType.DMA((n,))]


DIRECT = (1, 2, 4, 6)
RELAYED = (2, 4, 6)


def _relay_step(steps):
    return max(1, (steps * 11) // 16) if steps > 1 else 0


def _gather_copies(names, ins, outs, send_sems, recv_sems, local_sems):
    x, y, c, me = _mesh_place()
    sibling, _ = _peer(x, y, c, 1)
    local, direct, relays, recv = [], [], {}, {}
    for wi, name in enumerate(names):
        def sems(k, to):
            return dict(send_sem=send_sems.at[wi, k - 1], recv_sem=recv_sems.at[wi, k - 1], device_id=to,
                        device_id_type=MESH)
        local.append(pltpu.make_async_copy(ins[wi], _window(outs[wi], name, me), local_sems.at[wi]))
        for k in range(1, N_DEV):
            peer, pidx = _peer(x, y, c, k)
            win = _window(outs[wi], name, pidx)
            recv[wi, k] = pltpu.make_async_remote_copy(src_ref=ins[wi], dst_ref=win, **sems(k, peer))
            if k in DIRECT:
                direct.append(pltpu.make_async_remote_copy(src_ref=ins[wi], dst_ref=_window(outs[wi], name, me),
                                                           **sems(k, peer)))
            if k in RELAYED:
                relays[wi, k] = pltpu.make_async_remote_copy(src_ref=win, dst_ref=win, **sems(k + 1, sibling))
    return local, direct, relays, recv


def _exchange_copies(names, ins, outs, send_sems, recv_sems, local_sems):
    x, y, c, me = _mesh_place()
    local, sent, recv = [], [], []
    for wi, name in enumerate(names):
        local.append(pltpu.make_async_copy(_window(ins[wi], name, me), outs[wi].at[0], local_sems.at[wi]))
        for k in range(1, N_DEV):
            peer, pidx = _peer(x, y, c, k)
            sems = dict(send_sem=send_sems.at[wi, k - 1], recv_sem=recv_sems.at[wi, k - 1], device_id=peer,
                        device_id_type=MESH)
            sent.append(pltpu.make_async_remote_copy(src_ref=_window(ins[wi], name, pidx), dst_ref=outs[wi].at[k], **sems))
            recv.append(pltpu.make_async_remote_copy(src_ref=_window(ins[wi], name, me), dst_ref=outs[wi].at[k], **sems))
    return local, sent, recv


def _down_pads(down_ref, zero_ref, zero_sems):
    return [pltpu.make_async_copy(zero_ref, down_ref.at[pl.ds(t * FF_TILE + FF_SHARD, FF_TILE - FF_SHARD), :],
                                  zero_sems.at[t]) for t in range(FF_PAD // FF_TILE)]


def _gather_scratch(names):
    if not names:
        return []
    pad = [pltpu.VMEM((FF_TILE - FF_SHARD, D_MODEL), BF16), pltpu.SemaphoreType.DMA((FF_PAD // FF_TILE,))]
    return _comm_sems(len(names)) + (pad if "w_down" in names else [])


def _gather_ops(names, ins, outs, scratch):
    local, direct, relays, recv = _gather_copies(names, ins, outs, *scratch[:3])
    pads = _down_pads(outs[names.index("w_down")], scratch[3], scratch[4]) if "w_down" in names else []

    def start():
        if pads:
            scratch[3][...] = jnp.zeros_like(scratch[3])
        for cp in local + direct + pads:
            cp.start()

    def relay(only=None):
        for (wi, k), cp in relays.items():
            if only is None or names[wi] in only:
                recv[wi, k].wait_recv()
                cp.start()

    def finish():
        for (wi, k), cp in recv.items():
            if k not in RELAYED:
                cp.wait_recv()
        for cp in direct + list(relays.values()):
            cp.wait_send()
        for cp in local + pads:
            cp.wait()

    return start, relay, finish


def _start(copies):
    local, sent, _ = copies
    for cp in local + sent:
        cp.start()


def _finish(copies):
    local, sent, recv = copies
    for cp in recv:
        cp.wait_recv()
    for cp in sent:
        cp.wait_send()
    for cp in local:
        cp.wait()


def _gather_weights(shards):
    names = tuple(shards)
    nw = len(names)

    def body(*refs):
        start, relay, finish = _gather_ops(names, refs[:nw], refs[nw:2 * nw], refs[2 * nw:])
        start()
        relay()
        finish()

    outs = _pcall(
        body, name="gather_weights",
        in_specs=[HBM_SPEC] * nw, out_specs=[HBM_SPEC] * nw,
        out_shape=[SDS(FULL[n][0], BF16) for n in names],
        scratch_shapes=_comm_sems(nw))(*[shards[n] for n in names])
    return dict(zip(names, outs))


def _exchange_small(blocks):
    nb = len(blocks)

    def body(*refs):
        ins, outs = refs[:nb], refs[nb:2 * nb]
        send_sems, recv_sems, local_sems = refs[2 * nb:]
        x, y, c, me = _mesh_place()
        local, sent, recv = [], [], []
        for bi in range(nb):
            local.append(pltpu.make_async_copy(ins[bi], outs[bi].at[me], local_sems.at[bi]))
            for k in range(1, N_DEV):
                peer, pidx = _peer(x, y, c, k)
                sems = dict(send_sem=send_sems.at[bi, k - 1], recv_sem=recv_sems.at[bi, k - 1], device_id=peer,
                            device_id_type=MESH)
                sent.append(pltpu.make_async_remote_copy(src_ref=ins[bi], dst_ref=outs[bi].at[me], **sems))
                recv.append(pltpu.make_async_remote_copy(src_ref=ins[bi], dst_ref=outs[bi].at[pidx], **sems))
        _start((local, sent, recv))
        _finish((local, sent, recv))

    return _pcall(
        body, name="exchange_small", in_specs=[HBM_SPEC] * nb, out_specs=[HBM_SPEC] * nb,
        out_shape=[SDS((N_DEV,) + a.shape, F32) for a in blocks], scratch_shapes=_comm_sems(nb))(*blocks)


def _adam_math(g, w, m, v):
    m_new = ADAM_B1 * m + (1.0 - ADAM_B1) * g
    v_new = ADAM_B2 * v + (1.0 - ADAM_B2) * (g * g)
    m_hat = m_new / (1.0 - ADAM_B1 ** ADAM_STEP)
    v_hat = v_new / (1.0 - ADAM_B2 ** ADAM_STEP)
    return -ADAM_LR * (m_hat / (jnp.sqrt(v_hat) + ADAM_EPS) + ADAM_WD * w), m_new, v_new


def _adamw_small(parts, w, m, v, sq_parts, *, name):
    n = len(parts)

    def body(*refs):
        p_refs, w_refs, m_refs, v_refs = refs[:n], refs[n:2 * n], refs[2 * n:3 * n], refs[3 * n:4 * n]
        sq_ref, outs = refs[4 * n], refs[4 * n + 1:]
        for i in range(n):
            g = p_refs[i][0]
            for k in range(1, N_DEV):
                g = g + p_refs[i][k]
            delta, m_new, v_new = _adam_math(g, w_refs[i][...], m_refs[i][...], v_refs[i][...])
            outs[4 * i][...] = g
            outs[4 * i + 1][...] = delta
            outs[4 * i + 2][...] = m_new
            outs[4 * i + 3][...] = v_new
        tot = sq_ref[0]
        for k in range(1, N_DEV):
            tot = tot + sq_ref[k]
        outs[4 * n][...] = (0.5 / D_MODEL) * jnp.sum(tot, axis=1, keepdims=True)

    out_shape = [SDS(a.shape, F32) for a in w for _ in range(4)] + [SDS((1, 1), F32)]
    outs = _pcall(body, name=name, out_shape=out_shape, compiler_params=_cp())(*parts, *w, *m, *v, sq_parts)
    return [outs[4 * i:4 * i + 4] for i in range(n)], outs[4 * n][0, 0]


def _adamw(parts, w, m, v, *, name):
    r, c = w.shape
    tr = r
    for cand in (256, 128, 88):
        if r % cand == 0:
            tr = cand
            break

    def body(p_ref, w_ref, m_ref, v_ref, g_ref, d_ref, mo_ref, vo_ref):
        g = p_ref[0].astype(F32)
        for k in range(1, N_DEV):
            g = g + p_ref[k].astype(F32)
        g_ref[...] = g
        d_ref[...], mo_ref[...], vo_ref[...] = _adam_math(g, w_ref[...], m_ref[...], v_ref[...])

    row = pl.BlockSpec((tr, c), lambda i: (i, 0))
    return _pcall(
        body, name=name, grid=(r // tr,),
        in_specs=[pl.BlockSpec((N_DEV, tr, c), lambda i: (0, i, 0)), row, row, row],
        out_specs=[row] * 4, out_shape=[SDS((r, c), F32)] * 4, compiler_params=_cp())(parts, w, m, v)


def _pad_cols(a, width):
    return jnp.pad(a, ((0, 0), (0, width - a.shape[1])))


def kernel(x, mem, positions, mix_norm_g, w_in, q_norm_g, k_norm_g, pool_w, pool_scale, w_out, cross_norm_g, mem_norm_g, w_cq, w_ckv, cq_norm_g, ck_norm_g, w_co, ffn_norm_g, w_gate_up, w_down, loss_target, m_mix_norm_g, m_w_in, m_q_norm_g, m_k_norm_g, m_pool_w, m_pool_scale, m_w_out, m_cross_norm_g, m_mem_norm_g, m_w_cq, m_w_ckv, m_cq_norm_g, m_ck_norm_g, m_w_co, m_ffn_norm_g, m_w_gate_up, m_w_down, v_mix_norm_g, v_w_in, v_q_norm_g, v_k_norm_g, v_pool_w, v_pool_scale, v_w_out, v_cross_norm_g, v_mem_norm_g, v_w_cq, v_w_ckv, v_cq_norm_g, v_ck_norm_g, v_w_co, v_ffn_norm_g, v_w_gate_up, v_w_down):
    given = dict(locals())
    w_f32 = {n: given[n][0] for n in SHARDED + SMALL}
    m_f32 = {n: given["m_" + n][0] for n in SHARDED + SMALL}
    v_f32 = {n: given["v_" + n][0] for n in SHARDED + SMALL}
    for d in (w_f32, m_f32, v_f32):
        d["w_gate_up"] = _pad_cols(d["w_gate_up"], FF_TILE)

    shards = {n: w_f32[n].astype(BF16) for n in SHARDED}
    w_in_full = _gather_weights({"w_in": shards.pop("w_in")})["w_in"]
    sm_rows = {n: (w_f32[n] if w_f32[n].ndim == 3 else w_f32[n].reshape(1, -1)) for n in SMALL}
    sq, dx, parts, gs = _local_step(x[0], mem[0], positions[0], loss_target[0], w_in_full, shards, sm_rows)

    flat = lambda a: a.reshape(-1, a.shape[-1])
    got = _exchange_small([flat(gs[n]) for n in SMALL] + [sq])
    small_res, loss = _adamw_small(got[:-1], [flat(sm_rows[n]) for n in SMALL],
                                   [flat(m_f32[n].reshape(sm_rows[n].shape)) for n in SMALL],
                                   [flat(v_f32[n].reshape(sm_rows[n].shape)) for n in SMALL], got[-1],
                                   name="adamw_small")
    res = {n: [a.reshape(w_f32[n].shape) for a in small_res[i]] for i, n in enumerate(SMALL)}
    for n in SHARDED:
        res[n] = _adamw(parts[n], w_f32[n], m_f32[n], v_f32[n], name="adamw_" + n)
    res["w_gate_up"] = [a[:, :FF_SHARD] for a in res["w_gate_up"]]
    order = ("mix_norm_g", "w_in", "q_norm_g", "k_norm_g", "pool_w", "pool_scale", "w_out", "cross_norm_g",
             "mem_norm_g", "w_cq", "w_ckv", "cq_norm_g", "ck_norm_g", "w_co", "ffn_norm_g", "w_gate_up", "w_down")
    outs = [loss, dx[None]]
    for which in range(4):
        outs += [res[n][which][None] for n in order]
    return tuple(outs)
```

```python
import functools

import jax
import jax.numpy as jnp
from jax import lax
from jax.experimental import pallas as pl
from jax.experimental.pallas import tpu as pltpu

F32 = jnp.float32
BF16 = jnp.bfloat16
SDS = jax.ShapeDtypeStruct

D_MODEL = 1024
HEAD_DIM = 128
N_GROUPS = 3
DILATIONS = (1, 4, 16)
ATT_HEADS = 4
Q_W = 1536
KV_W = 512
POOL_W = 512
POOL_WINDOWS = (2, 4, 8, 16)
IN_W = 3072
X_W = 512
N_MEM = 256
D_FF = 2816
FF_TILE = 768
FF_SHARD = 704
FF_PAD = 4 * FF_TILE
DOWN_SHARD = 352
ROT_DIM = 32
ROT_HALF = 16
ROPE_THETA = 500000.0
EPS = 1e-6
NEG_INF = -1e30
SCALE = HEAD_DIM ** -0.5
BLOCK = 128
HALO = 16

ADAM_LR = 0.001
ADAM_B1 = 0.9
ADAM_B2 = 0.999
ADAM_EPS = 1e-08
ADAM_WD = 0.01
ADAM_STEP = 10

N_DEV = 8
VMEM_LIMIT_BYTES = 56 * 1024 * 1024
MESH = pl.DeviceIdType.MESH


def _pcall(body, **kw):
    return pl.pallas_call(body, **kw)


def _cp():
    return pltpu.CompilerParams(vmem_limit_bytes=VMEM_LIMIT_BYTES)


def _dot(a, b):
    return lax.dot_general(a, b, (((1,), (0,)), ((), ())), preferred_element_type=F32)


def _dot_nt(a, b):
    return lax.dot_general(a, b, (((1,), (1,)), ((), ())), preferred_element_type=F32)


def _dot_tn(a, b):
    return lax.dot_general(a, b, (((0,), (0,)), ((), ())), preferred_element_type=F32)


def _rows(s):
    return min(512, s)


def _rms_r(x):
    return lax.rsqrt(jnp.mean(x * x, axis=-1, keepdims=True) + EPS)


def _norm_bwd(x, r, gain, dxn):
    z = dxn * gain
    dx = r * z - x * (r * r * r * jnp.mean(z * x, axis=-1, keepdims=True))
    dgain = jnp.sum(dxn * x * r, axis=0, keepdims=True)
    return dx, dgain


def _split_bf16(t):
    hi = t.astype(BF16)
    return hi, (t - hi.astype(F32)).astype(BF16)


def _lane_sums(t, ones):
    hi, lo = _split_bf16(t)
    return _dot(hi, ones) + _dot(lo, ones)


def _head_r(y, ones):
    return lax.rsqrt(_lane_sums(y * y, ones) * (1.0 / HEAD_DIM) + EPS)


def _head_norm_bwd(y, r, gain, dn, ones):
    z = dn * gain
    dy = r * z - y * (r * r * r * (_dot((z * y).astype(BF16), ones) * (1.0 / HEAD_DIM)))
    return dy, jnp.sum(dn * y * r, axis=0, keepdims=True)


def _swap_matrix():
    src = lax.broadcasted_iota(jnp.int32, (HEAD_DIM, HEAD_DIM), 0)
    dst = lax.broadcasted_iota(jnp.int32, (HEAD_DIM, HEAD_DIM), 1)
    hit = ((dst < ROT_HALF) & (src == dst + ROT_HALF)) | ((dst >= ROT_HALF) & (dst < ROT_DIM) & (src == dst - ROT_HALF))
    return jnp.where(hit, 1.0, 0.0).astype(BF16)


def _partner(t, swap):
    hi, lo = _split_bf16(t)
    return _dot(hi, swap) + _dot(lo, swap)


def _rope(n, cos_t, sin_t, swap):
    return n * cos_t + _partner(n, swap) * sin_t


def _rope_bwd(d, cos_t, sin_t, swap):
    return d * cos_t + _dot((d * sin_t).astype(BF16), swap)


def _resident(shape):
    return pl.BlockSpec(shape, lambda i: (0,) * len(shape), pipeline_mode=pl.Buffered(1))


def _chunks(n, cn):
    return [slice(j * cn, (j + 1) * cn) for j in range(n // cn)]


def _norm_linear(x, gain, w, *, cn, out_dtype, name, shards=None):
    s, k = x.shape
    n = w.shape[1]
    tm = _rows(s)
    names = tuple(shards or ())
    nr = len(names)

    def body(*refs):
        x_ref, g_ref, w_ref = refs[:3]
        y_ref, xn_ref = refs[3 + nr:5 + nr]
        if nr:
            start, relay, finish = _gather_ops(names, refs[3:3 + nr], refs[5 + nr:5 + 2 * nr], refs[5 + 2 * nr:])
            pl.when(pl.program_id(0) == 0)(start)
            pl.when(pl.program_id(0) == _relay_step(s // tm))(relay)
        xv = x_ref[...]
        xn_ref[...] = (xv * _rms_r(xv) * g_ref[...]).astype(BF16)
        for c in _chunks(n, cn):
            y_ref[:, c] = _dot(xn_ref[...], w_ref[:, c]).astype(out_dtype)
        if nr:
            pl.when(pl.program_id(0) == s // tm - 1)(finish)

    row = lambda w_: pl.BlockSpec((tm, w_), lambda i: (i, 0))
    outs = _pcall(
        body, name=name, grid=(s // tm,),
        in_specs=[row(k), _resident((1, k)), _resident((k, n))] + [HBM_SPEC] * nr,
        out_specs=[row(n), row(k)] + [HBM_SPEC] * nr,
        out_shape=[SDS((s, n), out_dtype), SDS((s, k), BF16)] + [SDS(FULL[a][0], BF16) for a in names],
        scratch_shapes=_gather_scratch(names),
        compiler_params=_cp())(x, gain, w, *[shards[a] for a in names])
    return (outs[0], outs[1], dict(zip(names, outs[2:]))) if nr else tuple(outs)


def _norm_linear_swiglu(x, gain, wgu, *, name):
    s, k = x.shape
    tm = _rows(s)

    def body(x_ref, g_ref, w_ref, gu_ref, a_ref, xn_ref):
        xv = x_ref[...]
        xn_ref[...] = (xv * _rms_r(xv) * g_ref[...]).astype(BF16)
        for c in _chunks(FF_PAD, FF_TILE):
            g = _dot(xn_ref[...], w_ref[:, c])
            u = _dot(xn_ref[...], w_ref[:, slice(FF_PAD + c.start, FF_PAD + c.stop)])
            a_ref[:, c] = (g * jax.nn.sigmoid(g) * u).astype(BF16)
            gu_ref[0, :, c] = g.astype(BF16)
            gu_ref[1, :, c] = u.astype(BF16)

    row = lambda w_: pl.BlockSpec((tm, w_), lambda i: (i, 0))
    return _pcall(
        body, name=name, grid=(s // tm,),
        in_specs=[row(k), _resident((1, k)), _resident((k, 2 * FF_PAD))],
        out_specs=[pl.BlockSpec((2, tm, FF_PAD), lambda i: (0, i, 0)), row(FF_PAD), row(k)],
        out_shape=[SDS((2, s, FF_PAD), BF16), SDS((s, FF_PAD), BF16), SDS((s, k), BF16)],
        compiler_params=_cp())(x, gain, wgu)


def _linear_res(a, w, res, *, cn, name):
    s, k = a.shape
    n = w.shape[1]
    tm = _rows(s)

    def body(a_ref, w_ref, r_ref, y_ref):
        for c in _chunks(n, cn):
            y_ref[:, c] = r_ref[:, c] + _dot(a_ref[...], w_ref[:, c])

    row = lambda w_: pl.BlockSpec((tm, w_), lambda i: (i, 0))
    return _pcall(
        body, name=name, grid=(s // tm,),
        in_specs=[row(k), _resident((k, n)), row(n)], out_specs=row(n),
        out_shape=SDS((s, n), F32), compiler_params=_cp())(a, w, res)


def _linear_res_loss(a, w, res, tgt, *, name):
    s, k = a.shape
    n = w.shape[1]
    tm = _rows(s)

    def body(a_ref, w_ref, r_ref, t_ref, dy_ref, dyb_ref, sq_ref):
        e = r_ref[...] + _dot(a_ref[...], w_ref[...]) - t_ref[...]
        dy = e * (1.0 / n)
        dy_ref[...] = dy
        dyb_ref[...] = dy.astype(BF16)

        @pl.when(pl.program_id(0) == 0)
        def _():
            sq_ref[...] = jnp.zeros_like(sq_ref)
        sq_ref[...] += jnp.sum(e * e, axis=0, keepdims=True)

    row = lambda w_: pl.BlockSpec((tm, w_), lambda i: (i, 0))
    return _pcall(
        body, name=name, grid=(s // tm,),
        in_specs=[row(k), _resident((k, n)), row(n), row(n)],
        out_specs=[row(n), row(n), pl.BlockSpec((1, n), lambda i: (0, 0))],
        out_shape=[SDS((s, n), F32), SDS((s, n), BF16), SDS((1, n), F32)],
        compiler_params=_cp())(a, w, res, tgt)


def _linear_nt(g, w, *, cn, name):
    s, k = g.shape
    n = w.shape[0]
    tm = _rows(s)

    def body(g_ref, w_ref, y_ref):
        for c in _chunks(n, cn):
            y_ref[:, c] = _dot_nt(g_ref[...], w_ref[c, :]).astype(BF16)

    row = lambda w_: pl.BlockSpec((tm, w_), lambda i: (i, 0))
    return _pcall(
        body, name=name, grid=(s // tm,),
        in_specs=[row(k), _resident((n, k))], out_specs=row(n),
        out_shape=SDS((s, n), BF16), compiler_params=_cp())(g, w)


def _swiglu_bwd(dyb, wd, gu, *, name):
    s, n = dyb.shape
    tm = _rows(s)

    def body(dy_ref, wd_ref, gu_ref, dgu_ref):
        for c in _chunks(FF_PAD, FF_TILE):
            da = _dot_nt(dy_ref[...], wd_ref[c, :])
            g = gu_ref[0, :, c].astype(F32)
            u = gu_ref[1, :, c].astype(F32)
            sg = jax.nn.sigmoid(g)
            dgu_ref[0, :, c] = (da * u * (sg * (1.0 + g * (1.0 - sg)))).astype(BF16)
            dgu_ref[1, :, c] = (da * (g * sg)).astype(BF16)

    half = pl.BlockSpec((2, tm, FF_PAD), lambda i: (0, i, 0))
    return _pcall(
        body, name=name, grid=(s // tm,),
        in_specs=[pl.BlockSpec((tm, n), lambda i: (i, 0)), _resident((FF_PAD, n)), half],
        out_specs=half, out_shape=SDS((2, s, FF_PAD), BF16), compiler_params=_cp())(dyb, wd, gu)


def _linear_nt_normbwd(g, w, x, dres, gain, *, name, grads=None):
    d, k = w.shape
    s = x.shape[0]
    tm = _rows(s)
    names = tuple(grads or ())
    nr = len(names)

    def body(*refs):
        g_ref, w_ref, x_ref, dr_ref, gn_ref = refs[:5]
        dx_ref, dxb_ref, dg_ref = refs[5 + nr:8 + nr]
        if nr:
            copies = _exchange_copies(names, refs[5:5 + nr], refs[8 + nr:8 + 2 * nr], *refs[8 + 2 * nr:])

        @pl.when(pl.program_id(0) == 0)
        def _():
            dg_ref[...] = jnp.zeros_like(dg_ref)
            if nr:
                _start(copies)

        if g.ndim == 3:
            dxn = _dot_nt(g_ref[0], w_ref[:, :k // 2]) + _dot_nt(g_ref[1], w_ref[:, k // 2:])
        else:
            dxn = _dot_nt(g_ref[...], w_ref[...])
        xv = x_ref[...]
        dx, dgain = _norm_bwd(xv, _rms_r(xv), gn_ref[...], dxn)
        out = dr_ref[...] + dx
        dx_ref[...] = out
        dxb_ref[...] = out.astype(BF16)
        dg_ref[...] += dgain

        if nr:
            @pl.when(pl.program_id(0) == s // tm - 1)
            def _():
                _finish(copies)

    row = pl.BlockSpec((tm, d), lambda i: (i, 0))
    g_spec = (pl.BlockSpec((2, tm, k // 2), lambda i: (0, i, 0)) if g.ndim == 3
              else pl.BlockSpec((tm, k), lambda i: (i, 0)))
    outs = _pcall(
        body, name=name, grid=(s // tm,),
        in_specs=[g_spec, _resident((d, k)), row, row, _resident((1, d))] + [HBM_SPEC] * nr,
        out_specs=[row, row, pl.BlockSpec((1, d), lambda i: (0, 0))] + [HBM_SPEC] * nr,
        out_shape=[SDS((s, d), F32), SDS((s, d), BF16), SDS((1, d), F32)]
        + [SDS((N_DEV,) + _shard_shape(n), BF16) for n in names],
        scratch_shapes=_comm_sems(nr) if nr else [],
        compiler_params=_cp())(g, w, x, dres, gain, *[grads[n] for n in names])
    return (outs[0], outs[1], outs[2], dict(zip(names, outs[3:]))) if nr else tuple(outs)


def _dw_tn(x, g, *, tkw, tn, name):
    s, kw = x.shape
    halves = g.ndim == 3
    n = 2 * g.shape[2] if halves else g.shape[1]
    ts = min(2048, s)
    ns = s // ts
    per_half = n // 2 // tn

    def body(x_ref, g_ref, o_ref, acc_ref):
        ss = pl.program_id(2)

        @pl.when(ss == 0)
        def _():
            acc_ref[...] = jnp.zeros_like(acc_ref)

        acc_ref[...] += _dot_tn(x_ref[...], g_ref[...])

        @pl.when(ss == ns - 1)
        def _():
            o_ref[...] = acc_ref[...].astype(BF16)

    g_spec = (pl.BlockSpec((None, ts, tn), lambda a, b, ss: (b // per_half, ss, b % per_half)) if halves
              else pl.BlockSpec((ts, tn), lambda a, b, ss: (ss, b)))
    return _pcall(
        body, name=name, grid=(kw // tkw, n // tn, ns),
        in_specs=[pl.BlockSpec((ts, tkw), lambda a, b, ss: (ss, a)), g_spec],
        out_specs=pl.BlockSpec((tkw, tn), lambda a, b, ss: (a, b)),
        out_shape=SDS((kw, n), BF16),
        scratch_shapes=[pltpu.VMEM((tkw, tn), F32)], compiler_params=_cp())(x, g)


def _qk_prep(proj, cos_t, sin_t, qg, kg, *, name):
    s = proj.shape[0]
    tm = _rows(s)
    nqh = Q_W // HEAD_DIM

    def body(q_ref, k_ref, v_ref, c_ref, s_ref, qg_ref, kg_ref, qo_ref, ko_ref, vo_ref):
        c, sn = c_ref[...], s_ref[...]
        ones, swap = jnp.ones((HEAD_DIM, HEAD_DIM), BF16), _swap_matrix()
        for h in range(nqh):
            sl = slice(h * HEAD_DIM, (h + 1) * HEAD_DIM)
            y = q_ref[:, sl]
            qo_ref[:, sl] = (_rope(y * _head_r(y, ones) * qg_ref[...], c, sn, swap) * SCALE).astype(BF16)
        for h in range(ATT_HEADS):
            sl = slice(h * HEAD_DIM, (h + 1) * HEAD_DIM)
            y = k_ref[:, sl]
            ko_ref[:, sl] = _rope(y * _head_r(y, ones) * kg_ref[...], c, sn, swap).astype(BF16)
        vo_ref[...] = v_ref[...].astype(BF16)

    row = lambda w, j: pl.BlockSpec((tm, w), lambda i: (i, j))
    one = pl.BlockSpec((1, HEAD_DIM), lambda i: (0, 0))
    return _pcall(
        body, name=name, grid=(s // tm,),
        in_specs=[row(Q_W, 0), row(KV_W, 3), row(KV_W, 4), row(HEAD_DIM, 0), row(HEAD_DIM, 0), one, one],
        out_specs=[row(Q_W, 0), row(KV_W, 0), row(KV_W, 0)],
        out_shape=[SDS((s, Q_W), BF16), SDS((s, KV_W), BF16), SDS((s, KV_W), BF16)],
        compiler_params=_cp())(proj, proj, proj, cos_t, sin_t, qg, kg)


def _qk_prep_bwd(proj, dq, dk, dv, du, cos_t, sin_t, qg, kg, *, name):
    s = proj.shape[0]
    tm = _rows(s)

    def body(q_ref, k_ref, dq_ref, dk_ref, dv_ref, du_ref, c_ref, s_ref, qg_ref, kg_ref, dp_ref, dqg_ref, dkg_ref):
        c, sn = c_ref[...], s_ref[...]

        @pl.when(pl.program_id(0) == 0)
        def _():
            dqg_ref[...] = jnp.zeros_like(dqg_ref)
            dkg_ref[...] = jnp.zeros_like(dkg_ref)

        ones, swap = jnp.ones((HEAD_DIM, HEAD_DIM), BF16), _swap_matrix()
        dqg = jnp.zeros((1, HEAD_DIM), F32)
        for g in range(N_GROUPS):
            for h in range(ATT_HEADS):
                sl = slice(h * HEAD_DIM, (h + 1) * HEAD_DIM)
                col = slice(g * KV_W + h * HEAD_DIM, g * KV_W + (h + 1) * HEAD_DIM)
                y = q_ref[:, col]
                dn = _rope_bwd(dq_ref[g, :, sl].astype(F32) * SCALE, c, sn, swap)
                dy, dgain = _head_norm_bwd(y, _head_r(y, ones), qg_ref[...], dn, ones)
                dp_ref[:, col] = dy.astype(BF16)
                dqg = dqg + dgain
        dqg_ref[...] += dqg

        dkg = jnp.zeros((1, HEAD_DIM), F32)
        for h in range(ATT_HEADS):
            sl = slice(h * HEAD_DIM, (h + 1) * HEAD_DIM)
            y = k_ref[:, sl]
            dn = _rope_bwd(dk_ref[:, sl], c, sn, swap)
            dy, dgain = _head_norm_bwd(y, _head_r(y, ones), kg_ref[...], dn, ones)
            dp_ref[:, Q_W + h * HEAD_DIM:Q_W + (h + 1) * HEAD_DIM] = dy.astype(BF16)
            dkg = dkg + dgain
        dkg_ref[...] += dkg

        dp_ref[:, Q_W + KV_W:Q_W + 2 * KV_W] = dv_ref[...].astype(BF16)
        dp_ref[:, Q_W + 2 * KV_W:] = du_ref[...].astype(BF16)

    row = lambda w, j: pl.BlockSpec((tm, w), lambda i: (i, j))
    one = pl.BlockSpec((1, HEAD_DIM), lambda i: (0, 0))
    return _pcall(
        body, name=name, grid=(s // tm,),
        in_specs=[row(Q_W, 0), row(KV_W, 3), pl.BlockSpec((N_GROUPS, tm, KV_W), lambda i: (0, i, 0))]
        + [row(KV_W, 0)] * 3 + [row(HEAD_DIM, 0), row(HEAD_DIM, 0), one, one],
        out_specs=[row(IN_W, 0), one, one],
        out_shape=[SDS((s, IN_W), BF16), SDS((1, HEAD_DIM), F32), SDS((1, HEAD_DIM), F32)],
        compiler_params=_cp())(proj, proj, dq, dk, dv, du, cos_t, sin_t, qg, kg)


ATT_ROWS = 16 * BLOCK


def _sub(ref, start, d, size=BLOCK):
    return ref[pl.ds(start, size, stride=d), :] if d > 1 else ref[pl.ds(start, size), :]


def _sub_set(ref, start, d, val):
    if d > 1:
        ref[pl.ds(start, BLOCK, stride=d), :] = val
    else:
        ref[pl.ds(start, BLOCK), :] = val


def _band_masks():
    row = lax.broadcasted_iota(jnp.int32, (BLOCK, BLOCK), 0)
    col = lax.broadcasted_iota(jnp.int32, (BLOCK, BLOCK), 1)
    return col <= row, col >= row


def _eye(n=BLOCK):
    row = lax.broadcasted_iota(jnp.int32, (n, n), 0)
    col = lax.broadcasted_iota(jnp.int32, (n, n), 1)
    return jnp.where(row == col, 1.0, 0.0).astype(BF16)


def _attn_fwd(q_rot, k_rot, v, shards, *, name):
    s = q_rot.shape[0]
    rr = ATT_ROWS
    nblk = s // rr
    names = tuple(shards)
    nr = len(names)

    def body(*refs):
        q0, q1, q2, kp, kc, vp, vc = refs[:7]
        mix_ref, lse_ref = refs[7 + nr:9 + nr]
        qs, ks, vs, os_, ls = refs[9 + 2 * nr:14 + 2 * nr]
        h, n = pl.program_id(0), pl.program_id(1)
        start, relay, finish = _gather_ops(names, refs[7:7 + nr], refs[9 + nr:9 + 2 * nr], refs[14 + 2 * nr:])
        pl.when((h == 0) & (n == 0))(start)
        steps = ATT_HEADS * nblk
        early, late = max(1, (steps * 6) // 16), max(1, (steps * 14) // 16)
        pl.when(h * nblk + n == early)(functools.partial(relay, names[:1]))
        pl.when(h * nblk + n == late)(functools.partial(relay, names[1:]))
        for g, q_ref in enumerate((q0, q1, q2)):
            qs[g] = q_ref[...].astype(F32)
        ks[:rr] = kp[...].astype(F32)
        ks[rr:] = kc[...].astype(F32)
        vs[:rr] = vp[...].astype(F32)
        vs[rr:] = vc[...].astype(F32)
        m_cur, m_band = _band_masks()
        mask_in = jnp.concatenate([m_band, m_cur], axis=1)
        mask_first = jnp.concatenate([m_band & (n > 0), m_cur], axis=1)
        ones = jnp.ones((2 * BLOCK, HEAD_DIM), BF16)
        pieces = [(g, d, j * BLOCK * d + r, j) for g, d in enumerate(DILATIONS) for r in range(d)
                  for j in range(rr // (BLOCK * d))]

        def scores(piece):
            g, d, base, j = piece
            q = _sub(qs.at[g], base, d).astype(BF16)
            k2 = _sub(ks, rr + base - BLOCK * d, d, 2 * BLOCK).astype(BF16)
            return jnp.where(mask_first if j == 0 else mask_in, _dot_nt(q, k2), NEG_INF)

        sc = scores(pieces[0])
        for i, (g, d, base, j) in enumerate(pieces):
            cur = sc
            if i + 1 < len(pieces):
                sc = scores(pieces[i + 1])
            m = jnp.max(cur, axis=-1, keepdims=True)
            p = jnp.exp(cur - m).astype(BF16)
            v2 = _sub(vs, rr + base - BLOCK * d, d, 2 * BLOCK).astype(BF16)
            acc_l = _dot(p, jnp.concatenate([v2, ones], axis=1))
            l = acc_l[:, HEAD_DIM:]
            _sub_set(os_.at[g], base, d, acc_l[:, :HEAD_DIM] / l)
            _sub_set(ls.at[g], base, d, m + jnp.log(l))
        for c in _chunks(rr, 2 * BLOCK):
            a, b, cc = ls[0, c, :], ls[1, c, :], ls[2, c, :]
            m = jnp.maximum(jnp.maximum(a, b), cc)
            wa, wb, wc = jnp.exp(a - m), jnp.exp(b - m), jnp.exp(cc - m)
            den = wa + wb + wc
            mix_ref[c, :] = ((wa * os_[0, c, :] + wb * os_[1, c, :] + wc * os_[2, c, :]) / den).astype(BF16)
            lse_ref[c, :] = m + jnp.log(den)

        pl.when((h == ATT_HEADS - 1) & (n == nblk - 1))(finish)

    blk = lambda f: pl.BlockSpec((rr, HEAD_DIM), f)
    prv = lambda n: jnp.maximum(n - 1, 0)
    outs = _pcall(
        body, name=name, grid=(ATT_HEADS, nblk),
        in_specs=[blk(lambda h, n, g=g: (n, g * ATT_HEADS + h)) for g in range(N_GROUPS)]
        + [blk(lambda h, n: (prv(n), h)), blk(lambda h, n: (n, h))] * 2 + [HBM_SPEC] * nr,
        out_specs=[blk(lambda h, n: (n, h)), blk(lambda h, n: (n, h))] + [HBM_SPEC] * nr,
        out_shape=[SDS((s, KV_W + POOL_W), BF16), SDS((s, KV_W), F32)] + [SDS(FULL[w][0], BF16) for w in names],
        scratch_shapes=[pltpu.VMEM((N_GROUPS, rr, HEAD_DIM), F32), pltpu.VMEM((2 * rr, HEAD_DIM), F32),
                        pltpu.VMEM((2 * rr, HEAD_DIM), F32), pltpu.VMEM((N_GROUPS, rr, HEAD_DIM), F32),
                        pltpu.VMEM((N_GROUPS, rr, HEAD_DIM), F32)] + _gather_scratch(names),
        compiler_params=_cp())(q_rot, q_rot, q_rot, k_rot, k_rot, v, v, *[shards[w] for w in names])
    return outs[0], outs[1], dict(zip(names, outs[2:]))


def _attn_bwd(q_rot, k_rot, v, mix, dmix, lse, grads, *, name):
    s = q_rot.shape[0]
    rr = ATT_ROWS
    nblk = s // rr
    names = tuple(grads)
    nr = len(names)

    def body(*refs):
        q0, q1, q2, qx0, qx1, qx2, kp, kc, vp, vc, do_c, do_x, o_c, o_x, l_c, l_x = refs[:16]
        dq_ref, dk_ref, dv_ref = refs[16 + nr:19 + nr]
        qs, ks, vs, dos, lss, dls, dqs, dks, dvs, send_sems, recv_sems, local_sems = refs[19 + 2 * nr:]
        h, n = pl.program_id(0), pl.program_id(1)
        copies = _exchange_copies(names, refs[16:16 + nr], refs[19 + nr:19 + 2 * nr], send_sems, recv_sems, local_sems)

        @pl.when((h == 0) & (n == 0))
        def _():
            _start(copies)

        for g, (qc_ref, qx_ref) in enumerate(((q0, qx0), (q1, qx1), (q2, qx2))):
            qs[g, :rr] = qc_ref[...].astype(F32)
            qs[g, rr:] = qx_ref[...].astype(F32)
        ks[:rr] = kp[...].astype(F32)
        ks[rr:] = kc[...].astype(F32)
        vs[:rr] = vp[...].astype(F32)
        vs[rr:] = vc[...].astype(F32)
        lss[:rr] = l_c[...]
        lss[rr:] = l_x[...]
        for half, (d_ref, o_ref) in enumerate(((do_c, o_c), (do_x, o_x))):
            for c in _chunks(rr, 2 * BLOCK):
                cs = slice(half * rr + c.start, half * rr + c.stop)
                dof = d_ref[c, :].astype(F32)
                dos[cs, :] = dof
                dls[cs, :] = jnp.broadcast_to(jnp.sum(dof * o_ref[c, :].astype(F32), axis=-1, keepdims=True),
                                              (2 * BLOCK, HEAD_DIM))
        m_cur, m_band = _band_masks()
        mask_in = jnp.concatenate([m_cur, m_band], axis=0)
        mask_last = jnp.concatenate([m_cur, m_band & (n + 1 < nblk)], axis=0)
        m_first = m_band & (n > 0)
        eye = _eye()
        pieces = [(g, d, j * BLOCK * d + r, j, rr // (BLOCK * d)) for g, d in enumerate(DILATIONS) for r in range(d)
                  for j in range(rr // (BLOCK * d))]

        def front(piece):
            g, d, base, _, _ = piece
            q2 = _sub(qs.at[g], base, d, 2 * BLOCK).astype(BF16)
            do2 = _sub(dos, base, d, 2 * BLOCK).astype(BF16)
            k = _sub(ks, rr + base, d).astype(BF16)
            vv = _sub(vs, rr + base, d).astype(BF16)
            return q2, do2, k, _dot_nt(q2, k), _dot_nt(do2, vv)

        def middle(piece, fr, dq_acc):
            g, d, base, j, nsub = piece
            q2, do2, k, s2, dp2 = fr
            if j == 0:
                kp_ = _sub(ks, rr + base - BLOCK * d, d).astype(BF16)
                p0 = jnp.where(m_first, jnp.exp(_dot_nt(q2[:BLOCK], kp_) - _sub(lss, base, d)), 0.0)
                ds0 = p0 * (_dot_nt(do2[:BLOCK], _sub(vs, rr + base - BLOCK * d, d).astype(BF16)) - _sub(dls, base, d))
                dq_acc = _dot(ds0.astype(BF16), kp_)
            p2 = jnp.where(mask_last if j + 1 == nsub else mask_in,
                           jnp.exp(s2 - _sub(lss, base, d, 2 * BLOCK)), 0.0)
            ds2 = (p2 * (dp2 - _sub(dls, base, d, 2 * BLOCK))).astype(BF16)
            dq2 = _dot(ds2, k)
            return dq2[BLOCK:], (_dot_nt(eye, ds2), _dot_nt(eye, p2.astype(BF16)), q2, do2, dq_acc + dq2[:BLOCK])

        def back(piece, tr):
            g, d, base, _, _ = piece
            ds_t, p_t, q2, do2, dq = tr
            _sub_set(dqs.at[g], base, d, dq)
            dk, dv = _dot(ds_t.astype(BF16), q2), _dot(p_t.astype(BF16), do2)
            if g == 0:
                _sub_set(dks, base, d, dk)
                _sub_set(dvs, base, d, dv)
            else:
                _sub_set(dks, base, d, _sub(dks, base, d) + dk)
                _sub_set(dvs, base, d, _sub(dvs, base, d) + dv)

        fr, held, dq_acc = front(pieces[0]), None, None
        for i, piece in enumerate(pieces):
            cur = fr
            if i + 1 < len(pieces):
                fr = front(pieces[i + 1])
            dq_acc, now = middle(piece, cur, dq_acc)
            if held is not None:
                back(pieces[i - 1], held)
            held = now
        back(pieces[-1], held)
        for g in range(N_GROUPS):
            dq_ref[g] = dqs[g].astype(BF16)
        dk_ref[...] = dks[...]
        dv_ref[...] = dvs[...]

        @pl.when((h == ATT_HEADS - 1) & (n == nblk - 1))
        def _():
            _finish(copies)

    blk = lambda f: pl.BlockSpec((rr, HEAD_DIM), f)
    prv = lambda n: jnp.maximum(n - 1, 0)
    nxt = lambda n: jnp.minimum(n + 1, nblk - 1)
    cur_kv = blk(lambda h, n: (n, h))
    outs = _pcall(
        body, name=name, grid=(ATT_HEADS, nblk),
        in_specs=[blk(lambda h, n, g=g: (n, g * ATT_HEADS + h)) for g in range(N_GROUPS)]
        + [blk(lambda h, n, g=g: (nxt(n), g * ATT_HEADS + h)) for g in range(N_GROUPS)]
        + [blk(lambda h, n: (prv(n), h)), cur_kv] * 2
        + [cur_kv, blk(lambda h, n: (nxt(n), h))] * 3 + [HBM_SPEC] * nr,
        out_specs=[pl.BlockSpec((N_GROUPS, rr, HEAD_DIM), lambda h, n: (0, n, h)), cur_kv, cur_kv] + [HBM_SPEC] * nr,
        out_shape=[SDS((N_GROUPS, s, KV_W), BF16), SDS((s, KV_W), F32), SDS((s, KV_W), F32)]
        + [SDS((N_DEV,) + _shard_shape(w), BF16) for w in names],
        scratch_shapes=[pltpu.VMEM((N_GROUPS, 2 * rr, HEAD_DIM), F32)] + [pltpu.VMEM((2 * rr, HEAD_DIM), F32)] * 5
        + [pltpu.VMEM((N_GROUPS, rr, HEAD_DIM), F32), pltpu.VMEM((rr, HEAD_DIM), F32), pltpu.VMEM((rr, HEAD_DIM), F32)]
        + _comm_sems(nr),
        compiler_params=_cp())(q_rot, q_rot, q_rot, q_rot, q_rot, q_rot, k_rot, k_rot, v, v, dmix, dmix, mix, mix,
                               lse, lse, *[grads[w] for w in names])
    return outs[0], outs[1], outs[2], dict(zip(names, outs[3:]))


def _pool_d(u_ref, halo_ref, i, tm):
    halo = jnp.where(i > 0, halo_ref[...], 0.0)
    t = i * tm + lax.broadcasted_iota(jnp.int32, (tm, 1), 0)
    out = []
    for g, w in enumerate(POOL_WINDOWS):
        sl = slice(g * HEAD_DIM, (g + 1) * HEAD_DIM)
        u = u_ref[:, sl]
        acc = jnp.concatenate([halo[:, sl], u], axis=0)
        sh = 1
        while sh < w:
            acc = acc + pltpu.roll(acc, sh, 0)
            sh *= 2
        cnt = jnp.minimum(t + 1, w).astype(F32)
        out.append(acc[HALO:, :] / cnt - u)
    return out


def _pool_fwd(proj, mix, pool_w, pool_scale, *, name):
    s = proj.shape[0]
    tm = _rows(s)
    ucol = (IN_W - POOL_W) // POOL_W

    def body(u_ref, halo_ref, mix_in, w_ref, sc_ref, o_ref):
        del mix_in
        dd = _pool_d(u_ref, halo_ref, pl.program_id(0), tm)
        for g in range(len(POOL_WINDOWS)):
            sl = slice(g * HEAD_DIM, (g + 1) * HEAD_DIM)
            y = _dot(dd[g].astype(BF16), w_ref[g].astype(BF16))
            o_ref[:, sl] = (y * sc_ref[:, sl]).astype(BF16)

    return _pcall(
        body, name=name, grid=(s // tm,),
        in_specs=[pl.BlockSpec((tm, POOL_W), lambda i: (i, ucol)),
                  pl.BlockSpec((HALO, POOL_W), lambda i: (jnp.maximum(i * (tm // HALO) - 1, 0), ucol)),
                  pl.BlockSpec(memory_space=pl.ANY),
                  pl.BlockSpec((len(POOL_WINDOWS), HEAD_DIM, HEAD_DIM), lambda i: (0, 0, 0)),
                  pl.BlockSpec((1, POOL_W), lambda i: (0, 0))],
        out_specs=pl.BlockSpec((tm, POOL_W), lambda i: (i, 1)),
        out_shape=SDS(mix.shape, BF16), input_output_aliases={2: 0},
        compiler_params=_cp())(proj, proj, mix, pool_w, pool_scale)


def _pool_bwd(proj, dmix, pool_w, pool_scale, *, name):
    s = proj.shape[0]
    tm = _rows(s)
    nblk = s // tm
    ucol = (IN_W - POOL_W) // POOL_W
    ng = len(POOL_WINDOWS)

    def body(u_ref, halo_ref, dp_ref, dpn_ref, w_ref, sc_ref, du_ref, dw_ref, dsc_ref):
        i = pl.program_id(0)

        @pl.when(i == 0)
        def _():
            dw_ref[...] = jnp.zeros_like(dw_ref)
            dsc_ref[...] = jnp.zeros_like(dsc_ref)

        dd = _pool_d(u_ref, halo_ref, i, tm)
        t = i * tm + lax.broadcasted_iota(jnp.int32, (tm, 1), 0)
        dpn = jnp.where(i + 1 < nblk, dpn_ref[...].astype(F32), 0.0)
        for g, w in enumerate(POOL_WINDOWS):
            sl = slice(g * HEAD_DIM, (g + 1) * HEAD_DIM)
            wg = w_ref[g].astype(BF16)
            db = dd[g].astype(BF16)
            dp = dp_ref[:, sl].astype(F32)
            dsc_ref[:, sl] += jnp.sum(dp * _dot(db, wg), axis=0, keepdims=True)
            dy = (dp * sc_ref[:, sl]).astype(BF16)
            dw_ref[g] += _dot_tn(db, dy)
            g_d = _dot_nt(dy, wg)
            g_dn = _dot_nt((dpn[:, sl] * sc_ref[:, sl]).astype(BF16), wg)
            cnt = jnp.minimum(t + 1, w).astype(F32)
            acc = jnp.concatenate([g_d / cnt, g_dn * (1.0 / w)], axis=0)
            sh = 1
            while sh < w:
                acc = acc + pltpu.roll(acc, tm + HALO - sh, 0)
                sh *= 2
            du_ref[:, sl] = acc[:tm, :] - g_d

    nh = s // HALO
    return _pcall(
        body, name=name, grid=(nblk,),
        in_specs=[pl.BlockSpec((tm, POOL_W), lambda i: (i, ucol)),
                  pl.BlockSpec((HALO, POOL_W), lambda i: (jnp.maximum(i * (tm // HALO) - 1, 0), ucol)),
                  pl.BlockSpec((tm, POOL_W), lambda i: (i, 1)),
                  pl.BlockSpec((HALO, POOL_W), lambda i: (jnp.minimum((i + 1) * (tm // HALO), nh - 1), 1)),
                  pl.BlockSpec((ng, HEAD_DIM, HEAD_DIM), lambda i: (0, 0, 0)),
                  pl.BlockSpec((1, POOL_W), lambda i: (0, 0))],
        out_specs=[pl.BlockSpec((tm, POOL_W), lambda i: (i, 0)),
                   pl.BlockSpec((ng, HEAD_DIM, HEAD_DIM), lambda i: (0, 0, 0)),
                   pl.BlockSpec((1, POOL_W), lambda i: (0, 0))],
        out_shape=[SDS((s, POOL_W), F32), SDS((ng, HEAD_DIM, HEAD_DIM), F32), SDS((1, POOL_W), F32)],
        compiler_params=_cp())(proj, proj, dmix, dmix, pool_w, pool_scale)


def _mem_fwd(mem, mem_g, wckv, ck_g, *, name):
    def body(m_ref, g_ref, w_ref, kg_ref, mn_ref, ckr_ref, ckn_ref, cv_ref):
        mv = m_ref[...]
        mn = (mv * _rms_r(mv) * g_ref[...]).astype(BF16)
        mn_ref[...] = mn
        ckv = _dot(mn, w_ref[...])
        ckr_ref[...] = ckv[:, :X_W]
        cv_ref[...] = ckv[:, X_W:].astype(BF16)
        for h in range(ATT_HEADS):
            sl = slice(h * HEAD_DIM, (h + 1) * HEAD_DIM)
            y = ckv[:, sl]
            ckn_ref[:, sl] = (y * _rms_r(y) * kg_ref[...]).astype(BF16)

    return _pcall(
        body, name=name,
        out_shape=[SDS((N_MEM, D_MODEL), BF16), SDS((N_MEM, X_W), F32), SDS((N_MEM, X_W), BF16),
                   SDS((N_MEM, X_W), BF16)],
        compiler_params=_cp())(mem, mem_g, wckv, ck_g)


def _cross_q(cq_ref, g_ref, sl):
    y = cq_ref[:, sl]
    r = _rms_r(y)
    return y, r, y * r * g_ref[...] * SCALE


def _cross_fwd(h1, gain, wcq, ck_n, cv, cq_g, wco, *, name):
    s, d = h1.shape
    tm = _rows(s)

    def body(x_ref, gn_ref, wq_ref, k_ref, v_ref, g_ref, wo_ref, h2_ref, xn_ref, cq_ref, o_ref):
        xv = x_ref[...]
        xn_ref[...] = (xv * _rms_r(xv) * gn_ref[...]).astype(BF16)
        cq_ref[...] = _dot(xn_ref[...], wq_ref[...])
        for h in range(ATT_HEADS):
            sl = slice(h * HEAD_DIM, (h + 1) * HEAD_DIM)
            _, _, qn = _cross_q(cq_ref, g_ref, sl)
            sc = _dot_nt(qn.astype(BF16), k_ref[:, sl])
            p = jnp.exp(sc - jnp.max(sc, axis=-1, keepdims=True))
            p = p / jnp.sum(p, axis=-1, keepdims=True)
            o_ref[:, sl] = _dot(p.astype(BF16), v_ref[:, sl]).astype(BF16)
        for c in _chunks(d, X_W):
            h2_ref[:, c] = x_ref[:, c] + _dot(o_ref[...], wo_ref[:, c])

    row = lambda w_: pl.BlockSpec((tm, w_), lambda i: (i, 0))
    return _pcall(
        body, name=name, grid=(s // tm,),
        in_specs=[row(d), _resident((1, d)), _resident(wcq.shape), _resident(ck_n.shape), _resident(cv.shape),
                  _resident(cq_g.shape), _resident(wco.shape)],
        out_specs=[row(d), row(d), row(X_W), row(X_W)],
        out_shape=[SDS((s, d), F32), SDS((s, d), BF16), SDS((s, X_W), F32), SDS((s, X_W), BF16)],
        compiler_params=_cp())(h1, gain, wcq, ck_n, cv, cq_g, wco)


def _cross_bwd(d_o, cq_raw, ck_n, cv, cq_g, *, name):
    s = cq_raw.shape[0]
    tm = _rows(s)

    def body(do_ref, cq_ref, k_ref, v_ref, g_ref, dcq_ref, dk_ref, dv_ref, dg_ref):
        @pl.when(pl.program_id(0) == 0)
        def _():
            dk_ref[...] = jnp.zeros_like(dk_ref)
            dv_ref[...] = jnp.zeros_like(dv_ref)
            dg_ref[...] = jnp.zeros_like(dg_ref)

        def front(h):
            sl = slice(h * HEAD_DIM, (h + 1) * HEAD_DIM)
            y, r, qn = _cross_q(cq_ref, g_ref, sl)
            qb = qn.astype(BF16)
            do = do_ref[:, sl]
            return y, r, qb, do, _dot_nt(qb, k_ref[:, sl]), _dot_nt(do, v_ref[:, sl])

        dg = jnp.zeros((1, HEAD_DIM), F32)
        nxt = front(0)
        for h in range(ATT_HEADS):
            sl = slice(h * HEAD_DIM, (h + 1) * HEAD_DIM)
            y, r, qb, do, sc, dp = nxt
            if h + 1 < ATT_HEADS:
                nxt = front(h + 1)
            p = jnp.exp(sc - jnp.max(sc, axis=-1, keepdims=True))
            p = p / jnp.sum(p, axis=-1, keepdims=True)
            dv_ref[:, sl] += _dot_tn(p.astype(BF16), do)
            ds = (p * (dp - jnp.sum(dp * p, axis=-1, keepdims=True))).astype(BF16)
            dk_ref[:, sl] += _dot_tn(ds, qb)
            dn = _dot(ds, k_ref[:, sl]) * SCALE
            dy, dgain = _norm_bwd(y, r, g_ref[...], dn)
            dcq_ref[:, sl] = dy.astype(BF16)
            dg = dg + dgain
        dg_ref[...] += dg

    full = lambda a: pl.BlockSpec(a.shape, lambda i: (0, 0))
    row = pl.BlockSpec((tm, X_W), lambda i: (i, 0))
    return _pcall(
        body, name=name, grid=(s // tm,),
        in_specs=[row, row, full(ck_n), full(cv), full(cq_g)],
        out_specs=[row, pl.BlockSpec((N_MEM, X_W), lambda i: (0, 0)), pl.BlockSpec((N_MEM, X_W), lambda i: (0, 0)),
                   pl.BlockSpec((1, HEAD_DIM), lambda i: (0, 0))],
        out_shape=[SDS((s, X_W), BF16), SDS((N_MEM, X_W), F32), SDS((N_MEM, X_W), F32), SDS((1, HEAD_DIM), F32)],
        compiler_params=_cp())(d_o, cq_raw, ck_n, cv, cq_g)


def _mem_bwd(dck_n, dcv, ck_raw, memn, mem, wckv, mem_g, ck_g, *, name):
    def body(dk_ref, dv_ref, ckr_ref, mn_ref, m_ref, w_ref, mg_ref, kg_ref, dw_ref, dmg_ref, dkg_ref, dckv_s):
        dkg = jnp.zeros((1, HEAD_DIM), F32)
        for h in range(ATT_HEADS):
            sl = slice(h * HEAD_DIM, (h + 1) * HEAD_DIM)
            y = ckr_ref[:, sl]
            dy, dgain = _norm_bwd(y, _rms_r(y), kg_ref[...], dk_ref[:, sl])
            dckv_s[:, sl] = dy.astype(BF16)
            dkg = dkg + dgain
        dkg_ref[...] = dkg
        dckv_s[:, X_W:] = dv_ref[...].astype(BF16)
        dckv = dckv_s[...]
        dw_ref[...] = _dot_tn(mn_ref[...], dckv).astype(BF16)
        dmn = _dot_nt(dckv, w_ref[...])
        mv = m_ref[...]
        dmg_ref[...] = jnp.sum(dmn * mv * _rms_r(mv), axis=0, keepdims=True)

    return _pcall(
        body, name=name,
        out_shape=[SDS((D_MODEL, 2 * X_W), BF16), SDS((1, D_MODEL), F32), SDS((1, HEAD_DIM), F32)],
        scratch_shapes=[pltpu.VMEM((N_MEM, 2 * X_W), BF16)],
        compiler_params=_cp())(dck_n, dcv, ck_raw, memn, mem, wckv, mem_g, ck_g)


def _rope_tables(pos):
    inv_freq = ROPE_THETA ** (-jnp.arange(0, ROT_DIM, 2, dtype=F32) / ROT_DIM)
    ang = pos.astype(F32)[:, None] * inv_freq
    cos, sin = jnp.cos(ang), jnp.sin(ang)
    s = pos.shape[0]
    rest = HEAD_DIM - ROT_DIM
    cos_t = jnp.concatenate([cos, cos, jnp.ones((s, rest), F32)], axis=1)
    sin_t = jnp.concatenate([-sin, sin, jnp.zeros((s, rest), F32)], axis=1)
    return cos_t, sin_t


def _local_step(x, mem, pos, tgt, w_in, shards, sm):
    cos_t, sin_t = _rope_tables(pos)
    wb = {"w_in": w_in}

    pick = lambda *names: {a: shards[a] for a in names}
    proj, xn, got = _norm_linear(x, sm["mix_norm_g"], wb["w_in"], cn=768, out_dtype=F32, name="fwd_in_proj",
                                 shards=pick("w_out", "w_cq", "w_ckv", "w_co"))
    wb.update(got)
    q_rot, k_rot, v = _qk_prep(proj, cos_t, sin_t, sm["q_norm_g"], sm["k_norm_g"], name="fwd_qk_prep")
    mix, lse_all, got = _attn_fwd(q_rot, k_rot, v, pick("w_down", "w_gate_up"), name="fwd_attn")
    wb.update(got)
    mix = _pool_fwd(proj, mix, sm["pool_w"], sm["pool_scale"], name="fwd_pool")
    h1 = _linear_res(mix, wb["w_out"], x, cn=512, name="fwd_out_proj")
    memn, ck_raw, ck_n, cv = _mem_fwd(mem, sm["mem_norm_g"], wb["w_ckv"], sm["ck_norm_g"], name="fwd_mem")
    h2, hn, cq_raw, co = _cross_fwd(h1, sm["cross_norm_g"], wb["w_cq"], ck_n, cv, sm["cq_norm_g"], wb["w_co"],
                                    name="fwd_cross")
    gu, act, fn = _norm_linear_swiglu(h2, sm["ffn_norm_g"], wb["w_gate_up"], name="fwd_gate_up")
    dy, dyb, sq = _linear_res_loss(act, wb["w_down"], h2, tgt, name="fwd_down_loss")

    gw = {}
    gs = {}
    dgu = _swiglu_bwd(dyb, wb["w_down"], gu, name="bwd_swiglu")
    gw_down = _dw_tn(act, dyb, tkw=1024, tn=1024, name="bwd_dw_down")
    dh2, dh2b, gs["ffn_norm_g"], parts = _linear_nt_normbwd(dgu, wb["w_gate_up"], h2, dy, sm["ffn_norm_g"],
                                                            name="bwd_ffn_in", grads={"w_down": gw_down})
    gw["w_gate_up"] = _dw_tn(fn, dgu, tkw=1024, tn=1536, name="bwd_dw_gate_up")

    d_co = _linear_nt(dh2b, wb["w_co"], cn=512, name="bwd_co_proj")
    gw["w_co"] = _dw_tn(co, dh2b, tkw=512, tn=1024, name="bwd_dw_co")
    dcq, dck_n, dcv, gs["cq_norm_g"] = _cross_bwd(d_co, cq_raw, ck_n, cv, sm["cq_norm_g"], name="bwd_cross")
    gw["w_ckv"], gs["mem_norm_g"], gs["ck_norm_g"] = _mem_bwd(dck_n, dcv, ck_raw, memn, mem, wb["w_ckv"],
                                                             sm["mem_norm_g"], sm["ck_norm_g"], name="bwd_mem")
    dh1, dh1b, gs["cross_norm_g"] = _linear_nt_normbwd(dcq, wb["w_cq"], h1, dh2, sm["cross_norm_g"],
                                                       name="bwd_cq_in")
    gw["w_cq"] = _dw_tn(hn, dcq, tkw=1024, tn=512, name="bwd_dw_cq")

    dmix = _linear_nt(dh1b, wb["w_out"], cn=512, name="bwd_out_proj")
    gw["w_out"] = _dw_tn(mix, dh1b, tkw=1024, tn=1024, name="bwd_dw_out")
    du, gs["pool_w"], gs["pool_scale"] = _pool_bwd(proj, dmix, sm["pool_w"], sm["pool_scale"], name="bwd_pool")
    dq, dk, dv, got = _attn_bwd(q_rot, k_rot, v, mix, dmix, lse_all, gw, name="bwd_attn")
    parts.update(got)
    dproj, gs["q_norm_g"], gs["k_norm_g"] = _qk_prep_bwd(proj, dq, dk, dv, du, cos_t, sin_t, sm["q_norm_g"],
                                                         sm["k_norm_g"], name="bwd_qk_prep")
    gw_in = _dw_tn(xn, dproj, tkw=1024, tn=1536, name="bwd_dw_in")
    dx, _, gs["mix_norm_g"], last = _linear_nt_normbwd(dproj, wb["w_in"], x, dh1, sm["mix_norm_g"],
                                                       name="bwd_in_proj", grads={"w_in": gw_in})
    parts.update(last)
    return sq, dx, parts, gs


SHARDED = ("w_in", "w_out", "w_cq", "w_ckv", "w_co", "w_gate_up", "w_down")
SMALL = ("mix_norm_g", "q_norm_g", "k_norm_g", "pool_w", "pool_scale", "cross_norm_g", "mem_norm_g", "cq_norm_g",
         "ck_norm_g", "ffn_norm_g")
FULL = {
    "w_in": ((D_MODEL, IN_W), 1, IN_W // N_DEV),
    "w_out": ((D_MODEL, D_MODEL), 0, D_MODEL // N_DEV),
    "w_cq": ((D_MODEL, X_W), 0, D_MODEL // N_DEV),
    "w_ckv": ((D_MODEL, 2 * X_W), 0, D_MODEL // N_DEV),
    "w_co": ((X_W, D_MODEL), 1, D_MODEL // N_DEV),
    "w_gate_up": ((D_MODEL, 2 * FF_PAD), 1, FF_TILE),
    "w_down": ((FF_PAD, D_MODEL), 0, DOWN_SHARD),
}


def _shard_shape(name):
    shape, axis, width = FULL[name]
    return tuple(width if a == axis else n for a, n in enumerate(shape))


def _window(ref, name, dev):
    _, axis, width = FULL[name]
    if name == "w_down":
        start = pl.multiple_of((dev // 2) * FF_TILE + (dev % 2) * DOWN_SHARD, HALO)
    else:
        start = pl.multiple_of(dev * width, BLOCK)
    return ref.at[pl.ds(start, width), :] if axis == 0 else ref.at[:, pl.ds(start, width)]


def _mesh_place():
    x, y, c = lax.axis_index("x"), lax.axis_index("y"), lax.axis_index("c")
    return x, y, c, 4 * x + 2 * y + c


def _peer(x, y, c, k):
    px = 1 - x if k & 4 else x
    py = 1 - y if k & 2 else y
    pc = 1 - c if k & 1 else c
    return (px, py, pc), 4 * px + 2 * py + pc


HBM_SPEC = pl.BlockSpec(memory_space=pltpu.HBM)


def _comm_sems(n):
    return [pltpu.SemaphoreType.DMA((n, N_DEV - 1)), pltpu.SemaphoreType.DMA((n, N_DEV - 1)),
            pltpu.SemaphoreType.DMA((n,))]


DIRECT = (1, 2, 4, 6)
RELAYED = (2, 4, 6)


def _relay_step(steps):
    return max(1, (steps * 11) // 16) if steps > 1 else 0


def _gather_copies(names, ins, outs, send_sems, recv_sems, local_sems):
    x, y, c, me = _mesh_place()
    sibling, _ = _peer(x, y, c, 1)
    local, direct, relays, recv = [], [], {}, {}
    for wi, name in enumerate(names):
        def sems(k, to):
            return dict(send_sem=send_sems.at[wi, k - 1], recv_sem=recv_sems.at[wi, k - 1], device_id=to,
                        device_id_type=MESH)
        local.append(pltpu.make_async_copy(ins[wi], _window(outs[wi], name, me), local_sems.at[wi]))
        for k in range(1, N_DEV):
            peer, pidx = _peer(x, y, c, k)
            win = _window(outs[wi], name, pidx)
            recv[wi, k] = pltpu.make_async_remote_copy(src_ref=ins[wi], dst_ref=win, **sems(k, peer))
            if k in DIRECT:
                direct.append(pltpu.make_async_remote_copy(src_ref=ins[wi], dst_ref=_window(outs[wi], name, me),
                                                           **sems(k, peer)))
            if k in RELAYED:
                relays[wi, k] = pltpu.make_async_remote_copy(src_ref=win, dst_ref=win, **sems(k + 1, sibling))
    return local, direct, relays, recv


def _exchange_copies(names, ins, outs, send_sems, recv_sems, local_sems):
    x, y, c, me = _mesh_place()
    local, sent, recv = [], [], []
    for wi, name in enumerate(names):
        local.append(pltpu.make_async_copy(_window(ins[wi], name, me), outs[wi].at[0], local_sems.at[wi]))
        for k in range(1, N_DEV):
            peer, pidx = _peer(x, y, c, k)
            sems = dict(send_sem=send_sems.at[wi, k - 1], recv_sem=recv_sems.at[wi, k - 1], device_id=peer,
                        device_id_type=MESH)
            sent.append(pltpu.make_async_remote_copy(src_ref=_window(ins[wi], name, pidx), dst_ref=outs[wi].at[k], **sems))
            recv.append(pltpu.make_async_remote_copy(src_ref=_window(ins[wi], name, me), dst_ref=outs[wi].at[k], **sems))
    return local, sent, recv


def _down_pads(down_ref, zero_ref, zero_sems):
    return [pltpu.make_async_copy(zero_ref, down_ref.at[pl.ds(t * FF_TILE + FF_SHARD, FF_TILE - FF_SHARD), :],
                                  zero_sems.at[t]) for t in range(FF_PAD // FF_TILE)]


def _gather_scratch(names):
    if not names:
        return []
    pad = [pltpu.VMEM((FF_TILE - FF_SHARD, D_MODEL), BF16), pltpu.SemaphoreType.DMA((FF_PAD // FF_TILE,))]
    return _comm_sems(len(names)) + (pad if "w_down" in names else [])


def _gather_ops(names, ins, outs, scratch):
    local, direct, relays, recv = _gather_copies(names, ins, outs, *scratch[:3])
    pads = _down_pads(outs[names.index("w_down")], scratch[3], scratch[4]) if "w_down" in names else []

    def start():
        if pads:
            scratch[3][...] = jnp.zeros_like(scratch[3])
        for cp in local + direct + pads:
            cp.start()

    def relay(only=None):
        for (wi, k), cp in relays.items():
            if only is None or names[wi] in only:
                recv[wi, k].wait_recv()
                cp.start()

    def finish():
        for (wi, k), cp in recv.items():
            if k not in RELAYED:
                cp.wait_recv()
        for cp in direct + list(relays.values()):
            cp.wait_send()
        for cp in local + pads:
            cp.wait()

    return start, relay, finish


def _start(copies):
    local, sent, _ = copies
    for cp in local + sent:
        cp.start()


def _finish(copies):
    local, sent, recv = copies
    for cp in recv:
        cp.wait_recv()
    for cp in sent:
        cp.wait_send()
    for cp in local:
        cp.wait()


def _gather_weights(shards):
    names = tuple(shards)
    nw = len(names)

    def body(*refs):
        start, relay, finish = _gather_ops(names, refs[:nw], refs[nw:2 * nw], refs[2 * nw:])
        start()
        relay()
        finish()

    outs = _pcall(
        body, name="gather_weights",
        in_specs=[HBM_SPEC] * nw, out_specs=[HBM_SPEC] * nw,
        out_shape=[SDS(FULL[n][0], BF16) for n in names],
        scratch_shapes=_comm_sems(nw))(*[shards[n] for n in names])
    return dict(zip(names, outs))


def _exchange_small(blocks):
    nb = len(blocks)

    def body(*refs):
        ins, outs = refs[:nb], refs[nb:2 * nb]
        send_sems, recv_sems, local_sems = refs[2 * nb:]
        x, y, c, me = _mesh_place()
        local, sent, recv = [], [], []
        for bi in range(nb):
            local.append(pltpu.make_async_copy(ins[bi], outs[bi].at[me], local_sems.at[bi]))
            for k in range(1, N_DEV):
                peer, pidx = _peer(x, y, c, k)
                sems = dict(send_sem=send_sems.at[bi, k - 1], recv_sem=recv_sems.at[bi, k - 1], device_id=peer,
                            device_id_type=MESH)
                sent.append(pltpu.make_async_remote_copy(src_ref=ins[bi], dst_ref=outs[bi].at[me], **sems))
                recv.append(pltpu.make_async_remote_copy(src_ref=ins[bi], dst_ref=outs[bi].at[pidx], **sems))
        _start((local, sent, recv))
        _finish((local, sent, recv))

    return _pcall(
        body, name="exchange_small", in_specs=[HBM_SPEC] * nb, out_specs=[HBM_SPEC] * nb,
        out_shape=[SDS((N_DEV,) + a.shape, F32) for a in blocks], scratch_shapes=_comm_sems(nb))(*blocks)


def _adam_math(g, w, m, v):
    m_new = ADAM_B1 * m + (1.0 - ADAM_B1) * g
    v_new = ADAM_B2 * v + (1.0 - ADAM_B2) * (g * g)
    m_hat = m_new / (1.0 - ADAM_B1 ** ADAM_STEP)
    v_hat = v_new / (1.0 - ADAM_B2 ** ADAM_STEP)
    return -ADAM_LR * (m_hat / (jnp.sqrt(v_hat) + ADAM_EPS) + ADAM_WD * w), m_new, v_new


def _adamw_small(parts, w, m, v, sq_parts, *, name):
    n = len(parts)

    def body(*refs):
        p_refs, w_refs, m_refs, v_refs = refs[:n], refs[n:2 * n], refs[2 * n:3 * n], refs[3 * n:4 * n]
        sq_ref, outs = refs[4 * n], refs[4 * n + 1:]
        for i in range(n):
            g = p_refs[i][0]
            for k in range(1, N_DEV):
                g = g + p_refs[i][k]
            delta, m_new, v_new = _adam_math(g, w_refs[i][...], m_refs[i][...], v_refs[i][...])
            outs[4 * i][...] = g
            outs[4 * i + 1][...] = delta
            outs[4 * i + 2][...] = m_new
            outs[4 * i + 3][...] = v_new
        tot = sq_ref[0]
        for k in range(1, N_DEV):
            tot = tot + sq_ref[k]
        outs[4 * n][...] = (0.5 / D_MODEL) * jnp.sum(tot, axis=1, keepdims=True)

    out_shape = [SDS(a.shape, F32) for a in w for _ in range(4)] + [SDS((1, 1), F32)]
    outs = _pcall(body, name=name, out_shape=out_shape, compiler_params=_cp())(*parts, *w, *m, *v, sq_parts)
    return [outs[4 * i:4 * i + 4] for i in range(n)], outs[4 * n][0, 0]


def _adamw(parts, w, m, v, *, name):
    r, c = w.shape
    pc = parts.shape[2]
    tr = r
    for cand in (256, 128, 88):
        if r % cand == 0:
            tr = cand
            break

    def body(p_ref, w_ref, m_ref, v_ref, g_ref, d_ref, mo_ref, vo_ref):
        g = p_ref[0, :, :c].astype(F32)
        for k in range(1, N_DEV):
            g = g + p_ref[k, :, :c].astype(F32)
        g_ref[...] = g
        d_ref[...], mo_ref[...], vo_ref[...] = _adam_math(g, w_ref[...], m_ref[...], v_ref[...])

    row = pl.BlockSpec((tr, c), lambda i: (i, 0))
    return _pcall(
        body, name=name, grid=(r // tr,),
        in_specs=[pl.BlockSpec((N_DEV, tr, pc), lambda i: (0, i, 0)), row, row, row],
        out_specs=[row] * 4, out_shape=[SDS((r, c), F32)] * 4, compiler_params=_cp())(parts, w, m, v)


def _pad_cols(a, width):
    return jnp.pad(a, ((0, 0), (0, width - a.shape[1])))


def kernel(x, mem, positions, mix_norm_g, w_in, q_norm_g, k_norm_g, pool_w, pool_scale, w_out, cross_norm_g, mem_norm_g, w_cq, w_ckv, cq_norm_g, ck_norm_g, w_co, ffn_norm_g, w_gate_up, w_down, loss_target, m_mix_norm_g, m_w_in, m_q_norm_g, m_k_norm_g, m_pool_w, m_pool_scale, m_w_out, m_cross_norm_g, m_mem_norm_g, m_w_cq, m_w_ckv, m_cq_norm_g, m_ck_norm_g, m_w_co, m_ffn_norm_g, m_w_gate_up, m_w_down, v_mix_norm_g, v_w_in, v_q_norm_g, v_k_norm_g, v_pool_w, v_pool_scale, v_w_out, v_cross_norm_g, v_mem_norm_g, v_w_cq, v_w_ckv, v_cq_norm_g, v_ck_norm_g, v_w_co, v_ffn_norm_g, v_w_gate_up, v_w_down):
    given = dict(locals())
    w_f32 = {n: given[n][0] for n in SHARDED + SMALL}
    m_f32 = {n: given["m_" + n][0] for n in SHARDED + SMALL}
    v_f32 = {n: given["v_" + n][0] for n in SHARDED + SMALL}
    shards = {n: w_f32[n].astype(BF16) for n in SHARDED}
    shards["w_gate_up"] = _pad_cols(shards["w_gate_up"], FF_TILE)
    w_in_full = _gather_weights({"w_in": shards.pop("w_in")})["w_in"]
    sm_rows = {n: (w_f32[n] if w_f32[n].ndim == 3 else w_f32[n].reshape(1, -1)) for n in SMALL}
    sq, dx, parts, gs = _local_step(x[0], mem[0], positions[0], loss_target[0], w_in_full, shards, sm_rows)

    flat = lambda a: a.reshape(-1, a.shape[-1])
    got = _exchange_small([flat(gs[n]) for n in SMALL] + [sq])
    small_res, loss = _adamw_small(got[:-1], [flat(sm_rows[n]) for n in SMALL],
                                   [flat(m_f32[n].reshape(sm_rows[n].shape)) for n in SMALL],
                                   [flat(v_f32[n].reshape(sm_rows[n].shape)) for n in SMALL], got[-1],
                                   name="adamw_small")
    res = {n: [a.reshape(w_f32[n].shape) for a in small_res[i]] for i, n in enumerate(SMALL)}
    for n in SHARDED:
        res[n] = _adamw(parts[n], w_f32[n], m_f32[n], v_f32[n], name="adamw_" + n)
    order = ("mix_norm_g", "w_in", "q_norm_g", "k_norm_g", "pool_w", "pool_scale", "w_out", "cross_norm_g",
             "mem_norm_g", "w_cq", "w_ckv", "cq_norm_g", "ck_norm_g", "w_co", "ffn_norm_g", "w_gate_up", "w_down")
    outs = [loss, dx[None]]
    for which in range(4):
        outs += [res[n][which][None] for n in order]
    return tuple(outs)
```

```python
import functools

import jax
import jax.numpy as jnp
from jax import lax
from jax.experimental import pallas as pl
from jax.experimental.pallas import tpu as pltpu

F32 = jnp.float32
BF16 = jnp.bfloat16
SDS = jax.ShapeDtypeStruct

D_MODEL = 1024
HEAD_DIM = 128
N_GROUPS = 3
DILATIONS = (1, 4, 16)
ATT_HEADS = 4
Q_W = 1536
KV_W = 512
POOL_W = 512
POOL_WINDOWS = (2, 4, 8, 16)
IN_W = 3072
X_W = 512
N_MEM = 256
D_FF = 2816
FF_TILE = 768
FF_SHARD = 704
FF_PAD = 4 * FF_TILE
DOWN_SHARD = 352
ROT_DIM = 32
ROT_HALF = 16
ROPE_THETA = 500000.0
EPS = 1e-6
NEG_INF = -1e30
SCALE = HEAD_DIM ** -0.5
BLOCK = 128
HALO = 16

ADAM_LR = 0.001
ADAM_B1 = 0.9
ADAM_B2 = 0.999
ADAM_EPS = 1e-08
ADAM_WD = 0.01
ADAM_STEP = 10

N_DEV = 8
VMEM_LIMIT_BYTES = 56 * 1024 * 1024
MESH = pl.DeviceIdType.MESH


def _pcall(body, **kw):
    return pl.pallas_call(body, **kw)


def _cp():
    return pltpu.CompilerParams(vmem_limit_bytes=VMEM_LIMIT_BYTES)


def _dot(a, b):
    return lax.dot_general(a, b, (((1,), (0,)), ((), ())), preferred_element_type=F32)


def _dot_nt(a, b):
    return lax.dot_general(a, b, (((1,), (1,)), ((), ())), preferred_element_type=F32)


def _dot_tn(a, b):
    return lax.dot_general(a, b, (((0,), (0,)), ((), ())), preferred_element_type=F32)


def _rows(s):
    return min(512, s)


def _rms_r(x):
    return lax.rsqrt(jnp.mean(x * x, axis=-1, keepdims=True) + EPS)


def _norm_bwd(x, r, gain, dxn):
    z = dxn * gain
    dx = r * z - x * (r * r * r * jnp.mean(z * x, axis=-1, keepdims=True))
    dgain = jnp.sum(dxn * x * r, axis=0, keepdims=True)
    return dx, dgain


def _split_bf16(t):
    hi = t.astype(BF16)
    return hi, (t - hi.astype(F32)).astype(BF16)


def _lane_sums(t, ones):
    hi, lo = _split_bf16(t)
    return _dot(hi, ones) + _dot(lo, ones)


def _head_r(y, ones):
    return lax.rsqrt(_lane_sums(y * y, ones) * (1.0 / HEAD_DIM) + EPS)


def _head_norm_bwd(y, r, gain, dn, ones):
    z = dn * gain
    dy = r * z - y * (r * r * r * (_dot((z * y).astype(BF16), ones) * (1.0 / HEAD_DIM)))
    return dy, jnp.sum(dn * y * r, axis=0, keepdims=True)


def _swap_matrix():
    src = lax.broadcasted_iota(jnp.int32, (HEAD_DIM, HEAD_DIM), 0)
    dst = lax.broadcasted_iota(jnp.int32, (HEAD_DIM, HEAD_DIM), 1)
    hit = ((dst < ROT_HALF) & (src == dst + ROT_HALF)) | ((dst >= ROT_HALF) & (dst < ROT_DIM) & (src == dst - ROT_HALF))
    return jnp.where(hit, 1.0, 0.0).astype(BF16)


def _partner(t, swap):
    hi, lo = _split_bf16(t)
    return _dot(hi, swap) + _dot(lo, swap)


def _rope(n, cos_t, sin_t, swap):
    return n * cos_t + _partner(n, swap) * sin_t


def _rope_bwd(d, cos_t, sin_t, swap):
    return d * cos_t + _dot((d * sin_t).astype(BF16), swap)


def _resident(shape):
    return pl.BlockSpec(shape, lambda i: (0,) * len(shape), pipeline_mode=pl.Buffered(1))


def _chunks(n, cn):
    return [slice(j * cn, (j + 1) * cn) for j in range(n // cn)]


def _norm_linear(x, gain, w, *, cn, out_dtype, name, shards=None):
    s, k = x.shape
    n = w.shape[1]
    tm = _rows(s)
    names = tuple(shards or ())
    nr = len(names)

    def body(*refs):
        x_ref, g_ref, w_ref = refs[:3]
        y_ref, xn_ref = refs[3 + nr:5 + nr]
        if nr:
            start, relay, finish = _gather_ops(names, refs[3:3 + nr], refs[5 + nr:5 + 2 * nr], refs[5 + 2 * nr:])
            pl.when(pl.program_id(0) == 0)(start)
            pl.when(pl.program_id(0) == _relay_step(s // tm))(relay)
        xv = x_ref[...]
        xn_ref[...] = (xv * _rms_r(xv) * g_ref[...]).astype(BF16)
        for c in _chunks(n, cn):
            y_ref[:, c] = _dot(xn_ref[...], w_ref[:, c]).astype(out_dtype)
        if nr:
            pl.when(pl.program_id(0) == s // tm - 1)(finish)

    row = lambda w_: pl.BlockSpec((tm, w_), lambda i: (i, 0))
    outs = _pcall(
        body, name=name, grid=(s // tm,),
        in_specs=[row(k), _resident((1, k)), _resident((k, n))] + [HBM_SPEC] * nr,
        out_specs=[row(n), row(k)] + [HBM_SPEC] * nr,
        out_shape=[SDS((s, n), out_dtype), SDS((s, k), BF16)] + [SDS(FULL[a][0], BF16) for a in names],
        scratch_shapes=_gather_scratch(names),
        compiler_params=_cp())(x, gain, w, *[shards[a] for a in names])
    return (outs[0], outs[1], dict(zip(names, outs[2:]))) if nr else tuple(outs)


def _norm_linear_swiglu(x, gain, wgu, *, name):
    s, k = x.shape
    tm = _rows(s)

    def body(x_ref, g_ref, w_ref, gu_ref, a_ref, xn_ref):
        xv = x_ref[...]
        xn_ref[...] = (xv * _rms_r(xv) * g_ref[...]).astype(BF16)
        for c in _chunks(FF_PAD, FF_TILE):
            g = _dot(xn_ref[...], w_ref[:, c])
            u = _dot(xn_ref[...], w_ref[:, slice(FF_PAD + c.start, FF_PAD + c.stop)])
            a_ref[:, c] = (g * jax.nn.sigmoid(g) * u).astype(BF16)
            gu_ref[0, :, c] = g.astype(BF16)
            gu_ref[1, :, c] = u.astype(BF16)

    row = lambda w_: pl.BlockSpec((tm, w_), lambda i: (i, 0))
    return _pcall(
        body, name=name, grid=(s // tm,),
        in_specs=[row(k), _resident((1, k)), _resident((k, 2 * FF_PAD))],
        out_specs=[pl.BlockSpec((2, tm, FF_PAD), lambda i: (0, i, 0)), row(FF_PAD), row(k)],
        out_shape=[SDS((2, s, FF_PAD), BF16), SDS((s, FF_PAD), BF16), SDS((s, k), BF16)],
        compiler_params=_cp())(x, gain, wgu)


def _linear_res(a, w, res, *, cn, name):
    s, k = a.shape
    n = w.shape[1]
    tm = _rows(s)

    def body(a_ref, w_ref, r_ref, y_ref):
        for c in _chunks(n, cn):
            y_ref[:, c] = r_ref[:, c] + _dot(a_ref[...], w_ref[:, c])

    row = lambda w_: pl.BlockSpec((tm, w_), lambda i: (i, 0))
    return _pcall(
        body, name=name, grid=(s // tm,),
        in_specs=[row(k), _resident((k, n)), row(n)], out_specs=row(n),
        out_shape=SDS((s, n), F32), compiler_params=_cp())(a, w, res)


def _linear_res_loss(a, w, res, tgt, *, name):
    s, k = a.shape
    n = w.shape[1]
    tm = _rows(s)

    def body(a_ref, w_ref, r_ref, t_ref, dy_ref, dyb_ref, sq_ref):
        e = r_ref[...] + _dot(a_ref[...], w_ref[...]) - t_ref[...]
        dy = e * (1.0 / n)
        dy_ref[...] = dy
        dyb_ref[...] = dy.astype(BF16)

        @pl.when(pl.program_id(0) == 0)
        def _():
            sq_ref[...] = jnp.zeros_like(sq_ref)
        sq_ref[...] += jnp.sum(e * e, axis=0, keepdims=True)

    row = lambda w_: pl.BlockSpec((tm, w_), lambda i: (i, 0))
    return _pcall(
        body, name=name, grid=(s // tm,),
        in_specs=[row(k), _resident((k, n)), row(n), row(n)],
        out_specs=[row(n), row(n), pl.BlockSpec((1, n), lambda i: (0, 0))],
        out_shape=[SDS((s, n), F32), SDS((s, n), BF16), SDS((1, n), F32)],
        compiler_params=_cp())(a, w, res, tgt)


def _linear_nt(g, w, *, cn, name):
    s, k = g.shape
    n = w.shape[0]
    tm = _rows(s)

    def body(g_ref, w_ref, y_ref):
        for c in _chunks(n, cn):
            y_ref[:, c] = _dot_nt(g_ref[...], w_ref[c, :]).astype(BF16)

    row = lambda w_: pl.BlockSpec((tm, w_), lambda i: (i, 0))
    return _pcall(
        body, name=name, grid=(s // tm,),
        in_specs=[row(k), _resident((n, k))], out_specs=row(n),
        out_shape=SDS((s, n), BF16), compiler_params=_cp())(g, w)


def _swiglu_bwd(dyb, wd, gu, *, name):
    s, n = dyb.shape
    tm = _rows(s)

    def body(dy_ref, wd_ref, gu_ref, dgu_ref):
        for c in _chunks(FF_PAD, FF_TILE):
            da = _dot_nt(dy_ref[...], wd_ref[c, :])
            g = gu_ref[0, :, c].astype(F32)
            u = gu_ref[1, :, c].astype(F32)
            sg = jax.nn.sigmoid(g)
            dgu_ref[0, :, c] = (da * u * (sg * (1.0 + g * (1.0 - sg)))).astype(BF16)
            dgu_ref[1, :, c] = (da * (g * sg)).astype(BF16)

    half = pl.BlockSpec((2, tm, FF_PAD), lambda i: (0, i, 0))
    return _pcall(
        body, name=name, grid=(s // tm,),
        in_specs=[pl.BlockSpec((tm, n), lambda i: (i, 0)), _resident((FF_PAD, n)), half],
        out_specs=half, out_shape=SDS((2, s, FF_PAD), BF16), compiler_params=_cp())(dyb, wd, gu)


def _linear_nt_normbwd(g, w, x, dres, gain, *, name, grads=None):
    d, k = w.shape
    s = x.shape[0]
    tm = _rows(s)
    names = tuple(grads or ())
    nr = len(names)

    def body(*refs):
        g_ref, w_ref, x_ref, dr_ref, gn_ref = refs[:5]
        dx_ref, dxb_ref, dg_ref = refs[5 + nr:8 + nr]
        if nr:
            copies = _exchange_copies(names, refs[5:5 + nr], refs[8 + nr:8 + 2 * nr], *refs[8 + 2 * nr:])

        @pl.when(pl.program_id(0) == 0)
        def _():
            dg_ref[...] = jnp.zeros_like(dg_ref)
            if nr:
                _start(copies)

        if g.ndim == 3:
            dxn = _dot_nt(g_ref[0], w_ref[:, :k // 2]) + _dot_nt(g_ref[1], w_ref[:, k // 2:])
        else:
            dxn = _dot_nt(g_ref[...], w_ref[...])
        xv = x_ref[...]
        dx, dgain = _norm_bwd(xv, _rms_r(xv), gn_ref[...], dxn)
        out = dr_ref[...] + dx
        dx_ref[...] = out
        dxb_ref[...] = out.astype(BF16)
        dg_ref[...] += dgain

        if nr:
            @pl.when(pl.program_id(0) == s // tm - 1)
            def _():
                _finish(copies)

    row = pl.BlockSpec((tm, d), lambda i: (i, 0))
    g_spec = (pl.BlockSpec((2, tm, k // 2), lambda i: (0, i, 0)) if g.ndim == 3
              else pl.BlockSpec((tm, k), lambda i: (i, 0)))
    outs = _pcall(
        body, name=name, grid=(s // tm,),
        in_specs=[g_spec, _resident((d, k)), row, row, _resident((1, d))] + [HBM_SPEC] * nr,
        out_specs=[row, row, pl.BlockSpec((1, d), lambda i: (0, 0))] + [HBM_SPEC] * nr,
        out_shape=[SDS((s, d), F32), SDS((s, d), BF16), SDS((1, d), F32)]
        + [SDS((N_DEV,) + _shard_shape(n), BF16) for n in names],
        scratch_shapes=_comm_sems(nr) if nr else [],
        compiler_params=_cp())(g, w, x, dres, gain, *[grads[n] for n in names])
    return (outs[0], outs[1], outs[2], dict(zip(names, outs[3:]))) if nr else tuple(outs)


def _dw_tn(x, g, *, tkw, tn, name):
    s, kw = x.shape
    halves = g.ndim == 3
    n = 2 * g.shape[2] if halves else g.shape[1]
    ts = min(2048, s)
    ns = s // ts
    per_half = n // 2 // tn

    def body(x_ref, g_ref, o_ref, acc_ref):
        ss = pl.program_id(2)

        @pl.when(ss == 0)
        def _():
            acc_ref[...] = jnp.zeros_like(acc_ref)

        acc_ref[...] += _dot_tn(x_ref[...], g_ref[...])

        @pl.when(ss == ns - 1)
        def _():
            o_ref[...] = acc_ref[...].astype(BF16)

    g_spec = (pl.BlockSpec((None, ts, tn), lambda a, b, ss: (b // per_half, ss, b % per_half)) if halves
              else pl.BlockSpec((ts, tn), lambda a, b, ss: (ss, b)))
    return _pcall(
        body, name=name, grid=(kw // tkw, n // tn, ns),
        in_specs=[pl.BlockSpec((ts, tkw), lambda a, b, ss: (ss, a)), g_spec],
        out_specs=pl.BlockSpec((tkw, tn), lambda a, b, ss: (a, b)),
        out_shape=SDS((kw, n), BF16),
        scratch_shapes=[pltpu.VMEM((tkw, tn), F32)], compiler_params=_cp())(x, g)


def _qk_prep(proj, cos_t, sin_t, qg, kg, *, name):
    s = proj.shape[0]
    tm = _rows(s)
    nqh = Q_W // HEAD_DIM

    def body(q_ref, k_ref, v_ref, c_ref, s_ref, qg_ref, kg_ref, qo_ref, ko_ref, vo_ref):
        c, sn = c_ref[...], s_ref[...]
        ones, swap = jnp.ones((HEAD_DIM, HEAD_DIM), BF16), _swap_matrix()
        for h in range(nqh):
            sl = slice(h * HEAD_DIM, (h + 1) * HEAD_DIM)
            y = q_ref[:, sl]
            qo_ref[:, sl] = (_rope(y * _head_r(y, ones) * qg_ref[...], c, sn, swap) * SCALE).astype(BF16)
        for h in range(ATT_HEADS):
            sl = slice(h * HEAD_DIM, (h + 1) * HEAD_DIM)
            y = k_ref[:, sl]
            ko_ref[:, sl] = _rope(y * _head_r(y, ones) * kg_ref[...], c, sn, swap).astype(BF16)
        vo_ref[...] = v_ref[...].astype(BF16)

    row = lambda w, j: pl.BlockSpec((tm, w), lambda i: (i, j))
    one = pl.BlockSpec((1, HEAD_DIM), lambda i: (0, 0))
    return _pcall(
        body, name=name, grid=(s // tm,),
        in_specs=[row(Q_W, 0), row(KV_W, 3), row(KV_W, 4), row(HEAD_DIM, 0), row(HEAD_DIM, 0), one, one],
        out_specs=[row(Q_W, 0), row(KV_W, 0), row(KV_W, 0)],
        out_shape=[SDS((s, Q_W), BF16), SDS((s, KV_W), BF16), SDS((s, KV_W), BF16)],
        compiler_params=_cp())(proj, proj, proj, cos_t, sin_t, qg, kg)


def _qk_prep_bwd(proj, dq, dk, dv, du, cos_t, sin_t, qg, kg, *, name):
    s = proj.shape[0]
    tm = _rows(s)

    def body(q_ref, k_ref, dq_ref, dk_ref, dv_ref, du_ref, c_ref, s_ref, qg_ref, kg_ref, dp_ref, dqg_ref, dkg_ref):
        c, sn = c_ref[...], s_ref[...]

        @pl.when(pl.program_id(0) == 0)
        def _():
            dqg_ref[...] = jnp.zeros_like(dqg_ref)
            dkg_ref[...] = jnp.zeros_like(dkg_ref)

        ones, swap = jnp.ones((HEAD_DIM, HEAD_DIM), BF16), _swap_matrix()
        dqg = jnp.zeros((1, HEAD_DIM), F32)
        for g in range(N_GROUPS):
            for h in range(ATT_HEADS):
                sl = slice(h * HEAD_DIM, (h + 1) * HEAD_DIM)
                col = slice(g * KV_W + h * HEAD_DIM, g * KV_W + (h + 1) * HEAD_DIM)
                y = q_ref[:, col]
                dn = _rope_bwd(dq_ref[g, :, sl].astype(F32) * SCALE, c, sn, swap)
                dy, dgain = _head_norm_bwd(y, _head_r(y, ones), qg_ref[...], dn, ones)
                dp_ref[:, col] = dy.astype(BF16)
                dqg = dqg + dgain
        dqg_ref[...] += dqg

        dkg = jnp.zeros((1, HEAD_DIM), F32)
        for h in range(ATT_HEADS):
            sl = slice(h * HEAD_DIM, (h + 1) * HEAD_DIM)
            y = k_ref[:, sl]
            dn = _rope_bwd(dk_ref[:, sl], c, sn, swap)
            dy, dgain = _head_norm_bwd(y, _head_r(y, ones), kg_ref[...], dn, ones)
            dp_ref[:, Q_W + h * HEAD_DIM:Q_W + (h + 1) * HEAD_DIM] = dy.astype(BF16)
            dkg = dkg + dgain
        dkg_ref[...] += dkg

        dp_ref[:, Q_W + KV_W:Q_W + 2 * KV_W] = dv_ref[...].astype(BF16)
        dp_ref[:, Q_W + 2 * KV_W:] = du_ref[...].astype(BF16)

    row = lambda w, j: pl.BlockSpec((tm, w), lambda i: (i, j))
    one = pl.BlockSpec((1, HEAD_DIM), lambda i: (0, 0))
    return _pcall(
        body, name=name, grid=(s // tm,),
        in_specs=[row(Q_W, 0), row(KV_W, 3), pl.BlockSpec((N_GROUPS, tm, KV_W), lambda i: (0, i, 0))]
        + [row(KV_W, 0)] * 3 + [row(HEAD_DIM, 0), row(HEAD_DIM, 0), one, one],
        out_specs=[row(IN_W, 0), one, one],
        out_shape=[SDS((s, IN_W), BF16), SDS((1, HEAD_DIM), F32), SDS((1, HEAD_DIM), F32)],
        compiler_params=_cp())(proj, proj, dq, dk, dv, du, cos_t, sin_t, qg, kg)


ATT_ROWS = 16 * BLOCK


def _sub(ref, start, d, size=BLOCK):
    return ref[pl.ds(start, size, stride=d), :] if d > 1 else ref[pl.ds(start, size), :]


def _sub_set(ref, start, d, val):
    if d > 1:
        ref[pl.ds(start, BLOCK, stride=d), :] = val
    else:
        ref[pl.ds(start, BLOCK), :] = val


def _band_masks():
    row = lax.broadcasted_iota(jnp.int32, (BLOCK, BLOCK), 0)
    col = lax.broadcasted_iota(jnp.int32, (BLOCK, BLOCK), 1)
    return col <= row, col >= row


def _eye(n=BLOCK):
    row = lax.broadcasted_iota(jnp.int32, (n, n), 0)
    col = lax.broadcasted_iota(jnp.int32, (n, n), 1)
    return jnp.where(row == col, 1.0, 0.0).astype(BF16)


def _attn_fwd(q_rot, k_rot, v, shards, *, name):
    s = q_rot.shape[0]
    rr = ATT_ROWS
    nblk = s // rr
    names = tuple(shards)
    nr = len(names)

    def body(*refs):
        q0, q1, q2, kp, kc, vp, vc = refs[:7]
        mix_ref, lse_ref = refs[7 + nr:9 + nr]
        qs, ks, vs, os_, ls = refs[9 + 2 * nr:14 + 2 * nr]
        h, n = pl.program_id(0), pl.program_id(1)
        start, relay, finish = _gather_ops(names, refs[7:7 + nr], refs[9 + nr:9 + 2 * nr], refs[14 + 2 * nr:])
        pl.when((h == 0) & (n == 0))(start)
        steps = ATT_HEADS * nblk
        early, late = max(1, (steps * 6) // 16), max(1, (steps * 14) // 16)
        pl.when(h * nblk + n == early)(functools.partial(relay, names[:1]))
        pl.when(h * nblk + n == late)(functools.partial(relay, names[1:]))
        for g, q_ref in enumerate((q0, q1, q2)):
            qs[g] = q_ref[...].astype(F32)
        ks[:rr] = kp[...].astype(F32)
        ks[rr:] = kc[...].astype(F32)
        vs[:rr] = vp[...].astype(F32)
        vs[rr:] = vc[...].astype(F32)
        m_cur, m_band = _band_masks()
        mask_in = jnp.concatenate([m_band, m_cur], axis=1)
        mask_first = jnp.concatenate([m_band & (n > 0), m_cur], axis=1)
        ones = jnp.ones((2 * BLOCK, HEAD_DIM), BF16)
        pieces = [(g, d, j * BLOCK * d + r, j) for g, d in enumerate(DILATIONS) for r in range(d)
                  for j in range(rr // (BLOCK * d))]

        def scores(piece):
            g, d, base, j = piece
            q = _sub(qs.at[g], base, d).astype(BF16)
            k2 = _sub(ks, rr + base - BLOCK * d, d, 2 * BLOCK).astype(BF16)
            return jnp.where(mask_first if j == 0 else mask_in, _dot_nt(q, k2), NEG_INF)

        sc = scores(pieces[0])
        for i, (g, d, base, j) in enumerate(pieces):
            cur = sc
            if i + 1 < len(pieces):
                sc = scores(pieces[i + 1])
            m = jnp.max(cur, axis=-1, keepdims=True)
            p = jnp.exp(cur - m).astype(BF16)
            v2 = _sub(vs, rr + base - BLOCK * d, d, 2 * BLOCK).astype(BF16)
            acc_l = _dot(p, jnp.concatenate([v2, ones], axis=1))
            l = acc_l[:, HEAD_DIM:]
            _sub_set(os_.at[g], base, d, acc_l[:, :HEAD_DIM] / l)
            _sub_set(ls.at[g], base, d, m + jnp.log(l))
        for c in _chunks(rr, 2 * BLOCK):
            a, b, cc = ls[0, c, :], ls[1, c, :], ls[2, c, :]
            m = jnp.maximum(jnp.maximum(a, b), cc)
            wa, wb, wc = jnp.exp(a - m), jnp.exp(b - m), jnp.exp(cc - m)
            den = wa + wb + wc
            mix_ref[c, :] = ((wa * os_[0, c, :] + wb * os_[1, c, :] + wc * os_[2, c, :]) / den).astype(BF16)
            lse_ref[c, :] = m + jnp.log(den)

        pl.when((h == ATT_HEADS - 1) & (n == nblk - 1))(finish)

    blk = lambda f: pl.BlockSpec((rr, HEAD_DIM), f)
    prv = lambda n: jnp.maximum(n - 1, 0)
    outs = _pcall(
        body, name=name, grid=(ATT_HEADS, nblk),
        in_specs=[blk(lambda h, n, g=g: (n, g * ATT_HEADS + h)) for g in range(N_GROUPS)]
        + [blk(lambda h, n: (prv(n), h)), blk(lambda h, n: (n, h))] * 2 + [HBM_SPEC] * nr,
        out_specs=[blk(lambda h, n: (n, h)), blk(lambda h, n: (n, h))] + [HBM_SPEC] * nr,
        out_shape=[SDS((s, KV_W + POOL_W), BF16), SDS((s, KV_W), F32)] + [SDS(FULL[w][0], BF16) for w in names],
        scratch_shapes=[pltpu.VMEM((N_GROUPS, rr, HEAD_DIM), F32), pltpu.VMEM((2 * rr, HEAD_DIM), F32),
                        pltpu.VMEM((2 * rr, HEAD_DIM), F32), pltpu.VMEM((N_GROUPS, rr, HEAD_DIM), F32),
                        pltpu.VMEM((N_GROUPS, rr, HEAD_DIM), F32)] + _gather_scratch(names),
        compiler_params=_cp())(q_rot, q_rot, q_rot, k_rot, k_rot, v, v, *[shards[w] for w in names])
    return outs[0], outs[1], dict(zip(names, outs[2:]))


def _attn_bwd(q_rot, k_rot, v, mix, dmix, lse, grads, *, name):
    s = q_rot.shape[0]
    rr = ATT_ROWS
    nblk = s // rr
    names = tuple(grads)
    nr = len(names)

    def body(*refs):
        q0, q1, q2, qx0, qx1, qx2, kp, kc, vp, vc, do_c, do_x, o_c, o_x, l_c, l_x = refs[:16]
        dq_ref, dk_ref, dv_ref = refs[16 + nr:19 + nr]
        qs, ks, vs, dos, lss, dls, dqs, dks, dvs, send_sems, recv_sems, local_sems = refs[19 + 2 * nr:]
        h, n = pl.program_id(0), pl.program_id(1)
        copies = _exchange_copies(names, refs[16:16 + nr], refs[19 + nr:19 + 2 * nr], send_sems, recv_sems, local_sems)

        @pl.when((h == 0) & (n == 0))
        def _():
            _start(copies)

        for g, (qc_ref, qx_ref) in enumerate(((q0, qx0), (q1, qx1), (q2, qx2))):
            qs[g, :rr] = qc_ref[...].astype(F32)
            qs[g, rr:] = qx_ref[...].astype(F32)
        ks[:rr] = kp[...].astype(F32)
        ks[rr:] = kc[...].astype(F32)
        vs[:rr] = vp[...].astype(F32)
        vs[rr:] = vc[...].astype(F32)
        lss[:rr] = l_c[...]
        lss[rr:] = l_x[...]
        for half, (d_ref, o_ref) in enumerate(((do_c, o_c), (do_x, o_x))):
            for c in _chunks(rr, 2 * BLOCK):
                cs = slice(half * rr + c.start, half * rr + c.stop)
                dof = d_ref[c, :].astype(F32)
                dos[cs, :] = dof
                dls[cs, :] = jnp.broadcast_to(jnp.sum(dof * o_ref[c, :].astype(F32), axis=-1, keepdims=True),
                                              (2 * BLOCK, HEAD_DIM))
        m_cur, m_band = _band_masks()
        mask_in = jnp.concatenate([m_cur, m_band], axis=0)
        mask_last = jnp.concatenate([m_cur, m_band & (n + 1 < nblk)], axis=0)
        m_first = m_band & (n > 0)
        eye = _eye()
        pieces = [(g, d, j * BLOCK * d + r, j, rr // (BLOCK * d)) for g, d in enumerate(DILATIONS) for r in range(d)
                  for j in range(rr // (BLOCK * d))]

        def front(piece):
            g, d, base, _, _ = piece
            q2 = _sub(qs.at[g], base, d, 2 * BLOCK).astype(BF16)
            do2 = _sub(dos, base, d, 2 * BLOCK).astype(BF16)
            k = _sub(ks, rr + base, d).astype(BF16)
            vv = _sub(vs, rr + base, d).astype(BF16)
            return q2, do2, k, _dot_nt(q2, k), _dot_nt(do2, vv)

        def middle(piece, fr, dq_acc):
            g, d, base, j, nsub = piece
            q2, do2, k, s2, dp2 = fr
            if j == 0:
                kp_ = _sub(ks, rr + base - BLOCK * d, d).astype(BF16)
                p0 = jnp.where(m_first, jnp.exp(_dot_nt(q2[:BLOCK], kp_) - _sub(lss, base, d)), 0.0)
                ds0 = p0 * (_dot_nt(do2[:BLOCK], _sub(vs, rr + base - BLOCK * d, d).astype(BF16)) - _sub(dls, base, d))
                dq_acc = _dot(ds0.astype(BF16), kp_)
            p2 = jnp.where(mask_last if j + 1 == nsub else mask_in,
                           jnp.exp(s2 - _sub(lss, base, d, 2 * BLOCK)), 0.0)
            ds2 = (p2 * (dp2 - _sub(dls, base, d, 2 * BLOCK))).astype(BF16)
            dq2 = _dot(ds2, k)
            return dq2[BLOCK:], (_dot_nt(eye, ds2), _dot_nt(eye, p2.astype(BF16)), q2, do2, dq_acc + dq2[:BLOCK])

        def back(piece, tr):
            g, d, base, _, _ = piece
            ds_t, p_t, q2, do2, dq = tr
            _sub_set(dqs.at[g], base, d, dq)
            dk, dv = _dot(ds_t.astype(BF16), q2), _dot(p_t.astype(BF16), do2)
            if g == 0:
                _sub_set(dks, base, d, dk)
                _sub_set(dvs, base, d, dv)
            else:
                _sub_set(dks, base, d, _sub(dks, base, d) + dk)
                _sub_set(dvs, base, d, _sub(dvs, base, d) + dv)

        fr, held, dq_acc = front(pieces[0]), None, None
        for i, piece in enumerate(pieces):
            cur = fr
            if i + 1 < len(pieces):
                fr = front(pieces[i + 1])
            dq_acc, now = middle(piece, cur, dq_acc)
            if held is not None:
                back(pieces[i - 1], held)
            held = now
        back(pieces[-1], held)
        for g in range(N_GROUPS):
            dq_ref[g] = dqs[g].astype(BF16)
        dk_ref[...] = dks[...]
        dv_ref[...] = dvs[...]

        @pl.when((h == ATT_HEADS - 1) & (n == nblk - 1))
        def _():
            _finish(copies)

    blk = lambda f: pl.BlockSpec((rr, HEAD_DIM), f)
    prv = lambda n: jnp.maximum(n - 1, 0)
    nxt = lambda n: jnp.minimum(n + 1, nblk - 1)
    cur_kv = blk(lambda h, n: (n, h))
    outs = _pcall(
        body, name=name, grid=(ATT_HEADS, nblk),
        in_specs=[blk(lambda h, n, g=g: (n, g * ATT_HEADS + h)) for g in range(N_GROUPS)]
        + [blk(lambda h, n, g=g: (nxt(n), g * ATT_HEADS + h)) for g in range(N_GROUPS)]
        + [blk(lambda h, n: (prv(n), h)), cur_kv] * 2
        + [cur_kv, blk(lambda h, n: (nxt(n), h))] * 3 + [HBM_SPEC] * nr,
        out_specs=[pl.BlockSpec((N_GROUPS, rr, HEAD_DIM), lambda h, n: (0, n, h)), cur_kv, cur_kv] + [HBM_SPEC] * nr,
        out_shape=[SDS((N_GROUPS, s, KV_W), BF16), SDS((s, KV_W), F32), SDS((s, KV_W), F32)]
        + [SDS((N_DEV,) + _shard_shape(w), BF16) for w in names],
        scratch_shapes=[pltpu.VMEM((N_GROUPS, 2 * rr, HEAD_DIM), F32)] + [pltpu.VMEM((2 * rr, HEAD_DIM), F32)] * 5
        + [pltpu.VMEM((N_GROUPS, rr, HEAD_DIM), F32), pltpu.VMEM((rr, HEAD_DIM), F32), pltpu.VMEM((rr, HEAD_DIM), F32)]
        + _comm_sems(nr),
        compiler_params=_cp())(q_rot, q_rot, q_rot, q_rot, q_rot, q_rot, k_rot, k_rot, v, v, dmix, dmix, mix, mix,
                               lse, lse, *[grads[w] for w in names])
    return outs[0], outs[1], outs[2], dict(zip(names, outs[3:]))


def _pool_d(u_ref, halo_ref, i, tm):
    halo = jnp.where(i > 0, halo_ref[...], 0.0)
    t = i * tm + lax.broadcasted_iota(jnp.int32, (tm, 1), 0)
    out = []
    for g, w in enumerate(POOL_WINDOWS):
        sl = slice(g * HEAD_DIM, (g + 1) * HEAD_DIM)
        u = u_ref[:, sl]
        acc = jnp.concatenate([halo[:, sl], u], axis=0)
        sh = 1
        while sh < w:
            acc = acc + pltpu.roll(acc, sh, 0)
            sh *= 2
        cnt = jnp.minimum(t + 1, w).astype(F32)
        out.append(acc[HALO:, :] / cnt - u)
    return out


def _pool_fwd(proj, mix, pool_w, pool_scale, *, name):
    s = proj.shape[0]
    tm = _rows(s)
    ucol = (IN_W - POOL_W) // POOL_W

    def body(u_ref, halo_ref, mix_in, w_ref, sc_ref, o_ref):
        del mix_in
        dd = _pool_d(u_ref, halo_ref, pl.program_id(0), tm)
        for g in range(len(POOL_WINDOWS)):
            sl = slice(g * HEAD_DIM, (g + 1) * HEAD_DIM)
            y = _dot(dd[g].astype(BF16), w_ref[g].astype(BF16))
            o_ref[:, sl] = (y * sc_ref[:, sl]).astype(BF16)

    return _pcall(
        body, name=name, grid=(s // tm,),
        in_specs=[pl.BlockSpec((tm, POOL_W), lambda i: (i, ucol)),
                  pl.BlockSpec((HALO, POOL_W), lambda i: (jnp.maximum(i * (tm // HALO) - 1, 0), ucol)),
                  pl.BlockSpec(memory_space=pl.ANY),
                  pl.BlockSpec((len(POOL_WINDOWS), HEAD_DIM, HEAD_DIM), lambda i: (0, 0, 0)),
                  pl.BlockSpec((1, POOL_W), lambda i: (0, 0))],
        out_specs=pl.BlockSpec((tm, POOL_W), lambda i: (i, 1)),
        out_shape=SDS(mix.shape, BF16), input_output_aliases={2: 0},
        compiler_params=_cp())(proj, proj, mix, pool_w, pool_scale)


def _pool_bwd(proj, dmix, pool_w, pool_scale, *, name):
    s = proj.shape[0]
    tm = _rows(s)
    nblk = s // tm
    ucol = (IN_W - POOL_W) // POOL_W
    ng = len(POOL_WINDOWS)

    def body(u_ref, halo_ref, dp_ref, dpn_ref, w_ref, sc_ref, du_ref, dw_ref, dsc_ref):
        i = pl.program_id(0)

        @pl.when(i == 0)
        def _():
            dw_ref[...] = jnp.zeros_like(dw_ref)
            dsc_ref[...] = jnp.zeros_like(dsc_ref)

        dd = _pool_d(u_ref, halo_ref, i, tm)
        t = i * tm + lax.broadcasted_iota(jnp.int32, (tm, 1), 0)
        dpn = jnp.where(i + 1 < nblk, dpn_ref[...].astype(F32), 0.0)
        for g, w in enumerate(POOL_WINDOWS):
            sl = slice(g * HEAD_DIM, (g + 1) * HEAD_DIM)
            wg = w_ref[g].astype(BF16)
            db = dd[g].astype(BF16)
            dp = dp_ref[:, sl].astype(F32)
            dsc_ref[:, sl] += jnp.sum(dp * _dot(db, wg), axis=0, keepdims=True)
            dy = (dp * sc_ref[:, sl]).astype(BF16)
            dw_ref[g] += _dot_tn(db, dy)
            g_d = _dot_nt(dy, wg)
            g_dn = _dot_nt((dpn[:, sl] * sc_ref[:, sl]).astype(BF16), wg)
            cnt = jnp.minimum(t + 1, w).astype(F32)
            acc = jnp.concatenate([g_d / cnt, g_dn * (1.0 / w)], axis=0)
            sh = 1
            while sh < w:
                acc = acc + pltpu.roll(acc, tm + HALO - sh, 0)
                sh *= 2
            du_ref[:, sl] = acc[:tm, :] - g_d

    nh = s // HALO
    return _pcall(
        body, name=name, grid=(nblk,),
        in_specs=[pl.BlockSpec((tm, POOL_W), lambda i: (i, ucol)),
                  pl.BlockSpec((HALO, POOL_W), lambda i: (jnp.maximum(i * (tm // HALO) - 1, 0), ucol)),
                  pl.BlockSpec((tm, POOL_W), lambda i: (i, 1)),
                  pl.BlockSpec((HALO, POOL_W), lambda i: (jnp.minimum((i + 1) * (tm // HALO), nh - 1), 1)),
                  pl.BlockSpec((ng, HEAD_DIM, HEAD_DIM), lambda i: (0, 0, 0)),
                  pl.BlockSpec((1, POOL_W), lambda i: (0, 0))],
        out_specs=[pl.BlockSpec((tm, POOL_W), lambda i: (i, 0)),
                   pl.BlockSpec((ng, HEAD_DIM, HEAD_DIM), lambda i: (0, 0, 0)),
                   pl.BlockSpec((1, POOL_W), lambda i: (0, 0))],
        out_shape=[SDS((s, POOL_W), F32), SDS((ng, HEAD_DIM, HEAD_DIM), F32), SDS((1, POOL_W), F32)],
        compiler_params=_cp())(proj, proj, dmix, dmix, pool_w, pool_scale)


def _mem_fwd(mem, mem_g, wckv, ck_g, *, name):
    def body(m_ref, g_ref, w_ref, kg_ref, mn_ref, ckr_ref, ckn_ref, cv_ref):
        mv = m_ref[...]
        mn = (mv * _rms_r(mv) * g_ref[...]).astype(BF16)
        mn_ref[...] = mn
        ckv = _dot(mn, w_ref[...])
        ckr_ref[...] = ckv[:, :X_W]
        cv_ref[...] = ckv[:, X_W:].astype(BF16)
        for h in range(ATT_HEADS):
            sl = slice(h * HEAD_DIM, (h + 1) * HEAD_DIM)
            y = ckv[:, sl]
            ckn_ref[:, sl] = (y * _rms_r(y) * kg_ref[...]).astype(BF16)

    return _pcall(
        body, name=name,
        out_shape=[SDS((N_MEM, D_MODEL), BF16), SDS((N_MEM, X_W), F32), SDS((N_MEM, X_W), BF16),
                   SDS((N_MEM, X_W), BF16)],
        compiler_params=_cp())(mem, mem_g, wckv, ck_g)


def _cross_q(cq_ref, g_ref, sl):
    y = cq_ref[:, sl]
    r = _rms_r(y)
    return y, r, y * r * g_ref[...] * SCALE


def _cross_fwd(h1, gain, wcq, ck_n, cv, cq_g, wco, *, name):
    s, d = h1.shape
    tm = _rows(s)

    def body(x_ref, gn_ref, wq_ref, k_ref, v_ref, g_ref, wo_ref, h2_ref, xn_ref, cq_ref, o_ref):
        xv = x_ref[...]
        xn_ref[...] = (xv * _rms_r(xv) * gn_ref[...]).astype(BF16)
        cq_ref[...] = _dot(xn_ref[...], wq_ref[...])
        for h in range(ATT_HEADS):
            sl = slice(h * HEAD_DIM, (h + 1) * HEAD_DIM)
            _, _, qn = _cross_q(cq_ref, g_ref, sl)
            sc = _dot_nt(qn.astype(BF16), k_ref[:, sl])
            p = jnp.exp(sc - jnp.max(sc, axis=-1, keepdims=True))
            p = p / jnp.sum(p, axis=-1, keepdims=True)
            o_ref[:, sl] = _dot(p.astype(BF16), v_ref[:, sl]).astype(BF16)
        for c in _chunks(d, X_W):
            h2_ref[:, c] = x_ref[:, c] + _dot(o_ref[...], wo_ref[:, c])

    row = lambda w_: pl.BlockSpec((tm, w_), lambda i: (i, 0))
    return _pcall(
        body, name=name, grid=(s // tm,),
        in_specs=[row(d), _resident((1, d)), _resident(wcq.shape), _resident(ck_n.shape), _resident(cv.shape),
                  _resident(cq_g.shape), _resident(wco.shape)],
        out_specs=[row(d), row(d), row(X_W), row(X_W)],
        out_shape=[SDS((s, d), F32), SDS((s, d), BF16), SDS((s, X_W), F32), SDS((s, X_W), BF16)],
        compiler_params=_cp())(h1, gain, wcq, ck_n, cv, cq_g, wco)


def _cross_bwd(dh2, dh2b, h1, cq_raw, ck_n, cv, cq_g, gain, wcq, wco, *, name):
    s, d = h1.shape
    tm = _rows(s)

    def body(dh2_ref, dh2b_ref, x_ref, cq_ref, k_ref, v_ref, g_ref, gn_ref, wq_ref, wo_ref,
             dcq_ref, dk_ref, dv_ref, dg_ref, dx_ref, dxb_ref, dgn_ref, do_ref):
        @pl.when(pl.program_id(0) == 0)
        def _():
            dk_ref[...] = jnp.zeros_like(dk_ref)
            dv_ref[...] = jnp.zeros_like(dv_ref)
            dg_ref[...] = jnp.zeros_like(dg_ref)
            dgn_ref[...] = jnp.zeros_like(dgn_ref)

        do_ref[...] = _dot_nt(dh2b_ref[...], wo_ref[...]).astype(BF16)

        def front(h):
            sl = slice(h * HEAD_DIM, (h + 1) * HEAD_DIM)
            y, r, qn = _cross_q(cq_ref, g_ref, sl)
            qb = qn.astype(BF16)
            do = do_ref[:, sl]
            return y, r, qb, do, _dot_nt(qb, k_ref[:, sl]), _dot_nt(do, v_ref[:, sl])

        dg = jnp.zeros((1, HEAD_DIM), F32)
        nxt = front(0)
        for h in range(ATT_HEADS):
            sl = slice(h * HEAD_DIM, (h + 1) * HEAD_DIM)
            y, r, qb, do, sc, dp = nxt
            if h + 1 < ATT_HEADS:
                nxt = front(h + 1)
            p = jnp.exp(sc - jnp.max(sc, axis=-1, keepdims=True))
            p = p / jnp.sum(p, axis=-1, keepdims=True)
            dv_ref[:, sl] += _dot_tn(p.astype(BF16), do)
            ds = (p * (dp - jnp.sum(dp * p, axis=-1, keepdims=True))).astype(BF16)
            dk_ref[:, sl] += _dot_tn(ds, qb)
            dn = _dot(ds, k_ref[:, sl]) * SCALE
            dy, dgain = _norm_bwd(y, r, g_ref[...], dn)
            dcq_ref[:, sl] = dy.astype(BF16)
            dg = dg + dgain
        dg_ref[...] += dg

        xv = x_ref[...]
        dx, dgain = _norm_bwd(xv, _rms_r(xv), gn_ref[...], _dot_nt(dcq_ref[...], wq_ref[...]))
        out = dh2_ref[...] + dx
        dx_ref[...] = out
        dxb_ref[...] = out.astype(BF16)
        dgn_ref[...] += dgain

    row = lambda w_: pl.BlockSpec((tm, w_), lambda i: (i, 0))
    acc = lambda r_, w_: pl.BlockSpec((r_, w_), lambda i: (0, 0))
    return _pcall(
        body, name=name, grid=(s // tm,),
        in_specs=[row(d), row(d), row(d), row(X_W), _resident(ck_n.shape), _resident(cv.shape), _resident(cq_g.shape),
                  _resident((1, d)), _resident(wcq.shape), _resident(wco.shape)],
        out_specs=[row(X_W), acc(N_MEM, X_W), acc(N_MEM, X_W), acc(1, HEAD_DIM), row(d), row(d), acc(1, d)],
        out_shape=[SDS((s, X_W), BF16), SDS((N_MEM, X_W), F32), SDS((N_MEM, X_W), F32), SDS((1, HEAD_DIM), F32),
                   SDS((s, d), F32), SDS((s, d), BF16), SDS((1, d), F32)],
        scratch_shapes=[pltpu.VMEM((tm, X_W), BF16)],
        compiler_params=_cp())(dh2, dh2b, h1, cq_raw, ck_n, cv, cq_g, gain, wcq, wco)


def _mem_bwd(dck_n, dcv, ck_raw, memn, mem, wckv, mem_g, ck_g, *, name):
    def body(dk_ref, dv_ref, ckr_ref, mn_ref, m_ref, w_ref, mg_ref, kg_ref, dw_ref, dmg_ref, dkg_ref, dckv_s):
        dkg = jnp.zeros((1, HEAD_DIM), F32)
        for h in range(ATT_HEADS):
            sl = slice(h * HEAD_DIM, (h + 1) * HEAD_DIM)
            y = ckr_ref[:, sl]
            dy, dgain = _norm_bwd(y, _rms_r(y), kg_ref[...], dk_ref[:, sl])
            dckv_s[:, sl] = dy.astype(BF16)
            dkg = dkg + dgain
        dkg_ref[...] = dkg
        dckv_s[:, X_W:] = dv_ref[...].astype(BF16)
        dckv = dckv_s[...]
        dw_ref[...] = _dot_tn(mn_ref[...], dckv).astype(BF16)
        dmn = _dot_nt(dckv, w_ref[...])
        mv = m_ref[...]
        dmg_ref[...] = jnp.sum(dmn * mv * _rms_r(mv), axis=0, keepdims=True)

    return _pcall(
        body, name=name,
        out_shape=[SDS((D_MODEL, 2 * X_W), BF16), SDS((1, D_MODEL), F32), SDS((1, HEAD_DIM), F32)],
        scratch_shapes=[pltpu.VMEM((N_MEM, 2 * X_W), BF16)],
        compiler_params=_cp())(dck_n, dcv, ck_raw, memn, mem, wckv, mem_g, ck_g)


def _rope_tables(pos):
    inv_freq = ROPE_THETA ** (-jnp.arange(0, ROT_DIM, 2, dtype=F32) / ROT_DIM)
    ang = pos.astype(F32)[:, None] * inv_freq
    cos, sin = jnp.cos(ang), jnp.sin(ang)
    s = pos.shape[0]
    rest = HEAD_DIM - ROT_DIM
    cos_t = jnp.concatenate([cos, cos, jnp.ones((s, rest), F32)], axis=1)
    sin_t = jnp.concatenate([-sin, sin, jnp.zeros((s, rest), F32)], axis=1)
    return cos_t, sin_t


def _local_step(x, mem, pos, tgt, w_in, shards, sm):
    cos_t, sin_t = _rope_tables(pos)
    wb = {"w_in": w_in}

    pick = lambda *names: {a: shards[a] for a in names}
    proj, xn, got = _norm_linear(x, sm["mix_norm_g"], wb["w_in"], cn=768, out_dtype=F32, name="fwd_in_proj",
                                 shards=pick("w_out", "w_cq", "w_ckv", "w_co"))
    wb.update(got)
    q_rot, k_rot, v = _qk_prep(proj, cos_t, sin_t, sm["q_norm_g"], sm["k_norm_g"], name="fwd_qk_prep")
    mix, lse_all, got = _attn_fwd(q_rot, k_rot, v, pick("w_down", "w_gate_up"), name="fwd_attn")
    wb.update(got)
    mix = _pool_fwd(proj, mix, sm["pool_w"], sm["pool_scale"], name="fwd_pool")
    h1 = _linear_res(mix, wb["w_out"], x, cn=512, name="fwd_out_proj")
    memn, ck_raw, ck_n, cv = _mem_fwd(mem, sm["mem_norm_g"], wb["w_ckv"], sm["ck_norm_g"], name="fwd_mem")
    h2, hn, cq_raw, co = _cross_fwd(h1, sm["cross_norm_g"], wb["w_cq"], ck_n, cv, sm["cq_norm_g"], wb["w_co"],
                                    name="fwd_cross")
    gu, act, fn = _norm_linear_swiglu(h2, sm["ffn_norm_g"], wb["w_gate_up"], name="fwd_gate_up")
    dy, dyb, sq = _linear_res_loss(act, wb["w_down"], h2, tgt, name="fwd_down_loss")

    gw = {}
    gs = {}
    dgu = _swiglu_bwd(dyb, wb["w_down"], gu, name="bwd_swiglu")
    gw_down = _dw_tn(act, dyb, tkw=1024, tn=1024, name="bwd_dw_down")
    dh2, dh2b, gs["ffn_norm_g"], parts = _linear_nt_normbwd(dgu, wb["w_gate_up"], h2, dy, sm["ffn_norm_g"],
                                                            name="bwd_ffn_in", grads={"w_down": gw_down})
    gw["w_gate_up"] = _dw_tn(fn, dgu, tkw=1024, tn=1536, name="bwd_dw_gate_up")

    gw["w_co"] = _dw_tn(co, dh2b, tkw=512, tn=1024, name="bwd_dw_co")
    dcq, dck_n, dcv, gs["cq_norm_g"], dh1, dh1b, gs["cross_norm_g"] = _cross_bwd(
        dh2, dh2b, h1, cq_raw, ck_n, cv, sm["cq_norm_g"], sm["cross_norm_g"], wb["w_cq"], wb["w_co"], name="bwd_cross")
    gw["w_ckv"], gs["mem_norm_g"], gs["ck_norm_g"] = _mem_bwd(dck_n, dcv, ck_raw, memn, mem, wb["w_ckv"],
                                                             sm["mem_norm_g"], sm["ck_norm_g"], name="bwd_mem")
    gw["w_cq"] = _dw_tn(hn, dcq, tkw=1024, tn=512, name="bwd_dw_cq")

    dmix = _linear_nt(dh1b, wb["w_out"], cn=512, name="bwd_out_proj")
    gw["w_out"] = _dw_tn(mix, dh1b, tkw=1024, tn=1024, name="bwd_dw_out")
    du, gs["pool_w"], gs["pool_scale"] = _pool_bwd(proj, dmix, sm["pool_w"], sm["pool_scale"], name="bwd_pool")
    dq, dk, dv, got = _attn_bwd(q_rot, k_rot, v, mix, dmix, lse_all, gw, name="bwd_attn")
    parts.update(got)
    dproj, gs["q_norm_g"], gs["k_norm_g"] = _qk_prep_bwd(proj, dq, dk, dv, du, cos_t, sin_t, sm["q_norm_g"],
                                                         sm["k_norm_g"], name="bwd_qk_prep")
    gw_in = _dw_tn(xn, dproj, tkw=1024, tn=1536, name="bwd_dw_in")
    dx, _, gs["mix_norm_g"], last = _linear_nt_normbwd(dproj, wb["w_in"], x, dh1, sm["mix_norm_g"],
                                                       name="bwd_in_proj", grads={"w_in": gw_in})
    parts.update(last)
    return sq, dx, parts, gs


SHARDED = ("w_in", "w_out", "w_cq", "w_ckv", "w_co", "w_gate_up", "w_down")
SMALL = ("mix_norm_g", "q_norm_g", "k_norm_g", "pool_w", "pool_scale", "cross_norm_g", "mem_norm_g", "cq_norm_g",
         "ck_norm_g", "ffn_norm_g")
FULL = {
    "w_in": ((D_MODEL, IN_W), 1, IN_W // N_DEV),
    "w_out": ((D_MODEL, D_MODEL), 0, D_MODEL // N_DEV),
    "w_cq": ((D_MODEL, X_W), 0, D_MODEL // N_DEV),
    "w_ckv": ((D_MODEL, 2 * X_W), 0, D_MODEL // N_DEV),
    "w_co": ((X_W, D_MODEL), 1, D_MODEL // N_DEV),
    "w_gate_up": ((D_MODEL, 2 * FF_PAD), 1, FF_TILE),
    "w_down": ((FF_PAD, D_MODEL), 0, DOWN_SHARD),
}


def _shard_shape(name):
    shape, axis, width = FULL[name]
    return tuple(width if a == axis else n for a, n in enumerate(shape))


def _window(ref, name, dev):
    _, axis, width = FULL[name]
    if name == "w_down":
        start = pl.multiple_of((dev // 2) * FF_TILE + (dev % 2) * DOWN_SHARD, HALO)
    else:
        start = pl.multiple_of(dev * width, BLOCK)
    return ref.at[pl.ds(start, width), :] if axis == 0 else ref.at[:, pl.ds(start, width)]


def _mesh_place():
    x, y, c = lax.axis_index("x"), lax.axis_index("y"), lax.axis_index("c")
    return x, y, c, 4 * x + 2 * y + c


def _peer(x, y, c, k):
    px = 1 - x if k & 4 else x
    py = 1 - y if k & 2 else y
    pc = 1 - c if k & 1 else c
    return (px, py, pc), 4 * px + 2 * py + pc


HBM_SPEC = pl.BlockSpec(memory_space=pltpu.HBM)


def _comm_sems(n):
    return [pltpu.SemaphoreType.DMA((n, N_DEV - 1)), pltpu.SemaphoreType.DMA((n, N_DEV - 1)),
            pltpu.SemaphoreType.DMA((n,))]


DIRECT = (1, 2, 4, 6)
RELAYED = (2, 4, 6)


def _relay_step(steps):
    return max(1, (steps * 11) // 16) if steps > 1 else 0


def _gather_copies(names, ins, outs, send_sems, recv_sems, local_sems):
    x, y, c, me = _mesh_place()
    sibling, _ = _peer(x, y, c, 1)
    local, direct, relays, recv = [], [], {}, {}
    for wi, name in enumerate(names):
        def sems(k, to):
            return dict(send_sem=send_sems.at[wi, k - 1], recv_sem=recv_sems.at[wi, k - 1], device_id=to,
                        device_id_type=MESH)
        local.append(pltpu.make_async_copy(ins[wi], _window(outs[wi], name, me), local_sems.at[wi]))
        for k in range(1, N_DEV):
            peer, pidx = _peer(x, y, c, k)
            win = _window(outs[wi], name, pidx)
            recv[wi, k] = pltpu.make_async_remote_copy(src_ref=ins[wi], dst_ref=win, **sems(k, peer))
            if k in DIRECT:
                direct.append(pltpu.make_async_remote_copy(src_ref=ins[wi], dst_ref=_window(outs[wi], name, me),
                                                           **sems(k, peer)))
            if k in RELAYED:
                relays[wi, k] = pltpu.make_async_remote_copy(src_ref=win, dst_ref=win, **sems(k + 1, sibling))
    return local, direct, relays, recv


def _exchange_copies(names, ins, outs, send_sems, recv_sems, local_sems):
    x, y, c, me = _mesh_place()
    local, sent, recv = [], [], []
    for wi, name in enumerate(names):
        local.append(pltpu.make_async_copy(_window(ins[wi], name, me), outs[wi].at[0], local_sems.at[wi]))
        for k in range(1, N_DEV):
            peer, pidx = _peer(x, y, c, k)
            sems = dict(send_sem=send_sems.at[wi, k - 1], recv_sem=recv_sems.at[wi, k - 1], device_id=peer,
                        device_id_type=MESH)
            sent.append(pltpu.make_async_remote_copy(src_ref=_window(ins[wi], name, pidx), dst_ref=outs[wi].at[k], **sems))
            recv.append(pltpu.make_async_remote_copy(src_ref=_window(ins[wi], name, me), dst_ref=outs[wi].at[k], **sems))
    return local, sent, recv


def _down_pads(down_ref, zero_ref, zero_sems):
    return [pltpu.make_async_copy(zero_ref, down_ref.at[pl.ds(t * FF_TILE + FF_SHARD, FF_TILE - FF_SHARD), :],
                                  zero_sems.at[t]) for t in range(FF_PAD // FF_TILE)]


def _gather_scratch(names):
    if not names:
        return []
    pad = [pltpu.VMEM((FF_TILE - FF_SHARD, D_MODEL), BF16), pltpu.SemaphoreType.DMA((FF_PAD // FF_TILE,))]
    return _comm_sems(len(names)) + (pad if "w_down" in names else [])


def _gather_ops(names, ins, outs, scratch):
    local, direct, relays, recv = _gather_copies(names, ins, outs, *scratch[:3])
    pads = _down_pads(outs[names.index("w_down")], scratch[3], scratch[4]) if "w_down" in names else []

    def start():
        if pads:
            scratch[3][...] = jnp.zeros_like(scratch[3])
        for cp in local + direct + pads:
            cp.start()

    def relay(only=None):
        for (wi, k), cp in relays.items():
            if only is None or names[wi] in only:
                recv[wi, k].wait_recv()
                cp.start()

    def finish():
        for (wi, k), cp in recv.items():
            if k not in RELAYED:
                cp.wait_recv()
        for cp in direct + list(relays.values()):
            cp.wait_send()
        for cp in local + pads:
            cp.wait()

    return start, relay, finish


def _start(copies):
    local, sent, _ = copies
    for cp in local + sent:
        cp.start()


def _finish(copies):
    local, sent, recv = copies
    for cp in recv:
        cp.wait_recv()
    for cp in sent:
        cp.wait_send()
    for cp in local:
        cp.wait()


def _gather_weights(shards):
    names = tuple(shards)
    nw = len(names)

    def body(*refs):
        start, relay, finish = _gather_ops(names, refs[:nw], refs[nw:2 * nw], refs[2 * nw:])
        start()
        relay()
        finish()

    outs = _pcall(
        body, name="gather_weights",
        in_specs=[HBM_SPEC] * nw, out_specs=[HBM_SPEC] * nw,
        out_shape=[SDS(FULL[n][0], BF16) for n in names],
        scratch_shapes=_comm_sems(nw))(*[shards[n] for n in names])
    return dict(zip(names, outs))


def _exchange_small(blocks):
    nb = len(blocks)

    def body(*refs):
        ins, outs = refs[:nb], refs[nb:2 * nb]
        send_sems, recv_sems, local_sems = refs[2 * nb:]
        x, y, c, me = _mesh_place()
        local, sent, recv = [], [], []
        for bi in range(nb):
            local.append(pltpu.make_async_copy(ins[bi], outs[bi].at[me], local_sems.at[bi]))
            for k in range(1, N_DEV):
                peer, pidx = _peer(x, y, c, k)
                sems = dict(send_sem=send_sems.at[bi, k - 1], recv_sem=recv_sems.at[bi, k - 1], device_id=peer,
                            device_id_type=MESH)
                sent.append(pltpu.make_async_remote_copy(src_ref=ins[bi], dst_ref=outs[bi].at[me], **sems))
                recv.append(pltpu.make_async_remote_copy(src_ref=ins[bi], dst_ref=outs[bi].at[pidx], **sems))
        _start((local, sent, recv))
        _finish((local, sent, recv))

    return _pcall(
        body, name="exchange_small", in_specs=[HBM_SPEC] * nb, out_specs=[HBM_SPEC] * nb,
        out_shape=[SDS((N_DEV,) + a.shape, F32) for a in blocks], scratch_shapes=_comm_sems(nb))(*blocks)


def _adam_math(g, w, m, v):
    m_new = ADAM_B1 * m + (1.0 - ADAM_B1) * g
    v_new = ADAM_B2 * v + (1.0 - ADAM_B2) * (g * g)
    m_hat = m_new / (1.0 - ADAM_B1 ** ADAM_STEP)
    v_hat = v_new / (1.0 - ADAM_B2 ** ADAM_STEP)
    return -ADAM_LR * (m_hat / (jnp.sqrt(v_hat) + ADAM_EPS) + ADAM_WD * w), m_new, v_new


def _adamw_small(parts, w, m, v, sq_parts, *, name):
    n = len(parts)

    def body(*refs):
        p_refs, w_refs, m_refs, v_refs = refs[:n], refs[n:2 * n], refs[2 * n:3 * n], refs[3 * n:4 * n]
        sq_ref, outs = refs[4 * n], refs[4 * n + 1:]
        for i in range(n):
            g = p_refs[i][0]
            for k in range(1, N_DEV):
                g = g + p_refs[i][k]
            delta, m_new, v_new = _adam_math(g, w_refs[i][...], m_refs[i][...], v_refs[i][...])
            outs[4 * i][...] = g
            outs[4 * i + 1][...] = delta
            outs[4 * i + 2][...] = m_new
            outs[4 * i + 3][...] = v_new
        tot = sq_ref[0]
        for k in range(1, N_DEV):
            tot = tot + sq_ref[k]
        outs[4 * n][...] = (0.5 / D_MODEL) * jnp.sum(tot, axis=1, keepdims=True)

    out_shape = [SDS(a.shape, F32) for a in w for _ in range(4)] + [SDS((1, 1), F32)]
    outs = _pcall(body, name=name, out_shape=out_shape, compiler_params=_cp())(*parts, *w, *m, *v, sq_parts)
    return [outs[4 * i:4 * i + 4] for i in range(n)], outs[4 * n][0, 0]


def _adamw(parts, w, m, v, *, name):
    r, c = w.shape
    pc = parts.shape[2]
    tr = r
    for cand in (256, 128, 88):
        if r % cand == 0:
            tr = cand
            break

    def body(p_ref, w_ref, m_ref, v_ref, g_ref, d_ref, mo_ref, vo_ref):
        g = p_ref[0, :, :c].astype(F32)
        for k in range(1, N_DEV):
            g = g + p_ref[k, :, :c].astype(F32)
        g_ref[...] = g
        d_ref[...], mo_ref[...], vo_ref[...] = _adam_math(g, w_ref[...], m_ref[...], v_ref[...])

    row = pl.BlockSpec((tr, c), lambda i: (i, 0))
    return _pcall(
        body, name=name, grid=(r // tr,),
        in_specs=[pl.BlockSpec((N_DEV, tr, pc), lambda i: (0, i, 0)), row, row, row],
        out_specs=[row] * 4, out_shape=[SDS((r, c), F32)] * 4, compiler_params=_cp())(parts, w, m, v)


def _pad_cols(a, width):
    return jnp.pad(a, ((0, 0), (0, width - a.shape[1])))


def kernel(x, mem, positions, mix_norm_g, w_in, q_norm_g, k_norm_g, pool_w, pool_scale, w_out, cross_norm_g, mem_norm_g, w_cq, w_ckv, cq_norm_g, ck_norm_g, w_co, ffn_norm_g, w_gate_up, w_down, loss_target, m_mix_norm_g, m_w_in, m_q_norm_g, m_k_norm_g, m_pool_w, m_pool_scale, m_w_out, m_cross_norm_g, m_mem_norm_g, m_w_cq, m_w_ckv, m_cq_norm_g, m_ck_norm_g, m_w_co, m_ffn_norm_g, m_w_gate_up, m_w_down, v_mix_norm_g, v_w_in, v_q_norm_g, v_k_norm_g, v_pool_w, v_pool_scale, v_w_out, v_cross_norm_g, v_mem_norm_g, v_w_cq, v_w_ckv, v_cq_norm_g, v_ck_norm_g, v_w_co, v_ffn_norm_g, v_w_gate_up, v_w_down):
    given = dict(locals())
    w_f32 = {n: given[n][0] for n in SHARDED + SMALL}
    m_f32 = {n: given["m_" + n][0] for n in SHARDED + SMALL}
    v_f32 = {n: given["v_" + n][0] for n in SHARDED + SMALL}
    shards = {n: w_f32[n].astype(BF16) for n in SHARDED}
    shards["w_gate_up"] = _pad_cols(shards["w_gate_up"], FF_TILE)
    w_in_full = _gather_weights({"w_in": shards.pop("w_in")})["w_in"]
    sm_rows = {n: (w_f32[n] if w_f32[n].ndim == 3 else w_f32[n].reshape(1, -1)) for n in SMALL}
    sq, dx, parts, gs = _local_step(x[0], mem[0], positions[0], loss_target[0], w_in_full, shards, sm_rows)

    flat = lambda a: a.reshape(-1, a.shape[-1])
    got = _exchange_small([flat(gs[n]) for n in SMALL] + [sq])
    small_res, loss = _adamw_small(got[:-1], [flat(sm_rows[n]) for n in SMALL],
                                   [flat(m_f32[n].reshape(sm_rows[n].shape)) for n in SMALL],
                                   [flat(v_f32[n].reshape(sm_rows[n].shape)) for n in SMALL], got[-1],
                                   name="adamw_small")
    res = {n: [a.reshape(w_f32[n].shape) for a in small_res[i]] for i, n in enumerate(SMALL)}
    for n in SHARDED:
        res[n] = _adamw(parts[n], w_f32[n], m_f32[n], v_f32[n], name="adamw_" + n)
    order = ("mix_norm_g", "w_in", "q_norm_g", "k_norm_g", "pool_w", "pool_scale", "w_out", "cross_norm_g",
             "mem_norm_g", "w_cq", "w_ckv", "cq_norm_g", "ck_norm_g", "w_co", "ffn_norm_g", "w_gate_up", "w_down")
    outs = [loss, dx[None]]
    for which in range(4):
        outs += [res[n][which][None] for n in order]
    return tuple(outs)
```

```python
import functools

import jax
import jax.numpy as jnp
from jax import lax
from jax.experimental import pallas as pl
from jax.experimental.pallas import tpu as pltpu

F32 = jnp.float32
BF16 = jnp.bfloat16
SDS = jax.ShapeDtypeStruct

D_MODEL = 1024
HEAD_DIM = 128
N_GROUPS = 3
DILATIONS = (1, 4, 16)
ATT_HEADS = 4
Q_W = 1536
KV_W = 512
POOL_W = 512
POOL_WINDOWS = (2, 4, 8, 16)
IN_W = 3072
X_W = 512
N_MEM = 256
D_FF = 2816
FF_TILE = 768
FF_SHARD = 704
FF_PAD = 4 * FF_TILE
DOWN_SHARD = 352
ROT_DIM = 32
ROT_HALF = 16
ROPE_THETA = 500000.0
EPS = 1e-6
NEG_INF = -1e30
SCALE = HEAD_DIM ** -0.5
BLOCK = 128
HALO = 16

ADAM_LR = 0.001
ADAM_B1 = 0.9
ADAM_B2 = 0.999
ADAM_EPS = 1e-08
ADAM_WD = 0.01
ADAM_STEP = 10

N_DEV = 8
VMEM_LIMIT_BYTES = 56 * 1024 * 1024
MESH = pl.DeviceIdType.MESH


def _pcall(body, **kw):
    return pl.pallas_call(body, **kw)


def _cp():
    return pltpu.CompilerParams(vmem_limit_bytes=VMEM_LIMIT_BYTES)


def _dot(a, b):
    return lax.dot_general(a, b, (((1,), (0,)), ((), ())), preferred_element_type=F32)


def _dot_nt(a, b):
    return lax.dot_general(a, b, (((1,), (1,)), ((), ())), preferred_element_type=F32)


def _dot_tn(a, b):
    return lax.dot_general(a, b, (((0,), (0,)), ((), ())), preferred_element_type=F32)


def _rows(s):
    return min(512, s)


def _rms_r(x):
    return lax.rsqrt(jnp.mean(x * x, axis=-1, keepdims=True) + EPS)


def _norm_bwd(x, r, gain, dxn):
    z = dxn * gain
    dx = r * z - x * (r * r * r * jnp.mean(z * x, axis=-1, keepdims=True))
    dgain = jnp.sum(dxn * x * r, axis=0, keepdims=True)
    return dx, dgain


def _split_bf16(t):
    hi = t.astype(BF16)
    return hi, (t - hi.astype(F32)).astype(BF16)


def _lane_sums(t, ones):
    hi, lo = _split_bf16(t)
    return _dot(hi, ones) + _dot(lo, ones)


def _head_r(y, ones):
    return lax.rsqrt(_lane_sums(y * y, ones) * (1.0 / HEAD_DIM) + EPS)


def _head_norm_bwd(y, r, gain, dn, ones):
    z = dn * gain
    dy = r * z - y * (r * r * r * (_dot((z * y).astype(BF16), ones) * (1.0 / HEAD_DIM)))
    return dy, jnp.sum(dn * y * r, axis=0, keepdims=True)


def _swap_matrix():
    src = lax.broadcasted_iota(jnp.int32, (HEAD_DIM, HEAD_DIM), 0)
    dst = lax.broadcasted_iota(jnp.int32, (HEAD_DIM, HEAD_DIM), 1)
    hit = ((dst < ROT_HALF) & (src == dst + ROT_HALF)) | ((dst >= ROT_HALF) & (dst < ROT_DIM) & (src == dst - ROT_HALF))
    return jnp.where(hit, 1.0, 0.0).astype(BF16)


def _partner(t, swap):
    hi, lo = _split_bf16(t)
    return _dot(hi, swap) + _dot(lo, swap)


def _rope(n, cos_t, sin_t, swap):
    return n * cos_t + _partner(n, swap) * sin_t


def _rope_bwd(d, cos_t, sin_t, swap):
    return d * cos_t + _dot((d * sin_t).astype(BF16), swap)


def _resident(shape):
    return pl.BlockSpec(shape, lambda i: (0,) * len(shape), pipeline_mode=pl.Buffered(1))


def _chunks(n, cn):
    return [slice(j * cn, (j + 1) * cn) for j in range(n // cn)]


def _in_proj(x, gain, w, cos_t, sin_t, qg, kg, shards, *, name):
    s, k = x.shape
    n = w.shape[1]
    tm = _rows(s)
    names = tuple(shards)
    nr = len(names)

    def body(*refs):
        x_ref, g_ref, w_ref, c_ref, s_ref, qg_ref, kg_ref = refs[:7]
        y_ref, xn_ref, qo_ref, ko_ref, vo_ref = refs[7 + nr:12 + nr]
        start, relay, finish = _gather_ops(names, refs[7:7 + nr], refs[12 + nr:12 + 2 * nr], refs[12 + 2 * nr:])
        pl.when(pl.program_id(0) == 0)(start)
        pl.when(pl.program_id(0) == _relay_step(s // tm))(relay)
        xv = x_ref[...]
        xn_ref[...] = (xv * _rms_r(xv) * g_ref[...]).astype(BF16)
        cs, sn = c_ref[...], s_ref[...]
        ones, swap = jnp.ones((HEAD_DIM, HEAD_DIM), BF16), _swap_matrix()
        for c in _chunks(n, 6 * HEAD_DIM):
            pr = _dot(xn_ref[...], w_ref[:, c])
            y_ref[:, c] = pr
            for hb in range(6):
                off = c.start + hb * HEAD_DIM
                y = pr[:, hb * HEAD_DIM:(hb + 1) * HEAD_DIM]
                if off < Q_W:
                    qo_ref[:, off:off + HEAD_DIM] = (
                        _rope(y * _head_r(y, ones) * qg_ref[...], cs, sn, swap) * SCALE).astype(BF16)
                elif off < Q_W + KV_W:
                    ko_ref[:, off - Q_W:off - Q_W + HEAD_DIM] = _rope(
                        y * _head_r(y, ones) * kg_ref[...], cs, sn, swap).astype(BF16)
                elif off < Q_W + 2 * KV_W:
                    vo_ref[:, off - Q_W - KV_W:off - Q_W - KV_W + HEAD_DIM] = y.astype(BF16)
        pl.when(pl.program_id(0) == s // tm - 1)(finish)

    row = lambda w_: pl.BlockSpec((tm, w_), lambda i: (i, 0))
    outs = _pcall(
        body, name=name, grid=(s // tm,),
        in_specs=[row(k), _resident((1, k)), _resident((k, n)), row(HEAD_DIM), row(HEAD_DIM),
                  _resident((1, HEAD_DIM)), _resident((1, HEAD_DIM))] + [HBM_SPEC] * nr,
        out_specs=[row(n), row(k), row(Q_W), row(KV_W), row(KV_W)] + [HBM_SPEC] * nr,
        out_shape=[SDS((s, n), F32), SDS((s, k), BF16), SDS((s, Q_W), BF16), SDS((s, KV_W), BF16),
                   SDS((s, KV_W), BF16)] + [SDS(FULL[a][0], BF16) for a in names],
        scratch_shapes=_gather_scratch(names),
        compiler_params=_cp())(x, gain, w, cos_t, sin_t, qg, kg, *[shards[a] for a in names])
    return outs[:5], dict(zip(names, outs[5:]))


def _norm_linear_swiglu(x, gain, wgu, *, name):
    s, k = x.shape
    tm = _rows(s)

    def body(x_ref, g_ref, w_ref, gu_ref, a_ref, xn_ref):
        xv = x_ref[...]
        xn_ref[...] = (xv * _rms_r(xv) * g_ref[...]).astype(BF16)
        for c in _chunks(FF_PAD, FF_TILE):
            g = _dot(xn_ref[...], w_ref[:, c])
            u = _dot(xn_ref[...], w_ref[:, slice(FF_PAD + c.start, FF_PAD + c.stop)])
            a_ref[:, c] = (g * jax.nn.sigmoid(g) * u).astype(BF16)
            gu_ref[0, :, c] = g.astype(BF16)
            gu_ref[1, :, c] = u.astype(BF16)

    row = lambda w_: pl.BlockSpec((tm, w_), lambda i: (i, 0))
    return _pcall(
        body, name=name, grid=(s // tm,),
        in_specs=[row(k), _resident((1, k)), _resident((k, 2 * FF_PAD))],
        out_specs=[pl.BlockSpec((2, tm, FF_PAD), lambda i: (0, i, 0)), row(FF_PAD), row(k)],
        out_shape=[SDS((2, s, FF_PAD), BF16), SDS((s, FF_PAD), BF16), SDS((s, k), BF16)],
        compiler_params=_cp())(x, gain, wgu)


def _linear_res(a, w, res, *, cn, name):
    s, k = a.shape
    n = w.shape[1]
    tm = _rows(s)

    def body(a_ref, w_ref, r_ref, y_ref):
        for c in _chunks(n, cn):
            y_ref[:, c] = r_ref[:, c] + _dot(a_ref[...], w_ref[:, c])

    row = lambda w_: pl.BlockSpec((tm, w_), lambda i: (i, 0))
    return _pcall(
        body, name=name, grid=(s // tm,),
        in_specs=[row(k), _resident((k, n)), row(n)], out_specs=row(n),
        out_shape=SDS((s, n), F32), compiler_params=_cp())(a, w, res)


def _linear_res_loss(a, w, res, tgt, *, name):
    s, k = a.shape
    n = w.shape[1]
    tm = _rows(s)

    def body(a_ref, w_ref, r_ref, t_ref, dy_ref, dyb_ref, sq_ref):
        e = r_ref[...] + _dot(a_ref[...], w_ref[...]) - t_ref[...]
        dy = e * (1.0 / n)
        dy_ref[...] = dy
        dyb_ref[...] = dy.astype(BF16)

        @pl.when(pl.program_id(0) == 0)
        def _():
            sq_ref[...] = jnp.zeros_like(sq_ref)
        sq_ref[...] += jnp.sum(e * e, axis=0, keepdims=True)

    row = lambda w_: pl.BlockSpec((tm, w_), lambda i: (i, 0))
    return _pcall(
        body, name=name, grid=(s // tm,),
        in_specs=[row(k), _resident((k, n)), row(n), row(n)],
        out_specs=[row(n), row(n), pl.BlockSpec((1, n), lambda i: (0, 0))],
        out_shape=[SDS((s, n), F32), SDS((s, n), BF16), SDS((1, n), F32)],
        compiler_params=_cp())(a, w, res, tgt)


def _linear_nt(g, w, *, cn, name):
    s, k = g.shape
    n = w.shape[0]
    tm = _rows(s)

    def body(g_ref, w_ref, y_ref):
        for c in _chunks(n, cn):
            y_ref[:, c] = _dot_nt(g_ref[...], w_ref[c, :]).astype(BF16)

    row = lambda w_: pl.BlockSpec((tm, w_), lambda i: (i, 0))
    return _pcall(
        body, name=name, grid=(s // tm,),
        in_specs=[row(k), _resident((n, k))], out_specs=row(n),
        out_shape=SDS((s, n), BF16), compiler_params=_cp())(g, w)


def _swiglu_bwd(dyb, wd, gu, *, name):
    s, n = dyb.shape
    tm = _rows(s)

    def body(dy_ref, wd_ref, gu_ref, dgu_ref):
        for c in _chunks(FF_PAD, FF_TILE):
            da = _dot_nt(dy_ref[...], wd_ref[c, :])
            g = gu_ref[0, :, c].astype(F32)
            u = gu_ref[1, :, c].astype(F32)
            sg = jax.nn.sigmoid(g)
            dgu_ref[0, :, c] = (da * u * (sg * (1.0 + g * (1.0 - sg)))).astype(BF16)
            dgu_ref[1, :, c] = (da * (g * sg)).astype(BF16)

    half = pl.BlockSpec((2, tm, FF_PAD), lambda i: (0, i, 0))
    return _pcall(
        body, name=name, grid=(s // tm,),
        in_specs=[pl.BlockSpec((tm, n), lambda i: (i, 0)), _resident((FF_PAD, n)), half],
        out_specs=half, out_shape=SDS((2, s, FF_PAD), BF16), compiler_params=_cp())(dyb, wd, gu)


def _linear_nt_normbwd(g, w, x, dres, gain, *, name, grads=None):
    d, k = w.shape
    s = x.shape[0]
    tm = _rows(s)
    names = tuple(grads or ())
    nr = len(names)

    def body(*refs):
        g_ref, w_ref, x_ref, dr_ref, gn_ref = refs[:5]
        dx_ref, dxb_ref, dg_ref = refs[5 + nr:8 + nr]
        if nr:
            copies = _exchange_copies(names, refs[5:5 + nr], refs[8 + nr:8 + 2 * nr], *refs[8 + 2 * nr:])

        @pl.when(pl.program_id(0) == 0)
        def _():
            dg_ref[...] = jnp.zeros_like(dg_ref)
            if nr:
                _start(copies)

        if g.ndim == 3:
            dxn = _dot_nt(g_ref[0], w_ref[:, :k // 2]) + _dot_nt(g_ref[1], w_ref[:, k // 2:])
        else:
            dxn = _dot_nt(g_ref[...], w_ref[...])
        xv = x_ref[...]
        dx, dgain = _norm_bwd(xv, _rms_r(xv), gn_ref[...], dxn)
        out = dr_ref[...] + dx
        dx_ref[...] = out
        dxb_ref[...] = out.astype(BF16)
        dg_ref[...] += dgain

        if nr:
            @pl.when(pl.program_id(0) == s // tm - 1)
            def _():
                _finish(copies)

    row = pl.BlockSpec((tm, d), lambda i: (i, 0))
    g_spec = (pl.BlockSpec((2, tm, k // 2), lambda i: (0, i, 0)) if g.ndim == 3
              else pl.BlockSpec((tm, k), lambda i: (i, 0)))
    outs = _pcall(
        body, name=name, grid=(s // tm,),
        in_specs=[g_spec, _resident((d, k)), row, row, _resident((1, d))] + [HBM_SPEC] * nr,
        out_specs=[row, row, pl.BlockSpec((1, d), lambda i: (0, 0))] + [HBM_SPEC] * nr,
        out_shape=[SDS((s, d), F32), SDS((s, d), BF16), SDS((1, d), F32)]
        + [SDS((N_DEV,) + _shard_shape(n), BF16) for n in names],
        scratch_shapes=_comm_sems(nr) if nr else [],
        compiler_params=_cp())(g, w, x, dres, gain, *[grads[n] for n in names])
    return (outs[0], outs[1], outs[2], dict(zip(names, outs[3:]))) if nr else tuple(outs)


def _dw_tn(x, g, *, tkw, tn, name):
    s, kw = x.shape
    halves = g.ndim == 3
    n = 2 * g.shape[2] if halves else g.shape[1]
    ts = min(2048, s)
    ns = s // ts
    per_half = n // 2 // tn

    def body(x_ref, g_ref, o_ref, acc_ref):
        ss = pl.program_id(2)

        @pl.when(ss == 0)
        def _():
            acc_ref[...] = jnp.zeros_like(acc_ref)

        acc_ref[...] += _dot_tn(x_ref[...], g_ref[...])

        @pl.when(ss == ns - 1)
        def _():
            o_ref[...] = acc_ref[...].astype(BF16)

    g_spec = (pl.BlockSpec((None, ts, tn), lambda a, b, ss: (b // per_half, ss, b % per_half)) if halves
              else pl.BlockSpec((ts, tn), lambda a, b, ss: (ss, b)))
    return _pcall(
        body, name=name, grid=(kw // tkw, n // tn, ns),
        in_specs=[pl.BlockSpec((ts, tkw), lambda a, b, ss: (ss, a)), g_spec],
        out_specs=pl.BlockSpec((tkw, tn), lambda a, b, ss: (a, b)),
        out_shape=SDS((kw, n), BF16),
        scratch_shapes=[pltpu.VMEM((tkw, tn), F32)], compiler_params=_cp())(x, g)


def _qk_prep_bwd(proj, dq, dk, dv, du, cos_t, sin_t, qg, kg, *, name):
    s = proj.shape[0]
    tm = _rows(s)

    def body(q_ref, k_ref, dq_ref, dk_ref, dv_ref, du_ref, c_ref, s_ref, qg_ref, kg_ref, dp_ref, dqg_ref, dkg_ref):
        c, sn = c_ref[...], s_ref[...]

        @pl.when(pl.program_id(0) == 0)
        def _():
            dqg_ref[...] = jnp.zeros_like(dqg_ref)
            dkg_ref[...] = jnp.zeros_like(dkg_ref)

        ones, swap = jnp.ones((HEAD_DIM, HEAD_DIM), BF16), _swap_matrix()
        dqg = jnp.zeros((1, HEAD_DIM), F32)
        for g in range(N_GROUPS):
            for h in range(ATT_HEADS):
                sl = slice(h * HEAD_DIM, (h + 1) * HEAD_DIM)
                col = slice(g * KV_W + h * HEAD_DIM, g * KV_W + (h + 1) * HEAD_DIM)
                y = q_ref[:, col]
                dn = _rope_bwd(dq_ref[g, :, sl].astype(F32) * SCALE, c, sn, swap)
                dy, dgain = _head_norm_bwd(y, _head_r(y, ones), qg_ref[...], dn, ones)
                dp_ref[:, col] = dy.astype(BF16)
                dqg = dqg + dgain
        dqg_ref[...] += dqg

        dkg = jnp.zeros((1, HEAD_DIM), F32)
        for h in range(ATT_HEADS):
            sl = slice(h * HEAD_DIM, (h + 1) * HEAD_DIM)
            y = k_ref[:, sl]
            dn = _rope_bwd(dk_ref[:, sl], c, sn, swap)
            dy, dgain = _head_norm_bwd(y, _head_r(y, ones), kg_ref[...], dn, ones)
            dp_ref[:, Q_W + h * HEAD_DIM:Q_W + (h + 1) * HEAD_DIM] = dy.astype(BF16)
            dkg = dkg + dgain
        dkg_ref[...] += dkg

        dp_ref[:, Q_W + KV_W:Q_W + 2 * KV_W] = dv_ref[...].astype(BF16)
        dp_ref[:, Q_W + 2 * KV_W:] = du_ref[...].astype(BF16)

    row = lambda w, j: pl.BlockSpec((tm, w), lambda i: (i, j))
    one = pl.BlockSpec((1, HEAD_DIM), lambda i: (0, 0))
    return _pcall(
        body, name=name, grid=(s // tm,),
        in_specs=[row(Q_W, 0), row(KV_W, 3), pl.BlockSpec((N_GROUPS, tm, KV_W), lambda i: (0, i, 0))]
        + [row(KV_W, 0)] * 3 + [row(HEAD_DIM, 0), row(HEAD_DIM, 0), one, one],
        out_specs=[row(IN_W, 0), one, one],
        out_shape=[SDS((s, IN_W), BF16), SDS((1, HEAD_DIM), F32), SDS((1, HEAD_DIM), F32)],
        compiler_params=_cp())(proj, proj, dq, dk, dv, du, cos_t, sin_t, qg, kg)


ATT_ROWS = 16 * BLOCK


def _sub(ref, start, d, size=BLOCK):
    return ref[pl.ds(start, size, stride=d), :] if d > 1 else ref[pl.ds(start, size), :]


def _sub_set(ref, start, d, val):
    if d > 1:
        ref[pl.ds(start, BLOCK, stride=d), :] = val
    else:
        ref[pl.ds(start, BLOCK), :] = val


def _band_masks():
    row = lax.broadcasted_iota(jnp.int32, (BLOCK, BLOCK), 0)
    col = lax.broadcasted_iota(jnp.int32, (BLOCK, BLOCK), 1)
    return col <= row, col >= row


def _eye(n=BLOCK):
    row = lax.broadcasted_iota(jnp.int32, (n, n), 0)
    col = lax.broadcasted_iota(jnp.int32, (n, n), 1)
    return jnp.where(row == col, 1.0, 0.0).astype(BF16)


def _attn_fwd(q_rot, k_rot, v, shards, *, name):
    s = q_rot.shape[0]
    rr = ATT_ROWS
    nblk = s // rr
    names = tuple(shards)
    nr = len(names)

    def body(*refs):
        q0, q1, q2, kp, kc, vp, vc = refs[:7]
        mix_ref, lse_ref = refs[7 + nr:9 + nr]
        qs, ks, vs, os_, ls = refs[9 + 2 * nr:14 + 2 * nr]
        h, n = pl.program_id(0), pl.program_id(1)
        start, relay, finish = _gather_ops(names, refs[7:7 + nr], refs[9 + nr:9 + 2 * nr], refs[14 + 2 * nr:])
        pl.when((h == 0) & (n == 0))(start)
        steps = ATT_HEADS * nblk
        early, late = max(1, (steps * 6) // 16), max(1, (steps * 14) // 16)
        pl.when(h * nblk + n == early)(functools.partial(relay, names[:1]))
        pl.when(h * nblk + n == late)(functools.partial(relay, names[1:]))
        for g, q_ref in enumerate((q0, q1, q2)):
            qs[g] = q_ref[...].astype(F32)
        ks[:rr] = kp[...].astype(F32)
        ks[rr:] = kc[...].astype(F32)
        vs[:rr] = vp[...].astype(F32)
        vs[rr:] = vc[...].astype(F32)
        m_cur, m_band = _band_masks()
        mask_in = jnp.concatenate([m_band, m_cur], axis=1)
        mask_first = jnp.concatenate([m_band & (n > 0), m_cur], axis=1)
        ones = jnp.ones((2 * BLOCK, HEAD_DIM), BF16)
        pieces = [(g, d, j * BLOCK * d + r, j) for g, d in enumerate(DILATIONS) for r in range(d)
                  for j in range(rr // (BLOCK * d))]

        def scores(piece):
            g, d, base, j = piece
            q = _sub(qs.at[g], base, d).astype(BF16)
            k2 = _sub(ks, rr + base - BLOCK * d, d, 2 * BLOCK).astype(BF16)
            return jnp.where(mask_first if j == 0 else mask_in, _dot_nt(q, k2), NEG_INF)

        sc = scores(pieces[0])
        for i, (g, d, base, j) in enumerate(pieces):
            cur = sc
            if i + 1 < len(pieces):
                sc = scores(pieces[i + 1])
            m = jnp.max(cur, axis=-1, keepdims=True)
            p = jnp.exp(cur - m).astype(BF16)
            v2 = _sub(vs, rr + base - BLOCK * d, d, 2 * BLOCK).astype(BF16)
            acc_l = _dot(p, jnp.concatenate([v2, ones], axis=1))
            l = acc_l[:, HEAD_DIM:]
            _sub_set(os_.at[g], base, d, acc_l[:, :HEAD_DIM] / l)
            _sub_set(ls.at[g], base, d, m + jnp.log(l))
        for c in _chunks(rr, 2 * BLOCK):
            a, b, cc = ls[0, c, :], ls[1, c, :], ls[2, c, :]
            m = jnp.maximum(jnp.maximum(a, b), cc)
            wa, wb, wc = jnp.exp(a - m), jnp.exp(b - m), jnp.exp(cc - m)
            den = wa + wb + wc
            mix_ref[c, :] = ((wa * os_[0, c, :] + wb * os_[1, c, :] + wc * os_[2, c, :]) / den).astype(BF16)
            lse_ref[c, :] = m + jnp.log(den)

        pl.when((h == ATT_HEADS - 1) & (n == nblk - 1))(finish)

    blk = lambda f: pl.BlockSpec((rr, HEAD_DIM), f)
    prv = lambda n: jnp.maximum(n - 1, 0)
    outs = _pcall(
        body, name=name, grid=(ATT_HEADS, nblk),
        in_specs=[blk(lambda h, n, g=g: (n, g * ATT_HEADS + h)) for g in range(N_GROUPS)]
        + [blk(lambda h, n: (prv(n), h)), blk(lambda h, n: (n, h))] * 2 + [HBM_SPEC] * nr,
        out_specs=[blk(lambda h, n: (n, h)), blk(lambda h, n: (n, h))] + [HBM_SPEC] * nr,
        out_shape=[SDS((s, KV_W + POOL_W), BF16), SDS((s, KV_W), F32)] + [SDS(FULL[w][0], BF16) for w in names],
        scratch_shapes=[pltpu.VMEM((N_GROUPS, rr, HEAD_DIM), F32), pltpu.VMEM((2 * rr, HEAD_DIM), F32),
                        pltpu.VMEM((2 * rr, HEAD_DIM), F32), pltpu.VMEM((N_GROUPS, rr, HEAD_DIM), F32),
                        pltpu.VMEM((N_GROUPS, rr, HEAD_DIM), F32)] + _gather_scratch(names),
        compiler_params=_cp())(q_rot, q_rot, q_rot, k_rot, k_rot, v, v, *[shards[w] for w in names])
    return outs[0], outs[1], dict(zip(names, outs[2:]))


def _attn_bwd(q_rot, k_rot, v, mix, dmix, lse, grads, *, name):
    s = q_rot.shape[0]
    rr = ATT_ROWS
    nblk = s // rr
    names = tuple(grads)
    nr = len(names)

    def body(*refs):
        q0, q1, q2, qx0, qx1, qx2, kp, kc, vp, vc, do_c, do_x, o_c, o_x, l_c, l_x = refs[:16]
        dq_ref, dk_ref, dv_ref = refs[16 + nr:19 + nr]
        qs, ks, vs, dos, lss, dls, dqs, dks, dvs, send_sems, recv_sems, local_sems = refs[19 + 2 * nr:]
        h, n = pl.program_id(0), pl.program_id(1)
        copies = _exchange_copies(names, refs[16:16 + nr], refs[19 + nr:19 + 2 * nr], send_sems, recv_sems, local_sems)

        @pl.when((h == 0) & (n == 0))
        def _():
            _start(copies)

        for g, (qc_ref, qx_ref) in enumerate(((q0, qx0), (q1, qx1), (q2, qx2))):
            qs[g, :rr] = qc_ref[...].astype(F32)
            qs[g, rr:] = qx_ref[...].astype(F32)
        ks[:rr] = kp[...].astype(F32)
        ks[rr:] = kc[...].astype(F32)
        vs[:rr] = vp[...].astype(F32)
        vs[rr:] = vc[...].astype(F32)
        lss[:rr] = l_c[...]
        lss[rr:] = l_x[...]
        for half, (d_ref, o_ref) in enumerate(((do_c, o_c), (do_x, o_x))):
            for c in _chunks(rr, 2 * BLOCK):
                cs = slice(half * rr + c.start, half * rr + c.stop)
                dof = d_ref[c, :].astype(F32)
                dos[cs, :] = dof
                dls[cs, :] = jnp.broadcast_to(jnp.sum(dof * o_ref[c, :].astype(F32), axis=-1, keepdims=True),
                                              (2 * BLOCK, HEAD_DIM))
        m_cur, m_band = _band_masks()
        mask_in = jnp.concatenate([m_cur, m_band], axis=0)
        mask_last = jnp.concatenate([m_cur, m_band & (n + 1 < nblk)], axis=0)
        m_first = m_band & (n > 0)
        eye = _eye()
        pieces = [(g, d, j * BLOCK * d + r, j, rr // (BLOCK * d)) for g, d in enumerate(DILATIONS) for r in range(d)
                  for j in range(rr // (BLOCK * d))]

        def front(piece):
            g, d, base, _, _ = piece
            q2 = _sub(qs.at[g], base, d, 2 * BLOCK).astype(BF16)
            do2 = _sub(dos, base, d, 2 * BLOCK).astype(BF16)
            k = _sub(ks, rr + base, d).astype(BF16)
            vv = _sub(vs, rr + base, d).astype(BF16)
            return q2, do2, k, _dot_nt(q2, k), _dot_nt(do2, vv)

        def middle(piece, fr, dq_acc):
            g, d, base, j, nsub = piece
            q2, do2, k, s2, dp2 = fr
            if j == 0:
                kp_ = _sub(ks, rr + base - BLOCK * d, d).astype(BF16)
                p0 = jnp.where(m_first, jnp.exp(_dot_nt(q2[:BLOCK], kp_) - _sub(lss, base, d)), 0.0)
                ds0 = p0 * (_dot_nt(do2[:BLOCK], _sub(vs, rr + base - BLOCK * d, d).astype(BF16)) - _sub(dls, base, d))
                dq_acc = _dot(ds0.astype(BF16), kp_)
            p2 = jnp.where(mask_last if j + 1 == nsub else mask_in,
                           jnp.exp(s2 - _sub(lss, base, d, 2 * BLOCK)), 0.0)
            ds2 = (p2 * (dp2 - _sub(dls, base, d, 2 * BLOCK))).astype(BF16)
            dq2 = _dot(ds2, k)
            return dq2[BLOCK:], (_dot_nt(eye, ds2), _dot_nt(eye, p2.astype(BF16)), q2, do2, dq_acc + dq2[:BLOCK])

        def back(piece, tr):
            g, d, base, _, _ = piece
            ds_t, p_t, q2, do2, dq = tr
            _sub_set(dqs.at[g], base, d, dq)
            dk, dv = _dot(ds_t.astype(BF16), q2), _dot(p_t.astype(BF16), do2)
            if g == 0:
                _sub_set(dks, base, d, dk)
                _sub_set(dvs, base, d, dv)
            else:
                _sub_set(dks, base, d, _sub(dks, base, d) + dk)
                _sub_set(dvs, base, d, _sub(dvs, base, d) + dv)

        fr, held, dq_acc = front(pieces[0]), None, None
        for i, piece in enumerate(pieces):
            cur = fr
            if i + 1 < len(pieces):
                fr = front(pieces[i + 1])
            dq_acc, now = middle(piece, cur, dq_acc)
            if held is not None:
                back(pieces[i - 1], held)
            held = now
        back(pieces[-1], held)
        for g in range(N_GROUPS):
            dq_ref[g] = dqs[g].astype(BF16)
        dk_ref[...] = dks[...]
        dv_ref[...] = dvs[...]

        @pl.when((h == ATT_HEADS - 1) & (n == nblk - 1))
        def _():
            _finish(copies)

    blk = lambda f: pl.BlockSpec((rr, HEAD_DIM), f)
    prv = lambda n: jnp.maximum(n - 1, 0)
    nxt = lambda n: jnp.minimum(n + 1, nblk - 1)
    cur_kv = blk(lambda h, n: (n, h))
    outs = _pcall(
        body, name=name, grid=(ATT_HEADS, nblk),
        in_specs=[blk(lambda h, n, g=g: (n, g * ATT_HEADS + h)) for g in range(N_GROUPS)]
        + [blk(lambda h, n, g=g: (nxt(n), g * ATT_HEADS + h)) for g in range(N_GROUPS)]
        + [blk(lambda h, n: (prv(n), h)), cur_kv] * 2
        + [cur_kv, blk(lambda h, n: (nxt(n), h))] * 3 + [HBM_SPEC] * nr,
        out_specs=[pl.BlockSpec((N_GROUPS, rr, HEAD_DIM), lambda h, n: (0, n, h)), cur_kv, cur_kv] + [HBM_SPEC] * nr,
        out_shape=[SDS((N_GROUPS, s, KV_W), BF16), SDS((s, KV_W), F32), SDS((s, KV_W), F32)]
        + [SDS((N_DEV,) + _shard_shape(w), BF16) for w in names],
        scratch_shapes=[pltpu.VMEM((N_GROUPS, 2 * rr, HEAD_DIM), F32)] + [pltpu.VMEM((2 * rr, HEAD_DIM), F32)] * 5
        + [pltpu.VMEM((N_GROUPS, rr, HEAD_DIM), F32), pltpu.VMEM((rr, HEAD_DIM), F32), pltpu.VMEM((rr, HEAD_DIM), F32)]
        + _comm_sems(nr),
        compiler_params=_cp())(q_rot, q_rot, q_rot, q_rot, q_rot, q_rot, k_rot, k_rot, v, v, dmix, dmix, mix, mix,
                               lse, lse, *[grads[w] for w in names])
    return outs[0], outs[1], outs[2], dict(zip(names, outs[3:]))


def _pool_d(u_ref, halo_ref, i, tm):
    halo = jnp.where(i > 0, halo_ref[...], 0.0)
    t = i * tm + lax.broadcasted_iota(jnp.int32, (tm, 1), 0)
    out = []
    for g, w in enumerate(POOL_WINDOWS):
        sl = slice(g * HEAD_DIM, (g + 1) * HEAD_DIM)
        u = u_ref[:, sl]
        acc = jnp.concatenate([halo[:, sl], u], axis=0)
        sh = 1
        while sh < w:
            acc = acc + pltpu.roll(acc, sh, 0)
            sh *= 2
        cnt = jnp.minimum(t + 1, w).astype(F32)
        out.append(acc[HALO:, :] / cnt - u)
    return out


def _pool_fwd(proj, mix, pool_w, pool_scale, *, name):
    s = proj.shape[0]
    tm = _rows(s)
    ucol = (IN_W - POOL_W) // POOL_W

    def body(u_ref, halo_ref, mix_in, w_ref, sc_ref, o_ref):
        del mix_in
        dd = _pool_d(u_ref, halo_ref, pl.program_id(0), tm)
        for g in range(len(POOL_WINDOWS)):
            sl = slice(g * HEAD_DIM, (g + 1) * HEAD_DIM)
            y = _dot(dd[g].astype(BF16), w_ref[g].astype(BF16))
            o_ref[:, sl] = (y * sc_ref[:, sl]).astype(BF16)

    return _pcall(
        body, name=name, grid=(s // tm,),
        in_specs=[pl.BlockSpec((tm, POOL_W), lambda i: (i, ucol)),
                  pl.BlockSpec((HALO, POOL_W), lambda i: (jnp.maximum(i * (tm // HALO) - 1, 0), ucol)),
                  pl.BlockSpec(memory_space=pl.ANY),
                  pl.BlockSpec((len(POOL_WINDOWS), HEAD_DIM, HEAD_DIM), lambda i: (0, 0, 0)),
                  pl.BlockSpec((1, POOL_W), lambda i: (0, 0))],
        out_specs=pl.BlockSpec((tm, POOL_W), lambda i: (i, 1)),
        out_shape=SDS(mix.shape, BF16), input_output_aliases={2: 0},
        compiler_params=_cp())(proj, proj, mix, pool_w, pool_scale)


def _pool_bwd(proj, dmix, pool_w, pool_scale, *, name):
    s = proj.shape[0]
    tm = _rows(s)
    nblk = s // tm
    ucol = (IN_W - POOL_W) // POOL_W
    ng = len(POOL_WINDOWS)

    def body(u_ref, halo_ref, dp_ref, dpn_ref, w_ref, sc_ref, du_ref, dw_ref, dsc_ref):
        i = pl.program_id(0)

        @pl.when(i == 0)
        def _():
            dw_ref[...] = jnp.zeros_like(dw_ref)
            dsc_ref[...] = jnp.zeros_like(dsc_ref)

        dd = _pool_d(u_ref, halo_ref, i, tm)
        t = i * tm + lax.broadcasted_iota(jnp.int32, (tm, 1), 0)
        dpn = jnp.where(i + 1 < nblk, dpn_ref[...].astype(F32), 0.0)
        for g, w in enumerate(POOL_WINDOWS):
            sl = slice(g * HEAD_DIM, (g + 1) * HEAD_DIM)
            wg = w_ref[g].astype(BF16)
            db = dd[g].astype(BF16)
            dp = dp_ref[:, sl].astype(F32)
            dsc_ref[:, sl] += jnp.sum(dp * _dot(db, wg), axis=0, keepdims=True)
            dy = (dp * sc_ref[:, sl]).astype(BF16)
            dw_ref[g] += _dot_tn(db, dy)
            g_d = _dot_nt(dy, wg)
            g_dn = _dot_nt((dpn[:, sl] * sc_ref[:, sl]).astype(BF16), wg)
            cnt = jnp.minimum(t + 1, w).astype(F32)
            acc = jnp.concatenate([g_d / cnt, g_dn * (1.0 / w)], axis=0)
            sh = 1
            while sh < w:
                acc = acc + pltpu.roll(acc, tm + HALO - sh, 0)
                sh *= 2
            du_ref[:, sl] = acc[:tm, :] - g_d

    nh = s // HALO
    return _pcall(
        body, name=name, grid=(nblk,),
        in_specs=[pl.BlockSpec((tm, POOL_W), lambda i: (i, ucol)),
                  pl.BlockSpec((HALO, POOL_W), lambda i: (jnp.maximum(i * (tm // HALO) - 1, 0), ucol)),
                  pl.BlockSpec((tm, POOL_W), lambda i: (i, 1)),
                  pl.BlockSpec((HALO, POOL_W), lambda i: (jnp.minimum((i + 1) * (tm // HALO), nh - 1), 1)),
                  pl.BlockSpec((ng, HEAD_DIM, HEAD_DIM), lambda i: (0, 0, 0)),
                  pl.BlockSpec((1, POOL_W), lambda i: (0, 0))],
        out_specs=[pl.BlockSpec((tm, POOL_W), lambda i: (i, 0)),
                   pl.BlockSpec((ng, HEAD_DIM, HEAD_DIM), lambda i: (0, 0, 0)),
                   pl.BlockSpec((1, POOL_W), lambda i: (0, 0))],
        out_shape=[SDS((s, POOL_W), F32), SDS((ng, HEAD_DIM, HEAD_DIM), F32), SDS((1, POOL_W), F32)],
        compiler_params=_cp())(proj, proj, dmix, dmix, pool_w, pool_scale)


def _mem_fwd(mem, mem_g, wckv, ck_g, *, name):
    def body(m_ref, g_ref, w_ref, kg_ref, mn_ref, ckr_ref, ckn_ref, cv_ref):
        mv = m_ref[...]
        mn = (mv * _rms_r(mv) * g_ref[...]).astype(BF16)
        mn_ref[...] = mn
        ckv = _dot(mn, w_ref[...])
        ckr_ref[...] = ckv[:, :X_W]
        cv_ref[...] = ckv[:, X_W:].astype(BF16)
        for h in range(ATT_HEADS):
            sl = slice(h * HEAD_DIM, (h + 1) * HEAD_DIM)
            y = ckv[:, sl]
            ckn_ref[:, sl] = (y * _rms_r(y) * kg_ref[...]).astype(BF16)

    return _pcall(
        body, name=name,
        out_shape=[SDS((N_MEM, D_MODEL), BF16), SDS((N_MEM, X_W), F32), SDS((N_MEM, X_W), BF16),
                   SDS((N_MEM, X_W), BF16)],
        compiler_params=_cp())(mem, mem_g, wckv, ck_g)


def _cross_q(cq_ref, g_ref, sl):
    y = cq_ref[:, sl]
    r = _rms_r(y)
    return y, r, y * r * g_ref[...] * SCALE


def _cross_fwd(h1, gain, wcq, ck_n, cv, cq_g, wco, *, name):
    s, d = h1.shape
    tm = _rows(s)

    def body(x_ref, gn_ref, wq_ref, k_ref, v_ref, g_ref, wo_ref, h2_ref, xn_ref, cq_ref, o_ref):
        xv = x_ref[...]
        xn_ref[...] = (xv * _rms_r(xv) * gn_ref[...]).astype(BF16)
        cq_ref[...] = _dot(xn_ref[...], wq_ref[...])
        for h in range(ATT_HEADS):
            sl = slice(h * HEAD_DIM, (h + 1) * HEAD_DIM)
            _, _, qn = _cross_q(cq_ref, g_ref, sl)
            sc = _dot_nt(qn.astype(BF16), k_ref[:, sl])
            p = jnp.exp(sc - jnp.max(sc, axis=-1, keepdims=True))
            p = p / jnp.sum(p, axis=-1, keepdims=True)
            o_ref[:, sl] = _dot(p.astype(BF16), v_ref[:, sl]).astype(BF16)
        for c in _chunks(d, X_W):
            h2_ref[:, c] = x_ref[:, c] + _dot(o_ref[...], wo_ref[:, c])

    row = lambda w_: pl.BlockSpec((tm, w_), lambda i: (i, 0))
    return _pcall(
        body, name=name, grid=(s // tm,),
        in_specs=[row(d), _resident((1, d)), _resident(wcq.shape), _resident(ck_n.shape), _resident(cv.shape),
                  _resident(cq_g.shape), _resident(wco.shape)],
        out_specs=[row(d), row(d), row(X_W), row(X_W)],
        out_shape=[SDS((s, d), F32), SDS((s, d), BF16), SDS((s, X_W), F32), SDS((s, X_W), BF16)],
        compiler_params=_cp())(h1, gain, wcq, ck_n, cv, cq_g, wco)


def _cross_bwd(dh2, dh2b, h1, cq_raw, ck_n, cv, cq_g, gain, wcq, wco, *, name):
    s, d = h1.shape
    tm = _rows(s)

    def body(dh2_ref, dh2b_ref, x_ref, cq_ref, k_ref, v_ref, g_ref, gn_ref, wq_ref, wo_ref,
             dcq_ref, dk_ref, dv_ref, dg_ref, dx_ref, dxb_ref, dgn_ref, do_ref):
        @pl.when(pl.program_id(0) == 0)
        def _():
            dk_ref[...] = jnp.zeros_like(dk_ref)
            dv_ref[...] = jnp.zeros_like(dv_ref)
            dg_ref[...] = jnp.zeros_like(dg_ref)
            dgn_ref[...] = jnp.zeros_like(dgn_ref)

        do_ref[...] = _dot_nt(dh2b_ref[...], wo_ref[...]).astype(BF16)

        def front(h):
            sl = slice(h * HEAD_DIM, (h + 1) * HEAD_DIM)
            y, r, qn = _cross_q(cq_ref, g_ref, sl)
            qb = qn.astype(BF16)
            do = do_ref[:, sl]
            return y, r, qb, do, _dot_nt(qb, k_ref[:, sl]), _dot_nt(do, v_ref[:, sl])

        dg = jnp.zeros((1, HEAD_DIM), F32)
        nxt = front(0)
        for h in range(ATT_HEADS):
            sl = slice(h * HEAD_DIM, (h + 1) * HEAD_DIM)
            y, r, qb, do, sc, dp = nxt
            if h + 1 < ATT_HEADS:
                nxt = front(h + 1)
            p = jnp.exp(sc - jnp.max(sc, axis=-1, keepdims=True))
            p = p / jnp.sum(p, axis=-1, keepdims=True)
            dv_ref[:, sl] += _dot_tn(p.astype(BF16), do)
            ds = (p * (dp - jnp.sum(dp * p, axis=-1, keepdims=True))).astype(BF16)
            dk_ref[:, sl] += _dot_tn(ds, qb)
            dn = _dot(ds, k_ref[:, sl]) * SCALE
            dy, dgain = _norm_bwd(y, r, g_ref[...], dn)
            dcq_ref[:, sl] = dy.astype(BF16)
            dg = dg + dgain
        dg_ref[...] += dg

        xv = x_ref[...]
        dx, dgain = _norm_bwd(xv, _rms_r(xv), gn_ref[...], _dot_nt(dcq_ref[...], wq_ref[...]))
        out = dh2_ref[...] + dx
        dx_ref[...] = out
        dxb_ref[...] = out.astype(BF16)
        dgn_ref[...] += dgain

    row = lambda w_: pl.BlockSpec((tm, w_), lambda i: (i, 0))
    acc = lambda r_, w_: pl.BlockSpec((r_, w_), lambda i: (0, 0))
    return _pcall(
        body, name=name, grid=(s // tm,),
        in_specs=[row(d), row(d), row(d), row(X_W), _resident(ck_n.shape), _resident(cv.shape), _resident(cq_g.shape),
                  _resident((1, d)), _resident(wcq.shape), _resident(wco.shape)],
        out_specs=[row(X_W), acc(N_MEM, X_W), acc(N_MEM, X_W), acc(1, HEAD_DIM), row(d), row(d), acc(1, d)],
        out_shape=[SDS((s, X_W), BF16), SDS((N_MEM, X_W), F32), SDS((N_MEM, X_W), F32), SDS((1, HEAD_DIM), F32),
                   SDS((s, d), F32), SDS((s, d), BF16), SDS((1, d), F32)],
        scratch_shapes=[pltpu.VMEM((tm, X_W), BF16)],
        compiler_params=_cp())(dh2, dh2b, h1, cq_raw, ck_n, cv, cq_g, gain, wcq, wco)


def _mem_bwd(dck_n, dcv, ck_raw, memn, mem, wckv, mem_g, ck_g, *, name):
    def body(dk_ref, dv_ref, ckr_ref, mn_ref, m_ref, w_ref, mg_ref, kg_ref, dw_ref, dmg_ref, dkg_ref, dckv_s):
        dkg = jnp.zeros((1, HEAD_DIM), F32)
        for h in range(ATT_HEADS):
            sl = slice(h * HEAD_DIM, (h + 1) * HEAD_DIM)
            y = ckr_ref[:, sl]
            dy, dgain = _norm_bwd(y, _rms_r(y), kg_ref[...], dk_ref[:, sl])
            dckv_s[:, sl] = dy.astype(BF16)
            dkg = dkg + dgain
        dkg_ref[...] = dkg
        dckv_s[:, X_W:] = dv_ref[...].astype(BF16)
        dckv = dckv_s[...]
        dw_ref[...] = _dot_tn(mn_ref[...], dckv).astype(BF16)
        dmn = _dot_nt(dckv, w_ref[...])
        mv = m_ref[...]
        dmg_ref[...] = jnp.sum(dmn * mv * _rms_r(mv), axis=0, keepdims=True)

    return _pcall(
        body, name=name,
        out_shape=[SDS((D_MODEL, 2 * X_W), BF16), SDS((1, D_MODEL), F32), SDS((1, HEAD_DIM), F32)],
        scratch_shapes=[pltpu.VMEM((N_MEM, 2 * X_W), BF16)],
        compiler_params=_cp())(dck_n, dcv, ck_raw, memn, mem, wckv, mem_g, ck_g)


def _rope_tables(pos):
    inv_freq = ROPE_THETA ** (-jnp.arange(0, ROT_DIM, 2, dtype=F32) / ROT_DIM)
    ang = pos.astype(F32)[:, None] * inv_freq
    cos, sin = jnp.cos(ang), jnp.sin(ang)
    s = pos.shape[0]
    rest = HEAD_DIM - ROT_DIM
    cos_t = jnp.concatenate([cos, cos, jnp.ones((s, rest), F32)], axis=1)
    sin_t = jnp.concatenate([-sin, sin, jnp.zeros((s, rest), F32)], axis=1)
    return cos_t, sin_t


def _local_step(x, mem, pos, tgt, w_in, shards, sm):
    cos_t, sin_t = _rope_tables(pos)
    wb = {"w_in": w_in}

    pick = lambda *names: {a: shards[a] for a in names}
    (proj, xn, q_rot, k_rot, v), got = _in_proj(x, sm["mix_norm_g"], wb["w_in"], cos_t, sin_t, sm["q_norm_g"],
                                                sm["k_norm_g"], pick("w_out", "w_cq", "w_ckv", "w_co"),
                                                name="fwd_in_proj")
    wb.update(got)
    mix, lse_all, got = _attn_fwd(q_rot, k_rot, v, pick("w_down", "w_gate_up"), name="fwd_attn")
    wb.update(got)
    mix = _pool_fwd(proj, mix, sm["pool_w"], sm["pool_scale"], name="fwd_pool")
    h1 = _linear_res(mix, wb["w_out"], x, cn=512, name="fwd_out_proj")
    memn, ck_raw, ck_n, cv = _mem_fwd(mem, sm["mem_norm_g"], wb["w_ckv"], sm["ck_norm_g"], name="fwd_mem")
    h2, hn, cq_raw, co = _cross_fwd(h1, sm["cross_norm_g"], wb["w_cq"], ck_n, cv, sm["cq_norm_g"], wb["w_co"],
                                    name="fwd_cross")
    gu, act, fn = _norm_linear_swiglu(h2, sm["ffn_norm_g"], wb["w_gate_up"], name="fwd_gate_up")
    dy, dyb, sq = _linear_res_loss(act, wb["w_down"], h2, tgt, name="fwd_down_loss")

    gw = {}
    gs = {}
    dgu = _swiglu_bwd(dyb, wb["w_down"], gu, name="bwd_swiglu")
    gw_down = _dw_tn(act, dyb, tkw=1024, tn=1024, name="bwd_dw_down")
    dh2, dh2b, gs["ffn_norm_g"], parts = _linear_nt_normbwd(dgu, wb["w_gate_up"], h2, dy, sm["ffn_norm_g"],
                                                            name="bwd_ffn_in", grads={"w_down": gw_down})
    gw["w_gate_up"] = _dw_tn(fn, dgu, tkw=1024, tn=1536, name="bwd_dw_gate_up")

    gw["w_co"] = _dw_tn(co, dh2b, tkw=512, tn=1024, name="bwd_dw_co")
    dcq, dck_n, dcv, gs["cq_norm_g"], dh1, dh1b, gs["cross_norm_g"] = _cross_bwd(
        dh2, dh2b, h1, cq_raw, ck_n, cv, sm["cq_norm_g"], sm["cross_norm_g"], wb["w_cq"], wb["w_co"], name="bwd_cross")
    gw["w_ckv"], gs["mem_norm_g"], gs["ck_norm_g"] = _mem_bwd(dck_n, dcv, ck_raw, memn, mem, wb["w_ckv"],
                                                             sm["mem_norm_g"], sm["ck_norm_g"], name="bwd_mem")
    gw["w_cq"] = _dw_tn(hn, dcq, tkw=1024, tn=512, name="bwd_dw_cq")

    dmix = _linear_nt(dh1b, wb["w_out"], cn=512, name="bwd_out_proj")
    gw["w_out"] = _dw_tn(mix, dh1b, tkw=1024, tn=1024, name="bwd_dw_out")
    du, gs["pool_w"], gs["pool_scale"] = _pool_bwd(proj, dmix, sm["pool_w"], sm["pool_scale"], name="bwd_pool")
    dq, dk, dv, got = _attn_bwd(q_rot, k_rot, v, mix, dmix, lse_all, gw, name="bwd_attn")
    parts.update(got)
    dproj, gs["q_norm_g"], gs["k_norm_g"] = _qk_prep_bwd(proj, dq, dk, dv, du, cos_t, sin_t, sm["q_norm_g"],
                                                         sm["k_norm_g"], name="bwd_qk_prep")
    gw_in = _dw_tn(xn, dproj, tkw=1024, tn=1536, name="bwd_dw_in")
    dx, _, gs["mix_norm_g"], last = _linear_nt_normbwd(dproj, wb["w_in"], x, dh1, sm["mix_norm_g"],
                                                       name="bwd_in_proj", grads={"w_in": gw_in})
    parts.update(last)
    return sq, dx, parts, gs


SHARDED = ("w_in", "w_out", "w_cq", "w_ckv", "w_co", "w_gate_up", "w_down")
SMALL = ("mix_norm_g", "q_norm_g", "k_norm_g", "pool_w", "pool_scale", "cross_norm_g", "mem_norm_g", "cq_norm_g",
         "ck_norm_g", "ffn_norm_g")
FULL = {
    "w_in": ((D_MODEL, IN_W), 1, IN_W // N_DEV),
    "w_out": ((D_MODEL, D_MODEL), 0, D_MODEL // N_DEV),
    "w_cq": ((D_MODEL, X_W), 0, D_MODEL // N_DEV),
    "w_ckv": ((D_MODEL, 2 * X_W), 0, D_MODEL // N_DEV),
    "w_co": ((X_W, D_MODEL), 1, D_MODEL // N_DEV),
    "w_gate_up": ((D_MODEL, 2 * FF_PAD), 1, FF_TILE),
    "w_down": ((FF_PAD, D_MODEL), 0, DOWN_SHARD),
}


def _shard_shape(name):
    shape, axis, width = FULL[name]
    return tuple(width if a == axis else n for a, n in enumerate(shape))


def _window(ref, name, dev):
    _, axis, width = FULL[name]
    if name == "w_down":
        start = pl.multiple_of((dev // 2) * FF_TILE + (dev % 2) * DOWN_SHARD, HALO)
    else:
        start = pl.multiple_of(dev * width, BLOCK)
    return ref.at[pl.ds(start, width), :] if axis == 0 else ref.at[:, pl.ds(start, width)]


def _mesh_place():
    x, y, c = lax.axis_index("x"), lax.axis_index("y"), lax.axis_index("c")
    return x, y, c, 4 * x + 2 * y + c


def _peer(x, y, c, k):
    px = 1 - x if k & 4 else x
    py = 1 - y if k & 2 else y
    pc = 1 - c if k & 1 else c
    return (px, py, pc), 4 * px + 2 * py + pc


HBM_SPEC = pl.BlockSpec(memory_space=pltpu.HBM)


def _comm_sems(n):
    return [pltpu.SemaphoreType.DMA((n, N_DEV - 1)), pltpu.SemaphoreType.DMA((n, N_DEV - 1)),
            pltpu.SemaphoreType.DMA((n,))]


DIRECT = (1, 2, 4, 6)
RELAYED = (2, 4, 6)


def _relay_step(steps):
    return max(1, (steps * 11) // 16) if steps > 1 else 0


def _gather_copies(names, ins, outs, send_sems, recv_sems, local_sems):
    x, y, c, me = _mesh_place()
    sibling, _ = _peer(x, y, c, 1)
    local, direct, relays, recv = [], [], {}, {}
    for wi, name in enumerate(names):
        def sems(k, to):
            return dict(send_sem=send_sems.at[wi, k - 1], recv_sem=recv_sems.at[wi, k - 1], device_id=to,
                        device_id_type=MESH)
        local.append(pltpu.make_async_copy(ins[wi], _window(outs[wi], name, me), local_sems.at[wi]))
        for k in range(1, N_DEV):
            peer, pidx = _peer(x, y, c, k)
            win = _window(outs[wi], name, pidx)
            recv[wi, k] = pltpu.make_async_remote_copy(src_ref=ins[wi], dst_ref=win, **sems(k, peer))
            if k in DIRECT:
                direct.append(pltpu.make_async_remote_copy(src_ref=ins[wi], dst_ref=_window(outs[wi], name, me),
                                                           **sems(k, peer)))
            if k in RELAYED:
                relays[wi, k] = pltpu.make_async_remote_copy(src_ref=win, dst_ref=win, **sems(k + 1, sibling))
    return local, direct, relays, recv


def _exchange_copies(names, ins, outs, send_sems, recv_sems, local_sems):
    x, y, c, me = _mesh_place()
    local, sent, recv = [], [], []
    for wi, name in enumerate(names):
        local.append(pltpu.make_async_copy(_window(ins[wi], name, me), outs[wi].at[0], local_sems.at[wi]))
        for k in range(1, N_DEV):
            peer, pidx = _peer(x, y, c, k)
            sems = dict(send_sem=send_sems.at[wi, k - 1], recv_sem=recv_sems.at[wi, k - 1], device_id=peer,
                        device_id_type=MESH)
            sent.append(pltpu.make_async_remote_copy(src_ref=_window(ins[wi], name, pidx), dst_ref=outs[wi].at[k], **sems))
            recv.append(pltpu.make_async_remote_copy(src_ref=_window(ins[wi], name, me), dst_ref=outs[wi].at[k], **sems))
    return local, sent, recv


def _down_pads(down_ref, zero_ref, zero_sems):
    return [pltpu.make_async_copy(zero_ref, down_ref.at[pl.ds(t * FF_TILE + FF_SHARD, FF_TILE - FF_SHARD), :],
                                  zero_sems.at[t]) for t in range(FF_PAD // FF_TILE)]


def _gather_scratch(names):
    if not names:
        return []
    pad = [pltpu.VMEM((FF_TILE - FF_SHARD, D_MODEL), BF16), pltpu.SemaphoreType.DMA((FF_PAD // FF_TILE,))]
    return _comm_sems(len(names)) + (pad if "w_down" in names else [])


def _gather_ops(names, ins, outs, scratch):
    local, direct, relays, recv = _gather_copies(names, ins, outs, *scratch[:3])
    pads = _down_pads(outs[names.index("w_down")], scratch[3], scratch[4]) if "w_down" in names else []

    def start():
        if pads:
            scratch[3][...] = jnp.zeros_like(scratch[3])
        for cp in local + direct + pads:
            cp.start()

    def relay(only=None):
        for (wi, k), cp in relays.items():
            if only is None or names[wi] in only:
                recv[wi, k].wait_recv()
                cp.start()

    def finish():
        for (wi, k), cp in recv.items():
            if k not in RELAYED:
                cp.wait_recv()
        for cp in direct + list(relays.values()):
            cp.wait_send()
        for cp in local + pads:
            cp.wait()

    return start, relay, finish


def _start(copies):
    local, sent, _ = copies
    for cp in local + sent:
        cp.start()


def _finish(copies):
    local, sent, recv = copies
    for cp in recv:
        cp.wait_recv()
    for cp in sent:
        cp.wait_send()
    for cp in local:
        cp.wait()


def _gather_weights(shards):
    names = tuple(shards)
    nw = len(names)

    def body(*refs):
        start, relay, finish = _gather_ops(names, refs[:nw], refs[nw:2 * nw], refs[2 * nw:])
        start()
        relay()
        finish()

    outs = _pcall(
        body, name="gather_weights",
        in_specs=[HBM_SPEC] * nw, out_specs=[HBM_SPEC] * nw,
        out_shape=[SDS(FULL[n][0], BF16) for n in names],
        scratch_shapes=_comm_sems(nw))(*[shards[n] for n in names])
    return dict(zip(names, outs))


def _exchange_small(blocks):
    nb = len(blocks)

    def body(*refs):
        ins, outs = refs[:nb], refs[nb:2 * nb]
        send_sems, recv_sems, local_sems = refs[2 * nb:]
        x, y, c, me = _mesh_place()
        local, sent, recv = [], [], []
        for bi in range(nb):
            local.append(pltpu.make_async_copy(ins[bi], outs[bi].at[me], local_sems.at[bi]))
            for k in range(1, N_DEV):
                peer, pidx = _peer(x, y, c, k)
                sems = dict(send_sem=send_sems.at[bi, k - 1], recv_sem=recv_sems.at[bi, k - 1], device_id=peer,
                            device_id_type=MESH)
                sent.append(pltpu.make_async_remote_copy(src_ref=ins[bi], dst_ref=outs[bi].at[me], **sems))
                recv.append(pltpu.make_async_remote_copy(src_ref=ins[bi], dst_ref=outs[bi].at[pidx], **sems))
        _start((local, sent, recv))
        _finish((local, sent, recv))

    return _pcall(
        body, name="exchange_small", in_specs=[HBM_SPEC] * nb, out_specs=[HBM_SPEC] * nb,
        out_shape=[SDS((N_DEV,) + a.shape, F32) for a in blocks], scratch_shapes=_comm_sems(nb))(*blocks)


def _adam_math(g, w, m, v):
    m_new = ADAM_B1 * m + (1.0 - ADAM_B1) * g
    v_new = ADAM_B2 * v + (1.0 - ADAM_B2) * (g * g)
    m_hat = m_new / (1.0 - ADAM_B1 ** ADAM_STEP)
    v_hat = v_new / (1.0 - ADAM_B2 ** ADAM_STEP)
    return -ADAM_LR * (m_hat / (jnp.sqrt(v_hat) + ADAM_EPS) + ADAM_WD * w), m_new, v_new


def _adamw_small(parts, w, m, v, sq_parts, *, name):
    n = len(parts)

    def body(*refs):
        p_refs, w_refs, m_refs, v_refs = refs[:n], refs[n:2 * n], refs[2 * n:3 * n], refs[3 * n:4 * n]
        sq_ref, outs = refs[4 * n], refs[4 * n + 1:]
        for i in range(n):
            g = p_refs[i][0]
            for k in range(1, N_DEV):
                g = g + p_refs[i][k]
            delta, m_new, v_new = _adam_math(g, w_refs[i][...], m_refs[i][...], v_refs[i][...])
            outs[4 * i][...] = g
            outs[4 * i + 1][...] = delta
            outs[4 * i + 2][...] = m_new
            outs[4 * i + 3][...] = v_new
        tot = sq_ref[0]
        for k in range(1, N_DEV):
            tot = tot + sq_ref[k]
        outs[4 * n][...] = (0.5 / D_MODEL) * jnp.sum(tot, axis=1, keepdims=True)

    out_shape = [SDS(a.shape, F32) for a in w for _ in range(4)] + [SDS((1, 1), F32)]
    outs = _pcall(body, name=name, out_shape=out_shape, compiler_params=_cp())(*parts, *w, *m, *v, sq_parts)
    return [outs[4 * i:4 * i + 4] for i in range(n)], outs[4 * n][0, 0]


def _adamw(parts, w, m, v, *, name):
    r, c = w.shape
    pc = parts.shape[2]
    tr = r
    for cand in (256, 128, 88):
        if r % cand == 0:
            tr = cand
            break

    def body(p_ref, w_ref, m_ref, v_ref, g_ref, d_ref, mo_ref, vo_ref):
        g = p_ref[0, :, :c].astype(F32)
        for k in range(1, N_DEV):
            g = g + p_ref[k, :, :c].astype(F32)
        g_ref[...] = g
        d_ref[...], mo_ref[...], vo_ref[...] = _adam_math(g, w_ref[...], m_ref[...], v_ref[...])

    row = pl.BlockSpec((tr, c), lambda i: (i, 0))
    return _pcall(
        body, name=name, grid=(r // tr,),
        in_specs=[pl.BlockSpec((N_DEV, tr, pc), lambda i: (0, i, 0)), row, row, row],
        out_specs=[row] * 4, out_shape=[SDS((r, c), F32)] * 4, compiler_params=_cp())(parts, w, m, v)


def _pad_cols(a, width):
    return jnp.pad(a, ((0, 0), (0, width - a.shape[1])))


def kernel(x, mem, positions, mix_norm_g, w_in, q_norm_g, k_norm_g, pool_w, pool_scale, w_out, cross_norm_g, mem_norm_g, w_cq, w_ckv, cq_norm_g, ck_norm_g, w_co, ffn_norm_g, w_gate_up, w_down, loss_target, m_mix_norm_g, m_w_in, m_q_norm_g, m_k_norm_g, m_pool_w, m_pool_scale, m_w_out, m_cross_norm_g, m_mem_norm_g, m_w_cq, m_w_ckv, m_cq_norm_g, m_ck_norm_g, m_w_co, m_ffn_norm_g, m_w_gate_up, m_w_down, v_mix_norm_g, v_w_in, v_q_norm_g, v_k_norm_g, v_pool_w, v_pool_scale, v_w_out, v_cross_norm_g, v_mem_norm_g, v_w_cq, v_w_ckv, v_cq_norm_g, v_ck_norm_g, v_w_co, v_ffn_norm_g, v_w_gate_up, v_w_down):
    given = dict(locals())
    w_f32 = {n: given[n][0] for n in SHARDED + SMALL}
    m_f32 = {n: given["m_" + n][0] for n in SHARDED + SMALL}
    v_f32 = {n: given["v_" + n][0] for n in SHARDED + SMALL}
    shards = {n: w_f32[n].astype(BF16) for n in SHARDED}
    shards["w_gate_up"] = _pad_cols(shards["w_gate_up"], FF_TILE)
    w_in_full = _gather_weights({"w_in": shards.pop("w_in")})["w_in"]
    sm_rows = {n: (w_f32[n] if w_f32[n].ndim == 3 else w_f32[n].reshape(1, -1)) for n in SMALL}
    sq, dx, parts, gs = _local_step(x[0], mem[0], positions[0], loss_target[0], w_in_full, shards, sm_rows)

    flat = lambda a: a.reshape(-1, a.shape[-1])
    got = _exchange_small([flat(gs[n]) for n in SMALL] + [sq])
    small_res, loss = _adamw_small(got[:-1], [flat(sm_rows[n]) for n in SMALL],
                                   [flat(m_f32[n].reshape(sm_rows[n].shape)) for n in SMALL],
                                   [flat(v_f32[n].reshape(sm_rows[n].shape)) for n in SMALL], got[-1],
                                   name="adamw_small")
    res = {n: [a.reshape(w_f32[n].shape) for a in small_res[i]] for i, n in enumerate(SMALL)}
    for n in SHARDED:
        res[n] = _adamw(parts[n], w_f32[n], m_f32[n], v_f32[n], name="adamw_" + n)
    order = ("mix_norm_g", "w_in", "q_norm_g", "k_norm_g", "pool_w", "pool_scale", "w_out", "cross_norm_g",
             "mem_norm_g", "w_cq", "w_ckv", "cq_norm_g", "ck_norm_g", "w_co", "ffn_norm_g", "w_gate_up", "w_down")
    outs = [loss, dx[None]]
    for which in range(4):
        outs += [res[n][which][None] for n in order]
    return tuple(outs)
```

```python
import functools

import jax
import jax.numpy as jnp
from jax import lax
from jax.experimental import pallas as pl
from jax.experimental.pallas import tpu as pltpu

F32 = jnp.float32
BF16 = jnp.bfloat16
SDS = jax.ShapeDtypeStruct

D_MODEL = 1024
HEAD_DIM = 128
N_GROUPS = 3
DILATIONS = (1, 4, 16)
ATT_HEADS = 4
Q_W = 1536
KV_W = 512
POOL_W = 512
POOL_WINDOWS = (2, 4, 8, 16)
IN_W = 3072
X_W = 512
N_MEM = 256
D_FF = 2816
FF_TILE = 768
FF_SHARD = 704
FF_PAD = 4 * FF_TILE
DOWN_SHARD = 352
ROT_DIM = 32
ROT_HALF = 16
ROPE_THETA = 500000.0
EPS = 1e-6
NEG_INF = -1e30
SCALE = HEAD_DIM ** -0.5
BLOCK = 128
HALO = 16

ADAM_LR = 0.001
ADAM_B1 = 0.9
ADAM_B2 = 0.999
ADAM_EPS = 1e-08
ADAM_WD = 0.01
ADAM_STEP = 10

N_DEV = 8
VMEM_LIMIT_BYTES = 56 * 1024 * 1024
MESH = pl.DeviceIdType.MESH


def _pcall(body, **kw):
    return pl.pallas_call(body, **kw)


def _cp():
    return pltpu.CompilerParams(vmem_limit_bytes=VMEM_LIMIT_BYTES)


def _dot(a, b):
    return lax.dot_general(a, b, (((1,), (0,)), ((), ())), preferred_element_type=F32)


def _dot_nt(a, b):
    return lax.dot_general(a, b, (((1,), (1,)), ((), ())), preferred_element_type=F32)


def _dot_tn(a, b):
    return lax.dot_general(a, b, (((0,), (0,)), ((), ())), preferred_element_type=F32)


def _rows(s):
    return min(512, s)


def _rms_r(x):
    return lax.rsqrt(jnp.mean(x * x, axis=-1, keepdims=True) + EPS)


def _norm_bwd(x, r, gain, dxn):
    z = dxn * gain
    dx = r * z - x * (r * r * r * jnp.mean(z * x, axis=-1, keepdims=True))
    dgain = jnp.sum(dxn * x * r, axis=0, keepdims=True)
    return dx, dgain


def _split_bf16(t):
    hi = t.astype(BF16)
    return hi, (t - hi.astype(F32)).astype(BF16)


def _lane_sums(t, ones):
    hi, lo = _split_bf16(t)
    return _dot(hi, ones) + _dot(lo, ones)


def _head_r(y, ones):
    return lax.rsqrt(_lane_sums(y * y, ones) * (1.0 / HEAD_DIM) + EPS)


def _head_norm_bwd(y, r, gain, dn, ones):
    z = dn * gain
    dy = r * z - y * (r * r * r * (_dot((z * y).astype(BF16), ones) * (1.0 / HEAD_DIM)))
    return dy, jnp.sum(dn * y * r, axis=0, keepdims=True)


def _swap_matrix():
    src = lax.broadcasted_iota(jnp.int32, (HEAD_DIM, HEAD_DIM), 0)
    dst = lax.broadcasted_iota(jnp.int32, (HEAD_DIM, HEAD_DIM), 1)
    hit = ((dst < ROT_HALF) & (src == dst + ROT_HALF)) | ((dst >= ROT_HALF) & (dst < ROT_DIM) & (src == dst - ROT_HALF))
    return jnp.where(hit, 1.0, 0.0).astype(BF16)


def _partner(t, swap):
    hi, lo = _split_bf16(t)
    return _dot(hi, swap) + _dot(lo, swap)


def _rope(n, cos_t, sin_t, swap):
    return n * cos_t + _partner(n, swap) * sin_t


def _rope_bwd(d, cos_t, sin_t, swap):
    return d * cos_t + _dot((d * sin_t).astype(BF16), swap)


def _resident(shape):
    return pl.BlockSpec(shape, lambda i: (0,) * len(shape), pipeline_mode=pl.Buffered(1))


def _chunks(n, cn):
    return [slice(j * cn, (j + 1) * cn) for j in range(n // cn)]


def _norm_linear(x, gain, w, *, cn, out_dtype, name, shards=None):
    s, k = x.shape
    n = w.shape[1]
    tm = _rows(s)
    names = tuple(shards or ())
    nr = len(names)

    def body(*refs):
        x_ref, g_ref, w_ref = refs[:3]
        y_ref, xn_ref = refs[3 + nr:5 + nr]
        if nr:
            start, relay, finish = _gather_ops(names, refs[3:3 + nr], refs[5 + nr:5 + 2 * nr], refs[5 + 2 * nr:])
            pl.when(pl.program_id(0) == 0)(start)
            pl.when(pl.program_id(0) == _relay_step(s // tm))(relay)
        xv = x_ref[...]
        xn_ref[...] = (xv * _rms_r(xv) * g_ref[...]).astype(BF16)
        for c in _chunks(n, cn):
            y_ref[:, c] = _dot(xn_ref[...], w_ref[:, c]).astype(out_dtype)
        if nr:
            pl.when(pl.program_id(0) == s // tm - 1)(finish)

    row = lambda w_: pl.BlockSpec((tm, w_), lambda i: (i, 0))
    outs = _pcall(
        body, name=name, grid=(s // tm,),
        in_specs=[row(k), _resident((1, k)), _resident((k, n))] + [HBM_SPEC] * nr,
        out_specs=[row(n), row(k)] + [HBM_SPEC] * nr,
        out_shape=[SDS((s, n), out_dtype), SDS((s, k), BF16)] + [SDS(FULL[a][0], BF16) for a in names],
        scratch_shapes=_gather_scratch(names),
        compiler_params=_cp())(x, gain, w, *[shards[a] for a in names])
    return (outs[0], outs[1], dict(zip(names, outs[2:]))) if nr else tuple(outs)


def _norm_linear_swiglu(x, gain, wgu, *, name):
    s, k = x.shape
    tm = _rows(s)

    def body(x_ref, g_ref, w_ref, gu_ref, a_ref, xn_ref):
        xv = x_ref[...]
        xn_ref[...] = (xv * _rms_r(xv) * g_ref[...]).astype(BF16)
        for c in _chunks(FF_PAD, FF_TILE):
            g = _dot(xn_ref[...], w_ref[:, c])
            u = _dot(xn_ref[...], w_ref[:, slice(FF_PAD + c.start, FF_PAD + c.stop)])
            a_ref[:, c] = (g * jax.nn.sigmoid(g) * u).astype(BF16)
            gu_ref[0, :, c] = g.astype(BF16)
            gu_ref[1, :, c] = u.astype(BF16)

    row = lambda w_: pl.BlockSpec((tm, w_), lambda i: (i, 0))
    return _pcall(
        body, name=name, grid=(s // tm,),
        in_specs=[row(k), _resident((1, k)), _resident((k, 2 * FF_PAD))],
        out_specs=[pl.BlockSpec((2, tm, FF_PAD), lambda i: (0, i, 0)), row(FF_PAD), row(k)],
        out_shape=[SDS((2, s, FF_PAD), BF16), SDS((s, FF_PAD), BF16), SDS((s, k), BF16)],
        compiler_params=_cp())(x, gain, wgu)


def _linear_res(a, w, res, *, cn, name):
    s, k = a.shape
    n = w.shape[1]
    tm = _rows(s)

    def body(a_ref, w_ref, r_ref, y_ref):
        for c in _chunks(n, cn):
            y_ref[:, c] = r_ref[:, c] + _dot(a_ref[...], w_ref[:, c])

    row = lambda w_: pl.BlockSpec((tm, w_), lambda i: (i, 0))
    return _pcall(
        body, name=name, grid=(s // tm,),
        in_specs=[row(k), _resident((k, n)), row(n)], out_specs=row(n),
        out_shape=SDS((s, n), F32), compiler_params=_cp())(a, w, res)


def _linear_res_loss(a, w, res, tgt, *, name):
    s, k = a.shape
    n = w.shape[1]
    tm = _rows(s)

    def body(a_ref, w_ref, r_ref, t_ref, dy_ref, dyb_ref, sq_ref):
        e = r_ref[...] + _dot(a_ref[...], w_ref[...]) - t_ref[...]
        dy = e * (1.0 / n)
        dy_ref[...] = dy
        dyb_ref[...] = dy.astype(BF16)

        @pl.when(pl.program_id(0) == 0)
        def _():
            sq_ref[...] = jnp.zeros_like(sq_ref)
        sq_ref[...] += jnp.sum(e * e, axis=0, keepdims=True)

    row = lambda w_: pl.BlockSpec((tm, w_), lambda i: (i, 0))
    return _pcall(
        body, name=name, grid=(s // tm,),
        in_specs=[row(k), _resident((k, n)), row(n), row(n)],
        out_specs=[row(n), row(n), pl.BlockSpec((1, n), lambda i: (0, 0))],
        out_shape=[SDS((s, n), F32), SDS((s, n), BF16), SDS((1, n), F32)],
        compiler_params=_cp())(a, w, res, tgt)


def _linear_nt(g, w, *, cn, name):
    s, k = g.shape
    n = w.shape[0]
    tm = _rows(s)

    def body(g_ref, w_ref, y_ref):
        for c in _chunks(n, cn):
            y_ref[:, c] = _dot_nt(g_ref[...], w_ref[c, :]).astype(BF16)

    row = lambda w_: pl.BlockSpec((tm, w_), lambda i: (i, 0))
    return _pcall(
        body, name=name, grid=(s // tm,),
        in_specs=[row(k), _resident((n, k))], out_specs=row(n),
        out_shape=SDS((s, n), BF16), compiler_params=_cp())(g, w)


def _swiglu_bwd(dyb, wd, gu, *, name):
    s, n = dyb.shape
    tm = _rows(s)

    def body(dy_ref, wd_ref, gu_ref, dgu_ref):
        for c in _chunks(FF_PAD, FF_TILE):
            da = _dot_nt(dy_ref[...], wd_ref[c, :])
            g = gu_ref[0, :, c].astype(F32)
            u = gu_ref[1, :, c].astype(F32)
            sg = jax.nn.sigmoid(g)
            dgu_ref[0, :, c] = (da * u * (sg * (1.0 + g * (1.0 - sg)))).astype(BF16)
            dgu_ref[1, :, c] = (da * (g * sg)).astype(BF16)

    half = pl.BlockSpec((2, tm, FF_PAD), lambda i: (0, i, 0))
    return _pcall(
        body, name=name, grid=(s // tm,),
        in_specs=[pl.BlockSpec((tm, n), lambda i: (i, 0)), _resident((FF_PAD, n)), half],
        out_specs=half, out_shape=SDS((2, s, FF_PAD), BF16), compiler_params=_cp())(dyb, wd, gu)


def _linear_nt_normbwd(g, w, x, dres, gain, *, name, grads=None):
    d, k = w.shape
    s = x.shape[0]
    tm = _rows(s)
    names = tuple(grads or ())
    nr = len(names)

    def body(*refs):
        g_ref, w_ref, x_ref, dr_ref, gn_ref = refs[:5]
        dx_ref, dxb_ref, dg_ref = refs[5 + nr:8 + nr]
        if nr:
            copies = _exchange_copies(names, refs[5:5 + nr], refs[8 + nr:8 + 2 * nr], *refs[8 + 2 * nr:])

        @pl.when(pl.program_id(0) == 0)
        def _():
            dg_ref[...] = jnp.zeros_like(dg_ref)
            if nr:
                _start(copies)

        if g.ndim == 3:
            dxn = _dot_nt(g_ref[0], w_ref[:, :k // 2]) + _dot_nt(g_ref[1], w_ref[:, k // 2:])
        else:
            dxn = _dot_nt(g_ref[...], w_ref[...])
        xv = x_ref[...]
        dx, dgain = _norm_bwd(xv, _rms_r(xv), gn_ref[...], dxn)
        out = dr_ref[...] + dx
        dx_ref[...] = out
        dxb_ref[...] = out.astype(BF16)
        dg_ref[...] += dgain

        if nr:
            @pl.when(pl.program_id(0) == s // tm - 1)
            def _():
                _finish(copies)

    row = pl.BlockSpec((tm, d), lambda i: (i, 0))
    g_spec = (pl.BlockSpec((2, tm, k // 2), lambda i: (0, i, 0)) if g.ndim == 3
              else pl.BlockSpec((tm, k), lambda i: (i, 0)))
    outs = _pcall(
        body, name=name, grid=(s // tm,),
        in_specs=[g_spec, _resident((d, k)), row, row, _resident((1, d))] + [HBM_SPEC] * nr,
        out_specs=[row, row, pl.BlockSpec((1, d), lambda i: (0, 0))] + [HBM_SPEC] * nr,
        out_shape=[SDS((s, d), F32), SDS((s, d), BF16), SDS((1, d), F32)]
        + [SDS((N_DEV,) + _shard_shape(n), BF16) for n in names],
        scratch_shapes=_comm_sems(nr) if nr else [],
        compiler_params=_cp())(g, w, x, dres, gain, *[grads[n] for n in names])
    return (outs[0], outs[1], outs[2], dict(zip(names, outs[3:]))) if nr else tuple(outs)


def _dw_tn(x, g, *, tkw, tn, name):
    s, kw = x.shape
    halves = g.ndim == 3
    n = 2 * g.shape[2] if halves else g.shape[1]
    ts = min(2048, s)
    ns = s // ts
    per_half = n // 2 // tn

    def body(x_ref, g_ref, o_ref, acc_ref):
        ss = pl.program_id(2)

        @pl.when(ss == 0)
        def _():
            acc_ref[...] = jnp.zeros_like(acc_ref)

        acc_ref[...] += _dot_tn(x_ref[...], g_ref[...])

        @pl.when(ss == ns - 1)
        def _():
            o_ref[...] = acc_ref[...].astype(BF16)

    g_spec = (pl.BlockSpec((None, ts, tn), lambda a, b, ss: (b // per_half, ss, b % per_half)) if halves
              else pl.BlockSpec((ts, tn), lambda a, b, ss: (ss, b)))
    return _pcall(
        body, name=name, grid=(kw // tkw, n // tn, ns),
        in_specs=[pl.BlockSpec((ts, tkw), lambda a, b, ss: (ss, a)), g_spec],
        out_specs=pl.BlockSpec((tkw, tn), lambda a, b, ss: (a, b)),
        out_shape=SDS((kw, n), BF16),
        scratch_shapes=[pltpu.VMEM((tkw, tn), F32)], compiler_params=_cp())(x, g)


def _qk_prep(proj, cos_t, sin_t, qg, kg, *, name):
    s = proj.shape[0]
    tm = _rows(s)
    nqh = Q_W // HEAD_DIM

    def body(q_ref, k_ref, v_ref, c_ref, s_ref, qg_ref, kg_ref, qo_ref, ko_ref, vo_ref):
        c, sn = c_ref[...], s_ref[...]
        ones, swap = jnp.ones((HEAD_DIM, HEAD_DIM), BF16), _swap_matrix()
        for h in range(nqh):
            sl = slice(h * HEAD_DIM, (h + 1) * HEAD_DIM)
            y = q_ref[:, sl]
            qo_ref[:, sl] = (_rope(y * _head_r(y, ones) * qg_ref[...], c, sn, swap) * SCALE).astype(BF16)
        for h in range(ATT_HEADS):
            sl = slice(h * HEAD_DIM, (h + 1) * HEAD_DIM)
            y = k_ref[:, sl]
            ko_ref[:, sl] = _rope(y * _head_r(y, ones) * kg_ref[...], c, sn, swap).astype(BF16)
        vo_ref[...] = v_ref[...].astype(BF16)

    row = lambda w, j: pl.BlockSpec((tm, w), lambda i: (i, j))
    one = pl.BlockSpec((1, HEAD_DIM), lambda i: (0, 0))
    return _pcall(
        body, name=name, grid=(s // tm,),
        in_specs=[row(Q_W, 0), row(KV_W, 3), row(KV_W, 4), row(HEAD_DIM, 0), row(HEAD_DIM, 0), one, one],
        out_specs=[row(Q_W, 0), row(KV_W, 0), row(KV_W, 0)],
        out_shape=[SDS((s, Q_W), BF16), SDS((s, KV_W), BF16), SDS((s, KV_W), BF16)],
        compiler_params=_cp())(proj, proj, proj, cos_t, sin_t, qg, kg)


def _qk_prep_bwd(proj, dq, dk, dv, du, cos_t, sin_t, qg, kg, *, name):
    s = proj.shape[0]
    tm = _rows(s)

    def body(q_ref, k_ref, dq_ref, dk_ref, dv_ref, du_ref, c_ref, s_ref, qg_ref, kg_ref, dp_ref, dqg_ref, dkg_ref):
        c, sn = c_ref[...], s_ref[...]

        @pl.when(pl.program_id(0) == 0)
        def _():
            dqg_ref[...] = jnp.zeros_like(dqg_ref)
            dkg_ref[...] = jnp.zeros_like(dkg_ref)

        ones, swap = jnp.ones((HEAD_DIM, HEAD_DIM), BF16), _swap_matrix()
        dqg = jnp.zeros((1, HEAD_DIM), F32)
        for g in range(N_GROUPS):
            for h in range(ATT_HEADS):
                sl = slice(h * HEAD_DIM, (h + 1) * HEAD_DIM)
                col = slice(g * KV_W + h * HEAD_DIM, g * KV_W + (h + 1) * HEAD_DIM)
                y = q_ref[:, col]
                dn = _rope_bwd(dq_ref[g, :, sl].astype(F32) * SCALE, c, sn, swap)
                dy, dgain = _head_norm_bwd(y, _head_r(y, ones), qg_ref[...], dn, ones)
                dp_ref[:, col] = dy.astype(BF16)
                dqg = dqg + dgain
        dqg_ref[...] += dqg

        dkg = jnp.zeros((1, HEAD_DIM), F32)
        for h in range(ATT_HEADS):
            sl = slice(h * HEAD_DIM, (h + 1) * HEAD_DIM)
            y = k_ref[:, sl]
            dn = _rope_bwd(dk_ref[:, sl], c, sn, swap)
            dy, dgain = _head_norm_bwd(y, _head_r(y, ones), kg_ref[...], dn, ones)
            dp_ref[:, Q_W + h * HEAD_DIM:Q_W + (h + 1) * HEAD_DIM] = dy.astype(BF16)
            dkg = dkg + dgain
        dkg_ref[...] += dkg

        dp_ref[:, Q_W + KV_W:Q_W + 2 * KV_W] = dv_ref[...].astype(BF16)
        dp_ref[:, Q_W + 2 * KV_W:] = du_ref[...].astype(BF16)

    row = lambda w, j: pl.BlockSpec((tm, w), lambda i: (i, j))
    one = pl.BlockSpec((1, HEAD_DIM), lambda i: (0, 0))
    return _pcall(
        body, name=name, grid=(s // tm,),
        in_specs=[row(Q_W, 0), row(KV_W, 3), pl.BlockSpec((N_GROUPS, tm, KV_W), lambda i: (0, i, 0))]
        + [row(KV_W, 0)] * 3 + [row(HEAD_DIM, 0), row(HEAD_DIM, 0), one, one],
        out_specs=[row(IN_W, 0), one, one],
        out_shape=[SDS((s, IN_W), BF16), SDS((1, HEAD_DIM), F32), SDS((1, HEAD_DIM), F32)],
        compiler_params=_cp())(proj, proj, dq, dk, dv, du, cos_t, sin_t, qg, kg)


ATT_ROWS = 16 * BLOCK


def _sub(ref, start, d, size=BLOCK):
    return ref[pl.ds(start, size, stride=d), :] if d > 1 else ref[pl.ds(start, size), :]


def _sub_set(ref, start, d, val):
    if d > 1:
        ref[pl.ds(start, BLOCK, stride=d), :] = val
    else:
        ref[pl.ds(start, BLOCK), :] = val


def _band_masks():
    row = lax.broadcasted_iota(jnp.int32, (BLOCK, BLOCK), 0)
    col = lax.broadcasted_iota(jnp.int32, (BLOCK, BLOCK), 1)
    return col <= row, col >= row


def _eye(n=BLOCK):
    row = lax.broadcasted_iota(jnp.int32, (n, n), 0)
    col = lax.broadcasted_iota(jnp.int32, (n, n), 1)
    return jnp.where(row == col, 1.0, 0.0).astype(BF16)


def _attn_fwd(q_rot, k_rot, v, shards, *, name):
    s = q_rot.shape[0]
    rr = ATT_ROWS
    nblk = s // rr
    names = tuple(shards)
    nr = len(names)

    def body(*refs):
        q0, q1, q2, kp, kc, vp, vc = refs[:7]
        mix_ref, lse_ref = refs[7 + nr:9 + nr]
        qs, ks, vs, os_, ls = refs[9 + 2 * nr:14 + 2 * nr]
        h, n = pl.program_id(0), pl.program_id(1)
        start, relay, finish = _gather_ops(names, refs[7:7 + nr], refs[9 + nr:9 + 2 * nr], refs[14 + 2 * nr:])
        pl.when((h == 0) & (n == 0))(start)
        steps = ATT_HEADS * nblk
        early, late = max(1, (steps * 6) // 16), max(1, (steps * 14) // 16)
        pl.when(h * nblk + n == early)(functools.partial(relay, names[:1]))
        pl.when(h * nblk + n == late)(functools.partial(relay, names[1:]))
        for g, q_ref in enumerate((q0, q1, q2)):
            qs[g] = q_ref[...].astype(F32)
        ks[:rr] = kp[...].astype(F32)
        ks[rr:] = kc[...].astype(F32)
        vs[:rr] = vp[...].astype(F32)
        vs[rr:] = vc[...].astype(F32)
        m_cur, m_band = _band_masks()
        mask_in = jnp.concatenate([m_band, m_cur], axis=1)
        mask_first = jnp.concatenate([m_band & (n > 0), m_cur], axis=1)
        ones = jnp.ones((2 * BLOCK, HEAD_DIM), BF16)
        pieces = [(g, d, j * BLOCK * d + r, j) for g, d in enumerate(DILATIONS) for r in range(d)
                  for j in range(rr // (BLOCK * d))]

        def scores(piece):
            g, d, base, j = piece
            q = _sub(qs.at[g], base, d).astype(BF16)
            k2 = _sub(ks, rr + base - BLOCK * d, d, 2 * BLOCK).astype(BF16)
            return jnp.where(mask_first if j == 0 else mask_in, _dot_nt(q, k2), NEG_INF)

        sc = scores(pieces[0])
        for i, (g, d, base, j) in enumerate(pieces):
            cur = sc
            if i + 1 < len(pieces):
                sc = scores(pieces[i + 1])
            m = jnp.max(cur, axis=-1, keepdims=True)
            p = jnp.exp(cur - m).astype(BF16)
            v2 = _sub(vs, rr + base - BLOCK * d, d, 2 * BLOCK).astype(BF16)
            acc_l = _dot(p, jnp.concatenate([v2, ones], axis=1))
            l = acc_l[:, HEAD_DIM:]
            _sub_set(os_.at[g], base, d, acc_l[:, :HEAD_DIM] / l)
            _sub_set(ls.at[g], base, d, m + jnp.log(l))
        for c in _chunks(rr, 2 * BLOCK):
            a, b, cc = ls[0, c, :], ls[1, c, :], ls[2, c, :]
            m = jnp.maximum(jnp.maximum(a, b), cc)
            wa, wb, wc = jnp.exp(a - m), jnp.exp(b - m), jnp.exp(cc - m)
            den = wa + wb + wc
            mix_ref[c, :] = ((wa * os_[0, c, :] + wb * os_[1, c, :] + wc * os_[2, c, :]) / den).astype(BF16)
            lse_ref[c, :] = m + jnp.log(den)

        pl.when((h == ATT_HEADS - 1) & (n == nblk - 1))(finish)

    blk = lambda f: pl.BlockSpec((rr, HEAD_DIM), f)
    prv = lambda n: jnp.maximum(n - 1, 0)
    outs = _pcall(
        body, name=name, grid=(ATT_HEADS, nblk),
        in_specs=[blk(lambda h, n, g=g: (n, g * ATT_HEADS + h)) for g in range(N_GROUPS)]
        + [blk(lambda h, n: (prv(n), h)), blk(lambda h, n: (n, h))] * 2 + [HBM_SPEC] * nr,
        out_specs=[blk(lambda h, n: (n, h)), blk(lambda h, n: (n, h))] + [HBM_SPEC] * nr,
        out_shape=[SDS((s, KV_W + POOL_W), BF16), SDS((s, KV_W), F32)] + [SDS(FULL[w][0], BF16) for w in names],
        scratch_shapes=[pltpu.VMEM((N_GROUPS, rr, HEAD_DIM), F32), pltpu.VMEM((2 * rr, HEAD_DIM), F32),
                        pltpu.VMEM((2 * rr, HEAD_DIM), F32), pltpu.VMEM((N_GROUPS, rr, HEAD_DIM), F32),
                        pltpu.VMEM((N_GROUPS, rr, HEAD_DIM), F32)] + _gather_scratch(names),
        compiler_params=_cp())(q_rot, q_rot, q_rot, k_rot, k_rot, v, v, *[shards[w] for w in names])
    return outs[0], outs[1], dict(zip(names, outs[2:]))


def _attn_bwd(q_rot, k_rot, v, mix, dmix, lse, grads, blocks, *, name):
    s = q_rot.shape[0]
    rr = ATT_ROWS
    nblk = s // rr
    names = tuple(grads)
    nr, nb = len(names), len(blocks)
    nx = nr + nb

    def body(*refs):
        q0, q1, q2, qx0, qx1, qx2, kp, kc, vp, vc, do_c, do_x, o_c, o_x, l_c, l_x = refs[:16]
        dq_ref, dk_ref, dv_ref = refs[16 + nx:19 + nx]
        qs, ks, vs, dos, lss, dls, dqs, dks, dvs = refs[19 + 2 * nx:28 + 2 * nx]
        h, n = pl.program_id(0), pl.program_id(1)
        copies = _exchange_copies(names, refs[16:16 + nr], refs[19 + nx:19 + nx + nr], *refs[28 + 2 * nx:31 + 2 * nx])
        spread = _broadcast_copies(refs[16 + nr:16 + nx], refs[19 + nx + nr:19 + 2 * nx], *refs[31 + 2 * nx:])

        @pl.when((h == 0) & (n == 0))
        def _():
            _start(copies)
            _start(spread)

        for g, (qc_ref, qx_ref) in enumerate(((q0, qx0), (q1, qx1), (q2, qx2))):
            qs[g, :rr] = qc_ref[...].astype(F32)
            qs[g, rr:] = qx_ref[...].astype(F32)
        ks[:rr] = kp[...].astype(F32)
        ks[rr:] = kc[...].astype(F32)
        vs[:rr] = vp[...].astype(F32)
        vs[rr:] = vc[...].astype(F32)
        lss[:rr] = l_c[...]
        lss[rr:] = l_x[...]
        for half, (d_ref, o_ref) in enumerate(((do_c, o_c), (do_x, o_x))):
            for c in _chunks(rr, 2 * BLOCK):
                cs = slice(half * rr + c.start, half * rr + c.stop)
                dof = d_ref[c, :].astype(F32)
                dos[cs, :] = dof
                dls[cs, :] = jnp.broadcast_to(jnp.sum(dof * o_ref[c, :].astype(F32), axis=-1, keepdims=True),
                                              (2 * BLOCK, HEAD_DIM))
        m_cur, m_band = _band_masks()
        mask_in = jnp.concatenate([m_cur, m_band], axis=0)
        mask_last = jnp.concatenate([m_cur, m_band & (n + 1 < nblk)], axis=0)
        m_first = m_band & (n > 0)
        eye = _eye()
        pieces = [(g, d, j * BLOCK * d + r, j, rr // (BLOCK * d)) for g, d in enumerate(DILATIONS) for r in range(d)
                  for j in range(rr // (BLOCK * d))]

        def front(piece):
            g, d, base, _, _ = piece
            q2 = _sub(qs.at[g], base, d, 2 * BLOCK).astype(BF16)
            do2 = _sub(dos, base, d, 2 * BLOCK).astype(BF16)
            k = _sub(ks, rr + base, d).astype(BF16)
            vv = _sub(vs, rr + base, d).astype(BF16)
            return q2, do2, k, _dot_nt(q2, k), _dot_nt(do2, vv)

        def middle(piece, fr, dq_acc):
            g, d, base, j, nsub = piece
            q2, do2, k, s2, dp2 = fr
            if j == 0:
                kp_ = _sub(ks, rr + base - BLOCK * d, d).astype(BF16)
                p0 = jnp.where(m_first, jnp.exp(_dot_nt(q2[:BLOCK], kp_) - _sub(lss, base, d)), 0.0)
                ds0 = p0 * (_dot_nt(do2[:BLOCK], _sub(vs, rr + base - BLOCK * d, d).astype(BF16)) - _sub(dls, base, d))
                dq_acc = _dot(ds0.astype(BF16), kp_)
            p2 = jnp.where(mask_last if j + 1 == nsub else mask_in,
                           jnp.exp(s2 - _sub(lss, base, d, 2 * BLOCK)), 0.0)
            ds2 = (p2 * (dp2 - _sub(dls, base, d, 2 * BLOCK))).astype(BF16)
            dq2 = _dot(ds2, k)
            return dq2[BLOCK:], (_dot_nt(eye, ds2), _dot_nt(eye, p2.astype(BF16)), q2, do2, dq_acc + dq2[:BLOCK])

        def back(piece, tr):
            g, d, base, _, _ = piece
            ds_t, p_t, q2, do2, dq = tr
            _sub_set(dqs.at[g], base, d, dq)
            dk, dv = _dot(ds_t.astype(BF16), q2), _dot(p_t.astype(BF16), do2)
            if g == 0:
                _sub_set(dks, base, d, dk)
                _sub_set(dvs, base, d, dv)
            else:
                _sub_set(dks, base, d, _sub(dks, base, d) + dk)
                _sub_set(dvs, base, d, _sub(dvs, base, d) + dv)

        fr, held, dq_acc = front(pieces[0]), None, None
        for i, piece in enumerate(pieces):
            cur = fr
            if i + 1 < len(pieces):
                fr = front(pieces[i + 1])
            dq_acc, now = middle(piece, cur, dq_acc)
            if held is not None:
                back(pieces[i - 1], held)
            held = now
        back(pieces[-1], held)
        for g in range(N_GROUPS):
            dq_ref[g] = dqs[g].astype(BF16)
        dk_ref[...] = dks[...]
        dv_ref[...] = dvs[...]

        @pl.when((h == ATT_HEADS - 1) & (n == nblk - 1))
        def _():
            _finish(copies)
            _finish(spread)

    blk = lambda f: pl.BlockSpec((rr, HEAD_DIM), f)
    prv = lambda n: jnp.maximum(n - 1, 0)
    nxt = lambda n: jnp.minimum(n + 1, nblk - 1)
    cur_kv = blk(lambda h, n: (n, h))
    outs = _pcall(
        body, name=name, grid=(ATT_HEADS, nblk),
        in_specs=[blk(lambda h, n, g=g: (n, g * ATT_HEADS + h)) for g in range(N_GROUPS)]
        + [blk(lambda h, n, g=g: (nxt(n), g * ATT_HEADS + h)) for g in range(N_GROUPS)]
        + [blk(lambda h, n: (prv(n), h)), cur_kv] * 2
        + [cur_kv, blk(lambda h, n: (nxt(n), h))] * 3 + [HBM_SPEC] * nx,
        out_specs=[pl.BlockSpec((N_GROUPS, rr, HEAD_DIM), lambda h, n: (0, n, h)), cur_kv, cur_kv] + [HBM_SPEC] * nx,
        out_shape=[SDS((N_GROUPS, s, KV_W), BF16), SDS((s, KV_W), F32), SDS((s, KV_W), F32)]
        + [SDS((N_DEV,) + _shard_shape(w), BF16) for w in names] + [SDS((N_DEV,) + a.shape, F32) for a in blocks],
        scratch_shapes=[pltpu.VMEM((N_GROUPS, 2 * rr, HEAD_DIM), F32)] + [pltpu.VMEM((2 * rr, HEAD_DIM), F32)] * 5
        + [pltpu.VMEM((N_GROUPS, rr, HEAD_DIM), F32), pltpu.VMEM((rr, HEAD_DIM), F32), pltpu.VMEM((rr, HEAD_DIM), F32)]
        + _comm_sems(nr) + _comm_sems(nb),
        compiler_params=_cp())(q_rot, q_rot, q_rot, q_rot, q_rot, q_rot, k_rot, k_rot, v, v, dmix, dmix, mix, mix,
                               lse, lse, *[grads[w] for w in names], *blocks)
    return outs[0], outs[1], outs[2], dict(zip(names, outs[3:3 + nr])), list(outs[3 + nr:])


def _pool_d(u_ref, halo_ref, i, tm):
    halo = jnp.where(i > 0, halo_ref[...], 0.0)
    t = i * tm + lax.broadcasted_iota(jnp.int32, (tm, 1), 0)
    out = []
    for g, w in enumerate(POOL_WINDOWS):
        sl = slice(g * HEAD_DIM, (g + 1) * HEAD_DIM)
        u = u_ref[:, sl]
        acc = jnp.concatenate([halo[:, sl], u], axis=0)
        sh = 1
        while sh < w:
            acc = acc + pltpu.roll(acc, sh, 0)
            sh *= 2
        cnt = jnp.minimum(t + 1, w).astype(F32)
        out.append(acc[HALO:, :] / cnt - u)
    return out


def _pool_fwd(proj, mix, pool_w, pool_scale, *, name):
    s = proj.shape[0]
    tm = _rows(s)
    ucol = (IN_W - POOL_W) // POOL_W

    def body(u_ref, halo_ref, mix_in, w_ref, sc_ref, o_ref):
        del mix_in
        dd = _pool_d(u_ref, halo_ref, pl.program_id(0), tm)
        for g in range(len(POOL_WINDOWS)):
            sl = slice(g * HEAD_DIM, (g + 1) * HEAD_DIM)
            y = _dot(dd[g].astype(BF16), w_ref[g].astype(BF16))
            o_ref[:, sl] = (y * sc_ref[:, sl]).astype(BF16)

    return _pcall(
        body, name=name, grid=(s // tm,),
        in_specs=[pl.BlockSpec((tm, POOL_W), lambda i: (i, ucol)),
                  pl.BlockSpec((HALO, POOL_W), lambda i: (jnp.maximum(i * (tm // HALO) - 1, 0), ucol)),
                  pl.BlockSpec(memory_space=pl.ANY),
                  pl.BlockSpec((len(POOL_WINDOWS), HEAD_DIM, HEAD_DIM), lambda i: (0, 0, 0)),
                  pl.BlockSpec((1, POOL_W), lambda i: (0, 0))],
        out_specs=pl.BlockSpec((tm, POOL_W), lambda i: (i, 1)),
        out_shape=SDS(mix.shape, BF16), input_output_aliases={2: 0},
        compiler_params=_cp())(proj, proj, mix, pool_w, pool_scale)


def _pool_bwd(proj, dmix, pool_w, pool_scale, *, name):
    s = proj.shape[0]
    tm = _rows(s)
    nblk = s // tm
    ucol = (IN_W - POOL_W) // POOL_W
    ng = len(POOL_WINDOWS)

    def body(u_ref, halo_ref, dp_ref, dpn_ref, w_ref, sc_ref, du_ref, dw_ref, dsc_ref):
        i = pl.program_id(0)

        @pl.when(i == 0)
        def _():
            dw_ref[...] = jnp.zeros_like(dw_ref)
            dsc_ref[...] = jnp.zeros_like(dsc_ref)

        dd = _pool_d(u_ref, halo_ref, i, tm)
        t = i * tm + lax.broadcasted_iota(jnp.int32, (tm, 1), 0)
        dpn = jnp.where(i + 1 < nblk, dpn_ref[...].astype(F32), 0.0)
        for g, w in enumerate(POOL_WINDOWS):
            sl = slice(g * HEAD_DIM, (g + 1) * HEAD_DIM)
            wg = w_ref[g].astype(BF16)
            db = dd[g].astype(BF16)
            dp = dp_ref[:, sl].astype(F32)
            dsc_ref[:, sl] += jnp.sum(dp * _dot(db, wg), axis=0, keepdims=True)
            dy = (dp * sc_ref[:, sl]).astype(BF16)
            dw_ref[g] += _dot_tn(db, dy)
            g_d = _dot_nt(dy, wg)
            g_dn = _dot_nt((dpn[:, sl] * sc_ref[:, sl]).astype(BF16), wg)
            cnt = jnp.minimum(t + 1, w).astype(F32)
            acc = jnp.concatenate([g_d / cnt, g_dn * (1.0 / w)], axis=0)
            sh = 1
            while sh < w:
                acc = acc + pltpu.roll(acc, tm + HALO - sh, 0)
                sh *= 2
            du_ref[:, sl] = acc[:tm, :] - g_d

    nh = s // HALO
    return _pcall(
        body, name=name, grid=(nblk,),
        in_specs=[pl.BlockSpec((tm, POOL_W), lambda i: (i, ucol)),
                  pl.BlockSpec((HALO, POOL_W), lambda i: (jnp.maximum(i * (tm // HALO) - 1, 0), ucol)),
                  pl.BlockSpec((tm, POOL_W), lambda i: (i, 1)),
                  pl.BlockSpec((HALO, POOL_W), lambda i: (jnp.minimum((i + 1) * (tm // HALO), nh - 1), 1)),
                  pl.BlockSpec((ng, HEAD_DIM, HEAD_DIM), lambda i: (0, 0, 0)),
                  pl.BlockSpec((1, POOL_W), lambda i: (0, 0))],
        out_specs=[pl.BlockSpec((tm, POOL_W), lambda i: (i, 0)),
                   pl.BlockSpec((ng, HEAD_DIM, HEAD_DIM), lambda i: (0, 0, 0)),
                   pl.BlockSpec((1, POOL_W), lambda i: (0, 0))],
        out_shape=[SDS((s, POOL_W), F32), SDS((ng, HEAD_DIM, HEAD_DIM), F32), SDS((1, POOL_W), F32)],
        compiler_params=_cp())(proj, proj, dmix, dmix, pool_w, pool_scale)


def _mem_fwd(mem, mem_g, wckv, ck_g, *, name):
    def body(m_ref, g_ref, w_ref, kg_ref, mn_ref, ckr_ref, ckn_ref, cv_ref):
        mv = m_ref[...]
        mn = (mv * _rms_r(mv) * g_ref[...]).astype(BF16)
        mn_ref[...] = mn
        ckv = _dot(mn, w_ref[...])
        ckr_ref[...] = ckv[:, :X_W]
        cv_ref[...] = ckv[:, X_W:].astype(BF16)
        for h in range(ATT_HEADS):
            sl = slice(h * HEAD_DIM, (h + 1) * HEAD_DIM)
            y = ckv[:, sl]
            ckn_ref[:, sl] = (y * _rms_r(y) * kg_ref[...]).astype(BF16)

    return _pcall(
        body, name=name,
        out_shape=[SDS((N_MEM, D_MODEL), BF16), SDS((N_MEM, X_W), F32), SDS((N_MEM, X_W), BF16),
                   SDS((N_MEM, X_W), BF16)],
        compiler_params=_cp())(mem, mem_g, wckv, ck_g)


def _cross_q(cq_ref, g_ref, sl):
    y = cq_ref[:, sl]
    r = _rms_r(y)
    return y, r, y * r * g_ref[...] * SCALE


def _cross_fwd(h1, gain, wcq, ck_n, cv, cq_g, wco, *, name):
    s, d = h1.shape
    tm = _rows(s)

    def body(x_ref, gn_ref, wq_ref, k_ref, v_ref, g_ref, wo_ref, h2_ref, xn_ref, cq_ref, o_ref):
        xv = x_ref[...]
        xn_ref[...] = (xv * _rms_r(xv) * gn_ref[...]).astype(BF16)
        cq_ref[...] = _dot(xn_ref[...], wq_ref[...])
        for h in range(ATT_HEADS):
            sl = slice(h * HEAD_DIM, (h + 1) * HEAD_DIM)
            _, _, qn = _cross_q(cq_ref, g_ref, sl)
            sc = _dot_nt(qn.astype(BF16), k_ref[:, sl])
            p = jnp.exp(sc - jnp.max(sc, axis=-1, keepdims=True))
            p = p / jnp.sum(p, axis=-1, keepdims=True)
            o_ref[:, sl] = _dot(p.astype(BF16), v_ref[:, sl]).astype(BF16)
        for c in _chunks(d, X_W):
            h2_ref[:, c] = x_ref[:, c] + _dot(o_ref[...], wo_ref[:, c])

    row = lambda w_: pl.BlockSpec((tm, w_), lambda i: (i, 0))
    return _pcall(
        body, name=name, grid=(s // tm,),
        in_specs=[row(d), _resident((1, d)), _resident(wcq.shape), _resident(ck_n.shape), _resident(cv.shape),
                  _resident(cq_g.shape), _resident(wco.shape)],
        out_specs=[row(d), row(d), row(X_W), row(X_W)],
        out_shape=[SDS((s, d), F32), SDS((s, d), BF16), SDS((s, X_W), F32), SDS((s, X_W), BF16)],
        compiler_params=_cp())(h1, gain, wcq, ck_n, cv, cq_g, wco)


def _cross_bwd(dh2, dh2b, h1, cq_raw, ck_n, cv, cq_g, gain, wcq, wco, *, name):
    s, d = h1.shape
    tm = _rows(s)

    def body(dh2_ref, dh2b_ref, x_ref, cq_ref, k_ref, v_ref, g_ref, gn_ref, wq_ref, wo_ref,
             dcq_ref, dk_ref, dv_ref, dg_ref, dx_ref, dxb_ref, dgn_ref, do_ref):
        @pl.when(pl.program_id(0) == 0)
        def _():
            dk_ref[...] = jnp.zeros_like(dk_ref)
            dv_ref[...] = jnp.zeros_like(dv_ref)
            dg_ref[...] = jnp.zeros_like(dg_ref)
            dgn_ref[...] = jnp.zeros_like(dgn_ref)

        do_ref[...] = _dot_nt(dh2b_ref[...], wo_ref[...]).astype(BF16)

        def front(h):
            sl = slice(h * HEAD_DIM, (h + 1) * HEAD_DIM)
            y, r, qn = _cross_q(cq_ref, g_ref, sl)
            qb = qn.astype(BF16)
            do = do_ref[:, sl]
            return y, r, qb, do, _dot_nt(qb, k_ref[:, sl]), _dot_nt(do, v_ref[:, sl])

        dg = jnp.zeros((1, HEAD_DIM), F32)
        nxt = front(0)
        for h in range(ATT_HEADS):
            sl = slice(h * HEAD_DIM, (h + 1) * HEAD_DIM)
            y, r, qb, do, sc, dp = nxt
            if h + 1 < ATT_HEADS:
                nxt = front(h + 1)
            p = jnp.exp(sc - jnp.max(sc, axis=-1, keepdims=True))
            p = p / jnp.sum(p, axis=-1, keepdims=True)
            dv_ref[:, sl] += _dot_tn(p.astype(BF16), do)
            ds = (p * (dp - jnp.sum(dp * p, axis=-1, keepdims=True))).astype(BF16)
            dk_ref[:, sl] += _dot_tn(ds, qb)
            dn = _dot(ds, k_ref[:, sl]) * SCALE
            dy, dgain = _norm_bwd(y, r, g_ref[...], dn)
            dcq_ref[:, sl] = dy.astype(BF16)
            dg = dg + dgain
        dg_ref[...] += dg

        xv = x_ref[...]
        dx, dgain = _norm_bwd(xv, _rms_r(xv), gn_ref[...], _dot_nt(dcq_ref[...], wq_ref[...]))
        out = dh2_ref[...] + dx
        dx_ref[...] = out
        dxb_ref[...] = out.astype(BF16)
        dgn_ref[...] += dgain

    row = lambda w_: pl.BlockSpec((tm, w_), lambda i: (i, 0))
    acc = lambda r_, w_: pl.BlockSpec((r_, w_), lambda i: (0, 0))
    return _pcall(
        body, name=name, grid=(s // tm,),
        in_specs=[row(d), row(d), row(d), row(X_W), _resident(ck_n.shape), _resident(cv.shape), _resident(cq_g.shape),
                  _resident((1, d)), _resident(wcq.shape), _resident(wco.shape)],
        out_specs=[row(X_W), acc(N_MEM, X_W), acc(N_MEM, X_W), acc(1, HEAD_DIM), row(d), row(d), acc(1, d)],
        out_shape=[SDS((s, X_W), BF16), SDS((N_MEM, X_W), F32), SDS((N_MEM, X_W), F32), SDS((1, HEAD_DIM), F32),
                   SDS((s, d), F32), SDS((s, d), BF16), SDS((1, d), F32)],
        scratch_shapes=[pltpu.VMEM((tm, X_W), BF16)],
        compiler_params=_cp())(dh2, dh2b, h1, cq_raw, ck_n, cv, cq_g, gain, wcq, wco)


def _mem_bwd(dck_n, dcv, ck_raw, memn, mem, wckv, mem_g, ck_g, *, name):
    def body(dk_ref, dv_ref, ckr_ref, mn_ref, m_ref, w_ref, mg_ref, kg_ref, dw_ref, dmg_ref, dkg_ref, dckv_s):
        dkg = jnp.zeros((1, HEAD_DIM), F32)
        for h in range(ATT_HEADS):
            sl = slice(h * HEAD_DIM, (h + 1) * HEAD_DIM)
            y = ckr_ref[:, sl]
            dy, dgain = _norm_bwd(y, _rms_r(y), kg_ref[...], dk_ref[:, sl])
            dckv_s[:, sl] = dy.astype(BF16)
            dkg = dkg + dgain
        dkg_ref[...] = dkg
        dckv_s[:, X_W:] = dv_ref[...].astype(BF16)
        dckv = dckv_s[...]
        dw_ref[...] = _dot_tn(mn_ref[...], dckv).astype(BF16)
        dmn = _dot_nt(dckv, w_ref[...])
        mv = m_ref[...]
        dmg_ref[...] = jnp.sum(dmn * mv * _rms_r(mv), axis=0, keepdims=True)

    return _pcall(
        body, name=name,
        out_shape=[SDS((D_MODEL, 2 * X_W), BF16), SDS((1, D_MODEL), F32), SDS((1, HEAD_DIM), F32)],
        scratch_shapes=[pltpu.VMEM((N_MEM, 2 * X_W), BF16)],
        compiler_params=_cp())(dck_n, dcv, ck_raw, memn, mem, wckv, mem_g, ck_g)


def _rope_tables(pos):
    inv_freq = ROPE_THETA ** (-jnp.arange(0, ROT_DIM, 2, dtype=F32) / ROT_DIM)
    ang = pos.astype(F32)[:, None] * inv_freq
    cos, sin = jnp.cos(ang), jnp.sin(ang)
    s = pos.shape[0]
    rest = HEAD_DIM - ROT_DIM
    cos_t = jnp.concatenate([cos, cos, jnp.ones((s, rest), F32)], axis=1)
    sin_t = jnp.concatenate([-sin, sin, jnp.zeros((s, rest), F32)], axis=1)
    return cos_t, sin_t


def _local_step(x, mem, pos, tgt, w_in, shards, sm):
    cos_t, sin_t = _rope_tables(pos)
    wb = {"w_in": w_in}

    pick = lambda *names: {a: shards[a] for a in names}
    proj, xn, got = _norm_linear(x, sm["mix_norm_g"], wb["w_in"], cn=768, out_dtype=F32, name="fwd_in_proj",
                                 shards=pick("w_out", "w_cq", "w_ckv", "w_co"))
    wb.update(got)
    q_rot, k_rot, v = _qk_prep(proj, cos_t, sin_t, sm["q_norm_g"], sm["k_norm_g"], name="fwd_qk_prep")
    mix, lse_all, got = _attn_fwd(q_rot, k_rot, v, pick("w_down", "w_gate_up"), name="fwd_attn")
    wb.update(got)
    mix = _pool_fwd(proj, mix, sm["pool_w"], sm["pool_scale"], name="fwd_pool")
    h1 = _linear_res(mix, wb["w_out"], x, cn=512, name="fwd_out_proj")
    memn, ck_raw, ck_n, cv = _mem_fwd(mem, sm["mem_norm_g"], wb["w_ckv"], sm["ck_norm_g"], name="fwd_mem")
    h2, hn, cq_raw, co = _cross_fwd(h1, sm["cross_norm_g"], wb["w_cq"], ck_n, cv, sm["cq_norm_g"], wb["w_co"],
                                    name="fwd_cross")
    gu, act, fn = _norm_linear_swiglu(h2, sm["ffn_norm_g"], wb["w_gate_up"], name="fwd_gate_up")
    dy, dyb, sq = _linear_res_loss(act, wb["w_down"], h2, tgt, name="fwd_down_loss")

    gw = {}
    gs = {}
    dgu = _swiglu_bwd(dyb, wb["w_down"], gu, name="bwd_swiglu")
    gw_down = _dw_tn(act, dyb, tkw=1024, tn=1024, name="bwd_dw_down")
    dh2, dh2b, gs["ffn_norm_g"], parts = _linear_nt_normbwd(dgu, wb["w_gate_up"], h2, dy, sm["ffn_norm_g"],
                                                            name="bwd_ffn_in", grads={"w_down": gw_down})
    gw["w_gate_up"] = _dw_tn(fn, dgu, tkw=1024, tn=1536, name="bwd_dw_gate_up")

    gw["w_co"] = _dw_tn(co, dh2b, tkw=512, tn=1024, name="bwd_dw_co")
    dcq, dck_n, dcv, gs["cq_norm_g"], dh1, dh1b, gs["cross_norm_g"] = _cross_bwd(
        dh2, dh2b, h1, cq_raw, ck_n, cv, sm["cq_norm_g"], sm["cross_norm_g"], wb["w_cq"], wb["w_co"], name="bwd_cross")
    gw["w_ckv"], gs["mem_norm_g"], gs["ck_norm_g"] = _mem_bwd(dck_n, dcv, ck_raw, memn, mem, wb["w_ckv"],
                                                             sm["mem_norm_g"], sm["ck_norm_g"], name="bwd_mem")
    gw["w_cq"] = _dw_tn(hn, dcq, tkw=1024, tn=512, name="bwd_dw_cq")

    dmix = _linear_nt(dh1b, wb["w_out"], cn=512, name="bwd_out_proj")
    gw["w_out"] = _dw_tn(mix, dh1b, tkw=1024, tn=1024, name="bwd_dw_out")
    du, gs["pool_w"], gs["pool_scale"] = _pool_bwd(proj, dmix, sm["pool_w"], sm["pool_scale"], name="bwd_pool")
    flat = lambda a: a.reshape(-1, a.shape[-1])
    early = [n for n in SMALL if n in gs]
    dq, dk, dv, got, spread = _attn_bwd(q_rot, k_rot, v, mix, dmix, lse_all, gw, [flat(gs[n]) for n in early] + [sq],
                                        name="bwd_attn")
    parts.update(got)
    small_parts = dict(zip(early + ["sq"], spread))
    dproj, gs["q_norm_g"], gs["k_norm_g"] = _qk_prep_bwd(proj, dq, dk, dv, du, cos_t, sin_t, sm["q_norm_g"],
                                                         sm["k_norm_g"], name="bwd_qk_prep")
    gw_in = _dw_tn(xn, dproj, tkw=1024, tn=1536, name="bwd_dw_in")
    dx, _, gs["mix_norm_g"], last = _linear_nt_normbwd(dproj, wb["w_in"], x, dh1, sm["mix_norm_g"],
                                                       name="bwd_in_proj", grads={"w_in": gw_in})
    parts.update(last)
    late = [n for n in SMALL if n not in small_parts]
    small_parts.update(zip(late, _exchange_small([flat(gs[n]) for n in late])))
    return dx, parts, small_parts


SHARDED = ("w_in", "w_out", "w_cq", "w_ckv", "w_co", "w_gate_up", "w_down")
SMALL = ("mix_norm_g", "q_norm_g", "k_norm_g", "pool_w", "pool_scale", "cross_norm_g", "mem_norm_g", "cq_norm_g",
         "ck_norm_g", "ffn_norm_g")
FULL = {
    "w_in": ((D_MODEL, IN_W), 1, IN_W // N_DEV),
    "w_out": ((D_MODEL, D_MODEL), 0, D_MODEL // N_DEV),
    "w_cq": ((D_MODEL, X_W), 0, D_MODEL // N_DEV),
    "w_ckv": ((D_MODEL, 2 * X_W), 0, D_MODEL // N_DEV),
    "w_co": ((X_W, D_MODEL), 1, D_MODEL // N_DEV),
    "w_gate_up": ((D_MODEL, 2 * FF_PAD), 1, FF_TILE),
    "w_down": ((FF_PAD, D_MODEL), 0, DOWN_SHARD),
}


def _shard_shape(name):
    shape, axis, width = FULL[name]
    return tuple(width if a == axis else n for a, n in enumerate(shape))


def _window(ref, name, dev):
    _, axis, width = FULL[name]
    if name == "w_down":
        start = pl.multiple_of((dev // 2) * FF_TILE + (dev % 2) * DOWN_SHARD, HALO)
    else:
        start = pl.multiple_of(dev * width, BLOCK)
    return ref.at[pl.ds(start, width), :] if axis == 0 else ref.at[:, pl.ds(start, width)]


def _mesh_place():
    x, y, c = lax.axis_index("x"), lax.axis_index("y"), lax.axis_index("c")
    return x, y, c, 4 * x + 2 * y + c


def _peer(x, y, c, k):
    px = 1 - x if k & 4 else x
    py = 1 - y if k & 2 else y
    pc = 1 - c if k & 1 else c
    return (px, py, pc), 4 * px + 2 * py + pc


HBM_SPEC = pl.BlockSpec(memory_space=pltpu.HBM)


def _comm_sems(n):
    return [pltpu.SemaphoreType.DMA((n, N_DEV - 1)), pltpu.SemaphoreType.DMA((n, N_DEV - 1)),
            pltpu.SemaphoreType.DMA((n,))]


DIRECT = (1, 2, 4, 6)
RELAYED = (2, 4, 6)


def _relay_step(steps):
    return max(1, (steps * 11) // 16) if steps > 1 else 0


def _gather_copies(names, ins, outs, send_sems, recv_sems, local_sems):
    x, y, c, me = _mesh_place()
    sibling, _ = _peer(x, y, c, 1)
    local, direct, relays, recv = [], [], {}, {}
    for wi, name in enumerate(names):
        def sems(k, to):
            return dict(send_sem=send_sems.at[wi, k - 1], recv_sem=recv_sems.at[wi, k - 1], device_id=to,
                        device_id_type=MESH)
        local.append(pltpu.make_async_copy(ins[wi], _window(outs[wi], name, me), local_sems.at[wi]))
        for k in range(1, N_DEV):
            peer, pidx = _peer(x, y, c, k)
            win = _window(outs[wi], name, pidx)
            recv[wi, k] = pltpu.make_async_remote_copy(src_ref=ins[wi], dst_ref=win, **sems(k, peer))
            if k in DIRECT:
                direct.append(pltpu.make_async_remote_copy(src_ref=ins[wi], dst_ref=_window(outs[wi], name, me),
                                                           **sems(k, peer)))
            if k in RELAYED:
                relays[wi, k] = pltpu.make_async_remote_copy(src_ref=win, dst_ref=win, **sems(k + 1, sibling))
    return local, direct, relays, recv


def _exchange_copies(names, ins, outs, send_sems, recv_sems, local_sems):
    x, y, c, me = _mesh_place()
    local, sent, recv = [], [], []
    for wi, name in enumerate(names):
        local.append(pltpu.make_async_copy(_window(ins[wi], name, me), outs[wi].at[0], local_sems.at[wi]))
        for k in range(1, N_DEV):
            peer, pidx = _peer(x, y, c, k)
            sems = dict(send_sem=send_sems.at[wi, k - 1], recv_sem=recv_sems.at[wi, k - 1], device_id=peer,
                        device_id_type=MESH)
            sent.append(pltpu.make_async_remote_copy(src_ref=_window(ins[wi], name, pidx), dst_ref=outs[wi].at[k], **sems))
            recv.append(pltpu.make_async_remote_copy(src_ref=_window(ins[wi], name, me), dst_ref=outs[wi].at[k], **sems))
    return local, sent, recv


def _down_pads(down_ref, zero_ref, zero_sems):
    return [pltpu.make_async_copy(zero_ref, down_ref.at[pl.ds(t * FF_TILE + FF_SHARD, FF_TILE - FF_SHARD), :],
                                  zero_sems.at[t]) for t in range(FF_PAD // FF_TILE)]


def _gather_scratch(names):
    if not names:
        return []
    pad = [pltpu.VMEM((FF_TILE - FF_SHARD, D_MODEL), BF16), pltpu.SemaphoreType.DMA((FF_PAD // FF_TILE,))]
    return _comm_sems(len(names)) + (pad if "w_down" in names else [])


def _gather_ops(names, ins, outs, scratch):
    local, direct, relays, recv = _gather_copies(names, ins, outs, *scratch[:3])
    pads = _down_pads(outs[names.index("w_down")], scratch[3], scratch[4]) if "w_down" in names else []

    def start():
        if pads:
            scratch[3][...] = jnp.zeros_like(scratch[3])
        for cp in local + direct + pads:
            cp.start()

    def relay(only=None):
        for (wi, k), cp in relays.items():
            if only is None or names[wi] in only:
                recv[wi, k].wait_recv()
                cp.start()

    def finish():
        for (wi, k), cp in recv.items():
            if k not in RELAYED:
                cp.wait_recv()
        for cp in direct + list(relays.values()):
            cp.wait_send()
        for cp in local + pads:
            cp.wait()

    return start, relay, finish


def _start(copies):
    local, sent, _ = copies
    for cp in local + sent:
        cp.start()


def _finish(copies):
    local, sent, recv = copies
    for cp in recv:
        cp.wait_recv()
    for cp in sent:
        cp.wait_send()
    for cp in local:
        cp.wait()


def _gather_weights(shards):
    names = tuple(shards)
    nw = len(names)

    def body(*refs):
        start, relay, finish = _gather_ops(names, refs[:nw], refs[nw:2 * nw], refs[2 * nw:])
        start()
        relay()
        finish()

    outs = _pcall(
        body, name="gather_weights",
        in_specs=[HBM_SPEC] * nw, out_specs=[HBM_SPEC] * nw,
        out_shape=[SDS(FULL[n][0], BF16) for n in names],
        scratch_shapes=_comm_sems(nw))(*[shards[n] for n in names])
    return dict(zip(names, outs))


def _broadcast_copies(ins, outs, send_sems, recv_sems, local_sems):
    x, y, c, me = _mesh_place()
    local, sent, recv = [], [], []
    for bi in range(len(ins)):
        local.append(pltpu.make_async_copy(ins[bi], outs[bi].at[me], local_sems.at[bi]))
        for k in range(1, N_DEV):
            peer, pidx = _peer(x, y, c, k)
            sems = dict(send_sem=send_sems.at[bi, k - 1], recv_sem=recv_sems.at[bi, k - 1], device_id=peer,
                        device_id_type=MESH)
            sent.append(pltpu.make_async_remote_copy(src_ref=ins[bi], dst_ref=outs[bi].at[me], **sems))
            recv.append(pltpu.make_async_remote_copy(src_ref=ins[bi], dst_ref=outs[bi].at[pidx], **sems))
    return local, sent, recv


def _exchange_small(blocks):
    nb = len(blocks)

    def body(*refs):
        copies = _broadcast_copies(refs[:nb], refs[nb:2 * nb], *refs[2 * nb:])
        _start(copies)
        _finish(copies)

    return _pcall(
        body, name="exchange_small", in_specs=[HBM_SPEC] * nb, out_specs=[HBM_SPEC] * nb,
        out_shape=[SDS((N_DEV,) + a.shape, F32) for a in blocks], scratch_shapes=_comm_sems(nb))(*blocks)


def _adam_math(g, w, m, v):
    m_new = ADAM_B1 * m + (1.0 - ADAM_B1) * g
    v_new = ADAM_B2 * v + (1.0 - ADAM_B2) * (g * g)
    m_hat = m_new / (1.0 - ADAM_B1 ** ADAM_STEP)
    v_hat = v_new / (1.0 - ADAM_B2 ** ADAM_STEP)
    return -ADAM_LR * (m_hat / (jnp.sqrt(v_hat) + ADAM_EPS) + ADAM_WD * w), m_new, v_new


def _adamw_small(parts, w, m, v, sq_parts, *, name):
    n = len(parts)

    def body(*refs):
        p_refs, w_refs, m_refs, v_refs = refs[:n], refs[n:2 * n], refs[2 * n:3 * n], refs[3 * n:4 * n]
        sq_ref, outs = refs[4 * n], refs[4 * n + 1:]
        for i in range(n):
            g = p_refs[i][0]
            for k in range(1, N_DEV):
                g = g + p_refs[i][k]
            delta, m_new, v_new = _adam_math(g, w_refs[i][...], m_refs[i][...], v_refs[i][...])
            outs[4 * i][...] = g
            outs[4 * i + 1][...] = delta
            outs[4 * i + 2][...] = m_new
            outs[4 * i + 3][...] = v_new
        tot = sq_ref[0]
        for k in range(1, N_DEV):
            tot = tot + sq_ref[k]
        outs[4 * n][...] = (0.5 / D_MODEL) * jnp.sum(tot, axis=1, keepdims=True)

    out_shape = [SDS(a.shape, F32) for a in w for _ in range(4)] + [SDS((1, 1), F32)]
    outs = _pcall(body, name=name, out_shape=out_shape, compiler_params=_cp())(*parts, *w, *m, *v, sq_parts)
    return [outs[4 * i:4 * i + 4] for i in range(n)], outs[4 * n][0, 0]


def _adamw(parts, w, m, v, *, name):
    r, c = w.shape
    pc = parts.shape[2]
    tr = r
    for cand in (256, 128, 88):
        if r % cand == 0:
            tr = cand
            break

    def body(p_ref, w_ref, m_ref, v_ref, g_ref, d_ref, mo_ref, vo_ref):
        g = p_ref[0, :, :c].astype(F32)
        for k in range(1, N_DEV):
            g = g + p_ref[k, :, :c].astype(F32)
        g_ref[...] = g
        d_ref[...], mo_ref[...], vo_ref[...] = _adam_math(g, w_ref[...], m_ref[...], v_ref[...])

    row = pl.BlockSpec((tr, c), lambda i: (i, 0))
    return _pcall(
        body, name=name, grid=(r // tr,),
        in_specs=[pl.BlockSpec((N_DEV, tr, pc), lambda i: (0, i, 0)), row, row, row],
        out_specs=[row] * 4, out_shape=[SDS((r, c), F32)] * 4, compiler_params=_cp())(parts, w, m, v)


def _pad_cols(a, width):
    return jnp.pad(a, ((0, 0), (0, width - a.shape[1])))


def kernel(x, mem, positions, mix_norm_g, w_in, q_norm_g, k_norm_g, pool_w, pool_scale, w_out, cross_norm_g, mem_norm_g, w_cq, w_ckv, cq_norm_g, ck_norm_g, w_co, ffn_norm_g, w_gate_up, w_down, loss_target, m_mix_norm_g, m_w_in, m_q_norm_g, m_k_norm_g, m_pool_w, m_pool_scale, m_w_out, m_cross_norm_g, m_mem_norm_g, m_w_cq, m_w_ckv, m_cq_norm_g, m_ck_norm_g, m_w_co, m_ffn_norm_g, m_w_gate_up, m_w_down, v_mix_norm_g, v_w_in, v_q_norm_g, v_k_norm_g, v_pool_w, v_pool_scale, v_w_out, v_cross_norm_g, v_mem_norm_g, v_w_cq, v_w_ckv, v_cq_norm_g, v_ck_norm_g, v_w_co, v_ffn_norm_g, v_w_gate_up, v_w_down):
    given = dict(locals())
    w_f32 = {n: given[n][0] for n in SHARDED + SMALL}
    m_f32 = {n: given["m_" + n][0] for n in SHARDED + SMALL}
    v_f32 = {n: given["v_" + n][0] for n in SHARDED + SMALL}
    shards = {n: w_f32[n].astype(BF16) for n in SHARDED}
    shards["w_gate_up"] = _pad_cols(shards["w_gate_up"], FF_TILE)
    w_in_full = _gather_weights({"w_in": shards.pop("w_in")})["w_in"]
    sm_rows = {n: (w_f32[n] if w_f32[n].ndim == 3 else w_f32[n].reshape(1, -1)) for n in SMALL}
    dx, parts, small = _local_step(x[0], mem[0], positions[0], loss_target[0], w_in_full, shards, sm_rows)

    flat = lambda a: a.reshape(-1, a.shape[-1])
    small_res, loss = _adamw_small([small[n] for n in SMALL], [flat(sm_rows[n]) for n in SMALL],
                                   [flat(m_f32[n].reshape(sm_rows[n].shape)) for n in SMALL],
                                   [flat(v_f32[n].reshape(sm_rows[n].shape)) for n in SMALL], small["sq"],
                                   name="adamw_small")
    res = {n: [a.reshape(w_f32[n].shape) for a in small_res[i]] for i, n in enumerate(SMALL)}
    for n in SHARDED:
        res[n] = _adamw(parts[n], w_f32[n], m_f32[n], v_f32[n], name="adamw_" + n)
    order = ("mix_norm_g", "w_in", "q_norm_g", "k_norm_g", "pool_w", "pool_scale", "w_out", "cross_norm_g",
             "mem_norm_g", "w_cq", "w_ckv", "cq_norm_g", "ck_norm_g", "w_co", "ffn_norm_g", "w_gate_up", "w_down")
    outs = [loss, dx[None]]
    for which in range(4):
        outs += [res[n][which][None] for n in order]
    return tuple(outs)
```

```python
import functools

import jax
import jax.numpy as jnp
from jax import lax
from jax.experimental import pallas as pl
from jax.experimental.pallas import tpu as pltpu

F32 = jnp.float32
BF16 = jnp.bfloat16
SDS = jax.ShapeDtypeStruct

D_MODEL = 1024
HEAD_DIM = 128
N_GROUPS = 3
DILATIONS = (1, 4, 16)
ATT_HEADS = 4
Q_W = 1536
KV_W = 512
POOL_W = 512
POOL_WINDOWS = (2, 4, 8, 16)
IN_W = 3072
X_W = 512
N_MEM = 256
D_FF = 2816
FF_TILE = 768
FF_SHARD = 704
FF_PAD = 4 * FF_TILE
DOWN_SHARD = 352
ROT_DIM = 32
ROT_HALF = 16
ROPE_THETA = 500000.0
EPS = 1e-6
NEG_INF = -1e30
SCALE = HEAD_DIM ** -0.5
BLOCK = 128
HALO = 16

ADAM_LR = 0.001
ADAM_B1 = 0.9
ADAM_B2 = 0.999
ADAM_EPS = 1e-08
ADAM_WD = 0.01
ADAM_STEP = 10

N_DEV = 8
VMEM_LIMIT_BYTES = 56 * 1024 * 1024
MESH = pl.DeviceIdType.MESH


def _pcall(body, **kw):
    return pl.pallas_call(body, **kw)


def _cp():
    return pltpu.CompilerParams(vmem_limit_bytes=VMEM_LIMIT_BYTES)


def _dot(a, b):
    return lax.dot_general(a, b, (((1,), (0,)), ((), ())), preferred_element_type=F32)


def _dot_nt(a, b):
    return lax.dot_general(a, b, (((1,), (1,)), ((), ())), preferred_element_type=F32)


def _dot_tn(a, b):
    return lax.dot_general(a, b, (((0,), (0,)), ((), ())), preferred_element_type=F32)


def _rows(s):
    return min(512, s)


def _rms_r(x):
    return lax.rsqrt(jnp.mean(x * x, axis=-1, keepdims=True) + EPS)


def _norm_bwd(x, r, gain, dxn):
    z = dxn * gain
    dx = r * z - x * (r * r * r * jnp.mean(z * x, axis=-1, keepdims=True))
    dgain = jnp.sum(dxn * x * r, axis=0, keepdims=True)
    return dx, dgain


def _split_bf16(t):
    hi = t.astype(BF16)
    return hi, (t - hi.astype(F32)).astype(BF16)


def _lane_sums(t, ones):
    hi, lo = _split_bf16(t)
    return _dot(hi, ones) + _dot(lo, ones)


def _head_r(y, ones):
    return lax.rsqrt(_lane_sums(y * y, ones) * (1.0 / HEAD_DIM) + EPS)


def _head_norm_bwd(y, r, gain, dn, ones):
    z = dn * gain
    dy = r * z - y * (r * r * r * (_dot((z * y).astype(BF16), ones) * (1.0 / HEAD_DIM)))
    return dy, jnp.sum(dn * y * r, axis=0, keepdims=True)


def _swap_matrix():
    src = lax.broadcasted_iota(jnp.int32, (HEAD_DIM, HEAD_DIM), 0)
    dst = lax.broadcasted_iota(jnp.int32, (HEAD_DIM, HEAD_DIM), 1)
    hit = ((dst < ROT_HALF) & (src == dst + ROT_HALF)) | ((dst >= ROT_HALF) & (dst < ROT_DIM) & (src == dst - ROT_HALF))
    return jnp.where(hit, 1.0, 0.0).astype(BF16)


def _partner(t, swap):
    hi, lo = _split_bf16(t)
    return _dot(hi, swap) + _dot(lo, swap)


def _rope(n, cos_t, sin_t, swap):
    return n * cos_t + _partner(n, swap) * sin_t


def _rope_bwd(d, cos_t, sin_t, swap):
    return d * cos_t + _dot((d * sin_t).astype(BF16), swap)


def _resident(shape):
    return pl.BlockSpec(shape, lambda i: (0,) * len(shape), pipeline_mode=pl.Buffered(1))


def _chunks(n, cn):
    return [slice(j * cn, (j + 1) * cn) for j in range(n // cn)]


def _norm_linear(x, gain, w, *, cn, out_dtype, name, shards=None):
    s, k = x.shape
    n = w.shape[1]
    tm = _rows(s)
    names = tuple(shards or ())
    nr = len(names)

    def body(*refs):
        x_ref, g_ref, w_ref = refs[:3]
        y_ref, xn_ref = refs[3 + nr:5 + nr]
        if nr:
            start, relay, finish = _gather_ops(names, refs[3:3 + nr], refs[5 + nr:5 + 2 * nr], refs[5 + 2 * nr:])
            pl.when(pl.program_id(0) == 0)(start)
            pl.when(pl.program_id(0) == _relay_step(s // tm))(relay)
        xv = x_ref[...]
        xn_ref[...] = (xv * _rms_r(xv) * g_ref[...]).astype(BF16)
        for c in _chunks(n, cn):
            y_ref[:, c] = _dot(xn_ref[...], w_ref[:, c]).astype(out_dtype)
        if nr:
            pl.when(pl.program_id(0) == s // tm - 1)(finish)

    row = lambda w_: pl.BlockSpec((tm, w_), lambda i: (i, 0))
    outs = _pcall(
        body, name=name, grid=(s // tm,),
        in_specs=[row(k), _resident((1, k)), _resident((k, n))] + [HBM_SPEC] * nr,
        out_specs=[row(n), row(k)] + [HBM_SPEC] * nr,
        out_shape=[SDS((s, n), out_dtype), SDS((s, k), BF16)] + [SDS(FULL[a][0], BF16) for a in names],
        scratch_shapes=_gather_scratch(names),
        compiler_params=_cp())(x, gain, w, *[shards[a] for a in names])
    return (outs[0], outs[1], dict(zip(names, outs[2:]))) if nr else tuple(outs)


def _norm_linear_swiglu(x, gain, wgu, *, name):
    s, k = x.shape
    tm = _rows(s)

    def body(x_ref, g_ref, w_ref, gu_ref, a_ref, xn_ref):
        xv = x_ref[...]
        xn_ref[...] = (xv * _rms_r(xv) * g_ref[...]).astype(BF16)
        for c in _chunks(FF_PAD, FF_TILE):
            g = _dot(xn_ref[...], w_ref[:, c])
            u = _dot(xn_ref[...], w_ref[:, slice(FF_PAD + c.start, FF_PAD + c.stop)])
            a_ref[:, c] = (g * jax.nn.sigmoid(g) * u).astype(BF16)
            gu_ref[0, :, c] = g.astype(BF16)
            gu_ref[1, :, c] = u.astype(BF16)

    row = lambda w_: pl.BlockSpec((tm, w_), lambda i: (i, 0))
    return _pcall(
        body, name=name, grid=(s // tm,),
        in_specs=[row(k), _resident((1, k)), _resident((k, 2 * FF_PAD))],
        out_specs=[pl.BlockSpec((2, tm, FF_PAD), lambda i: (0, i, 0)), row(FF_PAD), row(k)],
        out_shape=[SDS((2, s, FF_PAD), BF16), SDS((s, FF_PAD), BF16), SDS((s, k), BF16)],
        compiler_params=_cp())(x, gain, wgu)


def _linear_res(a, w, res, *, cn, name):
    s, k = a.shape
    n = w.shape[1]
    tm = min(2 * _rows(s), s)

    def body(a_ref, w_ref, r_ref, y_ref):
        for c in _chunks(n, cn):
            y_ref[:, c] = r_ref[:, c] + _dot(a_ref[...], w_ref[:, c])

    row = lambda w_: pl.BlockSpec((tm, w_), lambda i: (i, 0))
    return _pcall(
        body, name=name, grid=(s // tm,),
        in_specs=[row(k), _resident((k, n)), row(n)], out_specs=row(n),
        out_shape=SDS((s, n), F32), compiler_params=_cp())(a, w, res)


def _linear_res_loss(a, w, res, tgt, *, name):
    s, k = a.shape
    n = w.shape[1]
    tm = min(2 * _rows(s), s)

    def body(a_ref, w_ref, r_ref, t_ref, dy_ref, dyb_ref, sq_ref):
        e = r_ref[...] + _dot(a_ref[...], w_ref[...]) - t_ref[...]
        dy = e * (1.0 / n)
        dy_ref[...] = dy
        dyb_ref[...] = dy.astype(BF16)

        @pl.when(pl.program_id(0) == 0)
        def _():
            sq_ref[...] = jnp.zeros_like(sq_ref)
        sq_ref[...] += jnp.sum(e * e, axis=0, keepdims=True)

    row = lambda w_: pl.BlockSpec((tm, w_), lambda i: (i, 0))
    return _pcall(
        body, name=name, grid=(s // tm,),
        in_specs=[row(k), _resident((k, n)), row(n), row(n)],
        out_specs=[row(n), row(n), pl.BlockSpec((1, n), lambda i: (0, 0))],
        out_shape=[SDS((s, n), F32), SDS((s, n), BF16), SDS((1, n), F32)],
        compiler_params=_cp())(a, w, res, tgt)


def _linear_nt(g, w, *, cn, name):
    s, k = g.shape
    n = w.shape[0]
    tm = _rows(s)

    def body(g_ref, w_ref, y_ref):
        for c in _chunks(n, cn):
            y_ref[:, c] = _dot_nt(g_ref[...], w_ref[c, :]).astype(BF16)

    row = lambda w_: pl.BlockSpec((tm, w_), lambda i: (i, 0))
    return _pcall(
        body, name=name, grid=(s // tm,),
        in_specs=[row(k), _resident((n, k))], out_specs=row(n),
        out_shape=SDS((s, n), BF16), compiler_params=_cp())(g, w)


def _swiglu_bwd(dyb, wd, gu, *, name):
    s, n = dyb.shape
    tm = _rows(s)

    def body(dy_ref, wd_ref, gu_ref, dgu_ref):
        for c in _chunks(FF_PAD, FF_TILE):
            da = _dot_nt(dy_ref[...], wd_ref[c, :])
            g = gu_ref[0, :, c].astype(F32)
            u = gu_ref[1, :, c].astype(F32)
            sg = jax.nn.sigmoid(g)
            dgu_ref[0, :, c] = (da * u * (sg * (1.0 + g * (1.0 - sg)))).astype(BF16)
            dgu_ref[1, :, c] = (da * (g * sg)).astype(BF16)

    half = pl.BlockSpec((2, tm, FF_PAD), lambda i: (0, i, 0))
    return _pcall(
        body, name=name, grid=(s // tm,),
        in_specs=[pl.BlockSpec((tm, n), lambda i: (i, 0)), _resident((FF_PAD, n)), half],
        out_specs=half, out_shape=SDS((2, s, FF_PAD), BF16), compiler_params=_cp())(dyb, wd, gu)


def _linear_nt_normbwd(g, w, x, dres, gain, *, name, grads=None):
    d, k = w.shape
    s = x.shape[0]
    tm = _rows(s)
    names = tuple(grads or ())
    nr = len(names)

    def body(*refs):
        g_ref, w_ref, x_ref, dr_ref, gn_ref = refs[:5]
        dx_ref, dxb_ref, dg_ref = refs[5 + nr:8 + nr]
        if nr:
            copies = _exchange_copies(names, refs[5:5 + nr], refs[8 + nr:8 + 2 * nr], *refs[8 + 2 * nr:])

        @pl.when(pl.program_id(0) == 0)
        def _():
            dg_ref[...] = jnp.zeros_like(dg_ref)
            if nr:
                _start(copies)

        if g.ndim == 3:
            dxn = _dot_nt(g_ref[0], w_ref[:, :k // 2]) + _dot_nt(g_ref[1], w_ref[:, k // 2:])
        else:
            dxn = _dot_nt(g_ref[...], w_ref[...])
        xv = x_ref[...]
        dx, dgain = _norm_bwd(xv, _rms_r(xv), gn_ref[...], dxn)
        out = dr_ref[...] + dx
        dx_ref[...] = out
        dxb_ref[...] = out.astype(BF16)
        dg_ref[...] += dgain

        if nr:
            @pl.when(pl.program_id(0) == s // tm - 1)
            def _():
                _finish(copies)

    row = pl.BlockSpec((tm, d), lambda i: (i, 0))
    g_spec = (pl.BlockSpec((2, tm, k // 2), lambda i: (0, i, 0)) if g.ndim == 3
              else pl.BlockSpec((tm, k), lambda i: (i, 0)))
    outs = _pcall(
        body, name=name, grid=(s // tm,),
        in_specs=[g_spec, _resident((d, k)), row, row, _resident((1, d))] + [HBM_SPEC] * nr,
        out_specs=[row, row, pl.BlockSpec((1, d), lambda i: (0, 0))] + [HBM_SPEC] * nr,
        out_shape=[SDS((s, d), F32), SDS((s, d), BF16), SDS((1, d), F32)]
        + [SDS((N_DEV,) + _shard_shape(n), BF16) for n in names],
        scratch_shapes=_comm_sems(nr) if nr else [],
        compiler_params=_cp())(g, w, x, dres, gain, *[grads[n] for n in names])
    return (outs[0], outs[1], outs[2], dict(zip(names, outs[3:]))) if nr else tuple(outs)


def _dw_tn(x, g, *, tkw, tn, name):
    s, kw = x.shape
    halves = g.ndim == 3
    n = 2 * g.shape[2] if halves else g.shape[1]
    ts = min(2048, s)
    ns = s // ts
    per_half = n // 2 // tn

    def body(x_ref, g_ref, o_ref, acc_ref):
        ss = pl.program_id(2)

        @pl.when(ss == 0)
        def _():
            acc_ref[...] = jnp.zeros_like(acc_ref)

        acc_ref[...] += _dot_tn(x_ref[...], g_ref[...])

        @pl.when(ss == ns - 1)
        def _():
            o_ref[...] = acc_ref[...].astype(BF16)

    g_spec = (pl.BlockSpec((None, ts, tn), lambda a, b, ss: (b // per_half, ss, b % per_half)) if halves
              else pl.BlockSpec((ts, tn), lambda a, b, ss: (ss, b)))
    return _pcall(
        body, name=name, grid=(kw // tkw, n // tn, ns),
        in_specs=[pl.BlockSpec((ts, tkw), lambda a, b, ss: (ss, a)), g_spec],
        out_specs=pl.BlockSpec((tkw, tn), lambda a, b, ss: (a, b)),
        out_shape=SDS((kw, n), BF16),
        scratch_shapes=[pltpu.VMEM((tkw, tn), F32)], compiler_params=_cp())(x, g)


def _qk_prep(proj, cos_t, sin_t, qg, kg, *, name):
    s = proj.shape[0]
    tm = _rows(s)
    nqh = Q_W // HEAD_DIM

    def body(q_ref, k_ref, v_ref, c_ref, s_ref, qg_ref, kg_ref, qo_ref, ko_ref, vo_ref):
        c, sn = c_ref[...], s_ref[...]
        ones, swap = jnp.ones((HEAD_DIM, HEAD_DIM), BF16), _swap_matrix()
        for h in range(nqh):
            sl = slice(h * HEAD_DIM, (h + 1) * HEAD_DIM)
            y = q_ref[:, sl]
            qo_ref[:, sl] = (_rope(y * _head_r(y, ones) * qg_ref[...], c, sn, swap) * SCALE).astype(BF16)
        for h in range(ATT_HEADS):
            sl = slice(h * HEAD_DIM, (h + 1) * HEAD_DIM)
            y = k_ref[:, sl]
            ko_ref[:, sl] = _rope(y * _head_r(y, ones) * kg_ref[...], c, sn, swap).astype(BF16)
        vo_ref[...] = v_ref[...].astype(BF16)

    row = lambda w, j: pl.BlockSpec((tm, w), lambda i: (i, j))
    one = pl.BlockSpec((1, HEAD_DIM), lambda i: (0, 0))
    return _pcall(
        body, name=name, grid=(s // tm,),
        in_specs=[row(Q_W, 0), row(KV_W, 3), row(KV_W, 4), row(HEAD_DIM, 0), row(HEAD_DIM, 0), one, one],
        out_specs=[row(Q_W, 0), row(KV_W, 0), row(KV_W, 0)],
        out_shape=[SDS((s, Q_W), BF16), SDS((s, KV_W), BF16), SDS((s, KV_W), BF16)],
        compiler_params=_cp())(proj, proj, proj, cos_t, sin_t, qg, kg)


def _qk_prep_bwd(proj, dq, dk, dv, du, cos_t, sin_t, qg, kg, *, name):
    s = proj.shape[0]
    tm = _rows(s)

    def body(q_ref, k_ref, dq_ref, dk_ref, dv_ref, du_ref, c_ref, s_ref, qg_ref, kg_ref, dp_ref, dqg_ref, dkg_ref):
        c, sn = c_ref[...], s_ref[...]

        @pl.when(pl.program_id(0) == 0)
        def _():
            dqg_ref[...] = jnp.zeros_like(dqg_ref)
            dkg_ref[...] = jnp.zeros_like(dkg_ref)

        ones, swap = jnp.ones((HEAD_DIM, HEAD_DIM), BF16), _swap_matrix()
        dqg = jnp.zeros((1, HEAD_DIM), F32)
        for g in range(N_GROUPS):
            for h in range(ATT_HEADS):
                sl = slice(h * HEAD_DIM, (h + 1) * HEAD_DIM)
                col = slice(g * KV_W + h * HEAD_DIM, g * KV_W + (h + 1) * HEAD_DIM)
                y = q_ref[:, col]
                dn = _rope_bwd(dq_ref[g, :, sl].astype(F32) * SCALE, c, sn, swap)
                dy, dgain = _head_norm_bwd(y, _head_r(y, ones), qg_ref[...], dn, ones)
                dp_ref[:, col] = dy.astype(BF16)
                dqg = dqg + dgain
        dqg_ref[...] += dqg

        dkg = jnp.zeros((1, HEAD_DIM), F32)
        for h in range(ATT_HEADS):
            sl = slice(h * HEAD_DIM, (h + 1) * HEAD_DIM)
            y = k_ref[:, sl]
            dn = _rope_bwd(dk_ref[:, sl], c, sn, swap)
            dy, dgain = _head_norm_bwd(y, _head_r(y, ones), kg_ref[...], dn, ones)
            dp_ref[:, Q_W + h * HEAD_DIM:Q_W + (h + 1) * HEAD_DIM] = dy.astype(BF16)
            dkg = dkg + dgain
        dkg_ref[...] += dkg

        dp_ref[:, Q_W + KV_W:Q_W + 2 * KV_W] = dv_ref[...].astype(BF16)
        dp_ref[:, Q_W + 2 * KV_W:] = du_ref[...].astype(BF16)

    row = lambda w, j: pl.BlockSpec((tm, w), lambda i: (i, j))
    one = pl.BlockSpec((1, HEAD_DIM), lambda i: (0, 0))
    return _pcall(
        body, name=name, grid=(s // tm,),
        in_specs=[row(Q_W, 0), row(KV_W, 3), pl.BlockSpec((N_GROUPS, tm, KV_W), lambda i: (0, i, 0))]
        + [row(KV_W, 0)] * 3 + [row(HEAD_DIM, 0), row(HEAD_DIM, 0), one, one],
        out_specs=[row(IN_W, 0), one, one],
        out_shape=[SDS((s, IN_W), BF16), SDS((1, HEAD_DIM), F32), SDS((1, HEAD_DIM), F32)],
        compiler_params=_cp())(proj, proj, dq, dk, dv, du, cos_t, sin_t, qg, kg)


ATT_ROWS = 16 * BLOCK


def _sub(ref, start, d, size=BLOCK):
    return ref[pl.ds(start, size, stride=d), :] if d > 1 else ref[pl.ds(start, size), :]


def _sub_set(ref, start, d, val):
    if d > 1:
        ref[pl.ds(start, BLOCK, stride=d), :] = val
    else:
        ref[pl.ds(start, BLOCK), :] = val


def _band_masks():
    row = lax.broadcasted_iota(jnp.int32, (BLOCK, BLOCK), 0)
    col = lax.broadcasted_iota(jnp.int32, (BLOCK, BLOCK), 1)
    return col <= row, col >= row


def _eye(n=BLOCK):
    row = lax.broadcasted_iota(jnp.int32, (n, n), 0)
    col = lax.broadcasted_iota(jnp.int32, (n, n), 1)
    return jnp.where(row == col, 1.0, 0.0).astype(BF16)


def _attn_fwd(q_rot, k_rot, v, shards, *, name):
    s = q_rot.shape[0]
    rr = ATT_ROWS
    nblk = s // rr
    names = tuple(shards)
    nr = len(names)

    def body(*refs):
        q0, q1, q2, kp, kc, vp, vc = refs[:7]
        mix_ref, lse_ref = refs[7 + nr:9 + nr]
        qs, ks, vs, os_, ls = refs[9 + 2 * nr:14 + 2 * nr]
        h, n = pl.program_id(0), pl.program_id(1)
        start, relay, finish = _gather_ops(names, refs[7:7 + nr], refs[9 + nr:9 + 2 * nr], refs[14 + 2 * nr:])
        pl.when((h == 0) & (n == 0))(start)
        steps = ATT_HEADS * nblk
        early, late = max(1, (steps * 6) // 16), max(1, (steps * 14) // 16)
        pl.when(h * nblk + n == early)(functools.partial(relay, names[:1]))
        pl.when(h * nblk + n == late)(functools.partial(relay, names[1:]))
        for g, q_ref in enumerate((q0, q1, q2)):
            qs[g] = q_ref[...].astype(F32)
        ks[:rr] = kp[...].astype(F32)
        ks[rr:] = kc[...].astype(F32)
        vs[:rr] = vp[...].astype(F32)
        vs[rr:] = vc[...].astype(F32)
        m_cur, m_band = _band_masks()
        mask_in = jnp.concatenate([m_band, m_cur], axis=1)
        mask_first = jnp.concatenate([m_band & (n > 0), m_cur], axis=1)
        ones = jnp.ones((2 * BLOCK, HEAD_DIM), BF16)
        pieces = [(g, d, j * BLOCK * d + r, j) for g, d in enumerate(DILATIONS) for r in range(d)
                  for j in range(rr // (BLOCK * d))]

        def scores(piece):
            g, d, base, j = piece
            q = _sub(qs.at[g], base, d).astype(BF16)
            k2 = _sub(ks, rr + base - BLOCK * d, d, 2 * BLOCK).astype(BF16)
            return jnp.where(mask_first if j == 0 else mask_in, _dot_nt(q, k2), NEG_INF)

        sc = scores(pieces[0])
        for i, (g, d, base, j) in enumerate(pieces):
            cur = sc
            if i + 1 < len(pieces):
                sc = scores(pieces[i + 1])
            m = jnp.max(cur, axis=-1, keepdims=True)
            p = jnp.exp(cur - m).astype(BF16)
            v2 = _sub(vs, rr + base - BLOCK * d, d, 2 * BLOCK).astype(BF16)
            acc_l = _dot(p, jnp.concatenate([v2, ones], axis=1))
            l = acc_l[:, HEAD_DIM:]
            _sub_set(os_.at[g], base, d, acc_l[:, :HEAD_DIM] / l)
            _sub_set(ls.at[g], base, d, m + jnp.log(l))
        for c in _chunks(rr, 2 * BLOCK):
            a, b, cc = ls[0, c, :], ls[1, c, :], ls[2, c, :]
            m = jnp.maximum(jnp.maximum(a, b), cc)
            wa, wb, wc = jnp.exp(a - m), jnp.exp(b - m), jnp.exp(cc - m)
            den = wa + wb + wc
            mix_ref[c, :] = ((wa * os_[0, c, :] + wb * os_[1, c, :] + wc * os_[2, c, :]) / den).astype(BF16)
            lse_ref[c, :] = m + jnp.log(den)

        pl.when((h == ATT_HEADS - 1) & (n == nblk - 1))(finish)

    blk = lambda f: pl.BlockSpec((rr, HEAD_DIM), f)
    prv = lambda n: jnp.maximum(n - 1, 0)
    outs = _pcall(
        body, name=name, grid=(ATT_HEADS, nblk),
        in_specs=[blk(lambda h, n, g=g: (n, g * ATT_HEADS + h)) for g in range(N_GROUPS)]
        + [blk(lambda h, n: (prv(n), h)), blk(lambda h, n: (n, h))] * 2 + [HBM_SPEC] * nr,
        out_specs=[blk(lambda h, n: (n, h)), blk(lambda h, n: (n, h))] + [HBM_SPEC] * nr,
        out_shape=[SDS((s, KV_W + POOL_W), BF16), SDS((s, KV_W), F32)] + [SDS(FULL[w][0], BF16) for w in names],
        scratch_shapes=[pltpu.VMEM((N_GROUPS, rr, HEAD_DIM), F32), pltpu.VMEM((2 * rr, HEAD_DIM), F32),
                        pltpu.VMEM((2 * rr, HEAD_DIM), F32), pltpu.VMEM((N_GROUPS, rr, HEAD_DIM), F32),
                        pltpu.VMEM((N_GROUPS, rr, HEAD_DIM), F32)] + _gather_scratch(names),
        compiler_params=_cp())(q_rot, q_rot, q_rot, k_rot, k_rot, v, v, *[shards[w] for w in names])
    return outs[0], outs[1], dict(zip(names, outs[2:]))


def _attn_bwd(q_rot, k_rot, v, mix, dmix, lse, grads, blocks, *, name):
    s = q_rot.shape[0]
    rr = ATT_ROWS
    nblk = s // rr
    names = tuple(grads)
    nr, nb = len(names), len(blocks)
    nx = nr + nb

    def body(*refs):
        q0, q1, q2, qx0, qx1, qx2, kp, kc, vp, vc, do_c, do_x, o_c, o_x, l_c, l_x = refs[:16]
        dq_ref, dk_ref, dv_ref = refs[16 + nx:19 + nx]
        qs, ks, vs, dos, lss, dls, dqs, dks, dvs = refs[19 + 2 * nx:28 + 2 * nx]
        h, n = pl.program_id(0), pl.program_id(1)
        copies = _exchange_copies(names, refs[16:16 + nr], refs[19 + nx:19 + nx + nr], *refs[28 + 2 * nx:31 + 2 * nx])
        spread = _broadcast_copies(refs[16 + nr:16 + nx], refs[19 + nx + nr:19 + 2 * nx], *refs[31 + 2 * nx:])

        @pl.when((h == 0) & (n == 0))
        def _():
            _start(copies)
            _start(spread)

        for g, (qc_ref, qx_ref) in enumerate(((q0, qx0), (q1, qx1), (q2, qx2))):
            qs[g, :rr] = qc_ref[...].astype(F32)
            qs[g, rr:] = qx_ref[...].astype(F32)
        ks[:rr] = kp[...].astype(F32)
        ks[rr:] = kc[...].astype(F32)
        vs[:rr] = vp[...].astype(F32)
        vs[rr:] = vc[...].astype(F32)
        lss[:rr] = l_c[...]
        lss[rr:] = l_x[...]
        for half, (d_ref, o_ref) in enumerate(((do_c, o_c), (do_x, o_x))):
            for c in _chunks(rr, 2 * BLOCK):
                cs = slice(half * rr + c.start, half * rr + c.stop)
                dof = d_ref[c, :].astype(F32)
                dos[cs, :] = dof
                dls[cs, :] = jnp.broadcast_to(jnp.sum(dof * o_ref[c, :].astype(F32), axis=-1, keepdims=True),
                                              (2 * BLOCK, HEAD_DIM))
        m_cur, m_band = _band_masks()
        mask_in = jnp.concatenate([m_cur, m_band], axis=0)
        mask_last = jnp.concatenate([m_cur, m_band & (n + 1 < nblk)], axis=0)
        m_first = m_band & (n > 0)
        eye = _eye()
        pieces = [(g, d, j * BLOCK * d + r, j, rr // (BLOCK * d)) for g, d in enumerate(DILATIONS) for r in range(d)
                  for j in range(rr // (BLOCK * d))]

        def front(piece):
            g, d, base, _, _ = piece
            q2 = _sub(qs.at[g], base, d, 2 * BLOCK).astype(BF16)
            do2 = _sub(dos, base, d, 2 * BLOCK).astype(BF16)
            k = _sub(ks, rr + base, d).astype(BF16)
            vv = _sub(vs, rr + base, d).astype(BF16)
            return q2, do2, k, _dot_nt(q2, k), _dot_nt(do2, vv)

        def middle(piece, fr, dq_acc):
            g, d, base, j, nsub = piece
            q2, do2, k, s2, dp2 = fr
            if j == 0:
                kp_ = _sub(ks, rr + base - BLOCK * d, d).astype(BF16)
                p0 = jnp.where(m_first, jnp.exp(_dot_nt(q2[:BLOCK], kp_) - _sub(lss, base, d)), 0.0)
                ds0 = p0 * (_dot_nt(do2[:BLOCK], _sub(vs, rr + base - BLOCK * d, d).astype(BF16)) - _sub(dls, base, d))
                dq_acc = _dot(ds0.astype(BF16), kp_)
            p2 = jnp.where(mask_last if j + 1 == nsub else mask_in,
                           jnp.exp(s2 - _sub(lss, base, d, 2 * BLOCK)), 0.0)
            ds2 = (p2 * (dp2 - _sub(dls, base, d, 2 * BLOCK))).astype(BF16)
            dq2 = _dot(ds2, k)
            return dq2[BLOCK:], (_dot_nt(eye, ds2), _dot_nt(eye, p2.astype(BF16)), q2, do2, dq_acc + dq2[:BLOCK])

        def back(piece, tr):
            g, d, base, _, _ = piece
            ds_t, p_t, q2, do2, dq = tr
            _sub_set(dqs.at[g], base, d, dq)
            dk, dv = _dot(ds_t.astype(BF16), q2), _dot(p_t.astype(BF16), do2)
            if g == 0:
                _sub_set(dks, base, d, dk)
                _sub_set(dvs, base, d, dv)
            else:
                _sub_set(dks, base, d, _sub(dks, base, d) + dk)
                _sub_set(dvs, base, d, _sub(dvs, base, d) + dv)

        fr, held, dq_acc = front(pieces[0]), None, None
        for i, piece in enumerate(pieces):
            cur = fr
            if i + 1 < len(pieces):
                fr = front(pieces[i + 1])
            dq_acc, now = middle(piece, cur, dq_acc)
            if held is not None:
                back(pieces[i - 1], held)
            held = now
        back(pieces[-1], held)
        for g in range(N_GROUPS):
            dq_ref[g] = dqs[g].astype(BF16)
        dk_ref[...] = dks[...]
        dv_ref[...] = dvs[...]

        @pl.when((h == ATT_HEADS - 1) & (n == nblk - 1))
        def _():
            _finish(copies)
            _finish(spread)

    blk = lambda f: pl.BlockSpec((rr, HEAD_DIM), f)
    prv = lambda n: jnp.maximum(n - 1, 0)
    nxt = lambda n: jnp.minimum(n + 1, nblk - 1)
    cur_kv = blk(lambda h, n: (n, h))
    outs = _pcall(
        body, name=name, grid=(ATT_HEADS, nblk),
        in_specs=[blk(lambda h, n, g=g: (n, g * ATT_HEADS + h)) for g in range(N_GROUPS)]
        + [blk(lambda h, n, g=g: (nxt(n), g * ATT_HEADS + h)) for g in range(N_GROUPS)]
        + [blk(lambda h, n: (prv(n), h)), cur_kv] * 2
        + [cur_kv, blk(lambda h, n: (nxt(n), h))] * 3 + [HBM_SPEC] * nx,
        out_specs=[pl.BlockSpec((N_GROUPS, rr, HEAD_DIM), lambda h, n: (0, n, h)), cur_kv, cur_kv] + [HBM_SPEC] * nx,
        out_shape=[SDS((N_GROUPS, s, KV_W), BF16), SDS((s, KV_W), F32), SDS((s, KV_W), F32)]
        + [SDS((N_DEV,) + _shard_shape(w), BF16) for w in names] + [SDS((N_DEV,) + a.shape, F32) for a in blocks],
        scratch_shapes=[pltpu.VMEM((N_GROUPS, 2 * rr, HEAD_DIM), F32)] + [pltpu.VMEM((2 * rr, HEAD_DIM), F32)] * 5
        + [pltpu.VMEM((N_GROUPS, rr, HEAD_DIM), F32), pltpu.VMEM((rr, HEAD_DIM), F32), pltpu.VMEM((rr, HEAD_DIM), F32)]
        + _comm_sems(nr) + _comm_sems(nb),
        compiler_params=_cp())(q_rot, q_rot, q_rot, q_rot, q_rot, q_rot, k_rot, k_rot, v, v, dmix, dmix, mix, mix,
                               lse, lse, *[grads[w] for w in names], *blocks)
    return outs[0], outs[1], outs[2], dict(zip(names, outs[3:3 + nr])), list(outs[3 + nr:])


def _pool_d(u_ref, halo_ref, i, tm):
    halo = jnp.where(i > 0, halo_ref[...], 0.0)
    t = i * tm + lax.broadcasted_iota(jnp.int32, (tm, 1), 0)
    out = []
    for g, w in enumerate(POOL_WINDOWS):
        sl = slice(g * HEAD_DIM, (g + 1) * HEAD_DIM)
        u = u_ref[:, sl]
        acc = jnp.concatenate([halo[:, sl], u], axis=0)
        sh = 1
        while sh < w:
            acc = acc + pltpu.roll(acc, sh, 0)
            sh *= 2
        cnt = jnp.minimum(t + 1, w).astype(F32)
        out.append(acc[HALO:, :] / cnt - u)
    return out


def _pool_fwd(proj, mix, pool_w, pool_scale, *, name):
    s = proj.shape[0]
    tm = _rows(s)
    ucol = (IN_W - POOL_W) // POOL_W

    def body(u_ref, halo_ref, mix_in, w_ref, sc_ref, o_ref):
        del mix_in
        dd = _pool_d(u_ref, halo_ref, pl.program_id(0), tm)
        for g in range(len(POOL_WINDOWS)):
            sl = slice(g * HEAD_DIM, (g + 1) * HEAD_DIM)
            y = _dot(dd[g].astype(BF16), w_ref[g].astype(BF16))
            o_ref[:, sl] = (y * sc_ref[:, sl]).astype(BF16)

    return _pcall(
        body, name=name, grid=(s // tm,),
        in_specs=[pl.BlockSpec((tm, POOL_W), lambda i: (i, ucol)),
                  pl.BlockSpec((HALO, POOL_W), lambda i: (jnp.maximum(i * (tm // HALO) - 1, 0), ucol)),
                  pl.BlockSpec(memory_space=pl.ANY),
                  pl.BlockSpec((len(POOL_WINDOWS), HEAD_DIM, HEAD_DIM), lambda i: (0, 0, 0)),
                  pl.BlockSpec((1, POOL_W), lambda i: (0, 0))],
        out_specs=pl.BlockSpec((tm, POOL_W), lambda i: (i, 1)),
        out_shape=SDS(mix.shape, BF16), input_output_aliases={2: 0},
        compiler_params=_cp())(proj, proj, mix, pool_w, pool_scale)


def _pool_bwd(proj, dmix, pool_w, pool_scale, *, name):
    s = proj.shape[0]
    tm = _rows(s)
    nblk = s // tm
    ucol = (IN_W - POOL_W) // POOL_W
    ng = len(POOL_WINDOWS)

    def body(u_ref, halo_ref, dp_ref, dpn_ref, w_ref, sc_ref, du_ref, dw_ref, dsc_ref):
        i = pl.program_id(0)

        @pl.when(i == 0)
        def _():
            dw_ref[...] = jnp.zeros_like(dw_ref)
            dsc_ref[...] = jnp.zeros_like(dsc_ref)

        dd = _pool_d(u_ref, halo_ref, i, tm)
        t = i * tm + lax.broadcasted_iota(jnp.int32, (tm, 1), 0)
        dpn = jnp.where(i + 1 < nblk, dpn_ref[...].astype(F32), 0.0)
        for g, w in enumerate(POOL_WINDOWS):
            sl = slice(g * HEAD_DIM, (g + 1) * HEAD_DIM)
            wg = w_ref[g].astype(BF16)
            db = dd[g].astype(BF16)
            dp = dp_ref[:, sl].astype(F32)
            dsc_ref[:, sl] += jnp.sum(dp * _dot(db, wg), axis=0, keepdims=True)
            dy = (dp * sc_ref[:, sl]).astype(BF16)
            dw_ref[g] += _dot_tn(db, dy)
            g_d = _dot_nt(dy, wg)
            g_dn = _dot_nt((dpn[:, sl] * sc_ref[:, sl]).astype(BF16), wg)
            cnt = jnp.minimum(t + 1, w).astype(F32)
            acc = jnp.concatenate([g_d / cnt, g_dn * (1.0 / w)], axis=0)
            sh = 1
            while sh < w:
                acc = acc + pltpu.roll(acc, tm + HALO - sh, 0)
                sh *= 2
            du_ref[:, sl] = acc[:tm, :] - g_d

    nh = s // HALO
    return _pcall(
        body, name=name, grid=(nblk,),
        in_specs=[pl.BlockSpec((tm, POOL_W), lambda i: (i, ucol)),
                  pl.BlockSpec((HALO, POOL_W), lambda i: (jnp.maximum(i * (tm // HALO) - 1, 0), ucol)),
                  pl.BlockSpec((tm, POOL_W), lambda i: (i, 1)),
                  pl.BlockSpec((HALO, POOL_W), lambda i: (jnp.minimum((i + 1) * (tm // HALO), nh - 1), 1)),
                  pl.BlockSpec((ng, HEAD_DIM, HEAD_DIM), lambda i: (0, 0, 0)),
                  pl.BlockSpec((1, POOL_W), lambda i: (0, 0))],
        out_specs=[pl.BlockSpec((tm, POOL_W), lambda i: (i, 0)),
                   pl.BlockSpec((ng, HEAD_DIM, HEAD_DIM), lambda i: (0, 0, 0)),
                   pl.BlockSpec((1, POOL_W), lambda i: (0, 0))],
        out_shape=[SDS((s, POOL_W), F32), SDS((ng, HEAD_DIM, HEAD_DIM), F32), SDS((1, POOL_W), F32)],
        compiler_params=_cp())(proj, proj, dmix, dmix, pool_w, pool_scale)


def _mem_fwd(mem, mem_g, wckv, ck_g, *, name):
    def body(m_ref, g_ref, w_ref, kg_ref, mn_ref, ckr_ref, ckn_ref, cv_ref):
        mv = m_ref[...]
        mn = (mv * _rms_r(mv) * g_ref[...]).astype(BF16)
        mn_ref[...] = mn
        ckv = _dot(mn, w_ref[...])
        ckr_ref[...] = ckv[:, :X_W]
        cv_ref[...] = ckv[:, X_W:].astype(BF16)
        for h in range(ATT_HEADS):
            sl = slice(h * HEAD_DIM, (h + 1) * HEAD_DIM)
            y = ckv[:, sl]
            ckn_ref[:, sl] = (y * _rms_r(y) * kg_ref[...]).astype(BF16)

    return _pcall(
        body, name=name,
        out_shape=[SDS((N_MEM, D_MODEL), BF16), SDS((N_MEM, X_W), F32), SDS((N_MEM, X_W), BF16),
                   SDS((N_MEM, X_W), BF16)],
        compiler_params=_cp())(mem, mem_g, wckv, ck_g)


def _cross_q(cq_ref, g_ref, sl):
    y = cq_ref[:, sl]
    r = _rms_r(y)
    return y, r, y * r * g_ref[...] * SCALE


def _cross_fwd(h1, gain, wcq, ck_n, cv, cq_g, wco, *, name):
    s, d = h1.shape
    tm = _rows(s)

    def body(x_ref, gn_ref, wq_ref, k_ref, v_ref, g_ref, wo_ref, h2_ref, xn_ref, cq_ref, o_ref):
        xv = x_ref[...]
        xn_ref[...] = (xv * _rms_r(xv) * gn_ref[...]).astype(BF16)
        cq_ref[...] = _dot(xn_ref[...], wq_ref[...])
        for h in range(ATT_HEADS):
            sl = slice(h * HEAD_DIM, (h + 1) * HEAD_DIM)
            _, _, qn = _cross_q(cq_ref, g_ref, sl)
            sc = _dot_nt(qn.astype(BF16), k_ref[:, sl])
            p = jnp.exp(sc - jnp.max(sc, axis=-1, keepdims=True))
            p = p / jnp.sum(p, axis=-1, keepdims=True)
            o_ref[:, sl] = _dot(p.astype(BF16), v_ref[:, sl]).astype(BF16)
        for c in _chunks(d, X_W):
            h2_ref[:, c] = x_ref[:, c] + _dot(o_ref[...], wo_ref[:, c])

    row = lambda w_: pl.BlockSpec((tm, w_), lambda i: (i, 0))
    return _pcall(
        body, name=name, grid=(s // tm,),
        in_specs=[row(d), _resident((1, d)), _resident(wcq.shape), _resident(ck_n.shape), _resident(cv.shape),
                  _resident(cq_g.shape), _resident(wco.shape)],
        out_specs=[row(d), row(d), row(X_W), row(X_W)],
        out_shape=[SDS((s, d), F32), SDS((s, d), BF16), SDS((s, X_W), F32), SDS((s, X_W), BF16)],
        compiler_params=_cp())(h1, gain, wcq, ck_n, cv, cq_g, wco)


def _cross_bwd(dh2, dh2b, h1, cq_raw, ck_n, cv, cq_g, gain, wcq, wco, *, name):
    s, d = h1.shape
    tm = _rows(s)

    def body(dh2_ref, dh2b_ref, x_ref, cq_ref, k_ref, v_ref, g_ref, gn_ref, wq_ref, wo_ref,
             dcq_ref, dk_ref, dv_ref, dg_ref, dx_ref, dxb_ref, dgn_ref, do_ref):
        @pl.when(pl.program_id(0) == 0)
        def _():
            dk_ref[...] = jnp.zeros_like(dk_ref)
            dv_ref[...] = jnp.zeros_like(dv_ref)
            dg_ref[...] = jnp.zeros_like(dg_ref)
            dgn_ref[...] = jnp.zeros_like(dgn_ref)

        do_ref[...] = _dot_nt(dh2b_ref[...], wo_ref[...]).astype(BF16)

        def front(h):
            sl = slice(h * HEAD_DIM, (h + 1) * HEAD_DIM)
            y, r, qn = _cross_q(cq_ref, g_ref, sl)
            qb = qn.astype(BF16)
            do = do_ref[:, sl]
            return y, r, qb, do, _dot_nt(qb, k_ref[:, sl]), _dot_nt(do, v_ref[:, sl])

        dg = jnp.zeros((1, HEAD_DIM), F32)
        nxt = front(0)
        for h in range(ATT_HEADS):
            sl = slice(h * HEAD_DIM, (h + 1) * HEAD_DIM)
            y, r, qb, do, sc, dp = nxt
            if h + 1 < ATT_HEADS:
                nxt = front(h + 1)
            p = jnp.exp(sc - jnp.max(sc, axis=-1, keepdims=True))
            p = p / jnp.sum(p, axis=-1, keepdims=True)
            dv_ref[:, sl] += _dot_tn(p.astype(BF16), do)
            ds = (p * (dp - jnp.sum(dp * p, axis=-1, keepdims=True))).astype(BF16)
            dk_ref[:, sl] += _dot_tn(ds, qb)
            dn = _dot(ds, k_ref[:, sl]) * SCALE
            dy, dgain = _norm_bwd(y, r, g_ref[...], dn)
            dcq_ref[:, sl] = dy.astype(BF16)
            dg = dg + dgain
        dg_ref[...] += dg

        xv = x_ref[...]
        dx, dgain = _norm_bwd(xv, _rms_r(xv), gn_ref[...], _dot_nt(dcq_ref[...], wq_ref[...]))
        out = dh2_ref[...] + dx
        dx_ref[...] = out
        dxb_ref[...] = out.astype(BF16)
        dgn_ref[...] += dgain

    row = lambda w_: pl.BlockSpec((tm, w_), lambda i: (i, 0))
    acc = lambda r_, w_: pl.BlockSpec((r_, w_), lambda i: (0, 0))
    return _pcall(
        body, name=name, grid=(s // tm,),
        in_specs=[row(d), row(d), row(d), row(X_W), _resident(ck_n.shape), _resident(cv.shape), _resident(cq_g.shape),
                  _resident((1, d)), _resident(wcq.shape), _resident(wco.shape)],
        out_specs=[row(X_W), acc(N_MEM, X_W), acc(N_MEM, X_W), acc(1, HEAD_DIM), row(d), row(d), acc(1, d)],
        out_shape=[SDS((s, X_W), BF16), SDS((N_MEM, X_W), F32), SDS((N_MEM, X_W), F32), SDS((1, HEAD_DIM), F32),
                   SDS((s, d), F32), SDS((s, d), BF16), SDS((1, d), F32)],
        scratch_shapes=[pltpu.VMEM((tm, X_W), BF16)],
        compiler_params=_cp())(dh2, dh2b, h1, cq_raw, ck_n, cv, cq_g, gain, wcq, wco)


def _mem_bwd(dck_n, dcv, ck_raw, memn, mem, wckv, mem_g, ck_g, *, name):
    def body(dk_ref, dv_ref, ckr_ref, mn_ref, m_ref, w_ref, mg_ref, kg_ref, dw_ref, dmg_ref, dkg_ref, dckv_s):
        dkg = jnp.zeros((1, HEAD_DIM), F32)
        for h in range(ATT_HEADS):
            sl = slice(h * HEAD_DIM, (h + 1) * HEAD_DIM)
            y = ckr_ref[:, sl]
            dy, dgain = _norm_bwd(y, _rms_r(y), kg_ref[...], dk_ref[:, sl])
            dckv_s[:, sl] = dy.astype(BF16)
            dkg = dkg + dgain
        dkg_ref[...] = dkg
        dckv_s[:, X_W:] = dv_ref[...].astype(BF16)
        dckv = dckv_s[...]
        dw_ref[...] = _dot_tn(mn_ref[...], dckv).astype(BF16)
        dmn = _dot_nt(dckv, w_ref[...])
        mv = m_ref[...]
        dmg_ref[...] = jnp.sum(dmn * mv * _rms_r(mv), axis=0, keepdims=True)

    return _pcall(
        body, name=name,
        out_shape=[SDS((D_MODEL, 2 * X_W), BF16), SDS((1, D_MODEL), F32), SDS((1, HEAD_DIM), F32)],
        scratch_shapes=[pltpu.VMEM((N_MEM, 2 * X_W), BF16)],
        compiler_params=_cp())(dck_n, dcv, ck_raw, memn, mem, wckv, mem_g, ck_g)


def _rope_tables(pos):
    inv_freq = ROPE_THETA ** (-jnp.arange(0, ROT_DIM, 2, dtype=F32) / ROT_DIM)
    ang = pos.astype(F32)[:, None] * inv_freq
    cos, sin = jnp.cos(ang), jnp.sin(ang)
    s = pos.shape[0]
    rest = HEAD_DIM - ROT_DIM
    cos_t = jnp.concatenate([cos, cos, jnp.ones((s, rest), F32)], axis=1)
    sin_t = jnp.concatenate([-sin, sin, jnp.zeros((s, rest), F32)], axis=1)
    return cos_t, sin_t


def _local_step(x, mem, pos, tgt, w_in, shards, sm):
    cos_t, sin_t = _rope_tables(pos)
    wb = {"w_in": w_in}

    pick = lambda *names: {a: shards[a] for a in names}
    proj, xn, got = _norm_linear(x, sm["mix_norm_g"], wb["w_in"], cn=768, out_dtype=F32, name="fwd_in_proj",
                                 shards=pick("w_out", "w_cq", "w_ckv", "w_co"))
    wb.update(got)
    q_rot, k_rot, v = _qk_prep(proj, cos_t, sin_t, sm["q_norm_g"], sm["k_norm_g"], name="fwd_qk_prep")
    mix, lse_all, got = _attn_fwd(q_rot, k_rot, v, pick("w_down", "w_gate_up"), name="fwd_attn")
    wb.update(got)
    mix = _pool_fwd(proj, mix, sm["pool_w"], sm["pool_scale"], name="fwd_pool")
    h1 = _linear_res(mix, wb["w_out"], x, cn=512, name="fwd_out_proj")
    memn, ck_raw, ck_n, cv = _mem_fwd(mem, sm["mem_norm_g"], wb["w_ckv"], sm["ck_norm_g"], name="fwd_mem")
    h2, hn, cq_raw, co = _cross_fwd(h1, sm["cross_norm_g"], wb["w_cq"], ck_n, cv, sm["cq_norm_g"], wb["w_co"],
                                    name="fwd_cross")
    gu, act, fn = _norm_linear_swiglu(h2, sm["ffn_norm_g"], wb["w_gate_up"], name="fwd_gate_up")
    dy, dyb, sq = _linear_res_loss(act, wb["w_down"], h2, tgt, name="fwd_down_loss")

    gw = {}
    gs = {}
    dgu = _swiglu_bwd(dyb, wb["w_down"], gu, name="bwd_swiglu")
    gw_down = _dw_tn(act, dyb, tkw=1024, tn=1024, name="bwd_dw_down")
    dh2, dh2b, gs["ffn_norm_g"], parts = _linear_nt_normbwd(dgu, wb["w_gate_up"], h2, dy, sm["ffn_norm_g"],
                                                            name="bwd_ffn_in", grads={"w_down": gw_down})
    gw["w_gate_up"] = _dw_tn(fn, dgu, tkw=1024, tn=1536, name="bwd_dw_gate_up")

    gw["w_co"] = _dw_tn(co, dh2b, tkw=512, tn=1024, name="bwd_dw_co")
    dcq, dck_n, dcv, gs["cq_norm_g"], dh1, dh1b, gs["cross_norm_g"] = _cross_bwd(
        dh2, dh2b, h1, cq_raw, ck_n, cv, sm["cq_norm_g"], sm["cross_norm_g"], wb["w_cq"], wb["w_co"], name="bwd_cross")
    gw["w_ckv"], gs["mem_norm_g"], gs["ck_norm_g"] = _mem_bwd(dck_n, dcv, ck_raw, memn, mem, wb["w_ckv"],
                                                             sm["mem_norm_g"], sm["ck_norm_g"], name="bwd_mem")
    gw["w_cq"] = _dw_tn(hn, dcq, tkw=1024, tn=512, name="bwd_dw_cq")

    dmix = _linear_nt(dh1b, wb["w_out"], cn=512, name="bwd_out_proj")
    gw["w_out"] = _dw_tn(mix, dh1b, tkw=1024, tn=1024, name="bwd_dw_out")
    du, gs["pool_w"], gs["pool_scale"] = _pool_bwd(proj, dmix, sm["pool_w"], sm["pool_scale"], name="bwd_pool")
    flat = lambda a: a.reshape(-1, a.shape[-1])
    early = [n for n in SMALL if n in gs]
    dq, dk, dv, got, spread = _attn_bwd(q_rot, k_rot, v, mix, dmix, lse_all, gw, [flat(gs[n]) for n in early] + [sq],
                                        name="bwd_attn")
    parts.update(got)
    small_parts = dict(zip(early + ["sq"], spread))
    dproj, gs["q_norm_g"], gs["k_norm_g"] = _qk_prep_bwd(proj, dq, dk, dv, du, cos_t, sin_t, sm["q_norm_g"],
                                                         sm["k_norm_g"], name="bwd_qk_prep")
    gw_in = _dw_tn(xn, dproj, tkw=1024, tn=1536, name="bwd_dw_in")
    dx, _, gs["mix_norm_g"], last = _linear_nt_normbwd(dproj, wb["w_in"], x, dh1, sm["mix_norm_g"],
                                                       name="bwd_in_proj", grads={"w_in": gw_in})
    parts.update(last)
    late = [n for n in SMALL if n not in small_parts]
    small_parts.update(zip(late, _exchange_small([flat(gs[n]) for n in late])))
    return dx, parts, small_parts


SHARDED = ("w_in", "w_out", "w_cq", "w_ckv", "w_co", "w_gate_up", "w_down")
SMALL = ("mix_norm_g", "q_norm_g", "k_norm_g", "pool_w", "pool_scale", "cross_norm_g", "mem_norm_g", "cq_norm_g",
         "ck_norm_g", "ffn_norm_g")
FULL = {
    "w_in": ((D_MODEL, IN_W), 1, IN_W // N_DEV),
    "w_out": ((D_MODEL, D_MODEL), 0, D_MODEL // N_DEV),
    "w_cq": ((D_MODEL, X_W), 0, D_MODEL // N_DEV),
    "w_ckv": ((D_MODEL, 2 * X_W), 0, D_MODEL // N_DEV),
    "w_co": ((X_W, D_MODEL), 1, D_MODEL // N_DEV),
    "w_gate_up": ((D_MODEL, 2 * FF_PAD), 1, FF_TILE),
    "w_down": ((FF_PAD, D_MODEL), 0, DOWN_SHARD),
}


def _shard_shape(name):
    shape, axis, width = FULL[name]
    return tuple(width if a == axis else n for a, n in enumerate(shape))


def _window(ref, name, dev):
    _, axis, width = FULL[name]
    if name == "w_down":
        start = pl.multiple_of((dev // 2) * FF_TILE + (dev % 2) * DOWN_SHARD, HALO)
    else:
        start = pl.multiple_of(dev * width, BLOCK)
    return ref.at[pl.ds(start, width), :] if axis == 0 else ref.at[:, pl.ds(start, width)]


def _mesh_place():
    x, y, c = lax.axis_index("x"), lax.axis_index("y"), lax.axis_index("c")
    return x, y, c, 4 * x + 2 * y + c


def _peer(x, y, c, k):
    px = 1 - x if k & 4 else x
    py = 1 - y if k & 2 else y
    pc = 1 - c if k & 1 else c
    return (px, py, pc), 4 * px + 2 * py + pc


HBM_SPEC = pl.BlockSpec(memory_space=pltpu.HBM)


def _comm_sems(n):
    return [pltpu.SemaphoreType.DMA((n, N_DEV - 1)), pltpu.SemaphoreType.DMA((n, N_DEV - 1)),
            pltpu.SemaphoreType.DMA((n,))]


DIRECT = (1, 2, 4, 6)
RELAYED = (2, 4, 6)


def _relay_step(steps):
    return max(1, (steps * 11) // 16) if steps > 1 else 0


def _gather_copies(names, ins, outs, send_sems, recv_sems, local_sems):
    x, y, c, me = _mesh_place()
    sibling, _ = _peer(x, y, c, 1)
    local, direct, relays, recv = [], [], {}, {}
    for wi, name in enumerate(names):
        def sems(k, to):
            return dict(send_sem=send_sems.at[wi, k - 1], recv_sem=recv_sems.at[wi, k - 1], device_id=to,
                        device_id_type=MESH)
        local.append(pltpu.make_async_copy(ins[wi], _window(outs[wi], name, me), local_sems.at[wi]))
        for k in range(1, N_DEV):
            peer, pidx = _peer(x, y, c, k)
            win = _window(outs[wi], name, pidx)
            recv[wi, k] = pltpu.make_async_remote_copy(src_ref=ins[wi], dst_ref=win, **sems(k, peer))
            if k in DIRECT:
                direct.append(pltpu.make_async_remote_copy(src_ref=ins[wi], dst_ref=_window(outs[wi], name, me),
                                                           **sems(k, peer)))
            if k in RELAYED:
                relays[wi, k] = pltpu.make_async_remote_copy(src_ref=win, dst_ref=win, **sems(k + 1, sibling))
    return local, direct, relays, recv


def _exchange_copies(names, ins, outs, send_sems, recv_sems, local_sems):
    x, y, c, me = _mesh_place()
    local, sent, recv = [], [], []
    for wi, name in enumerate(names):
        local.append(pltpu.make_async_copy(_window(ins[wi], name, me), outs[wi].at[0], local_sems.at[wi]))
        for k in range(1, N_DEV):
            peer, pidx = _peer(x, y, c, k)
            sems = dict(send_sem=send_sems.at[wi, k - 1], recv_sem=recv_sems.at[wi, k - 1], device_id=peer,
                        device_id_type=MESH)
            sent.append(pltpu.make_async_remote_copy(src_ref=_window(ins[wi], name, pidx), dst_ref=outs[wi].at[k], **sems))
            recv.append(pltpu.make_async_remote_copy(src_ref=_window(ins[wi], name, me), dst_ref=outs[wi].at[k], **sems))
    return local, sent, recv


def _down_pads(down_ref, zero_ref, zero_sems):
    return [pltpu.make_async_copy(zero_ref, down_ref.at[pl.ds(t * FF_TILE + FF_SHARD, FF_TILE - FF_SHARD), :],
                                  zero_sems.at[t]) for t in range(FF_PAD // FF_TILE)]


def _gather_scratch(names):
    if not names:
        return []
    pad = [pltpu.VMEM((FF_TILE - FF_SHARD, D_MODEL), BF16), pltpu.SemaphoreType.DMA((FF_PAD // FF_TILE,))]
    return _comm_sems(len(names)) + (pad if "w_down" in names else [])


def _gather_ops(names, ins, outs, scratch):
    local, direct, relays, recv = _gather_copies(names, ins, outs, *scratch[:3])
    pads = _down_pads(outs[names.index("w_down")], scratch[3], scratch[4]) if "w_down" in names else []

    def start():
        if pads:
            scratch[3][...] = jnp.zeros_like(scratch[3])
        for cp in local + direct + pads:
            cp.start()

    def relay(only=None):
        for (wi, k), cp in relays.items():
            if only is None or names[wi] in only:
                recv[wi, k].wait_recv()
                cp.start()

    def finish():
        for (wi, k), cp in recv.items():
            if k not in RELAYED:
                cp.wait_recv()
        for cp in direct + list(relays.values()):
            cp.wait_send()
        for cp in local + pads:
            cp.wait()

    return start, relay, finish


def _start(copies):
    local, sent, _ = copies
    for cp in local + sent:
        cp.start()


def _finish(copies):
    local, sent, recv = copies
    for cp in recv:
        cp.wait_recv()
    for cp in sent:
        cp.wait_send()
    for cp in local:
        cp.wait()


def _gather_weights(shards):
    names = tuple(shards)
    nw = len(names)

    def body(*refs):
        start, relay, finish = _gather_ops(names, refs[:nw], refs[nw:2 * nw], refs[2 * nw:])
        start()
        relay()
        finish()

    outs = _pcall(
        body, name="gather_weights",
        in_specs=[HBM_SPEC] * nw, out_specs=[HBM_SPEC] * nw,
        out_shape=[SDS(FULL[n][0], BF16) for n in names],
        scratch_shapes=_comm_sems(nw))(*[shards[n] for n in names])
    return dict(zip(names, outs))


def _broadcast_copies(ins, outs, send_sems, recv_sems, local_sems):
    x, y, c, me = _mesh_place()
    local, sent, recv = [], [], []
    for bi in range(len(ins)):
        local.append(pltpu.make_async_copy(ins[bi], outs[bi].at[me], local_sems.at[bi]))
        for k in range(1, N_DEV):
            peer, pidx = _peer(x, y, c, k)
            sems = dict(send_sem=send_sems.at[bi, k - 1], recv_sem=recv_sems.at[bi, k - 1], device_id=peer,
                        device_id_type=MESH)
            sent.append(pltpu.make_async_remote_copy(src_ref=ins[bi], dst_ref=outs[bi].at[me], **sems))
            recv.append(pltpu.make_async_remote_copy(src_ref=ins[bi], dst_ref=outs[bi].at[pidx], **sems))
    return local, sent, recv


def _exchange_small(blocks):
    nb = len(blocks)

    def body(*refs):
        copies = _broadcast_copies(refs[:nb], refs[nb:2 * nb], *refs[2 * nb:])
        _start(copies)
        _finish(copies)

    return _pcall(
        body, name="exchange_small", in_specs=[HBM_SPEC] * nb, out_specs=[HBM_SPEC] * nb,
        out_shape=[SDS((N_DEV,) + a.shape, F32) for a in blocks], scratch_shapes=_comm_sems(nb))(*blocks)


def _adam_math(g, w, m, v):
    m_new = ADAM_B1 * m + (1.0 - ADAM_B1) * g
    v_new = ADAM_B2 * v + (1.0 - ADAM_B2) * (g * g)
    m_hat = m_new / (1.0 - ADAM_B1 ** ADAM_STEP)
    v_hat = v_new / (1.0 - ADAM_B2 ** ADAM_STEP)
    return -ADAM_LR * (m_hat / (jnp.sqrt(v_hat) + ADAM_EPS) + ADAM_WD * w), m_new, v_new


def _adamw_small(parts, w, m, v, sq_parts, *, name):
    n = len(parts)

    def body(*refs):
        p_refs, w_refs, m_refs, v_refs = refs[:n], refs[n:2 * n], refs[2 * n:3 * n], refs[3 * n:4 * n]
        sq_ref, outs = refs[4 * n], refs[4 * n + 1:]
        for i in range(n):
            g = p_refs[i][0]
            for k in range(1, N_DEV):
                g = g + p_refs[i][k]
            delta, m_new, v_new = _adam_math(g, w_refs[i][...], m_refs[i][...], v_refs[i][...])
            outs[4 * i][...] = g
            outs[4 * i + 1][...] = delta
            outs[4 * i + 2][...] = m_new
            outs[4 * i + 3][...] = v_new
        tot = sq_ref[0]
        for k in range(1, N_DEV):
            tot = tot + sq_ref[k]
        outs[4 * n][...] = (0.5 / D_MODEL) * jnp.sum(tot, axis=1, keepdims=True)

    out_shape = [SDS(a.shape, F32) for a in w for _ in range(4)] + [SDS((1, 1), F32)]
    outs = _pcall(body, name=name, out_shape=out_shape, compiler_params=_cp())(*parts, *w, *m, *v, sq_parts)
    return [outs[4 * i:4 * i + 4] for i in range(n)], outs[4 * n][0, 0]


def _adamw(parts, w, m, v, *, name):
    r, c = w.shape
    pc = parts.shape[2]
    tr = r
    for cand in (256, 128, 88):
        if r % cand == 0:
            tr = cand
            break

    def body(p_ref, w_ref, m_ref, v_ref, g_ref, d_ref, mo_ref, vo_ref):
        g = p_ref[0, :, :c].astype(F32)
        for k in range(1, N_DEV):
            g = g + p_ref[k, :, :c].astype(F32)
        g_ref[...] = g
        d_ref[...], mo_ref[...], vo_ref[...] = _adam_math(g, w_ref[...], m_ref[...], v_ref[...])

    row = pl.BlockSpec((tr, c), lambda i: (i, 0))
    return _pcall(
        body, name=name, grid=(r // tr,),
        in_specs=[pl.BlockSpec((N_DEV, tr, pc), lambda i: (0, i, 0)), row, row, row],
        out_specs=[row] * 4, out_shape=[SDS((r, c), F32)] * 4, compiler_params=_cp())(parts, w, m, v)


def _pad_cols(a, width):
    return jnp.pad(a, ((0, 0), (0, width - a.shape[1])))


def kernel(x, mem, positions, mix_norm_g, w_in, q_norm_g, k_norm_g, pool_w, pool_scale, w_out, cross_norm_g, mem_norm_g, w_cq, w_ckv, cq_norm_g, ck_norm_g, w_co, ffn_norm_g, w_gate_up, w_down, loss_target, m_mix_norm_g, m_w_in, m_q_norm_g, m_k_norm_g, m_pool_w, m_pool_scale, m_w_out, m_cross_norm_g, m_mem_norm_g, m_w_cq, m_w_ckv, m_cq_norm_g, m_ck_norm_g, m_w_co, m_ffn_norm_g, m_w_gate_up, m_w_down, v_mix_norm_g, v_w_in, v_q_norm_g, v_k_norm_g, v_pool_w, v_pool_scale, v_w_out, v_cross_norm_g, v_mem_norm_g, v_w_cq, v_w_ckv, v_cq_norm_g, v_ck_norm_g, v_w_co, v_ffn_norm_g, v_w_gate_up, v_w_down):
    given = dict(locals())
    w_f32 = {n: given[n][0] for n in SHARDED + SMALL}
    m_f32 = {n: given["m_" + n][0] for n in SHARDED + SMALL}
    v_f32 = {n: given["v_" + n][0] for n in SHARDED + SMALL}
    shards = {n: w_f32[n].astype(BF16) for n in SHARDED}
    shards["w_gate_up"] = _pad_cols(shards["w_gate_up"], FF_TILE)
    w_in_full = _gather_weights({"w_in": shards.pop("w_in")})["w_in"]
    sm_rows = {n: (w_f32[n] if w_f32[n].ndim == 3 else w_f32[n].reshape(1, -1)) for n in SMALL}
    dx, parts, small = _local_step(x[0], mem[0], positions[0], loss_target[0], w_in_full, shards, sm_rows)

    flat = lambda a: a.reshape(-1, a.shape[-1])
    small_res, loss = _adamw_small([small[n] for n in SMALL], [flat(sm_rows[n]) for n in SMALL],
                                   [flat(m_f32[n].reshape(sm_rows[n].shape)) for n in SMALL],
                                   [flat(v_f32[n].reshape(sm_rows[n].shape)) for n in SMALL], small["sq"],
                                   name="adamw_small")
    res = {n: [a.reshape(w_f32[n].shape) for a in small_res[i]] for i, n in enumerate(SMALL)}
    for n in SHARDED:
        res[n] = _adamw(parts[n], w_f32[n], m_f32[n], v_f32[n], name="adamw_" + n)
    order = ("mix_norm_g", "w_in", "q_norm_g", "k_norm_g", "pool_w", "pool_scale", "w_out", "cross_norm_g",
             "mem_norm_g", "w_cq", "w_ckv", "cq_norm_g", "ck_norm_g", "w_co", "ffn_norm_g", "w_gate_up", "w_down")
    outs = [loss, dx[None]]
    for which in range(4):
        outs += [res[n][which][None] for n in order]
    return tuple(outs)
```

```python
import functools

import jax
import jax.numpy as jnp
from jax import lax
from jax.experimental import pallas as pl
from jax.experimental.pallas import tpu as pltpu

F32 = jnp.float32
BF16 = jnp.bfloat16
SDS = jax.ShapeDtypeStruct

D_MODEL = 1024
HEAD_DIM = 128
N_GROUPS = 3
DILATIONS = (1, 4, 16)
ATT_HEADS = 4
Q_W = 1536
KV_W = 512
POOL_W = 512
POOL_WINDOWS = (2, 4, 8, 16)
IN_W = 3072
X_W = 512
N_MEM = 256
D_FF = 2816
FF_TILE = 768
FF_SHARD = 704
FF_PAD = 4 * FF_TILE
DOWN_SHARD = 352
ROT_DIM = 32
ROT_HALF = 16
ROPE_THETA = 500000.0
EPS = 1e-6
NEG_INF = -1e30
SCALE = HEAD_DIM ** -0.5
BLOCK = 128
HALO = 16

ADAM_LR = 0.001
ADAM_B1 = 0.9
ADAM_B2 = 0.999
ADAM_EPS = 1e-08
ADAM_WD = 0.01
ADAM_STEP = 10

N_DEV = 8
VMEM_LIMIT_BYTES = 56 * 1024 * 1024
MESH = pl.DeviceIdType.MESH


def _pcall(body, **kw):
    return pl.pallas_call(body, **kw)


def _cp():
    return pltpu.CompilerParams(vmem_limit_bytes=VMEM_LIMIT_BYTES)


def _dot(a, b):
    return lax.dot_general(a, b, (((1,), (0,)), ((), ())), preferred_element_type=F32)


def _dot_nt(a, b):
    return lax.dot_general(a, b, (((1,), (1,)), ((), ())), preferred_element_type=F32)


def _dot_tn(a, b):
    return lax.dot_general(a, b, (((0,), (0,)), ((), ())), preferred_element_type=F32)


def _rows(s):
    return min(512, s)


def _rms_r(x):
    return lax.rsqrt(jnp.mean(x * x, axis=-1, keepdims=True) + EPS)


def _norm_bwd(x, r, gain, dxn):
    z = dxn * gain
    dx = r * z - x * (r * r * r * jnp.mean(z * x, axis=-1, keepdims=True))
    dgain = jnp.sum(dxn * x * r, axis=0, keepdims=True)
    return dx, dgain


def _split_bf16(t):
    hi = t.astype(BF16)
    return hi, (t - hi.astype(F32)).astype(BF16)


def _lane_sums(t, ones):
    hi, lo = _split_bf16(t)
    return _dot(hi, ones) + _dot(lo, ones)


def _head_r(y, ones):
    return lax.rsqrt(_lane_sums(y * y, ones) * (1.0 / HEAD_DIM) + EPS)


def _head_norm_bwd(y, r, gain, dn, ones):
    z = dn * gain
    dy = r * z - y * (r * r * r * (_dot((z * y).astype(BF16), ones) * (1.0 / HEAD_DIM)))
    return dy, jnp.sum(dn * y * r, axis=0, keepdims=True)


def _swap_matrix():
    src = lax.broadcasted_iota(jnp.int32, (HEAD_DIM, HEAD_DIM), 0)
    dst = lax.broadcasted_iota(jnp.int32, (HEAD_DIM, HEAD_DIM), 1)
    hit = ((dst < ROT_HALF) & (src == dst + ROT_HALF)) | ((dst >= ROT_HALF) & (dst < ROT_DIM) & (src == dst - ROT_HALF))
    return jnp.where(hit, 1.0, 0.0).astype(BF16)


def _partner(t, swap):
    hi, lo = _split_bf16(t)
    return _dot(hi, swap) + _dot(lo, swap)


def _rope(n, cos_t, sin_t, swap):
    return n * cos_t + _partner(n, swap) * sin_t


def _rope_bwd(d, cos_t, sin_t, swap):
    return d * cos_t + _dot((d * sin_t).astype(BF16), swap)


def _resident(shape):
    return pl.BlockSpec(shape, lambda i: (0,) * len(shape), pipeline_mode=pl.Buffered(1))


def _chunks(n, cn):
    return [slice(j * cn, (j + 1) * cn) for j in range(n // cn)]


def _norm_linear(x, gain, w, *, cn, out_dtype, name, shards=None):
    s, k = x.shape
    n = w.shape[1]
    tm = _rows(s)
    names = tuple(shards or ())
    nr = len(names)

    def body(*refs):
        x_ref, g_ref, w_ref = refs[:3]
        y_ref, xn_ref = refs[3 + nr:5 + nr]
        if nr:
            start, relay, finish = _gather_ops(names, refs[3:3 + nr], refs[5 + nr:5 + 2 * nr], refs[5 + 2 * nr:])
            pl.when(pl.program_id(0) == 0)(start)
            pl.when(pl.program_id(0) == _relay_step(s // tm))(relay)
        xv = x_ref[...]
        xn_ref[...] = (xv * _rms_r(xv) * g_ref[...]).astype(BF16)
        for c in _chunks(n, cn):
            y_ref[:, c] = _dot(xn_ref[...], w_ref[:, c]).astype(out_dtype)
        if nr:
            pl.when(pl.program_id(0) == s // tm - 1)(finish)

    row = lambda w_: pl.BlockSpec((tm, w_), lambda i: (i, 0))
    outs = _pcall(
        body, name=name, grid=(s // tm,),
        in_specs=[row(k), _resident((1, k)), _resident((k, n))] + [HBM_SPEC] * nr,
        out_specs=[row(n), row(k)] + [HBM_SPEC] * nr,
        out_shape=[SDS((s, n), out_dtype), SDS((s, k), BF16)] + [SDS(FULL[a][0], BF16) for a in names],
        scratch_shapes=_gather_scratch(names),
        compiler_params=_cp())(x, gain, w, *[shards[a] for a in names])
    return (outs[0], outs[1], dict(zip(names, outs[2:]))) if nr else tuple(outs)


def _norm_linear_swiglu(x, gain, wgu, *, name):
    s, k = x.shape
    tm = _rows(s)

    def body(x_ref, g_ref, w_ref, gu_ref, a_ref, xn_ref):
        xv = x_ref[...]
        xn_ref[...] = (xv * _rms_r(xv) * g_ref[...]).astype(BF16)
        for c in _chunks(FF_PAD, FF_TILE):
            g = _dot(xn_ref[...], w_ref[:, c])
            u = _dot(xn_ref[...], w_ref[:, slice(FF_PAD + c.start, FF_PAD + c.stop)])
            a_ref[:, c] = (g * jax.nn.sigmoid(g) * u).astype(BF16)
            gu_ref[0, :, c] = g.astype(BF16)
            gu_ref[1, :, c] = u.astype(BF16)

    row = lambda w_: pl.BlockSpec((tm, w_), lambda i: (i, 0))
    return _pcall(
        body, name=name, grid=(s // tm,),
        in_specs=[row(k), _resident((1, k)), _resident((k, 2 * FF_PAD))],
        out_specs=[pl.BlockSpec((2, tm, FF_PAD), lambda i: (0, i, 0)), row(FF_PAD), row(k)],
        out_shape=[SDS((2, s, FF_PAD), BF16), SDS((s, FF_PAD), BF16), SDS((s, k), BF16)],
        compiler_params=_cp())(x, gain, wgu)


def _linear_res(a, w, res, *, cn, name):
    s, k = a.shape
    n = w.shape[1]
    tm = min(2 * _rows(s), s)

    def body(a_ref, w_ref, r_ref, y_ref):
        for c in _chunks(n, cn):
            y_ref[:, c] = r_ref[:, c] + _dot(a_ref[...], w_ref[:, c])

    row = lambda w_: pl.BlockSpec((tm, w_), lambda i: (i, 0))
    return _pcall(
        body, name=name, grid=(s // tm,),
        in_specs=[row(k), _resident((k, n)), row(n)], out_specs=row(n),
        out_shape=SDS((s, n), F32), compiler_params=_cp())(a, w, res)


def _linear_res_loss(a, w, res, tgt, *, name):
    s, k = a.shape
    n = w.shape[1]
    tm = min(2 * _rows(s), s)

    def body(a_ref, w_ref, r_ref, t_ref, dy_ref, dyb_ref, sq_ref):
        e = r_ref[...] + _dot(a_ref[...], w_ref[...]) - t_ref[...]
        dy = e * (1.0 / n)
        dy_ref[...] = dy
        dyb_ref[...] = dy.astype(BF16)

        @pl.when(pl.program_id(0) == 0)
        def _():
            sq_ref[...] = jnp.zeros_like(sq_ref)
        sq_ref[...] += jnp.sum(e * e, axis=0, keepdims=True)

    row = lambda w_: pl.BlockSpec((tm, w_), lambda i: (i, 0))
    return _pcall(
        body, name=name, grid=(s // tm,),
        in_specs=[row(k), _resident((k, n)), row(n), row(n)],
        out_specs=[row(n), row(n), pl.BlockSpec((1, n), lambda i: (0, 0))],
        out_shape=[SDS((s, n), F32), SDS((s, n), BF16), SDS((1, n), F32)],
        compiler_params=_cp())(a, w, res, tgt)


def _linear_nt(g, w, *, cn, name):
    s, k = g.shape
    n = w.shape[0]
    tm = min(2 * _rows(s), s)

    def body(g_ref, w_ref, y_ref):
        for c in _chunks(n, cn):
            y_ref[:, c] = _dot_nt(g_ref[...], w_ref[c, :]).astype(BF16)

    row = lambda w_: pl.BlockSpec((tm, w_), lambda i: (i, 0))
    return _pcall(
        body, name=name, grid=(s // tm,),
        in_specs=[row(k), _resident((n, k))], out_specs=row(n),
        out_shape=SDS((s, n), BF16), compiler_params=_cp())(g, w)


def _swiglu_bwd(dyb, wd, gu, *, name):
    s, n = dyb.shape
    tm = _rows(s)

    def body(dy_ref, wd_ref, gu_ref, dgu_ref):
        for c in _chunks(FF_PAD, FF_TILE):
            da = _dot_nt(dy_ref[...], wd_ref[c, :])
            g = gu_ref[0, :, c].astype(F32)
            u = gu_ref[1, :, c].astype(F32)
            sg = jax.nn.sigmoid(g)
            dgu_ref[0, :, c] = (da * u * (sg * (1.0 + g * (1.0 - sg)))).astype(BF16)
            dgu_ref[1, :, c] = (da * (g * sg)).astype(BF16)

    half = pl.BlockSpec((2, tm, FF_PAD), lambda i: (0, i, 0))
    return _pcall(
        body, name=name, grid=(s // tm,),
        in_specs=[pl.BlockSpec((tm, n), lambda i: (i, 0)), _resident((FF_PAD, n)), half],
        out_specs=half, out_shape=SDS((2, s, FF_PAD), BF16), compiler_params=_cp())(dyb, wd, gu)


def _linear_nt_normbwd(g, w, x, dres, gain, *, name, grads=None):
    d, k = w.shape
    s = x.shape[0]
    tm = _rows(s)
    names = tuple(grads or ())
    nr = len(names)

    def body(*refs):
        g_ref, w_ref, x_ref, dr_ref, gn_ref = refs[:5]
        dx_ref, dxb_ref, dg_ref = refs[5 + nr:8 + nr]
        if nr:
            copies = _exchange_copies(names, refs[5:5 + nr], refs[8 + nr:8 + 2 * nr], *refs[8 + 2 * nr:])

        @pl.when(pl.program_id(0) == 0)
        def _():
            dg_ref[...] = jnp.zeros_like(dg_ref)
            if nr:
                _start(copies)

        if g.ndim == 3:
            dxn = _dot_nt(g_ref[0], w_ref[:, :k // 2]) + _dot_nt(g_ref[1], w_ref[:, k // 2:])
        else:
            dxn = _dot_nt(g_ref[...], w_ref[...])
        xv = x_ref[...]
        dx, dgain = _norm_bwd(xv, _rms_r(xv), gn_ref[...], dxn)
        out = dr_ref[...] + dx
        dx_ref[...] = out
        dxb_ref[...] = out.astype(BF16)
        dg_ref[...] += dgain

        if nr:
            @pl.when(pl.program_id(0) == s // tm - 1)
            def _():
                _finish(copies)

    row = pl.BlockSpec((tm, d), lambda i: (i, 0))
    g_spec = (pl.BlockSpec((2, tm, k // 2), lambda i: (0, i, 0)) if g.ndim == 3
              else pl.BlockSpec((tm, k), lambda i: (i, 0)))
    outs = _pcall(
        body, name=name, grid=(s // tm,),
        in_specs=[g_spec, _resident((d, k)), row, row, _resident((1, d))] + [HBM_SPEC] * nr,
        out_specs=[row, row, pl.BlockSpec((1, d), lambda i: (0, 0))] + [HBM_SPEC] * nr,
        out_shape=[SDS((s, d), F32), SDS((s, d), BF16), SDS((1, d), F32)]
        + [SDS((N_DEV,) + _shard_shape(n), BF16) for n in names],
        scratch_shapes=_comm_sems(nr) if nr else [],
        compiler_params=_cp())(g, w, x, dres, gain, *[grads[n] for n in names])
    return (outs[0], outs[1], outs[2], dict(zip(names, outs[3:]))) if nr else tuple(outs)


def _dw_tn(x, g, *, tkw, tn, name):
    s, kw = x.shape
    halves = g.ndim == 3
    n = 2 * g.shape[2] if halves else g.shape[1]
    ts = min(2048, s)
    ns = s // ts
    per_half = n // 2 // tn

    def body(x_ref, g_ref, o_ref, acc_ref):
        ss = pl.program_id(2)

        @pl.when(ss == 0)
        def _():
            acc_ref[...] = jnp.zeros_like(acc_ref)

        acc_ref[...] += _dot_tn(x_ref[...], g_ref[...])

        @pl.when(ss == ns - 1)
        def _():
            o_ref[...] = acc_ref[...].astype(BF16)

    g_spec = (pl.BlockSpec((None, ts, tn), lambda a, b, ss: (b // per_half, ss, b % per_half)) if halves
              else pl.BlockSpec((ts, tn), lambda a, b, ss: (ss, b)))
    return _pcall(
        body, name=name, grid=(kw // tkw, n // tn, ns),
        in_specs=[pl.BlockSpec((ts, tkw), lambda a, b, ss: (ss, a)), g_spec],
        out_specs=pl.BlockSpec((tkw, tn), lambda a, b, ss: (a, b)),
        out_shape=SDS((kw, n), BF16),
        scratch_shapes=[pltpu.VMEM((tkw, tn), F32)], compiler_params=_cp())(x, g)


def _qk_prep(proj, cos_t, sin_t, qg, kg, *, name):
    s = proj.shape[0]
    tm = _rows(s)
    nqh = Q_W // HEAD_DIM

    def body(q_ref, k_ref, v_ref, c_ref, s_ref, qg_ref, kg_ref, qo_ref, ko_ref, vo_ref):
        c, sn = c_ref[...], s_ref[...]
        ones, swap = jnp.ones((HEAD_DIM, HEAD_DIM), BF16), _swap_matrix()
        for h in range(nqh):
            sl = slice(h * HEAD_DIM, (h + 1) * HEAD_DIM)
            y = q_ref[:, sl]
            qo_ref[:, sl] = (_rope(y * _head_r(y, ones) * qg_ref[...], c, sn, swap) * SCALE).astype(BF16)
        for h in range(ATT_HEADS):
            sl = slice(h * HEAD_DIM, (h + 1) * HEAD_DIM)
            y = k_ref[:, sl]
            ko_ref[:, sl] = _rope(y * _head_r(y, ones) * kg_ref[...], c, sn, swap).astype(BF16)
        vo_ref[...] = v_ref[...].astype(BF16)

    row = lambda w, j: pl.BlockSpec((tm, w), lambda i: (i, j))
    one = pl.BlockSpec((1, HEAD_DIM), lambda i: (0, 0))
    return _pcall(
        body, name=name, grid=(s // tm,),
        in_specs=[row(Q_W, 0), row(KV_W, 3), row(KV_W, 4), row(HEAD_DIM, 0), row(HEAD_DIM, 0), one, one],
        out_specs=[row(Q_W, 0), row(KV_W, 0), row(KV_W, 0)],
        out_shape=[SDS((s, Q_W), BF16), SDS((s, KV_W), BF16), SDS((s, KV_W), BF16)],
        compiler_params=_cp())(proj, proj, proj, cos_t, sin_t, qg, kg)


def _qk_prep_bwd(proj, dq, dk, dv, du, cos_t, sin_t, qg, kg, *, name):
    s = proj.shape[0]
    tm = _rows(s)

    def body(q_ref, k_ref, dq_ref, dk_ref, dv_ref, du_ref, c_ref, s_ref, qg_ref, kg_ref, dp_ref, dqg_ref, dkg_ref):
        c, sn = c_ref[...], s_ref[...]

        @pl.when(pl.program_id(0) == 0)
        def _():
            dqg_ref[...] = jnp.zeros_like(dqg_ref)
            dkg_ref[...] = jnp.zeros_like(dkg_ref)

        ones, swap = jnp.ones((HEAD_DIM, HEAD_DIM), BF16), _swap_matrix()
        dqg = jnp.zeros((1, HEAD_DIM), F32)
        for g in range(N_GROUPS):
            for h in range(ATT_HEADS):
                sl = slice(h * HEAD_DIM, (h + 1) * HEAD_DIM)
                col = slice(g * KV_W + h * HEAD_DIM, g * KV_W + (h + 1) * HEAD_DIM)
                y = q_ref[:, col]
                dn = _rope_bwd(dq_ref[g, :, sl].astype(F32) * SCALE, c, sn, swap)
                dy, dgain = _head_norm_bwd(y, _head_r(y, ones), qg_ref[...], dn, ones)
                dp_ref[:, col] = dy.astype(BF16)
                dqg = dqg + dgain
        dqg_ref[...] += dqg

        dkg = jnp.zeros((1, HEAD_DIM), F32)
        for h in range(ATT_HEADS):
            sl = slice(h * HEAD_DIM, (h + 1) * HEAD_DIM)
            y = k_ref[:, sl]
            dn = _rope_bwd(dk_ref[:, sl], c, sn, swap)
            dy, dgain = _head_norm_bwd(y, _head_r(y, ones), kg_ref[...], dn, ones)
            dp_ref[:, Q_W + h * HEAD_DIM:Q_W + (h + 1) * HEAD_DIM] = dy.astype(BF16)
            dkg = dkg + dgain
        dkg_ref[...] += dkg

        dp_ref[:, Q_W + KV_W:Q_W + 2 * KV_W] = dv_ref[...].astype(BF16)
        dp_ref[:, Q_W + 2 * KV_W:] = du_ref[...].astype(BF16)

    row = lambda w, j: pl.BlockSpec((tm, w), lambda i: (i, j))
    one = pl.BlockSpec((1, HEAD_DIM), lambda i: (0, 0))
    return _pcall(
        body, name=name, grid=(s // tm,),
        in_specs=[row(Q_W, 0), row(KV_W, 3), pl.BlockSpec((N_GROUPS, tm, KV_W), lambda i: (0, i, 0))]
        + [row(KV_W, 0)] * 3 + [row(HEAD_DIM, 0), row(HEAD_DIM, 0), one, one],
        out_specs=[row(IN_W, 0), one, one],
        out_shape=[SDS((s, IN_W), BF16), SDS((1, HEAD_DIM), F32), SDS((1, HEAD_DIM), F32)],
        compiler_params=_cp())(proj, proj, dq, dk, dv, du, cos_t, sin_t, qg, kg)


ATT_ROWS = 16 * BLOCK


def _sub(ref, start, d, size=BLOCK):
    return ref[pl.ds(start, size, stride=d), :] if d > 1 else ref[pl.ds(start, size), :]


def _sub_set(ref, start, d, val):
    if d > 1:
        ref[pl.ds(start, BLOCK, stride=d), :] = val
    else:
        ref[pl.ds(start, BLOCK), :] = val


def _band_masks():
    row = lax.broadcasted_iota(jnp.int32, (BLOCK, BLOCK), 0)
    col = lax.broadcasted_iota(jnp.int32, (BLOCK, BLOCK), 1)
    return col <= row, col >= row


def _eye(n=BLOCK):
    row = lax.broadcasted_iota(jnp.int32, (n, n), 0)
    col = lax.broadcasted_iota(jnp.int32, (n, n), 1)
    return jnp.where(row == col, 1.0, 0.0).astype(BF16)


def _attn_fwd(q_rot, k_rot, v, shards, *, name):
    s = q_rot.shape[0]
    rr = ATT_ROWS
    nblk = s // rr
    names = tuple(shards)
    nr = len(names)

    def body(*refs):
        q0, q1, q2, kp, kc, vp, vc = refs[:7]
        mix_ref, lse_ref = refs[7 + nr:9 + nr]
        qs, ks, vs, os_, ls = refs[9 + 2 * nr:14 + 2 * nr]
        h, n = pl.program_id(0), pl.program_id(1)
        start, relay, finish = _gather_ops(names, refs[7:7 + nr], refs[9 + nr:9 + 2 * nr], refs[14 + 2 * nr:])
        pl.when((h == 0) & (n == 0))(start)
        steps = ATT_HEADS * nblk
        early, late = max(1, (steps * 6) // 16), max(1, (steps * 14) // 16)
        pl.when(h * nblk + n == early)(functools.partial(relay, names[:1]))
        pl.when(h * nblk + n == late)(functools.partial(relay, names[1:]))
        for g, q_ref in enumerate((q0, q1, q2)):
            qs[g] = q_ref[...].astype(F32)
        ks[:rr] = kp[...].astype(F32)
        ks[rr:] = kc[...].astype(F32)
        vs[:rr] = vp[...].astype(F32)
        vs[rr:] = vc[...].astype(F32)
        m_cur, m_band = _band_masks()
        mask_in = jnp.concatenate([m_band, m_cur], axis=1)
        mask_first = jnp.concatenate([m_band & (n > 0), m_cur], axis=1)
        ones = jnp.ones((2 * BLOCK, HEAD_DIM), BF16)
        pieces = [(g, d, j * BLOCK * d + r, j) for g, d in enumerate(DILATIONS) for r in range(d)
                  for j in range(rr // (BLOCK * d))]

        def scores(piece):
            g, d, base, j = piece
            q = _sub(qs.at[g], base, d).astype(BF16)
            k2 = _sub(ks, rr + base - BLOCK * d, d, 2 * BLOCK).astype(BF16)
            return jnp.where(mask_first if j == 0 else mask_in, _dot_nt(q, k2), NEG_INF)

        sc = scores(pieces[0])
        for i, (g, d, base, j) in enumerate(pieces):
            cur = sc
            if i + 1 < len(pieces):
                sc = scores(pieces[i + 1])
            m = jnp.max(cur, axis=-1, keepdims=True)
            p = jnp.exp(cur - m).astype(BF16)
            v2 = _sub(vs, rr + base - BLOCK * d, d, 2 * BLOCK).astype(BF16)
            acc_l = _dot(p, jnp.concatenate([v2, ones], axis=1))
            l = acc_l[:, HEAD_DIM:]
            _sub_set(os_.at[g], base, d, acc_l[:, :HEAD_DIM] / l)
            _sub_set(ls.at[g], base, d, m + jnp.log(l))
        for c in _chunks(rr, 2 * BLOCK):
            a, b, cc = ls[0, c, :], ls[1, c, :], ls[2, c, :]
            m = jnp.maximum(jnp.maximum(a, b), cc)
            wa, wb, wc = jnp.exp(a - m), jnp.exp(b - m), jnp.exp(cc - m)
            den = wa + wb + wc
            mix_ref[c, :] = ((wa * os_[0, c, :] + wb * os_[1, c, :] + wc * os_[2, c, :]) / den).astype(BF16)
            lse_ref[c, :] = m + jnp.log(den)

        pl.when((h == ATT_HEADS - 1) & (n == nblk - 1))(finish)

    blk = lambda f: pl.BlockSpec((rr, HEAD_DIM), f)
    prv = lambda n: jnp.maximum(n - 1, 0)
    outs = _pcall(
        body, name=name, grid=(ATT_HEADS, nblk),
        in_specs=[blk(lambda h, n, g=g: (n, g * ATT_HEADS + h)) for g in range(N_GROUPS)]
        + [blk(lambda h, n: (prv(n), h)), blk(lambda h, n: (n, h))] * 2 + [HBM_SPEC] * nr,
        out_specs=[blk(lambda h, n: (n, h)), blk(lambda h, n: (n, h))] + [HBM_SPEC] * nr,
        out_shape=[SDS((s, KV_W + POOL_W), BF16), SDS((s, KV_W), F32)] + [SDS(FULL[w][0], BF16) for w in names],
        scratch_shapes=[pltpu.VMEM((N_GROUPS, rr, HEAD_DIM), F32), pltpu.VMEM((2 * rr, HEAD_DIM), F32),
                        pltpu.VMEM((2 * rr, HEAD_DIM), F32), pltpu.VMEM((N_GROUPS, rr, HEAD_DIM), F32),
                        pltpu.VMEM((N_GROUPS, rr, HEAD_DIM), F32)] + _gather_scratch(names),
        compiler_params=_cp())(q_rot, q_rot, q_rot, k_rot, k_rot, v, v, *[shards[w] for w in names])
    return outs[0], outs[1], dict(zip(names, outs[2:]))


def _attn_bwd(q_rot, k_rot, v, mix, dmix, lse, grads, blocks, *, name):
    s = q_rot.shape[0]
    rr = ATT_ROWS
    nblk = s // rr
    names = tuple(grads)
    nr, nb = len(names), len(blocks)
    nx = nr + nb

    def body(*refs):
        q0, q1, q2, qx0, qx1, qx2, kp, kc, vp, vc, do_c, do_x, o_c, o_x, l_c, l_x = refs[:16]
        dq_ref, dk_ref, dv_ref = refs[16 + nx:19 + nx]
        qs, ks, vs, dos, lss, dls, dqs, dks, dvs = refs[19 + 2 * nx:28 + 2 * nx]
        h, n = pl.program_id(0), pl.program_id(1)
        copies = _exchange_copies(names, refs[16:16 + nr], refs[19 + nx:19 + nx + nr], *refs[28 + 2 * nx:31 + 2 * nx])
        spread = _broadcast_copies(refs[16 + nr:16 + nx], refs[19 + nx + nr:19 + 2 * nx], *refs[31 + 2 * nx:])

        @pl.when((h == 0) & (n == 0))
        def _():
            _start(copies)
            _start(spread)

        for g, (qc_ref, qx_ref) in enumerate(((q0, qx0), (q1, qx1), (q2, qx2))):
            qs[g, :rr] = qc_ref[...].astype(F32)
            qs[g, rr:] = qx_ref[...].astype(F32)
        ks[:rr] = kp[...].astype(F32)
        ks[rr:] = kc[...].astype(F32)
        vs[:rr] = vp[...].astype(F32)
        vs[rr:] = vc[...].astype(F32)
        lss[:rr] = l_c[...]
        lss[rr:] = l_x[...]
        for half, (d_ref, o_ref) in enumerate(((do_c, o_c), (do_x, o_x))):
            for c in _chunks(rr, 2 * BLOCK):
                cs = slice(half * rr + c.start, half * rr + c.stop)
                dof = d_ref[c, :].astype(F32)
                dos[cs, :] = dof
                dls[cs, :] = jnp.broadcast_to(jnp.sum(dof * o_ref[c, :].astype(F32), axis=-1, keepdims=True),
                                              (2 * BLOCK, HEAD_DIM))
        m_cur, m_band = _band_masks()
        mask_in = jnp.concatenate([m_cur, m_band], axis=0)
        mask_last = jnp.concatenate([m_cur, m_band & (n + 1 < nblk)], axis=0)
        m_first = m_band & (n > 0)
        eye = _eye()
        pieces = [(g, d, j * BLOCK * d + r, j, rr // (BLOCK * d)) for g, d in enumerate(DILATIONS) for r in range(d)
                  for j in range(rr // (BLOCK * d))]

        def front(piece):
            g, d, base, _, _ = piece
            q2 = _sub(qs.at[g], base, d, 2 * BLOCK).astype(BF16)
            do2 = _sub(dos, base, d, 2 * BLOCK).astype(BF16)
            k = _sub(ks, rr + base, d).astype(BF16)
            vv = _sub(vs, rr + base, d).astype(BF16)
            return q2, do2, k, _dot_nt(q2, k), _dot_nt(do2, vv)

        def middle(piece, fr, dq_acc):
            g, d, base, j, nsub = piece
            q2, do2, k, s2, dp2 = fr
            if j == 0:
                kp_ = _sub(ks, rr + base - BLOCK * d, d).astype(BF16)
                p0 = jnp.where(m_first, jnp.exp(_dot_nt(q2[:BLOCK], kp_) - _sub(lss, base, d)), 0.0)
                ds0 = p0 * (_dot_nt(do2[:BLOCK], _sub(vs, rr + base - BLOCK * d, d).astype(BF16)) - _sub(dls, base, d))
                dq_acc = _dot(ds0.astype(BF16), kp_)
            p2 = jnp.where(mask_last if j + 1 == nsub else mask_in,
                           jnp.exp(s2 - _sub(lss, base, d, 2 * BLOCK)), 0.0)
            ds2 = (p2 * (dp2 - _sub(dls, base, d, 2 * BLOCK))).astype(BF16)
            dq2 = _dot(ds2, k)
            return dq2[BLOCK:], (_dot_nt(eye, ds2), _dot_nt(eye, p2.astype(BF16)), q2, do2, dq_acc + dq2[:BLOCK])

        def back(piece, tr):
            g, d, base, _, _ = piece
            ds_t, p_t, q2, do2, dq = tr
            _sub_set(dqs.at[g], base, d, dq)
            dk, dv = _dot(ds_t.astype(BF16), q2), _dot(p_t.astype(BF16), do2)
            if g == 0:
                _sub_set(dks, base, d, dk)
                _sub_set(dvs, base, d, dv)
            else:
                _sub_set(dks, base, d, _sub(dks, base, d) + dk)
                _sub_set(dvs, base, d, _sub(dvs, base, d) + dv)

        fr, held, dq_acc = front(pieces[0]), None, None
        for i, piece in enumerate(pieces):
            cur = fr
            if i + 1 < len(pieces):
                fr = front(pieces[i + 1])
            dq_acc, now = middle(piece, cur, dq_acc)
            if held is not None:
                back(pieces[i - 1], held)
            held = now
        back(pieces[-1], held)
        for g in range(N_GROUPS):
            dq_ref[g] = dqs[g].astype(BF16)
        dk_ref[...] = dks[...]
        dv_ref[...] = dvs[...]

        @pl.when((h == ATT_HEADS - 1) & (n == nblk - 1))
        def _():
            _finish(copies)
            _finish(spread)

    blk = lambda f: pl.BlockSpec((rr, HEAD_DIM), f)
    prv = lambda n: jnp.maximum(n - 1, 0)
    nxt = lambda n: jnp.minimum(n + 1, nblk - 1)
    cur_kv = blk(lambda h, n: (n, h))
    outs = _pcall(
        body, name=name, grid=(ATT_HEADS, nblk),
        in_specs=[blk(lambda h, n, g=g: (n, g * ATT_HEADS + h)) for g in range(N_GROUPS)]
        + [blk(lambda h, n, g=g: (nxt(n), g * ATT_HEADS + h)) for g in range(N_GROUPS)]
        + [blk(lambda h, n: (prv(n), h)), cur_kv] * 2
        + [cur_kv, blk(lambda h, n: (nxt(n), h))] * 3 + [HBM_SPEC] * nx,
        out_specs=[pl.BlockSpec((N_GROUPS, rr, HEAD_DIM), lambda h, n: (0, n, h)), cur_kv, cur_kv] + [HBM_SPEC] * nx,
        out_shape=[SDS((N_GROUPS, s, KV_W), BF16), SDS((s, KV_W), F32), SDS((s, KV_W), F32)]
        + [SDS((N_DEV,) + _shard_shape(w), BF16) for w in names] + [SDS((N_DEV,) + a.shape, F32) for a in blocks],
        scratch_shapes=[pltpu.VMEM((N_GROUPS, 2 * rr, HEAD_DIM), F32)] + [pltpu.VMEM((2 * rr, HEAD_DIM), F32)] * 5
        + [pltpu.VMEM((N_GROUPS, rr, HEAD_DIM), F32), pltpu.VMEM((rr, HEAD_DIM), F32), pltpu.VMEM((rr, HEAD_DIM), F32)]
        + _comm_sems(nr) + _comm_sems(nb),
        compiler_params=_cp())(q_rot, q_rot, q_rot, q_rot, q_rot, q_rot, k_rot, k_rot, v, v, dmix, dmix, mix, mix,
                               lse, lse, *[grads[w] for w in names], *blocks)
    return outs[0], outs[1], outs[2], dict(zip(names, outs[3:3 + nr])), list(outs[3 + nr:])


def _pool_d(u_ref, halo_ref, i, tm):
    halo = jnp.where(i > 0, halo_ref[...], 0.0)
    t = i * tm + lax.broadcasted_iota(jnp.int32, (tm, 1), 0)
    out = []
    for g, w in enumerate(POOL_WINDOWS):
        sl = slice(g * HEAD_DIM, (g + 1) * HEAD_DIM)
        u = u_ref[:, sl]
        acc = jnp.concatenate([halo[:, sl], u], axis=0)
        sh = 1
        while sh < w:
            acc = acc + pltpu.roll(acc, sh, 0)
            sh *= 2
        cnt = jnp.minimum(t + 1, w).astype(F32)
        out.append(acc[HALO:, :] / cnt - u)
    return out


def _pool_fwd(proj, mix, pool_w, pool_scale, *, name):
    s = proj.shape[0]
    tm = _rows(s)
    ucol = (IN_W - POOL_W) // POOL_W

    def body(u_ref, halo_ref, mix_in, w_ref, sc_ref, o_ref):
        del mix_in
        dd = _pool_d(u_ref, halo_ref, pl.program_id(0), tm)
        for g in range(len(POOL_WINDOWS)):
            sl = slice(g * HEAD_DIM, (g + 1) * HEAD_DIM)
            y = _dot(dd[g].astype(BF16), w_ref[g].astype(BF16))
            o_ref[:, sl] = (y * sc_ref[:, sl]).astype(BF16)

    return _pcall(
        body, name=name, grid=(s // tm,),
        in_specs=[pl.BlockSpec((tm, POOL_W), lambda i: (i, ucol)),
                  pl.BlockSpec((HALO, POOL_W), lambda i: (jnp.maximum(i * (tm // HALO) - 1, 0), ucol)),
                  pl.BlockSpec(memory_space=pl.ANY),
                  pl.BlockSpec((len(POOL_WINDOWS), HEAD_DIM, HEAD_DIM), lambda i: (0, 0, 0)),
                  pl.BlockSpec((1, POOL_W), lambda i: (0, 0))],
        out_specs=pl.BlockSpec((tm, POOL_W), lambda i: (i, 1)),
        out_shape=SDS(mix.shape, BF16), input_output_aliases={2: 0},
        compiler_params=_cp())(proj, proj, mix, pool_w, pool_scale)


def _pool_bwd(proj, dmix, pool_w, pool_scale, *, name):
    s = proj.shape[0]
    tm = _rows(s)
    nblk = s // tm
    ucol = (IN_W - POOL_W) // POOL_W
    ng = len(POOL_WINDOWS)

    def body(u_ref, halo_ref, dp_ref, dpn_ref, w_ref, sc_ref, du_ref, dw_ref, dsc_ref):
        i = pl.program_id(0)

        @pl.when(i == 0)
        def _():
            dw_ref[...] = jnp.zeros_like(dw_ref)
            dsc_ref[...] = jnp.zeros_like(dsc_ref)

        dd = _pool_d(u_ref, halo_ref, i, tm)
        t = i * tm + lax.broadcasted_iota(jnp.int32, (tm, 1), 0)
        dpn = jnp.where(i + 1 < nblk, dpn_ref[...].astype(F32), 0.0)
        for g, w in enumerate(POOL_WINDOWS):
            sl = slice(g * HEAD_DIM, (g + 1) * HEAD_DIM)
            wg = w_ref[g].astype(BF16)
            db = dd[g].astype(BF16)
            dp = dp_ref[:, sl].astype(F32)
            dsc_ref[:, sl] += jnp.sum(dp * _dot(db, wg), axis=0, keepdims=True)
            dy = (dp * sc_ref[:, sl]).astype(BF16)
            dw_ref[g] += _dot_tn(db, dy)
            g_d = _dot_nt(dy, wg)
            g_dn = _dot_nt((dpn[:, sl] * sc_ref[:, sl]).astype(BF16), wg)
            cnt = jnp.minimum(t + 1, w).astype(F32)
            acc = jnp.concatenate([g_d / cnt, g_dn * (1.0 / w)], axis=0)
            sh = 1
            while sh < w:
                acc = acc + pltpu.roll(acc, tm + HALO - sh, 0)
                sh *= 2
            du_ref[:, sl] = acc[:tm, :] - g_d

    nh = s // HALO
    return _pcall(
        body, name=name, grid=(nblk,),
        in_specs=[pl.BlockSpec((tm, POOL_W), lambda i: (i, ucol)),
                  pl.BlockSpec((HALO, POOL_W), lambda i: (jnp.maximum(i * (tm // HALO) - 1, 0), ucol)),
                  pl.BlockSpec((tm, POOL_W), lambda i: (i, 1)),
                  pl.BlockSpec((HALO, POOL_W), lambda i: (jnp.minimum((i + 1) * (tm // HALO), nh - 1), 1)),
                  pl.BlockSpec((ng, HEAD_DIM, HEAD_DIM), lambda i: (0, 0, 0)),
                  pl.BlockSpec((1, POOL_W), lambda i: (0, 0))],
        out_specs=[pl.BlockSpec((tm, POOL_W), lambda i: (i, 0)),
                   pl.BlockSpec((ng, HEAD_DIM, HEAD_DIM), lambda i: (0, 0, 0)),
                   pl.BlockSpec((1, POOL_W), lambda i: (0, 0))],
        out_shape=[SDS((s, POOL_W), F32), SDS((ng, HEAD_DIM, HEAD_DIM), F32), SDS((1, POOL_W), F32)],
        compiler_params=_cp())(proj, proj, dmix, dmix, pool_w, pool_scale)


def _mem_fwd(mem, mem_g, wckv, ck_g, *, name):
    def body(m_ref, g_ref, w_ref, kg_ref, mn_ref, ckr_ref, ckn_ref, cv_ref):
        mv = m_ref[...]
        mn = (mv * _rms_r(mv) * g_ref[...]).astype(BF16)
        mn_ref[...] = mn
        ckv = _dot(mn, w_ref[...])
        ckr_ref[...] = ckv[:, :X_W]
        cv_ref[...] = ckv[:, X_W:].astype(BF16)
        for h in range(ATT_HEADS):
            sl = slice(h * HEAD_DIM, (h + 1) * HEAD_DIM)
            y = ckv[:, sl]
            ckn_ref[:, sl] = (y * _rms_r(y) * kg_ref[...]).astype(BF16)

    return _pcall(
        body, name=name,
        out_shape=[SDS((N_MEM, D_MODEL), BF16), SDS((N_MEM, X_W), F32), SDS((N_MEM, X_W), BF16),
                   SDS((N_MEM, X_W), BF16)],
        compiler_params=_cp())(mem, mem_g, wckv, ck_g)


def _cross_q(cq_ref, g_ref, sl):
    y = cq_ref[:, sl]
    r = _rms_r(y)
    return y, r, y * r * g_ref[...] * SCALE


def _cross_fwd(h1, gain, wcq, ck_n, cv, cq_g, wco, *, name):
    s, d = h1.shape
    tm = min(2 * _rows(s), s)

    def body(x_ref, gn_ref, wq_ref, k_ref, v_ref, g_ref, wo_ref, h2_ref, xn_ref, cq_ref, o_ref):
        xv = x_ref[...]
        xn_ref[...] = (xv * _rms_r(xv) * gn_ref[...]).astype(BF16)
        cq_ref[...] = _dot(xn_ref[...], wq_ref[...])
        for h in range(ATT_HEADS):
            sl = slice(h * HEAD_DIM, (h + 1) * HEAD_DIM)
            _, _, qn = _cross_q(cq_ref, g_ref, sl)
            sc = _dot_nt(qn.astype(BF16), k_ref[:, sl])
            p = jnp.exp(sc - jnp.max(sc, axis=-1, keepdims=True))
            p = p / jnp.sum(p, axis=-1, keepdims=True)
            o_ref[:, sl] = _dot(p.astype(BF16), v_ref[:, sl]).astype(BF16)
        for c in _chunks(d, X_W):
            h2_ref[:, c] = x_ref[:, c] + _dot(o_ref[...], wo_ref[:, c])

    row = lambda w_: pl.BlockSpec((tm, w_), lambda i: (i, 0))
    return _pcall(
        body, name=name, grid=(s // tm,),
        in_specs=[row(d), _resident((1, d)), _resident(wcq.shape), _resident(ck_n.shape), _resident(cv.shape),
                  _resident(cq_g.shape), _resident(wco.shape)],
        out_specs=[row(d), row(d), row(X_W), row(X_W)],
        out_shape=[SDS((s, d), F32), SDS((s, d), BF16), SDS((s, X_W), F32), SDS((s, X_W), BF16)],
        compiler_params=_cp())(h1, gain, wcq, ck_n, cv, cq_g, wco)


def _cross_bwd(dh2, dh2b, h1, cq_raw, ck_n, cv, cq_g, gain, wcq, wco, *, name):
    s, d = h1.shape
    tm = _rows(s)

    def body(dh2_ref, dh2b_ref, x_ref, cq_ref, k_ref, v_ref, g_ref, gn_ref, wq_ref, wo_ref,
             dcq_ref, dk_ref, dv_ref, dg_ref, dx_ref, dxb_ref, dgn_ref, do_ref):
        @pl.when(pl.program_id(0) == 0)
        def _():
            dk_ref[...] = jnp.zeros_like(dk_ref)
            dv_ref[...] = jnp.zeros_like(dv_ref)
            dg_ref[...] = jnp.zeros_like(dg_ref)
            dgn_ref[...] = jnp.zeros_like(dgn_ref)

        do_ref[...] = _dot_nt(dh2b_ref[...], wo_ref[...]).astype(BF16)

        def front(h):
            sl = slice(h * HEAD_DIM, (h + 1) * HEAD_DIM)
            y, r, qn = _cross_q(cq_ref, g_ref, sl)
            qb = qn.astype(BF16)
            do = do_ref[:, sl]
            return y, r, qb, do, _dot_nt(qb, k_ref[:, sl]), _dot_nt(do, v_ref[:, sl])

        dg = jnp.zeros((1, HEAD_DIM), F32)
        nxt = front(0)
        for h in range(ATT_HEADS):
            sl = slice(h * HEAD_DIM, (h + 1) * HEAD_DIM)
            y, r, qb, do, sc, dp = nxt
            if h + 1 < ATT_HEADS:
                nxt = front(h + 1)
            p = jnp.exp(sc - jnp.max(sc, axis=-1, keepdims=True))
            p = p / jnp.sum(p, axis=-1, keepdims=True)
            dv_ref[:, sl] += _dot_tn(p.astype(BF16), do)
            ds = (p * (dp - jnp.sum(dp * p, axis=-1, keepdims=True))).astype(BF16)
            dk_ref[:, sl] += _dot_tn(ds, qb)
            dn = _dot(ds, k_ref[:, sl]) * SCALE
            dy, dgain = _norm_bwd(y, r, g_ref[...], dn)
            dcq_ref[:, sl] = dy.astype(BF16)
            dg = dg + dgain
        dg_ref[...] += dg

        xv = x_ref[...]
        dx, dgain = _norm_bwd(xv, _rms_r(xv), gn_ref[...], _dot_nt(dcq_ref[...], wq_ref[...]))
        out = dh2_ref[...] + dx
        dx_ref[...] = out
        dxb_ref[...] = out.astype(BF16)
        dgn_ref[...] += dgain

    row = lambda w_: pl.BlockSpec((tm, w_), lambda i: (i, 0))
    acc = lambda r_, w_: pl.BlockSpec((r_, w_), lambda i: (0, 0))
    return _pcall(
        body, name=name, grid=(s // tm,),
        in_specs=[row(d), row(d), row(d), row(X_W), _resident(ck_n.shape), _resident(cv.shape), _resident(cq_g.shape),
                  _resident((1, d)), _resident(wcq.shape), _resident(wco.shape)],
        out_specs=[row(X_W), acc(N_MEM, X_W), acc(N_MEM, X_W), acc(1, HEAD_DIM), row(d), row(d), acc(1, d)],
        out_shape=[SDS((s, X_W), BF16), SDS((N_MEM, X_W), F32), SDS((N_MEM, X_W), F32), SDS((1, HEAD_DIM), F32),
                   SDS((s, d), F32), SDS((s, d), BF16), SDS((1, d), F32)],
        scratch_shapes=[pltpu.VMEM((tm, X_W), BF16)],
        compiler_params=_cp())(dh2, dh2b, h1, cq_raw, ck_n, cv, cq_g, gain, wcq, wco)


def _mem_bwd(dck_n, dcv, ck_raw, memn, mem, wckv, mem_g, ck_g, *, name):
    def body(dk_ref, dv_ref, ckr_ref, mn_ref, m_ref, w_ref, mg_ref, kg_ref, dw_ref, dmg_ref, dkg_ref, dckv_s):
        dkg = jnp.zeros((1, HEAD_DIM), F32)
        for h in range(ATT_HEADS):
            sl = slice(h * HEAD_DIM, (h + 1) * HEAD_DIM)
            y = ckr_ref[:, sl]
            dy, dgain = _norm_bwd(y, _rms_r(y), kg_ref[...], dk_ref[:, sl])
            dckv_s[:, sl] = dy.astype(BF16)
            dkg = dkg + dgain
        dkg_ref[...] = dkg
        dckv_s[:, X_W:] = dv_ref[...].astype(BF16)
        dckv = dckv_s[...]
        dw_ref[...] = _dot_tn(mn_ref[...], dckv).astype(BF16)
        dmn = _dot_nt(dckv, w_ref[...])
        mv = m_ref[...]
        dmg_ref[...] = jnp.sum(dmn * mv * _rms_r(mv), axis=0, keepdims=True)

    return _pcall(
        body, name=name,
        out_shape=[SDS((D_MODEL, 2 * X_W), BF16), SDS((1, D_MODEL), F32), SDS((1, HEAD_DIM), F32)],
        scratch_shapes=[pltpu.VMEM((N_MEM, 2 * X_W), BF16)],
        compiler_params=_cp())(dck_n, dcv, ck_raw, memn, mem, wckv, mem_g, ck_g)


def _rope_tables(pos):
    inv_freq = ROPE_THETA ** (-jnp.arange(0, ROT_DIM, 2, dtype=F32) / ROT_DIM)
    ang = pos.astype(F32)[:, None] * inv_freq
    cos, sin = jnp.cos(ang), jnp.sin(ang)
    s = pos.shape[0]
    rest = HEAD_DIM - ROT_DIM
    cos_t = jnp.concatenate([cos, cos, jnp.ones((s, rest), F32)], axis=1)
    sin_t = jnp.concatenate([-sin, sin, jnp.zeros((s, rest), F32)], axis=1)
    return cos_t, sin_t


def _local_step(x, mem, pos, tgt, w_in, shards, sm):
    cos_t, sin_t = _rope_tables(pos)
    wb = {"w_in": w_in}

    pick = lambda *names: {a: shards[a] for a in names}
    proj, xn, got = _norm_linear(x, sm["mix_norm_g"], wb["w_in"], cn=768, out_dtype=F32, name="fwd_in_proj",
                                 shards=pick("w_out", "w_cq", "w_ckv", "w_co"))
    wb.update(got)
    q_rot, k_rot, v = _qk_prep(proj, cos_t, sin_t, sm["q_norm_g"], sm["k_norm_g"], name="fwd_qk_prep")
    mix, lse_all, got = _attn_fwd(q_rot, k_rot, v, pick("w_down", "w_gate_up"), name="fwd_attn")
    wb.update(got)
    mix = _pool_fwd(proj, mix, sm["pool_w"], sm["pool_scale"], name="fwd_pool")
    h1 = _linear_res(mix, wb["w_out"], x, cn=512, name="fwd_out_proj")
    memn, ck_raw, ck_n, cv = _mem_fwd(mem, sm["mem_norm_g"], wb["w_ckv"], sm["ck_norm_g"], name="fwd_mem")
    h2, hn, cq_raw, co = _cross_fwd(h1, sm["cross_norm_g"], wb["w_cq"], ck_n, cv, sm["cq_norm_g"], wb["w_co"],
                                    name="fwd_cross")
    gu, act, fn = _norm_linear_swiglu(h2, sm["ffn_norm_g"], wb["w_gate_up"], name="fwd_gate_up")
    dy, dyb, sq = _linear_res_loss(act, wb["w_down"], h2, tgt, name="fwd_down_loss")

    gw = {}
    gs = {}
    dgu = _swiglu_bwd(dyb, wb["w_down"], gu, name="bwd_swiglu")
    gw_down = _dw_tn(act, dyb, tkw=1024, tn=1024, name="bwd_dw_down")
    dh2, dh2b, gs["ffn_norm_g"], parts = _linear_nt_normbwd(dgu, wb["w_gate_up"], h2, dy, sm["ffn_norm_g"],
                                                            name="bwd_ffn_in", grads={"w_down": gw_down})
    gw["w_gate_up"] = _dw_tn(fn, dgu, tkw=1024, tn=1536, name="bwd_dw_gate_up")

    gw["w_co"] = _dw_tn(co, dh2b, tkw=512, tn=1024, name="bwd_dw_co")
    dcq, dck_n, dcv, gs["cq_norm_g"], dh1, dh1b, gs["cross_norm_g"] = _cross_bwd(
        dh2, dh2b, h1, cq_raw, ck_n, cv, sm["cq_norm_g"], sm["cross_norm_g"], wb["w_cq"], wb["w_co"], name="bwd_cross")
    gw["w_ckv"], gs["mem_norm_g"], gs["ck_norm_g"] = _mem_bwd(dck_n, dcv, ck_raw, memn, mem, wb["w_ckv"],
                                                             sm["mem_norm_g"], sm["ck_norm_g"], name="bwd_mem")
    gw["w_cq"] = _dw_tn(hn, dcq, tkw=1024, tn=512, name="bwd_dw_cq")

    dmix = _linear_nt(dh1b, wb["w_out"], cn=512, name="bwd_out_proj")
    gw["w_out"] = _dw_tn(mix, dh1b, tkw=1024, tn=1024, name="bwd_dw_out")
    du, gs["pool_w"], gs["pool_scale"] = _pool_bwd(proj, dmix, sm["pool_w"], sm["pool_scale"], name="bwd_pool")
    flat = lambda a: a.reshape(-1, a.shape[-1])
    early = [n for n in SMALL if n in gs]
    dq, dk, dv, got, spread = _attn_bwd(q_rot, k_rot, v, mix, dmix, lse_all, gw, [flat(gs[n]) for n in early] + [sq],
                                        name="bwd_attn")
    parts.update(got)
    small_parts = dict(zip(early + ["sq"], spread))
    dproj, gs["q_norm_g"], gs["k_norm_g"] = _qk_prep_bwd(proj, dq, dk, dv, du, cos_t, sin_t, sm["q_norm_g"],
                                                         sm["k_norm_g"], name="bwd_qk_prep")
    gw_in = _dw_tn(xn, dproj, tkw=1024, tn=1536, name="bwd_dw_in")
    dx, _, gs["mix_norm_g"], last = _linear_nt_normbwd(dproj, wb["w_in"], x, dh1, sm["mix_norm_g"],
                                                       name="bwd_in_proj", grads={"w_in": gw_in})
    parts.update(last)
    late = [n for n in SMALL if n not in small_parts]
    small_parts.update(zip(late, _exchange_small([flat(gs[n]) for n in late])))
    return dx, parts, small_parts


SHARDED = ("w_in", "w_out", "w_cq", "w_ckv", "w_co", "w_gate_up", "w_down")
SMALL = ("mix_norm_g", "q_norm_g", "k_norm_g", "pool_w", "pool_scale", "cross_norm_g", "mem_norm_g", "cq_norm_g",
         "ck_norm_g", "ffn_norm_g")
FULL = {
    "w_in": ((D_MODEL, IN_W), 1, IN_W // N_DEV),
    "w_out": ((D_MODEL, D_MODEL), 0, D_MODEL // N_DEV),
    "w_cq": ((D_MODEL, X_W), 0, D_MODEL // N_DEV),
    "w_ckv": ((D_MODEL, 2 * X_W), 0, D_MODEL // N_DEV),
    "w_co": ((X_W, D_MODEL), 1, D_MODEL // N_DEV),
    "w_gate_up": ((D_MODEL, 2 * FF_PAD), 1, FF_TILE),
    "w_down": ((FF_PAD, D_MODEL), 0, DOWN_SHARD),
}


def _shard_shape(name):
    shape, axis, width = FULL[name]
    return tuple(width if a == axis else n for a, n in enumerate(shape))


def _window(ref, name, dev):
    _, axis, width = FULL[name]
    if name == "w_down":
        start = pl.multiple_of((dev // 2) * FF_TILE + (dev % 2) * DOWN_SHARD, HALO)
    else:
        start = pl.multiple_of(dev * width, BLOCK)
    return ref.at[pl.ds(start, width), :] if axis == 0 else ref.at[:, pl.ds(start, width)]


def _mesh_place():
    x, y, c = lax.axis_index("x"), lax.axis_index("y"), lax.axis_index("c")
    return x, y, c, 4 * x + 2 * y + c


def _peer(x, y, c, k):
    px = 1 - x if k & 4 else x
    py = 1 - y if k & 2 else y
    pc = 1 - c if k & 1 else c
    return (px, py, pc), 4 * px + 2 * py + pc


HBM_SPEC = pl.BlockSpec(memory_space=pltpu.HBM)


def _comm_sems(n):
    return [pltpu.SemaphoreType.DMA((n, N_DEV - 1)), pltpu.SemaphoreType.DMA((n, N_DEV - 1)),
            pltpu.SemaphoreType.DMA((n,))]


DIRECT = (1, 2, 4, 6)
RELAYED = (2, 4, 6)


def _relay_step(steps):
    return max(1, (steps * 11) // 16) if steps > 1 else 0


def _gather_copies(names, ins, outs, send_sems, recv_sems, local_sems):
    x, y, c, me = _mesh_place()
    sibling, _ = _peer(x, y, c, 1)
    local, direct, relays, recv = [], [], {}, {}
    for wi, name in enumerate(names):
        def sems(k, to):
            return dict(send_sem=send_sems.at[wi, k - 1], recv_sem=recv_sems.at[wi, k - 1], device_id=to,
                        device_id_type=MESH)
        local.append(pltpu.make_async_copy(ins[wi], _window(outs[wi], name, me), local_sems.at[wi]))
        for k in range(1, N_DEV):
            peer, pidx = _peer(x, y, c, k)
            win = _window(outs[wi], name, pidx)
            recv[wi, k] = pltpu.make_async_remote_copy(src_ref=ins[wi], dst_ref=win, **sems(k, peer))
            if k in DIRECT:
                direct.append(pltpu.make_async_remote_copy(src_ref=ins[wi], dst_ref=_window(outs[wi], name, me),
                                                           **sems(k, peer)))
            if k in RELAYED:
                relays[wi, k] = pltpu.make_async_remote_copy(src_ref=win, dst_ref=win, **sems(k + 1, sibling))
    return local, direct, relays, recv


def _exchange_copies(names, ins, outs, send_sems, recv_sems, local_sems):
    x, y, c, me = _mesh_place()
    local, sent, recv = [], [], []
    for wi, name in enumerate(names):
        local.append(pltpu.make_async_copy(_window(ins[wi], name, me), outs[wi].at[0], local_sems.at[wi]))
        for k in range(1, N_DEV):
            peer, pidx = _peer(x, y, c, k)
            sems = dict(send_sem=send_sems.at[wi, k - 1], recv_sem=recv_sems.at[wi, k - 1], device_id=peer,
                        device_id_type=MESH)
            sent.append(pltpu.make_async_remote_copy(src_ref=_window(ins[wi], name, pidx), dst_ref=outs[wi].at[k], **sems))
            recv.append(pltpu.make_async_remote_copy(src_ref=_window(ins[wi], name, me), dst_ref=outs[wi].at[k], **sems))
    return local, sent, recv


def _down_pads(down_ref, zero_ref, zero_sems):
    return [pltpu.make_async_copy(zero_ref, down_ref.at[pl.ds(t * FF_TILE + FF_SHARD, FF_TILE - FF_SHARD), :],
                                  zero_sems.at[t]) for t in range(FF_PAD // FF_TILE)]


def _gather_scratch(names):
    if not names:
        return []
    pad = [pltpu.VMEM((FF_TILE - FF_SHARD, D_MODEL), BF16), pltpu.SemaphoreType.DMA((FF_PAD // FF_TILE,))]
    return _comm_sems(len(names)) + (pad if "w_down" in names else [])


def _gather_ops(names, ins, outs, scratch):
    local, direct, relays, recv = _gather_copies(names, ins, outs, *scratch[:3])
    pads = _down_pads(outs[names.index("w_down")], scratch[3], scratch[4]) if "w_down" in names else []

    def start():
        if pads:
            scratch[3][...] = jnp.zeros_like(scratch[3])
        for cp in local + direct + pads:
            cp.start()

    def relay(only=None):
        for (wi, k), cp in relays.items():
            if only is None or names[wi] in only:
                recv[wi, k].wait_recv()
                cp.start()

    def finish():
        for (wi, k), cp in recv.items():
            if k not in RELAYED:
                cp.wait_recv()
        for cp in direct + list(relays.values()):
            cp.wait_send()
        for cp in local + pads:
            cp.wait()

    return start, relay, finish


def _start(copies):
    local, sent, _ = copies
    for cp in local + sent:
        cp.start()


def _finish(copies):
    local, sent, recv = copies
    for cp in recv:
        cp.wait_recv()
    for cp in sent:
        cp.wait_send()
    for cp in local:
        cp.wait()


def _gather_weights(shards):
    names = tuple(shards)
    nw = len(names)

    def body(*refs):
        start, relay, finish = _gather_ops(names, refs[:nw], refs[nw:2 * nw], refs[2 * nw:])
        start()
        relay()
        finish()

    outs = _pcall(
        body, name="gather_weights",
        in_specs=[HBM_SPEC] * nw, out_specs=[HBM_SPEC] * nw,
        out_shape=[SDS(FULL[n][0], BF16) for n in names],
        scratch_shapes=_comm_sems(nw))(*[shards[n] for n in names])
    return dict(zip(names, outs))


def _broadcast_copies(ins, outs, send_sems, recv_sems, local_sems):
    x, y, c, me = _mesh_place()
    local, sent, recv = [], [], []
    for bi in range(len(ins)):
        local.append(pltpu.make_async_copy(ins[bi], outs[bi].at[me], local_sems.at[bi]))
        for k in range(1, N_DEV):
            peer, pidx = _peer(x, y, c, k)
            sems = dict(send_sem=send_sems.at[bi, k - 1], recv_sem=recv_sems.at[bi, k - 1], device_id=peer,
                        device_id_type=MESH)
            sent.append(pltpu.make_async_remote_copy(src_ref=ins[bi], dst_ref=outs[bi].at[me], **sems))
            recv.append(pltpu.make_async_remote_copy(src_ref=ins[bi], dst_ref=outs[bi].at[pidx], **sems))
    return local, sent, recv


def _exchange_small(blocks):
    nb = len(blocks)

    def body(*refs):
        copies = _broadcast_copies(refs[:nb], refs[nb:2 * nb], *refs[2 * nb:])
        _start(copies)
        _finish(copies)

    return _pcall(
        body, name="exchange_small", in_specs=[HBM_SPEC] * nb, out_specs=[HBM_SPEC] * nb,
        out_shape=[SDS((N_DEV,) + a.shape, F32) for a in blocks], scratch_shapes=_comm_sems(nb))(*blocks)


def _adam_math(g, w, m, v):
    m_new = ADAM_B1 * m + (1.0 - ADAM_B1) * g
    v_new = ADAM_B2 * v + (1.0 - ADAM_B2) * (g * g)
    m_hat = m_new / (1.0 - ADAM_B1 ** ADAM_STEP)
    v_hat = v_new / (1.0 - ADAM_B2 ** ADAM_STEP)
    return -ADAM_LR * (m_hat / (jnp.sqrt(v_hat) + ADAM_EPS) + ADAM_WD * w), m_new, v_new


def _adamw_small(parts, w, m, v, sq_parts, *, name):
    n = len(parts)

    def body(*refs):
        p_refs, w_refs, m_refs, v_refs = refs[:n], refs[n:2 * n], refs[2 * n:3 * n], refs[3 * n:4 * n]
        sq_ref, outs = refs[4 * n], refs[4 * n + 1:]
        for i in range(n):
            g = p_refs[i][0]
            for k in range(1, N_DEV):
                g = g + p_refs[i][k]
            delta, m_new, v_new = _adam_math(g, w_refs[i][...], m_refs[i][...], v_refs[i][...])
            outs[4 * i][...] = g
            outs[4 * i + 1][...] = delta
            outs[4 * i + 2][...] = m_new
            outs[4 * i + 3][...] = v_new
        tot = sq_ref[0]
        for k in range(1, N_DEV):
            tot = tot + sq_ref[k]
        outs[4 * n][...] = (0.5 / D_MODEL) * jnp.sum(tot, axis=1, keepdims=True)

    out_shape = [SDS(a.shape, F32) for a in w for _ in range(4)] + [SDS((1, 1), F32)]
    outs = _pcall(body, name=name, out_shape=out_shape, compiler_params=_cp())(*parts, *w, *m, *v, sq_parts)
    return [outs[4 * i:4 * i + 4] for i in range(n)], outs[4 * n][0, 0]


def _adamw(parts, w, m, v, *, name):
    r, c = w.shape
    pc = parts.shape[2]
    tr = r
    for cand in (256, 128, 88):
        if r % cand == 0:
            tr = cand
            break

    def body(p_ref, w_ref, m_ref, v_ref, g_ref, d_ref, mo_ref, vo_ref):
        g = p_ref[0, :, :c].astype(F32)
        for k in range(1, N_DEV):
            g = g + p_ref[k, :, :c].astype(F32)
        g_ref[...] = g
        d_ref[...], mo_ref[...], vo_ref[...] = _adam_math(g, w_ref[...], m_ref[...], v_ref[...])

    row = pl.BlockSpec((tr, c), lambda i: (i, 0))
    return _pcall(
        body, name=name, grid=(r // tr,),
        in_specs=[pl.BlockSpec((N_DEV, tr, pc), lambda i: (0, i, 0)), row, row, row],
        out_specs=[row] * 4, out_shape=[SDS((r, c), F32)] * 4, compiler_params=_cp())(parts, w, m, v)


def _pad_cols(a, width):
    return jnp.pad(a, ((0, 0), (0, width - a.shape[1])))


def kernel(x, mem, positions, mix_norm_g, w_in, q_norm_g, k_norm_g, pool_w, pool_scale, w_out, cross_norm_g, mem_norm_g, w_cq, w_ckv, cq_norm_g, ck_norm_g, w_co, ffn_norm_g, w_gate_up, w_down, loss_target, m_mix_norm_g, m_w_in, m_q_norm_g, m_k_norm_g, m_pool_w, m_pool_scale, m_w_out, m_cross_norm_g, m_mem_norm_g, m_w_cq, m_w_ckv, m_cq_norm_g, m_ck_norm_g, m_w_co, m_ffn_norm_g, m_w_gate_up, m_w_down, v_mix_norm_g, v_w_in, v_q_norm_g, v_k_norm_g, v_pool_w, v_pool_scale, v_w_out, v_cross_norm_g, v_mem_norm_g, v_w_cq, v_w_ckv, v_cq_norm_g, v_ck_norm_g, v_w_co, v_ffn_norm_g, v_w_gate_up, v_w_down):
    given = dict(locals())
    w_f32 = {n: given[n][0] for n in SHARDED + SMALL}
    m_f32 = {n: given["m_" + n][0] for n in SHARDED + SMALL}
    v_f32 = {n: given["v_" + n][0] for n in SHARDED + SMALL}
    shards = {n: w_f32[n].astype(BF16) for n in SHARDED}
    shards["w_gate_up"] = _pad_cols(shards["w_gate_up"], FF_TILE)
    w_in_full = _gather_weights({"w_in": shards.pop("w_in")})["w_in"]
    sm_rows = {n: (w_f32[n] if w_f32[n].ndim == 3 else w_f32[n].reshape(1, -1)) for n in SMALL}
    dx, parts, small = _local_step(x[0], mem[0], positions[0], loss_target[0], w_in_full, shards, sm_rows)

    flat = lambda a: a.reshape(-1, a.shape[-1])
    small_res, loss = _adamw_small([small[n] for n in SMALL], [flat(sm_rows[n]) for n in SMALL],
                                   [flat(m_f32[n].reshape(sm_rows[n].shape)) for n in SMALL],
                                   [flat(v_f32[n].reshape(sm_rows[n].shape)) for n in SMALL], small["sq"],
                                   name="adamw_small")
    res = {n: [a.reshape(w_f32[n].shape) for a in small_res[i]] for i, n in enumerate(SMALL)}
    for n in SHARDED:
        res[n] = _adamw(parts[n], w_f32[n], m_f32[n], v_f32[n], name="adamw_" + n)
    order = ("mix_norm_g", "w_in", "q_norm_g", "k_norm_g", "pool_w", "pool_scale", "w_out", "cross_norm_g",
             "mem_norm_g", "w_cq", "w_ckv", "cq_norm_g", "ck_norm_g", "w_co", "ffn_norm_g", "w_gate_up", "w_down")
    outs = [loss, dx[None]]
    for which in range(4):
        outs += [res[n][which][None] for n in order]
    return tuple(outs)
```

```python
import functools

import jax
import jax.numpy as jnp
from jax import lax
from jax.experimental import pallas as pl
from jax.experimental.pallas import tpu as pltpu

F32 = jnp.float32
BF16 = jnp.bfloat16
SDS = jax.ShapeDtypeStruct

D_MODEL = 1024
HEAD_DIM = 128
N_GROUPS = 3
DILATIONS = (1, 4, 16)
ATT_HEADS = 4
Q_W = 1536
KV_W = 512
POOL_W = 512
POOL_WINDOWS = (2, 4, 8, 16)
IN_W = 3072
X_W = 512
N_MEM = 256
D_FF = 2816
FF_TILE = 768
FF_SHARD = 704
FF_PAD = 4 * FF_TILE
DOWN_SHARD = 352
ROT_DIM = 32
ROT_HALF = 16
ROPE_THETA = 500000.0
EPS = 1e-6
NEG_INF = -1e30
SCALE = HEAD_DIM ** -0.5
BLOCK = 128
HALO = 16

ADAM_LR = 0.001
ADAM_B1 = 0.9
ADAM_B2 = 0.999
ADAM_EPS = 1e-08
ADAM_WD = 0.01
ADAM_STEP = 10

N_DEV = 8
VMEM_LIMIT_BYTES = 56 * 1024 * 1024
MESH = pl.DeviceIdType.MESH


def _pcall(body, **kw):
    return pl.pallas_call(body, **kw)


def _cp():
    return pltpu.CompilerParams(vmem_limit_bytes=VMEM_LIMIT_BYTES)


def _dot(a, b):
    return lax.dot_general(a, b, (((1,), (0,)), ((), ())), preferred_element_type=F32)


def _dot_nt(a, b):
    return lax.dot_general(a, b, (((1,), (1,)), ((), ())), preferred_element_type=F32)


def _dot_tn(a, b):
    return lax.dot_general(a, b, (((0,), (0,)), ((), ())), preferred_element_type=F32)


def _rows(s):
    return min(512, s)


def _rms_r(x):
    return lax.rsqrt(jnp.mean(x * x, axis=-1, keepdims=True) + EPS)


def _norm_bwd(x, r, gain, dxn):
    z = dxn * gain
    dx = r * z - x * (r * r * r * jnp.mean(z * x, axis=-1, keepdims=True))
    dgain = jnp.sum(dxn * x * r, axis=0, keepdims=True)
    return dx, dgain


def _split_bf16(t):
    hi = t.astype(BF16)
    return hi, (t - hi.astype(F32)).astype(BF16)


def _lane_sums(t, ones):
    hi, lo = _split_bf16(t)
    return _dot(hi, ones) + _dot(lo, ones)


def _head_r(y, ones):
    return lax.rsqrt(_lane_sums(y * y, ones) * (1.0 / HEAD_DIM) + EPS)


def _head_norm_bwd(y, r, gain, dn, ones):
    z = dn * gain
    dy = r * z - y * (r * r * r * (_dot((z * y).astype(BF16), ones) * (1.0 / HEAD_DIM)))
    return dy, jnp.sum(dn * y * r, axis=0, keepdims=True)


def _swap_matrix():
    src = lax.broadcasted_iota(jnp.int32, (HEAD_DIM, HEAD_DIM), 0)
    dst = lax.broadcasted_iota(jnp.int32, (HEAD_DIM, HEAD_DIM), 1)
    hit = ((dst < ROT_HALF) & (src == dst + ROT_HALF)) | ((dst >= ROT_HALF) & (dst < ROT_DIM) & (src == dst - ROT_HALF))
    return jnp.where(hit, 1.0, 0.0).astype(BF16)


def _partner(t, swap):
    hi, lo = _split_bf16(t)
    return _dot(hi, swap) + _dot(lo, swap)


def _rope(n, cos_t, sin_t, swap):
    return n * cos_t + _partner(n, swap) * sin_t


def _rope_bwd(d, cos_t, sin_t, swap):
    return d * cos_t + _dot((d * sin_t).astype(BF16), swap)


def _resident(shape):
    return pl.BlockSpec(shape, lambda i: (0,) * len(shape), pipeline_mode=pl.Buffered(1))


def _chunks(n, cn):
    return [slice(j * cn, (j + 1) * cn) for j in range(n // cn)]


def _norm_linear(x, gain, w, *, cn, out_dtype, name, shards=None):
    s, k = x.shape
    n = w.shape[1]
    tm = _rows(s)
    names = tuple(shards or ())
    nr = len(names)

    def body(*refs):
        x_ref, g_ref, w_ref = refs[:3]
        y_ref, xn_ref = refs[3 + nr:5 + nr]
        if nr:
            start, relay, finish = _gather_ops(names, refs[3:3 + nr], refs[5 + nr:5 + 2 * nr], refs[5 + 2 * nr:])
            pl.when(pl.program_id(0) == 0)(start)
            pl.when(pl.program_id(0) == _relay_step(s // tm))(relay)
        xv = x_ref[...]
        xn_ref[...] = (xv * _rms_r(xv) * g_ref[...]).astype(BF16)
        for c in _chunks(n, cn):
            y_ref[:, c] = _dot(xn_ref[...], w_ref[:, c]).astype(out_dtype)
        if nr:
            pl.when(pl.program_id(0) == s // tm - 1)(finish)

    row = lambda w_: pl.BlockSpec((tm, w_), lambda i: (i, 0))
    outs = _pcall(
        body, name=name, grid=(s // tm,),
        in_specs=[row(k), _resident((1, k)), _resident((k, n))] + [HBM_SPEC] * nr,
        out_specs=[row(n), row(k)] + [HBM_SPEC] * nr,
        out_shape=[SDS((s, n), out_dtype), SDS((s, k), BF16)] + [SDS(FULL[a][0], BF16) for a in names],
        scratch_shapes=_gather_scratch(names),
        compiler_params=_cp())(x, gain, w, *[shards[a] for a in names])
    return (outs[0], outs[1], dict(zip(names, outs[2:]))) if nr else tuple(outs)


def _norm_linear_swiglu(x, gain, wgu, *, name):
    s, k = x.shape
    tm = _rows(s)

    def body(x_ref, g_ref, w_ref, gu_ref, a_ref, xn_ref):
        xv = x_ref[...]
        xn_ref[...] = (xv * _rms_r(xv) * g_ref[...]).astype(BF16)
        for c in _chunks(FF_PAD, FF_TILE):
            g = _dot(xn_ref[...], w_ref[:, c])
            u = _dot(xn_ref[...], w_ref[:, slice(FF_PAD + c.start, FF_PAD + c.stop)])
            a_ref[:, c] = (g * jax.nn.sigmoid(g) * u).astype(BF16)
            gu_ref[0, :, c] = g.astype(BF16)
            gu_ref[1, :, c] = u.astype(BF16)

    row = lambda w_: pl.BlockSpec((tm, w_), lambda i: (i, 0))
    return _pcall(
        body, name=name, grid=(s // tm,),
        in_specs=[row(k), _resident((1, k)), _resident((k, 2 * FF_PAD))],
        out_specs=[pl.BlockSpec((2, tm, FF_PAD), lambda i: (0, i, 0)), row(FF_PAD), row(k)],
        out_shape=[SDS((2, s, FF_PAD), BF16), SDS((s, FF_PAD), BF16), SDS((s, k), BF16)],
        compiler_params=_cp())(x, gain, wgu)


def _linear_res(a, w, res, *, cn, name):
    s, k = a.shape
    n = w.shape[1]
    tm = min(2 * _rows(s), s)

    def body(a_ref, w_ref, r_ref, y_ref):
        for c in _chunks(n, cn):
            y_ref[:, c] = r_ref[:, c] + _dot(a_ref[...], w_ref[:, c])

    row = lambda w_: pl.BlockSpec((tm, w_), lambda i: (i, 0))
    return _pcall(
        body, name=name, grid=(s // tm,),
        in_specs=[row(k), _resident((k, n)), row(n)], out_specs=row(n),
        out_shape=SDS((s, n), F32), compiler_params=_cp())(a, w, res)


def _linear_res_loss(a, w, res, tgt, *, name):
    s, k = a.shape
    n = w.shape[1]
    tm = min(2 * _rows(s), s)

    def body(a_ref, w_ref, r_ref, t_ref, dy_ref, dyb_ref, sq_ref):
        e = r_ref[...] + _dot(a_ref[...], w_ref[...]) - t_ref[...]
        dy = e * (1.0 / n)
        dy_ref[...] = dy
        dyb_ref[...] = dy.astype(BF16)

        @pl.when(pl.program_id(0) == 0)
        def _():
            sq_ref[...] = jnp.zeros_like(sq_ref)
        sq_ref[...] += jnp.sum(e * e, axis=0, keepdims=True)

    row = lambda w_: pl.BlockSpec((tm, w_), lambda i: (i, 0))
    return _pcall(
        body, name=name, grid=(s // tm,),
        in_specs=[row(k), _resident((k, n)), row(n), row(n)],
        out_specs=[row(n), row(n), pl.BlockSpec((1, n), lambda i: (0, 0))],
        out_shape=[SDS((s, n), F32), SDS((s, n), BF16), SDS((1, n), F32)],
        compiler_params=_cp())(a, w, res, tgt)


def _linear_nt(g, w, *, cn, name):
    s, k = g.shape
    n = w.shape[0]
    tm = min(2 * _rows(s), s)

    def body(g_ref, w_ref, y_ref):
        for c in _chunks(n, cn):
            y_ref[:, c] = _dot_nt(g_ref[...], w_ref[c, :]).astype(BF16)

    row = lambda w_: pl.BlockSpec((tm, w_), lambda i: (i, 0))
    return _pcall(
        body, name=name, grid=(s // tm,),
        in_specs=[row(k), _resident((n, k))], out_specs=row(n),
        out_shape=SDS((s, n), BF16), compiler_params=_cp())(g, w)


def _swiglu_bwd(dyb, wd, gu, *, name):
    s, n = dyb.shape
    tm = _rows(s)

    def body(dy_ref, wd_ref, gu_ref, dgu_ref):
        for c in _chunks(FF_PAD, FF_TILE):
            da = _dot_nt(dy_ref[...], wd_ref[c, :])
            g = gu_ref[0, :, c].astype(F32)
            u = gu_ref[1, :, c].astype(F32)
            sg = jax.nn.sigmoid(g)
            dgu_ref[0, :, c] = (da * u * (sg * (1.0 + g * (1.0 - sg)))).astype(BF16)
            dgu_ref[1, :, c] = (da * (g * sg)).astype(BF16)

    half = pl.BlockSpec((2, tm, FF_PAD), lambda i: (0, i, 0))
    return _pcall(
        body, name=name, grid=(s // tm,),
        in_specs=[pl.BlockSpec((tm, n), lambda i: (i, 0)), _resident((FF_PAD, n)), half],
        out_specs=half, out_shape=SDS((2, s, FF_PAD), BF16), compiler_params=_cp())(dyb, wd, gu)


def _linear_nt_normbwd(g, w, x, dres, gain, *, name, grads=None):
    d, k = w.shape
    s = x.shape[0]
    tm = _rows(s)
    names = tuple(grads or ())
    nr = len(names)

    def body(*refs):
        g_ref, w_ref, x_ref, dr_ref, gn_ref = refs[:5]
        dx_ref, dxb_ref, dg_ref = refs[5 + nr:8 + nr]
        if nr:
            copies = _exchange_copies(names, refs[5:5 + nr], refs[8 + nr:8 + 2 * nr], *refs[8 + 2 * nr:])

        @pl.when(pl.program_id(0) == 0)
        def _():
            dg_ref[...] = jnp.zeros_like(dg_ref)
            if nr:
                _start(copies)

        if g.ndim == 3:
            dxn = _dot_nt(g_ref[0], w_ref[:, :k // 2]) + _dot_nt(g_ref[1], w_ref[:, k // 2:])
        else:
            dxn = _dot_nt(g_ref[...], w_ref[...])
        xv = x_ref[...]
        dx, dgain = _norm_bwd(xv, _rms_r(xv), gn_ref[...], dxn)
        out = dr_ref[...] + dx
        dx_ref[...] = out
        dxb_ref[...] = out.astype(BF16)
        dg_ref[...] += dgain

        if nr:
            @pl.when(pl.program_id(0) == s // tm - 1)
            def _():
                _finish(copies)

    row = pl.BlockSpec((tm, d), lambda i: (i, 0))
    g_spec = (pl.BlockSpec((2, tm, k // 2), lambda i: (0, i, 0)) if g.ndim == 3
              else pl.BlockSpec((tm, k), lambda i: (i, 0)))
    outs = _pcall(
        body, name=name, grid=(s // tm,),
        in_specs=[g_spec, _resident((d, k)), row, row, _resident((1, d))] + [HBM_SPEC] * nr,
        out_specs=[row, row, pl.BlockSpec((1, d), lambda i: (0, 0))] + [HBM_SPEC] * nr,
        out_shape=[SDS((s, d), F32), SDS((s, d), BF16), SDS((1, d), F32)]
        + [SDS((N_DEV,) + _shard_shape(n), BF16) for n in names],
        scratch_shapes=_comm_sems(nr) if nr else [],
        compiler_params=_cp())(g, w, x, dres, gain, *[grads[n] for n in names])
    return (outs[0], outs[1], outs[2], dict(zip(names, outs[3:]))) if nr else tuple(outs)


def _dw_tn(x, g, *, tkw, tn, name):
    s, kw = x.shape
    halves = g.ndim == 3
    n = 2 * g.shape[2] if halves else g.shape[1]
    ts = min(2048, s)
    ns = s // ts
    per_half = n // 2 // tn

    def body(x_ref, g_ref, o_ref, acc_ref):
        ss = pl.program_id(2)

        @pl.when(ss == 0)
        def _():
            acc_ref[...] = jnp.zeros_like(acc_ref)

        acc_ref[...] += _dot_tn(x_ref[...], g_ref[...])

        @pl.when(ss == ns - 1)
        def _():
            o_ref[...] = acc_ref[...].astype(BF16)

    g_spec = (pl.BlockSpec((None, ts, tn), lambda a, b, ss: (b // per_half, ss, b % per_half)) if halves
              else pl.BlockSpec((ts, tn), lambda a, b, ss: (ss, b)))
    return _pcall(
        body, name=name, grid=(kw // tkw, n // tn, ns),
        in_specs=[pl.BlockSpec((ts, tkw), lambda a, b, ss: (ss, a)), g_spec],
        out_specs=pl.BlockSpec((tkw, tn), lambda a, b, ss: (a, b)),
        out_shape=SDS((kw, n), BF16),
        scratch_shapes=[pltpu.VMEM((tkw, tn), F32)], compiler_params=_cp())(x, g)


def _qk_prep(proj, cos_t, sin_t, qg, kg, *, name):
    s = proj.shape[0]
    tm = _rows(s)
    nqh = Q_W // HEAD_DIM

    def body(q_ref, k_ref, v_ref, c_ref, s_ref, qg_ref, kg_ref, qo_ref, ko_ref, vo_ref):
        c, sn = c_ref[...], s_ref[...]
        ones, swap = jnp.ones((HEAD_DIM, HEAD_DIM), BF16), _swap_matrix()
        for h in range(nqh):
            sl = slice(h * HEAD_DIM, (h + 1) * HEAD_DIM)
            y = q_ref[:, sl].astype(F32)
            qo_ref[:, sl] = (_rope(y * _head_r(y, ones) * qg_ref[...], c, sn, swap) * SCALE).astype(BF16)
        for h in range(ATT_HEADS):
            sl = slice(h * HEAD_DIM, (h + 1) * HEAD_DIM)
            y = k_ref[:, sl].astype(F32)
            ko_ref[:, sl] = _rope(y * _head_r(y, ones) * kg_ref[...], c, sn, swap).astype(BF16)
        vo_ref[...] = v_ref[...].astype(BF16)

    row = lambda w, j: pl.BlockSpec((tm, w), lambda i: (i, j))
    one = pl.BlockSpec((1, HEAD_DIM), lambda i: (0, 0))
    return _pcall(
        body, name=name, grid=(s // tm,),
        in_specs=[row(Q_W, 0), row(KV_W, 3), row(KV_W, 4), row(HEAD_DIM, 0), row(HEAD_DIM, 0), one, one],
        out_specs=[row(Q_W, 0), row(KV_W, 0), row(KV_W, 0)],
        out_shape=[SDS((s, Q_W), BF16), SDS((s, KV_W), BF16), SDS((s, KV_W), BF16)],
        compiler_params=_cp())(proj, proj, proj, cos_t, sin_t, qg, kg)


def _qk_prep_bwd(proj, dq, dk, dv, du, cos_t, sin_t, qg, kg, *, name):
    s = proj.shape[0]
    tm = _rows(s)

    def body(q_ref, k_ref, dq_ref, dk_ref, dv_ref, du_ref, c_ref, s_ref, qg_ref, kg_ref, dp_ref, dqg_ref, dkg_ref):
        c, sn = c_ref[...], s_ref[...]

        @pl.when(pl.program_id(0) == 0)
        def _():
            dqg_ref[...] = jnp.zeros_like(dqg_ref)
            dkg_ref[...] = jnp.zeros_like(dkg_ref)

        ones, swap = jnp.ones((HEAD_DIM, HEAD_DIM), BF16), _swap_matrix()
        dqg = jnp.zeros((1, HEAD_DIM), F32)
        for g in range(N_GROUPS):
            for h in range(ATT_HEADS):
                sl = slice(h * HEAD_DIM, (h + 1) * HEAD_DIM)
                col = slice(g * KV_W + h * HEAD_DIM, g * KV_W + (h + 1) * HEAD_DIM)
                y = q_ref[:, col].astype(F32)
                dn = _rope_bwd(dq_ref[g, :, sl].astype(F32) * SCALE, c, sn, swap)
                dy, dgain = _head_norm_bwd(y, _head_r(y, ones), qg_ref[...], dn, ones)
                dp_ref[:, col] = dy.astype(BF16)
                dqg = dqg + dgain
        dqg_ref[...] += dqg

        dkg = jnp.zeros((1, HEAD_DIM), F32)
        for h in range(ATT_HEADS):
            sl = slice(h * HEAD_DIM, (h + 1) * HEAD_DIM)
            y = k_ref[:, sl].astype(F32)
            dn = _rope_bwd(dk_ref[:, sl], c, sn, swap)
            dy, dgain = _head_norm_bwd(y, _head_r(y, ones), kg_ref[...], dn, ones)
            dp_ref[:, Q_W + h * HEAD_DIM:Q_W + (h + 1) * HEAD_DIM] = dy.astype(BF16)
            dkg = dkg + dgain
        dkg_ref[...] += dkg

        dp_ref[:, Q_W + KV_W:Q_W + 2 * KV_W] = dv_ref[...].astype(BF16)
        dp_ref[:, Q_W + 2 * KV_W:] = du_ref[...].astype(BF16)

    row = lambda w, j: pl.BlockSpec((tm, w), lambda i: (i, j))
    one = pl.BlockSpec((1, HEAD_DIM), lambda i: (0, 0))
    return _pcall(
        body, name=name, grid=(s // tm,),
        in_specs=[row(Q_W, 0), row(KV_W, 3), pl.BlockSpec((N_GROUPS, tm, KV_W), lambda i: (0, i, 0))]
        + [row(KV_W, 0)] * 3 + [row(HEAD_DIM, 0), row(HEAD_DIM, 0), one, one],
        out_specs=[row(IN_W, 0), one, one],
        out_shape=[SDS((s, IN_W), BF16), SDS((1, HEAD_DIM), F32), SDS((1, HEAD_DIM), F32)],
        compiler_params=_cp())(proj, proj, dq, dk, dv, du, cos_t, sin_t, qg, kg)


ATT_ROWS = 16 * BLOCK


def _sub(ref, start, d, size=BLOCK):
    return ref[pl.ds(start, size, stride=d), :] if d > 1 else ref[pl.ds(start, size), :]


def _sub_set(ref, start, d, val):
    if d > 1:
        ref[pl.ds(start, BLOCK, stride=d), :] = val
    else:
        ref[pl.ds(start, BLOCK), :] = val


def _band_masks():
    row = lax.broadcasted_iota(jnp.int32, (BLOCK, BLOCK), 0)
    col = lax.broadcasted_iota(jnp.int32, (BLOCK, BLOCK), 1)
    return col <= row, col >= row


def _eye(n=BLOCK):
    row = lax.broadcasted_iota(jnp.int32, (n, n), 0)
    col = lax.broadcasted_iota(jnp.int32, (n, n), 1)
    return jnp.where(row == col, 1.0, 0.0).astype(BF16)


def _attn_fwd(q_rot, k_rot, v, shards, *, name):
    s = q_rot.shape[0]
    rr = ATT_ROWS
    nblk = s // rr
    names = tuple(shards)
    nr = len(names)

    def body(*refs):
        q0, q1, q2, kp, kc, vp, vc = refs[:7]
        mix_ref, lse_ref = refs[7 + nr:9 + nr]
        qs, ks, vs, os_, ls = refs[9 + 2 * nr:14 + 2 * nr]
        h, n = pl.program_id(0), pl.program_id(1)
        start, relay, finish = _gather_ops(names, refs[7:7 + nr], refs[9 + nr:9 + 2 * nr], refs[14 + 2 * nr:])
        pl.when((h == 0) & (n == 0))(start)
        steps = ATT_HEADS * nblk
        early, late = max(1, (steps * 6) // 16), max(1, (steps * 14) // 16)
        pl.when(h * nblk + n == early)(functools.partial(relay, names[:1]))
        pl.when(h * nblk + n == late)(functools.partial(relay, names[1:]))
        for g, q_ref in enumerate((q0, q1, q2)):
            qs[g] = q_ref[...].astype(F32)
        ks[:rr] = kp[...].astype(F32)
        ks[rr:] = kc[...].astype(F32)
        vs[:rr] = vp[...].astype(F32)
        vs[rr:] = vc[...].astype(F32)
        m_cur, m_band = _band_masks()
        mask_in = jnp.concatenate([m_band, m_cur], axis=1)
        mask_first = jnp.concatenate([m_band & (n > 0), m_cur], axis=1)
        ones = jnp.ones((2 * BLOCK, HEAD_DIM), BF16)
        pieces = [(g, d, j * BLOCK * d + r, j) for g, d in enumerate(DILATIONS) for r in range(d)
                  for j in range(rr // (BLOCK * d))]

        def scores(piece):
            g, d, base, j = piece
            q = _sub(qs.at[g], base, d).astype(BF16)
            k2 = _sub(ks, rr + base - BLOCK * d, d, 2 * BLOCK).astype(BF16)
            return jnp.where(mask_first if j == 0 else mask_in, _dot_nt(q, k2), NEG_INF)

        sc = scores(pieces[0])
        for i, (g, d, base, j) in enumerate(pieces):
            cur = sc
            if i + 1 < len(pieces):
                sc = scores(pieces[i + 1])
            m = jnp.max(cur, axis=-1, keepdims=True)
            p = jnp.exp(cur - m).astype(BF16)
            v2 = _sub(vs, rr + base - BLOCK * d, d, 2 * BLOCK).astype(BF16)
            acc_l = _dot(p, jnp.concatenate([v2, ones], axis=1))
            l = acc_l[:, HEAD_DIM:]
            _sub_set(os_.at[g], base, d, acc_l[:, :HEAD_DIM] / l)
            _sub_set(ls.at[g], base, d, m + jnp.log(l))
        for c in _chunks(rr, 2 * BLOCK):
            a, b, cc = ls[0, c, :], ls[1, c, :], ls[2, c, :]
            m = jnp.maximum(jnp.maximum(a, b), cc)
            wa, wb, wc = jnp.exp(a - m), jnp.exp(b - m), jnp.exp(cc - m)
            den = wa + wb + wc
            mix_ref[c, :] = ((wa * os_[0, c, :] + wb * os_[1, c, :] + wc * os_[2, c, :]) / den).astype(BF16)
            lse_ref[c, :] = m + jnp.log(den)

        pl.when((h == ATT_HEADS - 1) & (n == nblk - 1))(finish)

    blk = lambda f: pl.BlockSpec((rr, HEAD_DIM), f)
    prv = lambda n: jnp.maximum(n - 1, 0)
    outs = _pcall(
        body, name=name, grid=(ATT_HEADS, nblk),
        in_specs=[blk(lambda h, n, g=g: (n, g * ATT_HEADS + h)) for g in range(N_GROUPS)]
        + [blk(lambda h, n: (prv(n), h)), blk(lambda h, n: (n, h))] * 2 + [HBM_SPEC] * nr,
        out_specs=[blk(lambda h, n: (n, h)), blk(lambda h, n: (n, h))] + [HBM_SPEC] * nr,
        out_shape=[SDS((s, KV_W + POOL_W), BF16), SDS((s, KV_W), F32)] + [SDS(FULL[w][0], BF16) for w in names],
        scratch_shapes=[pltpu.VMEM((N_GROUPS, rr, HEAD_DIM), F32), pltpu.VMEM((2 * rr, HEAD_DIM), F32),
                        pltpu.VMEM((2 * rr, HEAD_DIM), F32), pltpu.VMEM((N_GROUPS, rr, HEAD_DIM), F32),
                        pltpu.VMEM((N_GROUPS, rr, HEAD_DIM), F32)] + _gather_scratch(names),
        compiler_params=_cp())(q_rot, q_rot, q_rot, k_rot, k_rot, v, v, *[shards[w] for w in names])
    return outs[0], outs[1], dict(zip(names, outs[2:]))


def _attn_bwd(q_rot, k_rot, v, mix, dmix, lse, grads, blocks, *, name):
    s = q_rot.shape[0]
    rr = ATT_ROWS
    nblk = s // rr
    names = tuple(grads)
    nr, nb = len(names), len(blocks)
    nx = nr + nb

    def body(*refs):
        q0, q1, q2, qx0, qx1, qx2, kp, kc, vp, vc, do_c, do_x, o_c, o_x, l_c, l_x = refs[:16]
        dq_ref, dk_ref, dv_ref = refs[16 + nx:19 + nx]
        qs, ks, vs, dos, lss, dls, dqs, dks, dvs = refs[19 + 2 * nx:28 + 2 * nx]
        h, n = pl.program_id(0), pl.program_id(1)
        copies = _exchange_copies(names, refs[16:16 + nr], refs[19 + nx:19 + nx + nr], *refs[28 + 2 * nx:31 + 2 * nx])
        spread = _broadcast_copies(refs[16 + nr:16 + nx], refs[19 + nx + nr:19 + 2 * nx], *refs[31 + 2 * nx:])

        @pl.when((h == 0) & (n == 0))
        def _():
            _start(copies)
            _start(spread)

        for g, (qc_ref, qx_ref) in enumerate(((q0, qx0), (q1, qx1), (q2, qx2))):
            qs[g, :rr] = qc_ref[...].astype(F32)
            qs[g, rr:] = qx_ref[...].astype(F32)
        ks[:rr] = kp[...].astype(F32)
        ks[rr:] = kc[...].astype(F32)
        vs[:rr] = vp[...].astype(F32)
        vs[rr:] = vc[...].astype(F32)
        lss[:rr] = l_c[...]
        lss[rr:] = l_x[...]
        for half, (d_ref, o_ref) in enumerate(((do_c, o_c), (do_x, o_x))):
            for c in _chunks(rr, 2 * BLOCK):
                cs = slice(half * rr + c.start, half * rr + c.stop)
                dof = d_ref[c, :].astype(F32)
                dos[cs, :] = dof
                dls[cs, :] = jnp.broadcast_to(jnp.sum(dof * o_ref[c, :].astype(F32), axis=-1, keepdims=True),
                                              (2 * BLOCK, HEAD_DIM))
        m_cur, m_band = _band_masks()
        mask_in = jnp.concatenate([m_cur, m_band], axis=0)
        mask_last = jnp.concatenate([m_cur, m_band & (n + 1 < nblk)], axis=0)
        m_first = m_band & (n > 0)
        eye = _eye()
        pieces = [(g, d, j * BLOCK * d + r, j, rr // (BLOCK * d)) for g, d in enumerate(DILATIONS) for r in range(d)
                  for j in range(rr // (BLOCK * d))]

        def front(piece):
            g, d, base, _, _ = piece
            q2 = _sub(qs.at[g], base, d, 2 * BLOCK).astype(BF16)
            do2 = _sub(dos, base, d, 2 * BLOCK).astype(BF16)
            k = _sub(ks, rr + base, d).astype(BF16)
            vv = _sub(vs, rr + base, d).astype(BF16)
            return q2, do2, k, _dot_nt(q2, k), _dot_nt(do2, vv)

        def middle(piece, fr, dq_acc):
            g, d, base, j, nsub = piece
            q2, do2, k, s2, dp2 = fr
            if j == 0:
                kp_ = _sub(ks, rr + base - BLOCK * d, d).astype(BF16)
                p0 = jnp.where(m_first, jnp.exp(_dot_nt(q2[:BLOCK], kp_) - _sub(lss, base, d)), 0.0)
                ds0 = p0 * (_dot_nt(do2[:BLOCK], _sub(vs, rr + base - BLOCK * d, d).astype(BF16)) - _sub(dls, base, d))
                dq_acc = _dot(ds0.astype(BF16), kp_)
            p2 = jnp.where(mask_last if j + 1 == nsub else mask_in,
                           jnp.exp(s2 - _sub(lss, base, d, 2 * BLOCK)), 0.0)
            ds2 = (p2 * (dp2 - _sub(dls, base, d, 2 * BLOCK))).astype(BF16)
            dq2 = _dot(ds2, k)
            return dq2[BLOCK:], (_dot_nt(eye, ds2), _dot_nt(eye, p2.astype(BF16)), q2, do2, dq_acc + dq2[:BLOCK])

        def back(piece, tr):
            g, d, base, _, _ = piece
            ds_t, p_t, q2, do2, dq = tr
            _sub_set(dqs.at[g], base, d, dq)
            dk, dv = _dot(ds_t.astype(BF16), q2), _dot(p_t.astype(BF16), do2)
            if g == 0:
                _sub_set(dks, base, d, dk)
                _sub_set(dvs, base, d, dv)
            else:
                _sub_set(dks, base, d, _sub(dks, base, d) + dk)
                _sub_set(dvs, base, d, _sub(dvs, base, d) + dv)

        fr, held, dq_acc = front(pieces[0]), None, None
        for i, piece in enumerate(pieces):
            cur = fr
            if i + 1 < len(pieces):
                fr = front(pieces[i + 1])
            dq_acc, now = middle(piece, cur, dq_acc)
            if held is not None:
                back(pieces[i - 1], held)
            held = now
        back(pieces[-1], held)
        for g in range(N_GROUPS):
            dq_ref[g] = dqs[g].astype(BF16)
        dk_ref[...] = dks[...]
        dv_ref[...] = dvs[...]

        @pl.when((h == ATT_HEADS - 1) & (n == nblk - 1))
        def _():
            _finish(copies)
            _finish(spread)

    blk = lambda f: pl.BlockSpec((rr, HEAD_DIM), f)
    prv = lambda n: jnp.maximum(n - 1, 0)
    nxt = lambda n: jnp.minimum(n + 1, nblk - 1)
    cur_kv = blk(lambda h, n: (n, h))
    outs = _pcall(
        body, name=name, grid=(ATT_HEADS, nblk),
        in_specs=[blk(lambda h, n, g=g: (n, g * ATT_HEADS + h)) for g in range(N_GROUPS)]
        + [blk(lambda h, n, g=g: (nxt(n), g * ATT_HEADS + h)) for g in range(N_GROUPS)]
        + [blk(lambda h, n: (prv(n), h)), cur_kv] * 2
        + [cur_kv, blk(lambda h, n: (nxt(n), h))] * 3 + [HBM_SPEC] * nx,
        out_specs=[pl.BlockSpec((N_GROUPS, rr, HEAD_DIM), lambda h, n: (0, n, h)), cur_kv, cur_kv] + [HBM_SPEC] * nx,
        out_shape=[SDS((N_GROUPS, s, KV_W), BF16), SDS((s, KV_W), F32), SDS((s, KV_W), F32)]
        + [SDS((N_DEV,) + _shard_shape(w), BF16) for w in names] + [SDS((N_DEV,) + a.shape, F32) for a in blocks],
        scratch_shapes=[pltpu.VMEM((N_GROUPS, 2 * rr, HEAD_DIM), F32)] + [pltpu.VMEM((2 * rr, HEAD_DIM), F32)] * 5
        + [pltpu.VMEM((N_GROUPS, rr, HEAD_DIM), F32), pltpu.VMEM((rr, HEAD_DIM), F32), pltpu.VMEM((rr, HEAD_DIM), F32)]
        + _comm_sems(nr) + _comm_sems(nb),
        compiler_params=_cp())(q_rot, q_rot, q_rot, q_rot, q_rot, q_rot, k_rot, k_rot, v, v, dmix, dmix, mix, mix,
                               lse, lse, *[grads[w] for w in names], *blocks)
    return outs[0], outs[1], outs[2], dict(zip(names, outs[3:3 + nr])), list(outs[3 + nr:])


def _pool_d(u_ref, halo_ref, i, tm):
    halo = jnp.where(i > 0, halo_ref[...].astype(F32), 0.0)
    t = i * tm + lax.broadcasted_iota(jnp.int32, (tm, 1), 0)
    out = []
    for g, w in enumerate(POOL_WINDOWS):
        sl = slice(g * HEAD_DIM, (g + 1) * HEAD_DIM)
        u = u_ref[:, sl].astype(F32)
        acc = jnp.concatenate([halo[:, sl], u], axis=0)
        sh = 1
        while sh < w:
            acc = acc + pltpu.roll(acc, sh, 0)
            sh *= 2
        cnt = jnp.minimum(t + 1, w).astype(F32)
        out.append(acc[HALO:, :] / cnt - u)
    return out


def _pool_fwd(proj, mix, pool_w, pool_scale, *, name):
    s = proj.shape[0]
    tm = _rows(s)
    ucol = (IN_W - POOL_W) // POOL_W

    def body(u_ref, halo_ref, mix_in, w_ref, sc_ref, o_ref):
        del mix_in
        dd = _pool_d(u_ref, halo_ref, pl.program_id(0), tm)
        for g in range(len(POOL_WINDOWS)):
            sl = slice(g * HEAD_DIM, (g + 1) * HEAD_DIM)
            y = _dot(dd[g].astype(BF16), w_ref[g].astype(BF16))
            o_ref[:, sl] = (y * sc_ref[:, sl]).astype(BF16)

    return _pcall(
        body, name=name, grid=(s // tm,),
        in_specs=[pl.BlockSpec((tm, POOL_W), lambda i: (i, ucol)),
                  pl.BlockSpec((HALO, POOL_W), lambda i: (jnp.maximum(i * (tm // HALO) - 1, 0), ucol)),
                  pl.BlockSpec(memory_space=pl.ANY),
                  pl.BlockSpec((len(POOL_WINDOWS), HEAD_DIM, HEAD_DIM), lambda i: (0, 0, 0)),
                  pl.BlockSpec((1, POOL_W), lambda i: (0, 0))],
        out_specs=pl.BlockSpec((tm, POOL_W), lambda i: (i, 1)),
        out_shape=SDS(mix.shape, BF16), input_output_aliases={2: 0},
        compiler_params=_cp())(proj, proj, mix, pool_w, pool_scale)


def _pool_bwd(proj, dmix, pool_w, pool_scale, *, name):
    s = proj.shape[0]
    tm = _rows(s)
    nblk = s // tm
    ucol = (IN_W - POOL_W) // POOL_W
    ng = len(POOL_WINDOWS)

    def body(u_ref, halo_ref, dp_ref, dpn_ref, w_ref, sc_ref, du_ref, dw_ref, dsc_ref):
        i = pl.program_id(0)

        @pl.when(i == 0)
        def _():
            dw_ref[...] = jnp.zeros_like(dw_ref)
            dsc_ref[...] = jnp.zeros_like(dsc_ref)

        dd = _pool_d(u_ref, halo_ref, i, tm)
        t = i * tm + lax.broadcasted_iota(jnp.int32, (tm, 1), 0)
        dpn = jnp.where(i + 1 < nblk, dpn_ref[...].astype(F32), 0.0)
        for g, w in enumerate(POOL_WINDOWS):
            sl = slice(g * HEAD_DIM, (g + 1) * HEAD_DIM)
            wg = w_ref[g].astype(BF16)
            db = dd[g].astype(BF16)
            dp = dp_ref[:, sl].astype(F32)
            dsc_ref[:, sl] += jnp.sum(dp * _dot(db, wg), axis=0, keepdims=True)
            dy = (dp * sc_ref[:, sl]).astype(BF16)
            dw_ref[g] += _dot_tn(db, dy)
            g_d = _dot_nt(dy, wg)
            g_dn = _dot_nt((dpn[:, sl] * sc_ref[:, sl]).astype(BF16), wg)
            cnt = jnp.minimum(t + 1, w).astype(F32)
            acc = jnp.concatenate([g_d / cnt, g_dn * (1.0 / w)], axis=0)
            sh = 1
            while sh < w:
                acc = acc + pltpu.roll(acc, tm + HALO - sh, 0)
                sh *= 2
            du_ref[:, sl] = acc[:tm, :] - g_d

    nh = s // HALO
    return _pcall(
        body, name=name, grid=(nblk,),
        in_specs=[pl.BlockSpec((tm, POOL_W), lambda i: (i, ucol)),
                  pl.BlockSpec((HALO, POOL_W), lambda i: (jnp.maximum(i * (tm // HALO) - 1, 0), ucol)),
                  pl.BlockSpec((tm, POOL_W), lambda i: (i, 1)),
                  pl.BlockSpec((HALO, POOL_W), lambda i: (jnp.minimum((i + 1) * (tm // HALO), nh - 1), 1)),
                  pl.BlockSpec((ng, HEAD_DIM, HEAD_DIM), lambda i: (0, 0, 0)),
                  pl.BlockSpec((1, POOL_W), lambda i: (0, 0))],
        out_specs=[pl.BlockSpec((tm, POOL_W), lambda i: (i, 0)),
                   pl.BlockSpec((ng, HEAD_DIM, HEAD_DIM), lambda i: (0, 0, 0)),
                   pl.BlockSpec((1, POOL_W), lambda i: (0, 0))],
        out_shape=[SDS((s, POOL_W), F32), SDS((ng, HEAD_DIM, HEAD_DIM), F32), SDS((1, POOL_W), F32)],
        compiler_params=_cp())(proj, proj, dmix, dmix, pool_w, pool_scale)


def _mem_fwd(mem, mem_g, wckv, ck_g, *, name):
    def body(m_ref, g_ref, w_ref, kg_ref, mn_ref, ckr_ref, ckn_ref, cv_ref):
        mv = m_ref[...]
        mn = (mv * _rms_r(mv) * g_ref[...]).astype(BF16)
        mn_ref[...] = mn
        ckv = _dot(mn, w_ref[...])
        ckr_ref[...] = ckv[:, :X_W]
        cv_ref[...] = ckv[:, X_W:].astype(BF16)
        for h in range(ATT_HEADS):
            sl = slice(h * HEAD_DIM, (h + 1) * HEAD_DIM)
            y = ckv[:, sl]
            ckn_ref[:, sl] = (y * _rms_r(y) * kg_ref[...]).astype(BF16)

    return _pcall(
        body, name=name,
        out_shape=[SDS((N_MEM, D_MODEL), BF16), SDS((N_MEM, X_W), F32), SDS((N_MEM, X_W), BF16),
                   SDS((N_MEM, X_W), BF16)],
        compiler_params=_cp())(mem, mem_g, wckv, ck_g)


def _cross_q(cq_ref, g_ref, sl):
    y = cq_ref[:, sl]
    r = _rms_r(y)
    return y, r, y * r * g_ref[...] * SCALE


def _cross_fwd(h1, gain, wcq, ck_n, cv, cq_g, wco, *, name):
    s, d = h1.shape
    tm = min(2 * _rows(s), s)

    def body(x_ref, gn_ref, wq_ref, k_ref, v_ref, g_ref, wo_ref, h2_ref, xn_ref, cq_ref, o_ref):
        xv = x_ref[...]
        xn_ref[...] = (xv * _rms_r(xv) * gn_ref[...]).astype(BF16)
        cq_ref[...] = _dot(xn_ref[...], wq_ref[...])
        for h in range(ATT_HEADS):
            sl = slice(h * HEAD_DIM, (h + 1) * HEAD_DIM)
            _, _, qn = _cross_q(cq_ref, g_ref, sl)
            sc = _dot_nt(qn.astype(BF16), k_ref[:, sl])
            p = jnp.exp(sc - jnp.max(sc, axis=-1, keepdims=True))
            p = p / jnp.sum(p, axis=-1, keepdims=True)
            o_ref[:, sl] = _dot(p.astype(BF16), v_ref[:, sl]).astype(BF16)
        for c in _chunks(d, X_W):
            h2_ref[:, c] = x_ref[:, c] + _dot(o_ref[...], wo_ref[:, c])

    row = lambda w_: pl.BlockSpec((tm, w_), lambda i: (i, 0))
    return _pcall(
        body, name=name, grid=(s // tm,),
        in_specs=[row(d), _resident((1, d)), _resident(wcq.shape), _resident(ck_n.shape), _resident(cv.shape),
                  _resident(cq_g.shape), _resident(wco.shape)],
        out_specs=[row(d), row(d), row(X_W), row(X_W)],
        out_shape=[SDS((s, d), F32), SDS((s, d), BF16), SDS((s, X_W), F32), SDS((s, X_W), BF16)],
        compiler_params=_cp())(h1, gain, wcq, ck_n, cv, cq_g, wco)


def _cross_bwd(dh2, dh2b, h1, cq_raw, ck_n, cv, cq_g, gain, wcq, wco, *, name):
    s, d = h1.shape
    tm = _rows(s)

    def body(dh2_ref, dh2b_ref, x_ref, cq_ref, k_ref, v_ref, g_ref, gn_ref, wq_ref, wo_ref,
             dcq_ref, dk_ref, dv_ref, dg_ref, dx_ref, dxb_ref, dgn_ref, do_ref):
        @pl.when(pl.program_id(0) == 0)
        def _():
            dk_ref[...] = jnp.zeros_like(dk_ref)
            dv_ref[...] = jnp.zeros_like(dv_ref)
            dg_ref[...] = jnp.zeros_like(dg_ref)
            dgn_ref[...] = jnp.zeros_like(dgn_ref)

        do_ref[...] = _dot_nt(dh2b_ref[...], wo_ref[...]).astype(BF16)

        def front(h):
            sl = slice(h * HEAD_DIM, (h + 1) * HEAD_DIM)
            y, r, qn = _cross_q(cq_ref, g_ref, sl)
            qb = qn.astype(BF16)
            do = do_ref[:, sl]
            return y, r, qb, do, _dot_nt(qb, k_ref[:, sl]), _dot_nt(do, v_ref[:, sl])

        dg = jnp.zeros((1, HEAD_DIM), F32)
        nxt = front(0)
        for h in range(ATT_HEADS):
            sl = slice(h * HEAD_DIM, (h + 1) * HEAD_DIM)
            y, r, qb, do, sc, dp = nxt
            if h + 1 < ATT_HEADS:
                nxt = front(h + 1)
            p = jnp.exp(sc - jnp.max(sc, axis=-1, keepdims=True))
            p = p / jnp.sum(p, axis=-1, keepdims=True)
            dv_ref[:, sl] += _dot_tn(p.astype(BF16), do)
            ds = (p * (dp - jnp.sum(dp * p, axis=-1, keepdims=True))).astype(BF16)
            dk_ref[:, sl] += _dot_tn(ds, qb)
            dn = _dot(ds, k_ref[:, sl]) * SCALE
            dy, dgain = _norm_bwd(y, r, g_ref[...], dn)
            dcq_ref[:, sl] = dy.astype(BF16)
            dg = dg + dgain
        dg_ref[...] += dg

        xv = x_ref[...]
        dx, dgain = _norm_bwd(xv, _rms_r(xv), gn_ref[...], _dot_nt(dcq_ref[...], wq_ref[...]))
        out = dh2_ref[...] + dx
        dx_ref[...] = out
        dxb_ref[...] = out.astype(BF16)
        dgn_ref[...] += dgain

    row = lambda w_: pl.BlockSpec((tm, w_), lambda i: (i, 0))
    acc = lambda r_, w_: pl.BlockSpec((r_, w_), lambda i: (0, 0))
    return _pcall(
        body, name=name, grid=(s // tm,),
        in_specs=[row(d), row(d), row(d), row(X_W), _resident(ck_n.shape), _resident(cv.shape), _resident(cq_g.shape),
                  _resident((1, d)), _resident(wcq.shape), _resident(wco.shape)],
        out_specs=[row(X_W), acc(N_MEM, X_W), acc(N_MEM, X_W), acc(1, HEAD_DIM), row(d), row(d), acc(1, d)],
        out_shape=[SDS((s, X_W), BF16), SDS((N_MEM, X_W), F32), SDS((N_MEM, X_W), F32), SDS((1, HEAD_DIM), F32),
                   SDS((s, d), F32), SDS((s, d), BF16), SDS((1, d), F32)],
        scratch_shapes=[pltpu.VMEM((tm, X_W), BF16)],
        compiler_params=_cp())(dh2, dh2b, h1, cq_raw, ck_n, cv, cq_g, gain, wcq, wco)


def _mem_bwd(dck_n, dcv, ck_raw, memn, mem, wckv, mem_g, ck_g, *, name):
    def body(dk_ref, dv_ref, ckr_ref, mn_ref, m_ref, w_ref, mg_ref, kg_ref, dw_ref, dmg_ref, dkg_ref, dckv_s):
        dkg = jnp.zeros((1, HEAD_DIM), F32)
        for h in range(ATT_HEADS):
            sl = slice(h * HEAD_DIM, (h + 1) * HEAD_DIM)
            y = ckr_ref[:, sl]
            dy, dgain = _norm_bwd(y, _rms_r(y), kg_ref[...], dk_ref[:, sl])
            dckv_s[:, sl] = dy.astype(BF16)
            dkg = dkg + dgain
        dkg_ref[...] = dkg
        dckv_s[:, X_W:] = dv_ref[...].astype(BF16)
        dckv = dckv_s[...]
        dw_ref[...] = _dot_tn(mn_ref[...], dckv).astype(BF16)
        dmn = _dot_nt(dckv, w_ref[...])
        mv = m_ref[...]
        dmg_ref[...] = jnp.sum(dmn * mv * _rms_r(mv), axis=0, keepdims=True)

    return _pcall(
        body, name=name,
        out_shape=[SDS((D_MODEL, 2 * X_W), BF16), SDS((1, D_MODEL), F32), SDS((1, HEAD_DIM), F32)],
        scratch_shapes=[pltpu.VMEM((N_MEM, 2 * X_W), BF16)],
        compiler_params=_cp())(dck_n, dcv, ck_raw, memn, mem, wckv, mem_g, ck_g)


def _rope_tables(pos):
    inv_freq = ROPE_THETA ** (-jnp.arange(0, ROT_DIM, 2, dtype=F32) / ROT_DIM)
    ang = pos.astype(F32)[:, None] * inv_freq
    cos, sin = jnp.cos(ang), jnp.sin(ang)
    s = pos.shape[0]
    rest = HEAD_DIM - ROT_DIM
    cos_t = jnp.concatenate([cos, cos, jnp.ones((s, rest), F32)], axis=1)
    sin_t = jnp.concatenate([-sin, sin, jnp.zeros((s, rest), F32)], axis=1)
    return cos_t, sin_t


def _local_step(x, mem, pos, tgt, w_in, shards, sm):
    cos_t, sin_t = _rope_tables(pos)
    wb = {"w_in": w_in}

    pick = lambda *names: {a: shards[a] for a in names}
    proj, xn, got = _norm_linear(x, sm["mix_norm_g"], wb["w_in"], cn=768, out_dtype=BF16, name="fwd_in_proj",
                                 shards=pick("w_out", "w_cq", "w_ckv", "w_co"))
    wb.update(got)
    q_rot, k_rot, v = _qk_prep(proj, cos_t, sin_t, sm["q_norm_g"], sm["k_norm_g"], name="fwd_qk_prep")
    mix, lse_all, got = _attn_fwd(q_rot, k_rot, v, pick("w_down", "w_gate_up"), name="fwd_attn")
    wb.update(got)
    mix = _pool_fwd(proj, mix, sm["pool_w"], sm["pool_scale"], name="fwd_pool")
    h1 = _linear_res(mix, wb["w_out"], x, cn=512, name="fwd_out_proj")
    memn, ck_raw, ck_n, cv = _mem_fwd(mem, sm["mem_norm_g"], wb["w_ckv"], sm["ck_norm_g"], name="fwd_mem")
    h2, hn, cq_raw, co = _cross_fwd(h1, sm["cross_norm_g"], wb["w_cq"], ck_n, cv, sm["cq_norm_g"], wb["w_co"],
                                    name="fwd_cross")
    gu, act, fn = _norm_linear_swiglu(h2, sm["ffn_norm_g"], wb["w_gate_up"], name="fwd_gate_up")
    dy, dyb, sq = _linear_res_loss(act, wb["w_down"], h2, tgt, name="fwd_down_loss")

    gw = {}
    gs = {}
    dgu = _swiglu_bwd(dyb, wb["w_down"], gu, name="bwd_swiglu")
    gw_down = _dw_tn(act, dyb, tkw=1024, tn=1024, name="bwd_dw_down")
    dh2, dh2b, gs["ffn_norm_g"], parts = _linear_nt_normbwd(dgu, wb["w_gate_up"], h2, dy, sm["ffn_norm_g"],
                                                            name="bwd_ffn_in", grads={"w_down": gw_down})
    gw["w_gate_up"] = _dw_tn(fn, dgu, tkw=1024, tn=1536, name="bwd_dw_gate_up")

    gw["w_co"] = _dw_tn(co, dh2b, tkw=512, tn=1024, name="bwd_dw_co")
    dcq, dck_n, dcv, gs["cq_norm_g"], dh1, dh1b, gs["cross_norm_g"] = _cross_bwd(
        dh2, dh2b, h1, cq_raw, ck_n, cv, sm["cq_norm_g"], sm["cross_norm_g"], wb["w_cq"], wb["w_co"], name="bwd_cross")
    gw["w_ckv"], gs["mem_norm_g"], gs["ck_norm_g"] = _mem_bwd(dck_n, dcv, ck_raw, memn, mem, wb["w_ckv"],
                                                             sm["mem_norm_g"], sm["ck_norm_g"], name="bwd_mem")
    gw["w_cq"] = _dw_tn(hn, dcq, tkw=1024, tn=512, name="bwd_dw_cq")

    dmix = _linear_nt(dh1b, wb["w_out"], cn=512, name="bwd_out_proj")
    gw["w_out"] = _dw_tn(mix, dh1b, tkw=1024, tn=1024, name="bwd_dw_out")
    du, gs["pool_w"], gs["pool_scale"] = _pool_bwd(proj, dmix, sm["pool_w"], sm["pool_scale"], name="bwd_pool")
    flat = lambda a: a.reshape(-1, a.shape[-1])
    early = [n for n in SMALL if n in gs]
    dq, dk, dv, got, spread = _attn_bwd(q_rot, k_rot, v, mix, dmix, lse_all, gw, [flat(gs[n]) for n in early] + [sq],
                                        name="bwd_attn")
    parts.update(got)
    small_parts = dict(zip(early + ["sq"], spread))
    dproj, gs["q_norm_g"], gs["k_norm_g"] = _qk_prep_bwd(proj, dq, dk, dv, du, cos_t, sin_t, sm["q_norm_g"],
                                                         sm["k_norm_g"], name="bwd_qk_prep")
    gw_in = _dw_tn(xn, dproj, tkw=1024, tn=1536, name="bwd_dw_in")
    dx, _, gs["mix_norm_g"], last = _linear_nt_normbwd(dproj, wb["w_in"], x, dh1, sm["mix_norm_g"],
                                                       name="bwd_in_proj", grads={"w_in": gw_in})
    parts.update(last)
    late = [n for n in SMALL if n not in small_parts]
    small_parts.update(zip(late, _exchange_small([flat(gs[n]) for n in late])))
    return dx, parts, small_parts


SHARDED = ("w_in", "w_out", "w_cq", "w_ckv", "w_co", "w_gate_up", "w_down")
SMALL = ("mix_norm_g", "q_norm_g", "k_norm_g", "pool_w", "pool_scale", "cross_norm_g", "mem_norm_g", "cq_norm_g",
         "ck_norm_g", "ffn_norm_g")
FULL = {
    "w_in": ((D_MODEL, IN_W), 1, IN_W // N_DEV),
    "w_out": ((D_MODEL, D_MODEL), 0, D_MODEL // N_DEV),
    "w_cq": ((D_MODEL, X_W), 0, D_MODEL // N_DEV),
    "w_ckv": ((D_MODEL, 2 * X_W), 0, D_MODEL // N_DEV),
    "w_co": ((X_W, D_MODEL), 1, D_MODEL // N_DEV),
    "w_gate_up": ((D_MODEL, 2 * FF_PAD), 1, FF_TILE),
    "w_down": ((FF_PAD, D_MODEL), 0, DOWN_SHARD),
}


def _shard_shape(name):
    shape, axis, width = FULL[name]
    return tuple(width if a == axis else n for a, n in enumerate(shape))


def _window(ref, name, dev):
    _, axis, width = FULL[name]
    if name == "w_down":
        start = pl.multiple_of((dev // 2) * FF_TILE + (dev % 2) * DOWN_SHARD, HALO)
    else:
        start = pl.multiple_of(dev * width, BLOCK)
    return ref.at[pl.ds(start, width), :] if axis == 0 else ref.at[:, pl.ds(start, width)]


def _mesh_place():
    x, y, c = lax.axis_index("x"), lax.axis_index("y"), lax.axis_index("c")
    return x, y, c, 4 * x + 2 * y + c


def _peer(x, y, c, k):
    px = 1 - x if k & 4 else x
    py = 1 - y if k & 2 else y
    pc = 1 - c if k & 1 else c
    return (px, py, pc), 4 * px + 2 * py + pc


HBM_SPEC = pl.BlockSpec(memory_space=pltpu.HBM)


def _comm_sems(n):
    return [pltpu.SemaphoreType.DMA((n, N_DEV - 1)), pltpu.SemaphoreType.DMA((n, N_DEV - 1)),
            pltpu.SemaphoreType.DMA((n,))]


DIRECT = (1, 2, 4, 6)
RELAYED = (2, 4, 6)


def _relay_step(steps):
    return max(1, (steps * 11) // 16) if steps > 1 else 0


def _gather_copies(names, ins, outs, send_sems, recv_sems, local_sems):
    x, y, c, me = _mesh_place()
    sibling, _ = _peer(x, y, c, 1)
    local, direct, relays, recv = [], [], {}, {}
    for wi, name in enumerate(names):
        def sems(k, to):
            return dict(send_sem=send_sems.at[wi, k - 1], recv_sem=recv_sems.at[wi, k - 1], device_id=to,
                        device_id_type=MESH)
        local.append(pltpu.make_async_copy(ins[wi], _window(outs[wi], name, me), local_sems.at[wi]))
        for k in range(1, N_DEV):
            peer, pidx = _peer(x, y, c, k)
            win = _window(outs[wi], name, pidx)
            recv[wi, k] = pltpu.make_async_remote_copy(src_ref=ins[wi], dst_ref=win, **sems(k, peer))
            if k in DIRECT:
                direct.append(pltpu.make_async_remote_copy(src_ref=ins[wi], dst_ref=_window(outs[wi], name, me),
                                                           **sems(k, peer)))
            if k in RELAYED:
                relays[wi, k] = pltpu.make_async_remote_copy(src_ref=win, dst_ref=win, **sems(k + 1, sibling))
    return local, direct, relays, recv


def _exchange_copies(names, ins, outs, send_sems, recv_sems, local_sems):
    x, y, c, me = _mesh_place()
    local, sent, recv = [], [], []
    for wi, name in enumerate(names):
        local.append(pltpu.make_async_copy(_window(ins[wi], name, me), outs[wi].at[0], local_sems.at[wi]))
        for k in range(1, N_DEV):
            peer, pidx = _peer(x, y, c, k)
            sems = dict(send_sem=send_sems.at[wi, k - 1], recv_sem=recv_sems.at[wi, k - 1], device_id=peer,
                        device_id_type=MESH)
            sent.append(pltpu.make_async_remote_copy(src_ref=_window(ins[wi], name, pidx), dst_ref=outs[wi].at[k], **sems))
            recv.append(pltpu.make_async_remote_copy(src_ref=_window(ins[wi], name, me), dst_ref=outs[wi].at[k], **sems))
    return local, sent, recv


def _down_pads(down_ref, zero_ref, zero_sems):
    return [pltpu.make_async_copy(zero_ref, down_ref.at[pl.ds(t * FF_TILE + FF_SHARD, FF_TILE - FF_SHARD), :],
                                  zero_sems.at[t]) for t in range(FF_PAD // FF_TILE)]


def _gather_scratch(names):
    if not names:
        return []
    pad = [pltpu.VMEM((FF_TILE - FF_SHARD, D_MODEL), BF16), pltpu.SemaphoreType.DMA((FF_PAD // FF_TILE,))]
    return _comm_sems(len(names)) + (pad if "w_down" in names else [])


def _gather_ops(names, ins, outs, scratch):
    local, direct, relays, recv = _gather_copies(names, ins, outs, *scratch[:3])
    pads = _down_pads(outs[names.index("w_down")], scratch[3], scratch[4]) if "w_down" in names else []

    def start():
        if pads:
            scratch[3][...] = jnp.zeros_like(scratch[3])
        for cp in local + direct + pads:
            cp.start()

    def relay(only=None):
        for (wi, k), cp in relays.items():
            if only is None or names[wi] in only:
                recv[wi, k].wait_recv()
                cp.start()

    def finish():
        for (wi, k), cp in recv.items():
            if k not in RELAYED:
                cp.wait_recv()
        for cp in direct + list(relays.values()):
            cp.wait_send()
        for cp in local + pads:
            cp.wait()

    return start, relay, finish


def _start(copies):
    local, sent, _ = copies
    for cp in local + sent:
        cp.start()


def _finish(copies):
    local, sent, recv = copies
    for cp in recv:
        cp.wait_recv()
    for cp in sent:
        cp.wait_send()
    for cp in local:
        cp.wait()


def _gather_weights(shards):
    names = tuple(shards)
    nw = len(names)

    def body(*refs):
        start, relay, finish = _gather_ops(names, refs[:nw], refs[nw:2 * nw], refs[2 * nw:])
        start()
        relay()
        finish()

    outs = _pcall(
        body, name="gather_weights",
        in_specs=[HBM_SPEC] * nw, out_specs=[HBM_SPEC] * nw,
        out_shape=[SDS(FULL[n][0], BF16) for n in names],
        scratch_shapes=_comm_sems(nw))(*[shards[n] for n in names])
    return dict(zip(names, outs))


def _broadcast_copies(ins, outs, send_sems, recv_sems, local_sems):
    x, y, c, me = _mesh_place()
    local, sent, recv = [], [], []
    for bi in range(len(ins)):
        local.append(pltpu.make_async_copy(ins[bi], outs[bi].at[me], local_sems.at[bi]))
        for k in range(1, N_DEV):
            peer, pidx = _peer(x, y, c, k)
            sems = dict(send_sem=send_sems.at[bi, k - 1], recv_sem=recv_sems.at[bi, k - 1], device_id=peer,
                        device_id_type=MESH)
            sent.append(pltpu.make_async_remote_copy(src_ref=ins[bi], dst_ref=outs[bi].at[me], **sems))
            recv.append(pltpu.make_async_remote_copy(src_ref=ins[bi], dst_ref=outs[bi].at[pidx], **sems))
    return local, sent, recv


def _exchange_small(blocks):
    nb = len(blocks)

    def body(*refs):
        copies = _broadcast_copies(refs[:nb], refs[nb:2 * nb], *refs[2 * nb:])
        _start(copies)
        _finish(copies)

    return _pcall(
        body, name="exchange_small", in_specs=[HBM_SPEC] * nb, out_specs=[HBM_SPEC] * nb,
        out_shape=[SDS((N_DEV,) + a.shape, F32) for a in blocks], scratch_shapes=_comm_sems(nb))(*blocks)


def _adam_math(g, w, m, v):
    m_new = ADAM_B1 * m + (1.0 - ADAM_B1) * g
    v_new = ADAM_B2 * v + (1.0 - ADAM_B2) * (g * g)
    m_hat = m_new / (1.0 - ADAM_B1 ** ADAM_STEP)
    v_hat = v_new / (1.0 - ADAM_B2 ** ADAM_STEP)
    return -ADAM_LR * (m_hat / (jnp.sqrt(v_hat) + ADAM_EPS) + ADAM_WD * w), m_new, v_new


def _adamw_small(parts, w, m, v, sq_parts, *, name):
    n = len(parts)

    def body(*refs):
        p_refs, w_refs, m_refs, v_refs = refs[:n], refs[n:2 * n], refs[2 * n:3 * n], refs[3 * n:4 * n]
        sq_ref, outs = refs[4 * n], refs[4 * n + 1:]
        for i in range(n):
            g = p_refs[i][0]
            for k in range(1, N_DEV):
                g = g + p_refs[i][k]
            delta, m_new, v_new = _adam_math(g, w_refs[i][...], m_refs[i][...], v_refs[i][...])
            outs[4 * i][...] = g
            outs[4 * i + 1][...] = delta
            outs[4 * i + 2][...] = m_new
            outs[4 * i + 3][...] = v_new
        tot = sq_ref[0]
        for k in range(1, N_DEV):
            tot = tot + sq_ref[k]
        outs[4 * n][...] = (0.5 / D_MODEL) * jnp.sum(tot, axis=1, keepdims=True)

    out_shape = [SDS(a.shape, F32) for a in w for _ in range(4)] + [SDS((1, 1), F32)]
    outs = _pcall(body, name=name, out_shape=out_shape, compiler_params=_cp())(*parts, *w, *m, *v, sq_parts)
    return [outs[4 * i:4 * i + 4] for i in range(n)], outs[4 * n][0, 0]


def _adamw(parts, w, m, v, *, name):
    r, c = w.shape
    pc = parts.shape[2]
    tr = r
    for cand in (256, 128, 88):
        if r % cand == 0:
            tr = cand
            break

    def body(p_ref, w_ref, m_ref, v_ref, g_ref, d_ref, mo_ref, vo_ref):
        g = p_ref[0, :, :c].astype(F32)
        for k in range(1, N_DEV):
            g = g + p_ref[k, :, :c].astype(F32)
        g_ref[...] = g
        d_ref[...], mo_ref[...], vo_ref[...] = _adam_math(g, w_ref[...], m_ref[...], v_ref[...])

    row = pl.BlockSpec((tr, c), lambda i: (i, 0))
    return _pcall(
        body, name=name, grid=(r // tr,),
        in_specs=[pl.BlockSpec((N_DEV, tr, pc), lambda i: (0, i, 0)), row, row, row],
        out_specs=[row] * 4, out_shape=[SDS((r, c), F32)] * 4, compiler_params=_cp())(parts, w, m, v)


def _pad_cols(a, width):
    return jnp.pad(a, ((0, 0), (0, width - a.shape[1])))


def kernel(x, mem, positions, mix_norm_g, w_in, q_norm_g, k_norm_g, pool_w, pool_scale, w_out, cross_norm_g, mem_norm_g, w_cq, w_ckv, cq_norm_g, ck_norm_g, w_co, ffn_norm_g, w_gate_up, w_down, loss_target, m_mix_norm_g, m_w_in, m_q_norm_g, m_k_norm_g, m_pool_w, m_pool_scale, m_w_out, m_cross_norm_g, m_mem_norm_g, m_w_cq, m_w_ckv, m_cq_norm_g, m_ck_norm_g, m_w_co, m_ffn_norm_g, m_w_gate_up, m_w_down, v_mix_norm_g, v_w_in, v_q_norm_g, v_k_norm_g, v_pool_w, v_pool_scale, v_w_out, v_cross_norm_g, v_mem_norm_g, v_w_cq, v_w_ckv, v_cq_norm_g, v_ck_norm_g, v_w_co, v_ffn_norm_g, v_w_gate_up, v_w_down):
    given = dict(locals())
    w_f32 = {n: given[n][0] for n in SHARDED + SMALL}
    m_f32 = {n: given["m_" + n][0] for n in SHARDED + SMALL}
    v_f32 = {n: given["v_" + n][0] for n in SHARDED + SMALL}
    shards = {n: w_f32[n].astype(BF16) for n in SHARDED}
    shards["w_gate_up"] = _pad_cols(shards["w_gate_up"], FF_TILE)
    w_in_full = _gather_weights({"w_in": shards.pop("w_in")})["w_in"]
    sm_rows = {n: (w_f32[n] if w_f32[n].ndim == 3 else w_f32[n].reshape(1, -1)) for n in SMALL}
    dx, parts, small = _local_step(x[0], mem[0], positions[0], loss_target[0], w_in_full, shards, sm_rows)

    flat = lambda a: a.reshape(-1, a.shape[-1])
    small_res, loss = _adamw_small([small[n] for n in SMALL], [flat(sm_rows[n]) for n in SMALL],
                                   [flat(m_f32[n].reshape(sm_rows[n].shape)) for n in SMALL],
                                   [flat(v_f32[n].reshape(sm_rows[n].shape)) for n in SMALL], small["sq"],
                                   name="adamw_small")
    res = {n: [a.reshape(w_f32[n].shape) for a in small_res[i]] for i, n in enumerate(SMALL)}
    for n in SHARDED:
        res[n] = _adamw(parts[n], w_f32[n], m_f32[n], v_f32[n], name="adamw_" + n)
    order = ("mix_norm_g", "w_in", "q_norm_g", "k_norm_g", "pool_w", "pool_scale", "w_out", "cross_norm_g",
             "mem_norm_g", "w_cq", "w_ckv", "cq_norm_g", "ck_norm_g", "w_co", "ffn_norm_g", "w_gate_up", "w_down")
    outs = [loss, dx[None]]
    for which in range(4):
        outs += [res[n][which][None] for n in order]
    return tuple(outs)
```
